```python
import math
import jax
import jax.numpy as jnp
from jax import lax
import numpy as np

D_MODEL = 1024
BATCH = 8
SEQ = 8192
DEPTH = 4

HEAD_DIM = 64
GROUP_HEADS = 4
GROUP_WIDTH = GROUP_HEADS * HEAD_DIM
D_MIX = 4 * GROUP_WIDTH
GLA_RANK = 16
GLA_TAU = 16.0
GLA_CHUNK = 64
GRID_W = 64
NA_ROWS_MAX = 8
NA_COLS = 16
LRU_CONV = 4
LRU_CONV_LEFT = 2
LRU_C = 8.0
DIL_PAIRS = ((128, 1), (512, 4), (2048, 16))
ROPE_THETA = 10000.0
D_FF = -(-8 * D_MODEL // (3 * 256)) * 256
EPS = 1e-6
SPLITS = (GROUP_WIDTH,) * 4 + (2 * GLA_RANK,) + (GROUP_WIDTH,) * 3 + (GROUP_WIDTH,) * 2 + (GROUP_WIDTH,) * 3
D_IN = sum(SPLITS)

kernel_name = "hybrid_parallel_head_group_encoder"


def rmsnorm(x, g):
    xf = x.astype(jnp.float32)
    y = xf * lax.rsqrt(jnp.mean(xf * xf, axis=-1, keepdims=True) + EPS)
    return (y * g.astype(jnp.float32)).astype(x.dtype)


def to_heads(t):
    B, L, _ = t.shape
    return t.reshape(B, L, -1, HEAD_DIM).transpose(0, 2, 1, 3)


def from_heads(t):
    B, H, L, dh = t.shape
    return t.transpose(0, 2, 1, 3).reshape(B, L, H * dh)


def rope(t, cos, sin):
    t1, t2 = jnp.split(t, 2, axis=-1)
    c = cos.astype(t.dtype)
    s = sin.astype(t.dtype)
    return jnp.concatenate([t1 * c - t2 * s, t2 * c + t1 * s], axis=-1)


def gla_chunked(q, k, v, log_a):
    B, H, L, dk = q.shape
    dv = v.shape[-1]
    C = GLA_CHUNK
    n = L // C
    q, k, v, log_a = [t.reshape(B, H, n, C, -1) for t in (q, k, v, log_a)]
    b = jnp.cumsum(log_a, axis=3)
    b_last = b[:, :, :, C - 1:C, :]
    b_mid = b[:, :, :, C // 2 - 1:C // 2, :]
    att = jnp.einsum('bhnck,bhnsk->bhncs', q * jnp.exp(b - b_mid), k * jnp.exp(b_mid - b))
    tri = np.tril(np.ones((C, C), dtype=bool))
    att = jnp.where(tri, att, 0.0)
    o_intra = jnp.einsum('bhncs,bhnsv->bhncv', att, v)
    chunk_kv = jnp.einsum('bhnck,bhncv->bhnkv', k * jnp.exp(b_last - b), v)
    decay = jnp.exp(b_last[:, :, :, 0, :])

    def step(S, inp):
        kv_c, d_c = inp
        return S * d_c[..., None] + kv_c, S

    S0 = jnp.zeros((B, H, dk, dv), jnp.float32)
    _, S_prev = lax.scan(step, S0, (jnp.moveaxis(chunk_kv, 2, 0), jnp.moveaxis(decay, 2, 0)))
    S_prev = jnp.moveaxis(S_prev, 0, 2)
    o_inter = jnp.einsum('bhnck,bhnkv->bhncv', q * jnp.exp(b), S_prev)
    return (o_intra + o_inter).reshape(B, H, L, dv)


def gla_mixer(q, k, v, g, z, w_gate, b_gate, norm_g):
    B, L, _ = q.shape
    f32 = jnp.float32
    zl = z.astype(f32).reshape(B, L, 2, GLA_RANK)
    logit = jnp.einsum('bler,erc->eblc', zl, w_gate.astype(f32)) + b_gate.astype(f32)[:, None, None, :]
    log_a = jax.nn.log_sigmoid(logit) / GLA_TAU
    qh = to_heads(q).astype(f32) * (HEAD_DIM ** -0.5)
    kh = to_heads(k).astype(f32)
    vh = to_heads(v).astype(f32)
    flip = lambda t: jnp.flip(t, axis=2)
    o_f = gla_chunked(qh, kh, vh, to_heads(log_a[0]))
    o_b = flip(gla_chunked(flip(qh), flip(kh), flip(vh), flip(to_heads(log_a[1]))))
    o = o_f + o_b
    o = o * lax.rsqrt(jnp.mean(o * o, axis=-1, keepdims=True) + EPS)
    o = o * norm_g.astype(f32).reshape(GROUP_HEADS, 1, HEAD_DIM)
    return (from_heads(o) * jax.nn.silu(g.astype(f32))).astype(q.dtype)


def neighbourhood_attention(q, k, v, rpb):
    B, H, L, dh = q.shape
    rows = L // GRID_W
    kr = min(NA_ROWS_MAX, rows)
    grid = lambda t: t.reshape(B, H, rows, GRID_W, dh)
    qg, kg, vg = grid(q), grid(k), grid(v)
    r = np.arange(rows)
    row_idx = np.clip(r - kr // 2, 0, rows - kr)[:, None] + np.arange(kr)[None, :]
    k_rows = kg[:, :, row_idx]
    v_rows = vg[:, :, row_idx]
    c = np.arange(GRID_W)
    col_start = np.clip(c - NA_COLS // 2, 0, GRID_W - NA_COLS)
    col_ok = (c[None, :] >= col_start[:, None]) & (c[None, :] < col_start[:, None] + NA_COLS)
    dr = row_idx - r[:, None]
    dc = np.clip(c[None, :] - c[:, None], -(NA_COLS - 1), NA_COLS - 1)
    bias = rpb[:, (dr + NA_ROWS_MAX - 1)[:, None, :, None], (dc + NA_COLS - 1)[None, :, None, :]]
    s = jnp.einsum('bhrqd,bhrikd->bhrqik', qg, k_rows).astype(jnp.float32) * (dh ** -0.5)
    s = s + bias.astype(jnp.float32)
    s = jnp.where(col_ok[:, None, :], s, -jnp.inf)
    p = jax.nn.softmax(s.reshape(B, H, rows, GRID_W, kr * GRID_W), axis=-1).reshape(s.shape)
    o = jnp.einsum('bhrqik,bhrikd->bhrqd', p.astype(v.dtype), v_rows)
    return o.reshape(B, H, L, dh)


def linear_scan(a, u):
    def combine(left, right):
        a_l, u_l = left
        a_r, u_r = right
        return a_l * a_r, a_r * u_l + u_r
    return lax.associative_scan(combine, (a, u), axis=1)[1]


def rglru_mixer(xb, gate, conv_w, conv_b, w_a, b_a, w_x, b_x, lam):
    B, L, C = xb.shape
    f32 = jnp.float32
    xp = jnp.pad(xb.astype(f32), ((0, 0), (LRU_CONV_LEFT, LRU_CONV - 1 - LRU_CONV_LEFT), (0, 0)))
    xc = conv_b.astype(f32)
    for j in range(LRU_CONV):
        xc = xc + xp[:, j:j + L, :] * conv_w[j].astype(f32)
    xh = xc.reshape(B, L, GROUP_HEADS, HEAD_DIM)
    r = jax.nn.sigmoid(jnp.einsum('blhi,ehij->eblhj', xh, w_a.astype(f32)).reshape(2, B, L, C)
                       + b_a.astype(f32)[:, None, None, :])
    i = jax.nn.sigmoid(jnp.einsum('blhi,ehij->eblhj', xh, w_x.astype(f32)).reshape(2, B, L, C)
                       + b_x.astype(f32)[:, None, None, :])
    log_a = -LRU_C * r * jax.nn.softplus(-lam.astype(f32))[:, None, None, :]
    a = jnp.exp(log_a)
    u = jnp.sqrt(-jnp.expm1(2.0 * log_a)) * (i * xc[None])
    flip = lambda t: jnp.flip(t, axis=1)
    h = linear_scan(a[0], u[0]) + flip(linear_scan(flip(a[1]), flip(u[1])))
    return (h * jax.nn.gelu(gate.astype(f32))).astype(xb.dtype)


def band_attention(q, k, v, radius):
    lead = q.shape[:-2]
    n, dh = q.shape[-2], q.shape[-1]
    Q = radius
    nb = -(-n // Q)
    n_pad = nb * Q
    nl = len(lead)
    qb = jnp.pad(q, ((0, 0),) * nl + ((0, n_pad - n), (0, 0))).reshape(lead + (nb, Q, dh))
    padkv = lambda t: jnp.pad(t, ((0, 0),) * nl + ((Q, n_pad - n + Q), (0, 0))).reshape(lead + (nb + 2, Q, dh))
    kp, vp = padkv(k), padkv(v)
    band = lambda t: jnp.concatenate([t[..., 0:nb, :, :], t[..., 1:nb + 1, :, :], t[..., 2:nb + 2, :, :]], axis=-2)
    kb, vb = band(kp), band(vp)
    blk = np.arange(nb)[:, None, None]
    qpos = blk * Q + np.arange(Q)[None, :, None]
    kpos = blk * Q + np.arange(3 * Q)[None, None, :] - Q
    valid = (np.abs(kpos - qpos) <= radius) & (kpos >= 0) & (kpos < n)
    s = jnp.einsum('...bqd,...bkd->...bqk', qb, kb).astype(jnp.float32) * (dh ** -0.5)
    s = jnp.where(valid, s, -jnp.inf)
    m = jnp.max(s, axis=-1, keepdims=True)
    e = jnp.exp(s - m)
    den = jnp.sum(e, axis=-1, keepdims=True)
    o = jnp.einsum('...bqk,...bkd->...bqd', (e / den).astype(v.dtype), vb)
    lse = (m + jnp.log(den))[..., 0]
    o = o.reshape(lead + (n_pad, dh))[..., :n, :]
    lse = lse.reshape(lead + (n_pad,))[..., :n]
    return o, lse


def dilated_attention(q, k, v):
    B, H, L, dh = q.shape
    outs, lses = [], []
    for window, dil in DIL_PAIRS:
        radius = window // (2 * dil)
        n = L // dil
        sub = lambda t: t.reshape(B, H, n, dil, dh).transpose(0, 1, 3, 2, 4)
        o, lse = band_attention(sub(q), sub(k), sub(v), radius)
        outs.append(o.transpose(0, 1, 3, 2, 4).reshape(B, H, L, dh).astype(jnp.float32))
        lses.append(lse.transpose(0, 1, 3, 2).reshape(B, H, L))
    wts = jax.nn.softmax(jnp.stack(lses, axis=0), axis=0)
    return jnp.einsum('gbhl,gbhld->bhld', wts, jnp.stack(outs, axis=0)).astype(q.dtype)


def _fwd_setup_inputs(seed: int = 0) -> dict:
    key = jax.random.key(seed)
    ks = jax.random.split(key, 24)
    f32 = jnp.float32
    nrm = lambda k, shape, scale: scale * jax.random.normal(k, shape, f32)
    gain = lambda k, d: 1.0 + 0.02 * jax.random.normal(k, (DEPTH, d), f32)
    u = jax.random.uniform(ks[14], (DEPTH, 2, GROUP_WIDTH), f32, 0.9, 0.999)
    return {
        "x": jax.random.normal(ks[0], (BATCH, SEQ, D_MODEL), f32),
        "mix_norm_pre": gain(ks[1], D_MODEL),
        "mix_norm_post": gain(ks[2], D_MODEL),
        "w_in": nrm(ks[3], (DEPTH, D_MODEL, D_IN), D_MODEL ** -0.5),
        "gla_w_gate": nrm(ks[4], (DEPTH, 2, GLA_RANK, GROUP_WIDTH), GLA_RANK ** -0.5),
        "gla_b_gate": nrm(ks[5], (DEPTH, 2, GROUP_WIDTH), 0.1),
        "gla_norm": gain(ks[6], GROUP_WIDTH),
        "na_rpb": nrm(ks[7], (DEPTH, GROUP_HEADS, 2 * NA_ROWS_MAX - 1, 2 * NA_COLS - 1), 0.1),
        "lru_conv_w": nrm(ks[8], (DEPTH, LRU_CONV, GROUP_WIDTH), LRU_CONV ** -0.5),
        "lru_conv_b": nrm(ks[9], (DEPTH, GROUP_WIDTH), 0.02),
        "lru_w_a": nrm(ks[10], (DEPTH, 2, GROUP_HEADS, HEAD_DIM, HEAD_DIM), HEAD_DIM ** -0.5),
        "lru_b_a": nrm(ks[11], (DEPTH, 2, GROUP_WIDTH), 0.1),
        "lru_w_x": nrm(ks[12], (DEPTH, 2, GROUP_HEADS, HEAD_DIM, HEAD_DIM), HEAD_DIM ** -0.5),
        "lru_b_x": nrm(ks[13], (DEPTH, 2, GROUP_WIDTH), 0.1),
        "lru_lambda": jnp.log(u) - jnp.log1p(-u),
        "w_out": nrm(ks[15], (DEPTH, D_MIX, D_MODEL), D_MIX ** -0.5),
        "ffn_norm_pre": gain(ks[16], D_MODEL),
        "ffn_norm_post": gain(ks[17], D_MODEL),
        "ffn_w_in": nrm(ks[18], (DEPTH, D_MODEL, 2 * D_FF), D_MODEL ** -0.5),
        "ffn_w_out": nrm(ks[19], (DEPTH, D_FF, D_MODEL), D_FF ** -0.5),
    }


def _fwd_reference(x, mix_norm_pre, mix_norm_post, w_in, gla_w_gate, gla_b_gate, gla_norm, na_rpb,
              lru_conv_w, lru_conv_b, lru_w_a, lru_b_a, lru_w_x, lru_b_x, lru_lambda, w_out,
              ffn_norm_pre, ffn_norm_post, ffn_w_in, ffn_w_out):
    B, L, _ = x.shape
    pos = jnp.arange(L, dtype=jnp.float32)
    inv_freq = ROPE_THETA ** (-jnp.arange(0, HEAD_DIM, 2, dtype=jnp.float32) / HEAD_DIM)
    ang = pos[:, None] * inv_freq[None, :]
    cos, sin = jnp.cos(ang), jnp.sin(ang)
    split_at = [int(s) for s in np.cumsum(SPLITS)[:-1]]
    for l in range(DEPTH):
        h = rmsnorm(x, mix_norm_pre[l])
        p = h @ w_in[l]
        qa, ka, va, ga, za, qb, kb, vb, xc, gc, qd, kd, vd = jnp.split(p, split_at, axis=-1)
        ya = gla_mixer(qa, ka, va, ga, za, gla_w_gate[l], gla_b_gate[l], gla_norm[l])
        yb = from_heads(neighbourhood_attention(to_heads(qb), to_heads(kb), to_heads(vb), na_rpb[l]))
        yc = rglru_mixer(xc, gc, lru_conv_w[l], lru_conv_b[l], lru_w_a[l], lru_b_a[l],
                         lru_w_x[l], lru_b_x[l], lru_lambda[l])
        yd = from_heads(dilated_attention(rope(to_heads(qd), cos, sin), rope(to_heads(kd), cos, sin),
                                          to_heads(vd)))
        y = jnp.concatenate([ya, yb.astype(x.dtype), yc, yd.astype(x.dtype)], axis=-1) @ w_out[l]
        x = x + rmsnorm(y, mix_norm_post[l])
        h = rmsnorm(x, ffn_norm_pre[l])
        gate, up = jnp.split(h @ ffn_w_in[l], 2, axis=-1)
        f = (jax.nn.silu(gate) * up) @ ffn_w_out[l]
        x = x + rmsnorm(f, ffn_norm_post[l])
    return x


import jax as _jax
import jax.numpy as _jnp

TWIN_FORMAT = 'train_step'
FWD_PARAMS = ['x', 'mix_norm_pre', 'mix_norm_post', 'w_in', 'gla_w_gate', 'gla_b_gate', 'gla_norm', 'na_rpb', 'lru_conv_w', 'lru_conv_b', 'lru_w_a', 'lru_b_a', 'lru_w_x', 'lru_b_x', 'lru_lambda', 'w_out', 'ffn_norm_pre', 'ffn_norm_post', 'ffn_w_in', 'ffn_w_out']
TWIN_WEIGHTS = ['mix_norm_pre', 'mix_norm_post', 'w_in', 'gla_w_gate', 'gla_b_gate', 'gla_norm', 'na_rpb', 'lru_conv_w', 'lru_conv_b', 'lru_w_a', 'lru_b_a', 'lru_w_x', 'lru_b_x', 'lru_lambda', 'w_out', 'ffn_norm_pre', 'ffn_norm_post', 'ffn_w_in', 'ffn_w_out']
TWIN_DIFF_INPUT = 'x'
TWIN_INPUTS = ['x', 'mix_norm_pre', 'mix_norm_post', 'w_in', 'gla_w_gate', 'gla_b_gate', 'gla_norm', 'na_rpb', 'lru_conv_w', 'lru_conv_b', 'lru_w_a', 'lru_b_a', 'lru_w_x', 'lru_b_x', 'lru_lambda', 'w_out', 'ffn_norm_pre', 'ffn_norm_post', 'ffn_w_in', 'ffn_w_out', 'loss_target', 'm_mix_norm_pre', 'm_mix_norm_post', 'm_w_in', 'm_gla_w_gate', 'm_gla_b_gate', 'm_gla_norm', 'm_na_rpb', 'm_lru_conv_w', 'm_lru_conv_b', 'm_lru_w_a', 'm_lru_b_a', 'm_lru_w_x', 'm_lru_b_x', 'm_lru_lambda', 'm_w_out', 'm_ffn_norm_pre', 'm_ffn_norm_post', 'm_ffn_w_in', 'm_ffn_w_out', 'v_mix_norm_pre', 'v_mix_norm_post', 'v_w_in', 'v_gla_w_gate', 'v_gla_b_gate', 'v_gla_norm', 'v_na_rpb', 'v_lru_conv_w', 'v_lru_conv_b', 'v_lru_w_a', 'v_lru_b_a', 'v_lru_w_x', 'v_lru_b_x', 'v_lru_lambda', 'v_w_out', 'v_ffn_norm_pre', 'v_ffn_norm_post', 'v_ffn_w_in', 'v_ffn_w_out']
TWIN_OUTPUTS = ['loss', 'grad_x', 'grad_mix_norm_pre', 'grad_mix_norm_post', 'grad_w_in', 'grad_gla_w_gate', 'grad_gla_b_gate', 'grad_gla_norm', 'grad_na_rpb', 'grad_lru_conv_w', 'grad_lru_conv_b', 'grad_lru_w_a', 'grad_lru_b_a', 'grad_lru_w_x', 'grad_lru_b_x', 'grad_lru_lambda', 'grad_w_out', 'grad_ffn_norm_pre', 'grad_ffn_norm_post', 'grad_ffn_w_in', 'grad_ffn_w_out', 'delta_mix_norm_pre', 'delta_mix_norm_post', 'delta_w_in', 'delta_gla_w_gate', 'delta_gla_b_gate', 'delta_gla_norm', 'delta_na_rpb', 'delta_lru_conv_w', 'delta_lru_conv_b', 'delta_lru_w_a', 'delta_lru_b_a', 'delta_lru_w_x', 'delta_lru_b_x', 'delta_lru_lambda', 'delta_w_out', 'delta_ffn_norm_pre', 'delta_ffn_norm_post', 'delta_ffn_w_in', 'delta_ffn_w_out', 'new_m_mix_norm_pre', 'new_m_mix_norm_post', 'new_m_w_in', 'new_m_gla_w_gate', 'new_m_gla_b_gate', 'new_m_gla_norm', 'new_m_na_rpb', 'new_m_lru_conv_w', 'new_m_lru_conv_b', 'new_m_lru_w_a', 'new_m_lru_b_a', 'new_m_lru_w_x', 'new_m_lru_b_x', 'new_m_lru_lambda', 'new_m_w_out', 'new_m_ffn_norm_pre', 'new_m_ffn_norm_post', 'new_m_ffn_w_in', 'new_m_ffn_w_out', 'new_v_mix_norm_pre', 'new_v_mix_norm_post', 'new_v_w_in', 'new_v_gla_w_gate', 'new_v_gla_b_gate', 'new_v_gla_norm', 'new_v_na_rpb', 'new_v_lru_conv_w', 'new_v_lru_conv_b', 'new_v_lru_w_a', 'new_v_lru_b_a', 'new_v_lru_w_x', 'new_v_lru_b_x', 'new_v_lru_lambda', 'new_v_w_out', 'new_v_ffn_norm_pre', 'new_v_ffn_norm_post', 'new_v_ffn_w_in', 'new_v_ffn_w_out']
TWIN_LEAF_KINDS = {'loss': 'loss', 'grad_x': 'grad_x', 'grad_mix_norm_pre': 'grad_w', 'grad_mix_norm_post': 'grad_w', 'grad_w_in': 'grad_w', 'grad_gla_w_gate': 'grad_w', 'grad_gla_b_gate': 'grad_w', 'grad_gla_norm': 'grad_w', 'grad_na_rpb': 'grad_w', 'grad_lru_conv_w': 'grad_w', 'grad_lru_conv_b': 'grad_w', 'grad_lru_w_a': 'grad_w', 'grad_lru_b_a': 'grad_w', 'grad_lru_w_x': 'grad_w', 'grad_lru_b_x': 'grad_w', 'grad_lru_lambda': 'grad_w', 'grad_w_out': 'grad_w', 'grad_ffn_norm_pre': 'grad_w', 'grad_ffn_norm_post': 'grad_w', 'grad_ffn_w_in': 'grad_w', 'grad_ffn_w_out': 'grad_w', 'delta_mix_norm_pre': 'delta_w', 'delta_mix_norm_post': 'delta_w', 'delta_w_in': 'delta_w', 'delta_gla_w_gate': 'delta_w', 'delta_gla_b_gate': 'delta_w', 'delta_gla_norm': 'delta_w', 'delta_na_rpb': 'delta_w', 'delta_lru_conv_w': 'delta_w', 'delta_lru_conv_b': 'delta_w', 'delta_lru_w_a': 'delta_w', 'delta_lru_b_a': 'delta_w', 'delta_lru_w_x': 'delta_w', 'delta_lru_b_x': 'delta_w', 'delta_lru_lambda': 'delta_w', 'delta_w_out': 'delta_w', 'delta_ffn_norm_pre': 'delta_w', 'delta_ffn_norm_post': 'delta_w', 'delta_ffn_w_in': 'delta_w', 'delta_ffn_w_out': 'delta_w', 'new_m_mix_norm_pre': 'new_m', 'new_m_mix_norm_post': 'new_m', 'new_m_w_in': 'new_m', 'new_m_gla_w_gate': 'new_m', 'new_m_gla_b_gate': 'new_m', 'new_m_gla_norm': 'new_m', 'new_m_na_rpb': 'new_m', 'new_m_lru_conv_w': 'new_m', 'new_m_lru_conv_b': 'new_m', 'new_m_lru_w_a': 'new_m', 'new_m_lru_b_a': 'new_m', 'new_m_lru_w_x': 'new_m', 'new_m_lru_b_x': 'new_m', 'new_m_lru_lambda': 'new_m', 'new_m_w_out': 'new_m', 'new_m_ffn_norm_pre': 'new_m', 'new_m_ffn_norm_post': 'new_m', 'new_m_ffn_w_in': 'new_m', 'new_m_ffn_w_out': 'new_m', 'new_v_mix_norm_pre': 'new_v', 'new_v_mix_norm_post': 'new_v', 'new_v_w_in': 'new_v', 'new_v_gla_w_gate': 'new_v', 'new_v_gla_b_gate': 'new_v', 'new_v_gla_norm': 'new_v', 'new_v_na_rpb': 'new_v', 'new_v_lru_conv_w': 'new_v', 'new_v_lru_conv_b': 'new_v', 'new_v_lru_w_a': 'new_v', 'new_v_lru_b_a': 'new_v', 'new_v_lru_w_x': 'new_v', 'new_v_lru_b_x': 'new_v', 'new_v_lru_lambda': 'new_v', 'new_v_w_out': 'new_v', 'new_v_ffn_norm_pre': 'new_v', 'new_v_ffn_norm_post': 'new_v', 'new_v_ffn_w_in': 'new_v', 'new_v_ffn_w_out': 'new_v'}


def _forward(args):
    return _fwd_reference(*[args[k] for k in FWD_PARAMS])


def _output_shape():
    def fwd():
        inp = _fwd_setup_inputs(0)
        return _fwd_reference(*[inp[k] for k in FWD_PARAMS])
    out = _jax.eval_shape(fwd)
    return out.shape, out.dtype

N_MICROBATCH = 1
ADAM_LR = 0.001
ADAM_B1 = 0.9
ADAM_B2 = 0.999
ADAM_EPS = 1e-08
ADAM_WD = 0.01
ADAM_STEP = 10
PER_EXAMPLE_BATCH_AXIS = {'x': 0, 'loss_target': 0}
SHARED_INPUTS = []
_WEIGHT_DTYPES = {'mix_norm_pre': _jnp.float32, 'mix_norm_post': _jnp.float32, 'w_in': _jnp.float32, 'gla_w_gate': _jnp.float32, 'gla_b_gate': _jnp.float32, 'gla_norm': _jnp.float32, 'na_rpb': _jnp.float32, 'lru_conv_w': _jnp.float32, 'lru_conv_b': _jnp.float32, 'lru_w_a': _jnp.float32, 'lru_b_a': _jnp.float32, 'lru_w_x': _jnp.float32, 'lru_b_x': _jnp.float32, 'lru_lambda': _jnp.float32, 'w_out': _jnp.float32, 'ffn_norm_pre': _jnp.float32, 'ffn_norm_post': _jnp.float32, 'ffn_w_in': _jnp.float32, 'ffn_w_out': _jnp.float32}
MOMENT_SCALE = {'mix_norm_pre': 4.265768e+00, 'mix_norm_post': 6.317299e+01, 'w_in': 2.442174e+00, 'gla_w_gate': 2.474912e-01, 'gla_b_gate': 9.078843e-01, 'gla_norm': 3.685873e+00, 'na_rpb': 3.310069e-01, 'lru_conv_w': 4.362431e+00, 'lru_conv_b': 8.901529e+01, 'lru_w_a': 9.700960e-01, 'lru_b_a': 8.678365e-01, 'lru_w_x': 1.924562e+00, 'lru_b_x': 1.082086e+00, 'lru_lambda': 1.587577e+00, 'w_out': 3.037116e+00, 'ffn_norm_pre': 2.806973e+00, 'ffn_norm_post': 6.337279e+01, 'ffn_w_in': 1.155153e+00, 'ffn_w_out': 2.082734e+00}


def _to_microbatches(a, axis):
    t = _jnp.moveaxis(a, axis, 0)
    t = t.reshape((N_MICROBATCH, t.shape[0] // N_MICROBATCH) + t.shape[1:])
    return _jnp.moveaxis(t, 1, axis + 1)


def setup_inputs(seed: int = 0) -> dict:
    inp = _fwd_setup_inputs(seed)
    key = _jax.random.fold_in(_jax.random.key(seed), 7919)
    shape, _ = _output_shape()
    out = dict(inp)
    out["loss_target"] = _jax.random.normal(_jax.random.fold_in(key, 0), shape, _jnp.float32)
    for i, name in enumerate(TWIN_WEIGHTS):
        w = inp[name].astype(_jnp.float32)
        if MOMENT_SCALE is None:
            s = _jnp.sqrt(_jnp.mean(_jnp.square(w)) + 1e-30)
        else:
            s = MOMENT_SCALE[name]
        km, kv = _jax.random.split(_jax.random.fold_in(key, i + 1))
        out[name] = w
        out["m_" + name] = s * _jax.random.normal(km, w.shape, _jnp.float32)
        out["v_" + name] = (s * s) * _jax.random.uniform(kv, w.shape, _jnp.float32, 0.5, 1.5)
    if N_MICROBATCH > 1:
        for name, axis in PER_EXAMPLE_BATCH_AXIS.items():
            out[name] = _to_microbatches(out[name], axis)
    return {'x': out['x'], 'mix_norm_pre': out['mix_norm_pre'], 'mix_norm_post': out['mix_norm_post'], 'w_in': out['w_in'], 'gla_w_gate': out['gla_w_gate'], 'gla_b_gate': out['gla_b_gate'], 'gla_norm': out['gla_norm'], 'na_rpb': out['na_rpb'], 'lru_conv_w': out['lru_conv_w'], 'lru_conv_b': out['lru_conv_b'], 'lru_w_a': out['lru_w_a'], 'lru_b_a': out['lru_b_a'], 'lru_w_x': out['lru_w_x'], 'lru_b_x': out['lru_b_x'], 'lru_lambda': out['lru_lambda'], 'w_out': out['w_out'], 'ffn_norm_pre': out['ffn_norm_pre'], 'ffn_norm_post': out['ffn_norm_post'], 'ffn_w_in': out['ffn_w_in'], 'ffn_w_out': out['ffn_w_out'], 'loss_target': out['loss_target'], 'm_mix_norm_pre': out['m_mix_norm_pre'], 'm_mix_norm_post': out['m_mix_norm_post'], 'm_w_in': out['m_w_in'], 'm_gla_w_gate': out['m_gla_w_gate'], 'm_gla_b_gate': out['m_gla_b_gate'], 'm_gla_norm': out['m_gla_norm'], 'm_na_rpb': out['m_na_rpb'], 'm_lru_conv_w': out['m_lru_conv_w'], 'm_lru_conv_b': out['m_lru_conv_b'], 'm_lru_w_a': out['m_lru_w_a'], 'm_lru_b_a': out['m_lru_b_a'], 'm_lru_w_x': out['m_lru_w_x'], 'm_lru_b_x': out['m_lru_b_x'], 'm_lru_lambda': out['m_lru_lambda'], 'm_w_out': out['m_w_out'], 'm_ffn_norm_pre': out['m_ffn_norm_pre'], 'm_ffn_norm_post': out['m_ffn_norm_post'], 'm_ffn_w_in': out['m_ffn_w_in'], 'm_ffn_w_out': out['m_ffn_w_out'], 'v_mix_norm_pre': out['v_mix_norm_pre'], 'v_mix_norm_post': out['v_mix_norm_post'], 'v_w_in': out['v_w_in'], 'v_gla_w_gate': out['v_gla_w_gate'], 'v_gla_b_gate': out['v_gla_b_gate'], 'v_gla_norm': out['v_gla_norm'], 'v_na_rpb': out['v_na_rpb'], 'v_lru_conv_w': out['v_lru_conv_w'], 'v_lru_conv_b': out['v_lru_conv_b'], 'v_lru_w_a': out['v_lru_w_a'], 'v_lru_b_a': out['v_lru_b_a'], 'v_lru_w_x': out['v_lru_w_x'], 'v_lru_b_x': out['v_lru_b_x'], 'v_lru_lambda': out['v_lru_lambda'], 'v_w_out': out['v_w_out'], 'v_ffn_norm_pre': out['v_ffn_norm_pre'], 'v_ffn_norm_post': out['v_ffn_norm_post'], 'v_ffn_w_in': out['v_ffn_w_in'], 'v_ffn_w_out': out['v_ffn_w_out']}


def _loss(weights, diff, rest, loss_target):
    with _jax.named_scope("forward"):
        args = {**rest, TWIN_DIFF_INPUT: diff, **{k: w.astype(_WEIGHT_DTYPES[k]) for k, w in weights.items()}}
        y = _forward(args)
    with _jax.named_scope("loss_head"):
        err = _jnp.square(y.astype(_jnp.float32) - loss_target)
        return 0.5 * _jnp.sum(_jnp.mean(err, axis=-1)) if err.ndim else 0.5 * err


def _adamw(w, g, m, v):
    m = ADAM_B1 * m + (1.0 - ADAM_B1) * g
    v = ADAM_B2 * v + (1.0 - ADAM_B2) * _jnp.square(g)
    m_hat = m / (1.0 - ADAM_B1 ** ADAM_STEP)
    v_hat = v / (1.0 - ADAM_B2 ** ADAM_STEP)
    delta = -ADAM_LR * (m_hat / (_jnp.sqrt(v_hat) + ADAM_EPS) + ADAM_WD * w)
    return delta, m, v


def reference(x, mix_norm_pre, mix_norm_post, w_in, gla_w_gate, gla_b_gate, gla_norm, na_rpb, lru_conv_w, lru_conv_b, lru_w_a, lru_b_a, lru_w_x, lru_b_x, lru_lambda, w_out, ffn_norm_pre, ffn_norm_post, ffn_w_in, ffn_w_out, loss_target, m_mix_norm_pre, m_mix_norm_post, m_w_in, m_gla_w_gate, m_gla_b_gate, m_gla_norm, m_na_rpb, m_lru_conv_w, m_lru_conv_b, m_lru_w_a, m_lru_b_a, m_lru_w_x, m_lru_b_x, m_lru_lambda, m_w_out, m_ffn_norm_pre, m_ffn_norm_post, m_ffn_w_in, m_ffn_w_out, v_mix_norm_pre, v_mix_norm_post, v_w_in, v_gla_w_gate, v_gla_b_gate, v_gla_norm, v_na_rpb, v_lru_conv_w, v_lru_conv_b, v_lru_w_a, v_lru_b_a, v_lru_w_x, v_lru_b_x, v_lru_lambda, v_w_out, v_ffn_norm_pre, v_ffn_norm_post, v_ffn_w_in, v_ffn_w_out):
    given = dict(x=x, mix_norm_pre=mix_norm_pre, mix_norm_post=mix_norm_post, w_in=w_in, gla_w_gate=gla_w_gate, gla_b_gate=gla_b_gate, gla_norm=gla_norm, na_rpb=na_rpb, lru_conv_w=lru_conv_w, lru_conv_b=lru_conv_b, lru_w_a=lru_w_a, lru_b_a=lru_b_a, lru_w_x=lru_w_x, lru_b_x=lru_b_x, lru_lambda=lru_lambda, w_out=w_out, ffn_norm_pre=ffn_norm_pre, ffn_norm_post=ffn_norm_post, ffn_w_in=ffn_w_in, ffn_w_out=ffn_w_out, loss_target=loss_target, m_mix_norm_pre=m_mix_norm_pre, m_mix_norm_post=m_mix_norm_post, m_w_in=m_w_in, m_gla_w_gate=m_gla_w_gate, m_gla_b_gate=m_gla_b_gate, m_gla_norm=m_gla_norm, m_na_rpb=m_na_rpb, m_lru_conv_w=m_lru_conv_w, m_lru_conv_b=m_lru_conv_b, m_lru_w_a=m_lru_w_a, m_lru_b_a=m_lru_b_a, m_lru_w_x=m_lru_w_x, m_lru_b_x=m_lru_b_x, m_lru_lambda=m_lru_lambda, m_w_out=m_w_out, m_ffn_norm_pre=m_ffn_norm_pre, m_ffn_norm_post=m_ffn_norm_post, m_ffn_w_in=m_ffn_w_in, m_ffn_w_out=m_ffn_w_out, v_mix_norm_pre=v_mix_norm_pre, v_mix_norm_post=v_mix_norm_post, v_w_in=v_w_in, v_gla_w_gate=v_gla_w_gate, v_gla_b_gate=v_gla_b_gate, v_gla_norm=v_gla_norm, v_na_rpb=v_na_rpb, v_lru_conv_w=v_lru_conv_w, v_lru_conv_b=v_lru_conv_b, v_lru_w_a=v_lru_w_a, v_lru_b_a=v_lru_b_a, v_lru_w_x=v_lru_w_x, v_lru_b_x=v_lru_b_x, v_lru_lambda=v_lru_lambda, v_w_out=v_w_out, v_ffn_norm_pre=v_ffn_norm_pre, v_ffn_norm_post=v_ffn_norm_post, v_ffn_w_in=v_ffn_w_in, v_ffn_w_out=v_ffn_w_out)
    weights = {n: given[n] for n in TWIN_WEIGHTS}
    shared = {n: given[n] for n in SHARED_INPUTS}
    per_example = {n: given[n] for n in ['x']}
    grad_fn = _jax.value_and_grad(_loss, argnums=(0, 1))

    def one_microbatch(ex, loss_target):
        ex = dict(ex)
        diff = ex.pop(TWIN_DIFF_INPUT)
        return grad_fn(weights, diff, {**shared, **ex}, loss_target)

    if N_MICROBATCH == 1:
        loss, (grad_w, grad_x) = one_microbatch(per_example, given["loss_target"])
    else:
        def body(carry, xs):
            loss_sum, grad_sum = carry
            l_k, (gw_k, gx_k) = one_microbatch(xs[0], xs[1])
            with _jax.named_scope("update"):
                return (loss_sum + l_k, _jax.tree.map(_jnp.add, grad_sum, gw_k)), gx_k

        init = (_jnp.zeros((), _jnp.float32), _jax.tree.map(_jnp.zeros_like, weights))
        (loss, grad_w), grad_x = _jax.lax.scan(body, init, (per_example, given["loss_target"]))
    with _jax.named_scope("update"):
        delta_w, new_m, new_v = {}, {}, {}
        for n in TWIN_WEIGHTS:
            delta_w[n], new_m[n], new_v[n] = _adamw(weights[n], grad_w[n], given["m_" + n], given["v_" + n])
    return (loss, grad_x, *[grad_w[n] for n in TWIN_WEIGHTS], *[delta_w[n] for n in TWIN_WEIGHTS],
            *[new_m[n] for n in TWIN_WEIGHTS], *[new_v[n] for n in TWIN_WEIGHTS])
```

```python
import functools
import math

import numpy as np
import jax
import jax.numpy as jnp
from jax import lax
from jax.experimental import pallas as pl
from jax.experimental.pallas import tpu as pltpu

F32 = jnp.float32
BF16 = jnp.bfloat16

N_DEV = 8
HEAD_DIM = 64
GROUP_W = 256
GLA_RANK = 16
GLA_TAU = 16.0
GLA_CHUNK = 64
GRID_W = 64
NA_ROWS = 8
NA_COLS = 16
LRU_C = 8.0
DIL_PAIRS = ((128, 1), (512, 4), (2048, 16))
DIL_RADIUS = 64
ROPE_THETA = 10000.0
EPS = 1e-6
ATT_SCALE = HEAD_DIM ** -0.5
NEG = -1e30
LANES = 128
P_COLS = 12 * GROUP_W + LANES
Z_BLOCK = 12 * GROUP_W // LANES

ADAM_LR = 0.001
ADAM_B1 = 0.9
ADAM_B2 = 0.999
ADAM_EPS = 1e-08
ADAM_WD = 0.01
ADAM_STEP = 10

VMEM_LIMIT = 56 * 1024 * 1024
_ARB = lambda n: pltpu.CompilerParams(dimension_semantics=("arbitrary",) * n, vmem_limit_bytes=VMEM_LIMIT)


def _tile(dim, pref, unit):
    t = min(pref, dim) // unit * unit
    while t >= unit:
        if dim % t == 0:
            return t
        t -= unit
    return dim


def _mm(a, b, mode, out_dtype, name, tm=512, tn=512, tk=1024):
    if mode == "nn":
        (M, K), (_, N) = a.shape, b.shape
    elif mode == "nt":
        (M, K), (N, _) = a.shape, b.shape
    else:
        (K, M), (_, N) = a.shape, b.shape
    tm = _tile(M, tm, LANES if mode == "tn" else 8)
    tn = _tile(N, tn, LANES)
    tk = _tile(K, tk, LANES)
    nk = K // tk
    dims = {"nn": (((1,), (0,)), ((), ())), "nt": (((1,), (1,)), ((), ())), "tn": (((0,), (0,)), ((), ()))}[mode]

    def kern(a_ref, b_ref, o_ref, acc_ref):
        k = pl.program_id(2)

        @pl.when(k == 0)
        def _():
            acc_ref[...] = jnp.zeros_like(acc_ref)

        acc_ref[...] += lax.dot_general(a_ref[...].astype(BF16), b_ref[...].astype(BF16), dims,
                                        preferred_element_type=F32)

        @pl.when(k == nk - 1)
        def _():
            o_ref[...] = acc_ref[...].astype(out_dtype)

    a_spec = pl.BlockSpec((tk, tm), lambda i, j, k: (k, i)) if mode == "tn" else pl.BlockSpec((tm, tk), lambda i, j, k: (i, k))
    b_spec = pl.BlockSpec((tn, tk), lambda i, j, k: (j, k)) if mode == "nt" else pl.BlockSpec((tk, tn), lambda i, j, k: (k, j))
    return pl.pallas_call(
        kern, name=name, grid=(M // tm, N // tn, nk),
        in_specs=[a_spec, b_spec], out_specs=pl.BlockSpec((tm, tn), lambda i, j, k: (i, j)),
        out_shape=jax.ShapeDtypeStruct((M, N), out_dtype),
        scratch_shapes=[pltpu.VMEM((tm, tn), F32)],
        compiler_params=_ARB(3),
    )(a, b)


class Row:
    def __init__(self, a, width=None, cb=0, halo=False):
        self.a, self.width, self.cb, self.halo = a, width, cb, halo


class Full:
    def __init__(self, a):
        self.a = a


HALO = 8


def _rows(name, body, tm, ins, outs):
    L = next(s.a.shape[0] for s in ins if isinstance(s, Row))
    tm = _tile(L, tm, 16)
    nt = L // tm
    nb8 = L // HALO
    step = tm // HALO
    in_specs, arrays, layout = [], [], []
    for s in ins:
        if isinstance(s, Full):
            nd = s.a.ndim
            in_specs.append(pl.BlockSpec(s.a.shape, lambda i, _nd=nd: (0,) * _nd))
            arrays.append(s.a)
            layout.append(1)
        else:
            w = s.width or s.a.shape[1]
            in_specs.append(pl.BlockSpec((tm, w), lambda i, _cb=s.cb: (i, _cb)))
            arrays.append(s.a)
            if s.halo:
                in_specs.append(pl.BlockSpec((HALO, w), lambda i, _cb=s.cb: (jnp.maximum(i * step - 1, 0), _cb)))
                in_specs.append(pl.BlockSpec((HALO, w), lambda i, _cb=s.cb: (jnp.minimum((i + 1) * step, nb8 - 1), _cb)))
                arrays += [s.a, s.a]
                layout.append(3)
            else:
                layout.append(1)
    out_specs, out_shapes = [], []
    for kind, shp, dt in outs:
        if kind == "row":
            out_specs.append(pl.BlockSpec((tm, shp), lambda i: (i, 0)))
            out_shapes.append(jax.ShapeDtypeStruct((L, shp), dt))
        else:
            out_specs.append(pl.BlockSpec(shp, lambda i, _n=len(shp): (0,) * _n))
            out_shapes.append(jax.ShapeDtypeStruct(shp, dt))
    n_in = len(arrays)

    def kern(*refs):
        i = pl.program_id(0)
        vals, p = [], 0
        for n in layout:
            if n == 1:
                vals.append(refs[p][...])
            else:
                vals.append((refs[p + 1][...], refs[p][...], refs[p + 2][...]))
            p += n
        res = body(i, nt, *vals)
        if not isinstance(res, (tuple, list)):
            res = (res,)
        for (kind, shp, dt), o_ref, r in zip(outs, refs[n_in:], res):
            if kind == "row":
                o_ref[...] = r.astype(dt)
            else:
                @pl.when(i == 0)
                def _(o_ref=o_ref):
                    o_ref[...] = jnp.zeros_like(o_ref)
                o_ref[...] += r.astype(dt)

    res = pl.pallas_call(
        kern, name=name, grid=(nt,), in_specs=in_specs, out_specs=out_specs, out_shape=out_shapes,
        compiler_params=_ARB(1),
    )(*arrays)
    return res


def _shift(h, o, i, nt):
    prev, cur, nxt = h
    if o == 0:
        return cur
    tm = cur.shape[0]
    cat = jnp.concatenate([prev, cur, nxt], axis=0)
    sh = pltpu.roll(cat, (-o) % (tm + 2 * HALO), axis=0)[HALO:HALO + tm]
    row = lax.broadcasted_iota(jnp.int32, cur.shape, 0)
    if o < 0:
        ok = jnp.logical_or(i > 0, row >= -o)
    else:
        ok = jnp.logical_or(i < nt - 1, row < tm - o)
    return jnp.where(ok, sh, 0.0)


def _colsum(v):
    return jnp.sum(v, axis=0, keepdims=True)


def _sigmoid(x):
    return 1.0 / (1.0 + jnp.exp(-x))


def _softplus(x):
    return jnp.maximum(x, 0.0) + jnp.log1p(jnp.exp(-jnp.abs(x)))


def _silu(x):
    return x * _sigmoid(x)


def _dsilu(x):
    s = _sigmoid(x)
    return s * (1.0 + x * (1.0 - s))


_GELU_C = math.sqrt(2.0 / math.pi)


def _gelu(x):
    return 0.5 * x * (1.0 + jnp.tanh(_GELU_C * (x + 0.044715 * x * x * x)))


def _dgelu(x):
    t = jnp.tanh(_GELU_C * (x + 0.044715 * x * x * x))
    return 0.5 * (1.0 + t) + 0.5 * x * (1.0 - t * t) * _GELU_C * (1.0 + 3.0 * 0.044715 * x * x)


def _head_sum(v, bd):
    return jnp.dot(v, bd, precision=lax.Precision.HIGHEST, preferred_element_type=F32)


def _block_ones(n, blk):
    r = np.arange(n)
    return jnp.asarray((r[:, None] // blk == r[None, :] // blk).astype(np.float32))


def _rms_fwd(x, g, name):
    def body(i, nt, x, g):
        r = lax.rsqrt(jnp.mean(x * x, axis=-1, keepdims=True) + EPS)
        return x * r * g
    return _rows(name, body, 256, [Row(x), Full(g)], [("row", x.shape[1], BF16)])[0]


def _rms_resid_fwd(x, y, g, name):
    def body(i, nt, x, y, g):
        r = lax.rsqrt(jnp.mean(y * y, axis=-1, keepdims=True) + EPS)
        return x + y * r * g
    return _rows(name, body, 256, [Row(x), Row(y), Full(g)], [("row", x.shape[1], F32)])[0]


def _rms_bwd(dy, x, g, name, resid=None, out_dtype=F32):
    D = x.shape[1]

    def body(i, nt, dy, x, g, *rest):
        dy = dy.astype(F32)
        r = lax.rsqrt(jnp.mean(x * x, axis=-1, keepdims=True) + EPS)
        xh = x * r
        dxh = dy * g
        dx = r * (dxh - xh * jnp.mean(dxh * xh, axis=-1, keepdims=True))
        if rest:
            dx = dx + rest[0]
        return dx, _colsum(dy * xh)

    ins = [Row(dy), Row(x), Full(g)] + ([Row(resid)] if resid is not None else [])
    return _rows(name, body, 256, ins, [("row", D, out_dtype), ("acc", (1, D), F32)])


def _swiglu_fwd(gu, name):
    F = gu.shape[1] // 2

    def body(i, nt, gu):
        return _silu(gu[:, :F]) * gu[:, F:]
    return _rows(name, body, 128, [Row(gu)], [("row", F, BF16)])[0]


def _swiglu_bwd(dact, gu, name):
    F = gu.shape[1] // 2

    def body(i, nt, da, gu):
        gate, up = gu[:, :F], gu[:, F:]
        return jnp.concatenate([da * up * _dsilu(gate), da * _silu(gate)], axis=1)
    return _rows(name, body, 64, [Row(dact), Row(gu)], [("row", 2 * F, BF16)])[0]


def _loss_fwd_bwd(y, target):
    D = y.shape[1]

    def body(i, nt, y, t):
        err = y - t
        part = 0.5 * jnp.sum(jnp.mean(err * err, axis=-1, keepdims=True), axis=0, keepdims=True)
        return err * (1.0 / D), jnp.broadcast_to(part, (1, LANES))
    dy, loss = _rows("loss", body, 256, [Row(y), Row(target)], [("row", D, F32), ("acc", (1, LANES), F32)])
    return loss[0, 0], dy


def _adamw(w, g, m, v, name):
    C = w.shape[1]
    bc1 = 1.0 - ADAM_B1 ** ADAM_STEP
    bc2 = 1.0 - ADAM_B2 ** ADAM_STEP

    def body(i, nt, w, g, m, v):
        m = ADAM_B1 * m + (1.0 - ADAM_B1) * g
        v = ADAM_B2 * v + (1.0 - ADAM_B2) * (g * g)
        delta = -ADAM_LR * ((m / bc1) / (jnp.sqrt(v / bc2) + ADAM_EPS) + ADAM_WD * w)
        return delta, m, v
    return _rows(name, body, 256, [Row(w), Row(g), Row(m), Row(v)], [("row", C, F32)] * 3)


def _expm1(x):
    return jnp.tanh(0.5 * x) * (jnp.exp(x) + 1.0)


def _lru_gates(xh, i, nt, cw, cb, wa, wx, ba, bx, lam):
    xc = cb
    for j in range(4):
        xc = xc + cw[j:j + 1] * _shift(xh, j - 2, i, nt)
    xcb = xc.astype(BF16)
    gates = []
    for e in range(2):
        r = _sigmoid(jnp.dot(xcb, wa[e], preferred_element_type=F32) + ba[e:e + 1])
        ig = _sigmoid(jnp.dot(xcb, wx[e], preferred_element_type=F32) + bx[e:e + 1])
        sp = _softplus(-lam[e:e + 1])
        la = -LRU_C * r * sp
        gates.append((r, ig, sp, jnp.exp(la), jnp.sqrt(-_expm1(2.0 * la))))
    return xc, xcb, gates


def _scan2(af, uf, ab, ub, adjoint, name):
    L, W = af.shape
    tm = _tile(L, 512, 8)
    nt, nb = L // tm, tm // 8

    def blk(A, U, h, reverse, row):
        for d in (1, 2, 4):
            if reverse:
                ok, sh = row < 8 - d, 8 - d
            else:
                ok, sh = row >= d, d
            As = jnp.where(ok, pltpu.roll(A, sh, axis=0), 1.0)
            Us = jnp.where(ok, pltpu.roll(U, sh, axis=0), 0.0)
            U = A * Us + U
            A = A * As
        return A * h + U

    def kern(af_ref, uf_ref, ab_ref, ub_ref, of_ref, ob_ref, c_ref):
        @pl.when(pl.program_id(0) == 0)
        def _():
            c_ref[...] = jnp.zeros_like(c_ref)

        row = lax.broadcasted_iota(jnp.int32, (8, W), 0)
        full = lambda v: jnp.broadcast_to(v, (8, W))

        def body(j, carry):
            hF, aF, hB, aB = carry
            r0 = pl.multiple_of(j * 8, 8)
            r1 = pl.multiple_of((nb - 1 - j) * 8, 8)
            A, U = af_ref[pl.ds(r0, 8), :], uf_ref[pl.ds(r0, 8), :]
            if adjoint:
                C = jnp.where(row == 0, aF, pltpu.roll(A, 1, axis=0))
                aF = full(A[7:8])
            else:
                C = A
            H = blk(C, U, hF, False, row)
            of_ref[pl.ds(r0, 8), :] = H
            hF = full(H[7:8])
            A, U = ab_ref[pl.ds(r1, 8), :], ub_ref[pl.ds(r1, 8), :]
            if adjoint:
                C = jnp.where(row == 7, aB, pltpu.roll(A, 7, axis=0))
                aB = full(A[0:1])
            else:
                C = A
            H = blk(C, U, hB, True, row)
            ob_ref[pl.ds(r1, 8), :] = H
            hB = full(H[0:1])
            return hF, aF, hB, aB

        carry = lax.fori_loop(0, nb, body, (c_ref[0], c_ref[1], c_ref[2], c_ref[3]))
        for n in range(4):
            c_ref[n] = carry[n]

    fwd = pl.BlockSpec((tm, W), lambda i: (i, 0))
    bwd = pl.BlockSpec((tm, W), lambda i: (nt - 1 - i, 0))
    return pl.pallas_call(
        kern, name=name, grid=(nt,), in_specs=[fwd, fwd, bwd, bwd], out_specs=[fwd, bwd],
        out_shape=[jax.ShapeDtypeStruct((L, W), F32)] * 2,
        scratch_shapes=[pltpu.VMEM((4, 8, W), F32)], compiler_params=_ARB(1),
    )(af, uf, ab, ub)


def _block_diag(w):
    out = jnp.zeros((2, GROUP_W, GROUP_W), w.dtype)
    for h in range(4):
        out = out.at[:, h * 64:(h + 1) * 64, h * 64:(h + 1) * 64].set(w[:, h])
    return out.astype(BF16)


def _diag_blocks(w):
    return jnp.stack([w[:, h * 64:(h + 1) * 64, h * 64:(h + 1) * 64] for h in range(4)], axis=1)


def _lru_params(W, l):
    return [Full(W["lru_conv_w"][l]), Full(W["lru_conv_b"][l][None]), Full(_block_diag(W["lru_w_a"][l])),
            Full(_block_diag(W["lru_w_x"][l])), Full(W["lru_b_a"][l]), Full(W["lru_b_x"][l]), Full(W["lru_lambda"][l])]


def _lru_fwd(p, W, l):
    def pre(i, nt, xh, *prm):
        xc, _, g = _lru_gates(xh, i, nt, *prm)
        return g[0][3], g[0][4] * (g[0][1] * xc), g[1][3], g[1][4] * (g[1][1] * xc)

    a0, u0, a1, u1 = _rows("lru_pre", pre, 256, [Row(p, GROUP_W, 7, halo=True)] + _lru_params(W, l),
                           [("row", GROUP_W, F32)] * 4)
    hf, hb = _scan2(a0, u0, a1, u1, False, "lru_scan")
    yc = _rows("lru_post", lambda i, nt, hf, hb, gc: (hf + hb) * _gelu(gc), 512,
               [Row(hf), Row(hb), Row(p, GROUP_W, 8)], [("row", GROUP_W, BF16)])[0]
    return yc, (a0, a1, hf, hb)


def _lru_bwd(dy, p, W, l, saved):
    a0, a1, hf, hb = saved

    def post(i, nt, dy, hf, hb, gc):
        return dy * _gelu(gc), dy * (hf + hb) * _dgelu(gc)

    dh, dgc = _rows("lru_post_bwd", post, 512, [Row(dy), Row(hf), Row(hb), Row(p, GROUP_W, 8)],
                    [("row", GROUP_W, F32), ("row", GROUP_W, BF16)])
    gb, gf = _scan2(a1, dh, a0, dh, True, "lru_scan_adj")

    def gates_bwd(i, nt, xh, gf, gb, hfh, hbh, cw, cb, wa, wx, ba, bx, lam):
        xc, xcb, g = _lru_gates(xh, i, nt, cw, cb, wa, wx, ba, bx, lam)
        dxc = jnp.zeros_like(xc)
        dwa, dwx, dba, dbx, dlam = [], [], [], [], []
        for e, du, hprev in ((0, gf, _shift(hfh, -1, i, nt)), (1, gb, _shift(hbh, 1, i, nt))):
            r, ig, sp, a, s = g[e]
            dxc = dxc + du * s * ig
            dla = du * hprev * a - (du * ig * xc) * a * a / s
            dza = (dla * (-LRU_C) * sp) * r * (1.0 - r)
            dzx = (du * s * xc) * ig * (1.0 - ig)
            dlam.append(_colsum(dla * r) * (LRU_C * _sigmoid(-lam[e:e + 1])))
            dba.append(_colsum(dza))
            dbx.append(_colsum(dzx))
            dzab, dzxb = dza.astype(BF16), dzx.astype(BF16)
            tn = (((0,), (0,)), ((), ()))
            nt_ = (((1,), (1,)), ((), ()))
            dwa.append(lax.dot_general(xcb, dzab, tn, preferred_element_type=F32))
            dwx.append(lax.dot_general(xcb, dzxb, tn, preferred_element_type=F32))
            dxc = dxc + lax.dot_general(dzab, wa[e], nt_, preferred_element_type=F32)
            dxc = dxc + lax.dot_general(dzxb, wx[e], nt_, preferred_element_type=F32)
        cat = lambda v: jnp.concatenate(v, axis=0)
        return dxc, jnp.stack(dwa), jnp.stack(dwx), cat(dba), cat(dbx), cat(dlam)

    dxc, dwa, dwx, dba, dbx, dlam = _rows(
        "lru_gates_bwd", gates_bwd, 256,
        [Row(p, GROUP_W, 7, halo=True), Row(gf), Row(gb), Row(hf, halo=True), Row(hb, halo=True)] + _lru_params(W, l),
        [("row", GROUP_W, F32), ("acc", (2, GROUP_W, GROUP_W), F32), ("acc", (2, GROUP_W, GROUP_W), F32),
         ("acc", (2, GROUP_W), F32), ("acc", (2, GROUP_W), F32), ("acc", (2, GROUP_W), F32)])

    def conv_bwd(i, nt, dh_, xh, cw):
        dxb = jnp.zeros_like(dh_[1])
        dcw = []
        for j in range(4):
            dxb = dxb + cw[j:j + 1] * _shift(dh_, 2 - j, i, nt)
            dcw.append(_colsum(dh_[1] * _shift(xh, j - 2, i, nt)))
        return dxb, jnp.concatenate(dcw, axis=0), _colsum(dh_[1])

    dxb, dcw, dcb = _rows("lru_conv_bwd", conv_bwd, 512,
                          [Row(dxc, halo=True), Row(p, GROUP_W, 7, halo=True), Full(W["lru_conv_w"][l])],
                          [("row", GROUP_W, BF16), ("acc", (4, GROUP_W), F32), ("acc", (1, GROUP_W), F32)])
    grads = dict(lru_conv_w=dcw, lru_conv_b=dcb[0], lru_w_a=_diag_blocks(dwa), lru_w_x=_diag_blocks(dwx),
                 lru_b_a=dba, lru_b_x=dbx, lru_lambda=dlam)
    return dxb, dgc, grads


_NT = (((1,), (1,)), ((), ()))
_TN = (((0,), (0,)), ((), ()))


def _dot(a, b, dims=None):
    if dims is None:
        return jnp.dot(a, b, preferred_element_type=F32)
    return lax.dot_general(a, b, dims, preferred_element_type=F32)


def _dot_exact(a, b):
    return jnp.dot(a, b, precision=lax.Precision.HIGHEST, preferred_element_type=F32)


def _gla_gate_w(w_gate, b_gate):
    wg = jnp.zeros((LANES, 2 * GROUP_W), F32)
    for e in range(2):
        wg = wg.at[e * GLA_RANK:(e + 1) * GLA_RANK, e * GROUP_W:(e + 1) * GROUP_W].set(w_gate[e])
    return wg.astype(BF16), b_gate.reshape(1, 2 * GROUP_W)


def _gla_gates_fwd(p, wg, bg):
    def body(i, nt, z, wg, bg):
        logit = _dot(z.astype(BF16), wg) + bg
        la = -_softplus(-logit) * (1.0 / GLA_TAU)
        return la[:, :GROUP_W], la[:, GROUP_W:]
    return _rows("gla_gates", body, 512, [Row(p, LANES, Z_BLOCK), Full(wg), Full(bg)], [("row", GROUP_W, F32)] * 2)


def _gla_gates_bwd(p, dla0, dla1, wg, bg):
    def body(i, nt, z, d0, d1, wg, bg):
        zb = z.astype(BF16)
        logit = _dot(zb, wg) + bg
        dlogit = jnp.concatenate([d0, d1], axis=1) * (1.0 / GLA_TAU) * _sigmoid(-logit)
        dlb = dlogit.astype(BF16)
        return _dot(dlb, wg, _NT), _dot(zb, dlb, _TN), _colsum(dlogit)
    return _rows("gla_gates_bwd", body, 512, [Row(p, LANES, Z_BLOCK), Row(dla0), Row(dla1), Full(wg), Full(bg)],
                 [("row", LANES, BF16), ("acc", (LANES, 2 * GROUP_W), F32), ("acc", (1, 2 * GROUP_W), F32)])


def _gla_order(reverse):
    t = np.arange(GLA_CHUNK)
    m = (t[None, :] >= t[:, None]) if reverse else (t[None, :] <= t[:, None])
    return m.astype(np.float32), (32, 0) if reverse else (31, 63)


def _stack_heads(x, bd):
    return jnp.where(bd, jnp.concatenate([x] * 4, axis=0), 0.0)


def _diag_heads(r, bd):
    r = jnp.where(bd, r, 0.0)
    return r[0:64] + r[64:128] + r[128:192] + r[192:256]


def _gla_chunk_terms(q_ref, k_ref, la_ref, rows, mv, mid, last):
    b = _dot_exact(mv, la_ref[rows, :])
    bm, bl = b[mid:mid + 1], b[last:last + 1]
    qs = q_ref[rows, :] * ATT_SCALE
    k = k_ref[rows, :]
    P, N, E, Fd = jnp.exp(b - bm), jnp.exp(bm - b), jnp.exp(b), jnp.exp(bl - b)
    return (P, N, E, Fd, jnp.exp(bl)), (qs * P, k * N, qs * E, k * Fd)


def _gla_chunk_fwd(p, la, reverse, name):
    L = la.shape[0]
    tm = _tile(L, 512, GLA_CHUNK)
    nt, nc = L // tm, tm // GLA_CHUNK
    m_np, (mid, last) = _gla_order(reverse)

    def kern(q_ref, k_ref, v_ref, la_ref, m_ref, bd_ref, o_ref, s_ref, st_ref):
        @pl.when(pl.program_id(0) == 0)
        def _():
            st_ref[...] = jnp.zeros_like(st_ref)

        mv = m_ref[...]
        bd = bd_ref[...] > 0.5
        keep = jnp.concatenate([mv] * 4, axis=0) > 0.5

        def body(cc, carry):
            c = nc - 1 - cc if reverse else cc
            rows = pl.ds(pl.multiple_of(c * GLA_CHUNK, GLA_CHUNK), GLA_CHUNK)
            (_, _, _, _, d), (qP, kN, qE, kF) = _gla_chunk_terms(q_ref, k_ref, la_ref, rows, mv, mid, last)
            vb = v_ref[rows, :].astype(BF16)
            st = st_ref[...]
            s_ref[c] = st
            a = jnp.where(keep, _dot(_stack_heads(qP, bd).astype(BF16), kN.astype(BF16), _NT), 0.0)
            o = _diag_heads(_dot(a.astype(BF16), vb), bd) + _dot(qE.astype(BF16), st.astype(BF16), _NT)
            o_ref[rows, :] = o
            st_ref[...] = st * d + jnp.where(bd, _dot(vb, kF.astype(BF16), _TN), 0.0)
            return carry

        lax.fori_loop(0, nc, body, 0)

    tile = (lambda i: (nt - 1 - i, 0)) if reverse else (lambda i: (i, 0))
    tile3 = (lambda i: (nt - 1 - i, 0, 0)) if reverse else (lambda i: (i, 0, 0))
    col = lambda cb: pl.BlockSpec((tm, GROUP_W), lambda i, _cb=cb: (tile(i)[0], _cb))
    const = lambda shp: pl.BlockSpec(shp, lambda i: (0, 0))
    return pl.pallas_call(
        kern, name=name, grid=(nt,),
        in_specs=[col(0), col(1), col(2), pl.BlockSpec((tm, GROUP_W), tile), const((GLA_CHUNK, GLA_CHUNK)),
                  const((GROUP_W, GROUP_W))],
        out_specs=[pl.BlockSpec((tm, GROUP_W), tile), pl.BlockSpec((nc, GROUP_W, GROUP_W), tile3)],
        out_shape=[jax.ShapeDtypeStruct((L, GROUP_W), F32),
                   jax.ShapeDtypeStruct((L // GLA_CHUNK, GROUP_W, GROUP_W), F32)],
        scratch_shapes=[pltpu.VMEM((GROUP_W, GROUP_W), F32)], compiler_params=_ARB(1),
    )(p, p, p, la, jnp.asarray(m_np), _block_ones(GROUP_W, HEAD_DIM))


def _gla_chunk_bwd(p, la, do, sprev, reverse, name, acc=None, out_dtype=F32):
    L = la.shape[0]
    tm = _tile(L, 512, GLA_CHUNK)
    nt, nc = L // tm, tm // GLA_CHUNK
    m_np, (mid, last) = _gla_order(reverse)
    n_acc = 0 if acc is None else 3

    def kern(q_ref, k_ref, v_ref, la_ref, do_ref, s_ref, m_ref, mt_ref, bd_ref, *rest):
        acc_refs, (dq_ref, dk_ref, dv_ref, dla_ref, dst_ref) = rest[:n_acc], rest[n_acc:]

        @pl.when(pl.program_id(0) == 0)
        def _():
            dst_ref[...] = jnp.zeros_like(dst_ref)

        mv, mt = m_ref[...], mt_ref[...]
        bd = bd_ref[...] > 0.5
        keep = jnp.concatenate([mv] * 4, axis=0) > 0.5
        row = lax.broadcasted_iota(jnp.int32, (GLA_CHUNK, GROUP_W), 0)

        def body(cc, carry):
            c = cc if reverse else nc - 1 - cc
            rows = pl.ds(pl.multiple_of(c * GLA_CHUNK, GLA_CHUNK), GLA_CHUNK)
            (P, N, E, Fd, d), (qP, kN, qE, kF) = _gla_chunk_terms(q_ref, k_ref, la_ref, rows, mv, mid, last)
            vb = v_ref[rows, :].astype(BF16)
            dov = do_ref[rows, :]
            dob = dov.astype(BF16)
            st, dst = s_ref[c], dst_ref[...]
            stb, dstb = st.astype(BF16), dst.astype(BF16)
            qst = _stack_heads(qP, bd).astype(BF16)
            dost = _stack_heads(dov, bd).astype(BF16)
            kNb, kFb, qEb = kN.astype(BF16), kF.astype(BF16), qE.astype(BF16)
            a = jnp.where(keep, _dot(qst, kNb, _NT), 0.0).astype(BF16)
            da = jnp.where(keep, _dot(dost, vb, _NT), 0.0).astype(BF16)
            dv = _dot(a, dost, _TN) + _dot(kFb, dstb, _NT)
            dqP = _diag_heads(_dot(da, kNb), bd)
            dkN = _dot(da, qst, _TN)
            dqE = _dot(dob, stb)
            dkF = _dot(vb, dstb)
            dd = _colsum(dst * st)
            dst_ref[...] = jnp.where(bd, _dot(dob, qEb, _TN), 0.0) + dst * d
            dq = (dqP * P + dqE * E) * ATT_SCALE
            dk = dkN * N + dkF * Fd
            tP, tN, tE, tF = dqP * qP, dkN * kN, dqE * qE, dkF * kF
            db = tP - tN + tE - tF
            db = db + jnp.where(row == mid, _colsum(tN - tP), 0.0) + jnp.where(row == last, _colsum(tF) + dd * d, 0.0)
            dla_ref[rows, :] = _dot_exact(mt, db)
            if n_acc:
                dq, dk, dv = dq + acc_refs[0][rows, :], dk + acc_refs[1][rows, :], dv + acc_refs[2][rows, :]
            dq_ref[rows, :] = dq.astype(out_dtype)
            dk_ref[rows, :] = dk.astype(out_dtype)
            dv_ref[rows, :] = dv.astype(out_dtype)
            return carry

        lax.fori_loop(0, nc, body, 0)

    tile = (lambda i: (i, 0)) if reverse else (lambda i: (nt - 1 - i, 0))
    tile3 = (lambda i: (i, 0, 0)) if reverse else (lambda i: (nt - 1 - i, 0, 0))
    col = lambda cb: pl.BlockSpec((tm, GROUP_W), lambda i, _cb=cb: (tile(i)[0], _cb))
    rowspec = pl.BlockSpec((tm, GROUP_W), tile)
    const = lambda shp: pl.BlockSpec(shp, lambda i: (0, 0))
    outs = pl.pallas_call(
        kern, name=name, grid=(nt,),
        in_specs=[col(0), col(1), col(2), rowspec, rowspec, pl.BlockSpec((nc, GROUP_W, GROUP_W), tile3),
                  const((GLA_CHUNK, GLA_CHUNK)), const((GLA_CHUNK, GLA_CHUNK)), const((GROUP_W, GROUP_W))]
                 + [rowspec] * n_acc,
        out_specs=[rowspec] * 4,
        out_shape=[jax.ShapeDtypeStruct((L, GROUP_W), out_dtype)] * 3 + [jax.ShapeDtypeStruct((L, GROUP_W), F32)],
        scratch_shapes=[pltpu.VMEM((GROUP_W, GROUP_W), F32)], compiler_params=_ARB(1),
    )(p, p, p, la, do, sprev, jnp.asarray(m_np), jnp.asarray(m_np.T.copy()), _block_ones(GROUP_W, HEAD_DIM),
      *([] if acc is None else acc))
    return outs


def _gla_fwd(p, W, l):
    wg, bg = _gla_gate_w(W["gla_w_gate"][l], W["gla_b_gate"][l])
    la0, la1 = _gla_gates_fwd(p, wg, bg)
    of, s0 = _gla_chunk_fwd(p, la0, False, "gla_fwd_f")
    ob, s1 = _gla_chunk_fwd(p, la1, True, "gla_fwd_b")

    def post(i, nt, of, ob, g, ng, bd):
        o = of + ob
        r = lax.rsqrt(_head_sum(o * o, bd) * (1.0 / HEAD_DIM) + EPS)
        return o * r * ng * _silu(g)

    ya = _rows("gla_post", post, 512, [Row(of), Row(ob), Row(p, GROUP_W, 3), Full(W["gla_norm"][l][None]),
                                       Full(_block_ones(GROUP_W, HEAD_DIM))], [("row", GROUP_W, BF16)])[0]
    return ya, (la0, la1, of, ob, s0, s1)


def _gla_bwd(dy, p, W, l, saved):
    la0, la1, of, ob, s0, s1 = saved
    wg, bg = _gla_gate_w(W["gla_w_gate"][l], W["gla_b_gate"][l])

    def post(i, nt, dy, of, ob, g, ng, bd):
        o = of + ob
        r = lax.rsqrt(_head_sum(o * o, bd) * (1.0 / HEAD_DIM) + EPS)
        oh = o * r
        don = dy * _silu(g)
        doh = don * ng
        do = r * (doh - oh * _head_sum(doh * oh, bd) * (1.0 / HEAD_DIM))
        return do, dy * (oh * ng) * _dsilu(g), _colsum(don * oh)

    do, dg, dng = _rows("gla_post_bwd", post, 512,
                        [Row(dy), Row(of), Row(ob), Row(p, GROUP_W, 3), Full(W["gla_norm"][l][None]),
                         Full(_block_ones(GROUP_W, HEAD_DIM))],
                        [("row", GROUP_W, F32), ("row", GROUP_W, BF16), ("acc", (1, GROUP_W), F32)])
    dq, dk, dv, dla0 = _gla_chunk_bwd(p, la0, do, s0, False, "gla_bwd_f")
    dq, dk, dv, dla1 = _gla_chunk_bwd(p, la1, do, s1, True, "gla_bwd_b", acc=(dq, dk, dv), out_dtype=BF16)
    dz, dwg, dbg = _gla_gates_bwd(p, dla0, dla1, wg, bg)
    dw_gate = jnp.stack([dwg[e * GLA_RANK:(e + 1) * GLA_RANK, e * GROUP_W:(e + 1) * GROUP_W] for e in range(2)])
    grads = dict(gla_w_gate=dw_gate, gla_b_gate=dbg.reshape(2, GROUP_W), gla_norm=dng[0])
    return (dq, dk, dv, dg, dz), grads


def _rope_tables(L):
    pos = jnp.arange(L, dtype=F32)
    inv_freq = ROPE_THETA ** (-jnp.arange(0, HEAD_DIM, 2, dtype=F32) / HEAD_DIM)
    ang = pos[:, None] * inv_freq[None, :]
    cos, sin = jnp.cos(ang), jnp.sin(ang)
    return jnp.tile(jnp.concatenate([cos, cos], axis=1), (1, 4)), jnp.tile(jnp.concatenate([-sin, sin], axis=1), (1, 4))


def _swap_halves(t):
    lane = lax.broadcasted_iota(jnp.int32, t.shape, 1)
    first = (lane & (HEAD_DIM - 1)) < HEAD_DIM // 2
    return jnp.where(first, pltpu.roll(t, GROUP_W - HEAD_DIM // 2, axis=1), pltpu.roll(t, HEAD_DIM // 2, axis=1))


def _attn_prep(p, cosf, sinf):
    def body(i, nt, qb, kb, vb, qd, kd, vd, c, s):
        return qb, kb, vb, qd * c + _swap_halves(qd) * s, kd * c + _swap_halves(kd) * s, vd
    ins = [Row(p, GROUP_W, cb) for cb in (4, 5, 6, 9, 10, 11)] + [Row(cosf), Row(sinf)]
    return _rows("attn_prep", body, 512, ins, [("row", GROUP_W, BF16)] * 6)


def _na_onehot():
    c = np.arange(GRID_W)
    dc = np.clip(c[None, :] - c[:, None], -(NA_COLS - 1), NA_COLS - 1) + NA_COLS - 1
    oh = np.zeros((LANES, GRID_W * GRID_W), np.float32)
    oh[dc.reshape(-1), np.arange(GRID_W * GRID_W)] = 1.0
    return jnp.asarray(oh)


def _na_colmask():
    c = np.arange(GRID_W)
    start = np.clip(c - NA_COLS // 2, 0, GRID_W - NA_COLS)
    ok = (c[None, :] >= start[:, None]) & (c[None, :] < start[:, None] + NA_COLS)
    return jnp.asarray(np.where(ok, 0.0, NEG).astype(np.float32))


N_DR = 2 * NA_ROWS - 1


def _na_bias(rpb):
    rp = jnp.zeros((GRID_W, LANES), F32).at[:4 * N_DR, :2 * NA_COLS - 1].set(rpb.reshape(4 * N_DR, 2 * NA_COLS - 1))

    def expand(r_ref, oh_ref, o_ref):
        o_ref[...] = _dot_exact(r_ref[...], oh_ref[...])

    r = pl.pallas_call(expand, name="na_bias_expand",
                       out_shape=jax.ShapeDtypeStruct((GRID_W, GRID_W * GRID_W), F32))(rp, _na_onehot())
    r = r[:4 * N_DR].reshape(4, N_DR, GRID_W, GRID_W)

    def build(r_ref, m_ref, o_ref):
        for h in range(4):
            for c in range(NA_ROWS):
                for i in range(NA_ROWS):
                    o_ref[h, c, :, i * GRID_W:(i + 1) * GRID_W] = r_ref[h, i - c + NA_ROWS - 1] + m_ref[...]

    return pl.pallas_call(build, name="na_bias_build",
                          out_shape=jax.ShapeDtypeStruct((4, NA_ROWS, GRID_W, NA_ROWS * GRID_W), F32))(r, _na_colmask())


def _na_bias_bwd(dbias):
    def fold(d_ref, o_ref):
        for h in range(4):
            for a in range(N_DR):
                acc = jnp.zeros((GRID_W, GRID_W), F32)
                for c in range(NA_ROWS):
                    i = a + c - (NA_ROWS - 1)
                    if 0 <= i < NA_ROWS:
                        acc = acc + d_ref[h, c, :, i * GRID_W:(i + 1) * GRID_W]
                o_ref[h, a] = acc

    dr = pl.pallas_call(fold, name="na_bias_fold",
                        out_shape=jax.ShapeDtypeStruct((4, N_DR, GRID_W, GRID_W), F32))(dbias)
    dr = jnp.zeros((GRID_W, GRID_W * GRID_W), F32).at[:4 * N_DR].set(dr.reshape(4 * N_DR, GRID_W * GRID_W))

    def contract(d_ref, oh_ref, o_ref):
        o_ref[...] = lax.dot_general(d_ref[...], oh_ref[...], _NT, precision=lax.Precision.HIGHEST,
                                     preferred_element_type=F32)

    g = pl.pallas_call(contract, name="na_bias_contract",
                       out_shape=jax.ShapeDtypeStruct((GRID_W, LANES), F32))(dr, _na_onehot())
    return g[:4 * N_DR, :2 * NA_COLS - 1].reshape(4, N_DR, 2 * NA_COLS - 1)


def _na_window(r, n_rows):
    rs = jnp.clip(r - NA_ROWS // 2, 0, n_rows - NA_ROWS)
    return rs, r - rs


def _na_fwd(q, k, v, bias):
    L = q.shape[0]
    n_rows = L // GRID_W
    tm = _tile(L, 512, GRID_W)
    nt, nr = L // tm, tm // GRID_W
    win = NA_ROWS * GRID_W

    def kern(q_ref, k_ref, v_ref, b_ref, o_ref):
        i = pl.program_id(1)
        lane = lax.broadcasted_iota(jnp.int32, (GRID_W, LANES), 1)

        def body(rr, carry):
            rs, c = _na_window(i * nr + rr, n_rows)
            rows = pl.ds(pl.multiple_of(rr * GRID_W, GRID_W), GRID_W)
            wrows = pl.ds(pl.multiple_of(rs * GRID_W, GRID_W), win)
            qv, kw, vw = q_ref[rows, :], k_ref[wrows, :], v_ref[wrows, :]
            o = jnp.zeros((GRID_W, LANES), F32)
            for hh in range(2):
                mine = (lane < HEAD_DIM) == (hh == 0)
                qm = jnp.where(mine, qv, jnp.zeros_like(qv))
                s = _dot(qm, kw, _NT) * ATT_SCALE + b_ref[hh, c]
                e = jnp.exp(s - jnp.max(s, axis=-1, keepdims=True))
                pn = (e / jnp.sum(e, axis=-1, keepdims=True)).astype(BF16)
                o = jnp.where(mine, _dot(pn, vw), o)
            o_ref[rows, :] = o.astype(BF16)
            return carry

        lax.fori_loop(0, nr, body, 0)

    qspec = pl.BlockSpec((tm, LANES), lambda j, i: (i, j))
    kvspec = pl.BlockSpec((L, LANES), lambda j, i: (0, j))
    return pl.pallas_call(
        kern, name="na_fwd", grid=(2, nt),
        in_specs=[qspec, kvspec, kvspec, pl.BlockSpec((2, NA_ROWS, GRID_W, win), lambda j, i: (j, 0, 0, 0))],
        out_specs=qspec, out_shape=jax.ShapeDtypeStruct((L, GROUP_W), BF16), compiler_params=_ARB(2),
    )(q, k, v, bias)


def _na_bwd(dy, dy_block, q, k, v, bias):
    L = q.shape[0]
    n_rows = L // GRID_W
    tm = _tile(L, 512, GRID_W)
    nt, nr = L // tm, tm // GRID_W
    win = NA_ROWS * GRID_W

    def kern(dy_ref, q_ref, k_ref, v_ref, b_ref, dq_ref, dk_ref, dv_ref, db_ref):
        i = pl.program_id(1)

        @pl.when(i == 0)
        def _():
            dk_ref[...] = jnp.zeros_like(dk_ref)
            dv_ref[...] = jnp.zeros_like(dv_ref)
            db_ref[...] = jnp.zeros_like(db_ref)

        lane = lax.broadcasted_iota(jnp.int32, (GRID_W, LANES), 1)

        def body(rr, carry):
            rs, c = _na_window(i * nr + rr, n_rows)
            rows = pl.ds(pl.multiple_of(rr * GRID_W, GRID_W), GRID_W)
            wrows = pl.ds(pl.multiple_of(rs * GRID_W, GRID_W), win)
            qv, kw, vw = q_ref[rows, :], k_ref[wrows, :], v_ref[wrows, :]
            dyv = dy_ref[rows, :].astype(BF16)
            dq = jnp.zeros((GRID_W, LANES), F32)
            dkw = jnp.zeros((win, LANES), F32)
            dvw = jnp.zeros((win, LANES), F32)
            for hh in range(2):
                mine = (lane < HEAD_DIM) == (hh == 0)
                qm = jnp.where(mine, qv, jnp.zeros_like(qv))
                dom = jnp.where(mine, dyv, jnp.zeros_like(dyv))
                s = _dot(qm, kw, _NT) * ATT_SCALE + b_ref[hh, c]
                e = jnp.exp(s - jnp.max(s, axis=-1, keepdims=True))
                pn = e / jnp.sum(e, axis=-1, keepdims=True)
                dp = _dot(dom, vw, _NT)
                ds = pn * (dp - jnp.sum(pn * dp, axis=-1, keepdims=True))
                db_ref[hh, c] += ds
                dsb = ds.astype(BF16)
                dq = jnp.where(mine, _dot(dsb, kw) * ATT_SCALE, dq)
                dkw = dkw + _dot(dsb, qm, _TN) * ATT_SCALE
                dvw = dvw + _dot(pn.astype(BF16), dom, _TN)
            dq_ref[rows, :] = dq.astype(BF16)
            dk_ref[wrows, :] += dkw
            dv_ref[wrows, :] += dvw
            return carry

        lax.fori_loop(0, nr, body, 0)

    qspec = pl.BlockSpec((tm, LANES), lambda j, i: (i, j))
    kvspec = pl.BlockSpec((L, LANES), lambda j, i: (0, j))
    bspec = pl.BlockSpec((2, NA_ROWS, GRID_W, win), lambda j, i: (j, 0, 0, 0))
    return pl.pallas_call(
        kern, name="na_bwd", grid=(2, nt),
        in_specs=[pl.BlockSpec((tm, LANES), lambda j, i: (i, dy_block + j)), qspec, kvspec, kvspec, bspec],
        out_specs=[qspec, kvspec, kvspec, bspec],
        out_shape=[jax.ShapeDtypeStruct((L, GROUP_W), BF16), jax.ShapeDtypeStruct((L, GROUP_W), F32),
                   jax.ShapeDtypeStruct((L, GROUP_W), F32),
                   jax.ShapeDtypeStruct((4, NA_ROWS, GRID_W, win), F32)],
        compiler_params=_ARB(2),
    )(dy, q, k, v, bias)


def _dil_specs(n, tq):
    R = DIL_RADIUS
    step, nb = tq // R, n // R
    main = pl.BlockSpec((tq, LANES), lambda j, i: (i, j))
    prev = pl.BlockSpec((R, LANES), lambda j, i: (jnp.maximum(i * step - 1, 0), j))
    nxt = pl.BlockSpec((R, LANES), lambda j, i: (jnp.minimum((i + 1) * step, nb - 1), j))
    return main, prev, nxt


def _dil_valid(i, tq, n):
    R = DIL_RADIUS
    row = lax.broadcasted_iota(jnp.int32, (tq, tq + 2 * R), 0)
    col = lax.broadcasted_iota(jnp.int32, (tq, tq + 2 * R), 1)
    kpos = i * tq - R + col
    return (jnp.abs(col - R - row) <= R) & (kpos >= 0) & (kpos < n)


def _dil_fwd(q, k, v, dil):
    L = q.shape[0]
    n = L // dil
    tq = _tile(n, 256, DIL_RADIUS)
    view = lambda t: t.reshape(n, dil * GROUP_W)

    def kern(q_ref, kp_ref, k_ref, kn_ref, vp_ref, v_ref, vn_ref, o_ref, l_ref):
        i = pl.program_id(1)
        valid = _dil_valid(i, tq, n)
        qv = q_ref[...]
        ka = jnp.concatenate([kp_ref[...], k_ref[...], kn_ref[...]], axis=0)
        va = jnp.concatenate([vp_ref[...], v_ref[...], vn_ref[...]], axis=0)
        lane = lax.broadcasted_iota(jnp.int32, (tq, LANES), 1)
        o = jnp.zeros((tq, LANES), F32)
        lse = jnp.zeros((tq, LANES), F32)
        for hh in range(2):
            mine = (lane < HEAD_DIM) == (hh == 0)
            qm = jnp.where(mine, qv, jnp.zeros_like(qv))
            s = jnp.where(valid, _dot(qm, ka, _NT) * ATT_SCALE, NEG)
            m = jnp.max(s, axis=-1, keepdims=True)
            e = jnp.exp(s - m)
            den = jnp.sum(e, axis=-1, keepdims=True)
            o = jnp.where(mine, _dot((e / den).astype(BF16), va), o)
            lse = jnp.where(mine, m + jnp.log(den), lse)
        o_ref[...] = o
        l_ref[...] = lse

    main, prev, nxt = _dil_specs(n, tq)
    o, lse = pl.pallas_call(
        kern, name=f"dil_fwd_{dil}", grid=(2 * dil, n // tq),
        in_specs=[main, prev, main, nxt, prev, main, nxt], out_specs=[main, main],
        out_shape=[jax.ShapeDtypeStruct((n, dil * GROUP_W), F32)] * 2, compiler_params=_ARB(2),
    )(view(q), view(k), view(k), view(k), view(v), view(v), view(v))
    return o.reshape(L, GROUP_W), lse.reshape(L, GROUP_W)


def _dil_bwd(q, k, v, do, lse, dterm, dil):
    L = q.shape[0]
    n = L // dil
    R = DIL_RADIUS
    tq = _tile(n, 256, R)
    nq = n // tq
    view = lambda t: t.reshape(n, dil * GROUP_W)

    def kern(q_ref, kp_ref, k_ref, kn_ref, vp_ref, v_ref, vn_ref, do_ref, l_ref, dt_ref, dq_ref, dk_ref, dv_ref):
        i = pl.program_id(1)

        @pl.when(i == 0)
        def _():
            dk_ref[...] = jnp.zeros_like(dk_ref)
            dv_ref[...] = jnp.zeros_like(dv_ref)

        valid = _dil_valid(i, tq, n)
        qv, dov = q_ref[...], do_ref[...]
        ka = jnp.concatenate([kp_ref[...], k_ref[...], kn_ref[...]], axis=0)
        va = jnp.concatenate([vp_ref[...], v_ref[...], vn_ref[...]], axis=0)
        lv, dtv = l_ref[...], dt_ref[...]
        lane = lax.broadcasted_iota(jnp.int32, (tq, LANES), 1)
        dq = jnp.zeros((tq, LANES), F32)
        dka = jnp.zeros((tq + 2 * R, LANES), F32)
        dva = jnp.zeros((tq + 2 * R, LANES), F32)
        for hh in range(2):
            mine = (lane < HEAD_DIM) == (hh == 0)
            qm = jnp.where(mine, qv, jnp.zeros_like(qv))
            dom = jnp.where(mine, dov, jnp.zeros_like(dov))
            c0 = hh * HEAD_DIM
            s = _dot(qm, ka, _NT) * ATT_SCALE
            pn = jnp.where(valid, jnp.exp(s - lv[:, c0:c0 + 1]), 0.0)
            ds = pn * (_dot(dom, va, _NT) - dtv[:, c0:c0 + 1])
            dsb = ds.astype(BF16)
            dq = jnp.where(mine, _dot(dsb, ka) * ATT_SCALE, dq)
            dka = dka + _dot(dsb, qm, _TN) * ATT_SCALE
            dva = dva + _dot(pn.astype(BF16), dom, _TN)
        dq_ref[...] = dq
        r0 = pl.multiple_of(i * tq, R)
        dk_ref[pl.ds(r0, tq), :] += dka[R:R + tq]
        dv_ref[pl.ds(r0, tq), :] += dva[R:R + tq]

        @pl.when(i > 0)
        def _():
            dk_ref[pl.ds(r0 - R, R), :] += dka[:R]
            dv_ref[pl.ds(r0 - R, R), :] += dva[:R]

        @pl.when(i < nq - 1)
        def _():
            dk_ref[pl.ds(r0 + tq, R), :] += dka[R + tq:]
            dv_ref[pl.ds(r0 + tq, R), :] += dva[R + tq:]

    main, prev, nxt = _dil_specs(n, tq)
    whole = pl.BlockSpec((n, LANES), lambda j, i: (0, j))
    shp = jax.ShapeDtypeStruct((n, dil * GROUP_W), F32)
    dq, dk, dv = pl.pallas_call(
        kern, name=f"dil_bwd_{dil}", grid=(2 * dil, nq),
        in_specs=[main, prev, main, nxt, prev, main, nxt, main, main, main], out_specs=[main, whole, whole],
        out_shape=[shp] * 3, compiler_params=_ARB(2),
    )(view(q), view(k), view(k), view(k), view(v), view(v), view(v), view(do), view(lse), view(dterm))
    return dq.reshape(L, GROUP_W), dk.reshape(L, GROUP_W), dv.reshape(L, GROUP_W)


def _dil_weights(lses):
    m = jnp.maximum(jnp.maximum(lses[0], lses[1]), lses[2])
    e = [jnp.exp(l - m) for l in lses]
    tot = e[0] + e[1] + e[2]
    return [x / tot for x in e]


def _dilated_fwd(q, k, v):
    res = [_dil_fwd(q, k, v, dil) for _, dil in DIL_PAIRS]

    def body(i, nt, o0, o1, o2, l0, l1, l2):
        w = _dil_weights((l0, l1, l2))
        return w[0] * o0 + w[1] * o1 + w[2] * o2

    ins = [Row(r[0]) for r in res] + [Row(r[1]) for r in res]
    return _rows("dil_combine", body, 512, ins, [("row", GROUP_W, BF16)])[0], res


def _dilated_bwd(dy, dy_cb, q, k, v, saved, cosf, sinf):
    def split(i, nt, dy, o0, o1, o2, l0, l1, l2, bd):
        w = _dil_weights((l0, l1, l2))
        y = w[0] * o0 + w[1] * o1 + w[2] * o2
        dyy = _head_sum(dy * y, bd)
        return tuple(wg * dy for wg in w) + tuple(wg * dyy for wg in w)

    ins = [Row(dy, GROUP_W, dy_cb)] + [Row(r[0]) for r in saved] + [Row(r[1]) for r in saved]
    outs = _rows("dil_split_bwd", split, 512, ins + [Full(_block_ones(GROUP_W, HEAD_DIM))],
                 [("row", GROUP_W, BF16)] * 3 + [("row", GROUP_W, F32)] * 3)
    g = [_dil_bwd(q, k, v, outs[b], saved[b][1], outs[3 + b], dil) for b, (_, dil) in enumerate(DIL_PAIRS)]

    def finish(i, nt, q0, q1, q2, k0, k1, k2, v0, v1, v2, c, s):
        dq, dk = q0 + q1 + q2, k0 + k1 + k2
        return dq * c + _swap_halves(dq * s), dk * c + _swap_halves(dk * s), v0 + v1 + v2

    ins = [Row(g[b][t]) for t in range(3) for b in range(3)] + [Row(cosf), Row(sinf)]
    return _rows("dil_finish_bwd", finish, 512, ins, [("row", GROUP_W, BF16)] * 3)


def _layer_fwd(x, W, l, cosf, sinf):
    h1 = _rms_fwd(x, W["mix_norm_pre"][l][None], "mix_norm")
    p = _mm(h1, W["w_in"][l], "nn", F32, "proj_in", tn=640)
    ya, sa = _gla_fwd(p, W, l)
    qb, kb, vb, qd, kd, vd = _attn_prep(p, cosf, sinf)
    bias = _na_bias(W["na_rpb"][l])
    yb = _na_fwd(qb, kb, vb, bias)
    yc, sc = _lru_fwd(p, W, l)
    yd, sd = _dilated_fwd(qd, kd, vd)
    ycat = jnp.concatenate([ya, yb, yc, yd], axis=1)
    ymix = _mm(ycat, W["w_out"][l], "nn", F32, "proj_out")
    xm = _rms_resid_fwd(x, ymix, W["mix_norm_post"][l][None], "mix_resid")
    h2 = _rms_fwd(xm, W["ffn_norm_pre"][l][None], "ffn_norm")
    gu = _mm(h2, W["ffn_w_in"][l], "nn", F32, "ffn_in")
    act = _swiglu_fwd(gu, "swiglu")
    f = _mm(act, W["ffn_w_out"][l], "nn", F32, "ffn_out")
    xo = _rms_resid_fwd(xm, f, W["ffn_norm_post"][l][None], "ffn_resid")
    saved = dict(x=x, h1=h1, p=p, sa=sa, att=(qb, kb, vb, qd, kd, vd), bias=bias, sc=sc, sd=sd, ycat=ycat, ymix=ymix,
                 xm=xm, h2=h2, gu=gu, act=act, f=f)
    return xo, saved


def _layer_bwd(dxo, W, l, S, cosf, sinf):
    g = {}
    df, g["ffn_norm_post"] = _rms_bwd(dxo, S["f"], W["ffn_norm_post"][l][None], "ffn_resid_bwd", out_dtype=BF16)
    dact = _mm(df, W["ffn_w_out"][l], "nt", F32, "ffn_out_dx")
    g["ffn_w_out"] = _mm(S["act"], df, "tn", F32, "ffn_out_dw", tk=512)
    dgu = _swiglu_bwd(dact, S["gu"], "swiglu_bwd")
    dh2 = _mm(dgu, W["ffn_w_in"][l], "nt", F32, "ffn_in_dx")
    g["ffn_w_in"] = _mm(S["h2"], dgu, "tn", F32, "ffn_in_dw", tk=512)
    dxm, g["ffn_norm_pre"] = _rms_bwd(dh2, S["xm"], W["ffn_norm_pre"][l][None], "ffn_norm_bwd", resid=dxo)
    dymix, g["mix_norm_post"] = _rms_bwd(dxm, S["ymix"], W["mix_norm_post"][l][None], "mix_resid_bwd", out_dtype=BF16)
    dycat = _mm(dymix, W["w_out"][l], "nt", F32, "proj_out_dx")
    g["w_out"] = _mm(S["ycat"], dymix, "tn", F32, "proj_out_dw", tk=512)
    p = S["p"]
    qb, kb, vb, qd, kd, vd = S["att"]
    (dqa, dka, dva, dga, dz), ga = _gla_bwd(_col(dycat, 0), p, W, l, S["sa"])
    dqb, dkb, dvb, dbias = _na_bwd(dycat, 2, qb, kb, vb, S["bias"])
    g["na_rpb"] = _na_bias_bwd(dbias)
    dxc, dgc, gc = _lru_bwd(_col(dycat, 2), p, W, l, S["sc"])
    dqd, dkd, dvd = _dilated_bwd(dycat, 3, qd, kd, vd, S["sd"], cosf, sinf)
    g.update(ga)
    g.update(gc)
    dp = jnp.concatenate([dqa, dka, dva, dga, dqb, dkb.astype(BF16), dvb.astype(BF16), dxc, dgc, dqd, dkd, dvd, dz], axis=1)
    dh1 = _mm(dp, W["w_in"][l], "nt", F32, "proj_in_dx", tk=640)
    g["w_in"] = _mm(S["h1"], dp, "tn", F32, "proj_in_dw", tn=640, tk=512)
    dx, g["mix_norm_pre"] = _rms_bwd(dh1, S["x"], W["mix_norm_pre"][l][None], "mix_norm_bwd", resid=dxm)
    for n in ("ffn_norm_post", "ffn_norm_pre", "mix_norm_post", "mix_norm_pre"):
        g[n] = g[n][0]
    return dx, g


def _col(a, cb):
    return a[:, cb * GROUP_W:(cb + 1) * GROUP_W]


MESH_AXES = ("x", "y", "c")


def _exchange(send, gather, name):
    R, C = send.shape[-2:]

    def body(s_ref, o_ref, send_sems, recv_sems, local_sem):
        x, y, c = (lax.axis_index(a) for a in MESH_AXES)
        me = 4 * x + 2 * y + c
        mine = pltpu.make_async_copy(s_ref if gather else s_ref.at[me], o_ref.at[me], local_sem)
        mine.start()
        copies = []
        for k in range(1, N_DEV):
            px, py, pc = x ^ ((k >> 2) & 1), y ^ ((k >> 1) & 1), c ^ (k & 1)
            src = s_ref if gather else s_ref.at[4 * px + 2 * py + pc]
            cp = pltpu.make_async_remote_copy(src_ref=src, dst_ref=o_ref.at[me], send_sem=send_sems.at[k - 1],
                                              recv_sem=recv_sems.at[k - 1], device_id=(px, py, pc),
                                              device_id_type=pl.DeviceIdType.MESH)
            cp.start()
            copies.append(cp)
        for cp in copies:
            cp.wait()
        mine.wait()

    return pl.pallas_call(
        body, name=name, out_shape=jax.ShapeDtypeStruct((N_DEV, R, C), send.dtype),
        in_specs=[pl.BlockSpec(memory_space=pl.ANY)], out_specs=pl.BlockSpec(memory_space=pl.ANY),
        scratch_shapes=[pltpu.SemaphoreType.DMA((N_DEV - 1,)), pltpu.SemaphoreType.DMA((N_DEV - 1,)),
                        pltpu.SemaphoreType.DMA],
    )(send)


def _sum_slots(recv, name):
    n, R, C = recv.shape
    tm = _tile(R, 256, 16)

    def kern(*refs):
        acc = refs[0][...].astype(F32)
        for r in refs[1:n]:
            acc = acc + r[...].astype(F32)
        refs[n][...] = acc

    return pl.pallas_call(
        kern, name=name, grid=(R // tm,),
        in_specs=[pl.BlockSpec((None, tm, C), lambda i, _s=s: (_s, i, 0)) for s in range(n)],
        out_specs=pl.BlockSpec((tm, C), lambda i: (i, 0)), out_shape=jax.ShapeDtypeStruct((R, C), F32),
        compiler_params=_ARB(1),
    )(*([recv] * n))


BIG = (("w_in", 2), ("w_out", 1), ("ffn_w_in", 2), ("ffn_w_out", 1))
SMALL_SHARDED = ("gla_w_gate", "gla_b_gate", "lru_conv_w", "lru_b_a", "lru_b_x", "lru_lambda")
REPLICATED = ("mix_norm_pre", "mix_norm_post", "gla_norm", "na_rpb", "lru_conv_b", "lru_w_a", "lru_w_x",
              "ffn_norm_pre", "ffn_norm_post")
WEIGHTS = ("mix_norm_pre", "mix_norm_post", "w_in", "gla_w_gate", "gla_b_gate", "gla_norm", "na_rpb", "lru_conv_w",
           "lru_conv_b", "lru_w_a", "lru_b_a", "lru_w_x", "lru_b_x", "lru_lambda", "w_out", "ffn_norm_pre",
           "ffn_norm_post", "ffn_w_in", "ffn_w_out")
FLAT_C = 1024


def _to_rows(vec, row_unit):
    n = vec.shape[-1]
    rows = -(-n // (FLAT_C * row_unit)) * row_unit
    pad = [(0, 0)] * (vec.ndim - 1) + [(0, rows * FLAT_C - n)]
    return jnp.pad(vec, pad).reshape(vec.shape[:-1] + (rows, FLAT_C))


def _unshard(parts, axis):
    t = jnp.moveaxis(parts, 0, axis)
    shp = list(t.shape)
    return t.reshape(shp[:axis] + [shp[axis] * shp[axis + 1]] + shp[axis + 2:])


def _shards(full, axis):
    shp = list(full.shape)
    t = full.reshape(shp[:axis] + [N_DEV, shp[axis] // N_DEV] + shp[axis + 1:])
    return jnp.moveaxis(t, axis, 0)


def _permute_w_in(w):
    z0 = 4 * GROUP_W
    pad = jnp.zeros(w.shape[:-1] + (P_COLS - w.shape[-1],), w.dtype)
    return jnp.concatenate([w[..., :z0], w[..., z0 + 2 * GLA_RANK:], w[..., z0:z0 + 2 * GLA_RANK], pad], axis=-1)


def _unpermute_w_in(w):
    z0, zn = 4 * GROUP_W, 12 * GROUP_W
    return jnp.concatenate([w[..., :z0], w[..., zn:zn + 2 * GLA_RANK], w[..., z0:zn]], axis=-1)


def _gather_weights(W):
    U16 = jnp.uint16
    big = [lax.bitcast_convert_type(W[n].astype(BF16), U16).reshape(-1) for n, _ in BIG]
    small = jnp.concatenate([W[n].reshape(-1) for n in SMALL_SHARDED])
    small16 = lax.bitcast_convert_type(small, U16).reshape(-1)
    got = _exchange(_to_rows(jnp.concatenate(big + [small16]), 16), True, "gather_weights").reshape(N_DEV, -1)
    full, off = dict(W), 0
    for n, axis in BIG:
        size = W[n].size
        part = lax.bitcast_convert_type(got[:, off:off + size], BF16)
        full[n] = _unshard(part.reshape((N_DEV,) + W[n].shape), axis)
        off += size
    sm = lax.bitcast_convert_type(got[:, off:off + 2 * small.size].reshape(N_DEV, small.size, 2), F32)
    off = 0
    for n in SMALL_SHARDED:
        size = W[n].size
        full[n] = _unshard(sm[:, off:off + size].reshape((N_DEV,) + W[n].shape), W[n].ndim - 1)
        off += size
    full["w_in"] = _permute_w_in(full["w_in"])
    return full


def _reduce_grads(G, W):
    big = jnp.concatenate([_shards(G[n], axis).reshape(N_DEV, -1) for n, axis in BIG], axis=1)
    got = _exchange(_to_rows(big.astype(BF16), 16), False, "exchange_grads")
    gbig = _sum_slots(got, "sum_grads").reshape(-1)
    small = jnp.concatenate([_shards(G[n], G[n].ndim - 1).reshape(N_DEV, -1) for n in SMALL_SHARDED], axis=1)
    repl = jnp.concatenate([G[n].reshape(-1) for n in REPLICATED])
    both = jnp.concatenate([small, jnp.broadcast_to(repl, (N_DEV, repl.size))], axis=1)
    gsm = _sum_slots(_exchange(_to_rows(both, 8), False, "exchange_small_grads"), "sum_small_grads").reshape(-1)
    out, off = {}, 0
    for n, _ in BIG:
        out[n] = gbig[off:off + W[n].size].reshape(W[n].shape)
        off += W[n].size
    off = 0
    for n in SMALL_SHARDED + REPLICATED:
        out[n] = gsm[off:off + W[n].size].reshape(W[n].shape)
        off += W[n].size
    return out


def _update(W, G, M, V):
    delta, new_m, new_v = {}, {}, {}
    for n, _ in BIG:
        two_d = lambda a: a.reshape(-1, a.shape[-1])
        d, m, v = _adamw(two_d(W[n]), two_d(G[n]), two_d(M[n]), two_d(V[n]), "adamw_" + n)
        delta[n], new_m[n], new_v[n] = (t.reshape(W[n].shape) for t in (d, m, v))
    rest = SMALL_SHARDED + REPLICATED
    pack = lambda D: _to_rows(jnp.concatenate([D[n].reshape(-1) for n in rest]), 16)
    d, m, v = _adamw(pack(W), pack(G), pack(M), pack(V), "adamw_small")
    off = 0
    for n in rest:
        sl = lambda t: t.reshape(-1)[off:off + W[n].size].reshape(W[n].shape)
        delta[n], new_m[n], new_v[n] = sl(d), sl(m), sl(v)
        off += W[n].size
    return delta, new_m, new_v


def _step(x, target, Wf):
    L = x.shape[0]
    depth = Wf["w_in"].shape[0]
    cosf, sinf = _rope_tables(L)
    saved = []
    for l in range(depth):
        x, S = _layer_fwd(x, Wf, l, cosf, sinf)
        saved.append(S)
    loss, dx = _loss_fwd_bwd(x, target)
    grads = [None] * depth
    for l in reversed(range(depth)):
        dx, grads[l] = _layer_bwd(dx, Wf, l, saved[l], cosf, sinf)
    G = {n: jnp.stack([g[n] for g in grads]) for n in WEIGHTS}
    G["w_in"] = _unpermute_w_in(G["w_in"])
    return loss, dx, G


def kernel(x, mix_norm_pre, mix_norm_post, w_in, gla_w_gate, gla_b_gate, gla_norm, na_rpb, lru_conv_w, lru_conv_b, lru_w_a, lru_b_a, lru_w_x, lru_b_x, lru_lambda, w_out, ffn_norm_pre, ffn_norm_post, ffn_w_in, ffn_w_out, loss_target, m_mix_norm_pre, m_mix_norm_post, m_w_in, m_gla_w_gate, m_gla_b_gate, m_gla_norm, m_na_rpb, m_lru_conv_w, m_lru_conv_b, m_lru_w_a, m_lru_b_a, m_lru_w_x, m_lru_b_x, m_lru_lambda, m_w_out, m_ffn_norm_pre, m_ffn_norm_post, m_ffn_w_in, m_ffn_w_out, v_mix_norm_pre, v_mix_norm_post, v_w_in, v_gla_w_gate, v_gla_b_gate, v_gla_norm, v_na_rpb, v_lru_conv_w, v_lru_conv_b, v_lru_w_a, v_lru_b_a, v_lru_w_x, v_lru_b_x, v_lru_lambda, v_w_out, v_ffn_norm_pre, v_ffn_norm_post, v_ffn_w_in, v_ffn_w_out):
    W = dict(zip(WEIGHTS, (mix_norm_pre, mix_norm_post, w_in, gla_w_gate, gla_b_gate, gla_norm, na_rpb, lru_conv_w, lru_conv_b, lru_w_a, lru_b_a, lru_w_x, lru_b_x, lru_lambda, w_out, ffn_norm_pre, ffn_norm_post, ffn_w_in, ffn_w_out)))
    M = dict(zip(WEIGHTS, (m_mix_norm_pre, m_mix_norm_post, m_w_in, m_gla_w_gate, m_gla_b_gate, m_gla_norm, m_na_rpb, m_lru_conv_w, m_lru_conv_b, m_lru_w_a, m_lru_b_a, m_lru_w_x, m_lru_b_x, m_lru_lambda, m_w_out, m_ffn_norm_pre, m_ffn_norm_post, m_ffn_w_in, m_ffn_w_out)))
    V = dict(zip(WEIGHTS, (v_mix_norm_pre, v_mix_norm_post, v_w_in, v_gla_w_gate, v_gla_b_gate, v_gla_norm, v_na_rpb, v_lru_conv_w, v_lru_conv_b, v_lru_w_a, v_lru_b_a, v_lru_w_x, v_lru_b_x, v_lru_lambda, v_w_out, v_ffn_norm_pre, v_ffn_norm_post, v_ffn_w_in, v_ffn_w_out)))
    Wf = _gather_weights(W)
    loss, dx, Gfull = _step(x[0], loss_target[0], Wf)
    loss = lax.psum(loss, MESH_AXES)
    G = _reduce_grads(Gfull, W)
    delta, new_m, new_v = _update(W, G, M, V)
    return (loss, dx[None], *[G[n] for n in WEIGHTS], *[delta[n] for n in WEIGHTS], *[new_m[n] for n in WEIGHTS],
            *[new_v[n] for n in WEIGHTS])
```

```python
import functools
import math

import numpy as np
import jax
import jax.numpy as jnp
from jax import lax
from jax.experimental import pallas as pl
from jax.experimental.pallas import tpu as pltpu

F32 = jnp.float32
BF16 = jnp.bfloat16

N_DEV = 8
HEAD_DIM = 64
GROUP_W = 256
GLA_RANK = 16
GLA_TAU = 16.0
GLA_CHUNK = 64
GRID_W = 64
NA_ROWS = 8
NA_COLS = 16
LRU_C = 8.0
DIL_PAIRS = ((128, 1), (512, 4), (2048, 16))
DIL_RADIUS = 64
ROPE_THETA = 10000.0
EPS = 1e-6
ATT_SCALE = HEAD_DIM ** -0.5
NEG = -1e30
LANES = 128
P_COLS = 12 * GROUP_W + LANES
Z_BLOCK = 12 * GROUP_W // LANES

ADAM_LR = 0.001
ADAM_B1 = 0.9
ADAM_B2 = 0.999
ADAM_EPS = 1e-08
ADAM_WD = 0.01
ADAM_STEP = 10

VMEM_LIMIT = 56 * 1024 * 1024
_ARB = lambda n: pltpu.CompilerParams(dimension_semantics=("arbitrary",) * n, vmem_limit_bytes=VMEM_LIMIT)


def _tile(dim, pref, unit):
    t = min(pref, dim) // unit * unit
    while t >= unit:
        if dim % t == 0:
            return t
        t -= unit
    return dim


def _mm(a, b, mode, out_dtype, name, tm=512, tn=None, tk=None):
    if mode == "nn":
        (M, K), (_, N) = a.shape, b.shape
    elif mode == "nt":
        (M, K), (N, _) = a.shape, b.shape
    else:
        (K, M), (_, N) = a.shape, b.shape
    tm = _tile(M, tm, LANES if mode == "tn" else 8)
    tn = _tile(N, tn or N, LANES)
    tk = _tile(K, tk or K, LANES)
    nk = K // tk
    dims = {"nn": (((1,), (0,)), ((), ())), "nt": (((1,), (1,)), ((), ())), "tn": (((0,), (0,)), ((), ()))}[mode]

    def kern(a_ref, b_ref, o_ref, *acc):
        part = lax.dot_general(a_ref[...].astype(BF16), b_ref[...].astype(BF16), dims, preferred_element_type=F32)
        if nk == 1:
            o_ref[...] = part.astype(out_dtype)
            return
        k = pl.program_id(2)

        @pl.when(k == 0)
        def _():
            acc[0][...] = part

        @pl.when(jnp.logical_and(k > 0, k < nk - 1))
        def _():
            acc[0][...] += part

        @pl.when(k == nk - 1)
        def _():
            o_ref[...] = (acc[0][...] + part).astype(out_dtype)

    a_spec = pl.BlockSpec((tk, tm), lambda i, j, k: (k, i)) if mode == "tn" else pl.BlockSpec((tm, tk), lambda i, j, k: (i, k))
    b_spec = pl.BlockSpec((tn, tk), lambda i, j, k: (j, k)) if mode == "nt" else pl.BlockSpec((tk, tn), lambda i, j, k: (k, j))
    return pl.pallas_call(
        kern, name=name, grid=(M // tm, N // tn, nk),
        in_specs=[a_spec, b_spec], out_specs=pl.BlockSpec((tm, tn), lambda i, j, k: (i, j)),
        out_shape=jax.ShapeDtypeStruct((M, N), out_dtype),
        scratch_shapes=[pltpu.VMEM((tm, tn), F32)] if nk > 1 else [],
        compiler_params=_ARB(3),
    )(a, b)


class Row:
    def __init__(self, a, width=None, cb=0, halo=False):
        self.a, self.width, self.cb, self.halo = a, width, cb, halo


class Full:
    def __init__(self, a):
        self.a = a


HALO = 8


def _rows(name, body, tm, ins, outs):
    L = next(s.a.shape[0] for s in ins if isinstance(s, Row))
    tm = _tile(L, tm, 16)
    nt = L // tm
    nb8 = L // HALO
    step = tm // HALO
    in_specs, arrays, layout = [], [], []
    for s in ins:
        if isinstance(s, Full):
            nd = s.a.ndim
            in_specs.append(pl.BlockSpec(s.a.shape, lambda i, _nd=nd: (0,) * _nd))
            arrays.append(s.a)
            layout.append(1)
        else:
            w = s.width or s.a.shape[1]
            in_specs.append(pl.BlockSpec((tm, w), lambda i, _cb=s.cb: (i, _cb)))
            arrays.append(s.a)
            if s.halo:
                in_specs.append(pl.BlockSpec((HALO, w), lambda i, _cb=s.cb: (jnp.maximum(i * step - 1, 0), _cb)))
                in_specs.append(pl.BlockSpec((HALO, w), lambda i, _cb=s.cb: (jnp.minimum((i + 1) * step, nb8 - 1), _cb)))
                arrays += [s.a, s.a]
                layout.append(3)
            else:
                layout.append(1)
    out_specs, out_shapes = [], []
    for kind, shp, dt in outs:
        if kind == "row":
            out_specs.append(pl.BlockSpec((tm, shp), lambda i: (i, 0)))
            out_shapes.append(jax.ShapeDtypeStruct((L, shp), dt))
        else:
            out_specs.append(pl.BlockSpec(shp, lambda i, _n=len(shp): (0,) * _n))
            out_shapes.append(jax.ShapeDtypeStruct(shp, dt))
    n_in = len(arrays)

    def kern(*refs):
        i = pl.program_id(0)
        vals, p = [], 0
        for n in layout:
            if n == 1:
                vals.append(refs[p][...])
            else:
                vals.append((refs[p + 1][...], refs[p][...], refs[p + 2][...]))
            p += n
        res = body(i, nt, *vals)
        if not isinstance(res, (tuple, list)):
            res = (res,)
        for (kind, shp, dt), o_ref, r in zip(outs, refs[n_in:], res):
            if kind == "row":
                o_ref[...] = r.astype(dt)
            else:
                @pl.when(i == 0)
                def _(o_ref=o_ref):
                    o_ref[...] = jnp.zeros_like(o_ref)
                o_ref[...] += r.astype(dt)

    res = pl.pallas_call(
        kern, name=name, grid=(nt,), in_specs=in_specs, out_specs=out_specs, out_shape=out_shapes,
        compiler_params=_ARB(1),
    )(*arrays)
    return res


def _shift(h, o, i, nt):
    prev, cur, nxt = h
    if o == 0:
        return cur
    tm = cur.shape[0]
    cat = jnp.concatenate([prev, cur, nxt], axis=0)
    sh = pltpu.roll(cat, (-o) % (tm + 2 * HALO), axis=0)[HALO:HALO + tm]
    row = lax.broadcasted_iota(jnp.int32, cur.shape, 0)
    if o < 0:
        ok = jnp.logical_or(i > 0, row >= -o)
    else:
        ok = jnp.logical_or(i < nt - 1, row < tm - o)
    return jnp.where(ok, sh, 0.0)


def _colsum(v):
    return jnp.sum(v, axis=0, keepdims=True)


def _sigmoid(x):
    return 1.0 / (1.0 + jnp.exp(-x))


def _softplus(x):
    return jnp.maximum(x, 0.0) + jnp.log1p(jnp.exp(-jnp.abs(x)))


def _silu(x):
    return x * _sigmoid(x)


def _dsilu(x):
    s = _sigmoid(x)
    return s * (1.0 + x * (1.0 - s))


_GELU_C = math.sqrt(2.0 / math.pi)


def _gelu(x):
    return 0.5 * x * (1.0 + jnp.tanh(_GELU_C * (x + 0.044715 * x * x * x)))


def _dgelu(x):
    t = jnp.tanh(_GELU_C * (x + 0.044715 * x * x * x))
    return 0.5 * (1.0 + t) + 0.5 * x * (1.0 - t * t) * _GELU_C * (1.0 + 3.0 * 0.044715 * x * x)


def _head_sum(v, bd):
    return jnp.dot(v, bd, precision=lax.Precision.HIGHEST, preferred_element_type=F32)


def _block_ones(n, blk):
    r = np.arange(n)
    return jnp.asarray((r[:, None] // blk == r[None, :] // blk).astype(np.float32))


def _rms_fwd(x, g, name):
    def body(i, nt, x, g):
        r = lax.rsqrt(jnp.mean(x * x, axis=-1, keepdims=True) + EPS)
        return x * r * g
    return _rows(name, body, 256, [Row(x), Full(g)], [("row", x.shape[1], BF16)])[0]


def _rms_resid_fwd(x, y, g, name):
    def body(i, nt, x, y, g):
        r = lax.rsqrt(jnp.mean(y * y, axis=-1, keepdims=True) + EPS)
        return x + y * r * g
    return _rows(name, body, 256, [Row(x), Row(y), Full(g)], [("row", x.shape[1], F32)])[0]


def _rms_bwd(dy, x, g, name, resid=None, out_dtype=F32):
    D = x.shape[1]

    def body(i, nt, dy, x, g, *rest):
        dy = dy.astype(F32)
        r = lax.rsqrt(jnp.mean(x * x, axis=-1, keepdims=True) + EPS)
        xh = x * r
        dxh = dy * g
        dx = r * (dxh - xh * jnp.mean(dxh * xh, axis=-1, keepdims=True))
        if rest:
            dx = dx + rest[0]
        return dx, _colsum(dy * xh)

    ins = [Row(dy), Row(x), Full(g)] + ([Row(resid)] if resid is not None else [])
    return _rows(name, body, 256, ins, [("row", D, out_dtype), ("acc", (1, D), F32)])


def _swiglu_fwd(gu, name):
    F = gu.shape[1] // 2

    def body(i, nt, gu):
        return _silu(gu[:, :F]) * gu[:, F:]
    return _rows(name, body, 128, [Row(gu)], [("row", F, BF16)])[0]


def _swiglu_bwd(dact, gu, name):
    F = gu.shape[1] // 2

    def body(i, nt, da, gu):
        gate, up = gu[:, :F], gu[:, F:]
        return jnp.concatenate([da * up * _dsilu(gate), da * _silu(gate)], axis=1)
    return _rows(name, body, 64, [Row(dact), Row(gu)], [("row", 2 * F, BF16)])[0]


def _loss_fwd_bwd(y, target):
    D = y.shape[1]

    def body(i, nt, y, t):
        err = y - t
        part = 0.5 * jnp.sum(jnp.mean(err * err, axis=-1, keepdims=True), axis=0, keepdims=True)
        return err * (1.0 / D), jnp.broadcast_to(part, (1, LANES))
    dy, loss = _rows("loss", body, 256, [Row(y), Row(target)], [("row", D, F32), ("acc", (1, LANES), F32)])
    return loss[0, 0], dy


def _adamw(w, g, m, v, name):
    C = w.shape[1]
    bc1 = 1.0 - ADAM_B1 ** ADAM_STEP
    bc2 = 1.0 - ADAM_B2 ** ADAM_STEP

    def body(i, nt, w, g, m, v):
        m = ADAM_B1 * m + (1.0 - ADAM_B1) * g
        v = ADAM_B2 * v + (1.0 - ADAM_B2) * (g * g)
        delta = -ADAM_LR * ((m / bc1) / (jnp.sqrt(v / bc2) + ADAM_EPS) + ADAM_WD * w)
        return delta, m, v
    return _rows(name, body, 256, [Row(w), Row(g), Row(m), Row(v)], [("row", C, F32)] * 3)


def _expm1(x):
    return jnp.tanh(0.5 * x) * (jnp.exp(x) + 1.0)


def _lru_gates(xh, i, nt, cw, cb, wa, wx, ba, bx, lam):
    xc = cb
    for j in range(4):
        xc = xc + cw[j:j + 1] * _shift(xh, j - 2, i, nt)
    xcb = xc.astype(BF16)
    gates = []
    for e in range(2):
        r = _sigmoid(jnp.dot(xcb, wa[e], preferred_element_type=F32) + ba[e:e + 1])
        ig = _sigmoid(jnp.dot(xcb, wx[e], preferred_element_type=F32) + bx[e:e + 1])
        sp = _softplus(-lam[e:e + 1])
        la = -LRU_C * r * sp
        gates.append((r, ig, sp, jnp.exp(la), jnp.sqrt(-_expm1(2.0 * la))))
    return xc, xcb, gates


def _scan2(af, uf, ab, ub, adjoint, name):
    L, W = af.shape
    tm = _tile(L, 512, 8)
    nt, nb = L // tm, tm // 8

    def blk(A, U, h, reverse, row):
        for d in (1, 2, 4):
            if reverse:
                ok, sh = row < 8 - d, 8 - d
            else:
                ok, sh = row >= d, d
            As = jnp.where(ok, pltpu.roll(A, sh, axis=0), 1.0)
            Us = jnp.where(ok, pltpu.roll(U, sh, axis=0), 0.0)
            U = A * Us + U
            A = A * As
        return A * h + U

    def kern(af_ref, uf_ref, ab_ref, ub_ref, of_ref, ob_ref, c_ref):
        @pl.when(pl.program_id(0) == 0)
        def _():
            c_ref[...] = jnp.zeros_like(c_ref)

        row = lax.broadcasted_iota(jnp.int32, (8, W), 0)
        full = lambda v: jnp.broadcast_to(v, (8, W))

        def body(j, carry):
            hF, aF, hB, aB = carry
            r0 = pl.multiple_of(j * 8, 8)
            r1 = pl.multiple_of((nb - 1 - j) * 8, 8)
            A, U = af_ref[pl.ds(r0, 8), :], uf_ref[pl.ds(r0, 8), :]
            if adjoint:
                C = jnp.where(row == 0, aF, pltpu.roll(A, 1, axis=0))
                aF = full(A[7:8])
            else:
                C = A
            H = blk(C, U, hF, False, row)
            of_ref[pl.ds(r0, 8), :] = H
            hF = full(H[7:8])
            A, U = ab_ref[pl.ds(r1, 8), :], ub_ref[pl.ds(r1, 8), :]
            if adjoint:
                C = jnp.where(row == 7, aB, pltpu.roll(A, 7, axis=0))
                aB = full(A[0:1])
            else:
                C = A
            H = blk(C, U, hB, True, row)
            ob_ref[pl.ds(r1, 8), :] = H
            hB = full(H[0:1])
            return hF, aF, hB, aB

        carry = lax.fori_loop(0, nb, body, (c_ref[0], c_ref[1], c_ref[2], c_ref[3]))
        for n in range(4):
            c_ref[n] = carry[n]

    fwd = pl.BlockSpec((tm, W), lambda i: (i, 0))
    bwd = pl.BlockSpec((tm, W), lambda i: (nt - 1 - i, 0))
    return pl.pallas_call(
        kern, name=name, grid=(nt,), in_specs=[fwd, fwd, bwd, bwd], out_specs=[fwd, bwd],
        out_shape=[jax.ShapeDtypeStruct((L, W), F32)] * 2,
        scratch_shapes=[pltpu.VMEM((4, 8, W), F32)], compiler_params=_ARB(1),
    )(af, uf, ab, ub)


def _block_diag(w):
    out = jnp.zeros((2, GROUP_W, GROUP_W), w.dtype)
    for h in range(4):
        out = out.at[:, h * 64:(h + 1) * 64, h * 64:(h + 1) * 64].set(w[:, h])
    return out.astype(BF16)


def _diag_blocks(w):
    return jnp.stack([w[:, h * 64:(h + 1) * 64, h * 64:(h + 1) * 64] for h in range(4)], axis=1)


def _lru_params(W, l):
    return [Full(W["lru_conv_w"][l]), Full(W["lru_conv_b"][l][None]), Full(_block_diag(W["lru_w_a"][l])),
            Full(_block_diag(W["lru_w_x"][l])), Full(W["lru_b_a"][l]), Full(W["lru_b_x"][l]), Full(W["lru_lambda"][l])]


def _lru_fwd(p, W, l):
    def pre(i, nt, xh, *prm):
        xc, _, g = _lru_gates(xh, i, nt, *prm)
        return g[0][3], g[0][4] * (g[0][1] * xc), g[1][3], g[1][4] * (g[1][1] * xc)

    a0, u0, a1, u1 = _rows("lru_pre", pre, 256, [Row(p, GROUP_W, 7, halo=True)] + _lru_params(W, l),
                           [("row", GROUP_W, F32)] * 4)
    hf, hb = _scan2(a0, u0, a1, u1, False, "lru_scan")
    yc = _rows("lru_post", lambda i, nt, hf, hb, gc: (hf + hb) * _gelu(gc), 512,
               [Row(hf), Row(hb), Row(p, GROUP_W, 8)], [("row", GROUP_W, BF16)])[0]
    return yc, (a0, a1, hf, hb)


def _lru_bwd(dy, dy_cb, p, W, l, saved):
    a0, a1, hf, hb = saved

    def post(i, nt, dy, hf, hb, gc):
        return dy * _gelu(gc), dy * (hf + hb) * _dgelu(gc)

    dh, dgc = _rows("lru_post_bwd", post, 512, [Row(dy, GROUP_W, dy_cb), Row(hf), Row(hb), Row(p, GROUP_W, 8)],
                    [("row", GROUP_W, F32), ("row", GROUP_W, BF16)])
    gb, gf = _scan2(a1, dh, a0, dh, True, "lru_scan_adj")

    def gates_bwd(i, nt, xh, gf, gb, hfh, hbh, cw, cb, wa, wx, ba, bx, lam):
        xc, xcb, g = _lru_gates(xh, i, nt, cw, cb, wa, wx, ba, bx, lam)
        dxc = jnp.zeros_like(xc)
        dwa, dwx, dba, dbx, dlam = [], [], [], [], []
        for e, du, hprev in ((0, gf, _shift(hfh, -1, i, nt)), (1, gb, _shift(hbh, 1, i, nt))):
            r, ig, sp, a, s = g[e]
            dxc = dxc + du * s * ig
            dla = du * hprev * a - (du * ig * xc) * a * a / s
            dza = (dla * (-LRU_C) * sp) * r * (1.0 - r)
            dzx = (du * s * xc) * ig * (1.0 - ig)
            dlam.append(_colsum(dla * r) * (LRU_C * _sigmoid(-lam[e:e + 1])))
            dba.append(_colsum(dza))
            dbx.append(_colsum(dzx))
            dzab, dzxb = dza.astype(BF16), dzx.astype(BF16)
            tn = (((0,), (0,)), ((), ()))
            nt_ = (((1,), (1,)), ((), ()))
            dwa.append(lax.dot_general(xcb, dzab, tn, preferred_element_type=F32))
            dwx.append(lax.dot_general(xcb, dzxb, tn, preferred_element_type=F32))
            dxc = dxc + lax.dot_general(dzab, wa[e], nt_, preferred_element_type=F32)
            dxc = dxc + lax.dot_general(dzxb, wx[e], nt_, preferred_element_type=F32)
        cat = lambda v: jnp.concatenate(v, axis=0)
        return dxc, jnp.stack(dwa), jnp.stack(dwx), cat(dba), cat(dbx), cat(dlam)

    dxc, dwa, dwx, dba, dbx, dlam = _rows(
        "lru_gates_bwd", gates_bwd, 256,
        [Row(p, GROUP_W, 7, halo=True), Row(gf), Row(gb), Row(hf, halo=True), Row(hb, halo=True)] + _lru_params(W, l),
        [("row", GROUP_W, F32), ("acc", (2, GROUP_W, GROUP_W), F32), ("acc", (2, GROUP_W, GROUP_W), F32),
         ("acc", (2, GROUP_W), F32), ("acc", (2, GROUP_W), F32), ("acc", (2, GROUP_W), F32)])

    def conv_bwd(i, nt, dh_, xh, cw):
        dxb = jnp.zeros_like(dh_[1])
        dcw = []
        for j in range(4):
            dxb = dxb + cw[j:j + 1] * _shift(dh_, 2 - j, i, nt)
            dcw.append(_colsum(dh_[1] * _shift(xh, j - 2, i, nt)))
        return dxb, jnp.concatenate(dcw, axis=0), _colsum(dh_[1])

    dxb, dcw, dcb = _rows("lru_conv_bwd", conv_bwd, 512,
                          [Row(dxc, halo=True), Row(p, GROUP_W, 7, halo=True), Full(W["lru_conv_w"][l])],
                          [("row", GROUP_W, BF16), ("acc", (4, GROUP_W), F32), ("acc", (1, GROUP_W), F32)])
    grads = dict(lru_conv_w=dcw, lru_conv_b=dcb[0], lru_w_a=_diag_blocks(dwa), lru_w_x=_diag_blocks(dwx),
                 lru_b_a=dba, lru_b_x=dbx, lru_lambda=dlam)
    return dxb, dgc, grads


_NT = (((1,), (1,)), ((), ()))
_TN = (((0,), (0,)), ((), ()))


def _dot(a, b, dims=None):
    if dims is None:
        return jnp.dot(a, b, preferred_element_type=F32)
    return lax.dot_general(a, b, dims, preferred_element_type=F32)


def _dot_exact(a, b):
    return jnp.dot(a, b, precision=lax.Precision.HIGHEST, preferred_element_type=F32)


def _gla_gate_w(w_gate, b_gate):
    wg = jnp.zeros((LANES, 2 * GROUP_W), F32)
    for e in range(2):
        wg = wg.at[e * GLA_RANK:(e + 1) * GLA_RANK, e * GROUP_W:(e + 1) * GROUP_W].set(w_gate[e])
    return wg.astype(BF16), b_gate.reshape(1, 2 * GROUP_W)


def _gla_gates_fwd(p, wg, bg):
    def body(i, nt, z, wg, bg):
        logit = _dot(z.astype(BF16), wg) + bg
        la = -_softplus(-logit) * (1.0 / GLA_TAU)
        return la[:, :GROUP_W], la[:, GROUP_W:]
    return _rows("gla_gates", body, 512, [Row(p, LANES, Z_BLOCK), Full(wg), Full(bg)], [("row", GROUP_W, F32)] * 2)


def _gla_gates_bwd(p, dla0, dla1, wg, bg):
    def body(i, nt, z, d0, d1, wg, bg):
        zb = z.astype(BF16)
        logit = _dot(zb, wg) + bg
        dlogit = jnp.concatenate([d0, d1], axis=1) * (1.0 / GLA_TAU) * _sigmoid(-logit)
        dlb = dlogit.astype(BF16)
        return _dot(dlb, wg, _NT), _dot(zb, dlb, _TN), _colsum(dlogit)
    return _rows("gla_gates_bwd", body, 512, [Row(p, LANES, Z_BLOCK), Row(dla0), Row(dla1), Full(wg), Full(bg)],
                 [("row", LANES, BF16), ("acc", (LANES, 2 * GROUP_W), F32), ("acc", (1, 2 * GROUP_W), F32)])


def _gla_order(reverse):
    t = np.arange(GLA_CHUNK)
    m = (t[None, :] >= t[:, None]) if reverse else (t[None, :] <= t[:, None])
    return m.astype(np.float32), (32, 0) if reverse else (31, 63)


def _stack_heads(x, bd):
    return jnp.where(bd, jnp.concatenate([x] * 4, axis=0), 0.0)


def _diag_heads(r, bd):
    r = jnp.where(bd, r, 0.0)
    return r[0:64] + r[64:128] + r[128:192] + r[192:256]


def _gla_chunk_terms(q_ref, k_ref, la_ref, rows, mv, mid, last):
    b = _dot_exact(mv, la_ref[rows, :])
    bm, bl = b[mid:mid + 1], b[last:last + 1]
    qs = q_ref[rows, :] * ATT_SCALE
    k = k_ref[rows, :]
    P, N, E, Fd = jnp.exp(b - bm), jnp.exp(bm - b), jnp.exp(b), jnp.exp(bl - b)
    return (P, N, E, Fd, jnp.exp(bl)), (qs * P, k * N, qs * E, k * Fd)


def _gla_chunk_fwd(p, la, reverse, name):
    L = la.shape[0]
    tm = _tile(L, 512, GLA_CHUNK)
    nt, nc = L // tm, tm // GLA_CHUNK
    m_np, (mid, last) = _gla_order(reverse)

    def kern(q_ref, k_ref, v_ref, la_ref, m_ref, bd_ref, o_ref, s_ref, st_ref):
        @pl.when(pl.program_id(0) == 0)
        def _():
            st_ref[...] = jnp.zeros_like(st_ref)

        mv = m_ref[...]
        bd = bd_ref[...] > 0.5
        keep = jnp.concatenate([mv] * 4, axis=0) > 0.5

        def body(cc, carry):
            c = nc - 1 - cc if reverse else cc
            rows = pl.ds(pl.multiple_of(c * GLA_CHUNK, GLA_CHUNK), GLA_CHUNK)
            (_, _, _, _, d), (qP, kN, qE, kF) = _gla_chunk_terms(q_ref, k_ref, la_ref, rows, mv, mid, last)
            vb = v_ref[rows, :].astype(BF16)
            st = st_ref[...]
            s_ref[c] = st
            a = jnp.where(keep, _dot(_stack_heads(qP, bd).astype(BF16), kN.astype(BF16), _NT), 0.0)
            o = _diag_heads(_dot(a.astype(BF16), vb), bd) + _dot(qE.astype(BF16), st.astype(BF16), _NT)
            o_ref[rows, :] = o
            st_ref[...] = st * d + jnp.where(bd, _dot(vb, kF.astype(BF16), _TN), 0.0)
            return carry

        lax.fori_loop(0, nc, body, 0)

    tile = (lambda i: (nt - 1 - i, 0)) if reverse else (lambda i: (i, 0))
    tile3 = (lambda i: (nt - 1 - i, 0, 0)) if reverse else (lambda i: (i, 0, 0))
    col = lambda cb: pl.BlockSpec((tm, GROUP_W), lambda i, _cb=cb: (tile(i)[0], _cb))
    const = lambda shp: pl.BlockSpec(shp, lambda i: (0, 0))
    return pl.pallas_call(
        kern, name=name, grid=(nt,),
        in_specs=[col(0), col(1), col(2), pl.BlockSpec((tm, GROUP_W), tile), const((GLA_CHUNK, GLA_CHUNK)),
                  const((GROUP_W, GROUP_W))],
        out_specs=[pl.BlockSpec((tm, GROUP_W), tile), pl.BlockSpec((nc, GROUP_W, GROUP_W), tile3)],
        out_shape=[jax.ShapeDtypeStruct((L, GROUP_W), F32),
                   jax.ShapeDtypeStruct((L // GLA_CHUNK, GROUP_W, GROUP_W), F32)],
        scratch_shapes=[pltpu.VMEM((GROUP_W, GROUP_W), F32)], compiler_params=_ARB(1),
    )(p, p, p, la, jnp.asarray(m_np), _block_ones(GROUP_W, HEAD_DIM))


def _gla_chunk_bwd(p, la, do, sprev, reverse, name, acc=None, out_dtype=F32):
    L = la.shape[0]
    tm = _tile(L, 512, GLA_CHUNK)
    nt, nc = L // tm, tm // GLA_CHUNK
    m_np, (mid, last) = _gla_order(reverse)
    n_acc = 0 if acc is None else 3

    def kern(q_ref, k_ref, v_ref, la_ref, do_ref, s_ref, m_ref, mt_ref, bd_ref, *rest):
        acc_refs, (dq_ref, dk_ref, dv_ref, dla_ref, dst_ref) = rest[:n_acc], rest[n_acc:]

        @pl.when(pl.program_id(0) == 0)
        def _():
            dst_ref[...] = jnp.zeros_like(dst_ref)

        mv, mt = m_ref[...], mt_ref[...]
        bd = bd_ref[...] > 0.5
        keep = jnp.concatenate([mv] * 4, axis=0) > 0.5
        row = lax.broadcasted_iota(jnp.int32, (GLA_CHUNK, GROUP_W), 0)

        def body(cc, carry):
            c = cc if reverse else nc - 1 - cc
            rows = pl.ds(pl.multiple_of(c * GLA_CHUNK, GLA_CHUNK), GLA_CHUNK)
            (P, N, E, Fd, d), (qP, kN, qE, kF) = _gla_chunk_terms(q_ref, k_ref, la_ref, rows, mv, mid, last)
            vb = v_ref[rows, :].astype(BF16)
            dov = do_ref[rows, :]
            dob = dov.astype(BF16)
            st, dst = s_ref[c], dst_ref[...]
            stb, dstb = st.astype(BF16), dst.astype(BF16)
            qst = _stack_heads(qP, bd).astype(BF16)
            dost = _stack_heads(dov, bd).astype(BF16)
            kNb, kFb, qEb = kN.astype(BF16), kF.astype(BF16), qE.astype(BF16)
            a = jnp.where(keep, _dot(qst, kNb, _NT), 0.0).astype(BF16)
            da = jnp.where(keep, _dot(dost, vb, _NT), 0.0).astype(BF16)
            dv = _dot(a, dost, _TN) + _dot(kFb, dstb, _NT)
            dqP = _diag_heads(_dot(da, kNb), bd)
            dkN = _dot(da, qst, _TN)
            dqE = _dot(dob, stb)
            dkF = _dot(vb, dstb)
            dd = _colsum(dst * st)
            dst_ref[...] = jnp.where(bd, _dot(dob, qEb, _TN), 0.0) + dst * d
            dq = (dqP * P + dqE * E) * ATT_SCALE
            dk = dkN * N + dkF * Fd
            tP, tN, tE, tF = dqP * qP, dkN * kN, dqE * qE, dkF * kF
            db = tP - tN + tE - tF
            db = db + jnp.where(row == mid, _colsum(tN - tP), 0.0) + jnp.where(row == last, _colsum(tF) + dd * d, 0.0)
            dla_ref[rows, :] = _dot_exact(mt, db)
            if n_acc:
                dq, dk, dv = dq + acc_refs[0][rows, :], dk + acc_refs[1][rows, :], dv + acc_refs[2][rows, :]
            dq_ref[rows, :] = dq.astype(out_dtype)
            dk_ref[rows, :] = dk.astype(out_dtype)
            dv_ref[rows, :] = dv.astype(out_dtype)
            return carry

        lax.fori_loop(0, nc, body, 0)

    tile = (lambda i: (i, 0)) if reverse else (lambda i: (nt - 1 - i, 0))
    tile3 = (lambda i: (i, 0, 0)) if reverse else (lambda i: (nt - 1 - i, 0, 0))
    col = lambda cb: pl.BlockSpec((tm, GROUP_W), lambda i, _cb=cb: (tile(i)[0], _cb))
    rowspec = pl.BlockSpec((tm, GROUP_W), tile)
    const = lambda shp: pl.BlockSpec(shp, lambda i: (0, 0))
    outs = pl.pallas_call(
        kern, name=name, grid=(nt,),
        in_specs=[col(0), col(1), col(2), rowspec, rowspec, pl.BlockSpec((nc, GROUP_W, GROUP_W), tile3),
                  const((GLA_CHUNK, GLA_CHUNK)), const((GLA_CHUNK, GLA_CHUNK)), const((GROUP_W, GROUP_W))]
                 + [rowspec] * n_acc,
        out_specs=[rowspec] * 4,
        out_shape=[jax.ShapeDtypeStruct((L, GROUP_W), out_dtype)] * 3 + [jax.ShapeDtypeStruct((L, GROUP_W), F32)],
        scratch_shapes=[pltpu.VMEM((GROUP_W, GROUP_W), F32)], compiler_params=_ARB(1),
    )(p, p, p, la, do, sprev, jnp.asarray(m_np), jnp.asarray(m_np.T.copy()), _block_ones(GROUP_W, HEAD_DIM),
      *([] if acc is None else acc))
    return outs


def _gla_fwd(p, W, l):
    wg, bg = _gla_gate_w(W["gla_w_gate"][l], W["gla_b_gate"][l])
    la0, la1 = _gla_gates_fwd(p, wg, bg)
    of, s0 = _gla_chunk_fwd(p, la0, False, "gla_fwd_f")
    ob, s1 = _gla_chunk_fwd(p, la1, True, "gla_fwd_b")

    def post(i, nt, of, ob, g, ng, bd):
        o = of + ob
        r = lax.rsqrt(_head_sum(o * o, bd) * (1.0 / HEAD_DIM) + EPS)
        return o * r * ng * _silu(g)

    ya = _rows("gla_post", post, 512, [Row(of), Row(ob), Row(p, GROUP_W, 3), Full(W["gla_norm"][l][None]),
                                       Full(_block_ones(GROUP_W, HEAD_DIM))], [("row", GROUP_W, BF16)])[0]
    return ya, (la0, la1, of, ob, s0, s1)


def _gla_bwd(dy, dy_cb, p, W, l, saved):
    la0, la1, of, ob, s0, s1 = saved
    wg, bg = _gla_gate_w(W["gla_w_gate"][l], W["gla_b_gate"][l])

    def post(i, nt, dy, of, ob, g, ng, bd):
        o = of + ob
        r = lax.rsqrt(_head_sum(o * o, bd) * (1.0 / HEAD_DIM) + EPS)
        oh = o * r
        don = dy * _silu(g)
        doh = don * ng
        do = r * (doh - oh * _head_sum(doh * oh, bd) * (1.0 / HEAD_DIM))
        return do, dy * (oh * ng) * _dsilu(g), _colsum(don * oh)

    do, dg, dng = _rows("gla_post_bwd", post, 512,
                        [Row(dy, GROUP_W, dy_cb), Row(of), Row(ob), Row(p, GROUP_W, 3), Full(W["gla_norm"][l][None]),
                         Full(_block_ones(GROUP_W, HEAD_DIM))],
                        [("row", GROUP_W, F32), ("row", GROUP_W, BF16), ("acc", (1, GROUP_W), F32)])
    dq, dk, dv, dla0 = _gla_chunk_bwd(p, la0, do, s0, False, "gla_bwd_f")
    dq, dk, dv, dla1 = _gla_chunk_bwd(p, la1, do, s1, True, "gla_bwd_b", acc=(dq, dk, dv), out_dtype=BF16)
    dz, dwg, dbg = _gla_gates_bwd(p, dla0, dla1, wg, bg)
    dw_gate = jnp.stack([dwg[e * GLA_RANK:(e + 1) * GLA_RANK, e * GROUP_W:(e + 1) * GROUP_W] for e in range(2)])
    grads = dict(gla_w_gate=dw_gate, gla_b_gate=dbg.reshape(2, GROUP_W), gla_norm=dng[0])
    return (dq, dk, dv, dg, dz), grads


def _rope_tables(L):
    pos = jnp.arange(L, dtype=F32)
    inv_freq = ROPE_THETA ** (-jnp.arange(0, HEAD_DIM, 2, dtype=F32) / HEAD_DIM)
    ang = pos[:, None] * inv_freq[None, :]
    cos, sin = jnp.cos(ang), jnp.sin(ang)
    return jnp.tile(jnp.concatenate([cos, cos], axis=1), (1, 4)), jnp.tile(jnp.concatenate([-sin, sin], axis=1), (1, 4))


def _swap_halves(t):
    lane = lax.broadcasted_iota(jnp.int32, t.shape, 1)
    first = (lane & (HEAD_DIM - 1)) < HEAD_DIM // 2
    return jnp.where(first, pltpu.roll(t, GROUP_W - HEAD_DIM // 2, axis=1), pltpu.roll(t, HEAD_DIM // 2, axis=1))


def _attn_prep(p, cosf, sinf):
    def body(i, nt, qb, kb, vb, qd, kd, vd, c, s):
        return qb, kb, vb, qd * c + _swap_halves(qd) * s, kd * c + _swap_halves(kd) * s, vd
    ins = [Row(p, GROUP_W, cb) for cb in (4, 5, 6, 9, 10, 11)] + [Row(cosf), Row(sinf)]
    return _rows("attn_prep", body, 512, ins, [("row", GROUP_W, BF16)] * 6)


def _na_onehot():
    c = np.arange(GRID_W)
    dc = np.clip(c[None, :] - c[:, None], -(NA_COLS - 1), NA_COLS - 1) + NA_COLS - 1
    oh = np.zeros((LANES, GRID_W * GRID_W), np.float32)
    oh[dc.reshape(-1), np.arange(GRID_W * GRID_W)] = 1.0
    return jnp.asarray(oh)


def _na_colmask():
    c = np.arange(GRID_W)
    start = np.clip(c - NA_COLS // 2, 0, GRID_W - NA_COLS)
    ok = (c[None, :] >= start[:, None]) & (c[None, :] < start[:, None] + NA_COLS)
    return jnp.asarray(np.where(ok, 0.0, NEG).astype(np.float32))


N_DR = 2 * NA_ROWS - 1


def _na_bias(rpb):
    rp = jnp.zeros((GRID_W, LANES), F32).at[:4 * N_DR, :2 * NA_COLS - 1].set(rpb.reshape(4 * N_DR, 2 * NA_COLS - 1))

    def expand(r_ref, oh_ref, o_ref):
        o_ref[...] = _dot_exact(r_ref[...], oh_ref[...])

    r = pl.pallas_call(expand, name="na_bias_expand",
                       out_shape=jax.ShapeDtypeStruct((GRID_W, GRID_W * GRID_W), F32))(rp, _na_onehot())
    r = r[:4 * N_DR].reshape(4, N_DR, GRID_W, GRID_W)

    def build(r_ref, m_ref, o_ref):
        for h in range(4):
            for c in range(NA_ROWS):
                for i in range(NA_ROWS):
                    o_ref[h, c, :, i * GRID_W:(i + 1) * GRID_W] = r_ref[h, i - c + NA_ROWS - 1] + m_ref[...]

    return pl.pallas_call(build, name="na_bias_build",
                          out_shape=jax.ShapeDtypeStruct((4, NA_ROWS, GRID_W, NA_ROWS * GRID_W), F32))(r, _na_colmask())


def _na_bias_bwd(dbias):
    def fold(d_ref, o_ref):
        for h in range(4):
            for a in range(N_DR):
                acc = jnp.zeros((GRID_W, GRID_W), F32)
                for c in range(NA_ROWS):
                    i = a + c - (NA_ROWS - 1)
                    if 0 <= i < NA_ROWS:
                        acc = acc + d_ref[h, c, :, i * GRID_W:(i + 1) * GRID_W]
                o_ref[h, a] = acc

    dr = pl.pallas_call(fold, name="na_bias_fold",
                        out_shape=jax.ShapeDtypeStruct((4, N_DR, GRID_W, GRID_W), F32))(dbias)
    dr = jnp.zeros((GRID_W, GRID_W * GRID_W), F32).at[:4 * N_DR].set(dr.reshape(4 * N_DR, GRID_W * GRID_W))

    def contract(d_ref, oh_ref, o_ref):
        o_ref[...] = lax.dot_general(d_ref[...], oh_ref[...], _NT, precision=lax.Precision.HIGHEST,
                                     preferred_element_type=F32)

    g = pl.pallas_call(contract, name="na_bias_contract",
                       out_shape=jax.ShapeDtypeStruct((GRID_W, LANES), F32))(dr, _na_onehot())
    return g[:4 * N_DR, :2 * NA_COLS - 1].reshape(4, N_DR, 2 * NA_COLS - 1)


def _na_window(r, n_rows):
    rs = jnp.clip(r - NA_ROWS // 2, 0, n_rows - NA_ROWS)
    return rs, r - rs


def _na_fwd(q, k, v, bias):
    L = q.shape[0]
    n_rows = L // GRID_W
    tm = _tile(L, 512, GRID_W)
    nt, nr = L // tm, tm // GRID_W
    win = NA_ROWS * GRID_W

    def kern(q_ref, k_ref, v_ref, b_ref, o_ref):
        i = pl.program_id(1)
        lane = lax.broadcasted_iota(jnp.int32, (GRID_W, LANES), 1)

        def body(rr, carry):
            rs, c = _na_window(i * nr + rr, n_rows)
            rows = pl.ds(pl.multiple_of(rr * GRID_W, GRID_W), GRID_W)
            wrows = pl.ds(pl.multiple_of(rs * GRID_W, GRID_W), win)
            qv, kw, vw = q_ref[rows, :], k_ref[wrows, :], v_ref[wrows, :]
            o = jnp.zeros((GRID_W, LANES), F32)
            for hh in range(2):
                mine = (lane < HEAD_DIM) == (hh == 0)
                qm = jnp.where(mine, qv, jnp.zeros_like(qv))
                s = _dot(qm, kw, _NT) * ATT_SCALE + b_ref[hh, c]
                e = jnp.exp(s - jnp.max(s, axis=-1, keepdims=True))
                pn = (e / jnp.sum(e, axis=-1, keepdims=True)).astype(BF16)
                o = jnp.where(mine, _dot(pn, vw), o)
            o_ref[rows, :] = o.astype(BF16)
            return carry

        lax.fori_loop(0, nr, body, 0)

    qspec = pl.BlockSpec((tm, LANES), lambda j, i: (i, j))
    kvspec = pl.BlockSpec((L, LANES), lambda j, i: (0, j))
    return pl.pallas_call(
        kern, name="na_fwd", grid=(2, nt),
        in_specs=[qspec, kvspec, kvspec, pl.BlockSpec((2, NA_ROWS, GRID_W, win), lambda j, i: (j, 0, 0, 0))],
        out_specs=qspec, out_shape=jax.ShapeDtypeStruct((L, GROUP_W), BF16), compiler_params=_ARB(2),
    )(q, k, v, bias)


def _na_bwd(dy, dy_block, q, k, v, bias):
    L = q.shape[0]
    n_rows = L // GRID_W
    tm = _tile(L, 512, GRID_W)
    nt, nr = L // tm, tm // GRID_W
    win = NA_ROWS * GRID_W

    def kern(dy_ref, q_ref, k_ref, v_ref, b_ref, dq_ref, dk_ref, dv_ref, db_ref):
        i = pl.program_id(1)

        @pl.when(i == 0)
        def _():
            dk_ref[...] = jnp.zeros_like(dk_ref)
            dv_ref[...] = jnp.zeros_like(dv_ref)
            db_ref[...] = jnp.zeros_like(db_ref)

        lane = lax.broadcasted_iota(jnp.int32, (GRID_W, LANES), 1)

        def body(rr, carry):
            rs, c = _na_window(i * nr + rr, n_rows)
            rows = pl.ds(pl.multiple_of(rr * GRID_W, GRID_W), GRID_W)
            wrows = pl.ds(pl.multiple_of(rs * GRID_W, GRID_W), win)
            qv, kw, vw = q_ref[rows, :], k_ref[wrows, :], v_ref[wrows, :]
            dyv = dy_ref[rows, :].astype(BF16)
            dq = jnp.zeros((GRID_W, LANES), F32)
            dkw = jnp.zeros((win, LANES), F32)
            dvw = jnp.zeros((win, LANES), F32)
            for hh in range(2):
                mine = (lane < HEAD_DIM) == (hh == 0)
                qm = jnp.where(mine, qv, jnp.zeros_like(qv))
                dom = jnp.where(mine, dyv, jnp.zeros_like(dyv))
                s = _dot(qm, kw, _NT) * ATT_SCALE + b_ref[hh, c]
                e = jnp.exp(s - jnp.max(s, axis=-1, keepdims=True))
                pn = e / jnp.sum(e, axis=-1, keepdims=True)
                dp = _dot(dom, vw, _NT)
                ds = pn * (dp - jnp.sum(pn * dp, axis=-1, keepdims=True))
                db_ref[hh, c] += ds
                dsb = ds.astype(BF16)
                dq = jnp.where(mine, _dot(dsb, kw) * ATT_SCALE, dq)
                dkw = dkw + _dot(dsb, qm, _TN) * ATT_SCALE
                dvw = dvw + _dot(pn.astype(BF16), dom, _TN)
            dq_ref[rows, :] = dq.astype(BF16)
            dk_ref[wrows, :] += dkw
            dv_ref[wrows, :] += dvw
            return carry

        lax.fori_loop(0, nr, body, 0)

    qspec = pl.BlockSpec((tm, LANES), lambda j, i: (i, j))
    kvspec = pl.BlockSpec((L, LANES), lambda j, i: (0, j))
    bspec = pl.BlockSpec((2, NA_ROWS, GRID_W, win), lambda j, i: (j, 0, 0, 0))
    return pl.pallas_call(
        kern, name="na_bwd", grid=(2, nt),
        in_specs=[pl.BlockSpec((tm, LANES), lambda j, i: (i, dy_block + j)), qspec, kvspec, kvspec, bspec],
        out_specs=[qspec, kvspec, kvspec, bspec],
        out_shape=[jax.ShapeDtypeStruct((L, GROUP_W), BF16), jax.ShapeDtypeStruct((L, GROUP_W), F32),
                   jax.ShapeDtypeStruct((L, GROUP_W), F32),
                   jax.ShapeDtypeStruct((4, NA_ROWS, GRID_W, win), F32)],
        compiler_params=_ARB(2),
    )(dy, q, k, v, bias)


def _dil_specs(n, tq):
    R = DIL_RADIUS
    step, nb = tq // R, n // R
    main = pl.BlockSpec((tq, LANES), lambda j, i: (i, j))
    prev = pl.BlockSpec((R, LANES), lambda j, i: (jnp.maximum(i * step - 1, 0), j))
    nxt = pl.BlockSpec((R, LANES), lambda j, i: (jnp.minimum((i + 1) * step, nb - 1), j))
    return main, prev, nxt


def _dil_valid(i, tq, n):
    R = DIL_RADIUS
    row = lax.broadcasted_iota(jnp.int32, (tq, tq + 2 * R), 0)
    col = lax.broadcasted_iota(jnp.int32, (tq, tq + 2 * R), 1)
    kpos = i * tq - R + col
    return (jnp.abs(col - R - row) <= R) & (kpos >= 0) & (kpos < n)


def _dil_fwd(q, k, v, dil):
    L = q.shape[0]
    n = L // dil
    tq = _tile(n, 256, DIL_RADIUS)
    view = lambda t: t.reshape(n, dil * GROUP_W)

    def kern(q_ref, kp_ref, k_ref, kn_ref, vp_ref, v_ref, vn_ref, o_ref, l_ref):
        i = pl.program_id(1)
        valid = _dil_valid(i, tq, n)
        qv = q_ref[...]
        ka = jnp.concatenate([kp_ref[...], k_ref[...], kn_ref[...]], axis=0)
        va = jnp.concatenate([vp_ref[...], v_ref[...], vn_ref[...]], axis=0)
        lane = lax.broadcasted_iota(jnp.int32, (tq, LANES), 1)
        o = jnp.zeros((tq, LANES), F32)
        lse = jnp.zeros((tq, LANES), F32)
        for hh in range(2):
            mine = (lane < HEAD_DIM) == (hh == 0)
            qm = jnp.where(mine, qv, jnp.zeros_like(qv))
            s = jnp.where(valid, _dot(qm, ka, _NT) * ATT_SCALE, NEG)
            m = jnp.max(s, axis=-1, keepdims=True)
            e = jnp.exp(s - m)
            den = jnp.sum(e, axis=-1, keepdims=True)
            o = jnp.where(mine, _dot((e / den).astype(BF16), va), o)
            lse = jnp.where(mine, m + jnp.log(den), lse)
        o_ref[...] = o
        l_ref[...] = lse

    main, prev, nxt = _dil_specs(n, tq)
    o, lse = pl.pallas_call(
        kern, name=f"dil_fwd_{dil}", grid=(2 * dil, n // tq),
        in_specs=[main, prev, main, nxt, prev, main, nxt], out_specs=[main, main],
        out_shape=[jax.ShapeDtypeStruct((n, dil * GROUP_W), F32)] * 2, compiler_params=_ARB(2),
    )(view(q), view(k), view(k), view(k), view(v), view(v), view(v))
    return o.reshape(L, GROUP_W), lse.reshape(L, GROUP_W)


def _dil_bwd(q, k, v, do, lse, dterm, dil):
    L = q.shape[0]
    n = L // dil
    R = DIL_RADIUS
    tq = _tile(n, 256, R)
    nq = n // tq
    view = lambda t: t.reshape(n, dil * GROUP_W)

    def kern(q_ref, kp_ref, k_ref, kn_ref, vp_ref, v_ref, vn_ref, do_ref, l_ref, dt_ref, dq_ref, dk_ref, dv_ref):
        i = pl.program_id(1)

        @pl.when(i == 0)
        def _():
            dk_ref[...] = jnp.zeros_like(dk_ref)
            dv_ref[...] = jnp.zeros_like(dv_ref)

        valid = _dil_valid(i, tq, n)
        qv, dov = q_ref[...], do_ref[...]
        ka = jnp.concatenate([kp_ref[...], k_ref[...], kn_ref[...]], axis=0)
        va = jnp.concatenate([vp_ref[...], v_ref[...], vn_ref[...]], axis=0)
        lv, dtv = l_ref[...], dt_ref[...]
        lane = lax.broadcasted_iota(jnp.int32, (tq, LANES), 1)
        dq = jnp.zeros((tq, LANES), F32)
        dka = jnp.zeros((tq + 2 * R, LANES), F32)
        dva = jnp.zeros((tq + 2 * R, LANES), F32)
        for hh in range(2):
            mine = (lane < HEAD_DIM) == (hh == 0)
            qm = jnp.where(mine, qv, jnp.zeros_like(qv))
            dom = jnp.where(mine, dov, jnp.zeros_like(dov))
            c0 = hh * HEAD_DIM
            s = _dot(qm, ka, _NT) * ATT_SCALE
            pn = jnp.where(valid, jnp.exp(s - lv[:, c0:c0 + 1]), 0.0)
            ds = pn * (_dot(dom, va, _NT) - dtv[:, c0:c0 + 1])
            dsb = ds.astype(BF16)
            dq = jnp.where(mine, _dot(dsb, ka) * ATT_SCALE, dq)
            dka = dka + _dot(dsb, qm, _TN) * ATT_SCALE
            dva = dva + _dot(pn.astype(BF16), dom, _TN)
        dq_ref[...] = dq
        r0 = pl.multiple_of(i * tq, R)
        dk_ref[pl.ds(r0, tq), :] += dka[R:R + tq]
        dv_ref[pl.ds(r0, tq), :] += dva[R:R + tq]

        @pl.when(i > 0)
        def _():
            dk_ref[pl.ds(r0 - R, R), :] += dka[:R]
            dv_ref[pl.ds(r0 - R, R), :] += dva[:R]

        @pl.when(i < nq - 1)
        def _():
            dk_ref[pl.ds(r0 + tq, R), :] += dka[R + tq:]
            dv_ref[pl.ds(r0 + tq, R), :] += dva[R + tq:]

    main, prev, nxt = _dil_specs(n, tq)
    whole = pl.BlockSpec((n, LANES), lambda j, i: (0, j))
    shp = jax.ShapeDtypeStruct((n, dil * GROUP_W), F32)
    dq, dk, dv = pl.pallas_call(
        kern, name=f"dil_bwd_{dil}", grid=(2 * dil, nq),
        in_specs=[main, prev, main, nxt, prev, main, nxt, main, main, main], out_specs=[main, whole, whole],
        out_shape=[shp] * 3, compiler_params=_ARB(2),
    )(view(q), view(k), view(k), view(k), view(v), view(v), view(v), view(do), view(lse), view(dterm))
    return dq.reshape(L, GROUP_W), dk.reshape(L, GROUP_W), dv.reshape(L, GROUP_W)


def _dil_weights(lses):
    m = jnp.maximum(jnp.maximum(lses[0], lses[1]), lses[2])
    e = [jnp.exp(l - m) for l in lses]
    tot = e[0] + e[1] + e[2]
    return [x / tot for x in e]


def _dilated_fwd(q, k, v):
    res = [_dil_fwd(q, k, v, dil) for _, dil in DIL_PAIRS]

    def body(i, nt, o0, o1, o2, l0, l1, l2):
        w = _dil_weights((l0, l1, l2))
        return w[0] * o0 + w[1] * o1 + w[2] * o2

    ins = [Row(r[0]) for r in res] + [Row(r[1]) for r in res]
    return _rows("dil_combine", body, 512, ins, [("row", GROUP_W, BF16)])[0], res


def _dilated_bwd(dy, dy_cb, q, k, v, saved, cosf, sinf):
    def split(i, nt, dy, o0, o1, o2, l0, l1, l2, bd):
        w = _dil_weights((l0, l1, l2))
        y = w[0] * o0 + w[1] * o1 + w[2] * o2
        dyy = _head_sum(dy * y, bd)
        return tuple(wg * dy for wg in w) + tuple(wg * dyy for wg in w)

    ins = [Row(dy, GROUP_W, dy_cb)] + [Row(r[0]) for r in saved] + [Row(r[1]) for r in saved]
    outs = _rows("dil_split_bwd", split, 512, ins + [Full(_block_ones(GROUP_W, HEAD_DIM))],
                 [("row", GROUP_W, BF16)] * 3 + [("row", GROUP_W, F32)] * 3)
    g = [_dil_bwd(q, k, v, outs[b], saved[b][1], outs[3 + b], dil) for b, (_, dil) in enumerate(DIL_PAIRS)]

    def finish(i, nt, q0, q1, q2, k0, k1, k2, v0, v1, v2, c, s):
        dq, dk = q0 + q1 + q2, k0 + k1 + k2
        return dq * c + _swap_halves(dq * s), dk * c + _swap_halves(dk * s), v0 + v1 + v2

    ins = [Row(g[b][t]) for t in range(3) for b in range(3)] + [Row(cosf), Row(sinf)]
    return _rows("dil_finish_bwd", finish, 512, ins, [("row", GROUP_W, BF16)] * 3)


def _layer_fwd(x, W, l, cosf, sinf):
    h1 = _rms_fwd(x, W["mix_norm_pre"][l][None], "mix_norm")
    p = _mm(h1, W["w_in"][l], "nn", F32, "proj_in")
    ya, sa = _gla_fwd(p, W, l)
    qb, kb, vb, qd, kd, vd = _attn_prep(p, cosf, sinf)
    bias = _na_bias(W["na_rpb"][l])
    yb = _na_fwd(qb, kb, vb, bias)
    yc, sc = _lru_fwd(p, W, l)
    yd, sd = _dilated_fwd(qd, kd, vd)
    ycat = jnp.concatenate([ya, yb, yc, yd], axis=1)
    ymix = _mm(ycat, W["w_out"][l], "nn", F32, "proj_out", tm=1024)
    xm = _rms_resid_fwd(x, ymix, W["mix_norm_post"][l][None], "mix_resid")
    h2 = _rms_fwd(xm, W["ffn_norm_pre"][l][None], "ffn_norm")
    gu = _mm(h2, W["ffn_w_in"][l], "nn", F32, "ffn_in", tn=2816)
    act = _swiglu_fwd(gu, "swiglu")
    f = _mm(act, W["ffn_w_out"][l], "nn", F32, "ffn_out")
    xo = _rms_resid_fwd(xm, f, W["ffn_norm_post"][l][None], "ffn_resid")
    saved = dict(x=x, h1=h1, p=p, sa=sa, att=(qb, kb, vb, qd, kd, vd), bias=bias, sc=sc, sd=sd, ycat=ycat, ymix=ymix,
                 xm=xm, h2=h2, gu=gu, act=act, f=f)
    return xo, saved


def _layer_bwd(dxo, W, l, S, cosf, sinf):
    g = {}
    df, g["ffn_norm_post"] = _rms_bwd(dxo, S["f"], W["ffn_norm_post"][l][None], "ffn_resid_bwd", out_dtype=BF16)
    dact = _mm(df, W["ffn_w_out"][l], "nt", F32, "ffn_out_dx")
    g["ffn_w_out"] = _mm(S["act"], df, "tn", BF16, "ffn_out_dw", tm=256, tk=4096)
    dgu = _swiglu_bwd(dact, S["gu"], "swiglu_bwd")
    dh2 = _mm(dgu, W["ffn_w_in"][l], "nt", F32, "ffn_in_dx")
    g["ffn_w_in"] = _mm(S["h2"], dgu, "tn", BF16, "ffn_in_dw", tm=1024, tn=512, tk=4096)
    dxm, g["ffn_norm_pre"] = _rms_bwd(dh2, S["xm"], W["ffn_norm_pre"][l][None], "ffn_norm_bwd", resid=dxo)
    dymix, g["mix_norm_post"] = _rms_bwd(dxm, S["ymix"], W["mix_norm_post"][l][None], "mix_resid_bwd", out_dtype=BF16)
    dycat = _mm(dymix, W["w_out"][l], "nt", F32, "proj_out_dx", tm=1024)
    g["w_out"] = _mm(S["ycat"], dymix, "tn", BF16, "proj_out_dw", tm=1024, tn=512, tk=4096)
    p = S["p"]
    qb, kb, vb, qd, kd, vd = S["att"]
    (dqa, dka, dva, dga, dz), ga = _gla_bwd(dycat, 0, p, W, l, S["sa"])
    dqb, dkb, dvb, dbias = _na_bwd(dycat, 2, qb, kb, vb, S["bias"])
    g["na_rpb"] = _na_bias_bwd(dbias)
    dxc, dgc, gc = _lru_bwd(dycat, 2, p, W, l, S["sc"])
    dqd, dkd, dvd = _dilated_bwd(dycat, 3, qd, kd, vd, S["sd"], cosf, sinf)
    g.update(ga)
    g.update(gc)
    dp = jnp.concatenate([dqa, dka, dva, dga, dqb, dkb.astype(BF16), dvb.astype(BF16), dxc, dgc, dqd, dkd, dvd, dz], axis=1)
    dh1 = _mm(dp, W["w_in"][l], "nt", F32, "proj_in_dx")
    g["w_in"] = _mm(S["h1"], dp, "tn", BF16, "proj_in_dw", tm=1024, tn=640, tk=4096)
    dx, g["mix_norm_pre"] = _rms_bwd(dh1, S["x"], W["mix_norm_pre"][l][None], "mix_norm_bwd", resid=dxm)
    for n in ("ffn_norm_post", "ffn_norm_pre", "mix_norm_post", "mix_norm_pre"):
        g[n] = g[n][0]
    return dx, g


MESH_AXES = ("x", "y", "c")


class Xfer:
    def __init__(self, arr, kind):
        self.arr, self.kind = arr, kind
        shp = arr.shape
        if kind == "all":
            self.out = (N_DEV,) + shp
        elif kind == "slot":
            self.out = shp
        elif kind == "rows":
            self.r = shp[1] // N_DEV
            self.out = (N_DEV, shp[0], self.r, shp[2])
        else:
            self.r = shp[1]
            self.out = (shp[0], N_DEV * shp[1], shp[2])

    def src(self, ref, peer):
        if self.kind == "slot":
            return ref.at[peer]
        if self.kind == "rows":
            return ref.at[:, pl.ds(peer * self.r, self.r), :]
        return ref

    def dst(self, ref, me):
        if self.kind == "place":
            return ref.at[:, pl.ds(me * self.r, self.r), :]
        return ref.at[me]


def _exchange(items, name):
    n = len(items)

    def body(*refs):
        ins, outs = refs[:n], refs[n:2 * n]
        send_sems, recv_sems, local_sems = refs[2 * n:]
        x, y, c = (lax.axis_index(a) for a in MESH_AXES)
        me = 4 * x + 2 * y + c
        copies = []
        for t, it in enumerate(items):
            mine = pltpu.make_async_copy(it.src(ins[t], me), it.dst(outs[t], me), local_sems.at[t])
            mine.start()
            copies.append(mine)
            for k in range(1, N_DEV):
                px, py, pc = x ^ ((k >> 2) & 1), y ^ ((k >> 1) & 1), c ^ (k & 1)
                s = t * (N_DEV - 1) + k - 1
                cp = pltpu.make_async_remote_copy(
                    src_ref=it.src(ins[t], 4 * px + 2 * py + pc), dst_ref=it.dst(outs[t], me),
                    send_sem=send_sems.at[s], recv_sem=recv_sems.at[s], device_id=(px, py, pc),
                    device_id_type=pl.DeviceIdType.MESH)
                cp.start()
                copies.append(cp)
        for cp in copies:
            cp.wait()

    anyspec = pl.BlockSpec(memory_space=pl.ANY)
    return pl.pallas_call(
        body, name=name, out_shape=[jax.ShapeDtypeStruct(it.out, it.arr.dtype) for it in items],
        in_specs=[anyspec] * n, out_specs=[anyspec] * n,
        scratch_shapes=[pltpu.SemaphoreType.DMA((n * (N_DEV - 1),)), pltpu.SemaphoreType.DMA((n * (N_DEV - 1),)),
                        pltpu.SemaphoreType.DMA((n,))],
    )(*[it.arr for it in items])


def _column_segments(width, permuted):
    z0, z1, zn = 4 * GROUP_W, 4 * GROUP_W + 2 * GLA_RANK, 12 * GROUP_W
    segs = []
    for d in range(N_DEV):
        lo, hi = d * width, (d + 1) * width
        if not permuted:
            segs.append([(0, width, lo)])
            continue
        runs = []
        for a, b, shift in ((0, z0, 0), (z0, z1, zn - z0), (z1, 10 ** 9, -(z1 - z0))):
            s, e = max(lo, a), min(hi, b)
            if s < e:
                runs.append((s - lo, e - lo, s + shift))
        segs.append(runs)
    return segs


def _cols_from_pieces(pieces, segs, cols, name):
    _, R, w = pieces.shape
    tm = _tile(R, 256, 16)
    used = max(f + (b - a) for runs in segs for a, b, f in runs)

    def kern(p_ref, o_ref):
        for d, runs in enumerate(segs):
            for a, b, f in runs:
                o_ref[:, f:f + (b - a)] = p_ref[d, :, a:b]
        if used < cols:
            o_ref[:, used:cols] = jnp.zeros((tm, cols - used), o_ref.dtype)

    return pl.pallas_call(
        kern, name=name, grid=(R // tm,), in_specs=[pl.BlockSpec((N_DEV, tm, w), lambda i: (0, i, 0))],
        out_specs=pl.BlockSpec((tm, cols), lambda i: (i, 0)), out_shape=jax.ShapeDtypeStruct((R, cols), pieces.dtype),
        compiler_params=_ARB(1),
    )(pieces)


def _pieces_from_cols(full, segs, w, name):
    R, cols = full.shape
    tm = _tile(R, 256, 16)

    def kern(f_ref, o_ref):
        for d, runs in enumerate(segs):
            for a, b, f in runs:
                o_ref[d, :, a:b] = f_ref[:, f:f + (b - a)]

    return pl.pallas_call(
        kern, name=name, grid=(R // tm,), in_specs=[pl.BlockSpec((tm, cols), lambda i: (i, 0))],
        out_specs=pl.BlockSpec((N_DEV, tm, w), lambda i: (0, i, 0)),
        out_shape=jax.ShapeDtypeStruct((N_DEV, R, w), full.dtype), compiler_params=_ARB(1),
    )(full)


def _sum_slots(recv, name):
    n, R, C = recv.shape
    tm = _tile(R, 256, 16)

    def kern(*refs):
        acc = refs[0][...].astype(F32)
        for r in refs[1:n]:
            acc = acc + r[...].astype(F32)
        refs[n][...] = acc

    return pl.pallas_call(
        kern, name=name, grid=(R // tm,),
        in_specs=[pl.BlockSpec((None, tm, C), lambda i, _s=s: (_s, i, 0)) for s in range(n)],
        out_specs=pl.BlockSpec((tm, C), lambda i: (i, 0)), out_shape=jax.ShapeDtypeStruct((R, C), F32),
        compiler_params=_ARB(1),
    )(*([recv] * n))


BIG = (("w_in", 2), ("w_out", 1), ("ffn_w_in", 2), ("ffn_w_out", 1))
SMALL_SHARDED = ("gla_w_gate", "gla_b_gate", "lru_conv_w", "lru_b_a", "lru_b_x", "lru_lambda")
REPLICATED = ("mix_norm_pre", "mix_norm_post", "gla_norm", "na_rpb", "lru_conv_b", "lru_w_a", "lru_w_x",
              "ffn_norm_pre", "ffn_norm_post")
WEIGHTS = ("mix_norm_pre", "mix_norm_post", "w_in", "gla_w_gate", "gla_b_gate", "gla_norm", "na_rpb", "lru_conv_w",
           "lru_conv_b", "lru_w_a", "lru_b_a", "lru_w_x", "lru_b_x", "lru_lambda", "w_out", "ffn_norm_pre",
           "ffn_norm_post", "ffn_w_in", "ffn_w_out")
FLAT_C = 1024


def _to_rows(vec, row_unit):
    n = vec.shape[-1]
    rows = -(-n // (FLAT_C * row_unit)) * row_unit
    pad = [(0, 0)] * (vec.ndim - 1) + [(0, rows * FLAT_C - n)]
    return jnp.pad(vec, pad).reshape(vec.shape[:-1] + (rows, FLAT_C))


def _unshard(parts, axis):
    t = jnp.moveaxis(parts, 0, axis)
    shp = list(t.shape)
    return t.reshape(shp[:axis] + [shp[axis] * shp[axis + 1]] + shp[axis + 2:])


def _shards(full, axis):
    shp = list(full.shape)
    t = full.reshape(shp[:axis] + [N_DEV, shp[axis] // N_DEV] + shp[axis + 1:])
    return jnp.moveaxis(t, axis, 0)


def _two_d(a):
    return a.reshape(-1, a.shape[-1])


def _gather_weights(W):
    depth, dm, w_in_w = W["w_in"].shape
    ffn_w = W["ffn_w_in"].shape[-1]
    small = jnp.concatenate([W[n].reshape(-1) for n in SMALL_SHARDED])
    small16 = _to_rows(lax.bitcast_convert_type(small, jnp.uint16).reshape(-1), 16)
    items = [Xfer(_two_d(W["w_in"].astype(BF16)), "all"), Xfer(_two_d(W["ffn_w_in"].astype(BF16)), "all"),
             Xfer(W["w_out"].astype(BF16), "place"), Xfer(W["ffn_w_out"].astype(BF16), "place"), Xfer(small16, "all")]
    p_in, p_ffn, w_out, ffn_w_out, sm = _exchange(items, "gather_weights")
    full = dict(W)
    full["w_in"] = _cols_from_pieces(p_in, _column_segments(w_in_w, True), P_COLS, "unpack_w_in").reshape(depth, dm, P_COLS)
    full["ffn_w_in"] = _cols_from_pieces(p_ffn, _column_segments(ffn_w, False), N_DEV * ffn_w,
                                         "unpack_ffn_w_in").reshape(depth, dm, N_DEV * ffn_w)
    full["w_out"], full["ffn_w_out"] = w_out, ffn_w_out
    sm = lax.bitcast_convert_type(sm.reshape(N_DEV, -1)[:, :2 * small.size].reshape(N_DEV, small.size, 2), F32)
    off = 0
    for n in SMALL_SHARDED:
        size = W[n].size
        full[n] = _unshard(sm[:, off:off + size].reshape((N_DEV,) + W[n].shape), W[n].ndim - 1)
        off += size
    return full


def _reduce_grads(G, W):
    p_in = _pieces_from_cols(_two_d(G["w_in"]), _column_segments(W["w_in"].shape[-1], True), W["w_in"].shape[-1],
                             "pack_w_in")
    p_ffn = _pieces_from_cols(_two_d(G["ffn_w_in"]), _column_segments(W["ffn_w_in"].shape[-1], False),
                              W["ffn_w_in"].shape[-1], "pack_ffn_w_in")
    small = jnp.concatenate([_shards(G[n], G[n].ndim - 1).reshape(N_DEV, -1) for n in SMALL_SHARDED], axis=1)
    repl = jnp.concatenate([G[n].reshape(-1) for n in REPLICATED])
    both = _to_rows(jnp.concatenate([small, jnp.broadcast_to(repl, (N_DEV, repl.size))], axis=1), 8)
    got = _exchange([Xfer(p_in, "slot"), Xfer(p_ffn, "slot"), Xfer(G["w_out"], "rows"), Xfer(G["ffn_w_out"], "rows"),
                     Xfer(both, "slot")], "exchange_grads")
    out = {}
    for n, r in zip(("w_in", "ffn_w_in", "w_out", "ffn_w_out"), got):
        out[n] = _sum_slots(r.reshape(N_DEV, -1, r.shape[-1]), "sum_" + n).reshape(W[n].shape)
    gsm = _sum_slots(got[4], "sum_small").reshape(-1)
    off = 0
    for n in SMALL_SHARDED + REPLICATED:
        out[n] = gsm[off:off + W[n].size].reshape(W[n].shape)
        off += W[n].size
    return out


def _update(W, G, M, V):
    delta, new_m, new_v = {}, {}, {}
    for n, _ in BIG:
        two_d = lambda a: a.reshape(-1, a.shape[-1])
        d, m, v = _adamw(two_d(W[n]), two_d(G[n]), two_d(M[n]), two_d(V[n]), "adamw_" + n)
        delta[n], new_m[n], new_v[n] = (t.reshape(W[n].shape) for t in (d, m, v))
    rest = SMALL_SHARDED + REPLICATED
    pack = lambda D: _to_rows(jnp.concatenate([D[n].reshape(-1) for n in rest]), 16)
    d, m, v = _adamw(pack(W), pack(G), pack(M), pack(V), "adamw_small")
    off = 0
    for n in rest:
        sl = lambda t: t.reshape(-1)[off:off + W[n].size].reshape(W[n].shape)
        delta[n], new_m[n], new_v[n] = sl(d), sl(m), sl(v)
        off += W[n].size
    return delta, new_m, new_v


def _step(x, target, Wf):
    L = x.shape[0]
    depth = Wf["w_in"].shape[0]
    cosf, sinf = _rope_tables(L)
    saved = []
    for l in range(depth):
        x, S = _layer_fwd(x, Wf, l, cosf, sinf)
        saved.append(S)
    loss, dx = _loss_fwd_bwd(x, target)
    grads = [None] * depth
    for l in reversed(range(depth)):
        dx, grads[l] = _layer_bwd(dx, Wf, l, saved[l], cosf, sinf)
    return loss, dx, {n: jnp.stack([g[n] for g in grads]) for n in WEIGHTS}


def kernel(x, mix_norm_pre, mix_norm_post, w_in, gla_w_gate, gla_b_gate, gla_norm, na_rpb, lru_conv_w, lru_conv_b, lru_w_a, lru_b_a, lru_w_x, lru_b_x, lru_lambda, w_out, ffn_norm_pre, ffn_norm_post, ffn_w_in, ffn_w_out, loss_target, m_mix_norm_pre, m_mix_norm_post, m_w_in, m_gla_w_gate, m_gla_b_gate, m_gla_norm, m_na_rpb, m_lru_conv_w, m_lru_conv_b, m_lru_w_a, m_lru_b_a, m_lru_w_x, m_lru_b_x, m_lru_lambda, m_w_out, m_ffn_norm_pre, m_ffn_norm_post, m_ffn_w_in, m_ffn_w_out, v_mix_norm_pre, v_mix_norm_post, v_w_in, v_gla_w_gate, v_gla_b_gate, v_gla_norm, v_na_rpb, v_lru_conv_w, v_lru_conv_b, v_lru_w_a, v_lru_b_a, v_lru_w_x, v_lru_b_x, v_lru_lambda, v_w_out, v_ffn_norm_pre, v_ffn_norm_post, v_ffn_w_in, v_ffn_w_out):
    W = dict(zip(WEIGHTS, (mix_norm_pre, mix_norm_post, w_in, gla_w_gate, gla_b_gate, gla_norm, na_rpb, lru_conv_w, lru_conv_b, lru_w_a, lru_b_a, lru_w_x, lru_b_x, lru_lambda, w_out, ffn_norm_pre, ffn_norm_post, ffn_w_in, ffn_w_out)))
    M = dict(zip(WEIGHTS, (m_mix_norm_pre, m_mix_norm_post, m_w_in, m_gla_w_gate, m_gla_b_gate, m_gla_norm, m_na_rpb, m_lru_conv_w, m_lru_conv_b, m_lru_w_a, m_lru_b_a, m_lru_w_x, m_lru_b_x, m_lru_lambda, m_w_out, m_ffn_norm_pre, m_ffn_norm_post, m_ffn_w_in, m_ffn_w_out)))
    V = dict(zip(WEIGHTS, (v_mix_norm_pre, v_mix_norm_post, v_w_in, v_gla_w_gate, v_gla_b_gate, v_gla_norm, v_na_rpb, v_lru_conv_w, v_lru_conv_b, v_lru_w_a, v_lru_b_a, v_lru_w_x, v_lru_b_x, v_lru_lambda, v_w_out, v_ffn_norm_pre, v_ffn_norm_post, v_ffn_w_in, v_ffn_w_out)))
    Wf = _gather_weights(W)
    loss, dx, Gfull = _step(x[0], loss_target[0], Wf)
    loss = lax.psum(loss, MESH_AXES)
    G = _reduce_grads(Gfull, W)
    delta, new_m, new_v = _update(W, G, M, V)
    return (loss, dx[None], *[G[n] for n in WEIGHTS], *[delta[n] for n in WEIGHTS], *[new_m[n] for n in WEIGHTS],
            *[new_v[n] for n in WEIGHTS])
```

```python
import functools
import math

import numpy as np
import jax
import jax.numpy as jnp
from jax import lax
from jax.experimental import pallas as pl
from jax.experimental.pallas import tpu as pltpu

F32 = jnp.float32
BF16 = jnp.bfloat16

N_DEV = 8
HEAD_DIM = 64
GROUP_W = 256
GLA_RANK = 16
GLA_TAU = 16.0
GLA_CHUNK = 64
GRID_W = 64
NA_ROWS = 8
NA_COLS = 16
LRU_C = 8.0
DIL_PAIRS = ((128, 1), (512, 4), (2048, 16))
DIL_RADIUS = 64
ROPE_THETA = 10000.0
EPS = 1e-6
ATT_SCALE = HEAD_DIM ** -0.5
NEG = -1e30
LANES = 128
P_COLS = 12 * GROUP_W + LANES
Z_BLOCK = 12 * GROUP_W // LANES

ADAM_LR = 0.001
ADAM_B1 = 0.9
ADAM_B2 = 0.999
ADAM_EPS = 1e-08
ADAM_WD = 0.01
ADAM_STEP = 10

VMEM_LIMIT = 56 * 1024 * 1024
_ARB = lambda n: pltpu.CompilerParams(dimension_semantics=("arbitrary",) * n, vmem_limit_bytes=VMEM_LIMIT)


def _tile(dim, pref, unit):
    t = min(pref, dim) // unit * unit
    while t >= unit:
        if dim % t == 0:
            return t
        t -= unit
    return dim


def _mm(a, b, mode, out_dtype, name, tm=512, tn=None, tk=None):
    if mode == "nn":
        (M, K), (_, N) = a.shape, b.shape
    elif mode == "nt":
        (M, K), (N, _) = a.shape, b.shape
    else:
        (K, M), (_, N) = a.shape, b.shape
    tm = _tile(M, tm, LANES if mode == "tn" else 8)
    tn = _tile(N, tn or N, LANES)
    tk = _tile(K, tk or K, LANES)
    nk = K // tk
    dims = {"nn": (((1,), (0,)), ((), ())), "nt": (((1,), (1,)), ((), ())), "tn": (((0,), (0,)), ((), ()))}[mode]

    def kern(a_ref, b_ref, o_ref, *acc):
        part = lax.dot_general(a_ref[...].astype(BF16), b_ref[...].astype(BF16), dims, preferred_element_type=F32)
        if nk == 1:
            o_ref[...] = part.astype(out_dtype)
            return
        k = pl.program_id(2)

        @pl.when(k == 0)
        def _():
            acc[0][...] = part

        @pl.when(jnp.logical_and(k > 0, k < nk - 1))
        def _():
            acc[0][...] += part

        @pl.when(k == nk - 1)
        def _():
            o_ref[...] = (acc[0][...] + part).astype(out_dtype)

    a_spec = pl.BlockSpec((tk, tm), lambda i, j, k: (k, i)) if mode == "tn" else pl.BlockSpec((tm, tk), lambda i, j, k: (i, k))
    b_spec = pl.BlockSpec((tn, tk), lambda i, j, k: (j, k)) if mode == "nt" else pl.BlockSpec((tk, tn), lambda i, j, k: (k, j))
    return pl.pallas_call(
        kern, name=name, grid=(M // tm, N // tn, nk),
        in_specs=[a_spec, b_spec], out_specs=pl.BlockSpec((tm, tn), lambda i, j, k: (i, j)),
        out_shape=jax.ShapeDtypeStruct((M, N), out_dtype),
        scratch_shapes=[pltpu.VMEM((tm, tn), F32)] if nk > 1 else [],
        compiler_params=_ARB(3),
    )(a, b)


class Row:
    def __init__(self, a, width=None, cb=0, halo=False):
        self.a, self.width, self.cb, self.halo = a, width, cb, halo


class Full:
    def __init__(self, a):
        self.a = a


HALO = 8


def _rows(name, body, tm, ins, outs):
    L = next(s.a.shape[0] for s in ins if isinstance(s, Row))
    tm = _tile(L, tm, 16)
    nt = L // tm
    nb8 = L // HALO
    step = tm // HALO
    in_specs, arrays, layout = [], [], []
    for s in ins:
        if isinstance(s, Full):
            nd = s.a.ndim
            in_specs.append(pl.BlockSpec(s.a.shape, lambda i, _nd=nd: (0,) * _nd))
            arrays.append(s.a)
            layout.append(1)
        else:
            w = s.width or s.a.shape[1]
            in_specs.append(pl.BlockSpec((tm, w), lambda i, _cb=s.cb: (i, _cb)))
            arrays.append(s.a)
            if s.halo:
                in_specs.append(pl.BlockSpec((HALO, w), lambda i, _cb=s.cb: (jnp.maximum(i * step - 1, 0), _cb)))
                in_specs.append(pl.BlockSpec((HALO, w), lambda i, _cb=s.cb: (jnp.minimum((i + 1) * step, nb8 - 1), _cb)))
                arrays += [s.a, s.a]
                layout.append(3)
            else:
                layout.append(1)
    out_specs, out_shapes = [], []
    for kind, shp, dt in outs:
        if kind == "row":
            out_specs.append(pl.BlockSpec((tm, shp), lambda i: (i, 0)))
            out_shapes.append(jax.ShapeDtypeStruct((L, shp), dt))
        else:
            out_specs.append(pl.BlockSpec(shp, lambda i, _n=len(shp): (0,) * _n))
            out_shapes.append(jax.ShapeDtypeStruct(shp, dt))
    n_in = len(arrays)

    def kern(*refs):
        i = pl.program_id(0)
        vals, p = [], 0
        for n in layout:
            if n == 1:
                vals.append(refs[p][...])
            else:
                vals.append((refs[p + 1][...], refs[p][...], refs[p + 2][...]))
            p += n
        res = body(i, nt, *vals)
        if not isinstance(res, (tuple, list)):
            res = (res,)
        for (kind, shp, dt), o_ref, r in zip(outs, refs[n_in:], res):
            if kind == "row":
                o_ref[...] = r.astype(dt)
            else:
                @pl.when(i == 0)
                def _(o_ref=o_ref):
                    o_ref[...] = jnp.zeros_like(o_ref)
                o_ref[...] += r.astype(dt)

    res = pl.pallas_call(
        kern, name=name, grid=(nt,), in_specs=in_specs, out_specs=out_specs, out_shape=out_shapes,
        compiler_params=_ARB(1),
    )(*arrays)
    return res


def _shift(h, o, i, nt):
    prev, cur, nxt = h
    if o == 0:
        return cur
    tm = cur.shape[0]
    cat = jnp.concatenate([prev, cur, nxt], axis=0)
    sh = pltpu.roll(cat, (-o) % (tm + 2 * HALO), axis=0)[HALO:HALO + tm]
    row = lax.broadcasted_iota(jnp.int32, cur.shape, 0)
    if o < 0:
        ok = jnp.logical_or(i > 0, row >= -o)
    else:
        ok = jnp.logical_or(i < nt - 1, row < tm - o)
    return jnp.where(ok, sh, 0.0)


def _colsum(v):
    return jnp.sum(v, axis=0, keepdims=True)


def _sigmoid(x):
    return 1.0 / (1.0 + jnp.exp(-x))


def _softplus(x):
    return jnp.maximum(x, 0.0) + jnp.log1p(jnp.exp(-jnp.abs(x)))


def _silu(x):
    return x * _sigmoid(x)


def _dsilu(x):
    s = _sigmoid(x)
    return s * (1.0 + x * (1.0 - s))


_GELU_C = math.sqrt(2.0 / math.pi)


def _gelu(x):
    return 0.5 * x * (1.0 + jnp.tanh(_GELU_C * (x + 0.044715 * x * x * x)))


def _dgelu(x):
    t = jnp.tanh(_GELU_C * (x + 0.044715 * x * x * x))
    return 0.5 * (1.0 + t) + 0.5 * x * (1.0 - t * t) * _GELU_C * (1.0 + 3.0 * 0.044715 * x * x)


def _head_sum(v, bd):
    return jnp.dot(v, bd, precision=lax.Precision.HIGHEST, preferred_element_type=F32)


def _block_ones(n, blk):
    r = np.arange(n)
    return jnp.asarray((r[:, None] // blk == r[None, :] // blk).astype(np.float32))


def _rms_fwd(x, g, name):
    def body(i, nt, x, g):
        r = lax.rsqrt(jnp.mean(x * x, axis=-1, keepdims=True) + EPS)
        return x * r * g
    return _rows(name, body, 256, [Row(x), Full(g)], [("row", x.shape[1], BF16)])[0]


def _rms_resid_fwd(x, y, g, name):
    def body(i, nt, x, y, g):
        r = lax.rsqrt(jnp.mean(y * y, axis=-1, keepdims=True) + EPS)
        return x + y * r * g
    return _rows(name, body, 256, [Row(x), Row(y), Full(g)], [("row", x.shape[1], F32)])[0]


def _rms_bwd(dy, x, g, name, resid=None, out_dtype=F32):
    D = x.shape[1]

    def body(i, nt, dy, x, g, *rest):
        dy = dy.astype(F32)
        r = lax.rsqrt(jnp.mean(x * x, axis=-1, keepdims=True) + EPS)
        xh = x * r
        dxh = dy * g
        dx = r * (dxh - xh * jnp.mean(dxh * xh, axis=-1, keepdims=True))
        if rest:
            dx = dx + rest[0]
        return dx, _colsum(dy * xh)

    ins = [Row(dy), Row(x), Full(g)] + ([Row(resid)] if resid is not None else [])
    return _rows(name, body, 256, ins, [("row", D, out_dtype), ("acc", (1, D), F32)])


def _ffn_in_swiglu(h, w):
    (M, K), N = h.shape, w.shape[1]
    F = N // 2
    tm = _tile(M, 256, 16)

    def kern(a_ref, b_ref, gu_ref, act_ref):
        gu = _dot(a_ref[...], b_ref[...])
        gu_ref[...] = gu
        act_ref[...] = (_silu(gu[:, :F]) * gu[:, F:]).astype(BF16)

    return pl.pallas_call(
        kern, name="ffn_in_swiglu", grid=(M // tm,),
        in_specs=[pl.BlockSpec((tm, K), lambda i: (i, 0)), pl.BlockSpec((K, N), lambda i: (0, 0))],
        out_specs=[pl.BlockSpec((tm, N), lambda i: (i, 0)), pl.BlockSpec((tm, F), lambda i: (i, 0))],
        out_shape=[jax.ShapeDtypeStruct((M, N), F32), jax.ShapeDtypeStruct((M, F), BF16)], compiler_params=_ARB(1),
    )(h, w)


def _ffn_out_dx_swiglu(df, w, gu):
    (M, K), N = df.shape, gu.shape[1]
    F = N // 2
    tm = _tile(M, 256, 16)

    def kern(a_ref, b_ref, gu_ref, o_ref):
        da = _dot(a_ref[...], b_ref[...], _NT)
        gu = gu_ref[...]
        gate, up = gu[:, :F], gu[:, F:]
        o_ref[:, :F] = (da * up * _dsilu(gate)).astype(BF16)
        o_ref[:, F:] = (da * _silu(gate)).astype(BF16)

    return pl.pallas_call(
        kern, name="ffn_out_dx_swiglu", grid=(M // tm,),
        in_specs=[pl.BlockSpec((tm, K), lambda i: (i, 0)), pl.BlockSpec((F, K), lambda i: (0, 0)),
                  pl.BlockSpec((tm, N), lambda i: (i, 0))],
        out_specs=pl.BlockSpec((tm, N), lambda i: (i, 0)), out_shape=jax.ShapeDtypeStruct((M, N), BF16),
        compiler_params=_ARB(1),
    )(df, w, gu)


def _loss_fwd_bwd(y, target):
    D = y.shape[1]

    def body(i, nt, y, t):
        err = y - t
        part = 0.5 * jnp.sum(jnp.mean(err * err, axis=-1, keepdims=True), axis=0, keepdims=True)
        return err * (1.0 / D), jnp.broadcast_to(part, (1, LANES))
    dy, loss = _rows("loss", body, 256, [Row(y), Row(target)], [("row", D, F32), ("acc", (1, LANES), F32)])
    return loss[0, 0], dy


def _adamw(w, g, m, v, name):
    C = w.shape[1]
    bc1 = 1.0 - ADAM_B1 ** ADAM_STEP
    bc2 = 1.0 - ADAM_B2 ** ADAM_STEP

    def body(i, nt, w, g, m, v):
        m = ADAM_B1 * m + (1.0 - ADAM_B1) * g
        v = ADAM_B2 * v + (1.0 - ADAM_B2) * (g * g)
        delta = -ADAM_LR * ((m / bc1) / (jnp.sqrt(v / bc2) + ADAM_EPS) + ADAM_WD * w)
        return delta, m, v
    return _rows(name, body, 256, [Row(w), Row(g), Row(m), Row(v)], [("row", C, F32)] * 3)


def _expm1(x):
    return jnp.tanh(0.5 * x) * (jnp.exp(x) + 1.0)


def _lru_gates(xh, i, nt, cw, cb, wa, wx, ba, bx, lam):
    xc = cb
    for j in range(4):
        xc = xc + cw[j:j + 1] * _shift(xh, j - 2, i, nt)
    xcb = xc.astype(BF16)
    gates = []
    for e in range(2):
        r = _sigmoid(jnp.dot(xcb, wa[e], preferred_element_type=F32) + ba[e:e + 1])
        ig = _sigmoid(jnp.dot(xcb, wx[e], preferred_element_type=F32) + bx[e:e + 1])
        sp = _softplus(-lam[e:e + 1])
        la = -LRU_C * r * sp
        gates.append((r, ig, sp, jnp.exp(la), jnp.sqrt(-_expm1(2.0 * la))))
    return xc, xcb, gates


def _scan2(af, uf, ab, ub, adjoint, name):
    L, W = af.shape
    tm = _tile(L, 512, 8)
    nt, nb = L // tm, tm // 8

    def blk(A, U, h, reverse, row):
        for d in (1, 2, 4):
            if reverse:
                ok, sh = row < 8 - d, 8 - d
            else:
                ok, sh = row >= d, d
            As = jnp.where(ok, pltpu.roll(A, sh, axis=0), 1.0)
            Us = jnp.where(ok, pltpu.roll(U, sh, axis=0), 0.0)
            U = A * Us + U
            A = A * As
        return A * h + U

    def kern(af_ref, uf_ref, ab_ref, ub_ref, of_ref, ob_ref, c_ref):
        @pl.when(pl.program_id(0) == 0)
        def _():
            c_ref[...] = jnp.zeros_like(c_ref)

        row = lax.broadcasted_iota(jnp.int32, (8, W), 0)
        full = lambda v: jnp.broadcast_to(v, (8, W))

        def body(j, carry):
            hF, aF, hB, aB = carry
            r0 = pl.multiple_of(j * 8, 8)
            r1 = pl.multiple_of((nb - 1 - j) * 8, 8)
            A, U = af_ref[pl.ds(r0, 8), :], uf_ref[pl.ds(r0, 8), :]
            if adjoint:
                C = jnp.where(row == 0, aF, pltpu.roll(A, 1, axis=0))
                aF = full(A[7:8])
            else:
                C = A
            H = blk(C, U, hF, False, row)
            of_ref[pl.ds(r0, 8), :] = H
            hF = full(H[7:8])
            A, U = ab_ref[pl.ds(r1, 8), :], ub_ref[pl.ds(r1, 8), :]
            if adjoint:
                C = jnp.where(row == 7, aB, pltpu.roll(A, 7, axis=0))
                aB = full(A[0:1])
            else:
                C = A
            H = blk(C, U, hB, True, row)
            ob_ref[pl.ds(r1, 8), :] = H
            hB = full(H[0:1])
            return hF, aF, hB, aB

        carry = lax.fori_loop(0, nb, body, (c_ref[0], c_ref[1], c_ref[2], c_ref[3]))
        for n in range(4):
            c_ref[n] = carry[n]

    fwd = pl.BlockSpec((tm, W), lambda i: (i, 0))
    bwd = pl.BlockSpec((tm, W), lambda i: (nt - 1 - i, 0))
    return pl.pallas_call(
        kern, name=name, grid=(nt,), in_specs=[fwd, fwd, bwd, bwd], out_specs=[fwd, bwd],
        out_shape=[jax.ShapeDtypeStruct((L, W), F32)] * 2,
        scratch_shapes=[pltpu.VMEM((4, 8, W), F32)], compiler_params=_ARB(1),
    )(af, uf, ab, ub)


def _block_diag(w):
    out = jnp.zeros((2, GROUP_W, GROUP_W), w.dtype)
    for h in range(4):
        out = out.at[:, h * 64:(h + 1) * 64, h * 64:(h + 1) * 64].set(w[:, h])
    return out.astype(BF16)


def _diag_blocks(w):
    return jnp.stack([w[:, h * 64:(h + 1) * 64, h * 64:(h + 1) * 64] for h in range(4)], axis=1)


def _lru_params(W, l):
    return [Full(W["lru_conv_w"][l]), Full(W["lru_conv_b"][l][None]), Full(_block_diag(W["lru_w_a"][l])),
            Full(_block_diag(W["lru_w_x"][l])), Full(W["lru_b_a"][l]), Full(W["lru_b_x"][l]), Full(W["lru_lambda"][l])]


def _lru_fwd(p, W, l):
    def pre(i, nt, xh, *prm):
        xc, _, g = _lru_gates(xh, i, nt, *prm)
        return g[0][3], g[0][4] * (g[0][1] * xc), g[1][3], g[1][4] * (g[1][1] * xc)

    a0, u0, a1, u1 = _rows("lru_pre", pre, 256, [Row(p, GROUP_W, 7, halo=True)] + _lru_params(W, l),
                           [("row", GROUP_W, F32)] * 4)
    hf, hb = _scan2(a0, u0, a1, u1, False, "lru_scan")
    yc = _rows("lru_post", lambda i, nt, hf, hb, gc: (hf + hb) * _gelu(gc), 512,
               [Row(hf), Row(hb), Row(p, GROUP_W, 8)], [("row", GROUP_W, BF16)])[0]
    return yc, (a0, a1, hf, hb)


def _lru_bwd(dy, dy_cb, p, W, l, saved):
    a0, a1, hf, hb = saved

    def post(i, nt, dy, hf, hb, gc):
        return dy * _gelu(gc), dy * (hf + hb) * _dgelu(gc)

    dh, dgc = _rows("lru_post_bwd", post, 512, [Row(dy, GROUP_W, dy_cb), Row(hf), Row(hb), Row(p, GROUP_W, 8)],
                    [("row", GROUP_W, F32), ("row", GROUP_W, BF16)])
    gb, gf = _scan2(a1, dh, a0, dh, True, "lru_scan_adj")

    def gates_bwd(i, nt, xh, gf, gb, hfh, hbh, cw, cb, wa, wx, ba, bx, lam):
        xc, xcb, g = _lru_gates(xh, i, nt, cw, cb, wa, wx, ba, bx, lam)
        dxc = jnp.zeros_like(xc)
        dwa, dwx, dba, dbx, dlam = [], [], [], [], []
        for e, du, hprev in ((0, gf, _shift(hfh, -1, i, nt)), (1, gb, _shift(hbh, 1, i, nt))):
            r, ig, sp, a, s = g[e]
            dxc = dxc + du * s * ig
            dla = du * hprev * a - (du * ig * xc) * a * a / s
            dza = (dla * (-LRU_C) * sp) * r * (1.0 - r)
            dzx = (du * s * xc) * ig * (1.0 - ig)
            dlam.append(_colsum(dla * r) * (LRU_C * _sigmoid(-lam[e:e + 1])))
            dba.append(_colsum(dza))
            dbx.append(_colsum(dzx))
            dzab, dzxb = dza.astype(BF16), dzx.astype(BF16)
            tn = (((0,), (0,)), ((), ()))
            nt_ = (((1,), (1,)), ((), ()))
            dwa.append(lax.dot_general(xcb, dzab, tn, preferred_element_type=F32))
            dwx.append(lax.dot_general(xcb, dzxb, tn, preferred_element_type=F32))
            dxc = dxc + lax.dot_general(dzab, wa[e], nt_, preferred_element_type=F32)
            dxc = dxc + lax.dot_general(dzxb, wx[e], nt_, preferred_element_type=F32)
        cat = lambda v: jnp.concatenate(v, axis=0)
        return dxc, jnp.stack(dwa), jnp.stack(dwx), cat(dba), cat(dbx), cat(dlam)

    dxc, dwa, dwx, dba, dbx, dlam = _rows(
        "lru_gates_bwd", gates_bwd, 256,
        [Row(p, GROUP_W, 7, halo=True), Row(gf), Row(gb), Row(hf, halo=True), Row(hb, halo=True)] + _lru_params(W, l),
        [("row", GROUP_W, F32), ("acc", (2, GROUP_W, GROUP_W), F32), ("acc", (2, GROUP_W, GROUP_W), F32),
         ("acc", (2, GROUP_W), F32), ("acc", (2, GROUP_W), F32), ("acc", (2, GROUP_W), F32)])

    def conv_bwd(i, nt, dh_, xh, cw):
        dxb = jnp.zeros_like(dh_[1])
        dcw = []
        for j in range(4):
            dxb = dxb + cw[j:j + 1] * _shift(dh_, 2 - j, i, nt)
            dcw.append(_colsum(dh_[1] * _shift(xh, j - 2, i, nt)))
        return dxb, jnp.concatenate(dcw, axis=0), _colsum(dh_[1])

    dxb, dcw, dcb = _rows("lru_conv_bwd", conv_bwd, 512,
                          [Row(dxc, halo=True), Row(p, GROUP_W, 7, halo=True), Full(W["lru_conv_w"][l])],
                          [("row", GROUP_W, BF16), ("acc", (4, GROUP_W), F32), ("acc", (1, GROUP_W), F32)])
    grads = dict(lru_conv_w=dcw, lru_conv_b=dcb[0], lru_w_a=_diag_blocks(dwa), lru_w_x=_diag_blocks(dwx),
                 lru_b_a=dba, lru_b_x=dbx, lru_lambda=dlam)
    return dxb, dgc, grads


_NT = (((1,), (1,)), ((), ()))
_TN = (((0,), (0,)), ((), ()))


def _dot(a, b, dims=None):
    if dims is None:
        return jnp.dot(a, b, preferred_element_type=F32)
    return lax.dot_general(a, b, dims, preferred_element_type=F32)


def _dot_exact(a, b):
    return jnp.dot(a, b, precision=lax.Precision.HIGHEST, preferred_element_type=F32)


def _gla_gate_w(w_gate, b_gate):
    wg = jnp.zeros((LANES, 2 * GROUP_W), F32)
    for e in range(2):
        wg = wg.at[e * GLA_RANK:(e + 1) * GLA_RANK, e * GROUP_W:(e + 1) * GROUP_W].set(w_gate[e])
    return wg.astype(BF16), b_gate.reshape(1, 2 * GROUP_W)


def _gla_gates_fwd(p, wg, bg):
    def body(i, nt, z, wg, bg):
        logit = _dot(z.astype(BF16), wg) + bg
        la = -_softplus(-logit) * (1.0 / GLA_TAU)
        return la[:, :GROUP_W], la[:, GROUP_W:]
    return _rows("gla_gates", body, 512, [Row(p, LANES, Z_BLOCK), Full(wg), Full(bg)], [("row", GROUP_W, F32)] * 2)


def _gla_gates_bwd(p, dla0, dla1, wg, bg):
    def body(i, nt, z, d0, d1, wg, bg):
        zb = z.astype(BF16)
        logit = _dot(zb, wg) + bg
        dlogit = jnp.concatenate([d0, d1], axis=1) * (1.0 / GLA_TAU) * _sigmoid(-logit)
        dlb = dlogit.astype(BF16)
        return _dot(dlb, wg, _NT), _dot(zb, dlb, _TN), _colsum(dlogit)
    return _rows("gla_gates_bwd", body, 512, [Row(p, LANES, Z_BLOCK), Row(dla0), Row(dla1), Full(wg), Full(bg)],
                 [("row", LANES, BF16), ("acc", (LANES, 2 * GROUP_W), F32), ("acc", (1, 2 * GROUP_W), F32)])


def _gla_order(reverse):
    t = np.arange(GLA_CHUNK)
    m = (t[None, :] >= t[:, None]) if reverse else (t[None, :] <= t[:, None])
    return m.astype(np.float32), (32, 0) if reverse else (31, 63)


def _stack_heads(x, bd):
    return jnp.where(bd, jnp.concatenate([x] * 4, axis=0), 0.0)


def _diag_heads(r, bd):
    r = jnp.where(bd, r, 0.0)
    return r[0:64] + r[64:128] + r[128:192] + r[192:256]


def _gla_chunk_terms(q_ref, k_ref, la_ref, rows, mv, mid, last):
    b = _dot_exact(mv, la_ref[rows, :])
    bm, bl = b[mid:mid + 1], b[last:last + 1]
    qs = q_ref[rows, :] * ATT_SCALE
    k = k_ref[rows, :]
    P, N, E, Fd = jnp.exp(b - bm), jnp.exp(bm - b), jnp.exp(b), jnp.exp(bl - b)
    return (P, N, E, Fd, jnp.exp(bl)), (qs * P, k * N, qs * E, k * Fd)


def _gla_specs(L, walk_up):
    tm = _tile(L, 512, GLA_CHUNK)
    nt, nc = L // tm, tm // GLA_CHUNK
    specs = []
    for up in walk_up:
        t = (lambda i: i) if up else (lambda i: nt - 1 - i)
        specs.append(dict(
            col=lambda cb, _t=t: pl.BlockSpec((tm, GROUP_W), lambda i: (_t(i), cb)),
            row=pl.BlockSpec((tm, GROUP_W), lambda i, _t=t: (_t(i), 0)),
            state=pl.BlockSpec((nc, GROUP_W, GROUP_W), lambda i, _t=t: (_t(i), 0, 0))))
    return nt, nc, specs


def _gla_chunk_fwd(p, la0, la1):
    L = la0.shape[0]
    nt, nc, specs = _gla_specs(L, (True, False))
    orders = [_gla_order(False), _gla_order(True)]

    def kern(q0, k0, v0, l0, q1, k1, v1, l1, m0_ref, m1_ref, bd_ref, o0, s0, o1, s1, st_ref):
        @pl.when(pl.program_id(0) == 0)
        def _():
            st_ref[...] = jnp.zeros_like(st_ref)

        bd = bd_ref[...] > 0.5
        dirs = []
        for e, (q_ref, k_ref, v_ref, la_ref, m_ref, o_ref, s_ref) in enumerate(
                ((q0, k0, v0, l0, m0_ref, o0, s0), (q1, k1, v1, l1, m1_ref, o1, s1))):
            mv = m_ref[...]
            dirs.append((q_ref, k_ref, v_ref, la_ref, mv, jnp.concatenate([mv] * 4, axis=0) > 0.5, o_ref, s_ref))

        def body(cc, carry):
            for e, (q_ref, k_ref, v_ref, la_ref, mv, keep, o_ref, s_ref) in enumerate(dirs):
                mid, last = orders[e][1]
                c = nc - 1 - cc if e else cc
                rows = pl.ds(pl.multiple_of(c * GLA_CHUNK, GLA_CHUNK), GLA_CHUNK)
                (_, _, _, _, d), (qP, kN, qE, kF) = _gla_chunk_terms(q_ref, k_ref, la_ref, rows, mv, mid, last)
                vb = v_ref[rows, :].astype(BF16)
                st = st_ref[e]
                s_ref[c] = st
                a = jnp.where(keep, _dot(_stack_heads(qP, bd).astype(BF16), kN.astype(BF16), _NT), 0.0)
                o = _diag_heads(_dot(a.astype(BF16), vb), bd) + _dot(qE.astype(BF16), st.astype(BF16), _NT)
                o_ref[rows, :] = o
                st_ref[e] = st * d + jnp.where(bd, _dot(vb, kF.astype(BF16), _TN), 0.0)
            return carry

        lax.fori_loop(0, nc, body, 0)

    const = lambda shp: pl.BlockSpec(shp, lambda i: (0, 0))
    in_specs, out_specs = [], []
    for sp in specs:
        in_specs += [sp["col"](0), sp["col"](1), sp["col"](2), sp["row"]]
        out_specs += [sp["row"], sp["state"]]
    return pl.pallas_call(
        kern, name="gla_fwd", grid=(nt,),
        in_specs=in_specs + [const((GLA_CHUNK, GLA_CHUNK))] * 2 + [const((GROUP_W, GROUP_W))], out_specs=out_specs,
        out_shape=[jax.ShapeDtypeStruct((L, GROUP_W), F32),
                   jax.ShapeDtypeStruct((L // GLA_CHUNK, GROUP_W, GROUP_W), F32)] * 2,
        scratch_shapes=[pltpu.VMEM((2, GROUP_W, GROUP_W), F32)], compiler_params=_ARB(1),
    )(p, p, p, la0, p, p, p, la1, jnp.asarray(orders[0][0]), jnp.asarray(orders[1][0]),
      _block_ones(GROUP_W, HEAD_DIM))


def _gla_chunk_bwd(p, la0, la1, do, sprev0, sprev1):
    L = la0.shape[0]
    nt, nc, specs = _gla_specs(L, (False, True))
    orders = [_gla_order(False), _gla_order(True)]

    def kern(q0, k0, v0, l0, do0, s0, q1, k1, v1, l1, do1, s1, m0_ref, m1_ref, t0_ref, t1_ref, bd_ref, *rest):
        outs, dst_ref = (rest[0:4], rest[4:8]), rest[8]

        @pl.when(pl.program_id(0) == 0)
        def _():
            dst_ref[...] = jnp.zeros_like(dst_ref)

        bd = bd_ref[...] > 0.5
        row = lax.broadcasted_iota(jnp.int32, (GLA_CHUNK, GROUP_W), 0)
        dirs = []
        for ins, m_ref, t_ref in (((q0, k0, v0, l0, do0, s0), m0_ref, t0_ref), ((q1, k1, v1, l1, do1, s1), m1_ref, t1_ref)):
            mv = m_ref[...]
            dirs.append(ins + (mv, t_ref[...], jnp.concatenate([mv] * 4, axis=0) > 0.5))

        def body(cc, carry):
            for e, (q_ref, k_ref, v_ref, la_ref, do_ref, s_ref, mv, mt, keep) in enumerate(dirs):
                dq_ref, dk_ref, dv_ref, dla_ref = outs[e]
                mid, last = orders[e][1]
                c = cc if e else nc - 1 - cc
                rows = pl.ds(pl.multiple_of(c * GLA_CHUNK, GLA_CHUNK), GLA_CHUNK)
                (P, N, E, Fd, d), (qP, kN, qE, kF) = _gla_chunk_terms(q_ref, k_ref, la_ref, rows, mv, mid, last)
                vb = v_ref[rows, :].astype(BF16)
                dov = do_ref[rows, :]
                dob = dov.astype(BF16)
                st, dst = s_ref[c], dst_ref[e]
                stb, dstb = st.astype(BF16), dst.astype(BF16)
                qst = _stack_heads(qP, bd).astype(BF16)
                dost = _stack_heads(dov, bd).astype(BF16)
                kNb, kFb, qEb = kN.astype(BF16), kF.astype(BF16), qE.astype(BF16)
                a = jnp.where(keep, _dot(qst, kNb, _NT), 0.0).astype(BF16)
                da = jnp.where(keep, _dot(dost, vb, _NT), 0.0).astype(BF16)
                dv = _dot(a, dost, _TN) + _dot(kFb, dstb, _NT)
                dqP = _diag_heads(_dot(da, kNb), bd)
                dkN = _dot(da, qst, _TN)
                dqE = _dot(dob, stb)
                dkF = _dot(vb, dstb)
                dd = _colsum(dst * st)
                dst_ref[e] = jnp.where(bd, _dot(dob, qEb, _TN), 0.0) + dst * d
                tP, tN, tE, tF = dqP * qP, dkN * kN, dqE * qE, dkF * kF
                db = tP - tN + tE - tF
                db = db + jnp.where(row == mid, _colsum(tN - tP), 0.0) + jnp.where(row == last, _colsum(tF) + dd * d, 0.0)
                dla_ref[rows, :] = _dot_exact(mt, db)
                dq_ref[rows, :] = (dqP * P + dqE * E) * ATT_SCALE
                dk_ref[rows, :] = dkN * N + dkF * Fd
                dv_ref[rows, :] = dv
            return carry

        lax.fori_loop(0, nc, body, 0)

    const = lambda shp: pl.BlockSpec(shp, lambda i: (0, 0))
    in_specs, out_specs = [], []
    for sp in specs:
        in_specs += [sp["col"](0), sp["col"](1), sp["col"](2), sp["row"], sp["row"], sp["state"]]
        out_specs += [sp["row"]] * 4
    m0, m1 = orders[0][0], orders[1][0]
    return pl.pallas_call(
        kern, name="gla_bwd", grid=(nt,),
        in_specs=in_specs + [const((GLA_CHUNK, GLA_CHUNK))] * 4 + [const((GROUP_W, GROUP_W))], out_specs=out_specs,
        out_shape=[jax.ShapeDtypeStruct((L, GROUP_W), F32)] * 8,
        scratch_shapes=[pltpu.VMEM((2, GROUP_W, GROUP_W), F32)], compiler_params=_ARB(1),
    )(p, p, p, la0, do, sprev0, p, p, p, la1, do, sprev1, jnp.asarray(m0), jnp.asarray(m1),
      jnp.asarray(m0.T.copy()), jnp.asarray(m1.T.copy()), _block_ones(GROUP_W, HEAD_DIM))


def _gla_fwd(p, W, l):
    wg, bg = _gla_gate_w(W["gla_w_gate"][l], W["gla_b_gate"][l])
    la0, la1 = _gla_gates_fwd(p, wg, bg)
    of, s0, ob, s1 = _gla_chunk_fwd(p, la0, la1)

    def post(i, nt, of, ob, g, ng, bd):
        o = of + ob
        r = lax.rsqrt(_head_sum(o * o, bd) * (1.0 / HEAD_DIM) + EPS)
        return o * r * ng * _silu(g)

    ya = _rows("gla_post", post, 512, [Row(of), Row(ob), Row(p, GROUP_W, 3), Full(W["gla_norm"][l][None]),
                                       Full(_block_ones(GROUP_W, HEAD_DIM))], [("row", GROUP_W, BF16)])[0]
    return ya, (la0, la1, of, ob, s0, s1)


def _gla_bwd(dy, dy_cb, p, W, l, saved):
    la0, la1, of, ob, s0, s1 = saved
    wg, bg = _gla_gate_w(W["gla_w_gate"][l], W["gla_b_gate"][l])

    def post(i, nt, dy, of, ob, g, ng, bd):
        o = of + ob
        r = lax.rsqrt(_head_sum(o * o, bd) * (1.0 / HEAD_DIM) + EPS)
        oh = o * r
        don = dy * _silu(g)
        doh = don * ng
        do = r * (doh - oh * _head_sum(doh * oh, bd) * (1.0 / HEAD_DIM))
        return do, dy * (oh * ng) * _dsilu(g), _colsum(don * oh)

    do, dg, dng = _rows("gla_post_bwd", post, 512,
                        [Row(dy, GROUP_W, dy_cb), Row(of), Row(ob), Row(p, GROUP_W, 3), Full(W["gla_norm"][l][None]),
                         Full(_block_ones(GROUP_W, HEAD_DIM))],
                        [("row", GROUP_W, F32), ("row", GROUP_W, BF16), ("acc", (1, GROUP_W), F32)])
    dq0, dk0, dv0, dla0, dq1, dk1, dv1, dla1 = _gla_chunk_bwd(p, la0, la1, do, s0, s1)
    dq, dk, dv = _rows("gla_sum_bwd", lambda i, nt, a0, a1, b0, b1, c0, c1: (a0 + a1, b0 + b1, c0 + c1), 512,
                       [Row(t) for t in (dq0, dq1, dk0, dk1, dv0, dv1)], [("row", GROUP_W, BF16)] * 3)
    dz, dwg, dbg = _gla_gates_bwd(p, dla0, dla1, wg, bg)
    dw_gate = jnp.stack([dwg[e * GLA_RANK:(e + 1) * GLA_RANK, e * GROUP_W:(e + 1) * GROUP_W] for e in range(2)])
    grads = dict(gla_w_gate=dw_gate, gla_b_gate=dbg.reshape(2, GROUP_W), gla_norm=dng[0])
    return (dq, dk, dv, dg, dz), grads


def _rope_tables(L):
    pos = jnp.arange(L, dtype=F32)
    inv_freq = ROPE_THETA ** (-jnp.arange(0, HEAD_DIM, 2, dtype=F32) / HEAD_DIM)
    ang = pos[:, None] * inv_freq[None, :]
    cos, sin = jnp.cos(ang), jnp.sin(ang)
    return jnp.tile(jnp.concatenate([cos, cos], axis=1), (1, 4)), jnp.tile(jnp.concatenate([-sin, sin], axis=1), (1, 4))


def _swap_halves(t):
    lane = lax.broadcasted_iota(jnp.int32, t.shape, 1)
    first = (lane & (HEAD_DIM - 1)) < HEAD_DIM // 2
    return jnp.where(first, pltpu.roll(t, GROUP_W - HEAD_DIM // 2, axis=1), pltpu.roll(t, HEAD_DIM // 2, axis=1))


def _attn_prep(p, cosf, sinf):
    def body(i, nt, qb, kb, vb, qd, kd, vd, c, s):
        return qb, kb, vb, qd * c + _swap_halves(qd) * s, kd * c + _swap_halves(kd) * s, vd
    ins = [Row(p, GROUP_W, cb) for cb in (4, 5, 6, 9, 10, 11)] + [Row(cosf), Row(sinf)]
    return _rows("attn_prep", body, 512, ins, [("row", GROUP_W, BF16)] * 6)


def _na_onehot():
    c = np.arange(GRID_W)
    dc = np.clip(c[None, :] - c[:, None], -(NA_COLS - 1), NA_COLS - 1) + NA_COLS - 1
    oh = np.zeros((LANES, GRID_W * GRID_W), np.float32)
    oh[dc.reshape(-1), np.arange(GRID_W * GRID_W)] = 1.0
    return jnp.asarray(oh)


def _na_colmask():
    c = np.arange(GRID_W)
    start = np.clip(c - NA_COLS // 2, 0, GRID_W - NA_COLS)
    ok = (c[None, :] >= start[:, None]) & (c[None, :] < start[:, None] + NA_COLS)
    return jnp.asarray(np.where(ok, 0.0, NEG).astype(np.float32))


N_DR = 2 * NA_ROWS - 1


def _na_bias(rpb):
    rp = jnp.zeros((GRID_W, LANES), F32).at[:4 * N_DR, :2 * NA_COLS - 1].set(rpb.reshape(4 * N_DR, 2 * NA_COLS - 1))

    def expand(r_ref, oh_ref, o_ref):
        o_ref[...] = _dot_exact(r_ref[...], oh_ref[...])

    r = pl.pallas_call(expand, name="na_bias_expand",
                       out_shape=jax.ShapeDtypeStruct((GRID_W, GRID_W * GRID_W), F32))(rp, _na_onehot())
    r = r[:4 * N_DR].reshape(4, N_DR, GRID_W, GRID_W)

    def build(r_ref, m_ref, o_ref):
        for h in range(4):
            for c in range(NA_ROWS):
                for i in range(NA_ROWS):
                    o_ref[h, c, :, i * GRID_W:(i + 1) * GRID_W] = r_ref[h, i - c + NA_ROWS - 1] + m_ref[...]

    return pl.pallas_call(build, name="na_bias_build",
                          out_shape=jax.ShapeDtypeStruct((4, NA_ROWS, GRID_W, NA_ROWS * GRID_W), F32))(r, _na_colmask())


def _na_bias_bwd(dbias):
    def fold(d_ref, o_ref):
        for h in range(4):
            for a in range(N_DR):
                acc = jnp.zeros((GRID_W, GRID_W), F32)
                for c in range(NA_ROWS):
                    i = a + c - (NA_ROWS - 1)
                    if 0 <= i < NA_ROWS:
                        acc = acc + d_ref[h, c, :, i * GRID_W:(i + 1) * GRID_W]
                o_ref[h, a] = acc

    dr = pl.pallas_call(fold, name="na_bias_fold",
                        out_shape=jax.ShapeDtypeStruct((4, N_DR, GRID_W, GRID_W), F32))(dbias)
    dr = jnp.zeros((GRID_W, GRID_W * GRID_W), F32).at[:4 * N_DR].set(dr.reshape(4 * N_DR, GRID_W * GRID_W))

    def contract(d_ref, oh_ref, o_ref):
        o_ref[...] = lax.dot_general(d_ref[...], oh_ref[...], _NT, precision=lax.Precision.HIGHEST,
                                     preferred_element_type=F32)

    g = pl.pallas_call(contract, name="na_bias_contract",
                       out_shape=jax.ShapeDtypeStruct((GRID_W, LANES), F32))(dr, _na_onehot())
    return g[:4 * N_DR, :2 * NA_COLS - 1].reshape(4, N_DR, 2 * NA_COLS - 1)


def _na_window(r, n_rows):
    rs = jnp.clip(r - NA_ROWS // 2, 0, n_rows - NA_ROWS)
    return rs, r - rs


def _na_fwd(q, k, v, bias):
    L = q.shape[0]
    n_rows = L // GRID_W
    tm = _tile(L, 512, GRID_W)
    nt, nr = L // tm, tm // GRID_W
    win = NA_ROWS * GRID_W

    def kern(q_ref, k_ref, v_ref, b_ref, o_ref):
        i = pl.program_id(1)
        lane = lax.broadcasted_iota(jnp.int32, (GRID_W, LANES), 1)

        def body(rr, carry):
            rs, c = _na_window(i * nr + rr, n_rows)
            rows = pl.ds(pl.multiple_of(rr * GRID_W, GRID_W), GRID_W)
            wrows = pl.ds(pl.multiple_of(rs * GRID_W, GRID_W), win)
            qv, kw, vw = q_ref[rows, :], k_ref[wrows, :], v_ref[wrows, :]
            o = jnp.zeros((GRID_W, LANES), F32)
            for hh in range(2):
                mine = (lane < HEAD_DIM) == (hh == 0)
                qm = jnp.where(mine, qv, jnp.zeros_like(qv))
                s = _dot(qm, kw, _NT) * ATT_SCALE + b_ref[hh, c]
                e = jnp.exp(s - jnp.max(s, axis=-1, keepdims=True))
                pn = (e / jnp.sum(e, axis=-1, keepdims=True)).astype(BF16)
                o = jnp.where(mine, _dot(pn, vw), o)
            o_ref[rows, :] = o.astype(BF16)
            return carry

        lax.fori_loop(0, nr, body, 0, unroll=4)

    qspec = pl.BlockSpec((tm, LANES), lambda j, i: (i, j))
    kvspec = pl.BlockSpec((L, LANES), lambda j, i: (0, j))
    return pl.pallas_call(
        kern, name="na_fwd", grid=(2, nt),
        in_specs=[qspec, kvspec, kvspec, pl.BlockSpec((2, NA_ROWS, GRID_W, win), lambda j, i: (j, 0, 0, 0))],
        out_specs=qspec, out_shape=jax.ShapeDtypeStruct((L, GROUP_W), BF16), compiler_params=_ARB(2),
    )(q, k, v, bias)


def _na_bwd(dy, dy_block, q, k, v, bias):
    L = q.shape[0]
    n_rows = L // GRID_W
    tm = _tile(L, 512, GRID_W)
    nt, nr = L // tm, tm // GRID_W
    win = NA_ROWS * GRID_W

    def kern(dy_ref, q_ref, k_ref, v_ref, b_ref, dq_ref, dk_ref, dv_ref, db_ref):
        i = pl.program_id(1)

        @pl.when(i == 0)
        def _():
            dk_ref[...] = jnp.zeros_like(dk_ref)
            dv_ref[...] = jnp.zeros_like(dv_ref)
            db_ref[...] = jnp.zeros_like(db_ref)

        lane = lax.broadcasted_iota(jnp.int32, (GRID_W, LANES), 1)

        def body(rr, carry):
            rs, c = _na_window(i * nr + rr, n_rows)
            rows = pl.ds(pl.multiple_of(rr * GRID_W, GRID_W), GRID_W)
            wrows = pl.ds(pl.multiple_of(rs * GRID_W, GRID_W), win)
            qv, kw, vw = q_ref[rows, :], k_ref[wrows, :], v_ref[wrows, :]
            dyv = dy_ref[rows, :].astype(BF16)
            dq = jnp.zeros((GRID_W, LANES), F32)
            dkw = jnp.zeros((win, LANES), F32)
            dvw = jnp.zeros((win, LANES), F32)
            for hh in range(2):
                mine = (lane < HEAD_DIM) == (hh == 0)
                qm = jnp.where(mine, qv, jnp.zeros_like(qv))
                dom = jnp.where(mine, dyv, jnp.zeros_like(dyv))
                s = _dot(qm, kw, _NT) * ATT_SCALE + b_ref[hh, c]
                e = jnp.exp(s - jnp.max(s, axis=-1, keepdims=True))
                pn = e / jnp.sum(e, axis=-1, keepdims=True)
                dp = _dot(dom, vw, _NT)
                ds = pn * (dp - jnp.sum(pn * dp, axis=-1, keepdims=True))
                db_ref[hh, c] += ds
                dsb = ds.astype(BF16)
                dq = jnp.where(mine, _dot(dsb, kw) * ATT_SCALE, dq)
                dkw = dkw + _dot(dsb, qm, _TN) * ATT_SCALE
                dvw = dvw + _dot(pn.astype(BF16), dom, _TN)
            dq_ref[rows, :] = dq.astype(BF16)
            dk_ref[wrows, :] += dkw
            dv_ref[wrows, :] += dvw
            return carry

        lax.fori_loop(0, nr, body, 0, unroll=2)

    qspec = pl.BlockSpec((tm, LANES), lambda j, i: (i, j))
    kvspec = pl.BlockSpec((L, LANES), lambda j, i: (0, j))
    bspec = pl.BlockSpec((2, NA_ROWS, GRID_W, win), lambda j, i: (j, 0, 0, 0))
    return pl.pallas_call(
        kern, name="na_bwd", grid=(2, nt),
        in_specs=[pl.BlockSpec((tm, LANES), lambda j, i: (i, dy_block + j)), qspec, kvspec, kvspec, bspec],
        out_specs=[qspec, kvspec, kvspec, bspec],
        out_shape=[jax.ShapeDtypeStruct((L, GROUP_W), BF16), jax.ShapeDtypeStruct((L, GROUP_W), F32),
                   jax.ShapeDtypeStruct((L, GROUP_W), F32),
                   jax.ShapeDtypeStruct((4, NA_ROWS, GRID_W, win), F32)],
        compiler_params=_ARB(2),
    )(dy, q, k, v, bias)


def _dil_specs(n, tq):
    R = DIL_RADIUS
    step, nb = tq // R, n // R
    main = pl.BlockSpec((tq, LANES), lambda j, i: (i, j))
    prev = pl.BlockSpec((R, LANES), lambda j, i: (jnp.maximum(i * step - 1, 0), j))
    nxt = pl.BlockSpec((R, LANES), lambda j, i: (jnp.minimum((i + 1) * step, nb - 1), j))
    return main, prev, nxt


def _dil_valid(i, tq, n):
    R = DIL_RADIUS
    row = lax.broadcasted_iota(jnp.int32, (tq, tq + 2 * R), 0)
    col = lax.broadcasted_iota(jnp.int32, (tq, tq + 2 * R), 1)
    kpos = i * tq - R + col
    return (jnp.abs(col - R - row) <= R) & (kpos >= 0) & (kpos < n)


def _dil_fwd(q, k, v, dil):
    L = q.shape[0]
    n = L // dil
    tq = _tile(n, 256, DIL_RADIUS)
    view = lambda t: t.reshape(n, dil * GROUP_W)

    def kern(q_ref, kp_ref, k_ref, kn_ref, vp_ref, v_ref, vn_ref, o_ref, l_ref):
        i = pl.program_id(1)
        valid = _dil_valid(i, tq, n)
        qv = q_ref[...]
        ka = jnp.concatenate([kp_ref[...], k_ref[...], kn_ref[...]], axis=0)
        va = jnp.concatenate([vp_ref[...], v_ref[...], vn_ref[...]], axis=0)
        lane = lax.broadcasted_iota(jnp.int32, (tq, LANES), 1)
        o = jnp.zeros((tq, LANES), F32)
        lse = jnp.zeros((tq, LANES), F32)
        for hh in range(2):
            mine = (lane < HEAD_DIM) == (hh == 0)
            qm = jnp.where(mine, qv, jnp.zeros_like(qv))
            s = jnp.where(valid, _dot(qm, ka, _NT) * ATT_SCALE, NEG)
            m = jnp.max(s, axis=-1, keepdims=True)
            e = jnp.exp(s - m)
            den = jnp.sum(e, axis=-1, keepdims=True)
            o = jnp.where(mine, _dot((e / den).astype(BF16), va), o)
            lse = jnp.where(mine, m + jnp.log(den), lse)
        o_ref[...] = o
        l_ref[...] = lse

    main, prev, nxt = _dil_specs(n, tq)
    o, lse = pl.pallas_call(
        kern, name=f"dil_fwd_{dil}", grid=(2 * dil, n // tq),
        in_specs=[main, prev, main, nxt, prev, main, nxt], out_specs=[main, main],
        out_shape=[jax.ShapeDtypeStruct((n, dil * GROUP_W), F32)] * 2, compiler_params=_ARB(2),
    )(view(q), view(k), view(k), view(k), view(v), view(v), view(v))
    return o.reshape(L, GROUP_W), lse.reshape(L, GROUP_W)


def _dil_bwd(q, k, v, do, lse, dterm, dil):
    L = q.shape[0]
    n = L // dil
    R = DIL_RADIUS
    tq = _tile(n, 256, R)
    nq = n // tq
    view = lambda t: t.reshape(n, dil * GROUP_W)

    def kern(q_ref, kp_ref, k_ref, kn_ref, vp_ref, v_ref, vn_ref, do_ref, l_ref, dt_ref, dq_ref, dk_ref, dv_ref):
        i = pl.program_id(1)

        @pl.when(i == 0)
        def _():
            dk_ref[...] = jnp.zeros_like(dk_ref)
            dv_ref[...] = jnp.zeros_like(dv_ref)

        valid = _dil_valid(i, tq, n)
        qv, dov = q_ref[...], do_ref[...]
        ka = jnp.concatenate([kp_ref[...], k_ref[...], kn_ref[...]], axis=0)
        va = jnp.concatenate([vp_ref[...], v_ref[...], vn_ref[...]], axis=0)
        lv, dtv = l_ref[...], dt_ref[...]
        lane = lax.broadcasted_iota(jnp.int32, (tq, LANES), 1)
        dq = jnp.zeros((tq, LANES), F32)
        dka = jnp.zeros((tq + 2 * R, LANES), F32)
        dva = jnp.zeros((tq + 2 * R, LANES), F32)
        for hh in range(2):
            mine = (lane < HEAD_DIM) == (hh == 0)
            qm = jnp.where(mine, qv, jnp.zeros_like(qv))
            dom = jnp.where(mine, dov, jnp.zeros_like(dov))
            c0 = hh * HEAD_DIM
            s = _dot(qm, ka, _NT) * ATT_SCALE
            pn = jnp.where(valid, jnp.exp(s - lv[:, c0:c0 + 1]), 0.0)
            ds = pn * (_dot(dom, va, _NT) - dtv[:, c0:c0 + 1])
            dsb = ds.astype(BF16)
            dq = jnp.where(mine, _dot(dsb, ka) * ATT_SCALE, dq)
            dka = dka + _dot(dsb, qm, _TN) * ATT_SCALE
            dva = dva + _dot(pn.astype(BF16), dom, _TN)
        dq_ref[...] = dq
        r0 = pl.multiple_of(i * tq, R)
        dk_ref[pl.ds(r0, tq), :] += dka[R:R + tq]
        dv_ref[pl.ds(r0, tq), :] += dva[R:R + tq]

        @pl.when(i > 0)
        def _():
            dk_ref[pl.ds(r0 - R, R), :] += dka[:R]
            dv_ref[pl.ds(r0 - R, R), :] += dva[:R]

        @pl.when(i < nq - 1)
        def _():
            dk_ref[pl.ds(r0 + tq, R), :] += dka[R + tq:]
            dv_ref[pl.ds(r0 + tq, R), :] += dva[R + tq:]

    main, prev, nxt = _dil_specs(n, tq)
    whole = pl.BlockSpec((n, LANES), lambda j, i: (0, j))
    shp = jax.ShapeDtypeStruct((n, dil * GROUP_W), F32)
    dq, dk, dv = pl.pallas_call(
        kern, name=f"dil_bwd_{dil}", grid=(2 * dil, nq),
        in_specs=[main, prev, main, nxt, prev, main, nxt, main, main, main], out_specs=[main, whole, whole],
        out_shape=[shp] * 3, compiler_params=_ARB(2),
    )(view(q), view(k), view(k), view(k), view(v), view(v), view(v), view(do), view(lse), view(dterm))
    return dq.reshape(L, GROUP_W), dk.reshape(L, GROUP_W), dv.reshape(L, GROUP_W)


def _dil_weights(lses):
    m = jnp.maximum(jnp.maximum(lses[0], lses[1]), lses[2])
    e = [jnp.exp(l - m) for l in lses]
    tot = e[0] + e[1] + e[2]
    return [x / tot for x in e]


def _dilated_fwd(q, k, v):
    res = [_dil_fwd(q, k, v, dil) for _, dil in DIL_PAIRS]

    def body(i, nt, o0, o1, o2, l0, l1, l2):
        w = _dil_weights((l0, l1, l2))
        return w[0] * o0 + w[1] * o1 + w[2] * o2

    ins = [Row(r[0]) for r in res] + [Row(r[1]) for r in res]
    return _rows("dil_combine", body, 512, ins, [("row", GROUP_W, BF16)])[0], res


def _dilated_bwd(dy, dy_cb, q, k, v, saved, cosf, sinf):
    def split(i, nt, dy, o0, o1, o2, l0, l1, l2, bd):
        w = _dil_weights((l0, l1, l2))
        y = w[0] * o0 + w[1] * o1 + w[2] * o2
        dyy = _head_sum(dy * y, bd)
        return tuple(wg * dy for wg in w) + tuple(wg * dyy for wg in w)

    ins = [Row(dy, GROUP_W, dy_cb)] + [Row(r[0]) for r in saved] + [Row(r[1]) for r in saved]
    outs = _rows("dil_split_bwd", split, 512, ins + [Full(_block_ones(GROUP_W, HEAD_DIM))],
                 [("row", GROUP_W, BF16)] * 3 + [("row", GROUP_W, F32)] * 3)
    g = [_dil_bwd(q, k, v, outs[b], saved[b][1], outs[3 + b], dil) for b, (_, dil) in enumerate(DIL_PAIRS)]

    def finish(i, nt, q0, q1, q2, k0, k1, k2, v0, v1, v2, c, s):
        dq, dk = q0 + q1 + q2, k0 + k1 + k2
        return dq * c + _swap_halves(dq * s), dk * c + _swap_halves(dk * s), v0 + v1 + v2

    ins = [Row(g[b][t]) for t in range(3) for b in range(3)] + [Row(cosf), Row(sinf)]
    return _rows("dil_finish_bwd", finish, 512, ins, [("row", GROUP_W, BF16)] * 3)


def _layer_fwd(x, W, l, cosf, sinf):
    h1 = _rms_fwd(x, W["mix_norm_pre"][l][None], "mix_norm")
    p = _mm(h1, W["w_in"][l], "nn", F32, "proj_in")
    ya, sa = _gla_fwd(p, W, l)
    qb, kb, vb, qd, kd, vd = _attn_prep(p, cosf, sinf)
    bias = _na_bias(W["na_rpb"][l])
    yb = _na_fwd(qb, kb, vb, bias)
    yc, sc = _lru_fwd(p, W, l)
    yd, sd = _dilated_fwd(qd, kd, vd)
    ycat = jnp.concatenate([ya, yb, yc, yd], axis=1)
    ymix = _mm(ycat, W["w_out"][l], "nn", F32, "proj_out", tm=1024)
    xm = _rms_resid_fwd(x, ymix, W["mix_norm_post"][l][None], "mix_resid")
    h2 = _rms_fwd(xm, W["ffn_norm_pre"][l][None], "ffn_norm")
    gu, act = _ffn_in_swiglu(h2, W["ffn_w_in"][l])
    f = _mm(act, W["ffn_w_out"][l], "nn", F32, "ffn_out")
    xo = _rms_resid_fwd(xm, f, W["ffn_norm_post"][l][None], "ffn_resid")
    saved = dict(x=x, h1=h1, p=p, sa=sa, att=(qb, kb, vb, qd, kd, vd), bias=bias, sc=sc, sd=sd, ycat=ycat, ymix=ymix,
                 xm=xm, h2=h2, gu=gu, act=act, f=f)
    return xo, saved


def _layer_bwd(dxo, W, l, S, cosf, sinf):
    g = {}
    df, g["ffn_norm_post"] = _rms_bwd(dxo, S["f"], W["ffn_norm_post"][l][None], "ffn_resid_bwd", out_dtype=BF16)
    g["ffn_w_out"] = _mm(S["act"], df, "tn", BF16, "ffn_out_dw", tm=256, tk=4096)
    dgu = _ffn_out_dx_swiglu(df, W["ffn_w_out"][l], S["gu"])
    dh2 = _mm(dgu, W["ffn_w_in"][l], "nt", F32, "ffn_in_dx")
    g["ffn_w_in"] = _mm(S["h2"], dgu, "tn", BF16, "ffn_in_dw", tm=1024, tn=512, tk=4096)
    dxm, g["ffn_norm_pre"] = _rms_bwd(dh2, S["xm"], W["ffn_norm_pre"][l][None], "ffn_norm_bwd", resid=dxo)
    dymix, g["mix_norm_post"] = _rms_bwd(dxm, S["ymix"], W["mix_norm_post"][l][None], "mix_resid_bwd", out_dtype=BF16)
    dycat = _mm(dymix, W["w_out"][l], "nt", F32, "proj_out_dx", tm=1024)
    g["w_out"] = _mm(S["ycat"], dymix, "tn", BF16, "proj_out_dw", tm=1024, tn=512, tk=4096)
    p = S["p"]
    qb, kb, vb, qd, kd, vd = S["att"]
    (dqa, dka, dva, dga, dz), ga = _gla_bwd(dycat, 0, p, W, l, S["sa"])
    dqb, dkb, dvb, dbias = _na_bwd(dycat, 2, qb, kb, vb, S["bias"])
    g["na_rpb"] = _na_bias_bwd(dbias)
    dxc, dgc, gc = _lru_bwd(dycat, 2, p, W, l, S["sc"])
    dqd, dkd, dvd = _dilated_bwd(dycat, 3, qd, kd, vd, S["sd"], cosf, sinf)
    g.update(ga)
    g.update(gc)
    dp = jnp.concatenate([dqa, dka, dva, dga, dqb, dkb.astype(BF16), dvb.astype(BF16), dxc, dgc, dqd, dkd, dvd, dz], axis=1)
    dh1 = _mm(dp, W["w_in"][l], "nt", F32, "proj_in_dx")
    g["w_in"] = _mm(S["h1"], dp, "tn", BF16, "proj_in_dw", tm=1024, tn=640, tk=4096)
    dx, g["mix_norm_pre"] = _rms_bwd(dh1, S["x"], W["mix_norm_pre"][l][None], "mix_norm_bwd", resid=dxm)
    for n in ("ffn_norm_post", "ffn_norm_pre", "mix_norm_post", "mix_norm_pre"):
        g[n] = g[n][0]
    return dx, g


MESH_AXES = ("x", "y", "c")


class Xfer:
    def __init__(self, arr, kind):
        self.arr, self.kind = arr, kind
        shp = arr.shape
        if kind == "all":
            self.out = (N_DEV,) + shp
        elif kind == "slot":
            self.out = shp
        elif kind == "rows":
            self.r = shp[1] // N_DEV
            self.out = (N_DEV, shp[0], self.r, shp[2])
        else:
            self.r = shp[1]
            self.out = (shp[0], N_DEV * shp[1], shp[2])

    def src(self, ref, peer):
        if self.kind == "slot":
            return ref.at[peer]
        if self.kind == "rows":
            return ref.at[:, pl.ds(peer * self.r, self.r), :]
        return ref

    def dst(self, ref, me):
        if self.kind == "place":
            return ref.at[:, pl.ds(me * self.r, self.r), :]
        return ref.at[me]


def _exchange(items, name):
    n = len(items)

    def body(*refs):
        ins, outs = refs[:n], refs[n:2 * n]
        send_sems, recv_sems, local_sems = refs[2 * n:]
        x, y, c = (lax.axis_index(a) for a in MESH_AXES)
        me = 4 * x + 2 * y + c
        copies = []
        for t, it in enumerate(items):
            mine = pltpu.make_async_copy(it.src(ins[t], me), it.dst(outs[t], me), local_sems.at[t])
            mine.start()
            copies.append(mine)
            for k in range(1, N_DEV):
                px, py, pc = x ^ ((k >> 2) & 1), y ^ ((k >> 1) & 1), c ^ (k & 1)
                s = t * (N_DEV - 1) + k - 1
                cp = pltpu.make_async_remote_copy(
                    src_ref=it.src(ins[t], 4 * px + 2 * py + pc), dst_ref=it.dst(outs[t], me),
                    send_sem=send_sems.at[s], recv_sem=recv_sems.at[s], device_id=(px, py, pc),
                    device_id_type=pl.DeviceIdType.MESH)
                cp.start()
                copies.append(cp)
        for cp in copies:
            cp.wait()

    anyspec = pl.BlockSpec(memory_space=pl.ANY)
    return pl.pallas_call(
        body, name=name, out_shape=[jax.ShapeDtypeStruct(it.out, it.arr.dtype) for it in items],
        in_specs=[anyspec] * n, out_specs=[anyspec] * n,
        scratch_shapes=[pltpu.SemaphoreType.DMA((n * (N_DEV - 1),)), pltpu.SemaphoreType.DMA((n * (N_DEV - 1),)),
                        pltpu.SemaphoreType.DMA((n,))],
    )(*[it.arr for it in items])


def _column_segments(width, permuted):
    z0, z1, zn = 4 * GROUP_W, 4 * GROUP_W + 2 * GLA_RANK, 12 * GROUP_W
    segs = []
    for d in range(N_DEV):
        lo, hi = d * width, (d + 1) * width
        if not permuted:
            segs.append([(0, width, lo)])
            continue
        runs = []
        for a, b, shift in ((0, z0, 0), (z0, z1, zn - z0), (z1, 10 ** 9, -(z1 - z0))):
            s, e = max(lo, a), min(hi, b)
            if s < e:
                runs.append((s - lo, e - lo, s + shift))
        segs.append(runs)
    return segs


def _cols_from_pieces(pieces, segs, cols, name):
    _, R, w = pieces.shape
    tm = _tile(R, 256, 16)
    used = max(f + (b - a) for runs in segs for a, b, f in runs)

    def kern(p_ref, o_ref):
        for d, runs in enumerate(segs):
            for a, b, f in runs:
                o_ref[:, f:f + (b - a)] = p_ref[d, :, a:b]
        if used < cols:
            o_ref[:, used:cols] = jnp.zeros((tm, cols - used), o_ref.dtype)

    return pl.pallas_call(
        kern, name=name, grid=(R // tm,), in_specs=[pl.BlockSpec((N_DEV, tm, w), lambda i: (0, i, 0))],
        out_specs=pl.BlockSpec((tm, cols), lambda i: (i, 0)), out_shape=jax.ShapeDtypeStruct((R, cols), pieces.dtype),
        compiler_params=_ARB(1),
    )(pieces)


def _pieces_from_cols(full, segs, w, name):
    R, cols = full.shape
    tm = _tile(R, 256, 16)

    def kern(f_ref, o_ref):
        for d, runs in enumerate(segs):
            for a, b, f in runs:
                o_ref[d, :, a:b] = f_ref[:, f:f + (b - a)]

    return pl.pallas_call(
        kern, name=name, grid=(R // tm,), in_specs=[pl.BlockSpec((tm, cols), lambda i: (i, 0))],
        out_specs=pl.BlockSpec((N_DEV, tm, w), lambda i: (0, i, 0)),
        out_shape=jax.ShapeDtypeStruct((N_DEV, R, w), full.dtype), compiler_params=_ARB(1),
    )(full)


def _sum_slots(recv, name):
    n, R, C = recv.shape
    tm = _tile(R, 256, 16)

    def kern(*refs):
        acc = refs[0][...].astype(F32)
        for r in refs[1:n]:
            acc = acc + r[...].astype(F32)
        refs[n][...] = acc

    return pl.pallas_call(
        kern, name=name, grid=(R // tm,),
        in_specs=[pl.BlockSpec((None, tm, C), lambda i, _s=s: (_s, i, 0)) for s in range(n)],
        out_specs=pl.BlockSpec((tm, C), lambda i: (i, 0)), out_shape=jax.ShapeDtypeStruct((R, C), F32),
        compiler_params=_ARB(1),
    )(*([recv] * n))


BIG = (("w_in", 2), ("w_out", 1), ("ffn_w_in", 2), ("ffn_w_out", 1))
SMALL_SHARDED = ("gla_w_gate", "gla_b_gate", "lru_conv_w", "lru_b_a", "lru_b_x", "lru_lambda")
REPLICATED = ("mix_norm_pre", "mix_norm_post", "gla_norm", "na_rpb", "lru_conv_b", "lru_w_a", "lru_w_x",
              "ffn_norm_pre", "ffn_norm_post")
WEIGHTS = ("mix_norm_pre", "mix_norm_post", "w_in", "gla_w_gate", "gla_b_gate", "gla_norm", "na_rpb", "lru_conv_w",
           "lru_conv_b", "lru_w_a", "lru_b_a", "lru_w_x", "lru_b_x", "lru_lambda", "w_out", "ffn_norm_pre",
           "ffn_norm_post", "ffn_w_in", "ffn_w_out")
FLAT_C = 1024


def _to_rows(vec, row_unit):
    n = vec.shape[-1]
    rows = -(-n // (FLAT_C * row_unit)) * row_unit
    pad = [(0, 0)] * (vec.ndim - 1) + [(0, rows * FLAT_C - n)]
    return jnp.pad(vec, pad).reshape(vec.shape[:-1] + (rows, FLAT_C))


def _unshard(parts, axis):
    t = jnp.moveaxis(parts, 0, axis)
    shp = list(t.shape)
    return t.reshape(shp[:axis] + [shp[axis] * shp[axis + 1]] + shp[axis + 2:])


def _shards(full, axis):
    shp = list(full.shape)
    t = full.reshape(shp[:axis] + [N_DEV, shp[axis] // N_DEV] + shp[axis + 1:])
    return jnp.moveaxis(t, axis, 0)


def _two_d(a):
    return a.reshape(-1, a.shape[-1])


def _gather_weights(W):
    depth, dm, w_in_w = W["w_in"].shape
    ffn_w = W["ffn_w_in"].shape[-1]
    small = jnp.concatenate([W[n].reshape(-1) for n in SMALL_SHARDED])
    small16 = _to_rows(lax.bitcast_convert_type(small, jnp.uint16).reshape(-1), 16)
    items = [Xfer(_two_d(W["w_in"].astype(BF16)), "all"), Xfer(_two_d(W["ffn_w_in"].astype(BF16)), "all"),
             Xfer(W["w_out"].astype(BF16), "place"), Xfer(W["ffn_w_out"].astype(BF16), "place"), Xfer(small16, "all")]
    p_in, p_ffn, w_out, ffn_w_out, sm = _exchange(items, "gather_weights")
    full = dict(W)
    full["w_in"] = _cols_from_pieces(p_in, _column_segments(w_in_w, True), P_COLS, "unpack_w_in").reshape(depth, dm, P_COLS)
    full["ffn_w_in"] = _cols_from_pieces(p_ffn, _column_segments(ffn_w, False), N_DEV * ffn_w,
                                         "unpack_ffn_w_in").reshape(depth, dm, N_DEV * ffn_w)
    full["w_out"], full["ffn_w_out"] = w_out, ffn_w_out
    sm = lax.bitcast_convert_type(sm.reshape(N_DEV, -1)[:, :2 * small.size].reshape(N_DEV, small.size, 2), F32)
    off = 0
    for n in SMALL_SHARDED:
        size = W[n].size
        full[n] = _unshard(sm[:, off:off + size].reshape((N_DEV,) + W[n].shape), W[n].ndim - 1)
        off += size
    return full


def _reduce_grads(G, W):
    p_in = _pieces_from_cols(_two_d(G["w_in"]), _column_segments(W["w_in"].shape[-1], True), W["w_in"].shape[-1],
                             "pack_w_in")
    p_ffn = _pieces_from_cols(_two_d(G["ffn_w_in"]), _column_segments(W["ffn_w_in"].shape[-1], False),
                              W["ffn_w_in"].shape[-1], "pack_ffn_w_in")
    small = jnp.concatenate([_shards(G[n], G[n].ndim - 1).reshape(N_DEV, -1) for n in SMALL_SHARDED], axis=1)
    repl = jnp.concatenate([G[n].reshape(-1) for n in REPLICATED])
    got = _exchange([Xfer(p_in, "slot"), Xfer(p_ffn, "slot"), Xfer(G["w_out"], "rows"), Xfer(G["ffn_w_out"], "rows"),
                     Xfer(_to_rows(small, 8), "slot"), Xfer(_to_rows(repl, 8), "all")], "exchange_grads")
    out = {}
    for n, r in zip(("w_in", "ffn_w_in", "w_out", "ffn_w_out"), got):
        out[n] = _sum_slots(r.reshape(N_DEV, -1, r.shape[-1]), "sum_" + n).reshape(W[n].shape)
    for names, r, tag in ((SMALL_SHARDED, got[4], "sum_small"), (REPLICATED, got[5], "sum_replicated")):
        flat, off = _sum_slots(r, tag).reshape(-1), 0
        for n in names:
            out[n] = flat[off:off + W[n].size].reshape(W[n].shape)
            off += W[n].size
    return out


def _update(W, G, M, V):
    delta, new_m, new_v = {}, {}, {}
    for n, _ in BIG:
        two_d = lambda a: a.reshape(-1, a.shape[-1])
        d, m, v = _adamw(two_d(W[n]), two_d(G[n]), two_d(M[n]), two_d(V[n]), "adamw_" + n)
        delta[n], new_m[n], new_v[n] = (t.reshape(W[n].shape) for t in (d, m, v))
    rest = SMALL_SHARDED + REPLICATED
    pack = lambda D: _to_rows(jnp.concatenate([D[n].reshape(-1) for n in rest]), 16)
    d, m, v = _adamw(pack(W), pack(G), pack(M), pack(V), "adamw_small")
    off = 0
    for n in rest:
        sl = lambda t: t.reshape(-1)[off:off + W[n].size].reshape(W[n].shape)
        delta[n], new_m[n], new_v[n] = sl(d), sl(m), sl(v)
        off += W[n].size
    return delta, new_m, new_v


def _step(x, target, Wf):
    L = x.shape[0]
    depth = Wf["w_in"].shape[0]
    cosf, sinf = _rope_tables(L)
    saved = []
    for l in range(depth):
        x, S = _layer_fwd(x, Wf, l, cosf, sinf)
        saved.append(S)
    loss, dx = _loss_fwd_bwd(x, target)
    grads = [None] * depth
    for l in reversed(range(depth)):
        dx, grads[l] = _layer_bwd(dx, Wf, l, saved[l], cosf, sinf)
    return loss, dx, {n: jnp.stack([g[n] for g in grads]) for n in WEIGHTS}


def kernel(x, mix_norm_pre, mix_norm_post, w_in, gla_w_gate, gla_b_gate, gla_norm, na_rpb, lru_conv_w, lru_conv_b, lru_w_a, lru_b_a, lru_w_x, lru_b_x, lru_lambda, w_out, ffn_norm_pre, ffn_norm_post, ffn_w_in, ffn_w_out, loss_target, m_mix_norm_pre, m_mix_norm_post, m_w_in, m_gla_w_gate, m_gla_b_gate, m_gla_norm, m_na_rpb, m_lru_conv_w, m_lru_conv_b, m_lru_w_a, m_lru_b_a, m_lru_w_x, m_lru_b_x, m_lru_lambda, m_w_out, m_ffn_norm_pre, m_ffn_norm_post, m_ffn_w_in, m_ffn_w_out, v_mix_norm_pre, v_mix_norm_post, v_w_in, v_gla_w_gate, v_gla_b_gate, v_gla_norm, v_na_rpb, v_lru_conv_w, v_lru_conv_b, v_lru_w_a, v_lru_b_a, v_lru_w_x, v_lru_b_x, v_lru_lambda, v_w_out, v_ffn_norm_pre, v_ffn_norm_post, v_ffn_w_in, v_ffn_w_out):
    W = dict(zip(WEIGHTS, (mix_norm_pre, mix_norm_post, w_in, gla_w_gate, gla_b_gate, gla_norm, na_rpb, lru_conv_w, lru_conv_b, lru_w_a, lru_b_a, lru_w_x, lru_b_x, lru_lambda, w_out, ffn_norm_pre, ffn_norm_post, ffn_w_in, ffn_w_out)))
    M = dict(zip(WEIGHTS, (m_mix_norm_pre, m_mix_norm_post, m_w_in, m_gla_w_gate, m_gla_b_gate, m_gla_norm, m_na_rpb, m_lru_conv_w, m_lru_conv_b, m_lru_w_a, m_lru_b_a, m_lru_w_x, m_lru_b_x, m_lru_lambda, m_w_out, m_ffn_norm_pre, m_ffn_norm_post, m_ffn_w_in, m_ffn_w_out)))
    V = dict(zip(WEIGHTS, (v_mix_norm_pre, v_mix_norm_post, v_w_in, v_gla_w_gate, v_gla_b_gate, v_gla_norm, v_na_rpb, v_lru_conv_w, v_lru_conv_b, v_lru_w_a, v_lru_b_a, v_lru_w_x, v_lru_b_x, v_lru_lambda, v_w_out, v_ffn_norm_pre, v_ffn_norm_post, v_ffn_w_in, v_ffn_w_out)))
    Wf = _gather_weights(W)
    loss, dx, Gfull = _step(x[0], loss_target[0], Wf)
    loss = lax.psum(loss, MESH_AXES)
    G = _reduce_grads(Gfull, W)
    delta, new_m, new_v = _update(W, G, M, V)
    return (loss, dx[None], *[G[n] for n in WEIGHTS], *[delta[n] for n in WEIGHTS], *[new_m[n] for n in WEIGHTS],
            *[new_v[n] for n in WEIGHTS])
```

```python
import functools
import math

import numpy as np
import jax
import jax.numpy as jnp
from jax import lax
from jax.experimental import pallas as pl
from jax.experimental.pallas import tpu as pltpu

F32 = jnp.float32
BF16 = jnp.bfloat16

N_DEV = 8
HEAD_DIM = 64
GROUP_W = 256
GLA_RANK = 16
GLA_TAU = 16.0
GLA_CHUNK = 64
GRID_W = 64
NA_ROWS = 8
NA_COLS = 16
LRU_C = 8.0
DIL_PAIRS = ((128, 1), (512, 4), (2048, 16))
DIL_RADIUS = 64
ROPE_THETA = 10000.0
EPS = 1e-6
ATT_SCALE = HEAD_DIM ** -0.5
NEG = -1e30
LANES = 128
P_COLS = 12 * GROUP_W + LANES
Z_BLOCK = 12 * GROUP_W // LANES

ADAM_LR = 0.001
ADAM_B1 = 0.9
ADAM_B2 = 0.999
ADAM_EPS = 1e-08
ADAM_WD = 0.01
ADAM_STEP = 10

VMEM_LIMIT = 56 * 1024 * 1024
_ARB = lambda n: pltpu.CompilerParams(dimension_semantics=("arbitrary",) * n, vmem_limit_bytes=VMEM_LIMIT)


def _tile(dim, pref, unit):
    t = min(pref, dim) // unit * unit
    while t >= unit:
        if dim % t == 0:
            return t
        t -= unit
    return dim


def _mm(a, b, mode, out_dtype, name, tm=512, tn=None, tk=None):
    if mode == "nn":
        (M, K), (_, N) = a.shape, b.shape
    elif mode == "nt":
        (M, K), (N, _) = a.shape, b.shape
    else:
        (K, M), (_, N) = a.shape, b.shape
    tm = _tile(M, tm, LANES if mode == "tn" else 8)
    tn = _tile(N, tn or N, LANES)
    tk = _tile(K, tk or K, LANES)
    nk = K // tk
    dims = {"nn": (((1,), (0,)), ((), ())), "nt": (((1,), (1,)), ((), ())), "tn": (((0,), (0,)), ((), ()))}[mode]

    def kern(a_ref, b_ref, o_ref, *acc):
        part = lax.dot_general(a_ref[...].astype(BF16), b_ref[...].astype(BF16), dims, preferred_element_type=F32)
        if nk == 1:
            o_ref[...] = part.astype(out_dtype)
            return
        k = pl.program_id(2)

        @pl.when(k == 0)
        def _():
            acc[0][...] = part

        @pl.when(jnp.logical_and(k > 0, k < nk - 1))
        def _():
            acc[0][...] += part

        @pl.when(k == nk - 1)
        def _():
            o_ref[...] = (acc[0][...] + part).astype(out_dtype)

    a_spec = pl.BlockSpec((tk, tm), lambda i, j, k: (k, i)) if mode == "tn" else pl.BlockSpec((tm, tk), lambda i, j, k: (i, k))
    b_spec = pl.BlockSpec((tn, tk), lambda i, j, k: (j, k)) if mode == "nt" else pl.BlockSpec((tk, tn), lambda i, j, k: (k, j))
    return pl.pallas_call(
        kern, name=name, grid=(M // tm, N // tn, nk),
        in_specs=[a_spec, b_spec], out_specs=pl.BlockSpec((tm, tn), lambda i, j, k: (i, j)),
        out_shape=jax.ShapeDtypeStruct((M, N), out_dtype),
        scratch_shapes=[pltpu.VMEM((tm, tn), F32)] if nk > 1 else [],
        compiler_params=_ARB(3),
    )(a, b)


class Row:
    def __init__(self, a, width=None, cb=0, halo=False):
        self.a, self.width, self.cb, self.halo = a, width, cb, halo


class Full:
    def __init__(self, a):
        self.a = a


HALO = 8


def _rows(name, body, tm, ins, outs):
    L = next(s.a.shape[0] for s in ins if isinstance(s, Row))
    tm = _tile(L, tm, 16)
    nt = L // tm
    nb8 = L // HALO
    step = tm // HALO
    in_specs, arrays, layout = [], [], []
    for s in ins:
        if isinstance(s, Full):
            nd = s.a.ndim
            in_specs.append(pl.BlockSpec(s.a.shape, lambda i, _nd=nd: (0,) * _nd))
            arrays.append(s.a)
            layout.append(1)
        else:
            w = s.width or s.a.shape[1]
            in_specs.append(pl.BlockSpec((tm, w), lambda i, _cb=s.cb: (i, _cb)))
            arrays.append(s.a)
            if s.halo:
                in_specs.append(pl.BlockSpec((HALO, w), lambda i, _cb=s.cb: (jnp.maximum(i * step - 1, 0), _cb)))
                in_specs.append(pl.BlockSpec((HALO, w), lambda i, _cb=s.cb: (jnp.minimum((i + 1) * step, nb8 - 1), _cb)))
                arrays += [s.a, s.a]
                layout.append(3)
            else:
                layout.append(1)
    out_specs, out_shapes = [], []
    for kind, shp, dt in outs:
        if kind == "row":
            out_specs.append(pl.BlockSpec((tm, shp), lambda i: (i, 0)))
            out_shapes.append(jax.ShapeDtypeStruct((L, shp), dt))
        else:
            out_specs.append(pl.BlockSpec(shp, lambda i, _n=len(shp): (0,) * _n))
            out_shapes.append(jax.ShapeDtypeStruct(shp, dt))
    n_in = len(arrays)

    def kern(*refs):
        i = pl.program_id(0)
        vals, p = [], 0
        for n in layout:
            if n == 1:
                vals.append(refs[p][...])
            else:
                vals.append((refs[p + 1][...], refs[p][...], refs[p + 2][...]))
            p += n
        res = body(i, nt, *vals)
        if not isinstance(res, (tuple, list)):
            res = (res,)
        for (kind, shp, dt), o_ref, r in zip(outs, refs[n_in:], res):
            if kind == "row":
                o_ref[...] = r.astype(dt)
            else:
                @pl.when(i == 0)
                def _(o_ref=o_ref):
                    o_ref[...] = jnp.zeros_like(o_ref)
                o_ref[...] += r.astype(dt)

    res = pl.pallas_call(
        kern, name=name, grid=(nt,), in_specs=in_specs, out_specs=out_specs, out_shape=out_shapes,
        compiler_params=_ARB(1),
    )(*arrays)
    return res


def _shift(h, o, i, nt):
    prev, cur, nxt = h
    if o == 0:
        return cur
    tm = cur.shape[0]
    cat = jnp.concatenate([prev, cur, nxt], axis=0)
    sh = pltpu.roll(cat, (-o) % (tm + 2 * HALO), axis=0)[HALO:HALO + tm]
    row = lax.broadcasted_iota(jnp.int32, cur.shape, 0)
    if o < 0:
        ok = jnp.logical_or(i > 0, row >= -o)
    else:
        ok = jnp.logical_or(i < nt - 1, row < tm - o)
    return jnp.where(ok, sh, 0.0)


def _colsum(v):
    return jnp.sum(v, axis=0, keepdims=True)


def _sigmoid(x):
    return 1.0 / (1.0 + jnp.exp(-x))


def _softplus(x):
    return jnp.maximum(x, 0.0) + jnp.log1p(jnp.exp(-jnp.abs(x)))


def _silu(x):
    return x * _sigmoid(x)


def _dsilu(x):
    s = _sigmoid(x)
    return s * (1.0 + x * (1.0 - s))


_GELU_C = math.sqrt(2.0 / math.pi)


def _gelu(x):
    return 0.5 * x * (1.0 + jnp.tanh(_GELU_C * (x + 0.044715 * x * x * x)))


def _dgelu(x):
    t = jnp.tanh(_GELU_C * (x + 0.044715 * x * x * x))
    return 0.5 * (1.0 + t) + 0.5 * x * (1.0 - t * t) * _GELU_C * (1.0 + 3.0 * 0.044715 * x * x)


def _head_sum(v, bd):
    return jnp.dot(v, bd, precision=lax.Precision.HIGHEST, preferred_element_type=F32)


def _block_ones(n, blk):
    r = np.arange(n)
    return jnp.asarray((r[:, None] // blk == r[None, :] // blk).astype(np.float32))


def _rms_fwd(x, g, name):
    def body(i, nt, x, g):
        r = lax.rsqrt(jnp.mean(x * x, axis=-1, keepdims=True) + EPS)
        return x * r * g
    return _rows(name, body, 256, [Row(x), Full(g)], [("row", x.shape[1], BF16)])[0]


def _rms_resid_fwd(x, y, g, name):
    def body(i, nt, x, y, g):
        r = lax.rsqrt(jnp.mean(y * y, axis=-1, keepdims=True) + EPS)
        return x + y * r * g
    return _rows(name, body, 256, [Row(x), Row(y), Full(g)], [("row", x.shape[1], F32)])[0]


def _rms_bwd(dy, x, g, name, resid=None, out_dtype=F32):
    D = x.shape[1]

    def body(i, nt, dy, x, g, *rest):
        dy = dy.astype(F32)
        r = lax.rsqrt(jnp.mean(x * x, axis=-1, keepdims=True) + EPS)
        xh = x * r
        dxh = dy * g
        dx = r * (dxh - xh * jnp.mean(dxh * xh, axis=-1, keepdims=True))
        if rest:
            dx = dx + rest[0]
        return dx, _colsum(dy * xh)

    ins = [Row(dy), Row(x), Full(g)] + ([Row(resid)] if resid is not None else [])
    return _rows(name, body, 256, ins, [("row", D, out_dtype), ("acc", (1, D), F32)])


def _ffn_in_swiglu(h, w):
    (M, K), N = h.shape, w.shape[1]
    F = N // 2
    tm = _tile(M, 256, 16)

    def kern(a_ref, b_ref, gu_ref, act_ref):
        gu = _dot(a_ref[...], b_ref[...])
        gu_ref[...] = gu
        act_ref[...] = (_silu(gu[:, :F]) * gu[:, F:]).astype(BF16)

    return pl.pallas_call(
        kern, name="ffn_in_swiglu", grid=(M // tm,),
        in_specs=[pl.BlockSpec((tm, K), lambda i: (i, 0)), pl.BlockSpec((K, N), lambda i: (0, 0))],
        out_specs=[pl.BlockSpec((tm, N), lambda i: (i, 0)), pl.BlockSpec((tm, F), lambda i: (i, 0))],
        out_shape=[jax.ShapeDtypeStruct((M, N), F32), jax.ShapeDtypeStruct((M, F), BF16)], compiler_params=_ARB(1),
    )(h, w)


def _ffn_out_dx_swiglu(df, w, gu):
    (M, K), N = df.shape, gu.shape[1]
    F = N // 2
    tm = _tile(M, 256, 16)

    def kern(a_ref, b_ref, gu_ref, o_ref):
        da = _dot(a_ref[...], b_ref[...], _NT)
        gu = gu_ref[...]
        gate, up = gu[:, :F], gu[:, F:]
        o_ref[:, :F] = (da * up * _dsilu(gate)).astype(BF16)
        o_ref[:, F:] = (da * _silu(gate)).astype(BF16)

    return pl.pallas_call(
        kern, name="ffn_out_dx_swiglu", grid=(M // tm,),
        in_specs=[pl.BlockSpec((tm, K), lambda i: (i, 0)), pl.BlockSpec((F, K), lambda i: (0, 0)),
                  pl.BlockSpec((tm, N), lambda i: (i, 0))],
        out_specs=pl.BlockSpec((tm, N), lambda i: (i, 0)), out_shape=jax.ShapeDtypeStruct((M, N), BF16),
        compiler_params=_ARB(1),
    )(df, w, gu)


def _loss_fwd_bwd(y, target):
    D = y.shape[1]

    def body(i, nt, y, t):
        err = y - t
        part = 0.5 * jnp.sum(jnp.mean(err * err, axis=-1, keepdims=True), axis=0, keepdims=True)
        return err * (1.0 / D), jnp.broadcast_to(part, (1, LANES))
    dy, loss = _rows("loss", body, 256, [Row(y), Row(target)], [("row", D, F32), ("acc", (1, LANES), F32)])
    return loss[0, 0], dy


def _adamw(w, g, m, v, name):
    C = w.shape[1]
    bc1 = 1.0 - ADAM_B1 ** ADAM_STEP
    bc2 = 1.0 - ADAM_B2 ** ADAM_STEP

    def body(i, nt, w, g, m, v):
        m = ADAM_B1 * m + (1.0 - ADAM_B1) * g
        v = ADAM_B2 * v + (1.0 - ADAM_B2) * (g * g)
        delta = -ADAM_LR * ((m / bc1) / (jnp.sqrt(v / bc2) + ADAM_EPS) + ADAM_WD * w)
        return delta, m, v
    return _rows(name, body, 256, [Row(w), Row(g), Row(m), Row(v)], [("row", C, F32)] * 3)


def _expm1(x):
    return jnp.tanh(0.5 * x) * (jnp.exp(x) + 1.0)


def _lru_gates(xh, i, nt, cw, cb, wa, wx, ba, bx, lam):
    xc = cb
    for j in range(4):
        xc = xc + cw[j:j + 1] * _shift(xh, j - 2, i, nt)
    xcb = xc.astype(BF16)
    gates = []
    for e in range(2):
        r = _sigmoid(jnp.dot(xcb, wa[e], preferred_element_type=F32) + ba[e:e + 1])
        ig = _sigmoid(jnp.dot(xcb, wx[e], preferred_element_type=F32) + bx[e:e + 1])
        sp = _softplus(-lam[e:e + 1])
        la = -LRU_C * r * sp
        gates.append((r, ig, sp, jnp.exp(la), jnp.sqrt(-_expm1(2.0 * la))))
    return xc, xcb, gates


def _scan2(af, uf, ab, ub, adjoint, name):
    L, W = af.shape
    tm = _tile(L, 512, 8)
    nt, nb = L // tm, tm // 8

    def blk(A, U, h, reverse, row):
        for d in (1, 2, 4):
            if reverse:
                ok, sh = row < 8 - d, 8 - d
            else:
                ok, sh = row >= d, d
            As = jnp.where(ok, pltpu.roll(A, sh, axis=0), 1.0)
            Us = jnp.where(ok, pltpu.roll(U, sh, axis=0), 0.0)
            U = A * Us + U
            A = A * As
        return A * h + U

    def kern(af_ref, uf_ref, ab_ref, ub_ref, of_ref, ob_ref, c_ref):
        @pl.when(pl.program_id(0) == 0)
        def _():
            c_ref[...] = jnp.zeros_like(c_ref)

        row = lax.broadcasted_iota(jnp.int32, (8, W), 0)
        full = lambda v: jnp.broadcast_to(v, (8, W))

        def body(j, carry):
            hF, aF, hB, aB = carry
            r0 = pl.multiple_of(j * 8, 8)
            r1 = pl.multiple_of((nb - 1 - j) * 8, 8)
            A, U = af_ref[pl.ds(r0, 8), :], uf_ref[pl.ds(r0, 8), :]
            if adjoint:
                C = jnp.where(row == 0, aF, pltpu.roll(A, 1, axis=0))
                aF = full(A[7:8])
            else:
                C = A
            H = blk(C, U, hF, False, row)
            of_ref[pl.ds(r0, 8), :] = H
            hF = full(H[7:8])
            A, U = ab_ref[pl.ds(r1, 8), :], ub_ref[pl.ds(r1, 8), :]
            if adjoint:
                C = jnp.where(row == 7, aB, pltpu.roll(A, 7, axis=0))
                aB = full(A[0:1])
            else:
                C = A
            H = blk(C, U, hB, True, row)
            ob_ref[pl.ds(r1, 8), :] = H
            hB = full(H[0:1])
            return hF, aF, hB, aB

        carry = lax.fori_loop(0, nb, body, (c_ref[0], c_ref[1], c_ref[2], c_ref[3]))
        for n in range(4):
            c_ref[n] = carry[n]

    fwd = pl.BlockSpec((tm, W), lambda i: (i, 0))
    bwd = pl.BlockSpec((tm, W), lambda i: (nt - 1 - i, 0))
    return pl.pallas_call(
        kern, name=name, grid=(nt,), in_specs=[fwd, fwd, bwd, bwd], out_specs=[fwd, bwd],
        out_shape=[jax.ShapeDtypeStruct((L, W), F32)] * 2,
        scratch_shapes=[pltpu.VMEM((4, 8, W), F32)], compiler_params=_ARB(1),
    )(af, uf, ab, ub)


def _block_diag(w):
    out = jnp.zeros((2, GROUP_W, GROUP_W), w.dtype)
    for h in range(4):
        out = out.at[:, h * 64:(h + 1) * 64, h * 64:(h + 1) * 64].set(w[:, h])
    return out.astype(BF16)


def _diag_blocks(w):
    return jnp.stack([w[:, h * 64:(h + 1) * 64, h * 64:(h + 1) * 64] for h in range(4)], axis=1)


def _lru_params(W, l):
    return [Full(W["lru_conv_w"][l]), Full(W["lru_conv_b"][l][None]), Full(_block_diag(W["lru_w_a"][l])),
            Full(_block_diag(W["lru_w_x"][l])), Full(W["lru_b_a"][l]), Full(W["lru_b_x"][l]), Full(W["lru_lambda"][l])]


def _lru_fwd(p, W, l):
    def pre(i, nt, xh, *prm):
        xc, _, g = _lru_gates(xh, i, nt, *prm)
        return g[0][3], g[0][4] * (g[0][1] * xc), g[1][3], g[1][4] * (g[1][1] * xc)

    a0, u0, a1, u1 = _rows("lru_pre", pre, 256, [Row(p, GROUP_W, 7, halo=True)] + _lru_params(W, l),
                           [("row", GROUP_W, F32)] * 4)
    hf, hb = _scan2(a0, u0, a1, u1, False, "lru_scan")
    yc = _rows("lru_post", lambda i, nt, hf, hb, gc: (hf + hb) * _gelu(gc), 512,
               [Row(hf), Row(hb), Row(p, GROUP_W, 8)], [("row", GROUP_W, BF16)])[0]
    return yc, (a0, a1, hf, hb)


def _lru_bwd(dy, dy_cb, p, W, l, saved):
    a0, a1, hf, hb = saved

    def post(i, nt, dy, hf, hb, gc):
        return dy * _gelu(gc), dy * (hf + hb) * _dgelu(gc)

    dh, dgc = _rows("lru_post_bwd", post, 512, [Row(dy, GROUP_W, dy_cb), Row(hf), Row(hb), Row(p, GROUP_W, 8)],
                    [("row", GROUP_W, F32), ("row", GROUP_W, BF16)])
    gb, gf = _scan2(a1, dh, a0, dh, True, "lru_scan_adj")

    def gates_bwd(i, nt, xh, gf, gb, hfh, hbh, cw, cb, wa, wx, ba, bx, lam):
        xc, xcb, g = _lru_gates(xh, i, nt, cw, cb, wa, wx, ba, bx, lam)
        dxc = jnp.zeros_like(xc)
        dwa, dwx, dba, dbx, dlam = [], [], [], [], []
        for e, du, hprev in ((0, gf, _shift(hfh, -1, i, nt)), (1, gb, _shift(hbh, 1, i, nt))):
            r, ig, sp, a, s = g[e]
            dxc = dxc + du * s * ig
            dla = du * hprev * a - (du * ig * xc) * a * a / s
            dza = (dla * (-LRU_C) * sp) * r * (1.0 - r)
            dzx = (du * s * xc) * ig * (1.0 - ig)
            dlam.append(_colsum(dla * r) * (LRU_C * _sigmoid(-lam[e:e + 1])))
            dba.append(_colsum(dza))
            dbx.append(_colsum(dzx))
            dzab, dzxb = dza.astype(BF16), dzx.astype(BF16)
            tn = (((0,), (0,)), ((), ()))
            nt_ = (((1,), (1,)), ((), ()))
            dwa.append(lax.dot_general(xcb, dzab, tn, preferred_element_type=F32))
            dwx.append(lax.dot_general(xcb, dzxb, tn, preferred_element_type=F32))
            dxc = dxc + lax.dot_general(dzab, wa[e], nt_, preferred_element_type=F32)
            dxc = dxc + lax.dot_general(dzxb, wx[e], nt_, preferred_element_type=F32)
        cat = lambda v: jnp.concatenate(v, axis=0)
        return dxc, jnp.stack(dwa), jnp.stack(dwx), cat(dba), cat(dbx), cat(dlam)

    dxc, dwa, dwx, dba, dbx, dlam = _rows(
        "lru_gates_bwd", gates_bwd, 256,
        [Row(p, GROUP_W, 7, halo=True), Row(gf), Row(gb), Row(hf, halo=True), Row(hb, halo=True)] + _lru_params(W, l),
        [("row", GROUP_W, F32), ("acc", (2, GROUP_W, GROUP_W), F32), ("acc", (2, GROUP_W, GROUP_W), F32),
         ("acc", (2, GROUP_W), F32), ("acc", (2, GROUP_W), F32), ("acc", (2, GROUP_W), F32)])

    def conv_bwd(i, nt, dh_, xh, cw):
        dxb = jnp.zeros_like(dh_[1])
        dcw = []
        for j in range(4):
            dxb = dxb + cw[j:j + 1] * _shift(dh_, 2 - j, i, nt)
            dcw.append(_colsum(dh_[1] * _shift(xh, j - 2, i, nt)))
        return dxb, jnp.concatenate(dcw, axis=0), _colsum(dh_[1])

    dxb, dcw, dcb = _rows("lru_conv_bwd", conv_bwd, 512,
                          [Row(dxc, halo=True), Row(p, GROUP_W, 7, halo=True), Full(W["lru_conv_w"][l])],
                          [("row", GROUP_W, BF16), ("acc", (4, GROUP_W), F32), ("acc", (1, GROUP_W), F32)])
    grads = dict(lru_conv_w=dcw, lru_conv_b=dcb[0], lru_w_a=_diag_blocks(dwa), lru_w_x=_diag_blocks(dwx),
                 lru_b_a=dba, lru_b_x=dbx, lru_lambda=dlam)
    return dxb, dgc, grads


_NT = (((1,), (1,)), ((), ()))
_TN = (((0,), (0,)), ((), ()))


def _dot(a, b, dims=None):
    if dims is None:
        return jnp.dot(a, b, preferred_element_type=F32)
    return lax.dot_general(a, b, dims, preferred_element_type=F32)


def _dot_exact(a, b):
    return jnp.dot(a, b, precision=lax.Precision.HIGHEST, preferred_element_type=F32)


def _gla_gate_w(w_gate, b_gate):
    wg = jnp.zeros((LANES, 2 * GROUP_W), F32)
    for e in range(2):
        wg = wg.at[e * GLA_RANK:(e + 1) * GLA_RANK, e * GROUP_W:(e + 1) * GROUP_W].set(w_gate[e])
    return wg.astype(BF16), b_gate.reshape(1, 2 * GROUP_W)


def _gla_gates_fwd(p, wg, bg):
    def body(i, nt, z, wg, bg):
        logit = _dot(z.astype(BF16), wg) + bg
        la = -_softplus(-logit) * (1.0 / GLA_TAU)
        return la[:, :GROUP_W], la[:, GROUP_W:]
    return _rows("gla_gates", body, 512, [Row(p, LANES, Z_BLOCK), Full(wg), Full(bg)], [("row", GROUP_W, F32)] * 2)


def _gla_gates_bwd(p, dla0, dla1, wg, bg):
    def body(i, nt, z, d0, d1, wg, bg):
        zb = z.astype(BF16)
        logit = _dot(zb, wg) + bg
        dlogit = jnp.concatenate([d0, d1], axis=1) * (1.0 / GLA_TAU) * _sigmoid(-logit)
        dlb = dlogit.astype(BF16)
        return _dot(dlb, wg, _NT), _dot(zb, dlb, _TN), _colsum(dlogit)
    return _rows("gla_gates_bwd", body, 512, [Row(p, LANES, Z_BLOCK), Row(dla0), Row(dla1), Full(wg), Full(bg)],
                 [("row", LANES, BF16), ("acc", (LANES, 2 * GROUP_W), F32), ("acc", (1, 2 * GROUP_W), F32)])


def _gla_order(reverse):
    t = np.arange(GLA_CHUNK)
    m = (t[None, :] >= t[:, None]) if reverse else (t[None, :] <= t[:, None])
    return m.astype(np.float32), (32, 0) if reverse else (31, 63)


def _stack_heads(x, bd):
    return jnp.where(bd, jnp.concatenate([x] * 4, axis=0), 0.0)


def _diag_heads(r, bd):
    r = jnp.where(bd, r, 0.0)
    return r[0:64] + r[64:128] + r[128:192] + r[192:256]


def _gla_chunk_terms(q_ref, k_ref, la_ref, rows, mv, mid, last):
    b = _dot_exact(mv, la_ref[rows, :])
    bm, bl = b[mid:mid + 1], b[last:last + 1]
    qs = q_ref[rows, :] * ATT_SCALE
    k = k_ref[rows, :]
    P, N, E, Fd = jnp.exp(b - bm), jnp.exp(bm - b), jnp.exp(b), jnp.exp(bl - b)
    return (P, N, E, Fd, jnp.exp(bl)), (qs * P, k * N, qs * E, k * Fd)


def _gla_specs(L, walk_up):
    tm = _tile(L, 512, GLA_CHUNK)
    nt, nc = L // tm, tm // GLA_CHUNK
    specs = []
    for up in walk_up:
        t = (lambda i: i) if up else (lambda i: nt - 1 - i)
        specs.append(dict(
            col=lambda cb, _t=t: pl.BlockSpec((tm, GROUP_W), lambda i: (_t(i), cb)),
            row=pl.BlockSpec((tm, GROUP_W), lambda i, _t=t: (_t(i), 0)),
            state=pl.BlockSpec((nc, GROUP_W, GROUP_W), lambda i, _t=t: (_t(i), 0, 0))))
    return nt, nc, specs


def _gla_chunk_fwd(p, la0, la1, ride=None):
    L = la0.shape[0]
    nt, nc, specs = _gla_specs(L, (True, False))
    orders = [_gla_order(False), _gla_order(True)]

    def kern(q0, k0, v0, l0, q1, k1, v1, l1, m0_ref, m1_ref, bd_ref, o0, s0, o1, s1, st_ref):
        @pl.when(pl.program_id(0) == 0)
        def _():
            st_ref[...] = jnp.zeros_like(st_ref)

        bd = bd_ref[...] > 0.5
        dirs = []
        for e, (q_ref, k_ref, v_ref, la_ref, m_ref, o_ref, s_ref) in enumerate(
                ((q0, k0, v0, l0, m0_ref, o0, s0), (q1, k1, v1, l1, m1_ref, o1, s1))):
            mv = m_ref[...]
            dirs.append((q_ref, k_ref, v_ref, la_ref, mv, jnp.concatenate([mv] * 4, axis=0) > 0.5, o_ref, s_ref))

        def body(cc, carry):
            for e, (q_ref, k_ref, v_ref, la_ref, mv, keep, o_ref, s_ref) in enumerate(dirs):
                mid, last = orders[e][1]
                c = nc - 1 - cc if e else cc
                rows = pl.ds(pl.multiple_of(c * GLA_CHUNK, GLA_CHUNK), GLA_CHUNK)
                (_, _, _, _, d), (qP, kN, qE, kF) = _gla_chunk_terms(q_ref, k_ref, la_ref, rows, mv, mid, last)
                vb = v_ref[rows, :].astype(BF16)
                st = st_ref[e]
                s_ref[c] = st
                a = jnp.where(keep, _dot(_stack_heads(qP, bd).astype(BF16), kN.astype(BF16), _NT), 0.0)
                o = _diag_heads(_dot(a.astype(BF16), vb), bd) + _dot(qE.astype(BF16), st.astype(BF16), _NT)
                o_ref[rows, :] = o
                st_ref[e] = st * d + jnp.where(bd, _dot(vb, kF.astype(BF16), _TN), 0.0)
            return carry

        lax.fori_loop(0, nc, body, 0)

    const = lambda shp: pl.BlockSpec(shp, lambda i: (0, 0))
    in_specs, out_specs = [], []
    for sp in specs:
        in_specs += [sp["col"](0), sp["col"](1), sp["col"](2), sp["row"]]
        out_specs += [sp["row"], sp["state"]]
    return _call(
        kern, (p, p, p, la0, p, p, p, la1, jnp.asarray(orders[0][0]), jnp.asarray(orders[1][0]),
               _block_ones(GROUP_W, HEAD_DIM)),
        ride, lambda: (pl.program_id(0) == 0, pl.program_id(0) == nt - 1), name="gla_fwd", grid=(nt,),
        in_specs=in_specs + [const((GLA_CHUNK, GLA_CHUNK))] * 2 + [const((GROUP_W, GROUP_W))], out_specs=out_specs,
        out_shape=[jax.ShapeDtypeStruct((L, GROUP_W), F32),
                   jax.ShapeDtypeStruct((L // GLA_CHUNK, GROUP_W, GROUP_W), F32)] * 2,
        scratch_shapes=[pltpu.VMEM((2, GROUP_W, GROUP_W), F32)])


def _gla_chunk_bwd(p, la0, la1, do, sprev0, sprev1, ride=None):
    L = la0.shape[0]
    nt, nc, specs = _gla_specs(L, (False, True))
    orders = [_gla_order(False), _gla_order(True)]

    def kern(q0, k0, v0, l0, do0, s0, q1, k1, v1, l1, do1, s1, m0_ref, m1_ref, t0_ref, t1_ref, bd_ref, *rest):
        outs, dst_ref = (rest[0:4], rest[4:8]), rest[8]

        @pl.when(pl.program_id(0) == 0)
        def _():
            dst_ref[...] = jnp.zeros_like(dst_ref)

        bd = bd_ref[...] > 0.5
        row = lax.broadcasted_iota(jnp.int32, (GLA_CHUNK, GROUP_W), 0)
        dirs = []
        for ins, m_ref, t_ref in (((q0, k0, v0, l0, do0, s0), m0_ref, t0_ref), ((q1, k1, v1, l1, do1, s1), m1_ref, t1_ref)):
            mv = m_ref[...]
            dirs.append(ins + (mv, t_ref[...], jnp.concatenate([mv] * 4, axis=0) > 0.5))

        def body(cc, carry):
            for e, (q_ref, k_ref, v_ref, la_ref, do_ref, s_ref, mv, mt, keep) in enumerate(dirs):
                dq_ref, dk_ref, dv_ref, dla_ref = outs[e]
                mid, last = orders[e][1]
                c = cc if e else nc - 1 - cc
                rows = pl.ds(pl.multiple_of(c * GLA_CHUNK, GLA_CHUNK), GLA_CHUNK)
                (P, N, E, Fd, d), (qP, kN, qE, kF) = _gla_chunk_terms(q_ref, k_ref, la_ref, rows, mv, mid, last)
                vb = v_ref[rows, :].astype(BF16)
                dov = do_ref[rows, :]
                dob = dov.astype(BF16)
                st, dst = s_ref[c], dst_ref[e]
                stb, dstb = st.astype(BF16), dst.astype(BF16)
                qst = _stack_heads(qP, bd).astype(BF16)
                dost = _stack_heads(dov, bd).astype(BF16)
                kNb, kFb, qEb = kN.astype(BF16), kF.astype(BF16), qE.astype(BF16)
                a = jnp.where(keep, _dot(qst, kNb, _NT), 0.0).astype(BF16)
                da = jnp.where(keep, _dot(dost, vb, _NT), 0.0).astype(BF16)
                dv = _dot(a, dost, _TN) + _dot(kFb, dstb, _NT)
                dqP = _diag_heads(_dot(da, kNb), bd)
                dkN = _dot(da, qst, _TN)
                dqE = _dot(dob, stb)
                dkF = _dot(vb, dstb)
                dd = _colsum(dst * st)
                dst_ref[e] = jnp.where(bd, _dot(dob, qEb, _TN), 0.0) + dst * d
                tP, tN, tE, tF = dqP * qP, dkN * kN, dqE * qE, dkF * kF
                db = tP - tN + tE - tF
                db = db + jnp.where(row == mid, _colsum(tN - tP), 0.0) + jnp.where(row == last, _colsum(tF) + dd * d, 0.0)
                dla_ref[rows, :] = _dot_exact(mt, db)
                dq_ref[rows, :] = (dqP * P + dqE * E) * ATT_SCALE
                dk_ref[rows, :] = dkN * N + dkF * Fd
                dv_ref[rows, :] = dv
            return carry

        lax.fori_loop(0, nc, body, 0)

    const = lambda shp: pl.BlockSpec(shp, lambda i: (0, 0))
    in_specs, out_specs = [], []
    for sp in specs:
        in_specs += [sp["col"](0), sp["col"](1), sp["col"](2), sp["row"], sp["row"], sp["state"]]
        out_specs += [sp["row"]] * 4
    m0, m1 = orders[0][0], orders[1][0]
    return _call(
        kern, (p, p, p, la0, do, sprev0, p, p, p, la1, do, sprev1, jnp.asarray(m0), jnp.asarray(m1),
               jnp.asarray(m0.T.copy()), jnp.asarray(m1.T.copy()), _block_ones(GROUP_W, HEAD_DIM)),
        ride, lambda: (pl.program_id(0) == 0, pl.program_id(0) == nt - 1), name="gla_bwd", grid=(nt,),
        in_specs=in_specs + [const((GLA_CHUNK, GLA_CHUNK))] * 4 + [const((GROUP_W, GROUP_W))], out_specs=out_specs,
        out_shape=[jax.ShapeDtypeStruct((L, GROUP_W), F32)] * 8,
        scratch_shapes=[pltpu.VMEM((2, GROUP_W, GROUP_W), F32)])


def _gla_fwd(p, W, l, ride=None):
    wg, bg = _gla_gate_w(W["gla_w_gate"][l], W["gla_b_gate"][l])
    la0, la1 = _gla_gates_fwd(p, wg, bg)
    (of, s0, ob, s1), got = _gla_chunk_fwd(p, la0, la1, ride)

    def post(i, nt, of, ob, g, ng, bd):
        o = of + ob
        r = lax.rsqrt(_head_sum(o * o, bd) * (1.0 / HEAD_DIM) + EPS)
        return o * r * ng * _silu(g)

    ya = _rows("gla_post", post, 512, [Row(of), Row(ob), Row(p, GROUP_W, 3), Full(W["gla_norm"][l][None]),
                                       Full(_block_ones(GROUP_W, HEAD_DIM))], [("row", GROUP_W, BF16)])[0]
    return ya, (la0, la1, of, ob, s0, s1), got


def _gla_bwd(dy, dy_cb, p, W, l, saved, ride=None):
    la0, la1, of, ob, s0, s1 = saved
    wg, bg = _gla_gate_w(W["gla_w_gate"][l], W["gla_b_gate"][l])

    def post(i, nt, dy, of, ob, g, ng, bd):
        o = of + ob
        r = lax.rsqrt(_head_sum(o * o, bd) * (1.0 / HEAD_DIM) + EPS)
        oh = o * r
        don = dy * _silu(g)
        doh = don * ng
        do = r * (doh - oh * _head_sum(doh * oh, bd) * (1.0 / HEAD_DIM))
        return do, dy * (oh * ng) * _dsilu(g), _colsum(don * oh)

    do, dg, dng = _rows("gla_post_bwd", post, 512,
                        [Row(dy, GROUP_W, dy_cb), Row(of), Row(ob), Row(p, GROUP_W, 3), Full(W["gla_norm"][l][None]),
                         Full(_block_ones(GROUP_W, HEAD_DIM))],
                        [("row", GROUP_W, F32), ("row", GROUP_W, BF16), ("acc", (1, GROUP_W), F32)])
    (dq0, dk0, dv0, dla0, dq1, dk1, dv1, dla1), got = _gla_chunk_bwd(p, la0, la1, do, s0, s1, ride)
    dq, dk, dv = _rows("gla_sum_bwd", lambda i, nt, a0, a1, b0, b1, c0, c1: (a0 + a1, b0 + b1, c0 + c1), 512,
                       [Row(t) for t in (dq0, dq1, dk0, dk1, dv0, dv1)], [("row", GROUP_W, BF16)] * 3)
    dz, dwg, dbg = _gla_gates_bwd(p, dla0, dla1, wg, bg)
    dw_gate = jnp.stack([dwg[e * GLA_RANK:(e + 1) * GLA_RANK, e * GROUP_W:(e + 1) * GROUP_W] for e in range(2)])
    grads = dict(gla_w_gate=dw_gate, gla_b_gate=dbg.reshape(2, GROUP_W), gla_norm=dng[0])
    return (dq, dk, dv, dg, dz), grads, got


def _rope_tables(L):
    pos = jnp.arange(L, dtype=F32)
    inv_freq = ROPE_THETA ** (-jnp.arange(0, HEAD_DIM, 2, dtype=F32) / HEAD_DIM)
    ang = pos[:, None] * inv_freq[None, :]
    cos, sin = jnp.cos(ang), jnp.sin(ang)
    return jnp.tile(jnp.concatenate([cos, cos], axis=1), (1, 4)), jnp.tile(jnp.concatenate([-sin, sin], axis=1), (1, 4))


def _swap_halves(t):
    lane = lax.broadcasted_iota(jnp.int32, t.shape, 1)
    first = (lane & (HEAD_DIM - 1)) < HEAD_DIM // 2
    return jnp.where(first, pltpu.roll(t, GROUP_W - HEAD_DIM // 2, axis=1), pltpu.roll(t, HEAD_DIM // 2, axis=1))


def _attn_prep(p, cosf, sinf):
    def body(i, nt, qb, kb, vb, qd, kd, vd, c, s):
        return qb, kb, vb, qd * c + _swap_halves(qd) * s, kd * c + _swap_halves(kd) * s, vd
    ins = [Row(p, GROUP_W, cb) for cb in (4, 5, 6, 9, 10, 11)] + [Row(cosf), Row(sinf)]
    return _rows("attn_prep", body, 512, ins, [("row", GROUP_W, BF16)] * 6)


def _na_onehot():
    c = np.arange(GRID_W)
    dc = np.clip(c[None, :] - c[:, None], -(NA_COLS - 1), NA_COLS - 1) + NA_COLS - 1
    oh = np.zeros((LANES, GRID_W * GRID_W), np.float32)
    oh[dc.reshape(-1), np.arange(GRID_W * GRID_W)] = 1.0
    return jnp.asarray(oh)


def _na_colmask():
    c = np.arange(GRID_W)
    start = np.clip(c - NA_COLS // 2, 0, GRID_W - NA_COLS)
    ok = (c[None, :] >= start[:, None]) & (c[None, :] < start[:, None] + NA_COLS)
    return jnp.asarray(np.where(ok, 0.0, NEG).astype(np.float32))


N_DR = 2 * NA_ROWS - 1


def _na_bias(rpb):
    rp = jnp.zeros((GRID_W, LANES), F32).at[:4 * N_DR, :2 * NA_COLS - 1].set(rpb.reshape(4 * N_DR, 2 * NA_COLS - 1))

    def expand(r_ref, oh_ref, o_ref):
        o_ref[...] = _dot_exact(r_ref[...], oh_ref[...])

    r = pl.pallas_call(expand, name="na_bias_expand",
                       out_shape=jax.ShapeDtypeStruct((GRID_W, GRID_W * GRID_W), F32))(rp, _na_onehot())
    r = r[:4 * N_DR].reshape(4, N_DR, GRID_W, GRID_W)

    def build(r_ref, m_ref, o_ref):
        for h in range(4):
            for c in range(NA_ROWS):
                for i in range(NA_ROWS):
                    o_ref[h, c, :, i * GRID_W:(i + 1) * GRID_W] = r_ref[h, i - c + NA_ROWS - 1] + m_ref[...]

    return pl.pallas_call(build, name="na_bias_build",
                          out_shape=jax.ShapeDtypeStruct((4, NA_ROWS, GRID_W, NA_ROWS * GRID_W), F32))(r, _na_colmask())


def _na_bias_bwd(dbias):
    def fold(d_ref, o_ref):
        for h in range(4):
            for a in range(N_DR):
                acc = jnp.zeros((GRID_W, GRID_W), F32)
                for c in range(NA_ROWS):
                    i = a + c - (NA_ROWS - 1)
                    if 0 <= i < NA_ROWS:
                        acc = acc + d_ref[h, c, :, i * GRID_W:(i + 1) * GRID_W]
                o_ref[h, a] = acc

    dr = pl.pallas_call(fold, name="na_bias_fold",
                        out_shape=jax.ShapeDtypeStruct((4, N_DR, GRID_W, GRID_W), F32))(dbias)
    dr = jnp.zeros((GRID_W, GRID_W * GRID_W), F32).at[:4 * N_DR].set(dr.reshape(4 * N_DR, GRID_W * GRID_W))

    def contract(d_ref, oh_ref, o_ref):
        o_ref[...] = lax.dot_general(d_ref[...], oh_ref[...], _NT, precision=lax.Precision.HIGHEST,
                                     preferred_element_type=F32)

    g = pl.pallas_call(contract, name="na_bias_contract",
                       out_shape=jax.ShapeDtypeStruct((GRID_W, LANES), F32))(dr, _na_onehot())
    return g[:4 * N_DR, :2 * NA_COLS - 1].reshape(4, N_DR, 2 * NA_COLS - 1)


def _na_window(r, n_rows):
    rs = jnp.clip(r - NA_ROWS // 2, 0, n_rows - NA_ROWS)
    return rs, r - rs


def _na_edges(nt):
    j, i = pl.program_id(0), pl.program_id(1)
    return jnp.logical_and(j == 0, i == 0), jnp.logical_and(j == 1, i == nt - 1)


def _na_fwd(q, k, v, bias, ride=None):
    L = q.shape[0]
    n_rows = L // GRID_W
    tm = _tile(L, 512, GRID_W)
    nt, nr = L // tm, tm // GRID_W
    win = NA_ROWS * GRID_W

    def kern(q_ref, k_ref, v_ref, b_ref, o_ref):
        i = pl.program_id(1)
        lane = lax.broadcasted_iota(jnp.int32, (GRID_W, LANES), 1)

        def body(rr, carry):
            rs, c = _na_window(i * nr + rr, n_rows)
            rows = pl.ds(pl.multiple_of(rr * GRID_W, GRID_W), GRID_W)
            wrows = pl.ds(pl.multiple_of(rs * GRID_W, GRID_W), win)
            qv, kw, vw = q_ref[rows, :], k_ref[wrows, :], v_ref[wrows, :]
            o = jnp.zeros((GRID_W, LANES), F32)
            for hh in range(2):
                mine = (lane < HEAD_DIM) == (hh == 0)
                qm = jnp.where(mine, qv, jnp.zeros_like(qv))
                s = _dot(qm, kw, _NT) * ATT_SCALE + b_ref[hh, c]
                e = jnp.exp(s - jnp.max(s, axis=-1, keepdims=True))
                pn = (e / jnp.sum(e, axis=-1, keepdims=True)).astype(BF16)
                o = jnp.where(mine, _dot(pn, vw), o)
            o_ref[rows, :] = o.astype(BF16)
            return carry

        lax.fori_loop(0, nr, body, 0, unroll=4)

    qspec = pl.BlockSpec((tm, LANES), lambda j, i: (i, j))
    kvspec = pl.BlockSpec((L, LANES), lambda j, i: (0, j))
    (y,), got = _call(
        kern, (q, k, v, bias), ride, lambda: _na_edges(nt), name="na_fwd", grid=(2, nt),
        in_specs=[qspec, kvspec, kvspec, pl.BlockSpec((2, NA_ROWS, GRID_W, win), lambda j, i: (j, 0, 0, 0))],
        out_specs=[qspec], out_shape=[jax.ShapeDtypeStruct((L, GROUP_W), BF16)], scratch_shapes=[])
    return y, got


def _na_bwd(dy, dy_block, q, k, v, bias, ride=None):
    L = q.shape[0]
    n_rows = L // GRID_W
    tm = _tile(L, 512, GRID_W)
    nt, nr = L // tm, tm // GRID_W
    win = NA_ROWS * GRID_W

    def kern(dy_ref, q_ref, k_ref, v_ref, b_ref, dq_ref, dk_ref, dv_ref, db_ref):
        i = pl.program_id(1)

        @pl.when(i == 0)
        def _():
            dk_ref[...] = jnp.zeros_like(dk_ref)
            dv_ref[...] = jnp.zeros_like(dv_ref)
            db_ref[...] = jnp.zeros_like(db_ref)

        lane = lax.broadcasted_iota(jnp.int32, (GRID_W, LANES), 1)

        def body(rr, carry):
            rs, c = _na_window(i * nr + rr, n_rows)
            rows = pl.ds(pl.multiple_of(rr * GRID_W, GRID_W), GRID_W)
            wrows = pl.ds(pl.multiple_of(rs * GRID_W, GRID_W), win)
            qv, kw, vw = q_ref[rows, :], k_ref[wrows, :], v_ref[wrows, :]
            dyv = dy_ref[rows, :].astype(BF16)
            dq = jnp.zeros((GRID_W, LANES), F32)
            dkw = jnp.zeros((win, LANES), F32)
            dvw = jnp.zeros((win, LANES), F32)
            for hh in range(2):
                mine = (lane < HEAD_DIM) == (hh == 0)
                qm = jnp.where(mine, qv, jnp.zeros_like(qv))
                dom = jnp.where(mine, dyv, jnp.zeros_like(dyv))
                s = _dot(qm, kw, _NT) * ATT_SCALE + b_ref[hh, c]
                e = jnp.exp(s - jnp.max(s, axis=-1, keepdims=True))
                pn = e / jnp.sum(e, axis=-1, keepdims=True)
                dp = _dot(dom, vw, _NT)
                ds = pn * (dp - jnp.sum(pn * dp, axis=-1, keepdims=True))
                db_ref[hh, c] += ds
                dsb = ds.astype(BF16)
                dq = jnp.where(mine, _dot(dsb, kw) * ATT_SCALE, dq)
                dkw = dkw + _dot(dsb, qm, _TN) * ATT_SCALE
                dvw = dvw + _dot(pn.astype(BF16), dom, _TN)
            dq_ref[rows, :] = dq.astype(BF16)
            dk_ref[wrows, :] += dkw
            dv_ref[wrows, :] += dvw
            return carry

        lax.fori_loop(0, nr, body, 0, unroll=2)

    qspec = pl.BlockSpec((tm, LANES), lambda j, i: (i, j))
    kvspec = pl.BlockSpec((L, LANES), lambda j, i: (0, j))
    bspec = pl.BlockSpec((2, NA_ROWS, GRID_W, win), lambda j, i: (j, 0, 0, 0))
    return _call(
        kern, (dy, q, k, v, bias), ride, lambda: _na_edges(nt), name="na_bwd", grid=(2, nt),
        in_specs=[pl.BlockSpec((tm, LANES), lambda j, i: (i, dy_block + j)), qspec, kvspec, kvspec, bspec],
        out_specs=[qspec, kvspec, kvspec, bspec],
        out_shape=[jax.ShapeDtypeStruct((L, GROUP_W), BF16), jax.ShapeDtypeStruct((L, GROUP_W), F32),
                   jax.ShapeDtypeStruct((L, GROUP_W), F32),
                   jax.ShapeDtypeStruct((4, NA_ROWS, GRID_W, win), F32)], scratch_shapes=[])


def _dil_specs(n, tq):
    R = DIL_RADIUS
    step, nb = tq // R, n // R
    main = pl.BlockSpec((tq, LANES), lambda j, i: (i, j))
    prev = pl.BlockSpec((R, LANES), lambda j, i: (jnp.maximum(i * step - 1, 0), j))
    nxt = pl.BlockSpec((R, LANES), lambda j, i: (jnp.minimum((i + 1) * step, nb - 1), j))
    return main, prev, nxt


def _dil_valid(i, tq, n):
    R = DIL_RADIUS
    row = lax.broadcasted_iota(jnp.int32, (tq, tq + 2 * R), 0)
    col = lax.broadcasted_iota(jnp.int32, (tq, tq + 2 * R), 1)
    kpos = i * tq - R + col
    return (jnp.abs(col - R - row) <= R) & (kpos >= 0) & (kpos < n)


def _dil_fwd(q, k, v, dil):
    L = q.shape[0]
    n = L // dil
    tq = _tile(n, 256, DIL_RADIUS)
    view = lambda t: t.reshape(n, dil * GROUP_W)

    def kern(q_ref, kp_ref, k_ref, kn_ref, vp_ref, v_ref, vn_ref, o_ref, l_ref):
        i = pl.program_id(1)
        valid = _dil_valid(i, tq, n)
        qv = q_ref[...]
        ka = jnp.concatenate([kp_ref[...], k_ref[...], kn_ref[...]], axis=0)
        va = jnp.concatenate([vp_ref[...], v_ref[...], vn_ref[...]], axis=0)
        lane = lax.broadcasted_iota(jnp.int32, (tq, LANES), 1)
        o = jnp.zeros((tq, LANES), F32)
        lse = jnp.zeros((tq, LANES), F32)
        for hh in range(2):
            mine = (lane < HEAD_DIM) == (hh == 0)
            qm = jnp.where(mine, qv, jnp.zeros_like(qv))
            s = jnp.where(valid, _dot(qm, ka, _NT) * ATT_SCALE, NEG)
            m = jnp.max(s, axis=-1, keepdims=True)
            e = jnp.exp(s - m)
            den = jnp.sum(e, axis=-1, keepdims=True)
            o = jnp.where(mine, _dot((e / den).astype(BF16), va), o)
            lse = jnp.where(mine, m + jnp.log(den), lse)
        o_ref[...] = o
        l_ref[...] = lse

    main, prev, nxt = _dil_specs(n, tq)
    o, lse = pl.pallas_call(
        kern, name=f"dil_fwd_{dil}", grid=(2 * dil, n // tq),
        in_specs=[main, prev, main, nxt, prev, main, nxt], out_specs=[main, main],
        out_shape=[jax.ShapeDtypeStruct((n, dil * GROUP_W), F32)] * 2, compiler_params=_ARB(2),
    )(view(q), view(k), view(k), view(k), view(v), view(v), view(v))
    return o.reshape(L, GROUP_W), lse.reshape(L, GROUP_W)


def _dil_bwd(q, k, v, do, lse, dterm, dil):
    L = q.shape[0]
    n = L // dil
    R = DIL_RADIUS
    tq = _tile(n, 256, R)
    nq = n // tq
    view = lambda t: t.reshape(n, dil * GROUP_W)

    def kern(q_ref, kp_ref, k_ref, kn_ref, vp_ref, v_ref, vn_ref, do_ref, l_ref, dt_ref, dq_ref, dk_ref, dv_ref):
        i = pl.program_id(1)

        @pl.when(i == 0)
        def _():
            dk_ref[...] = jnp.zeros_like(dk_ref)
            dv_ref[...] = jnp.zeros_like(dv_ref)

        valid = _dil_valid(i, tq, n)
        qv, dov = q_ref[...], do_ref[...]
        ka = jnp.concatenate([kp_ref[...], k_ref[...], kn_ref[...]], axis=0)
        va = jnp.concatenate([vp_ref[...], v_ref[...], vn_ref[...]], axis=0)
        lv, dtv = l_ref[...], dt_ref[...]
        lane = lax.broadcasted_iota(jnp.int32, (tq, LANES), 1)
        dq = jnp.zeros((tq, LANES), F32)
        dka = jnp.zeros((tq + 2 * R, LANES), F32)
        dva = jnp.zeros((tq + 2 * R, LANES), F32)
        for hh in range(2):
            mine = (lane < HEAD_DIM) == (hh == 0)
            qm = jnp.where(mine, qv, jnp.zeros_like(qv))
            dom = jnp.where(mine, dov, jnp.zeros_like(dov))
            c0 = hh * HEAD_DIM
            s = _dot(qm, ka, _NT) * ATT_SCALE
            pn = jnp.where(valid, jnp.exp(s - lv[:, c0:c0 + 1]), 0.0)
            ds = pn * (_dot(dom, va, _NT) - dtv[:, c0:c0 + 1])
            dsb = ds.astype(BF16)
            dq = jnp.where(mine, _dot(dsb, ka) * ATT_SCALE, dq)
            dka = dka + _dot(dsb, qm, _TN) * ATT_SCALE
            dva = dva + _dot(pn.astype(BF16), dom, _TN)
        dq_ref[...] = dq
        r0 = pl.multiple_of(i * tq, R)
        dk_ref[pl.ds(r0, tq), :] += dka[R:R + tq]
        dv_ref[pl.ds(r0, tq), :] += dva[R:R + tq]

        @pl.when(i > 0)
        def _():
            dk_ref[pl.ds(r0 - R, R), :] += dka[:R]
            dv_ref[pl.ds(r0 - R, R), :] += dva[:R]

        @pl.when(i < nq - 1)
        def _():
            dk_ref[pl.ds(r0 + tq, R), :] += dka[R + tq:]
            dv_ref[pl.ds(r0 + tq, R), :] += dva[R + tq:]

    main, prev, nxt = _dil_specs(n, tq)
    whole = pl.BlockSpec((n, LANES), lambda j, i: (0, j))
    shp = jax.ShapeDtypeStruct((n, dil * GROUP_W), F32)
    dq, dk, dv = pl.pallas_call(
        kern, name=f"dil_bwd_{dil}", grid=(2 * dil, nq),
        in_specs=[main, prev, main, nxt, prev, main, nxt, main, main, main], out_specs=[main, whole, whole],
        out_shape=[shp] * 3, compiler_params=_ARB(2),
    )(view(q), view(k), view(k), view(k), view(v), view(v), view(v), view(do), view(lse), view(dterm))
    return dq.reshape(L, GROUP_W), dk.reshape(L, GROUP_W), dv.reshape(L, GROUP_W)


def _dil_weights(lses):
    m = jnp.maximum(jnp.maximum(lses[0], lses[1]), lses[2])
    e = [jnp.exp(l - m) for l in lses]
    tot = e[0] + e[1] + e[2]
    return [x / tot for x in e]


def _dilated_fwd(q, k, v):
    res = [_dil_fwd(q, k, v, dil) for _, dil in DIL_PAIRS]

    def body(i, nt, o0, o1, o2, l0, l1, l2):
        w = _dil_weights((l0, l1, l2))
        return w[0] * o0 + w[1] * o1 + w[2] * o2

    ins = [Row(r[0]) for r in res] + [Row(r[1]) for r in res]
    return _rows("dil_combine", body, 512, ins, [("row", GROUP_W, BF16)])[0], res


def _dilated_bwd(dy, dy_cb, q, k, v, saved, cosf, sinf):
    def split(i, nt, dy, o0, o1, o2, l0, l1, l2, bd):
        w = _dil_weights((l0, l1, l2))
        y = w[0] * o0 + w[1] * o1 + w[2] * o2
        dyy = _head_sum(dy * y, bd)
        return tuple(wg * dy for wg in w) + tuple(wg * dyy for wg in w)

    ins = [Row(dy, GROUP_W, dy_cb)] + [Row(r[0]) for r in saved] + [Row(r[1]) for r in saved]
    outs = _rows("dil_split_bwd", split, 512, ins + [Full(_block_ones(GROUP_W, HEAD_DIM))],
                 [("row", GROUP_W, BF16)] * 3 + [("row", GROUP_W, F32)] * 3)
    g = [_dil_bwd(q, k, v, outs[b], saved[b][1], outs[3 + b], dil) for b, (_, dil) in enumerate(DIL_PAIRS)]

    def finish(i, nt, q0, q1, q2, k0, k1, k2, v0, v1, v2, c, s):
        dq, dk = q0 + q1 + q2, k0 + k1 + k2
        return dq * c + _swap_halves(dq * s), dk * c + _swap_halves(dk * s), v0 + v1 + v2

    ins = [Row(g[b][t]) for t in range(3) for b in range(3)] + [Row(cosf), Row(sinf)]
    return _rows("dil_finish_bwd", finish, 512, ins, [("row", GROUP_W, BF16)] * 3)


def _layer_fwd(x, W, l, cosf, sinf, ride_a=None, ride_b=None):
    h1 = _rms_fwd(x, W["mix_norm_pre"][l][None], "mix_norm")
    p = _mm(h1, W["w_in"][l], "nn", F32, "proj_in")
    ya, sa, got_a = _gla_fwd(p, W, l, ride_a)
    qb, kb, vb, qd, kd, vd = _attn_prep(p, cosf, sinf)
    bias = _na_bias(W["na_rpb"][l])
    yb, got_b = _na_fwd(qb, kb, vb, bias, ride_b)
    yc, sc = _lru_fwd(p, W, l)
    yd, sd = _dilated_fwd(qd, kd, vd)
    ycat = jnp.concatenate([ya, yb, yc, yd], axis=1)
    ymix = _mm(ycat, W["w_out"][l], "nn", F32, "proj_out", tm=1024)
    xm = _rms_resid_fwd(x, ymix, W["mix_norm_post"][l][None], "mix_resid")
    h2 = _rms_fwd(xm, W["ffn_norm_pre"][l][None], "ffn_norm")
    gu, act = _ffn_in_swiglu(h2, W["ffn_w_in"][l])
    f = _mm(act, W["ffn_w_out"][l], "nn", F32, "ffn_out")
    xo = _rms_resid_fwd(xm, f, W["ffn_norm_post"][l][None], "ffn_resid")
    saved = dict(x=x, h1=h1, p=p, sa=sa, att=(qb, kb, vb, qd, kd, vd), bias=bias, sc=sc, sd=sd, ycat=ycat, ymix=ymix,
                 xm=xm, h2=h2, gu=gu, act=act, f=f)
    return xo, saved, got_a, got_b


def _layer_bwd(dxo, W, l, S, cosf, sinf, ride_a=None, ride_b=None):
    g = {}
    df, g["ffn_norm_post"] = _rms_bwd(dxo, S["f"], W["ffn_norm_post"][l][None], "ffn_resid_bwd", out_dtype=BF16)
    g["ffn_w_out"] = _mm(S["act"], df, "tn", BF16, "ffn_out_dw", tm=256, tk=4096)
    dgu = _ffn_out_dx_swiglu(df, W["ffn_w_out"][l], S["gu"])
    dh2 = _mm(dgu, W["ffn_w_in"][l], "nt", F32, "ffn_in_dx")
    g["ffn_w_in"] = _mm(S["h2"], dgu, "tn", BF16, "ffn_in_dw", tm=1024, tn=512, tk=4096)
    dxm, g["ffn_norm_pre"] = _rms_bwd(dh2, S["xm"], W["ffn_norm_pre"][l][None], "ffn_norm_bwd", resid=dxo)
    dymix, g["mix_norm_post"] = _rms_bwd(dxm, S["ymix"], W["mix_norm_post"][l][None], "mix_resid_bwd", out_dtype=BF16)
    dycat = _mm(dymix, W["w_out"][l], "nt", F32, "proj_out_dx", tm=1024)
    g["w_out"] = _mm(S["ycat"], dymix, "tn", BF16, "proj_out_dw", tm=1024, tn=512, tk=4096)
    p = S["p"]
    qb, kb, vb, qd, kd, vd = S["att"]
    (dqa, dka, dva, dga, dz), ga, got_a = _gla_bwd(dycat, 0, p, W, l, S["sa"], ride_a)
    (dqb, dkb, dvb, dbias), got_b = _na_bwd(dycat, 2, qb, kb, vb, S["bias"], ride_b)
    g["na_rpb"] = _na_bias_bwd(dbias)
    dxc, dgc, gc = _lru_bwd(dycat, 2, p, W, l, S["sc"])
    dqd, dkd, dvd = _dilated_bwd(dycat, 3, qd, kd, vd, S["sd"], cosf, sinf)
    g.update(ga)
    g.update(gc)
    dp = jnp.concatenate([dqa, dka, dva, dga, dqb, dkb.astype(BF16), dvb.astype(BF16), dxc, dgc, dqd, dkd, dvd, dz], axis=1)
    dh1 = _mm(dp, W["w_in"][l], "nt", F32, "proj_in_dx")
    g["w_in"] = _mm(S["h1"], dp, "tn", BF16, "proj_in_dw", tm=1024, tn=640, tk=4096)
    dx, g["mix_norm_pre"] = _rms_bwd(dh1, S["x"], W["mix_norm_pre"][l][None], "mix_norm_bwd", resid=dxm)
    for n in ("ffn_norm_post", "ffn_norm_pre", "mix_norm_post", "mix_norm_pre"):
        g[n] = g[n][0]
    return dx, g, got_a, got_b


MESH_AXES = ("x", "y", "c")


class Xfer:
    def __init__(self, arr, kind):
        self.arr, self.kind = arr, kind
        shp = arr.shape
        if kind == "all":
            self.out = (N_DEV,) + shp
        elif kind == "slot":
            self.out = shp
        elif kind == "rows":
            self.r = shp[1] // N_DEV
            self.out = (N_DEV, shp[0], self.r, shp[2])
        else:
            self.r = shp[1]
            self.out = (shp[0], N_DEV * shp[1], shp[2])

    def src(self, ref, peer):
        if self.kind == "slot":
            return ref.at[peer]
        if self.kind == "rows":
            return ref.at[:, pl.ds(peer * self.r, self.r), :]
        return ref

    def dst(self, ref, me):
        if self.kind == "place":
            return ref.at[:, pl.ds(me * self.r, self.r), :]
        return ref.at[me]


class Exchange:
    def __init__(self, items):
        n = len(items)
        self.items = items
        self.arrays = [it.arr for it in items]
        self.specs = [pl.BlockSpec(memory_space=pl.ANY)] * n
        self.out_shape = [jax.ShapeDtypeStruct(it.out, it.arr.dtype) for it in items]
        self.scratch = [pltpu.SemaphoreType.DMA((n * (N_DEV - 1),)), pltpu.SemaphoreType.DMA((n * (N_DEV - 1),)),
                        pltpu.SemaphoreType.DMA((n,))]

    def copies(self, ins, outs, sems):
        send_sems, recv_sems, local_sems = sems
        x, y, c = (lax.axis_index(a) for a in MESH_AXES)
        me = 4 * x + 2 * y + c
        out = []
        for t, it in enumerate(self.items):
            out.append(pltpu.make_async_copy(it.src(ins[t], me), it.dst(outs[t], me), local_sems.at[t]))
            for k in range(1, N_DEV):
                px, py, pc = x ^ ((k >> 2) & 1), y ^ ((k >> 1) & 1), c ^ (k & 1)
                s = t * (N_DEV - 1) + k - 1
                out.append(pltpu.make_async_remote_copy(
                    src_ref=it.src(ins[t], 4 * px + 2 * py + pc), dst_ref=it.dst(outs[t], me),
                    send_sem=send_sems.at[s], recv_sem=recv_sems.at[s], device_id=(px, py, pc),
                    device_id_type=pl.DeviceIdType.MESH))
        return out

    def start(self, ins, outs, sems):
        for cp in self.copies(ins, outs, sems):
            cp.start()

    def wait(self, ins, outs, sems):
        for cp in self.copies(ins, outs, sems):
            cp.wait()


def _exchange(items, name):
    ex = Exchange(items)
    n = len(items)

    def body(*refs):
        ex.start(refs[:n], refs[n:2 * n], refs[2 * n:])
        ex.wait(refs[:n], refs[n:2 * n], refs[2 * n:])

    return pl.pallas_call(body, name=name, out_shape=ex.out_shape, in_specs=ex.specs, out_specs=ex.specs,
                          scratch_shapes=ex.scratch)(*ex.arrays)


def _call(kern, arrays, ride, edges, *, name, grid, in_specs, out_specs, out_shape, scratch_shapes):
    params = _ARB(len(grid))
    if ride is None:
        return pl.pallas_call(kern, name=name, grid=grid, in_specs=in_specs, out_specs=out_specs, out_shape=out_shape,
                              scratch_shapes=scratch_shapes, compiler_params=params)(*arrays), None
    ni, no, ns, nx = len(in_specs), len(out_specs), len(scratch_shapes), len(ride.items)

    def wrapped(*refs):
        ins, xin = refs[:ni], refs[ni:ni + nx]
        outs, xout = refs[ni + nx:ni + nx + no], refs[ni + nx + no:ni + 2 * nx + no]
        scr, sems = refs[ni + 2 * nx + no:ni + 2 * nx + no + ns], refs[ni + 2 * nx + no + ns:]
        first, last = edges()

        @pl.when(first)
        def _():
            ride.start(xin, xout, sems)

        kern(*ins, *outs, *scr)

        @pl.when(last)
        def _():
            ride.wait(xin, xout, sems)

    res = pl.pallas_call(
        wrapped, name=name, grid=grid, in_specs=list(in_specs) + ride.specs, out_specs=list(out_specs) + ride.specs,
        out_shape=list(out_shape) + ride.out_shape, scratch_shapes=list(scratch_shapes) + ride.scratch,
        compiler_params=params)(*arrays, *ride.arrays)
    return res[:no], res[no:]


def _column_segments(width, permuted):
    z0, z1, zn = 4 * GROUP_W, 4 * GROUP_W + 2 * GLA_RANK, 12 * GROUP_W
    segs = []
    for d in range(N_DEV):
        lo, hi = d * width, (d + 1) * width
        if not permuted:
            segs.append([(0, width, lo)])
            continue
        runs = []
        for a, b, shift in ((0, z0, 0), (z0, z1, zn - z0), (z1, 10 ** 9, -(z1 - z0))):
            s, e = max(lo, a), min(hi, b)
            if s < e:
                runs.append((s - lo, e - lo, s + shift))
        segs.append(runs)
    return segs


def _cols_from_pieces(pieces, segs, cols, name):
    _, R, w = pieces.shape
    tm = _tile(R, 256, 16)
    used = max(f + (b - a) for runs in segs for a, b, f in runs)

    def kern(p_ref, o_ref):
        for d, runs in enumerate(segs):
            for a, b, f in runs:
                o_ref[:, f:f + (b - a)] = p_ref[d, :, a:b]
        if used < cols:
            o_ref[:, used:cols] = jnp.zeros((tm, cols - used), o_ref.dtype)

    return pl.pallas_call(
        kern, name=name, grid=(R // tm,), in_specs=[pl.BlockSpec((N_DEV, tm, w), lambda i: (0, i, 0))],
        out_specs=pl.BlockSpec((tm, cols), lambda i: (i, 0)), out_shape=jax.ShapeDtypeStruct((R, cols), pieces.dtype),
        compiler_params=_ARB(1),
    )(pieces)


def _pieces_from_cols(full, segs, w, name):
    R, cols = full.shape
    tm = _tile(R, 256, 16)

    def kern(f_ref, o_ref):
        for d, runs in enumerate(segs):
            for a, b, f in runs:
                o_ref[d, :, a:b] = f_ref[:, f:f + (b - a)]

    return pl.pallas_call(
        kern, name=name, grid=(R // tm,), in_specs=[pl.BlockSpec((tm, cols), lambda i: (i, 0))],
        out_specs=pl.BlockSpec((N_DEV, tm, w), lambda i: (0, i, 0)),
        out_shape=jax.ShapeDtypeStruct((N_DEV, R, w), full.dtype), compiler_params=_ARB(1),
    )(full)


def _sum_slots(recv, name):
    n, R, C = recv.shape
    tm = _tile(R, 256, 16)

    def kern(*refs):
        acc = refs[0][...].astype(F32)
        for r in refs[1:n]:
            acc = acc + r[...].astype(F32)
        refs[n][...] = acc

    return pl.pallas_call(
        kern, name=name, grid=(R // tm,),
        in_specs=[pl.BlockSpec((None, tm, C), lambda i, _s=s: (_s, i, 0)) for s in range(n)],
        out_specs=pl.BlockSpec((tm, C), lambda i: (i, 0)), out_shape=jax.ShapeDtypeStruct((R, C), F32),
        compiler_params=_ARB(1),
    )(*([recv] * n))


BIG = (("w_in", 2), ("w_out", 1), ("ffn_w_in", 2), ("ffn_w_out", 1))
SMALL_SHARDED = ("gla_w_gate", "gla_b_gate", "lru_conv_w", "lru_b_a", "lru_b_x", "lru_lambda")
REPLICATED = ("mix_norm_pre", "mix_norm_post", "gla_norm", "na_rpb", "lru_conv_b", "lru_w_a", "lru_w_x",
              "ffn_norm_pre", "ffn_norm_post")
WEIGHTS = ("mix_norm_pre", "mix_norm_post", "w_in", "gla_w_gate", "gla_b_gate", "gla_norm", "na_rpb", "lru_conv_w",
           "lru_conv_b", "lru_w_a", "lru_b_a", "lru_w_x", "lru_b_x", "lru_lambda", "w_out", "ffn_norm_pre",
           "ffn_norm_post", "ffn_w_in", "ffn_w_out")
FLAT_C = 1024


def _to_rows(vec, row_unit):
    n = vec.shape[-1]
    rows = -(-n // (FLAT_C * row_unit)) * row_unit
    pad = [(0, 0)] * (vec.ndim - 1) + [(0, rows * FLAT_C - n)]
    return jnp.pad(vec, pad).reshape(vec.shape[:-1] + (rows, FLAT_C))


def _unshard(parts, axis):
    t = jnp.moveaxis(parts, 0, axis)
    shp = list(t.shape)
    return t.reshape(shp[:axis] + [shp[axis] * shp[axis + 1]] + shp[axis + 2:])


def _shards(full, axis):
    shp = list(full.shape)
    t = full.reshape(shp[:axis] + [N_DEV, shp[axis] // N_DEV] + shp[axis + 1:])
    return jnp.moveaxis(t, axis, 0)


def _two_d(a):
    return a.reshape(-1, a.shape[-1])


def _weight_items(W, l):
    bf = lambda n: W[n][l].astype(BF16)
    return ([Xfer(bf("w_in"), "all"), Xfer(bf("w_out")[None], "place")],
            [Xfer(bf("ffn_w_in"), "all"), Xfer(bf("ffn_w_out")[None], "place")])


def _unpack_weights(full, W, got_a, got_b):
    w_in_w, ffn_w = W["w_in"].shape[-1], W["ffn_w_in"].shape[-1]
    full["w_in"].append(_cols_from_pieces(got_a[0], _column_segments(w_in_w, True), P_COLS, "unpack_w_in"))
    full["w_out"].append(got_a[1][0])
    full["ffn_w_in"].append(_cols_from_pieces(got_b[0], _column_segments(ffn_w, False), N_DEV * ffn_w, "unpack_ffn_w_in"))
    full["ffn_w_out"].append(got_b[1][0])


def _grad_items(g, W):
    w_in_w, ffn_w = W["w_in"].shape[-1], W["ffn_w_in"].shape[-1]
    p_in = _pieces_from_cols(g["w_in"], _column_segments(w_in_w, True), w_in_w, "pack_w_in")
    p_ffn = _pieces_from_cols(g["ffn_w_in"], _column_segments(ffn_w, False), ffn_w, "pack_ffn_w_in")
    return ([Xfer(p_in, "slot"), Xfer(g["w_out"][None], "rows")], [Xfer(p_ffn, "slot"), Xfer(g["ffn_w_out"][None], "rows")])


def _sum_big(got_a, got_b):
    out = {}
    for n, r in zip(("w_in", "w_out", "ffn_w_in", "ffn_w_out"), tuple(got_a) + tuple(got_b)):
        out[n] = _sum_slots(r.reshape(N_DEV, -1, r.shape[-1]), "sum_" + n)
    return out


def _train(x, target, W):
    L = x.shape[0]
    depth = W["w_in"].shape[0]
    cosf, sinf = _rope_tables(L)
    small = jnp.concatenate([W[n].reshape(-1) for n in SMALL_SHARDED])
    small16 = _to_rows(lax.bitcast_convert_type(small, jnp.uint16).reshape(-1), 16)
    items_a, items_b = _weight_items(W, 0)
    got = _exchange(items_a + items_b + [Xfer(small16, "all")], "gather_first")
    full = dict(W, w_in=[], w_out=[], ffn_w_in=[], ffn_w_out=[])
    _unpack_weights(full, W, got[0:2], got[2:4])
    sm = lax.bitcast_convert_type(got[4].reshape(N_DEV, -1)[:, :2 * small.size].reshape(N_DEV, small.size, 2), F32)
    off = 0
    for n in SMALL_SHARDED:
        full[n] = _unshard(sm[:, off:off + W[n].size].reshape((N_DEV,) + W[n].shape), W[n].ndim - 1)
        off += W[n].size

    saved = []
    for l in range(depth):
        rides = [Exchange(it) for it in _weight_items(W, l + 1)] if l + 1 < depth else [None, None]
        x, S, got_a, got_b = _layer_fwd(x, full, l, cosf, sinf, *rides)
        saved.append(S)
        if l + 1 < depth:
            _unpack_weights(full, W, got_a, got_b)
    loss, dx = _loss_fwd_bwd(x, target)

    grads, big, rides = [None] * depth, [None] * depth, [None, None]
    for l in reversed(range(depth)):
        dx, grads[l], got_a, got_b = _layer_bwd(dx, full, l, saved[l], cosf, sinf, *rides)
        if l + 1 < depth:
            big[l + 1] = _sum_big(got_a, got_b)
        rides = [Exchange(it) for it in _grad_items(grads[l], W)] if l > 0 else [None, None]
    G = {n: jnp.stack([g[n] for g in grads]) for n in SMALL_SHARDED + REPLICATED}
    small_g = jnp.concatenate([_shards(G[n], G[n].ndim - 1).reshape(N_DEV, -1) for n in SMALL_SHARDED], axis=1)
    repl_g = jnp.concatenate([G[n].reshape(-1) for n in REPLICATED])
    items_a, items_b = _grad_items(grads[0], W)
    got = _exchange(items_a + items_b + [Xfer(_to_rows(small_g, 8), "slot"), Xfer(_to_rows(repl_g, 8), "all")],
                    "exchange_last")
    big[0] = _sum_big(got[0:2], got[2:4])
    out = {n: jnp.stack([b[n] for b in big]).reshape(W[n].shape) for n, _ in BIG}
    for names, r, tag in ((SMALL_SHARDED, got[4], "sum_small"), (REPLICATED, got[5], "sum_replicated")):
        flat, off = _sum_slots(r, tag).reshape(-1), 0
        for n in names:
            out[n] = flat[off:off + W[n].size].reshape(W[n].shape)
            off += W[n].size
    return loss, dx, out


def _update(W, G, M, V):
    delta, new_m, new_v = {}, {}, {}
    for n, _ in BIG:
        two_d = lambda a: a.reshape(-1, a.shape[-1])
        d, m, v = _adamw(two_d(W[n]), two_d(G[n]), two_d(M[n]), two_d(V[n]), "adamw_" + n)
        delta[n], new_m[n], new_v[n] = (t.reshape(W[n].shape) for t in (d, m, v))
    rest = SMALL_SHARDED + REPLICATED
    pack = lambda D: _to_rows(jnp.concatenate([D[n].reshape(-1) for n in rest]), 16)
    d, m, v = _adamw(pack(W), pack(G), pack(M), pack(V), "adamw_small")
    off = 0
    for n in rest:
        sl = lambda t: t.reshape(-1)[off:off + W[n].size].reshape(W[n].shape)
        delta[n], new_m[n], new_v[n] = sl(d), sl(m), sl(v)
        off += W[n].size
    return delta, new_m, new_v


def kernel(x, mix_norm_pre, mix_norm_post, w_in, gla_w_gate, gla_b_gate, gla_norm, na_rpb, lru_conv_w, lru_conv_b, lru_w_a, lru_b_a, lru_w_x, lru_b_x, lru_lambda, w_out, ffn_norm_pre, ffn_norm_post, ffn_w_in, ffn_w_out, loss_target, m_mix_norm_pre, m_mix_norm_post, m_w_in, m_gla_w_gate, m_gla_b_gate, m_gla_norm, m_na_rpb, m_lru_conv_w, m_lru_conv_b, m_lru_w_a, m_lru_b_a, m_lru_w_x, m_lru_b_x, m_lru_lambda, m_w_out, m_ffn_norm_pre, m_ffn_norm_post, m_ffn_w_in, m_ffn_w_out, v_mix_norm_pre, v_mix_norm_post, v_w_in, v_gla_w_gate, v_gla_b_gate, v_gla_norm, v_na_rpb, v_lru_conv_w, v_lru_conv_b, v_lru_w_a, v_lru_b_a, v_lru_w_x, v_lru_b_x, v_lru_lambda, v_w_out, v_ffn_norm_pre, v_ffn_norm_post, v_ffn_w_in, v_ffn_w_out):
    W = dict(zip(WEIGHTS, (mix_norm_pre, mix_norm_post, w_in, gla_w_gate, gla_b_gate, gla_norm, na_rpb, lru_conv_w, lru_conv_b, lru_w_a, lru_b_a, lru_w_x, lru_b_x, lru_lambda, w_out, ffn_norm_pre, ffn_norm_post, ffn_w_in, ffn_w_out)))
    M = dict(zip(WEIGHTS, (m_mix_norm_pre, m_mix_norm_post, m_w_in, m_gla_w_gate, m_gla_b_gate, m_gla_norm, m_na_rpb, m_lru_conv_w, m_lru_conv_b, m_lru_w_a, m_lru_b_a, m_lru_w_x, m_lru_b_x, m_lru_lambda, m_w_out, m_ffn_norm_pre, m_ffn_norm_post, m_ffn_w_in, m_ffn_w_out)))
    V = dict(zip(WEIGHTS, (v_mix_norm_pre, v_mix_norm_post, v_w_in, v_gla_w_gate, v_gla_b_gate, v_gla_norm, v_na_rpb, v_lru_conv_w, v_lru_conv_b, v_lru_w_a, v_lru_b_a, v_lru_w_x, v_lru_b_x, v_lru_lambda, v_w_out, v_ffn_norm_pre, v_ffn_norm_post, v_ffn_w_in, v_ffn_w_out)))
    loss, dx, G = _train(x[0], loss_target[0], W)
    loss = lax.psum(loss, MESH_AXES)
    delta, new_m, new_v = _update(W, G, M, V)
    return (loss, dx[None], *[G[n] for n in WEIGHTS], *[delta[n] for n in WEIGHTS], *[new_m[n] for n in WEIGHTS],
            *[new_v[n] for n in WEIGHTS])
```

```python
import functools
import math

import numpy as np
import jax
import jax.numpy as jnp
from jax import lax
from jax.experimental import pallas as pl
from jax.experimental.pallas import tpu as pltpu

F32 = jnp.float32
BF16 = jnp.bfloat16

N_DEV = 8
HEAD_DIM = 64
GROUP_W = 256
GLA_RANK = 16
GLA_TAU = 16.0
GLA_CHUNK = 64
GRID_W = 64
NA_ROWS = 8
NA_COLS = 16
LRU_C = 8.0
DIL_PAIRS = ((128, 1), (512, 4), (2048, 16))
DIL_RADIUS = 64
ROPE_THETA = 10000.0
EPS = 1e-6
ATT_SCALE = HEAD_DIM ** -0.5
NEG = -1e30
LANES = 128
P_COLS = 12 * GROUP_W + LANES
Z_BLOCK = 12 * GROUP_W // LANES

ADAM_LR = 0.001
ADAM_B1 = 0.9
ADAM_B2 = 0.999
ADAM_EPS = 1e-08
ADAM_WD = 0.01
ADAM_STEP = 10

VMEM_LIMIT = 56 * 1024 * 1024
_ARB = lambda n: pltpu.CompilerParams(dimension_semantics=("arbitrary",) * n, vmem_limit_bytes=VMEM_LIMIT)


def _tile(dim, pref, unit):
    t = min(pref, dim) // unit * unit
    while t >= unit:
        if dim % t == 0:
            return t
        t -= unit
    return dim


def _mm(a, b, mode, out_dtype, name, tm=512, tn=None, tk=None):
    if mode == "nn":
        (M, K), (_, N) = a.shape, b.shape
    elif mode == "nt":
        (M, K), (N, _) = a.shape, b.shape
    else:
        (K, M), (_, N) = a.shape, b.shape
    tm = _tile(M, tm, LANES if mode == "tn" else 8)
    tn = _tile(N, tn or N, LANES)
    tk = _tile(K, tk or K, LANES)
    nk = K // tk
    dims = {"nn": (((1,), (0,)), ((), ())), "nt": (((1,), (1,)), ((), ())), "tn": (((0,), (0,)), ((), ()))}[mode]

    def kern(a_ref, b_ref, o_ref, *acc):
        part = lax.dot_general(a_ref[...].astype(BF16), b_ref[...].astype(BF16), dims, preferred_element_type=F32)
        if nk == 1:
            o_ref[...] = part.astype(out_dtype)
            return
        k = pl.program_id(2)

        @pl.when(k == 0)
        def _():
            acc[0][...] = part

        @pl.when(jnp.logical_and(k > 0, k < nk - 1))
        def _():
            acc[0][...] += part

        @pl.when(k == nk - 1)
        def _():
            o_ref[...] = (acc[0][...] + part).astype(out_dtype)

    a_spec = pl.BlockSpec((tk, tm), lambda i, j, k: (k, i)) if mode == "tn" else pl.BlockSpec((tm, tk), lambda i, j, k: (i, k))
    b_spec = pl.BlockSpec((tn, tk), lambda i, j, k: (j, k)) if mode == "nt" else pl.BlockSpec((tk, tn), lambda i, j, k: (k, j))
    return pl.pallas_call(
        kern, name=name, grid=(M // tm, N // tn, nk),
        in_specs=[a_spec, b_spec], out_specs=pl.BlockSpec((tm, tn), lambda i, j, k: (i, j)),
        out_shape=jax.ShapeDtypeStruct((M, N), out_dtype),
        scratch_shapes=[pltpu.VMEM((tm, tn), F32)] if nk > 1 else [],
        compiler_params=_ARB(3),
    )(a, b)


class Row:
    def __init__(self, a, width=None, cb=0, halo=False):
        self.a, self.width, self.cb, self.halo = a, width, cb, halo


class Full:
    def __init__(self, a):
        self.a = a


HALO = 8


def _rows(name, body, tm, ins, outs):
    L = next(s.a.shape[0] for s in ins if isinstance(s, Row))
    tm = _tile(L, tm, 16)
    nt = L // tm
    nb8 = L // HALO
    step = tm // HALO
    in_specs, arrays, layout = [], [], []
    for s in ins:
        if isinstance(s, Full):
            nd = s.a.ndim
            in_specs.append(pl.BlockSpec(s.a.shape, lambda i, _nd=nd: (0,) * _nd))
            arrays.append(s.a)
            layout.append(1)
        else:
            w = s.width or s.a.shape[1]
            in_specs.append(pl.BlockSpec((tm, w), lambda i, _cb=s.cb: (i, _cb)))
            arrays.append(s.a)
            if s.halo:
                in_specs.append(pl.BlockSpec((HALO, w), lambda i, _cb=s.cb: (jnp.maximum(i * step - 1, 0), _cb)))
                in_specs.append(pl.BlockSpec((HALO, w), lambda i, _cb=s.cb: (jnp.minimum((i + 1) * step, nb8 - 1), _cb)))
                arrays += [s.a, s.a]
                layout.append(3)
            else:
                layout.append(1)
    out_specs, out_shapes = [], []
    for kind, shp, dt in outs:
        if kind == "row":
            out_specs.append(pl.BlockSpec((tm, shp), lambda i: (i, 0)))
            out_shapes.append(jax.ShapeDtypeStruct((L, shp), dt))
        else:
            out_specs.append(pl.BlockSpec(shp, lambda i, _n=len(shp): (0,) * _n))
            out_shapes.append(jax.ShapeDtypeStruct(shp, dt))
    n_in = len(arrays)

    def kern(*refs):
        i = pl.program_id(0)
        vals, p = [], 0
        for n in layout:
            if n == 1:
                vals.append(refs[p][...])
            else:
                vals.append((refs[p + 1][...], refs[p][...], refs[p + 2][...]))
            p += n
        res = body(i, nt, *vals)
        if not isinstance(res, (tuple, list)):
            res = (res,)
        for (kind, shp, dt), o_ref, r in zip(outs, refs[n_in:], res):
            if kind == "row":
                o_ref[...] = r.astype(dt)
            else:
                @pl.when(i == 0)
                def _(o_ref=o_ref):
                    o_ref[...] = jnp.zeros_like(o_ref)
                o_ref[...] += r.astype(dt)

    res = pl.pallas_call(
        kern, name=name, grid=(nt,), in_specs=in_specs, out_specs=out_specs, out_shape=out_shapes,
        compiler_params=_ARB(1),
    )(*arrays)
    return res


def _shift(h, o, i, nt):
    prev, cur, nxt = h
    if o == 0:
        return cur
    tm = cur.shape[0]
    cat = jnp.concatenate([prev, cur, nxt], axis=0)
    sh = pltpu.roll(cat, (-o) % (tm + 2 * HALO), axis=0)[HALO:HALO + tm]
    row = lax.broadcasted_iota(jnp.int32, cur.shape, 0)
    if o < 0:
        ok = jnp.logical_or(i > 0, row >= -o)
    else:
        ok = jnp.logical_or(i < nt - 1, row < tm - o)
    return jnp.where(ok, sh, 0.0)


def _colsum(v):
    return jnp.sum(v, axis=0, keepdims=True)


def _sigmoid(x):
    return 1.0 / (1.0 + jnp.exp(-x))


def _softplus(x):
    return jnp.maximum(x, 0.0) + jnp.log1p(jnp.exp(-jnp.abs(x)))


def _silu(x):
    return x * _sigmoid(x)


def _dsilu(x):
    s = _sigmoid(x)
    return s * (1.0 + x * (1.0 - s))


_GELU_C = math.sqrt(2.0 / math.pi)


def _gelu(x):
    return 0.5 * x * (1.0 + jnp.tanh(_GELU_C * (x + 0.044715 * x * x * x)))


def _dgelu(x):
    t = jnp.tanh(_GELU_C * (x + 0.044715 * x * x * x))
    return 0.5 * (1.0 + t) + 0.5 * x * (1.0 - t * t) * _GELU_C * (1.0 + 3.0 * 0.044715 * x * x)


def _head_sum(v, bd):
    return jnp.dot(v, bd, precision=lax.Precision.HIGHEST, preferred_element_type=F32)


def _block_ones(n, blk):
    r = np.arange(n)
    return jnp.asarray((r[:, None] // blk == r[None, :] // blk).astype(np.float32))


def _rms_fwd(x, g, name):
    def body(i, nt, x, g):
        r = lax.rsqrt(jnp.mean(x * x, axis=-1, keepdims=True) + EPS)
        return x * r * g
    return _rows(name, body, 256, [Row(x), Full(g)], [("row", x.shape[1], BF16)])[0]


def _rms_resid_fwd(x, y, g, name):
    def body(i, nt, x, y, g):
        r = lax.rsqrt(jnp.mean(y * y, axis=-1, keepdims=True) + EPS)
        return x + y * r * g
    return _rows(name, body, 256, [Row(x), Row(y), Full(g)], [("row", x.shape[1], F32)])[0]


def _rms_bwd(dy, x, g, name, resid=None, out_dtype=F32):
    D = x.shape[1]

    def body(i, nt, dy, x, g, *rest):
        dy = dy.astype(F32)
        r = lax.rsqrt(jnp.mean(x * x, axis=-1, keepdims=True) + EPS)
        xh = x * r
        dxh = dy * g
        dx = r * (dxh - xh * jnp.mean(dxh * xh, axis=-1, keepdims=True))
        if rest:
            dx = dx + rest[0]
        return dx, _colsum(dy * xh)

    ins = [Row(dy), Row(x), Full(g)] + ([Row(resid)] if resid is not None else [])
    return _rows(name, body, 256, ins, [("row", D, out_dtype), ("acc", (1, D), F32)])


def _ffn_in_swiglu(h, w):
    (M, K), N = h.shape, w.shape[1]
    F = N // 2
    tm = _tile(M, 256, 16)

    def kern(a_ref, b_ref, gu_ref, act_ref):
        gu = _dot(a_ref[...], b_ref[...])
        gu_ref[...] = gu
        act_ref[...] = (_silu(gu[:, :F]) * gu[:, F:]).astype(BF16)

    return pl.pallas_call(
        kern, name="ffn_in_swiglu", grid=(M // tm,),
        in_specs=[pl.BlockSpec((tm, K), lambda i: (i, 0)), pl.BlockSpec((K, N), lambda i: (0, 0))],
        out_specs=[pl.BlockSpec((tm, N), lambda i: (i, 0)), pl.BlockSpec((tm, F), lambda i: (i, 0))],
        out_shape=[jax.ShapeDtypeStruct((M, N), F32), jax.ShapeDtypeStruct((M, F), BF16)], compiler_params=_ARB(1),
    )(h, w)


def _ffn_out_dx_swiglu(df, w, gu):
    (M, K), N = df.shape, gu.shape[1]
    F = N // 2
    tm = _tile(M, 256, 16)

    def kern(a_ref, b_ref, gu_ref, o_ref):
        da = _dot(a_ref[...], b_ref[...], _NT)
        gu = gu_ref[...]
        gate, up = gu[:, :F], gu[:, F:]
        o_ref[:, :F] = (da * up * _dsilu(gate)).astype(BF16)
        o_ref[:, F:] = (da * _silu(gate)).astype(BF16)

    return pl.pallas_call(
        kern, name="ffn_out_dx_swiglu", grid=(M // tm,),
        in_specs=[pl.BlockSpec((tm, K), lambda i: (i, 0)), pl.BlockSpec((F, K), lambda i: (0, 0)),
                  pl.BlockSpec((tm, N), lambda i: (i, 0))],
        out_specs=pl.BlockSpec((tm, N), lambda i: (i, 0)), out_shape=jax.ShapeDtypeStruct((M, N), BF16),
        compiler_params=_ARB(1),
    )(df, w, gu)


def _loss_fwd_bwd(y, target):
    D = y.shape[1]

    def body(i, nt, y, t):
        err = y - t
        part = 0.5 * jnp.sum(jnp.mean(err * err, axis=-1, keepdims=True), axis=0, keepdims=True)
        return err * (1.0 / D), jnp.broadcast_to(part, (1, LANES))
    dy, loss = _rows("loss", body, 256, [Row(y), Row(target)], [("row", D, F32), ("acc", (1, LANES), F32)])
    return loss[0, 0], dy


def _adamw(w, g, m, v, name):
    C = w.shape[1]
    bc1 = 1.0 - ADAM_B1 ** ADAM_STEP
    bc2 = 1.0 - ADAM_B2 ** ADAM_STEP

    def body(i, nt, w, g, m, v):
        m = ADAM_B1 * m + (1.0 - ADAM_B1) * g
        v = ADAM_B2 * v + (1.0 - ADAM_B2) * (g * g)
        delta = -ADAM_LR * ((m / bc1) / (jnp.sqrt(v / bc2) + ADAM_EPS) + ADAM_WD * w)
        return delta, m, v
    return _rows(name, body, 256, [Row(w), Row(g), Row(m), Row(v)], [("row", C, F32)] * 3)


def _expm1(x):
    return jnp.tanh(0.5 * x) * (jnp.exp(x) + 1.0)


def _lru_gates(xh, i, nt, cw, cb, wa, wx, ba, bx, lam):
    xc = cb
    for j in range(4):
        xc = xc + cw[j:j + 1] * _shift(xh, j - 2, i, nt)
    xcb = xc.astype(BF16)
    gates = []
    for e in range(2):
        r = _sigmoid(jnp.dot(xcb, wa[e], preferred_element_type=F32) + ba[e:e + 1])
        ig = _sigmoid(jnp.dot(xcb, wx[e], preferred_element_type=F32) + bx[e:e + 1])
        sp = _softplus(-lam[e:e + 1])
        la = -LRU_C * r * sp
        gates.append((r, ig, sp, jnp.exp(la), jnp.sqrt(-_expm1(2.0 * la))))
    return xc, xcb, gates


def _scan2(af, uf, ab, ub, adjoint, name):
    L, W = af.shape
    tm = _tile(L, 512, 8)
    nt, nb = L // tm, tm // 8

    def blk(A, U, h, reverse, row):
        for d in (1, 2, 4):
            if reverse:
                ok, sh = row < 8 - d, 8 - d
            else:
                ok, sh = row >= d, d
            As = jnp.where(ok, pltpu.roll(A, sh, axis=0), 1.0)
            Us = jnp.where(ok, pltpu.roll(U, sh, axis=0), 0.0)
            U = A * Us + U
            A = A * As
        return A * h + U

    def kern(af_ref, uf_ref, ab_ref, ub_ref, of_ref, ob_ref, c_ref):
        @pl.when(pl.program_id(0) == 0)
        def _():
            c_ref[...] = jnp.zeros_like(c_ref)

        row = lax.broadcasted_iota(jnp.int32, (8, W), 0)
        full = lambda v: jnp.broadcast_to(v, (8, W))

        def body(j, carry):
            hF, aF, hB, aB = carry
            r0 = pl.multiple_of(j * 8, 8)
            r1 = pl.multiple_of((nb - 1 - j) * 8, 8)
            A, U = af_ref[pl.ds(r0, 8), :], uf_ref[pl.ds(r0, 8), :]
            if adjoint:
                C = jnp.where(row == 0, aF, pltpu.roll(A, 1, axis=0))
                aF = full(A[7:8])
            else:
                C = A
            H = blk(C, U, hF, False, row)
            of_ref[pl.ds(r0, 8), :] = H
            hF = full(H[7:8])
            A, U = ab_ref[pl.ds(r1, 8), :], ub_ref[pl.ds(r1, 8), :]
            if adjoint:
                C = jnp.where(row == 7, aB, pltpu.roll(A, 7, axis=0))
                aB = full(A[0:1])
            else:
                C = A
            H = blk(C, U, hB, True, row)
            ob_ref[pl.ds(r1, 8), :] = H
            hB = full(H[0:1])
            return hF, aF, hB, aB

        carry = lax.fori_loop(0, nb, body, (c_ref[0], c_ref[1], c_ref[2], c_ref[3]))
        for n in range(4):
            c_ref[n] = carry[n]

    fwd = pl.BlockSpec((tm, W), lambda i: (i, 0))
    bwd = pl.BlockSpec((tm, W), lambda i: (nt - 1 - i, 0))
    return pl.pallas_call(
        kern, name=name, grid=(nt,), in_specs=[fwd, fwd, bwd, bwd], out_specs=[fwd, bwd],
        out_shape=[jax.ShapeDtypeStruct((L, W), F32)] * 2,
        scratch_shapes=[pltpu.VMEM((4, 8, W), F32)], compiler_params=_ARB(1),
    )(af, uf, ab, ub)


def _block_diag(w):
    out = jnp.zeros((2, GROUP_W, GROUP_W), w.dtype)
    for h in range(4):
        out = out.at[:, h * 64:(h + 1) * 64, h * 64:(h + 1) * 64].set(w[:, h])
    return out.astype(BF16)


def _diag_blocks(w):
    return jnp.stack([w[:, h * 64:(h + 1) * 64, h * 64:(h + 1) * 64] for h in range(4)], axis=1)


def _lru_params(W, l):
    return [Full(W["lru_conv_w"][l]), Full(W["lru_conv_b"][l][None]), Full(_block_diag(W["lru_w_a"][l])),
            Full(_block_diag(W["lru_w_x"][l])), Full(W["lru_b_a"][l]), Full(W["lru_b_x"][l]), Full(W["lru_lambda"][l])]


def _lru_fwd(p, W, l):
    def pre(i, nt, xh, *prm):
        xc, _, g = _lru_gates(xh, i, nt, *prm)
        return g[0][3], g[0][4] * (g[0][1] * xc), g[1][3], g[1][4] * (g[1][1] * xc)

    a0, u0, a1, u1 = _rows("lru_pre", pre, 256, [Row(p, GROUP_W, 7, halo=True)] + _lru_params(W, l),
                           [("row", GROUP_W, F32)] * 4)
    hf, hb = _scan2(a0, u0, a1, u1, False, "lru_scan")
    yc = _rows("lru_post", lambda i, nt, hf, hb, gc: (hf + hb) * _gelu(gc), 512,
               [Row(hf), Row(hb), Row(p, GROUP_W, 8)], [("row", GROUP_W, BF16)])[0]
    return yc, (a0, a1, hf, hb)


def _lru_bwd(dy, dy_cb, p, W, l, saved):
    a0, a1, hf, hb = saved

    def post(i, nt, dy, hf, hb, gc):
        return dy * _gelu(gc), dy * (hf + hb) * _dgelu(gc)

    dh, dgc = _rows("lru_post_bwd", post, 512, [Row(dy, GROUP_W, dy_cb), Row(hf), Row(hb), Row(p, GROUP_W, 8)],
                    [("row", GROUP_W, F32), ("row", GROUP_W, BF16)])
    gb, gf = _scan2(a1, dh, a0, dh, True, "lru_scan_adj")

    def gates_bwd(i, nt, xh, gf, gb, hfh, hbh, cw, cb, wa, wx, ba, bx, lam):
        xc, xcb, g = _lru_gates(xh, i, nt, cw, cb, wa, wx, ba, bx, lam)
        dxc = jnp.zeros_like(xc)
        dwa, dwx, dba, dbx, dlam = [], [], [], [], []
        for e, du, hprev in ((0, gf, _shift(hfh, -1, i, nt)), (1, gb, _shift(hbh, 1, i, nt))):
            r, ig, sp, a, s = g[e]
            dxc = dxc + du * s * ig
            dla = du * hprev * a - (du * ig * xc) * a * a / s
            dza = (dla * (-LRU_C) * sp) * r * (1.0 - r)
            dzx = (du * s * xc) * ig * (1.0 - ig)
            dlam.append(_colsum(dla * r) * (LRU_C * _sigmoid(-lam[e:e + 1])))
            dba.append(_colsum(dza))
            dbx.append(_colsum(dzx))
            dzab, dzxb = dza.astype(BF16), dzx.astype(BF16)
            tn = (((0,), (0,)), ((), ()))
            nt_ = (((1,), (1,)), ((), ()))
            dwa.append(lax.dot_general(xcb, dzab, tn, preferred_element_type=F32))
            dwx.append(lax.dot_general(xcb, dzxb, tn, preferred_element_type=F32))
            dxc = dxc + lax.dot_general(dzab, wa[e], nt_, preferred_element_type=F32)
            dxc = dxc + lax.dot_general(dzxb, wx[e], nt_, preferred_element_type=F32)
        cat = lambda v: jnp.concatenate(v, axis=0)
        return dxc, jnp.stack(dwa), jnp.stack(dwx), cat(dba), cat(dbx), cat(dlam)

    dxc, dwa, dwx, dba, dbx, dlam = _rows(
        "lru_gates_bwd", gates_bwd, 256,
        [Row(p, GROUP_W, 7, halo=True), Row(gf), Row(gb), Row(hf, halo=True), Row(hb, halo=True)] + _lru_params(W, l),
        [("row", GROUP_W, F32), ("acc", (2, GROUP_W, GROUP_W), F32), ("acc", (2, GROUP_W, GROUP_W), F32),
         ("acc", (2, GROUP_W), F32), ("acc", (2, GROUP_W), F32), ("acc", (2, GROUP_W), F32)])

    def conv_bwd(i, nt, dh_, xh, cw):
        dxb = jnp.zeros_like(dh_[1])
        dcw = []
        for j in range(4):
            dxb = dxb + cw[j:j + 1] * _shift(dh_, 2 - j, i, nt)
            dcw.append(_colsum(dh_[1] * _shift(xh, j - 2, i, nt)))
        return dxb, jnp.concatenate(dcw, axis=0), _colsum(dh_[1])

    dxb, dcw, dcb = _rows("lru_conv_bwd", conv_bwd, 512,
                          [Row(dxc, halo=True), Row(p, GROUP_W, 7, halo=True), Full(W["lru_conv_w"][l])],
                          [("row", GROUP_W, BF16), ("acc", (4, GROUP_W), F32), ("acc", (1, GROUP_W), F32)])
    grads = dict(lru_conv_w=dcw, lru_conv_b=dcb[0], lru_w_a=_diag_blocks(dwa), lru_w_x=_diag_blocks(dwx),
                 lru_b_a=dba, lru_b_x=dbx, lru_lambda=dlam)
    return dxb, dgc, grads


_NT = (((1,), (1,)), ((), ()))
_TN = (((0,), (0,)), ((), ()))


def _dot(a, b, dims=None):
    if dims is None:
        return jnp.dot(a, b, preferred_element_type=F32)
    return lax.dot_general(a, b, dims, preferred_element_type=F32)


def _dot_exact(a, b):
    return jnp.dot(a, b, precision=lax.Precision.HIGHEST, preferred_element_type=F32)


def _gla_gate_w(w_gate, b_gate):
    wg = jnp.zeros((LANES, 2 * GROUP_W), F32)
    for e in range(2):
        wg = wg.at[e * GLA_RANK:(e + 1) * GLA_RANK, e * GROUP_W:(e + 1) * GROUP_W].set(w_gate[e])
    return wg.astype(BF16), b_gate.reshape(1, 2 * GROUP_W)


def _gla_gates_fwd(p, wg, bg):
    def body(i, nt, z, wg, bg):
        logit = _dot(z.astype(BF16), wg) + bg
        la = -_softplus(-logit) * (1.0 / GLA_TAU)
        return la[:, :GROUP_W], la[:, GROUP_W:]
    return _rows("gla_gates", body, 512, [Row(p, LANES, Z_BLOCK), Full(wg), Full(bg)], [("row", GROUP_W, F32)] * 2)


def _gla_gates_bwd(p, dla0, dla1, wg, bg):
    def body(i, nt, z, d0, d1, wg, bg):
        zb = z.astype(BF16)
        logit = _dot(zb, wg) + bg
        dlogit = jnp.concatenate([d0, d1], axis=1) * (1.0 / GLA_TAU) * _sigmoid(-logit)
        dlb = dlogit.astype(BF16)
        return _dot(dlb, wg, _NT), _dot(zb, dlb, _TN), _colsum(dlogit)
    return _rows("gla_gates_bwd", body, 512, [Row(p, LANES, Z_BLOCK), Row(dla0), Row(dla1), Full(wg), Full(bg)],
                 [("row", LANES, BF16), ("acc", (LANES, 2 * GROUP_W), F32), ("acc", (1, 2 * GROUP_W), F32)])


def _gla_order(reverse):
    t = np.arange(GLA_CHUNK)
    m = (t[None, :] >= t[:, None]) if reverse else (t[None, :] <= t[:, None])
    return m.astype(np.float32), (32, 0) if reverse else (31, 63)


def _stack_heads(x, bd):
    return jnp.where(bd, jnp.concatenate([x] * 4, axis=0), 0.0)


def _diag_heads(r, bd):
    r = jnp.where(bd, r, 0.0)
    return r[0:64] + r[64:128] + r[128:192] + r[192:256]


def _gla_factors(q_ref, k_ref, rows, b, mid, last):
    bm, bl = b[mid:mid + 1], b[last:last + 1]
    qs = q_ref[rows, :] * ATT_SCALE
    k = k_ref[rows, :]
    P, N, E, Fd = jnp.exp(b - bm), jnp.exp(bm - b), jnp.exp(b), jnp.exp(bl - b)
    return (P, N, E, Fd, jnp.exp(bl)), (qs * P, k * N, qs * E, k * Fd)


def _gla_specs(L, walk_up):
    tm = _tile(L, 512, GLA_CHUNK)
    nt, nc = L // tm, tm // GLA_CHUNK
    specs = []
    for up in walk_up:
        t = (lambda i: i) if up else (lambda i: nt - 1 - i)
        specs.append(dict(
            col=lambda cb, _t=t: pl.BlockSpec((tm, GROUP_W), lambda i: (_t(i), cb)),
            row=pl.BlockSpec((tm, GROUP_W), lambda i, _t=t: (_t(i), 0)),
            state=pl.BlockSpec((nc, GROUP_W, GROUP_W), lambda i, _t=t: (_t(i), 0, 0))))
    return nt, nc, specs


def _gla_chunk_fwd(p, la0, la1, ride=None):
    L = la0.shape[0]
    nt, nc, specs = _gla_specs(L, (True, False))
    orders = [_gla_order(False), _gla_order(True)]

    def kern(q0, k0, v0, l0, q1, k1, v1, l1, m0_ref, m1_ref, bd_ref, o0, s0, o1, s1, st_ref):
        @pl.when(pl.program_id(0) == 0)
        def _():
            st_ref[...] = jnp.zeros_like(st_ref)

        bd = bd_ref[...] > 0.5
        dirs = []
        for e, (q_ref, k_ref, v_ref, la_ref, m_ref, o_ref, s_ref) in enumerate(
                ((q0, k0, v0, l0, m0_ref, o0, s0), (q1, k1, v1, l1, m1_ref, o1, s1))):
            mv = m_ref[...]
            dirs.append((q_ref, k_ref, v_ref, la_ref, mv, jnp.concatenate([mv] * 4, axis=0) > 0.5, o_ref, s_ref))

        def body(cc, carry):
            E = range(2)
            cs = [nc - 1 - cc if e else cc for e in E]
            rows = [pl.ds(pl.multiple_of(c * GLA_CHUNK, GLA_CHUNK), GLA_CHUNK) for c in cs]
            b = [_dot_exact(dirs[e][4], dirs[e][3][rows[e], :]) for e in E]
            t = [_gla_factors(dirs[e][0], dirs[e][1], rows[e], b[e], *orders[e][1]) for e in E]
            vb = [dirs[e][2][rows[e], :].astype(BF16) for e in E]
            st = [st_ref[e] for e in E]
            a = [_dot(_stack_heads(t[e][1][0], bd).astype(BF16), t[e][1][1].astype(BF16), _NT) for e in E]
            inter = [_dot(t[e][1][2].astype(BF16), st[e].astype(BF16), _NT) for e in E]
            kv = [_dot(vb[e], t[e][1][3].astype(BF16), _TN) for e in E]
            a = [jnp.where(dirs[e][5], a[e], 0.0).astype(BF16) for e in E]
            r = [_dot(a[e], vb[e]) for e in E]
            for e in E:
                dirs[e][7][cs[e]] = st[e]
                dirs[e][6][rows[e], :] = _diag_heads(r[e], bd) + inter[e]
                st_ref[e] = st[e] * t[e][0][4] + jnp.where(bd, kv[e], 0.0)
            return carry

        lax.fori_loop(0, nc, body, 0)

    const = lambda shp: pl.BlockSpec(shp, lambda i: (0, 0))
    in_specs, out_specs = [], []
    for sp in specs:
        in_specs += [sp["col"](0), sp["col"](1), sp["col"](2), sp["row"]]
        out_specs += [sp["row"], sp["state"]]
    return _call(
        kern, (p, p, p, la0, p, p, p, la1, jnp.asarray(orders[0][0]), jnp.asarray(orders[1][0]),
               _block_ones(GROUP_W, HEAD_DIM)),
        ride, lambda: (pl.program_id(0) == 0, pl.program_id(0) == nt - 1), name="gla_fwd", grid=(nt,),
        in_specs=in_specs + [const((GLA_CHUNK, GLA_CHUNK))] * 2 + [const((GROUP_W, GROUP_W))], out_specs=out_specs,
        out_shape=[jax.ShapeDtypeStruct((L, GROUP_W), F32),
                   jax.ShapeDtypeStruct((L // GLA_CHUNK, GROUP_W, GROUP_W), F32)] * 2,
        scratch_shapes=[pltpu.VMEM((2, GROUP_W, GROUP_W), F32)])


def _gla_chunk_bwd(p, la0, la1, do, sprev0, sprev1, ride=None):
    L = la0.shape[0]
    nt, nc, specs = _gla_specs(L, (False, True))
    orders = [_gla_order(False), _gla_order(True)]

    def kern(q0, k0, v0, l0, do0, s0, q1, k1, v1, l1, do1, s1, m0_ref, m1_ref, t0_ref, t1_ref, bd_ref, *rest):
        outs, dst_ref = (rest[0:4], rest[4:8]), rest[8]

        @pl.when(pl.program_id(0) == 0)
        def _():
            dst_ref[...] = jnp.zeros_like(dst_ref)

        bd = bd_ref[...] > 0.5
        row = lax.broadcasted_iota(jnp.int32, (GLA_CHUNK, GROUP_W), 0)
        dirs = []
        for ins, m_ref, t_ref in (((q0, k0, v0, l0, do0, s0), m0_ref, t0_ref), ((q1, k1, v1, l1, do1, s1), m1_ref, t1_ref)):
            mv = m_ref[...]
            dirs.append(ins + (mv, t_ref[...], jnp.concatenate([mv] * 4, axis=0) > 0.5))

        def body(cc, carry):
            E2 = range(2)
            cs = [cc if e else nc - 1 - cc for e in E2]
            rows = [pl.ds(pl.multiple_of(c * GLA_CHUNK, GLA_CHUNK), GLA_CHUNK) for c in cs]
            b = [_dot_exact(dirs[e][6], dirs[e][3][rows[e], :]) for e in E2]
            t = [_gla_factors(dirs[e][0], dirs[e][1], rows[e], b[e], *orders[e][1]) for e in E2]
            vb = [dirs[e][2][rows[e], :].astype(BF16) for e in E2]
            dov = [dirs[e][4][rows[e], :] for e in E2]
            dob = [x.astype(BF16) for x in dov]
            st = [dirs[e][5][cs[e]] for e in E2]
            dst = [dst_ref[e] for e in E2]
            stb, dstb = [x.astype(BF16) for x in st], [x.astype(BF16) for x in dst]
            qst = [_stack_heads(t[e][1][0], bd).astype(BF16) for e in E2]
            dost = [_stack_heads(dov[e], bd).astype(BF16) for e in E2]
            kNb, qEb, kFb = ([t[e][1][n].astype(BF16) for e in E2] for n in (1, 2, 3))
            a = [_dot(qst[e], kNb[e], _NT) for e in E2]
            da = [_dot(dost[e], vb[e], _NT) for e in E2]
            dqE = [_dot(dob[e], stb[e]) for e in E2]
            dkF = [_dot(vb[e], dstb[e]) for e in E2]
            dv_inter = [_dot(kFb[e], dstb[e], _NT) for e in E2]
            dst_in = [_dot(dob[e], qEb[e], _TN) for e in E2]
            a = [jnp.where(dirs[e][8], a[e], 0.0).astype(BF16) for e in E2]
            da = [jnp.where(dirs[e][8], da[e], 0.0).astype(BF16) for e in E2]
            dv_intra = [_dot(a[e], dost[e], _TN) for e in E2]
            dqP = [_dot(da[e], kNb[e]) for e in E2]
            dkN = [_dot(da[e], qst[e], _TN) for e in E2]
            db = []
            for e in E2:
                (P, N, Ef, Fd, d), (qP, kN, qE, kF) = t[e]
                mid, last = orders[e][1]
                dq_ref, dk_ref, dv_ref, _ = outs[e]
                dqp = _diag_heads(dqP[e], bd)
                dd = _colsum(dst[e] * st[e])
                dst_ref[e] = jnp.where(bd, dst_in[e], 0.0) + dst[e] * d
                tP, tN, tE, tF = dqp * qP, dkN[e] * kN, dqE[e] * qE, dkF[e] * kF
                db.append(tP - tN + tE - tF + jnp.where(row == mid, _colsum(tN - tP), 0.0)
                          + jnp.where(row == last, _colsum(tF) + dd * d, 0.0))
                dq_ref[rows[e], :] = (dqp * P + dqE[e] * Ef) * ATT_SCALE
                dk_ref[rows[e], :] = dkN[e] * N + dkF[e] * Fd
                dv_ref[rows[e], :] = dv_intra[e] + dv_inter[e]
            dla = [_dot_exact(dirs[e][7], db[e]) for e in E2]
            for e in E2:
                outs[e][3][rows[e], :] = dla[e]
            return carry

        lax.fori_loop(0, nc, body, 0)

    const = lambda shp: pl.BlockSpec(shp, lambda i: (0, 0))
    in_specs, out_specs = [], []
    for sp in specs:
        in_specs += [sp["col"](0), sp["col"](1), sp["col"](2), sp["row"], sp["row"], sp["state"]]
        out_specs += [sp["row"]] * 4
    m0, m1 = orders[0][0], orders[1][0]
    return _call(
        kern, (p, p, p, la0, do, sprev0, p, p, p, la1, do, sprev1, jnp.asarray(m0), jnp.asarray(m1),
               jnp.asarray(m0.T.copy()), jnp.asarray(m1.T.copy()), _block_ones(GROUP_W, HEAD_DIM)),
        ride, lambda: (pl.program_id(0) == 0, pl.program_id(0) == nt - 1), name="gla_bwd", grid=(nt,),
        in_specs=in_specs + [const((GLA_CHUNK, GLA_CHUNK))] * 4 + [const((GROUP_W, GROUP_W))], out_specs=out_specs,
        out_shape=[jax.ShapeDtypeStruct((L, GROUP_W), F32)] * 8,
        scratch_shapes=[pltpu.VMEM((2, GROUP_W, GROUP_W), F32)])


def _gla_fwd(p, W, l, ride=None):
    wg, bg = _gla_gate_w(W["gla_w_gate"][l], W["gla_b_gate"][l])
    la0, la1 = _gla_gates_fwd(p, wg, bg)
    (of, s0, ob, s1), got = _gla_chunk_fwd(p, la0, la1, ride)

    def post(i, nt, of, ob, g, ng, bd):
        o = of + ob
        r = lax.rsqrt(_head_sum(o * o, bd) * (1.0 / HEAD_DIM) + EPS)
        return o * r * ng * _silu(g)

    ya = _rows("gla_post", post, 512, [Row(of), Row(ob), Row(p, GROUP_W, 3), Full(W["gla_norm"][l][None]),
                                       Full(_block_ones(GROUP_W, HEAD_DIM))], [("row", GROUP_W, BF16)])[0]
    return ya, (la0, la1, of, ob, s0, s1), got


def _gla_bwd(dy, dy_cb, p, W, l, saved, ride=None):
    la0, la1, of, ob, s0, s1 = saved
    wg, bg = _gla_gate_w(W["gla_w_gate"][l], W["gla_b_gate"][l])

    def post(i, nt, dy, of, ob, g, ng, bd):
        o = of + ob
        r = lax.rsqrt(_head_sum(o * o, bd) * (1.0 / HEAD_DIM) + EPS)
        oh = o * r
        don = dy * _silu(g)
        doh = don * ng
        do = r * (doh - oh * _head_sum(doh * oh, bd) * (1.0 / HEAD_DIM))
        return do, dy * (oh * ng) * _dsilu(g), _colsum(don * oh)

    do, dg, dng = _rows("gla_post_bwd", post, 512,
                        [Row(dy, GROUP_W, dy_cb), Row(of), Row(ob), Row(p, GROUP_W, 3), Full(W["gla_norm"][l][None]),
                         Full(_block_ones(GROUP_W, HEAD_DIM))],
                        [("row", GROUP_W, F32), ("row", GROUP_W, BF16), ("acc", (1, GROUP_W), F32)])
    (dq0, dk0, dv0, dla0, dq1, dk1, dv1, dla1), got = _gla_chunk_bwd(p, la0, la1, do, s0, s1, ride)
    dq, dk, dv = _rows("gla_sum_bwd", lambda i, nt, a0, a1, b0, b1, c0, c1: (a0 + a1, b0 + b1, c0 + c1), 512,
                       [Row(t) for t in (dq0, dq1, dk0, dk1, dv0, dv1)], [("row", GROUP_W, BF16)] * 3)
    dz, dwg, dbg = _gla_gates_bwd(p, dla0, dla1, wg, bg)
    dw_gate = jnp.stack([dwg[e * GLA_RANK:(e + 1) * GLA_RANK, e * GROUP_W:(e + 1) * GROUP_W] for e in range(2)])
    grads = dict(gla_w_gate=dw_gate, gla_b_gate=dbg.reshape(2, GROUP_W), gla_norm=dng[0])
    return (dq, dk, dv, dg, dz), grads, got


def _rope_tables(L):
    pos = jnp.arange(L, dtype=F32)
    inv_freq = ROPE_THETA ** (-jnp.arange(0, HEAD_DIM, 2, dtype=F32) / HEAD_DIM)
    ang = pos[:, None] * inv_freq[None, :]
    cos, sin = jnp.cos(ang), jnp.sin(ang)
    return jnp.tile(jnp.concatenate([cos, cos], axis=1), (1, 4)), jnp.tile(jnp.concatenate([-sin, sin], axis=1), (1, 4))


def _swap_halves(t):
    lane = lax.broadcasted_iota(jnp.int32, t.shape, 1)
    first = (lane & (HEAD_DIM - 1)) < HEAD_DIM // 2
    return jnp.where(first, pltpu.roll(t, GROUP_W - HEAD_DIM // 2, axis=1), pltpu.roll(t, HEAD_DIM // 2, axis=1))


def _attn_prep(p, cosf, sinf):
    def body(i, nt, qb, kb, vb, qd, kd, vd, c, s):
        return qb, kb, vb, qd * c + _swap_halves(qd) * s, kd * c + _swap_halves(kd) * s, vd
    ins = [Row(p, GROUP_W, cb) for cb in (4, 5, 6, 9, 10, 11)] + [Row(cosf), Row(sinf)]
    return _rows("attn_prep", body, 512, ins, [("row", GROUP_W, BF16)] * 6)


def _na_onehot():
    c = np.arange(GRID_W)
    dc = np.clip(c[None, :] - c[:, None], -(NA_COLS - 1), NA_COLS - 1) + NA_COLS - 1
    oh = np.zeros((LANES, GRID_W * GRID_W), np.float32)
    oh[dc.reshape(-1), np.arange(GRID_W * GRID_W)] = 1.0
    return jnp.asarray(oh)


def _na_colmask():
    c = np.arange(GRID_W)
    start = np.clip(c - NA_COLS // 2, 0, GRID_W - NA_COLS)
    ok = (c[None, :] >= start[:, None]) & (c[None, :] < start[:, None] + NA_COLS)
    return jnp.asarray(np.where(ok, 0.0, NEG).astype(np.float32))


N_DR = 2 * NA_ROWS - 1


NA_HALF = GRID_W // 2
NA_KCOLS = 48
NA_WIN = NA_ROWS * NA_KCOLS
NA_ROWS_PER_STEP = 2


def _na_bias(rpb):
    rp = jnp.zeros((GRID_W, LANES), F32).at[:4 * N_DR, :2 * NA_COLS - 1].set(rpb.reshape(4 * N_DR, 2 * NA_COLS - 1))

    def expand(r_ref, oh_ref, o_ref):
        o_ref[...] = _dot_exact(r_ref[...], oh_ref[...])

    r = pl.pallas_call(expand, name="na_bias_expand",
                       out_shape=jax.ShapeDtypeStruct((GRID_W, GRID_W * GRID_W), F32))(rp, _na_onehot())
    r = r[:4 * N_DR].reshape(4, N_DR, GRID_W, GRID_W)

    def build(r_ref, m_ref, o_ref):
        for h in range(4):
            for c in range(NA_ROWS):
                for half in range(2):
                    q0, k0 = NA_HALF * half, 16 * half
                    for i in range(NA_ROWS):
                        o_ref[h, c, half, :, i * NA_KCOLS:(i + 1) * NA_KCOLS] = (
                            r_ref[h, i - c + NA_ROWS - 1, q0:q0 + NA_HALF, k0:k0 + NA_KCOLS]
                            + m_ref[q0:q0 + NA_HALF, k0:k0 + NA_KCOLS])

    return pl.pallas_call(build, name="na_bias_build",
                          out_shape=jax.ShapeDtypeStruct((4, NA_ROWS, 2, NA_HALF, NA_WIN), F32))(r, _na_colmask())


def _na_bias_bwd(dbias):
    def fold(d_ref, o_ref):
        o_ref[...] = jnp.zeros_like(o_ref)
        for h in range(4):
            for a in range(N_DR):
                for half in range(2):
                    q0, k0 = NA_HALF * half, 16 * half
                    acc = jnp.zeros((NA_HALF, NA_KCOLS), F32)
                    for c in range(NA_ROWS):
                        i = a + c - (NA_ROWS - 1)
                        if 0 <= i < NA_ROWS:
                            acc = acc + d_ref[h, c, half, :, i * NA_KCOLS:(i + 1) * NA_KCOLS]
                    o_ref[h, a, q0:q0 + NA_HALF, k0:k0 + NA_KCOLS] = acc

    dr = pl.pallas_call(fold, name="na_bias_fold",
                        out_shape=jax.ShapeDtypeStruct((4, N_DR, GRID_W, GRID_W), F32))(dbias)
    dr = jnp.zeros((GRID_W, GRID_W * GRID_W), F32).at[:4 * N_DR].set(dr.reshape(4 * N_DR, GRID_W * GRID_W))

    def contract(d_ref, oh_ref, o_ref):
        o_ref[...] = lax.dot_general(d_ref[...], oh_ref[...], _NT, precision=lax.Precision.HIGHEST,
                                     preferred_element_type=F32)

    g = pl.pallas_call(contract, name="na_bias_contract",
                       out_shape=jax.ShapeDtypeStruct((GRID_W, LANES), F32))(dr, _na_onehot())
    return g[:4 * N_DR, :2 * NA_COLS - 1].reshape(4, N_DR, 2 * NA_COLS - 1)


def _na_window(r, n_rows):
    rs = jnp.clip(r - NA_ROWS // 2, 0, n_rows - NA_ROWS)
    return rs, r - rs


def _na_key_rows(rs, half, t):
    return pl.ds(pl.multiple_of((rs + t) * GRID_W + 16 * half, 16), NA_KCOLS)


def _na_keys(ref, rs, half):
    return jnp.concatenate([ref[_na_key_rows(rs, half, t), :] for t in range(NA_ROWS)], axis=0)


def _na_stack(x, first):
    zero = jnp.zeros_like(x)
    return jnp.concatenate([jnp.where(first, x, zero), jnp.where(first, zero, x)], axis=0)


def _na_bias_spec():
    return pl.BlockSpec((2, NA_ROWS, 2, NA_HALF, NA_WIN), lambda j, i: (j, 0, 0, 0, 0))


def _na_edges(nt):
    j, i = pl.program_id(0), pl.program_id(1)
    return jnp.logical_and(j == 0, i == 0), jnp.logical_and(j == 1, i == nt - 1)


def _na_fwd(q, k, v, bias, ride=None):
    L = q.shape[0]
    n_rows = L // GRID_W
    tm = _tile(L, 512, GRID_W)
    nt, nr = L // tm, tm // GRID_W

    def kern(q_ref, k_ref, v_ref, b_ref, o_ref):
        i = pl.program_id(1)
        first = lax.broadcasted_iota(jnp.int32, (NA_HALF, LANES), 1) < HEAD_DIM

        def body(it, carry):
            parts = []
            for u in range(NA_ROWS_PER_STEP):
                rr = it * NA_ROWS_PER_STEP + u
                rs, c = _na_window(i * nr + rr, n_rows)
                for half in range(2):
                    rows = pl.ds(pl.multiple_of(rr * GRID_W + NA_HALF * half, NA_HALF), NA_HALF)
                    bias = jnp.concatenate([b_ref[0, c, half], b_ref[1, c, half]], axis=0)
                    parts.append((rows, _na_stack(q_ref[rows, :], first), bias, _na_keys(k_ref, rs, half),
                                  _na_keys(v_ref, rs, half)))
            s = [_dot(qs, kw, _NT) * ATT_SCALE + bias for _, qs, bias, kw, _ in parts]
            e = [jnp.exp(x - jnp.max(x, axis=-1, keepdims=True)) for x in s]
            pn = [(x / jnp.sum(x, axis=-1, keepdims=True)).astype(BF16) for x in e]
            o = [_dot(p, part[4]) for p, part in zip(pn, parts)]
            for x, (rows, *_) in zip(o, parts):
                o_ref[rows, :] = jnp.where(first, x[:NA_HALF], x[NA_HALF:]).astype(BF16)
            return carry

        lax.fori_loop(0, nr // NA_ROWS_PER_STEP, body, 0)

    qspec = pl.BlockSpec((tm, LANES), lambda j, i: (i, j))
    kvspec = pl.BlockSpec((L, LANES), lambda j, i: (0, j))
    (y,), got = _call(
        kern, (q, k, v, bias), ride, lambda: _na_edges(nt), name="na_fwd", grid=(2, nt),
        in_specs=[qspec, kvspec, kvspec, _na_bias_spec()],
        out_specs=[qspec], out_shape=[jax.ShapeDtypeStruct((L, GROUP_W), BF16)], scratch_shapes=[])
    return y, got


def _na_bwd(dy, dy_block, q, k, v, bias, ride=None):
    L = q.shape[0]
    n_rows = L // GRID_W
    tm = _tile(L, 512, GRID_W)
    nt, nr = L // tm, tm // GRID_W

    def kern(dy_ref, q_ref, k_ref, v_ref, b_ref, dq_ref, dk_ref, dv_ref, db_ref):
        i = pl.program_id(1)

        @pl.when(i == 0)
        def _():
            dk_ref[...] = jnp.zeros_like(dk_ref)
            dv_ref[...] = jnp.zeros_like(dv_ref)
            db_ref[...] = jnp.zeros_like(db_ref)

        first = lax.broadcasted_iota(jnp.int32, (NA_HALF, LANES), 1) < HEAD_DIM

        def body(rr, carry):
            rs, c = _na_window(i * nr + rr, n_rows)
            parts = []
            for half in range(2):
                rows = pl.ds(pl.multiple_of(rr * GRID_W + NA_HALF * half, NA_HALF), NA_HALF)
                bias = jnp.concatenate([b_ref[0, c, half], b_ref[1, c, half]], axis=0)
                parts.append((rows, half, _na_stack(q_ref[rows, :], first), _na_stack(dy_ref[rows, :].astype(BF16), first),
                              bias, _na_keys(k_ref, rs, half), _na_keys(v_ref, rs, half)))
            s = [_dot(qs, kw, _NT) * ATT_SCALE + bias for _, _, qs, _, bias, kw, _ in parts]
            dp = [_dot(dos, vw, _NT) for _, _, _, dos, _, _, vw in parts]
            e = [jnp.exp(x - jnp.max(x, axis=-1, keepdims=True)) for x in s]
            pn = [x / jnp.sum(x, axis=-1, keepdims=True) for x in e]
            ds = [p * (d - jnp.sum(p * d, axis=-1, keepdims=True)) for p, d in zip(pn, dp)]
            dsb = [x.astype(BF16) for x in ds]
            pnb = [x.astype(BF16) for x in pn]
            dq = [_dot(x, part[5]) for x, part in zip(dsb, parts)]
            dk = [_dot(x, part[2], _TN) for x, part in zip(dsb, parts)]
            dv = [_dot(x, part[3], _TN) for x, part in zip(pnb, parts)]
            for n, (rows, half, *_) in enumerate(parts):
                db_ref[0, c, half] += ds[n][:NA_HALF]
                db_ref[1, c, half] += ds[n][NA_HALF:]
                dq_ref[rows, :] = (jnp.where(first, dq[n][:NA_HALF], dq[n][NA_HALF:]) * ATT_SCALE).astype(BF16)
                for t in range(NA_ROWS):
                    kr = _na_key_rows(rs, half, t)
                    dk_ref[kr, :] += dk[n][t * NA_KCOLS:(t + 1) * NA_KCOLS] * ATT_SCALE
                    dv_ref[kr, :] += dv[n][t * NA_KCOLS:(t + 1) * NA_KCOLS]
            return carry

        lax.fori_loop(0, nr, body, 0)

    qspec = pl.BlockSpec((tm, LANES), lambda j, i: (i, j))
    kvspec = pl.BlockSpec((L, LANES), lambda j, i: (0, j))
    return _call(
        kern, (dy, q, k, v, bias), ride, lambda: _na_edges(nt), name="na_bwd", grid=(2, nt),
        in_specs=[pl.BlockSpec((tm, LANES), lambda j, i: (i, dy_block + j)), qspec, kvspec, kvspec, _na_bias_spec()],
        out_specs=[qspec, kvspec, kvspec, _na_bias_spec()],
        out_shape=[jax.ShapeDtypeStruct((L, GROUP_W), BF16), jax.ShapeDtypeStruct((L, GROUP_W), F32),
                   jax.ShapeDtypeStruct((L, GROUP_W), F32),
                   jax.ShapeDtypeStruct((4, NA_ROWS, 2, NA_HALF, NA_WIN), F32)], scratch_shapes=[])


def _dil_specs(n, tq):
    R = DIL_RADIUS
    step, nb = tq // R, n // R
    main = pl.BlockSpec((tq, LANES), lambda j, i: (i, j))
    prev = pl.BlockSpec((R, LANES), lambda j, i: (jnp.maximum(i * step - 1, 0), j))
    nxt = pl.BlockSpec((R, LANES), lambda j, i: (jnp.minimum((i + 1) * step, nb - 1), j))
    return main, prev, nxt


def _dil_valid(i, tq, n):
    R = DIL_RADIUS
    row = lax.broadcasted_iota(jnp.int32, (tq, tq + 2 * R), 0)
    col = lax.broadcasted_iota(jnp.int32, (tq, tq + 2 * R), 1)
    kpos = i * tq - R + col
    return (jnp.abs(col - R - row) <= R) & (kpos >= 0) & (kpos < n)


def _dil_fwd(q, k, v, dil):
    L = q.shape[0]
    n = L // dil
    tq = _tile(n, 256, DIL_RADIUS)
    view = lambda t: t.reshape(n, dil * GROUP_W)

    def kern(q_ref, kp_ref, k_ref, kn_ref, vp_ref, v_ref, vn_ref, o_ref, l_ref):
        i = pl.program_id(1)
        valid = _dil_valid(i, tq, n)
        qv = q_ref[...]
        ka = jnp.concatenate([kp_ref[...], k_ref[...], kn_ref[...]], axis=0)
        va = jnp.concatenate([vp_ref[...], v_ref[...], vn_ref[...]], axis=0)
        lane = lax.broadcasted_iota(jnp.int32, (tq, LANES), 1)
        first = lane < HEAD_DIM
        s = [_dot(jnp.where(first == (hh == 0), qv, jnp.zeros_like(qv)), ka, _NT) for hh in range(2)]
        s = [jnp.where(valid, x * ATT_SCALE, NEG) for x in s]
        m = [jnp.max(x, axis=-1, keepdims=True) for x in s]
        e = [jnp.exp(x - mx) for x, mx in zip(s, m)]
        den = [jnp.sum(x, axis=-1, keepdims=True) for x in e]
        o = [_dot((x / d).astype(BF16), va) for x, d in zip(e, den)]
        o_ref[...] = jnp.where(first, o[0], o[1])
        l_ref[...] = jnp.where(first, m[0] + jnp.log(den[0]), m[1] + jnp.log(den[1]))

    main, prev, nxt = _dil_specs(n, tq)
    o, lse = pl.pallas_call(
        kern, name=f"dil_fwd_{dil}", grid=(2 * dil, n // tq),
        in_specs=[main, prev, main, nxt, prev, main, nxt], out_specs=[main, main],
        out_shape=[jax.ShapeDtypeStruct((n, dil * GROUP_W), F32)] * 2, compiler_params=_ARB(2),
    )(view(q), view(k), view(k), view(k), view(v), view(v), view(v))
    return o.reshape(L, GROUP_W), lse.reshape(L, GROUP_W)


def _dil_bwd(q, k, v, do, lse, dterm, dil):
    L = q.shape[0]
    n = L // dil
    R = DIL_RADIUS
    tq = _tile(n, 256, R)
    nq = n // tq
    view = lambda t: t.reshape(n, dil * GROUP_W)

    def kern(q_ref, kp_ref, k_ref, kn_ref, vp_ref, v_ref, vn_ref, do_ref, l_ref, dt_ref, dq_ref, dk_ref, dv_ref):
        i = pl.program_id(1)

        @pl.when(i == 0)
        def _():
            dk_ref[...] = jnp.zeros_like(dk_ref)
            dv_ref[...] = jnp.zeros_like(dv_ref)

        valid = _dil_valid(i, tq, n)
        qv, dov = q_ref[...], do_ref[...]
        ka = jnp.concatenate([kp_ref[...], k_ref[...], kn_ref[...]], axis=0)
        va = jnp.concatenate([vp_ref[...], v_ref[...], vn_ref[...]], axis=0)
        lv, dtv = l_ref[...], dt_ref[...]
        lane = lax.broadcasted_iota(jnp.int32, (tq, LANES), 1)
        first = lane < HEAD_DIM
        H = range(2)
        qm = [jnp.where(first == (hh == 0), qv, jnp.zeros_like(qv)) for hh in H]
        dom = [jnp.where(first == (hh == 0), dov, jnp.zeros_like(dov)) for hh in H]
        s = [_dot(qm[hh], ka, _NT) for hh in H]
        dp = [_dot(dom[hh], va, _NT) for hh in H]
        pn = [jnp.where(valid, jnp.exp(s[hh] * ATT_SCALE - lv[:, hh * HEAD_DIM:hh * HEAD_DIM + 1]), 0.0) for hh in H]
        dsb = [(pn[hh] * (dp[hh] - dtv[:, hh * HEAD_DIM:hh * HEAD_DIM + 1])).astype(BF16) for hh in H]
        pnb = [x.astype(BF16) for x in pn]
        dq = [_dot(dsb[hh], ka) for hh in H]
        dk2 = [_dot(dsb[hh], qm[hh], _TN) for hh in H]
        dv2 = [_dot(pnb[hh], dom[hh], _TN) for hh in H]
        dka = (dk2[0] + dk2[1]) * ATT_SCALE
        dva = dv2[0] + dv2[1]
        dq_ref[...] = jnp.where(first, dq[0], dq[1]) * ATT_SCALE
        r0 = pl.multiple_of(i * tq, R)
        dk_ref[pl.ds(r0, tq), :] += dka[R:R + tq]
        dv_ref[pl.ds(r0, tq), :] += dva[R:R + tq]

        @pl.when(i > 0)
        def _():
            dk_ref[pl.ds(r0 - R, R), :] += dka[:R]
            dv_ref[pl.ds(r0 - R, R), :] += dva[:R]

        @pl.when(i < nq - 1)
        def _():
            dk_ref[pl.ds(r0 + tq, R), :] += dka[R + tq:]
            dv_ref[pl.ds(r0 + tq, R), :] += dva[R + tq:]

    main, prev, nxt = _dil_specs(n, tq)
    whole = pl.BlockSpec((n, LANES), lambda j, i: (0, j))
    shp = jax.ShapeDtypeStruct((n, dil * GROUP_W), F32)
    dq, dk, dv = pl.pallas_call(
        kern, name=f"dil_bwd_{dil}", grid=(2 * dil, nq),
        in_specs=[main, prev, main, nxt, prev, main, nxt, main, main, main], out_specs=[main, whole, whole],
        out_shape=[shp] * 3, compiler_params=_ARB(2),
    )(view(q), view(k), view(k), view(k), view(v), view(v), view(v), view(do), view(lse), view(dterm))
    return dq.reshape(L, GROUP_W), dk.reshape(L, GROUP_W), dv.reshape(L, GROUP_W)


def _dil_weights(lses):
    m = jnp.maximum(jnp.maximum(lses[0], lses[1]), lses[2])
    e = [jnp.exp(l - m) for l in lses]
    tot = e[0] + e[1] + e[2]
    return [x / tot for x in e]


def _dilated_fwd(q, k, v):
    res = [_dil_fwd(q, k, v, dil) for _, dil in DIL_PAIRS]

    def body(i, nt, o0, o1, o2, l0, l1, l2):
        w = _dil_weights((l0, l1, l2))
        return w[0] * o0 + w[1] * o1 + w[2] * o2

    ins = [Row(r[0]) for r in res] + [Row(r[1]) for r in res]
    return _rows("dil_combine", body, 512, ins, [("row", GROUP_W, BF16)])[0], res


def _dilated_bwd(dy, dy_cb, q, k, v, saved, cosf, sinf):
    def split(i, nt, dy, o0, o1, o2, l0, l1, l2, bd):
        w = _dil_weights((l0, l1, l2))
        y = w[0] * o0 + w[1] * o1 + w[2] * o2
        dyy = _head_sum(dy * y, bd)
        return tuple(wg * dy for wg in w) + tuple(wg * dyy for wg in w)

    ins = [Row(dy, GROUP_W, dy_cb)] + [Row(r[0]) for r in saved] + [Row(r[1]) for r in saved]
    outs = _rows("dil_split_bwd", split, 512, ins + [Full(_block_ones(GROUP_W, HEAD_DIM))],
                 [("row", GROUP_W, BF16)] * 3 + [("row", GROUP_W, F32)] * 3)
    g = [_dil_bwd(q, k, v, outs[b], saved[b][1], outs[3 + b], dil) for b, (_, dil) in enumerate(DIL_PAIRS)]

    def finish(i, nt, q0, q1, q2, k0, k1, k2, v0, v1, v2, c, s):
        dq, dk = q0 + q1 + q2, k0 + k1 + k2
        return dq * c + _swap_halves(dq * s), dk * c + _swap_halves(dk * s), v0 + v1 + v2

    ins = [Row(g[b][t]) for t in range(3) for b in range(3)] + [Row(cosf), Row(sinf)]
    return _rows("dil_finish_bwd", finish, 512, ins, [("row", GROUP_W, BF16)] * 3)


def _layer_fwd(x, W, l, cosf, sinf, ride_a=None, ride_b=None):
    h1 = _rms_fwd(x, W["mix_norm_pre"][l][None], "mix_norm")
    p = _mm(h1, W["w_in"][l], "nn", F32, "proj_in")
    ya, sa, got_a = _gla_fwd(p, W, l, ride_a)
    qb, kb, vb, qd, kd, vd = _attn_prep(p, cosf, sinf)
    bias = _na_bias(W["na_rpb"][l])
    yb, got_b = _na_fwd(qb, kb, vb, bias, ride_b)
    yc, sc = _lru_fwd(p, W, l)
    yd, sd = _dilated_fwd(qd, kd, vd)
    ycat = jnp.concatenate([ya, yb, yc, yd], axis=1)
    ymix = _mm(ycat, W["w_out"][l], "nn", F32, "proj_out", tm=1024)
    xm = _rms_resid_fwd(x, ymix, W["mix_norm_post"][l][None], "mix_resid")
    h2 = _rms_fwd(xm, W["ffn_norm_pre"][l][None], "ffn_norm")
    gu, act = _ffn_in_swiglu(h2, W["ffn_w_in"][l])
    f = _mm(act, W["ffn_w_out"][l], "nn", F32, "ffn_out")
    xo = _rms_resid_fwd(xm, f, W["ffn_norm_post"][l][None], "ffn_resid")
    saved = dict(x=x, h1=h1, p=p, sa=sa, att=(qb, kb, vb, qd, kd, vd), bias=bias, sc=sc, sd=sd, ycat=ycat, ymix=ymix,
                 xm=xm, h2=h2, gu=gu, act=act, f=f)
    return xo, saved, got_a, got_b


def _layer_bwd(dxo, W, l, S, cosf, sinf, ride_a=None, ride_b=None):
    g = {}
    df, g["ffn_norm_post"] = _rms_bwd(dxo, S["f"], W["ffn_norm_post"][l][None], "ffn_resid_bwd", out_dtype=BF16)
    g["ffn_w_out"] = _mm(S["act"], df, "tn", BF16, "ffn_out_dw", tm=256, tk=4096)
    dgu = _ffn_out_dx_swiglu(df, W["ffn_w_out"][l], S["gu"])
    dh2 = _mm(dgu, W["ffn_w_in"][l], "nt", F32, "ffn_in_dx")
    g["ffn_w_in"] = _mm(S["h2"], dgu, "tn", BF16, "ffn_in_dw", tm=1024, tn=512, tk=4096)
    dxm, g["ffn_norm_pre"] = _rms_bwd(dh2, S["xm"], W["ffn_norm_pre"][l][None], "ffn_norm_bwd", resid=dxo)
    dymix, g["mix_norm_post"] = _rms_bwd(dxm, S["ymix"], W["mix_norm_post"][l][None], "mix_resid_bwd", out_dtype=BF16)
    dycat = _mm(dymix, W["w_out"][l], "nt", F32, "proj_out_dx", tm=1024)
    g["w_out"] = _mm(S["ycat"], dymix, "tn", BF16, "proj_out_dw", tm=1024, tn=512, tk=4096)
    p = S["p"]
    qb, kb, vb, qd, kd, vd = S["att"]
    (dqa, dka, dva, dga, dz), ga, got_a = _gla_bwd(dycat, 0, p, W, l, S["sa"], ride_a)
    (dqb, dkb, dvb, dbias), got_b = _na_bwd(dycat, 2, qb, kb, vb, S["bias"], ride_b)
    g["na_rpb"] = _na_bias_bwd(dbias)
    dxc, dgc, gc = _lru_bwd(dycat, 2, p, W, l, S["sc"])
    dqd, dkd, dvd = _dilated_bwd(dycat, 3, qd, kd, vd, S["sd"], cosf, sinf)
    g.update(ga)
    g.update(gc)
    dp = jnp.concatenate([dqa, dka, dva, dga, dqb, dkb.astype(BF16), dvb.astype(BF16), dxc, dgc, dqd, dkd, dvd, dz], axis=1)
    dh1 = _mm(dp, W["w_in"][l], "nt", F32, "proj_in_dx")
    g["w_in"] = _mm(S["h1"], dp, "tn", BF16, "proj_in_dw", tm=1024, tn=640, tk=4096)
    dx, g["mix_norm_pre"] = _rms_bwd(dh1, S["x"], W["mix_norm_pre"][l][None], "mix_norm_bwd", resid=dxm)
    for n in ("ffn_norm_post", "ffn_norm_pre", "mix_norm_post", "mix_norm_pre"):
        g[n] = g[n][0]
    return dx, g, got_a, got_b


MESH_AXES = ("x", "y", "c")


class Xfer:
    def __init__(self, arr, kind):
        self.arr, self.kind = arr, kind
        shp = arr.shape
        if kind == "all":
            self.out = (N_DEV,) + shp
        elif kind == "slot":
            self.out = shp
        elif kind == "rows":
            self.r = shp[1] // N_DEV
            self.out = (N_DEV, shp[0], self.r, shp[2])
        else:
            self.r = shp[1]
            self.out = (shp[0], N_DEV * shp[1], shp[2])

    def src(self, ref, peer):
        if self.kind == "slot":
            return ref.at[peer]
        if self.kind == "rows":
            return ref.at[:, pl.ds(peer * self.r, self.r), :]
        return ref

    def dst(self, ref, me):
        if self.kind == "place":
            return ref.at[:, pl.ds(me * self.r, self.r), :]
        return ref.at[me]


class Exchange:
    def __init__(self, items):
        n = len(items)
        self.items = items
        self.arrays = [it.arr for it in items]
        self.specs = [pl.BlockSpec(memory_space=pl.ANY)] * n
        self.out_shape = [jax.ShapeDtypeStruct(it.out, it.arr.dtype) for it in items]
        self.scratch = [pltpu.SemaphoreType.DMA((n * (N_DEV - 1),)), pltpu.SemaphoreType.DMA((n * (N_DEV - 1),)),
                        pltpu.SemaphoreType.DMA((n,))]

    def copies(self, ins, outs, sems):
        send_sems, recv_sems, local_sems = sems
        x, y, c = (lax.axis_index(a) for a in MESH_AXES)
        me = 4 * x + 2 * y + c
        out = []
        for t, it in enumerate(self.items):
            out.append(pltpu.make_async_copy(it.src(ins[t], me), it.dst(outs[t], me), local_sems.at[t]))
            for k in range(1, N_DEV):
                px, py, pc = x ^ ((k >> 2) & 1), y ^ ((k >> 1) & 1), c ^ (k & 1)
                s = t * (N_DEV - 1) + k - 1
                out.append(pltpu.make_async_remote_copy(
                    src_ref=it.src(ins[t], 4 * px + 2 * py + pc), dst_ref=it.dst(outs[t], me),
                    send_sem=send_sems.at[s], recv_sem=recv_sems.at[s], device_id=(px, py, pc),
                    device_id_type=pl.DeviceIdType.MESH))
        return out

    def start(self, ins, outs, sems):
        for cp in self.copies(ins, outs, sems):
            cp.start()

    def wait(self, ins, outs, sems):
        for cp in self.copies(ins, outs, sems):
            cp.wait()


def _exchange(items, name):
    ex = Exchange(items)
    n = len(items)

    def body(*refs):
        ex.start(refs[:n], refs[n:2 * n], refs[2 * n:])
        ex.wait(refs[:n], refs[n:2 * n], refs[2 * n:])

    return pl.pallas_call(body, name=name, out_shape=ex.out_shape, in_specs=ex.specs, out_specs=ex.specs,
                          scratch_shapes=ex.scratch)(*ex.arrays)


def _call(kern, arrays, ride, edges, *, name, grid, in_specs, out_specs, out_shape, scratch_shapes):
    params = _ARB(len(grid))
    if ride is None:
        return pl.pallas_call(kern, name=name, grid=grid, in_specs=in_specs, out_specs=out_specs, out_shape=out_shape,
                              scratch_shapes=scratch_shapes, compiler_params=params)(*arrays), None
    ni, no, ns, nx = len(in_specs), len(out_specs), len(scratch_shapes), len(ride.items)

    def wrapped(*refs):
        ins, xin = refs[:ni], refs[ni:ni + nx]
        outs, xout = refs[ni + nx:ni + nx + no], refs[ni + nx + no:ni + 2 * nx + no]
        scr, sems = refs[ni + 2 * nx + no:ni + 2 * nx + no + ns], refs[ni + 2 * nx + no + ns:]
        first, last = edges()

        @pl.when(first)
        def _():
            ride.start(xin, xout, sems)

        kern(*ins, *outs, *scr)

        @pl.when(last)
        def _():
            ride.wait(xin, xout, sems)

    res = pl.pallas_call(
        wrapped, name=name, grid=grid, in_specs=list(in_specs) + ride.specs, out_specs=list(out_specs) + ride.specs,
        out_shape=list(out_shape) + ride.out_shape, scratch_shapes=list(scratch_shapes) + ride.scratch,
        compiler_params=params)(*arrays, *ride.arrays)
    return res[:no], res[no:]


def _column_segments(width, permuted):
    z0, z1, zn = 4 * GROUP_W, 4 * GROUP_W + 2 * GLA_RANK, 12 * GROUP_W
    segs = []
    for d in range(N_DEV):
        lo, hi = d * width, (d + 1) * width
        if not permuted:
            segs.append([(0, width, lo)])
            continue
        runs = []
        for a, b, shift in ((0, z0, 0), (z0, z1, zn - z0), (z1, 10 ** 9, -(z1 - z0))):
            s, e = max(lo, a), min(hi, b)
            if s < e:
                runs.append((s - lo, e - lo, s + shift))
        segs.append(runs)
    return segs


def _cols_from_pieces(pieces, segs, cols, name):
    _, R, w = pieces.shape
    tm = _tile(R, 256, 16)
    used = max(f + (b - a) for runs in segs for a, b, f in runs)

    def kern(p_ref, o_ref):
        for d, runs in enumerate(segs):
            for a, b, f in runs:
                o_ref[:, f:f + (b - a)] = p_ref[d, :, a:b]
        if used < cols:
            o_ref[:, used:cols] = jnp.zeros((tm, cols - used), o_ref.dtype)

    return pl.pallas_call(
        kern, name=name, grid=(R // tm,), in_specs=[pl.BlockSpec((N_DEV, tm, w), lambda i: (0, i, 0))],
        out_specs=pl.BlockSpec((tm, cols), lambda i: (i, 0)), out_shape=jax.ShapeDtypeStruct((R, cols), pieces.dtype),
        compiler_params=_ARB(1),
    )(pieces)


def _pieces_from_cols(full, segs, w, name):
    R, cols = full.shape
    tm = _tile(R, 256, 16)

    def kern(f_ref, o_ref):
        for d, runs in enumerate(segs):
            for a, b, f in runs:
                o_ref[d, :, a:b] = f_ref[:, f:f + (b - a)]

    return pl.pallas_call(
        kern, name=name, grid=(R // tm,), in_specs=[pl.BlockSpec((tm, cols), lambda i: (i, 0))],
        out_specs=pl.BlockSpec((N_DEV, tm, w), lambda i: (0, i, 0)),
        out_shape=jax.ShapeDtypeStruct((N_DEV, R, w), full.dtype), compiler_params=_ARB(1),
    )(full)


def _sum_slots(recv, name):
    n, R, C = recv.shape
    tm = _tile(R, 256, 16)

    def kern(*refs):
        acc = refs[0][...].astype(F32)
        for r in refs[1:n]:
            acc = acc + r[...].astype(F32)
        refs[n][...] = acc

    return pl.pallas_call(
        kern, name=name, grid=(R // tm,),
        in_specs=[pl.BlockSpec((None, tm, C), lambda i, _s=s: (_s, i, 0)) for s in range(n)],
        out_specs=pl.BlockSpec((tm, C), lambda i: (i, 0)), out_shape=jax.ShapeDtypeStruct((R, C), F32),
        compiler_params=_ARB(1),
    )(*([recv] * n))


BIG = (("w_in", 2), ("w_out", 1), ("ffn_w_in", 2), ("ffn_w_out", 1))
SMALL_SHARDED = ("gla_w_gate", "gla_b_gate", "lru_conv_w", "lru_b_a", "lru_b_x", "lru_lambda")
REPLICATED = ("mix_norm_pre", "mix_norm_post", "gla_norm", "na_rpb", "lru_conv_b", "lru_w_a", "lru_w_x",
              "ffn_norm_pre", "ffn_norm_post")
WEIGHTS = ("mix_norm_pre", "mix_norm_post", "w_in", "gla_w_gate", "gla_b_gate", "gla_norm", "na_rpb", "lru_conv_w",
           "lru_conv_b", "lru_w_a", "lru_b_a", "lru_w_x", "lru_b_x", "lru_lambda", "w_out", "ffn_norm_pre",
           "ffn_norm_post", "ffn_w_in", "ffn_w_out")
FLAT_C = 1024


def _to_rows(vec, row_unit):
    n = vec.shape[-1]
    rows = -(-n // (FLAT_C * row_unit)) * row_unit
    pad = [(0, 0)] * (vec.ndim - 1) + [(0, rows * FLAT_C - n)]
    return jnp.pad(vec, pad).reshape(vec.shape[:-1] + (rows, FLAT_C))


def _unshard(parts, axis):
    t = jnp.moveaxis(parts, 0, axis)
    shp = list(t.shape)
    return t.reshape(shp[:axis] + [shp[axis] * shp[axis + 1]] + shp[axis + 2:])


def _shards(full, axis):
    shp = list(full.shape)
    t = full.reshape(shp[:axis] + [N_DEV, shp[axis] // N_DEV] + shp[axis + 1:])
    return jnp.moveaxis(t, axis, 0)


def _two_d(a):
    return a.reshape(-1, a.shape[-1])


def _weight_items(W, l):
    bf = lambda n: W[n][l].astype(BF16)
    return ([Xfer(bf("w_in"), "all"), Xfer(bf("w_out")[None], "place")],
            [Xfer(bf("ffn_w_in"), "all"), Xfer(bf("ffn_w_out")[None], "place")])


def _unpack_weights(full, W, got_a, got_b):
    w_in_w, ffn_w = W["w_in"].shape[-1], W["ffn_w_in"].shape[-1]
    full["w_in"].append(_cols_from_pieces(got_a[0], _column_segments(w_in_w, True), P_COLS, "unpack_w_in"))
    full["w_out"].append(got_a[1][0])
    full["ffn_w_in"].append(_cols_from_pieces(got_b[0], _column_segments(ffn_w, False), N_DEV * ffn_w, "unpack_ffn_w_in"))
    full["ffn_w_out"].append(got_b[1][0])


def _grad_items(g, W):
    w_in_w, ffn_w = W["w_in"].shape[-1], W["ffn_w_in"].shape[-1]
    p_in = _pieces_from_cols(g["w_in"], _column_segments(w_in_w, True), w_in_w, "pack_w_in")
    p_ffn = _pieces_from_cols(g["ffn_w_in"], _column_segments(ffn_w, False), ffn_w, "pack_ffn_w_in")
    return ([Xfer(p_in, "slot"), Xfer(g["w_out"][None], "rows")], [Xfer(p_ffn, "slot"), Xfer(g["ffn_w_out"][None], "rows")])


def _sum_big(got_a, got_b):
    out = {}
    for n, r in zip(("w_in", "w_out", "ffn_w_in", "ffn_w_out"), tuple(got_a) + tuple(got_b)):
        out[n] = _sum_slots(r.reshape(N_DEV, -1, r.shape[-1]), "sum_" + n)
    return out


def _train(x, target, W):
    L = x.shape[0]
    depth = W["w_in"].shape[0]
    cosf, sinf = _rope_tables(L)
    small = jnp.concatenate([W[n].reshape(-1) for n in SMALL_SHARDED])
    small16 = _to_rows(lax.bitcast_convert_type(small, jnp.uint16).reshape(-1), 16)
    items_a, items_b = _weight_items(W, 0)
    got = _exchange(items_a + items_b + [Xfer(small16, "all")], "gather_first")
    full = dict(W, w_in=[], w_out=[], ffn_w_in=[], ffn_w_out=[])
    _unpack_weights(full, W, got[0:2], got[2:4])
    sm = lax.bitcast_convert_type(got[4].reshape(N_DEV, -1)[:, :2 * small.size].reshape(N_DEV, small.size, 2), F32)
    off = 0
    for n in SMALL_SHARDED:
        full[n] = _unshard(sm[:, off:off + W[n].size].reshape((N_DEV,) + W[n].shape), W[n].ndim - 1)
        off += W[n].size

    saved = []
    for l in range(depth):
        rides = [Exchange(it) for it in _weight_items(W, l + 1)] if l + 1 < depth else [None, None]
        x, S, got_a, got_b = _layer_fwd(x, full, l, cosf, sinf, *rides)
        saved.append(S)
        if l + 1 < depth:
            _unpack_weights(full, W, got_a, got_b)
    loss, dx = _loss_fwd_bwd(x, target)

    grads, big, rides = [None] * depth, [None] * depth, [None, None]
    for l in reversed(range(depth)):
        dx, grads[l], got_a, got_b = _layer_bwd(dx, full, l, saved[l], cosf, sinf, *rides)
        if l + 1 < depth:
            big[l + 1] = _sum_big(got_a, got_b)
        rides = [Exchange(it) for it in _grad_items(grads[l], W)] if l > 0 else [None, None]
    G = {n: jnp.stack([g[n] for g in grads]) for n in SMALL_SHARDED + REPLICATED}
    small_g = jnp.concatenate([_shards(G[n], G[n].ndim - 1).reshape(N_DEV, -1) for n in SMALL_SHARDED], axis=1)
    repl_g = jnp.concatenate([G[n].reshape(-1) for n in REPLICATED])
    items_a, items_b = _grad_items(grads[0], W)
    got = _exchange(items_a + items_b + [Xfer(_to_rows(small_g, 8), "slot"), Xfer(_to_rows(repl_g, 8), "all")],
                    "exchange_last")
    big[0] = _sum_big(got[0:2], got[2:4])
    out = {n: jnp.stack([b[n] for b in big]).reshape(W[n].shape) for n, _ in BIG}
    for names, r, tag in ((SMALL_SHARDED, got[4], "sum_small"), (REPLICATED, got[5], "sum_replicated")):
        flat, off = _sum_slots(r, tag).reshape(-1), 0
        for n in names:
            out[n] = flat[off:off + W[n].size].reshape(W[n].shape)
            off += W[n].size
    return loss, dx, out


def _update(W, G, M, V):
    delta, new_m, new_v = {}, {}, {}
    for n, _ in BIG:
        two_d = lambda a: a.reshape(-1, a.shape[-1])
        d, m, v = _adamw(two_d(W[n]), two_d(G[n]), two_d(M[n]), two_d(V[n]), "adamw_" + n)
        delta[n], new_m[n], new_v[n] = (t.reshape(W[n].shape) for t in (d, m, v))
    rest = SMALL_SHARDED + REPLICATED
    pack = lambda D: _to_rows(jnp.concatenate([D[n].reshape(-1) for n in rest]), 16)
    d, m, v = _adamw(pack(W), pack(G), pack(M), pack(V), "adamw_small")
    off = 0
    for n in rest:
        sl = lambda t: t.reshape(-1)[off:off + W[n].size].reshape(W[n].shape)
        delta[n], new_m[n], new_v[n] = sl(d), sl(m), sl(v)
        off += W[n].size
    return delta, new_m, new_v


def kernel(x, mix_norm_pre, mix_norm_post, w_in, gla_w_gate, gla_b_gate, gla_norm, na_rpb, lru_conv_w, lru_conv_b, lru_w_a, lru_b_a, lru_w_x, lru_b_x, lru_lambda, w_out, ffn_norm_pre, ffn_norm_post, ffn_w_in, ffn_w_out, loss_target, m_mix_norm_pre, m_mix_norm_post, m_w_in, m_gla_w_gate, m_gla_b_gate, m_gla_norm, m_na_rpb, m_lru_conv_w, m_lru_conv_b, m_lru_w_a, m_lru_b_a, m_lru_w_x, m_lru_b_x, m_lru_lambda, m_w_out, m_ffn_norm_pre, m_ffn_norm_post, m_ffn_w_in, m_ffn_w_out, v_mix_norm_pre, v_mix_norm_post, v_w_in, v_gla_w_gate, v_gla_b_gate, v_gla_norm, v_na_rpb, v_lru_conv_w, v_lru_conv_b, v_lru_w_a, v_lru_b_a, v_lru_w_x, v_lru_b_x, v_lru_lambda, v_w_out, v_ffn_norm_pre, v_ffn_norm_post, v_ffn_w_in, v_ffn_w_out):
    W = dict(zip(WEIGHTS, (mix_norm_pre, mix_norm_post, w_in, gla_w_gate, gla_b_gate, gla_norm, na_rpb, lru_conv_w, lru_conv_b, lru_w_a, lru_b_a, lru_w_x, lru_b_x, lru_lambda, w_out, ffn_norm_pre, ffn_norm_post, ffn_w_in, ffn_w_out)))
    M = dict(zip(WEIGHTS, (m_mix_norm_pre, m_mix_norm_post, m_w_in, m_gla_w_gate, m_gla_b_gate, m_gla_norm, m_na_rpb, m_lru_conv_w, m_lru_conv_b, m_lru_w_a, m_lru_b_a, m_lru_w_x, m_lru_b_x, m_lru_lambda, m_w_out, m_ffn_norm_pre, m_ffn_norm_post, m_ffn_w_in, m_ffn_w_out)))
    V = dict(zip(WEIGHTS, (v_mix_norm_pre, v_mix_norm_post, v_w_in, v_gla_w_gate, v_gla_b_gate, v_gla_norm, v_na_rpb, v_lru_conv_w, v_lru_conv_b, v_lru_w_a, v_lru_b_a, v_lru_w_x, v_lru_b_x, v_lru_lambda, v_w_out, v_ffn_norm_pre, v_ffn_norm_post, v_ffn_w_in, v_ffn_w_out)))
    loss, dx, G = _train(x[0], loss_target[0], W)
    loss = lax.psum(loss, MESH_AXES)
    delta, new_m, new_v = _update(W, G, M, V)
    return (loss, dx[None], *[G[n] for n in WEIGHTS], *[delta[n] for n in WEIGHTS], *[new_m[n] for n in WEIGHTS],
            *[new_v[n] for n in WEIGHTS])
```

```python
import functools
import math

import numpy as np
import jax
import jax.numpy as jnp
from jax import lax
from jax.experimental import pallas as pl
from jax.experimental.pallas import tpu as pltpu

F32 = jnp.float32
BF16 = jnp.bfloat16

N_DEV = 8
HEAD_DIM = 64
GROUP_W = 256
GLA_RANK = 16
GLA_TAU = 16.0
GLA_CHUNK = 64
GRID_W = 64
NA_ROWS = 8
NA_COLS = 16
LRU_C = 8.0
DIL_PAIRS = ((128, 1), (512, 4), (2048, 16))
DIL_RADIUS = 64
ROPE_THETA = 10000.0
EPS = 1e-6
ATT_SCALE = HEAD_DIM ** -0.5
NEG = -1e30
LANES = 128
P_COLS = 12 * GROUP_W + LANES
Z_BLOCK = 12 * GROUP_W // LANES

ADAM_LR = 0.001
ADAM_B1 = 0.9
ADAM_B2 = 0.999
ADAM_EPS = 1e-08
ADAM_WD = 0.01
ADAM_STEP = 10

VMEM_LIMIT = 56 * 1024 * 1024
_ARB = lambda n: pltpu.CompilerParams(dimension_semantics=("arbitrary",) * n, vmem_limit_bytes=VMEM_LIMIT)


def _tile(dim, pref, unit):
    t = min(pref, dim) // unit * unit
    while t >= unit:
        if dim % t == 0:
            return t
        t -= unit
    return dim


def _mm(a, b, mode, out_dtype, name, tm=512, tn=None, tk=None):
    if mode == "nn":
        (M, K), (_, N) = a.shape, b.shape
    elif mode == "nt":
        (M, K), (N, _) = a.shape, b.shape
    else:
        (K, M), (_, N) = a.shape, b.shape
    tm = _tile(M, tm, LANES if mode == "tn" else 8)
    tn = _tile(N, tn or N, LANES)
    tk = _tile(K, tk or K, LANES)
    nk = K // tk
    dims = {"nn": (((1,), (0,)), ((), ())), "nt": (((1,), (1,)), ((), ())), "tn": (((0,), (0,)), ((), ()))}[mode]

    def kern(a_ref, b_ref, o_ref, *acc):
        part = lax.dot_general(a_ref[...].astype(BF16), b_ref[...].astype(BF16), dims, preferred_element_type=F32)
        if nk == 1:
            o_ref[...] = part.astype(out_dtype)
            return
        k = pl.program_id(2)

        @pl.when(k == 0)
        def _():
            acc[0][...] = part

        @pl.when(jnp.logical_and(k > 0, k < nk - 1))
        def _():
            acc[0][...] += part

        @pl.when(k == nk - 1)
        def _():
            o_ref[...] = (acc[0][...] + part).astype(out_dtype)

    a_spec = pl.BlockSpec((tk, tm), lambda i, j, k: (k, i)) if mode == "tn" else pl.BlockSpec((tm, tk), lambda i, j, k: (i, k))
    b_spec = pl.BlockSpec((tn, tk), lambda i, j, k: (j, k)) if mode == "nt" else pl.BlockSpec((tk, tn), lambda i, j, k: (k, j))
    return pl.pallas_call(
        kern, name=name, grid=(M // tm, N // tn, nk),
        in_specs=[a_spec, b_spec], out_specs=pl.BlockSpec((tm, tn), lambda i, j, k: (i, j)),
        out_shape=jax.ShapeDtypeStruct((M, N), out_dtype),
        scratch_shapes=[pltpu.VMEM((tm, tn), F32)] if nk > 1 else [],
        compiler_params=_ARB(3),
    )(a, b)


class Row:
    def __init__(self, a, width=None, cb=0, halo=False):
        self.a, self.width, self.cb, self.halo = a, width, cb, halo


class Full:
    def __init__(self, a):
        self.a = a


HALO = 8


def _rows(name, body, tm, ins, outs):
    L = next(s.a.shape[0] for s in ins if isinstance(s, Row))
    tm = _tile(L, tm, 16)
    nt = L // tm
    nb8 = L // HALO
    step = tm // HALO
    in_specs, arrays, layout = [], [], []
    for s in ins:
        if isinstance(s, Full):
            nd = s.a.ndim
            in_specs.append(pl.BlockSpec(s.a.shape, lambda i, _nd=nd: (0,) * _nd))
            arrays.append(s.a)
            layout.append(1)
        else:
            w = s.width or s.a.shape[1]
            in_specs.append(pl.BlockSpec((tm, w), lambda i, _cb=s.cb: (i, _cb)))
            arrays.append(s.a)
            if s.halo:
                in_specs.append(pl.BlockSpec((HALO, w), lambda i, _cb=s.cb: (jnp.maximum(i * step - 1, 0), _cb)))
                in_specs.append(pl.BlockSpec((HALO, w), lambda i, _cb=s.cb: (jnp.minimum((i + 1) * step, nb8 - 1), _cb)))
                arrays += [s.a, s.a]
                layout.append(3)
            else:
                layout.append(1)
    out_specs, out_shapes = [], []
    for kind, shp, dt in outs:
        if kind == "row":
            out_specs.append(pl.BlockSpec((tm, shp), lambda i: (i, 0)))
            out_shapes.append(jax.ShapeDtypeStruct((L, shp), dt))
        else:
            out_specs.append(pl.BlockSpec(shp, lambda i, _n=len(shp): (0,) * _n))
            out_shapes.append(jax.ShapeDtypeStruct(shp, dt))
    n_in = len(arrays)

    def kern(*refs):
        i = pl.program_id(0)
        vals, p = [], 0
        for n in layout:
            if n == 1:
                vals.append(refs[p][...])
            else:
                vals.append((refs[p + 1][...], refs[p][...], refs[p + 2][...]))
            p += n
        res = body(i, nt, *vals)
        if not isinstance(res, (tuple, list)):
            res = (res,)
        for (kind, shp, dt), o_ref, r in zip(outs, refs[n_in:], res):
            if kind == "row":
                o_ref[...] = r.astype(dt)
            else:
                @pl.when(i == 0)
                def _(o_ref=o_ref):
                    o_ref[...] = jnp.zeros_like(o_ref)
                o_ref[...] += r.astype(dt)

    res = pl.pallas_call(
        kern, name=name, grid=(nt,), in_specs=in_specs, out_specs=out_specs, out_shape=out_shapes,
        compiler_params=_ARB(1),
    )(*arrays)
    return res


def _shift(h, o, i, nt):
    prev, cur, nxt = h
    if o == 0:
        return cur
    tm = cur.shape[0]
    cat = jnp.concatenate([prev, cur, nxt], axis=0)
    sh = pltpu.roll(cat, (-o) % (tm + 2 * HALO), axis=0)[HALO:HALO + tm]
    row = lax.broadcasted_iota(jnp.int32, cur.shape, 0)
    if o < 0:
        ok = jnp.logical_or(i > 0, row >= -o)
    else:
        ok = jnp.logical_or(i < nt - 1, row < tm - o)
    return jnp.where(ok, sh, 0.0)


def _colsum(v):
    return jnp.sum(v, axis=0, keepdims=True)


def _sigmoid(x):
    return 1.0 / (1.0 + jnp.exp(-x))


def _softplus(x):
    return jnp.maximum(x, 0.0) + jnp.log1p(jnp.exp(-jnp.abs(x)))


def _silu(x):
    return x * _sigmoid(x)


def _dsilu(x):
    s = _sigmoid(x)
    return s * (1.0 + x * (1.0 - s))


_GELU_C = math.sqrt(2.0 / math.pi)


def _gelu(x):
    return 0.5 * x * (1.0 + jnp.tanh(_GELU_C * (x + 0.044715 * x * x * x)))


def _dgelu(x):
    t = jnp.tanh(_GELU_C * (x + 0.044715 * x * x * x))
    return 0.5 * (1.0 + t) + 0.5 * x * (1.0 - t * t) * _GELU_C * (1.0 + 3.0 * 0.044715 * x * x)


def _head_sum(v, bd):
    return jnp.dot(v, bd, precision=lax.Precision.HIGHEST, preferred_element_type=F32)


def _block_ones(n, blk):
    r = np.arange(n)
    return jnp.asarray((r[:, None] // blk == r[None, :] // blk).astype(np.float32))


def _rms_fwd(x, g, name):
    def body(i, nt, x, g):
        r = lax.rsqrt(jnp.mean(x * x, axis=-1, keepdims=True) + EPS)
        return x * r * g
    return _rows(name, body, 256, [Row(x), Full(g)], [("row", x.shape[1], BF16)])[0]


def _rms_resid_fwd(x, y, g, name):
    def body(i, nt, x, y, g):
        r = lax.rsqrt(jnp.mean(y * y, axis=-1, keepdims=True) + EPS)
        return x + y * r * g
    return _rows(name, body, 256, [Row(x), Row(y), Full(g)], [("row", x.shape[1], F32)])[0]


def _rms_bwd(dy, x, g, name, resid=None, out_dtype=F32):
    D = x.shape[1]

    def body(i, nt, dy, x, g, *rest):
        dy = dy.astype(F32)
        r = lax.rsqrt(jnp.mean(x * x, axis=-1, keepdims=True) + EPS)
        xh = x * r
        dxh = dy * g
        dx = r * (dxh - xh * jnp.mean(dxh * xh, axis=-1, keepdims=True))
        if rest:
            dx = dx + rest[0]
        return dx, _colsum(dy * xh)

    ins = [Row(dy), Row(x), Full(g)] + ([Row(resid)] if resid is not None else [])
    return _rows(name, body, 256, ins, [("row", D, out_dtype), ("acc", (1, D), F32)])


def _ffn_in_swiglu(h, w):
    (M, K), N = h.shape, w.shape[1]
    F = N // 2
    tm = _tile(M, 256, 16)

    def kern(a_ref, b_ref, gu_ref, act_ref):
        gu = _dot(a_ref[...], b_ref[...])
        gu_ref[...] = gu
        act_ref[...] = (_silu(gu[:, :F]) * gu[:, F:]).astype(BF16)

    return pl.pallas_call(
        kern, name="ffn_in_swiglu", grid=(M // tm,),
        in_specs=[pl.BlockSpec((tm, K), lambda i: (i, 0)), pl.BlockSpec((K, N), lambda i: (0, 0))],
        out_specs=[pl.BlockSpec((tm, N), lambda i: (i, 0)), pl.BlockSpec((tm, F), lambda i: (i, 0))],
        out_shape=[jax.ShapeDtypeStruct((M, N), F32), jax.ShapeDtypeStruct((M, F), BF16)], compiler_params=_ARB(1),
    )(h, w)


def _ffn_out_dx_swiglu(df, w, gu):
    (M, K), N = df.shape, gu.shape[1]
    F = N // 2
    tm = _tile(M, 256, 16)

    def kern(a_ref, b_ref, gu_ref, o_ref):
        da = _dot(a_ref[...], b_ref[...], _NT)
        gu = gu_ref[...]
        gate, up = gu[:, :F], gu[:, F:]
        o_ref[:, :F] = (da * up * _dsilu(gate)).astype(BF16)
        o_ref[:, F:] = (da * _silu(gate)).astype(BF16)

    return pl.pallas_call(
        kern, name="ffn_out_dx_swiglu", grid=(M // tm,),
        in_specs=[pl.BlockSpec((tm, K), lambda i: (i, 0)), pl.BlockSpec((F, K), lambda i: (0, 0)),
                  pl.BlockSpec((tm, N), lambda i: (i, 0))],
        out_specs=pl.BlockSpec((tm, N), lambda i: (i, 0)), out_shape=jax.ShapeDtypeStruct((M, N), BF16),
        compiler_params=_ARB(1),
    )(df, w, gu)


def _loss_fwd_bwd(y, target):
    D = y.shape[1]

    def body(i, nt, y, t):
        err = y - t
        part = 0.5 * jnp.sum(jnp.mean(err * err, axis=-1, keepdims=True), axis=0, keepdims=True)
        return err * (1.0 / D), jnp.broadcast_to(part, (1, LANES))
    dy, loss = _rows("loss", body, 256, [Row(y), Row(target)], [("row", D, F32), ("acc", (1, LANES), F32)])
    return loss[0, 0], dy


def _adamw(w, g, m, v, name):
    C = w.shape[1]
    bc1 = 1.0 - ADAM_B1 ** ADAM_STEP
    bc2 = 1.0 - ADAM_B2 ** ADAM_STEP

    def body(i, nt, w, g, m, v):
        m = ADAM_B1 * m + (1.0 - ADAM_B1) * g
        v = ADAM_B2 * v + (1.0 - ADAM_B2) * (g * g)
        delta = -ADAM_LR * ((m / bc1) / (jnp.sqrt(v / bc2) + ADAM_EPS) + ADAM_WD * w)
        return delta, m, v
    return _rows(name, body, 256, [Row(w), Row(g), Row(m), Row(v)], [("row", C, F32)] * 3)


def _expm1(x):
    return jnp.tanh(0.5 * x) * (jnp.exp(x) + 1.0)


def _lru_gates(xh, i, nt, cw, cb, wa, wx, ba, bx, lam):
    xc = cb
    for j in range(4):
        xc = xc + cw[j:j + 1] * _shift(xh, j - 2, i, nt)
    xcb = xc.astype(BF16)
    gates = []
    for e in range(2):
        r = _sigmoid(jnp.dot(xcb, wa[e], preferred_element_type=F32) + ba[e:e + 1])
        ig = _sigmoid(jnp.dot(xcb, wx[e], preferred_element_type=F32) + bx[e:e + 1])
        sp = _softplus(-lam[e:e + 1])
        la = -LRU_C * r * sp
        gates.append((r, ig, sp, jnp.exp(la), jnp.sqrt(-_expm1(2.0 * la))))
    return xc, xcb, gates


def _scan2(af, uf, ab, ub, adjoint, name):
    L, W = af.shape
    tm = _tile(L, 512, 8)
    nt, nb = L // tm, tm // 8

    def blk(A, U, h, reverse, row):
        for d in (1, 2, 4):
            if reverse:
                ok, sh = row < 8 - d, 8 - d
            else:
                ok, sh = row >= d, d
            As = jnp.where(ok, pltpu.roll(A, sh, axis=0), 1.0)
            Us = jnp.where(ok, pltpu.roll(U, sh, axis=0), 0.0)
            U = A * Us + U
            A = A * As
        return A * h + U

    def kern(af_ref, uf_ref, ab_ref, ub_ref, of_ref, ob_ref, c_ref):
        @pl.when(pl.program_id(0) == 0)
        def _():
            c_ref[...] = jnp.zeros_like(c_ref)

        row = lax.broadcasted_iota(jnp.int32, (8, W), 0)
        full = lambda v: jnp.broadcast_to(v, (8, W))

        def body(j, carry):
            hF, aF, hB, aB = carry
            r0 = pl.multiple_of(j * 8, 8)
            r1 = pl.multiple_of((nb - 1 - j) * 8, 8)
            A, U = af_ref[pl.ds(r0, 8), :], uf_ref[pl.ds(r0, 8), :]
            if adjoint:
                C = jnp.where(row == 0, aF, pltpu.roll(A, 1, axis=0))
                aF = full(A[7:8])
            else:
                C = A
            H = blk(C, U, hF, False, row)
            of_ref[pl.ds(r0, 8), :] = H
            hF = full(H[7:8])
            A, U = ab_ref[pl.ds(r1, 8), :], ub_ref[pl.ds(r1, 8), :]
            if adjoint:
                C = jnp.where(row == 7, aB, pltpu.roll(A, 7, axis=0))
                aB = full(A[0:1])
            else:
                C = A
            H = blk(C, U, hB, True, row)
            ob_ref[pl.ds(r1, 8), :] = H
            hB = full(H[0:1])
            return hF, aF, hB, aB

        carry = lax.fori_loop(0, nb, body, (c_ref[0], c_ref[1], c_ref[2], c_ref[3]))
        for n in range(4):
            c_ref[n] = carry[n]

    fwd = pl.BlockSpec((tm, W), lambda i: (i, 0))
    bwd = pl.BlockSpec((tm, W), lambda i: (nt - 1 - i, 0))
    return pl.pallas_call(
        kern, name=name, grid=(nt,), in_specs=[fwd, fwd, bwd, bwd], out_specs=[fwd, bwd],
        out_shape=[jax.ShapeDtypeStruct((L, W), F32)] * 2,
        scratch_shapes=[pltpu.VMEM((4, 8, W), F32)], compiler_params=_ARB(1),
    )(af, uf, ab, ub)


def _block_diag(w):
    out = jnp.zeros((2, GROUP_W, GROUP_W), w.dtype)
    for h in range(4):
        out = out.at[:, h * 64:(h + 1) * 64, h * 64:(h + 1) * 64].set(w[:, h])
    return out.astype(BF16)


def _diag_blocks(w):
    return jnp.stack([w[:, h * 64:(h + 1) * 64, h * 64:(h + 1) * 64] for h in range(4)], axis=1)


def _lru_params(W, l):
    return [Full(W["lru_conv_w"][l]), Full(W["lru_conv_b"][l][None]), Full(_block_diag(W["lru_w_a"][l])),
            Full(_block_diag(W["lru_w_x"][l])), Full(W["lru_b_a"][l]), Full(W["lru_b_x"][l]), Full(W["lru_lambda"][l])]


def _lru_fwd(p, W, l):
    def pre(i, nt, xh, *prm):
        xc, _, g = _lru_gates(xh, i, nt, *prm)
        return g[0][3], g[0][4] * (g[0][1] * xc), g[1][3], g[1][4] * (g[1][1] * xc)

    a0, u0, a1, u1 = _rows("lru_pre", pre, 256, [Row(p, GROUP_W, 7, halo=True)] + _lru_params(W, l),
                           [("row", GROUP_W, F32)] * 4)
    hf, hb = _scan2(a0, u0, a1, u1, False, "lru_scan")
    yc = _rows("lru_post", lambda i, nt, hf, hb, gc: (hf + hb) * _gelu(gc), 512,
               [Row(hf), Row(hb), Row(p, GROUP_W, 8)], [("row", GROUP_W, BF16)])[0]
    return yc, (a0, a1, hf, hb)


def _lru_bwd(dy, dy_cb, p, W, l, saved):
    a0, a1, hf, hb = saved

    def post(i, nt, dy, hf, hb, gc):
        return dy * _gelu(gc), dy * (hf + hb) * _dgelu(gc)

    dh, dgc = _rows("lru_post_bwd", post, 512, [Row(dy, GROUP_W, dy_cb), Row(hf), Row(hb), Row(p, GROUP_W, 8)],
                    [("row", GROUP_W, F32), ("row", GROUP_W, BF16)])
    gb, gf = _scan2(a1, dh, a0, dh, True, "lru_scan_adj")

    def gates_bwd(i, nt, xh, gf, gb, hfh, hbh, cw, cb, wa, wx, ba, bx, lam):
        xc, xcb, g = _lru_gates(xh, i, nt, cw, cb, wa, wx, ba, bx, lam)
        dxc = jnp.zeros_like(xc)
        dwa, dwx, dba, dbx, dlam = [], [], [], [], []
        for e, du, hprev in ((0, gf, _shift(hfh, -1, i, nt)), (1, gb, _shift(hbh, 1, i, nt))):
            r, ig, sp, a, s = g[e]
            dxc = dxc + du * s * ig
            dla = du * hprev * a - (du * ig * xc) * a * a / s
            dza = (dla * (-LRU_C) * sp) * r * (1.0 - r)
            dzx = (du * s * xc) * ig * (1.0 - ig)
            dlam.append(_colsum(dla * r) * (LRU_C * _sigmoid(-lam[e:e + 1])))
            dba.append(_colsum(dza))
            dbx.append(_colsum(dzx))
            dzab, dzxb = dza.astype(BF16), dzx.astype(BF16)
            tn = (((0,), (0,)), ((), ()))
            nt_ = (((1,), (1,)), ((), ()))
            dwa.append(lax.dot_general(xcb, dzab, tn, preferred_element_type=F32))
            dwx.append(lax.dot_general(xcb, dzxb, tn, preferred_element_type=F32))
            dxc = dxc + lax.dot_general(dzab, wa[e], nt_, preferred_element_type=F32)
            dxc = dxc + lax.dot_general(dzxb, wx[e], nt_, preferred_element_type=F32)
        cat = lambda v: jnp.concatenate(v, axis=0)
        return dxc, jnp.stack(dwa), jnp.stack(dwx), cat(dba), cat(dbx), cat(dlam)

    dxc, dwa, dwx, dba, dbx, dlam = _rows(
        "lru_gates_bwd", gates_bwd, 256,
        [Row(p, GROUP_W, 7, halo=True), Row(gf), Row(gb), Row(hf, halo=True), Row(hb, halo=True)] + _lru_params(W, l),
        [("row", GROUP_W, F32), ("acc", (2, GROUP_W, GROUP_W), F32), ("acc", (2, GROUP_W, GROUP_W), F32),
         ("acc", (2, GROUP_W), F32), ("acc", (2, GROUP_W), F32), ("acc", (2, GROUP_W), F32)])

    def conv_bwd(i, nt, dh_, xh, cw):
        dxb = jnp.zeros_like(dh_[1])
        dcw = []
        for j in range(4):
            dxb = dxb + cw[j:j + 1] * _shift(dh_, 2 - j, i, nt)
            dcw.append(_colsum(dh_[1] * _shift(xh, j - 2, i, nt)))
        return dxb, jnp.concatenate(dcw, axis=0), _colsum(dh_[1])

    dxb, dcw, dcb = _rows("lru_conv_bwd", conv_bwd, 512,
                          [Row(dxc, halo=True), Row(p, GROUP_W, 7, halo=True), Full(W["lru_conv_w"][l])],
                          [("row", GROUP_W, BF16), ("acc", (4, GROUP_W), F32), ("acc", (1, GROUP_W), F32)])
    grads = dict(lru_conv_w=dcw, lru_conv_b=dcb[0], lru_w_a=_diag_blocks(dwa), lru_w_x=_diag_blocks(dwx),
                 lru_b_a=dba, lru_b_x=dbx, lru_lambda=dlam)
    return dxb, dgc, grads


_NT = (((1,), (1,)), ((), ()))
_TN = (((0,), (0,)), ((), ()))


def _dot(a, b, dims=None):
    if dims is None:
        return jnp.dot(a, b, preferred_element_type=F32)
    return lax.dot_general(a, b, dims, preferred_element_type=F32)


def _dot_exact(a, b):
    return jnp.dot(a, b, precision=lax.Precision.HIGHEST, preferred_element_type=F32)


def _gla_gate_w(w_gate, b_gate):
    wg = jnp.zeros((LANES, 2 * GROUP_W), F32)
    for e in range(2):
        wg = wg.at[e * GLA_RANK:(e + 1) * GLA_RANK, e * GROUP_W:(e + 1) * GROUP_W].set(w_gate[e])
    return wg.astype(BF16), b_gate.reshape(1, 2 * GROUP_W)


def _gla_gates_fwd(p, wg, bg):
    def body(i, nt, z, wg, bg):
        logit = _dot(z.astype(BF16), wg) + bg
        la = -_softplus(-logit) * (1.0 / GLA_TAU)
        return la[:, :GROUP_W], la[:, GROUP_W:]
    return _rows("gla_gates", body, 512, [Row(p, LANES, Z_BLOCK), Full(wg), Full(bg)], [("row", GROUP_W, F32)] * 2)


def _gla_gates_bwd(p, dla0, dla1, wg, bg):
    def body(i, nt, z, d0, d1, wg, bg):
        zb = z.astype(BF16)
        logit = _dot(zb, wg) + bg
        dlogit = jnp.concatenate([d0, d1], axis=1) * (1.0 / GLA_TAU) * _sigmoid(-logit)
        dlb = dlogit.astype(BF16)
        return _dot(dlb, wg, _NT), _dot(zb, dlb, _TN), _colsum(dlogit)
    return _rows("gla_gates_bwd", body, 512, [Row(p, LANES, Z_BLOCK), Row(dla0), Row(dla1), Full(wg), Full(bg)],
                 [("row", LANES, BF16), ("acc", (LANES, 2 * GROUP_W), F32), ("acc", (1, 2 * GROUP_W), F32)])


def _gla_order(reverse):
    t = np.arange(GLA_CHUNK)
    m = (t[None, :] >= t[:, None]) if reverse else (t[None, :] <= t[:, None])
    return m.astype(np.float32), (32, 0) if reverse else (31, 63)


def _stack_heads(x, bd):
    return jnp.where(bd, jnp.concatenate([x] * 4, axis=0), 0.0)


def _diag_heads(r, bd):
    r = jnp.where(bd, r, 0.0)
    return r[0:64] + r[64:128] + r[128:192] + r[192:256]


def _gla_factors(q_ref, k_ref, rows, b, mid, last):
    bm, bl = b[mid:mid + 1], b[last:last + 1]
    qs = q_ref[rows, :] * ATT_SCALE
    k = k_ref[rows, :]
    P, N, E, Fd = jnp.exp(b - bm), jnp.exp(bm - b), jnp.exp(b), jnp.exp(bl - b)
    return (P, N, E, Fd, jnp.exp(bl)), (qs * P, k * N, qs * E, k * Fd)


def _gla_specs(L, walk_up):
    tm = _tile(L, 512, GLA_CHUNK)
    nt, nc = L // tm, tm // GLA_CHUNK
    specs = []
    for up in walk_up:
        t = (lambda i: i) if up else (lambda i: nt - 1 - i)
        specs.append(dict(
            col=lambda cb, _t=t: pl.BlockSpec((tm, GROUP_W), lambda i: (_t(i), cb)),
            row=pl.BlockSpec((tm, GROUP_W), lambda i, _t=t: (_t(i), 0)),
            state=pl.BlockSpec((nc, GROUP_W, GROUP_W), lambda i, _t=t: (_t(i), 0, 0))))
    return nt, nc, specs


def _gla_chunk_fwd(p, la0, la1, ride=None):
    L = la0.shape[0]
    nt, nc, specs = _gla_specs(L, (True, False))
    orders = [_gla_order(False), _gla_order(True)]

    def kern(q0, k0, v0, l0, q1, k1, v1, l1, m0_ref, m1_ref, bd_ref, o0, s0, o1, s1, st_ref):
        @pl.when(pl.program_id(0) == 0)
        def _():
            st_ref[...] = jnp.zeros_like(st_ref)

        bd = bd_ref[...] > 0.5
        dirs = []
        for e, (q_ref, k_ref, v_ref, la_ref, m_ref, o_ref, s_ref) in enumerate(
                ((q0, k0, v0, l0, m0_ref, o0, s0), (q1, k1, v1, l1, m1_ref, o1, s1))):
            mv = m_ref[...]
            dirs.append((q_ref, k_ref, v_ref, la_ref, mv, jnp.concatenate([mv] * 4, axis=0) > 0.5, o_ref, s_ref))

        def body(cc, carry):
            E = range(2)
            cs = [nc - 1 - cc if e else cc for e in E]
            rows = [pl.ds(pl.multiple_of(c * GLA_CHUNK, GLA_CHUNK), GLA_CHUNK) for c in cs]
            b = [_dot_exact(dirs[e][4], dirs[e][3][rows[e], :]) for e in E]
            t = [_gla_factors(dirs[e][0], dirs[e][1], rows[e], b[e], *orders[e][1]) for e in E]
            vb = [dirs[e][2][rows[e], :].astype(BF16) for e in E]
            st = [st_ref[e] for e in E]
            a = [_dot(_stack_heads(t[e][1][0], bd).astype(BF16), t[e][1][1].astype(BF16), _NT) for e in E]
            inter = [_dot(t[e][1][2].astype(BF16), st[e].astype(BF16), _NT) for e in E]
            kv = [_dot(vb[e], t[e][1][3].astype(BF16), _TN) for e in E]
            a = [jnp.where(dirs[e][5], a[e], 0.0).astype(BF16) for e in E]
            r = [_dot(a[e], vb[e]) for e in E]
            for e in E:
                dirs[e][7][cs[e]] = st[e]
                dirs[e][6][rows[e], :] = _diag_heads(r[e], bd) + inter[e]
                st_ref[e] = st[e] * t[e][0][4] + jnp.where(bd, kv[e], 0.0)
            return carry

        lax.fori_loop(0, nc, body, 0)

    const = lambda shp: pl.BlockSpec(shp, lambda i: (0, 0))
    in_specs, out_specs = [], []
    for sp in specs:
        in_specs += [sp["col"](0), sp["col"](1), sp["col"](2), sp["row"]]
        out_specs += [sp["row"], sp["state"]]
    return _call(
        kern, (p, p, p, la0, p, p, p, la1, jnp.asarray(orders[0][0]), jnp.asarray(orders[1][0]),
               _block_ones(GROUP_W, HEAD_DIM)),
        ride, lambda: (pl.program_id(0) == 0, pl.program_id(0) == nt - 1), name="gla_fwd", grid=(nt,),
        in_specs=in_specs + [const((GLA_CHUNK, GLA_CHUNK))] * 2 + [const((GROUP_W, GROUP_W))], out_specs=out_specs,
        out_shape=[jax.ShapeDtypeStruct((L, GROUP_W), F32),
                   jax.ShapeDtypeStruct((L // GLA_CHUNK, GROUP_W, GROUP_W), F32)] * 2,
        scratch_shapes=[pltpu.VMEM((2, GROUP_W, GROUP_W), F32)])


def _gla_chunk_bwd(p, la0, la1, do, sprev0, sprev1, ride=None):
    L = la0.shape[0]
    nt, nc, specs = _gla_specs(L, (False, True))
    orders = [_gla_order(False), _gla_order(True)]

    def kern(q0, k0, v0, l0, do0, s0, q1, k1, v1, l1, do1, s1, m0_ref, m1_ref, t0_ref, t1_ref, bd_ref, *rest):
        outs, dst_ref = (rest[0:4], rest[4:8]), rest[8]

        @pl.when(pl.program_id(0) == 0)
        def _():
            dst_ref[...] = jnp.zeros_like(dst_ref)

        bd = bd_ref[...] > 0.5
        row = lax.broadcasted_iota(jnp.int32, (GLA_CHUNK, GROUP_W), 0)
        dirs = []
        for ins, m_ref, t_ref in (((q0, k0, v0, l0, do0, s0), m0_ref, t0_ref), ((q1, k1, v1, l1, do1, s1), m1_ref, t1_ref)):
            mv = m_ref[...]
            dirs.append(ins + (mv, t_ref[...], jnp.concatenate([mv] * 4, axis=0) > 0.5))

        def body(cc, carry):
            E2 = range(2)
            cs = [cc if e else nc - 1 - cc for e in E2]
            rows = [pl.ds(pl.multiple_of(c * GLA_CHUNK, GLA_CHUNK), GLA_CHUNK) for c in cs]
            b = [_dot_exact(dirs[e][6], dirs[e][3][rows[e], :]) for e in E2]
            t = [_gla_factors(dirs[e][0], dirs[e][1], rows[e], b[e], *orders[e][1]) for e in E2]
            vb = [dirs[e][2][rows[e], :].astype(BF16) for e in E2]
            dov = [dirs[e][4][rows[e], :] for e in E2]
            dob = [x.astype(BF16) for x in dov]
            st = [dirs[e][5][cs[e]] for e in E2]
            dst = [dst_ref[e] for e in E2]
            stb, dstb = [x.astype(BF16) for x in st], [x.astype(BF16) for x in dst]
            qst = [_stack_heads(t[e][1][0], bd).astype(BF16) for e in E2]
            dost = [_stack_heads(dov[e], bd).astype(BF16) for e in E2]
            kNb, qEb, kFb = ([t[e][1][n].astype(BF16) for e in E2] for n in (1, 2, 3))
            a = [_dot(qst[e], kNb[e], _NT) for e in E2]
            da = [_dot(dost[e], vb[e], _NT) for e in E2]
            dqE = [_dot(dob[e], stb[e]) for e in E2]
            dkF = [_dot(vb[e], dstb[e]) for e in E2]
            dv_inter = [_dot(kFb[e], dstb[e], _NT) for e in E2]
            dst_in = [_dot(dob[e], qEb[e], _TN) for e in E2]
            a = [jnp.where(dirs[e][8], a[e], 0.0).astype(BF16) for e in E2]
            da = [jnp.where(dirs[e][8], da[e], 0.0).astype(BF16) for e in E2]
            dv_intra = [_dot(a[e], dost[e], _TN) for e in E2]
            dqP = [_dot(da[e], kNb[e]) for e in E2]
            dkN = [_dot(da[e], qst[e], _TN) for e in E2]
            db = []
            for e in E2:
                (P, N, Ef, Fd, d), (qP, kN, qE, kF) = t[e]
                mid, last = orders[e][1]
                dq_ref, dk_ref, dv_ref, _ = outs[e]
                dqp = _diag_heads(dqP[e], bd)
                dd = _colsum(dst[e] * st[e])
                dst_ref[e] = jnp.where(bd, dst_in[e], 0.0) + dst[e] * d
                tP, tN, tE, tF = dqp * qP, dkN[e] * kN, dqE[e] * qE, dkF[e] * kF
                db.append(tP - tN + tE - tF + jnp.where(row == mid, _colsum(tN - tP), 0.0)
                          + jnp.where(row == last, _colsum(tF) + dd * d, 0.0))
                dq_ref[rows[e], :] = (dqp * P + dqE[e] * Ef) * ATT_SCALE
                dk_ref[rows[e], :] = dkN[e] * N + dkF[e] * Fd
                dv_ref[rows[e], :] = dv_intra[e] + dv_inter[e]
            dla = [_dot_exact(dirs[e][7], db[e]) for e in E2]
            for e in E2:
                outs[e][3][rows[e], :] = dla[e]
            return carry

        lax.fori_loop(0, nc, body, 0)

    const = lambda shp: pl.BlockSpec(shp, lambda i: (0, 0))
    in_specs, out_specs = [], []
    for sp in specs:
        in_specs += [sp["col"](0), sp["col"](1), sp["col"](2), sp["row"], sp["row"], sp["state"]]
        out_specs += [sp["row"]] * 4
    m0, m1 = orders[0][0], orders[1][0]
    return _call(
        kern, (p, p, p, la0, do, sprev0, p, p, p, la1, do, sprev1, jnp.asarray(m0), jnp.asarray(m1),
               jnp.asarray(m0.T.copy()), jnp.asarray(m1.T.copy()), _block_ones(GROUP_W, HEAD_DIM)),
        ride, lambda: (pl.program_id(0) == 0, pl.program_id(0) == nt - 1), name="gla_bwd", grid=(nt,),
        in_specs=in_specs + [const((GLA_CHUNK, GLA_CHUNK))] * 4 + [const((GROUP_W, GROUP_W))], out_specs=out_specs,
        out_shape=[jax.ShapeDtypeStruct((L, GROUP_W), F32)] * 8,
        scratch_shapes=[pltpu.VMEM((2, GROUP_W, GROUP_W), F32)])


def _gla_fwd(p, W, l, ride=None):
    wg, bg = _gla_gate_w(W["gla_w_gate"][l], W["gla_b_gate"][l])
    la0, la1 = _gla_gates_fwd(p, wg, bg)
    (of, s0, ob, s1), got = _gla_chunk_fwd(p, la0, la1, ride)

    def post(i, nt, of, ob, g, ng, bd):
        o = of + ob
        r = lax.rsqrt(_head_sum(o * o, bd) * (1.0 / HEAD_DIM) + EPS)
        return o * r * ng * _silu(g)

    ya = _rows("gla_post", post, 512, [Row(of), Row(ob), Row(p, GROUP_W, 3), Full(W["gla_norm"][l][None]),
                                       Full(_block_ones(GROUP_W, HEAD_DIM))], [("row", GROUP_W, BF16)])[0]
    return ya, (la0, la1, of, ob, s0, s1), got


def _gla_bwd(dy, dy_cb, p, W, l, saved, ride=None):
    la0, la1, of, ob, s0, s1 = saved
    wg, bg = _gla_gate_w(W["gla_w_gate"][l], W["gla_b_gate"][l])

    def post(i, nt, dy, of, ob, g, ng, bd):
        o = of + ob
        r = lax.rsqrt(_head_sum(o * o, bd) * (1.0 / HEAD_DIM) + EPS)
        oh = o * r
        don = dy * _silu(g)
        doh = don * ng
        do = r * (doh - oh * _head_sum(doh * oh, bd) * (1.0 / HEAD_DIM))
        return do, dy * (oh * ng) * _dsilu(g), _colsum(don * oh)

    do, dg, dng = _rows("gla_post_bwd", post, 512,
                        [Row(dy, GROUP_W, dy_cb), Row(of), Row(ob), Row(p, GROUP_W, 3), Full(W["gla_norm"][l][None]),
                         Full(_block_ones(GROUP_W, HEAD_DIM))],
                        [("row", GROUP_W, F32), ("row", GROUP_W, BF16), ("acc", (1, GROUP_W), F32)])
    (dq0, dk0, dv0, dla0, dq1, dk1, dv1, dla1), got = _gla_chunk_bwd(p, la0, la1, do, s0, s1, ride)
    dq, dk, dv = _rows("gla_sum_bwd", lambda i, nt, a0, a1, b0, b1, c0, c1: (a0 + a1, b0 + b1, c0 + c1), 512,
                       [Row(t) for t in (dq0, dq1, dk0, dk1, dv0, dv1)], [("row", GROUP_W, BF16)] * 3)
    dz, dwg, dbg = _gla_gates_bwd(p, dla0, dla1, wg, bg)
    dw_gate = jnp.stack([dwg[e * GLA_RANK:(e + 1) * GLA_RANK, e * GROUP_W:(e + 1) * GROUP_W] for e in range(2)])
    grads = dict(gla_w_gate=dw_gate, gla_b_gate=dbg.reshape(2, GROUP_W), gla_norm=dng[0])
    return (dq, dk, dv, dg, dz), grads, got


def _rope_tables(L):
    pos = jnp.arange(L, dtype=F32)
    inv_freq = ROPE_THETA ** (-jnp.arange(0, HEAD_DIM, 2, dtype=F32) / HEAD_DIM)
    ang = pos[:, None] * inv_freq[None, :]
    cos, sin = jnp.cos(ang), jnp.sin(ang)
    return jnp.tile(jnp.concatenate([cos, cos], axis=1), (1, 4)), jnp.tile(jnp.concatenate([-sin, sin], axis=1), (1, 4))


def _swap_halves(t):
    lane = lax.broadcasted_iota(jnp.int32, t.shape, 1)
    first = (lane & (HEAD_DIM - 1)) < HEAD_DIM // 2
    return jnp.where(first, pltpu.roll(t, GROUP_W - HEAD_DIM // 2, axis=1), pltpu.roll(t, HEAD_DIM // 2, axis=1))


def _attn_prep(p, cosf, sinf):
    def body(i, nt, qb, kb, vb, qd, kd, vd, c, s):
        return qb, kb, vb, qd * c + _swap_halves(qd) * s, kd * c + _swap_halves(kd) * s, vd
    ins = [Row(p, GROUP_W, cb) for cb in (4, 5, 6, 9, 10, 11)] + [Row(cosf), Row(sinf)]
    return _rows("attn_prep", body, 512, ins, [("row", GROUP_W, BF16)] * 6)


def _na_onehot():
    c = np.arange(GRID_W)
    dc = np.clip(c[None, :] - c[:, None], -(NA_COLS - 1), NA_COLS - 1) + NA_COLS - 1
    oh = np.zeros((LANES, GRID_W * GRID_W), np.float32)
    oh[dc.reshape(-1), np.arange(GRID_W * GRID_W)] = 1.0
    return jnp.asarray(oh)


def _na_colmask():
    c = np.arange(GRID_W)
    start = np.clip(c - NA_COLS // 2, 0, GRID_W - NA_COLS)
    ok = (c[None, :] >= start[:, None]) & (c[None, :] < start[:, None] + NA_COLS)
    return jnp.asarray(np.where(ok, 0.0, NEG).astype(np.float32))


N_DR = 2 * NA_ROWS - 1


NA_HALF = GRID_W // 2
NA_KCOLS = 48
NA_WIN = NA_ROWS * NA_KCOLS
NA_ROWS_PER_STEP = 2


def _na_bias(rpb):
    rp = jnp.zeros((GRID_W, LANES), F32).at[:4 * N_DR, :2 * NA_COLS - 1].set(rpb.reshape(4 * N_DR, 2 * NA_COLS - 1))

    def expand(r_ref, oh_ref, o_ref):
        o_ref[...] = _dot_exact(r_ref[...], oh_ref[...])

    r = pl.pallas_call(expand, name="na_bias_expand",
                       out_shape=jax.ShapeDtypeStruct((GRID_W, GRID_W * GRID_W), F32))(rp, _na_onehot())
    r = r[:4 * N_DR].reshape(4, N_DR, GRID_W, GRID_W)

    def build(r_ref, m_ref, o_ref):
        for h in range(4):
            for c in range(NA_ROWS):
                for half in range(2):
                    q0, k0 = NA_HALF * half, 16 * half
                    for i in range(NA_ROWS):
                        o_ref[h, c, half, :, i * NA_KCOLS:(i + 1) * NA_KCOLS] = (
                            r_ref[h, i - c + NA_ROWS - 1, q0:q0 + NA_HALF, k0:k0 + NA_KCOLS]
                            + m_ref[q0:q0 + NA_HALF, k0:k0 + NA_KCOLS])

    return pl.pallas_call(build, name="na_bias_build",
                          out_shape=jax.ShapeDtypeStruct((4, NA_ROWS, 2, NA_HALF, NA_WIN), F32))(r, _na_colmask())


def _na_bias_bwd(dbias):
    def fold(d_ref, o_ref):
        o_ref[...] = jnp.zeros_like(o_ref)
        for h in range(4):
            for a in range(N_DR):
                for half in range(2):
                    q0, k0 = NA_HALF * half, 16 * half
                    acc = jnp.zeros((NA_HALF, NA_KCOLS), F32)
                    for c in range(NA_ROWS):
                        i = a + c - (NA_ROWS - 1)
                        if 0 <= i < NA_ROWS:
                            acc = acc + d_ref[h, c, half, :, i * NA_KCOLS:(i + 1) * NA_KCOLS]
                    o_ref[h, a, q0:q0 + NA_HALF, k0:k0 + NA_KCOLS] = acc

    dr = pl.pallas_call(fold, name="na_bias_fold",
                        out_shape=jax.ShapeDtypeStruct((4, N_DR, GRID_W, GRID_W), F32))(dbias)
    dr = jnp.zeros((GRID_W, GRID_W * GRID_W), F32).at[:4 * N_DR].set(dr.reshape(4 * N_DR, GRID_W * GRID_W))

    def contract(d_ref, oh_ref, o_ref):
        o_ref[...] = lax.dot_general(d_ref[...], oh_ref[...], _NT, precision=lax.Precision.HIGHEST,
                                     preferred_element_type=F32)

    g = pl.pallas_call(contract, name="na_bias_contract",
                       out_shape=jax.ShapeDtypeStruct((GRID_W, LANES), F32))(dr, _na_onehot())
    return g[:4 * N_DR, :2 * NA_COLS - 1].reshape(4, N_DR, 2 * NA_COLS - 1)


def _na_window(r, n_rows):
    rs = jnp.clip(r - NA_ROWS // 2, 0, n_rows - NA_ROWS)
    return rs, r - rs


def _na_key_rows(rs, half, t):
    return pl.ds(pl.multiple_of((rs + t) * GRID_W + 16 * half, 16), NA_KCOLS)


def _na_keys(ref, rs, half):
    return jnp.concatenate([ref[_na_key_rows(rs, half, t), :] for t in range(NA_ROWS)], axis=0)


def _na_stack(x, first):
    zero = jnp.zeros_like(x)
    return jnp.concatenate([jnp.where(first, x, zero), jnp.where(first, zero, x)], axis=0)


def _na_bias_spec():
    return pl.BlockSpec((2, NA_ROWS, 2, NA_HALF, NA_WIN), lambda j, i: (j, 0, 0, 0, 0))


def _grid_edges(n0, n1):
    j, i = pl.program_id(0), pl.program_id(1)
    return jnp.logical_and(j == 0, i == 0), jnp.logical_and(j == n0 - 1, i == n1 - 1)


def _na_fwd(q, k, v, bias, ride=None):
    L = q.shape[0]
    n_rows = L // GRID_W
    tm = _tile(L, 512, GRID_W)
    nt, nr = L // tm, tm // GRID_W

    def kern(q_ref, k_ref, v_ref, b_ref, o_ref):
        i = pl.program_id(1)
        first = lax.broadcasted_iota(jnp.int32, (NA_HALF, LANES), 1) < HEAD_DIM

        def body(it, carry):
            parts = []
            for u in range(NA_ROWS_PER_STEP):
                rr = it * NA_ROWS_PER_STEP + u
                rs, c = _na_window(i * nr + rr, n_rows)
                for half in range(2):
                    rows = pl.ds(pl.multiple_of(rr * GRID_W + NA_HALF * half, NA_HALF), NA_HALF)
                    bias = jnp.concatenate([b_ref[0, c, half], b_ref[1, c, half]], axis=0)
                    parts.append((rows, _na_stack(q_ref[rows, :], first), bias, _na_keys(k_ref, rs, half),
                                  _na_keys(v_ref, rs, half)))
            s = [_dot(qs, kw, _NT) * ATT_SCALE + bias for _, qs, bias, kw, _ in parts]
            e = [jnp.exp(x - jnp.max(x, axis=-1, keepdims=True)) for x in s]
            pn = [(x / jnp.sum(x, axis=-1, keepdims=True)).astype(BF16) for x in e]
            o = [_dot(p, part[4]) for p, part in zip(pn, parts)]
            for x, (rows, *_) in zip(o, parts):
                o_ref[rows, :] = jnp.where(first, x[:NA_HALF], x[NA_HALF:]).astype(BF16)
            return carry

        lax.fori_loop(0, nr // NA_ROWS_PER_STEP, body, 0)

    qspec = pl.BlockSpec((tm, LANES), lambda j, i: (i, j))
    kvspec = pl.BlockSpec((L, LANES), lambda j, i: (0, j))
    (y,), got = _call(
        kern, (q, k, v, bias), ride, lambda: _grid_edges(2, nt), name="na_fwd", grid=(2, nt),
        in_specs=[qspec, kvspec, kvspec, _na_bias_spec()],
        out_specs=[qspec], out_shape=[jax.ShapeDtypeStruct((L, GROUP_W), BF16)], scratch_shapes=[])
    return y, got


def _na_bwd(dy, dy_block, q, k, v, bias, ride=None):
    L = q.shape[0]
    n_rows = L // GRID_W
    tm = _tile(L, 512, GRID_W)
    nt, nr = L // tm, tm // GRID_W

    def kern(dy_ref, q_ref, k_ref, v_ref, b_ref, dq_ref, dk_ref, dv_ref, db_ref):
        i = pl.program_id(1)

        @pl.when(i == 0)
        def _():
            dk_ref[...] = jnp.zeros_like(dk_ref)
            dv_ref[...] = jnp.zeros_like(dv_ref)
            db_ref[...] = jnp.zeros_like(db_ref)

        first = lax.broadcasted_iota(jnp.int32, (NA_HALF, LANES), 1) < HEAD_DIM

        def body(rr, carry):
            rs, c = _na_window(i * nr + rr, n_rows)
            parts = []
            for half in range(2):
                rows = pl.ds(pl.multiple_of(rr * GRID_W + NA_HALF * half, NA_HALF), NA_HALF)
                bias = jnp.concatenate([b_ref[0, c, half], b_ref[1, c, half]], axis=0)
                parts.append((rows, half, _na_stack(q_ref[rows, :], first), _na_stack(dy_ref[rows, :].astype(BF16), first),
                              bias, _na_keys(k_ref, rs, half), _na_keys(v_ref, rs, half)))
            s = [_dot(qs, kw, _NT) * ATT_SCALE + bias for _, _, qs, _, bias, kw, _ in parts]
            dp = [_dot(dos, vw, _NT) for _, _, _, dos, _, _, vw in parts]
            e = [jnp.exp(x - jnp.max(x, axis=-1, keepdims=True)) for x in s]
            pn = [x / jnp.sum(x, axis=-1, keepdims=True) for x in e]
            ds = [p * (d - jnp.sum(p * d, axis=-1, keepdims=True)) for p, d in zip(pn, dp)]
            dsb = [x.astype(BF16) for x in ds]
            pnb = [x.astype(BF16) for x in pn]
            dq = [_dot(x, part[5]) for x, part in zip(dsb, parts)]
            dk = [_dot(x, part[2], _TN) for x, part in zip(dsb, parts)]
            dv = [_dot(x, part[3], _TN) for x, part in zip(pnb, parts)]
            for n, (rows, half, *_) in enumerate(parts):
                db_ref[0, c, half] += ds[n][:NA_HALF]
                db_ref[1, c, half] += ds[n][NA_HALF:]
                dq_ref[rows, :] = (jnp.where(first, dq[n][:NA_HALF], dq[n][NA_HALF:]) * ATT_SCALE).astype(BF16)
                for t in range(NA_ROWS):
                    kr = _na_key_rows(rs, half, t)
                    dk_ref[kr, :] += dk[n][t * NA_KCOLS:(t + 1) * NA_KCOLS] * ATT_SCALE
                    dv_ref[kr, :] += dv[n][t * NA_KCOLS:(t + 1) * NA_KCOLS]
            return carry

        lax.fori_loop(0, nr, body, 0)

    qspec = pl.BlockSpec((tm, LANES), lambda j, i: (i, j))
    kvspec = pl.BlockSpec((L, LANES), lambda j, i: (0, j))
    return _call(
        kern, (dy, q, k, v, bias), ride, lambda: _grid_edges(2, nt), name="na_bwd", grid=(2, nt),
        in_specs=[pl.BlockSpec((tm, LANES), lambda j, i: (i, dy_block + j)), qspec, kvspec, kvspec, _na_bias_spec()],
        out_specs=[qspec, kvspec, kvspec, _na_bias_spec()],
        out_shape=[jax.ShapeDtypeStruct((L, GROUP_W), BF16), jax.ShapeDtypeStruct((L, GROUP_W), F32),
                   jax.ShapeDtypeStruct((L, GROUP_W), F32),
                   jax.ShapeDtypeStruct((4, NA_ROWS, 2, NA_HALF, NA_WIN), F32)], scratch_shapes=[])


def _dil_specs(n, tq):
    R = DIL_RADIUS
    step, nb = tq // R, n // R
    main = pl.BlockSpec((tq, LANES), lambda j, i: (i, j))
    prev = pl.BlockSpec((R, LANES), lambda j, i: (jnp.maximum(i * step - 1, 0), j))
    nxt = pl.BlockSpec((R, LANES), lambda j, i: (jnp.minimum((i + 1) * step, nb - 1), j))
    return main, prev, nxt


def _dil_valid(i, tq, n):
    R = DIL_RADIUS
    row = lax.broadcasted_iota(jnp.int32, (tq, tq + 2 * R), 0)
    col = lax.broadcasted_iota(jnp.int32, (tq, tq + 2 * R), 1)
    kpos = i * tq - R + col
    return (jnp.abs(col - R - row) <= R) & (kpos >= 0) & (kpos < n)


def _dil_fwd(q, k, v, dil, ride=None):
    L = q.shape[0]
    n = L // dil
    tq = _tile(n, 256, DIL_RADIUS)
    view = lambda t: t.reshape(n, dil * GROUP_W)

    def kern(q_ref, kp_ref, k_ref, kn_ref, vp_ref, v_ref, vn_ref, o_ref, l_ref):
        i = pl.program_id(1)
        valid = _dil_valid(i, tq, n)
        qv = q_ref[...]
        ka = jnp.concatenate([kp_ref[...], k_ref[...], kn_ref[...]], axis=0)
        va = jnp.concatenate([vp_ref[...], v_ref[...], vn_ref[...]], axis=0)
        lane = lax.broadcasted_iota(jnp.int32, (tq, LANES), 1)
        first = lane < HEAD_DIM
        s = [_dot(jnp.where(first == (hh == 0), qv, jnp.zeros_like(qv)), ka, _NT) for hh in range(2)]
        s = [jnp.where(valid, x * ATT_SCALE, NEG) for x in s]
        m = [jnp.max(x, axis=-1, keepdims=True) for x in s]
        e = [jnp.exp(x - mx) for x, mx in zip(s, m)]
        den = [jnp.sum(x, axis=-1, keepdims=True) for x in e]
        o = [_dot((x / d).astype(BF16), va) for x, d in zip(e, den)]
        o_ref[...] = jnp.where(first, o[0], o[1])
        l_ref[...] = jnp.where(first, m[0] + jnp.log(den[0]), m[1] + jnp.log(den[1]))

    main, prev, nxt = _dil_specs(n, tq)
    (o, lse), got = _call(
        kern, (view(q), view(k), view(k), view(k), view(v), view(v), view(v)), ride,
        lambda: _grid_edges(2 * dil, n // tq), name=f"dil_fwd_{dil}", grid=(2 * dil, n // tq),
        in_specs=[main, prev, main, nxt, prev, main, nxt], out_specs=[main, main],
        out_shape=[jax.ShapeDtypeStruct((n, dil * GROUP_W), F32)] * 2, scratch_shapes=[])
    return (o.reshape(L, GROUP_W), lse.reshape(L, GROUP_W)), got


def _dil_bwd(q, k, v, do, lse, dterm, dil, ride=None):
    L = q.shape[0]
    n = L // dil
    R = DIL_RADIUS
    tq = _tile(n, 256, R)
    nq = n // tq
    view = lambda t: t.reshape(n, dil * GROUP_W)

    def kern(q_ref, kp_ref, k_ref, kn_ref, vp_ref, v_ref, vn_ref, do_ref, l_ref, dt_ref, dq_ref, dk_ref, dv_ref):
        i = pl.program_id(1)

        @pl.when(i == 0)
        def _():
            dk_ref[...] = jnp.zeros_like(dk_ref)
            dv_ref[...] = jnp.zeros_like(dv_ref)

        valid = _dil_valid(i, tq, n)
        qv, dov = q_ref[...], do_ref[...]
        ka = jnp.concatenate([kp_ref[...], k_ref[...], kn_ref[...]], axis=0)
        va = jnp.concatenate([vp_ref[...], v_ref[...], vn_ref[...]], axis=0)
        lv, dtv = l_ref[...], dt_ref[...]
        lane = lax.broadcasted_iota(jnp.int32, (tq, LANES), 1)
        first = lane < HEAD_DIM
        H = range(2)
        qm = [jnp.where(first == (hh == 0), qv, jnp.zeros_like(qv)) for hh in H]
        dom = [jnp.where(first == (hh == 0), dov, jnp.zeros_like(dov)) for hh in H]
        s = [_dot(qm[hh], ka, _NT) for hh in H]
        dp = [_dot(dom[hh], va, _NT) for hh in H]
        pn = [jnp.where(valid, jnp.exp(s[hh] * ATT_SCALE - lv[:, hh * HEAD_DIM:hh * HEAD_DIM + 1]), 0.0) for hh in H]
        dsb = [(pn[hh] * (dp[hh] - dtv[:, hh * HEAD_DIM:hh * HEAD_DIM + 1])).astype(BF16) for hh in H]
        pnb = [x.astype(BF16) for x in pn]
        dq = [_dot(dsb[hh], ka) for hh in H]
        dk2 = [_dot(dsb[hh], qm[hh], _TN) for hh in H]
        dv2 = [_dot(pnb[hh], dom[hh], _TN) for hh in H]
        dka = (dk2[0] + dk2[1]) * ATT_SCALE
        dva = dv2[0] + dv2[1]
        dq_ref[...] = jnp.where(first, dq[0], dq[1]) * ATT_SCALE
        r0 = pl.multiple_of(i * tq, R)
        dk_ref[pl.ds(r0, tq), :] += dka[R:R + tq]
        dv_ref[pl.ds(r0, tq), :] += dva[R:R + tq]

        @pl.when(i > 0)
        def _():
            dk_ref[pl.ds(r0 - R, R), :] += dka[:R]
            dv_ref[pl.ds(r0 - R, R), :] += dva[:R]

        @pl.when(i < nq - 1)
        def _():
            dk_ref[pl.ds(r0 + tq, R), :] += dka[R + tq:]
            dv_ref[pl.ds(r0 + tq, R), :] += dva[R + tq:]

    main, prev, nxt = _dil_specs(n, tq)
    whole = pl.BlockSpec((n, LANES), lambda j, i: (0, j))
    shp = jax.ShapeDtypeStruct((n, dil * GROUP_W), F32)
    (dq, dk, dv), got = _call(
        kern, (view(q), view(k), view(k), view(k), view(v), view(v), view(v), view(do), view(lse), view(dterm)), ride,
        lambda: _grid_edges(2 * dil, nq), name=f"dil_bwd_{dil}", grid=(2 * dil, nq),
        in_specs=[main, prev, main, nxt, prev, main, nxt, main, main, main], out_specs=[main, whole, whole],
        out_shape=[shp] * 3, scratch_shapes=[])
    return (dq.reshape(L, GROUP_W), dk.reshape(L, GROUP_W), dv.reshape(L, GROUP_W)), got


def _dil_weights(lses):
    m = jnp.maximum(jnp.maximum(lses[0], lses[1]), lses[2])
    e = [jnp.exp(l - m) for l in lses]
    tot = e[0] + e[1] + e[2]
    return [x / tot for x in e]


def _dilated_fwd(q, k, v, rides):
    res, got = [], {}
    for _, dil in DIL_PAIRS:
        r, got[f"dil{dil}"] = _dil_fwd(q, k, v, dil, rides.get(f"dil{dil}"))
        res.append(r)

    def body(i, nt, o0, o1, o2, l0, l1, l2):
        w = _dil_weights((l0, l1, l2))
        return w[0] * o0 + w[1] * o1 + w[2] * o2

    ins = [Row(r[0]) for r in res] + [Row(r[1]) for r in res]
    return _rows("dil_combine", body, 512, ins, [("row", GROUP_W, BF16)])[0], res, got


def _dilated_bwd(dy, dy_cb, q, k, v, saved, cosf, sinf, rides):
    def split(i, nt, dy, o0, o1, o2, l0, l1, l2, bd):
        w = _dil_weights((l0, l1, l2))
        y = w[0] * o0 + w[1] * o1 + w[2] * o2
        dyy = _head_sum(dy * y, bd)
        return tuple(wg * dy for wg in w) + tuple(wg * dyy for wg in w)

    ins = [Row(dy, GROUP_W, dy_cb)] + [Row(r[0]) for r in saved] + [Row(r[1]) for r in saved]
    outs = _rows("dil_split_bwd", split, 512, ins + [Full(_block_ones(GROUP_W, HEAD_DIM))],
                 [("row", GROUP_W, BF16)] * 3 + [("row", GROUP_W, F32)] * 3)
    g, got = [], {}
    for b, (_, dil) in enumerate(DIL_PAIRS):
        r, got[f"dil{dil}"] = _dil_bwd(q, k, v, outs[b], saved[b][1], outs[3 + b], dil, rides.get(f"dil{dil}"))
        g.append(r)

    def finish(i, nt, q0, q1, q2, k0, k1, k2, v0, v1, v2, c, s):
        dq, dk = q0 + q1 + q2, k0 + k1 + k2
        return dq * c + _swap_halves(dq * s), dk * c + _swap_halves(dk * s), v0 + v1 + v2

    ins = [Row(g[b][t]) for t in range(3) for b in range(3)] + [Row(cosf), Row(sinf)]
    return _rows("dil_finish_bwd", finish, 512, ins, [("row", GROUP_W, BF16)] * 3), got


def _layer_fwd(x, W, l, cosf, sinf, rides):
    h1 = _rms_fwd(x, W["mix_norm_pre"][l][None], "mix_norm")
    p = _mm(h1, W["w_in"][l], "nn", F32, "proj_in")
    ya, sa, got_gla = _gla_fwd(p, W, l, rides.get("gla"))
    qb, kb, vb, qd, kd, vd = _attn_prep(p, cosf, sinf)
    bias = _na_bias(W["na_rpb"][l])
    yb, got_na = _na_fwd(qb, kb, vb, bias, rides.get("na"))
    yc, sc = _lru_fwd(p, W, l)
    yd, sd, got = _dilated_fwd(qd, kd, vd, rides)
    got.update(gla=got_gla, na=got_na)
    ycat = jnp.concatenate([ya, yb, yc, yd], axis=1)
    ymix = _mm(ycat, W["w_out"][l], "nn", F32, "proj_out", tm=1024)
    xm = _rms_resid_fwd(x, ymix, W["mix_norm_post"][l][None], "mix_resid")
    h2 = _rms_fwd(xm, W["ffn_norm_pre"][l][None], "ffn_norm")
    gu, act = _ffn_in_swiglu(h2, W["ffn_w_in"][l])
    f = _mm(act, W["ffn_w_out"][l], "nn", F32, "ffn_out")
    xo = _rms_resid_fwd(xm, f, W["ffn_norm_post"][l][None], "ffn_resid")
    saved = dict(x=x, h1=h1, p=p, sa=sa, att=(qb, kb, vb, qd, kd, vd), bias=bias, sc=sc, sd=sd, ycat=ycat, ymix=ymix,
                 xm=xm, h2=h2, gu=gu, act=act, f=f)
    return xo, saved, got


def _layer_bwd(dxo, W, l, S, cosf, sinf, rides):
    g = {}
    df, g["ffn_norm_post"] = _rms_bwd(dxo, S["f"], W["ffn_norm_post"][l][None], "ffn_resid_bwd", out_dtype=BF16)
    g["ffn_w_out"] = _mm(S["act"], df, "tn", BF16, "ffn_out_dw", tm=256, tk=4096)
    dgu = _ffn_out_dx_swiglu(df, W["ffn_w_out"][l], S["gu"])
    dh2 = _mm(dgu, W["ffn_w_in"][l], "nt", F32, "ffn_in_dx")
    g["ffn_w_in"] = _mm(S["h2"], dgu, "tn", BF16, "ffn_in_dw", tm=1024, tn=512, tk=4096)
    dxm, g["ffn_norm_pre"] = _rms_bwd(dh2, S["xm"], W["ffn_norm_pre"][l][None], "ffn_norm_bwd", resid=dxo)
    dymix, g["mix_norm_post"] = _rms_bwd(dxm, S["ymix"], W["mix_norm_post"][l][None], "mix_resid_bwd", out_dtype=BF16)
    dycat = _mm(dymix, W["w_out"][l], "nt", F32, "proj_out_dx", tm=1024)
    g["w_out"] = _mm(S["ycat"], dymix, "tn", BF16, "proj_out_dw", tm=1024, tn=512, tk=4096)
    p = S["p"]
    qb, kb, vb, qd, kd, vd = S["att"]
    (dqa, dka, dva, dga, dz), ga, got_gla = _gla_bwd(dycat, 0, p, W, l, S["sa"], rides.get("gla"))
    (dqb, dkb, dvb, dbias), got_na = _na_bwd(dycat, 2, qb, kb, vb, S["bias"], rides.get("na"))
    g["na_rpb"] = _na_bias_bwd(dbias)
    dxc, dgc, gc = _lru_bwd(dycat, 2, p, W, l, S["sc"])
    (dqd, dkd, dvd), got = _dilated_bwd(dycat, 3, qd, kd, vd, S["sd"], cosf, sinf, rides)
    got.update(gla=got_gla, na=got_na)
    g.update(ga)
    g.update(gc)
    dp = jnp.concatenate([dqa, dka, dva, dga, dqb, dkb.astype(BF16), dvb.astype(BF16), dxc, dgc, dqd, dkd, dvd, dz], axis=1)
    dh1 = _mm(dp, W["w_in"][l], "nt", F32, "proj_in_dx")
    g["w_in"] = _mm(S["h1"], dp, "tn", BF16, "proj_in_dw", tm=1024, tn=640, tk=4096)
    dx, g["mix_norm_pre"] = _rms_bwd(dh1, S["x"], W["mix_norm_pre"][l][None], "mix_norm_bwd", resid=dxm)
    for n in ("ffn_norm_post", "ffn_norm_pre", "mix_norm_post", "mix_norm_pre"):
        g[n] = g[n][0]
    return dx, g, got


MESH_AXES = ("x", "y", "c")


class Xfer:
    def __init__(self, arr, kind):
        self.arr, self.kind = arr, kind
        shp = arr.shape
        if kind == "all":
            self.out = (N_DEV,) + shp
        elif kind == "slot":
            self.out = shp
        elif kind == "rows":
            self.r = shp[1] // N_DEV
            self.out = (N_DEV, shp[0], self.r, shp[2])
        else:
            self.r = shp[1]
            self.out = (shp[0], N_DEV * shp[1], shp[2])

    def src(self, ref, peer):
        if self.kind == "slot":
            return ref.at[peer]
        if self.kind == "rows":
            return ref.at[:, pl.ds(peer * self.r, self.r), :]
        return ref

    def dst(self, ref, me):
        if self.kind == "place":
            return ref.at[:, pl.ds(me * self.r, self.r), :]
        return ref.at[me]


class Exchange:
    def __init__(self, items):
        n = len(items)
        self.items = items
        self.arrays = [it.arr for it in items]
        self.specs = [pl.BlockSpec(memory_space=pl.ANY)] * n
        self.out_shape = [jax.ShapeDtypeStruct(it.out, it.arr.dtype) for it in items]
        self.scratch = [pltpu.SemaphoreType.DMA((n * (N_DEV - 1),)), pltpu.SemaphoreType.DMA((n * (N_DEV - 1),)),
                        pltpu.SemaphoreType.DMA((n,))]

    def copies(self, ins, outs, sems):
        send_sems, recv_sems, local_sems = sems
        x, y, c = (lax.axis_index(a) for a in MESH_AXES)
        me = 4 * x + 2 * y + c
        out = []
        for t, it in enumerate(self.items):
            out.append(pltpu.make_async_copy(it.src(ins[t], me), it.dst(outs[t], me), local_sems.at[t]))
            for k in range(1, N_DEV):
                px, py, pc = x ^ ((k >> 2) & 1), y ^ ((k >> 1) & 1), c ^ (k & 1)
                s = t * (N_DEV - 1) + k - 1
                out.append(pltpu.make_async_remote_copy(
                    src_ref=it.src(ins[t], 4 * px + 2 * py + pc), dst_ref=it.dst(outs[t], me),
                    send_sem=send_sems.at[s], recv_sem=recv_sems.at[s], device_id=(px, py, pc),
                    device_id_type=pl.DeviceIdType.MESH))
        return out

    def start(self, ins, outs, sems):
        for cp in self.copies(ins, outs, sems):
            cp.start()

    def wait(self, ins, outs, sems):
        for cp in self.copies(ins, outs, sems):
            cp.wait()


def _exchange(items, name):
    ex = Exchange(items)
    n = len(items)

    def body(*refs):
        ex.start(refs[:n], refs[n:2 * n], refs[2 * n:])
        ex.wait(refs[:n], refs[n:2 * n], refs[2 * n:])

    return pl.pallas_call(body, name=name, out_shape=ex.out_shape, in_specs=ex.specs, out_specs=ex.specs,
                          scratch_shapes=ex.scratch)(*ex.arrays)


def _call(kern, arrays, ride, edges, *, name, grid, in_specs, out_specs, out_shape, scratch_shapes):
    params = _ARB(len(grid))
    if ride is None:
        return pl.pallas_call(kern, name=name, grid=grid, in_specs=in_specs, out_specs=out_specs, out_shape=out_shape,
                              scratch_shapes=scratch_shapes, compiler_params=params)(*arrays), None
    ni, no, ns, nx = len(in_specs), len(out_specs), len(scratch_shapes), len(ride.items)

    def wrapped(*refs):
        ins, xin = refs[:ni], refs[ni:ni + nx]
        outs, xout = refs[ni + nx:ni + nx + no], refs[ni + nx + no:ni + 2 * nx + no]
        scr, sems = refs[ni + 2 * nx + no:ni + 2 * nx + no + ns], refs[ni + 2 * nx + no + ns:]
        first, last = edges()

        @pl.when(first)
        def _():
            ride.start(xin, xout, sems)

        kern(*ins, *outs, *scr)

        @pl.when(last)
        def _():
            ride.wait(xin, xout, sems)

    res = pl.pallas_call(
        wrapped, name=name, grid=grid, in_specs=list(in_specs) + ride.specs, out_specs=list(out_specs) + ride.specs,
        out_shape=list(out_shape) + ride.out_shape, scratch_shapes=list(scratch_shapes) + ride.scratch,
        compiler_params=params)(*arrays, *ride.arrays)
    return res[:no], res[no:]


def _column_segments(width, permuted):
    z0, z1, zn = 4 * GROUP_W, 4 * GROUP_W + 2 * GLA_RANK, 12 * GROUP_W
    segs = []
    for d in range(N_DEV):
        lo, hi = d * width, (d + 1) * width
        if not permuted:
            segs.append([(0, width, lo)])
            continue
        runs = []
        for a, b, shift in ((0, z0, 0), (z0, z1, zn - z0), (z1, 10 ** 9, -(z1 - z0))):
            s, e = max(lo, a), min(hi, b)
            if s < e:
                runs.append((s - lo, e - lo, s + shift))
        segs.append(runs)
    return segs


def _cols_from_pieces(pieces, segs, cols, name):
    _, R, w = pieces.shape
    tm = _tile(R, 256, 16)
    used = max(f + (b - a) for runs in segs for a, b, f in runs)

    def kern(p_ref, o_ref):
        for d, runs in enumerate(segs):
            for a, b, f in runs:
                o_ref[:, f:f + (b - a)] = p_ref[d, :, a:b]
        if used < cols:
            o_ref[:, used:cols] = jnp.zeros((tm, cols - used), o_ref.dtype)

    return pl.pallas_call(
        kern, name=name, grid=(R // tm,), in_specs=[pl.BlockSpec((N_DEV, tm, w), lambda i: (0, i, 0))],
        out_specs=pl.BlockSpec((tm, cols), lambda i: (i, 0)), out_shape=jax.ShapeDtypeStruct((R, cols), pieces.dtype),
        compiler_params=_ARB(1),
    )(pieces)


def _pieces_from_cols(full, segs, w, name):
    R, cols = full.shape
    tm = _tile(R, 256, 16)

    def kern(f_ref, o_ref):
        for d, runs in enumerate(segs):
            for a, b, f in runs:
                o_ref[d, :, a:b] = f_ref[:, f:f + (b - a)]

    return pl.pallas_call(
        kern, name=name, grid=(R // tm,), in_specs=[pl.BlockSpec((tm, cols), lambda i: (i, 0))],
        out_specs=pl.BlockSpec((N_DEV, tm, w), lambda i: (0, i, 0)),
        out_shape=jax.ShapeDtypeStruct((N_DEV, R, w), full.dtype), compiler_params=_ARB(1),
    )(full)


def _sum_slots(recv, name):
    n, R, C = recv.shape
    tm = _tile(R, 256, 16)

    def kern(*refs):
        acc = refs[0][...].astype(F32)
        for r in refs[1:n]:
            acc = acc + r[...].astype(F32)
        refs[n][...] = acc

    return pl.pallas_call(
        kern, name=name, grid=(R // tm,),
        in_specs=[pl.BlockSpec((None, tm, C), lambda i, _s=s: (_s, i, 0)) for s in range(n)],
        out_specs=pl.BlockSpec((tm, C), lambda i: (i, 0)), out_shape=jax.ShapeDtypeStruct((R, C), F32),
        compiler_params=_ARB(1),
    )(*([recv] * n))


BIG = (("w_in", 2), ("w_out", 1), ("ffn_w_in", 2), ("ffn_w_out", 1))
SMALL_SHARDED = ("gla_w_gate", "gla_b_gate", "lru_conv_w", "lru_b_a", "lru_b_x", "lru_lambda")
REPLICATED = ("mix_norm_pre", "mix_norm_post", "gla_norm", "na_rpb", "lru_conv_b", "lru_w_a", "lru_w_x",
              "ffn_norm_pre", "ffn_norm_post")
WEIGHTS = ("mix_norm_pre", "mix_norm_post", "w_in", "gla_w_gate", "gla_b_gate", "gla_norm", "na_rpb", "lru_conv_w",
           "lru_conv_b", "lru_w_a", "lru_b_a", "lru_w_x", "lru_b_x", "lru_lambda", "w_out", "ffn_norm_pre",
           "ffn_norm_post", "ffn_w_in", "ffn_w_out")
FLAT_C = 1024


def _to_rows(vec, row_unit):
    n = vec.shape[-1]
    rows = -(-n // (FLAT_C * row_unit)) * row_unit
    pad = [(0, 0)] * (vec.ndim - 1) + [(0, rows * FLAT_C - n)]
    return jnp.pad(vec, pad).reshape(vec.shape[:-1] + (rows, FLAT_C))


def _unshard(parts, axis):
    t = jnp.moveaxis(parts, 0, axis)
    shp = list(t.shape)
    return t.reshape(shp[:axis] + [shp[axis] * shp[axis + 1]] + shp[axis + 2:])


def _shards(full, axis):
    shp = list(full.shape)
    t = full.reshape(shp[:axis] + [N_DEV, shp[axis] // N_DEV] + shp[axis + 1:])
    return jnp.moveaxis(t, axis, 0)


def _weight_rides(W, l):
    bf = lambda n: W[n][l].astype(BF16)
    ffn = bf("ffn_w_in")
    half = ffn.shape[0] // 2
    return {"gla": [Xfer(bf("w_in"), "all"), Xfer(bf("w_out")[None], "place")],
            "na": [Xfer(bf("ffn_w_out")[None], "place")],
            "dil1": [Xfer(ffn[:half], "all")], "dil4": [Xfer(ffn[half:], "all")]}


def _unpack_weights(full, W, got):
    w_in_w, ffn_w = W["w_in"].shape[-1], W["ffn_w_in"].shape[-1]
    full["w_in"].append(_cols_from_pieces(got["gla"][0], _column_segments(w_in_w, True), P_COLS, "unpack_w_in"))
    full["w_out"].append(got["gla"][1][0])
    full["ffn_w_out"].append(got["na"][0][0])
    full["ffn_w_in"].append(jnp.concatenate(
        [_cols_from_pieces(got[c][0], _column_segments(ffn_w, False), N_DEV * ffn_w, "unpack_ffn_w_in")
         for c in ("dil1", "dil4")], axis=0))


def _grad_rides(g, W):
    w_in_w, ffn_w = W["w_in"].shape[-1], W["ffn_w_in"].shape[-1]
    p_in = _pieces_from_cols(g["w_in"], _column_segments(w_in_w, True), w_in_w, "pack_w_in")
    p_ffn = _pieces_from_cols(g["ffn_w_in"], _column_segments(ffn_w, False), ffn_w, "pack_ffn_w_in")
    half = p_ffn.shape[1] // 2
    return {"gla": [Xfer(p_in, "slot"), Xfer(g["w_out"][None], "rows")],
            "na": [Xfer(g["ffn_w_out"][None], "rows"), Xfer(p_ffn[:, :half], "slot")],
            "dil1": [Xfer(p_ffn[:, half:], "slot")]}


def _sum_big(got):
    s = lambda r, n: _sum_slots(r.reshape(N_DEV, -1, r.shape[-1]), "sum_" + n)
    return {"w_in": s(got["gla"][0], "w_in"), "w_out": s(got["gla"][1], "w_out"), "ffn_w_out": s(got["na"][0], "ffn_w_out"),
            "ffn_w_in": jnp.concatenate([s(got["na"][1], "ffn_w_in"), s(got["dil1"][0], "ffn_w_in")], axis=0)}


def _exchange_named(rides, extra, name):
    names = list(rides)
    res = _exchange([it for n in names for it in rides[n]] + extra, name)
    got, at = {}, 0
    for n in names:
        got[n] = res[at:at + len(rides[n])]
        at += len(rides[n])
    return got, res[at:]


def _train(x, target, W):
    L = x.shape[0]
    depth = W["w_in"].shape[0]
    cosf, sinf = _rope_tables(L)
    small = jnp.concatenate([W[n].reshape(-1) for n in SMALL_SHARDED])
    small16 = _to_rows(lax.bitcast_convert_type(small, jnp.uint16).reshape(-1), 16)
    got, (sm,) = _exchange_named(_weight_rides(W, 0), [Xfer(small16, "all")], "gather_first")
    full = dict(W, w_in=[], w_out=[], ffn_w_in=[], ffn_w_out=[])
    _unpack_weights(full, W, got)
    sm = lax.bitcast_convert_type(sm.reshape(N_DEV, -1)[:, :2 * small.size].reshape(N_DEV, small.size, 2), F32)
    off = 0
    for n in SMALL_SHARDED:
        full[n] = _unshard(sm[:, off:off + W[n].size].reshape((N_DEV,) + W[n].shape), W[n].ndim - 1)
        off += W[n].size

    saved = []
    for l in range(depth):
        rides = {c: Exchange(it) for c, it in _weight_rides(W, l + 1).items()} if l + 1 < depth else {}
        x, S, got = _layer_fwd(x, full, l, cosf, sinf, rides)
        saved.append(S)
        if l + 1 < depth:
            _unpack_weights(full, W, got)
    loss, dx = _loss_fwd_bwd(x, target)

    grads, big, rides = [None] * depth, [None] * depth, {}
    for l in reversed(range(depth)):
        dx, grads[l], got = _layer_bwd(dx, full, l, saved[l], cosf, sinf, rides)
        if l + 1 < depth:
            big[l + 1] = _sum_big(got)
        rides = {c: Exchange(it) for c, it in _grad_rides(grads[l], W).items()} if l > 0 else {}
    G = {n: jnp.stack([g[n] for g in grads]) for n in SMALL_SHARDED + REPLICATED}
    small_g = jnp.concatenate([_shards(G[n], G[n].ndim - 1).reshape(N_DEV, -1) for n in SMALL_SHARDED], axis=1)
    repl_g = jnp.concatenate([G[n].reshape(-1) for n in REPLICATED])
    got, rest = _exchange_named(_grad_rides(grads[0], W), [Xfer(_to_rows(small_g, 8), "slot"),
                                                            Xfer(_to_rows(repl_g, 8), "all")], "exchange_last")
    big[0] = _sum_big(got)
    out = {n: jnp.stack([b[n] for b in big]).reshape(W[n].shape) for n, _ in BIG}
    for names, r, tag in ((SMALL_SHARDED, rest[0], "sum_small"), (REPLICATED, rest[1], "sum_replicated")):
        flat, off = _sum_slots(r, tag).reshape(-1), 0
        for n in names:
            out[n] = flat[off:off + W[n].size].reshape(W[n].shape)
            off += W[n].size
    return loss, dx, out


def _update(W, G, M, V):
    delta, new_m, new_v = {}, {}, {}
    for n, _ in BIG:
        two_d = lambda a: a.reshape(-1, a.shape[-1])
        d, m, v = _adamw(two_d(W[n]), two_d(G[n]), two_d(M[n]), two_d(V[n]), "adamw_" + n)
        delta[n], new_m[n], new_v[n] = (t.reshape(W[n].shape) for t in (d, m, v))
    rest = SMALL_SHARDED + REPLICATED
    pack = lambda D: _to_rows(jnp.concatenate([D[n].reshape(-1) for n in rest]), 16)
    d, m, v = _adamw(pack(W), pack(G), pack(M), pack(V), "adamw_small")
    off = 0
    for n in rest:
        sl = lambda t: t.reshape(-1)[off:off + W[n].size].reshape(W[n].shape)
        delta[n], new_m[n], new_v[n] = sl(d), sl(m), sl(v)
        off += W[n].size
    return delta, new_m, new_v


def kernel(x, mix_norm_pre, mix_norm_post, w_in, gla_w_gate, gla_b_gate, gla_norm, na_rpb, lru_conv_w, lru_conv_b, lru_w_a, lru_b_a, lru_w_x, lru_b_x, lru_lambda, w_out, ffn_norm_pre, ffn_norm_post, ffn_w_in, ffn_w_out, loss_target, m_mix_norm_pre, m_mix_norm_post, m_w_in, m_gla_w_gate, m_gla_b_gate, m_gla_norm, m_na_rpb, m_lru_conv_w, m_lru_conv_b, m_lru_w_a, m_lru_b_a, m_lru_w_x, m_lru_b_x, m_lru_lambda, m_w_out, m_ffn_norm_pre, m_ffn_norm_post, m_ffn_w_in, m_ffn_w_out, v_mix_norm_pre, v_mix_norm_post, v_w_in, v_gla_w_gate, v_gla_b_gate, v_gla_norm, v_na_rpb, v_lru_conv_w, v_lru_conv_b, v_lru_w_a, v_lru_b_a, v_lru_w_x, v_lru_b_x, v_lru_lambda, v_w_out, v_ffn_norm_pre, v_ffn_norm_post, v_ffn_w_in, v_ffn_w_out):
    W = dict(zip(WEIGHTS, (mix_norm_pre, mix_norm_post, w_in, gla_w_gate, gla_b_gate, gla_norm, na_rpb, lru_conv_w, lru_conv_b, lru_w_a, lru_b_a, lru_w_x, lru_b_x, lru_lambda, w_out, ffn_norm_pre, ffn_norm_post, ffn_w_in, ffn_w_out)))
    M = dict(zip(WEIGHTS, (m_mix_norm_pre, m_mix_norm_post, m_w_in, m_gla_w_gate, m_gla_b_gate, m_gla_norm, m_na_rpb, m_lru_conv_w, m_lru_conv_b, m_lru_w_a, m_lru_b_a, m_lru_w_x, m_lru_b_x, m_lru_lambda, m_w_out, m_ffn_norm_pre, m_ffn_norm_post, m_ffn_w_in, m_ffn_w_out)))
    V = dict(zip(WEIGHTS, (v_mix_norm_pre, v_mix_norm_post, v_w_in, v_gla_w_gate, v_gla_b_gate, v_gla_norm, v_na_rpb, v_lru_conv_w, v_lru_conv_b, v_lru_w_a, v_lru_b_a, v_lru_w_x, v_lru_b_x, v_lru_lambda, v_w_out, v_ffn_norm_pre, v_ffn_norm_post, v_ffn_w_in, v_ffn_w_out)))
    loss, dx, G = _train(x[0], loss_target[0], W)
    loss = lax.psum(loss, MESH_AXES)
    delta, new_m, new_v = _update(W, G, M, V)
    return (loss, dx[None], *[G[n] for n in WEIGHTS], *[delta[n] for n in WEIGHTS], *[new_m[n] for n in WEIGHTS],
            *[new_v[n] for n in WEIGHTS])
```

```python
import functools
import math

import numpy as np
import jax
import jax.numpy as jnp
from jax import lax
from jax.experimental import pallas as pl
from jax.experimental.pallas import tpu as pltpu

F32 = jnp.float32
BF16 = jnp.bfloat16

N_DEV = 8
HEAD_DIM = 64
GROUP_W = 256
GLA_RANK = 16
GLA_TAU = 16.0
GLA_CHUNK = 64
GRID_W = 64
NA_ROWS = 8
NA_COLS = 16
LRU_C = 8.0
DIL_PAIRS = ((128, 1), (512, 4), (2048, 16))
DIL_RADIUS = 64
ROPE_THETA = 10000.0
EPS = 1e-6
ATT_SCALE = HEAD_DIM ** -0.5
NEG = -1e30
LANES = 128
P_COLS = 12 * GROUP_W + LANES
Z_BLOCK = 12 * GROUP_W // LANES

ADAM_LR = 0.001
ADAM_B1 = 0.9
ADAM_B2 = 0.999
ADAM_EPS = 1e-08
ADAM_WD = 0.01
ADAM_STEP = 10

VMEM_LIMIT = 56 * 1024 * 1024
_ARB = lambda n: pltpu.CompilerParams(dimension_semantics=("arbitrary",) * n, vmem_limit_bytes=VMEM_LIMIT)


def _tile(dim, pref, unit):
    t = min(pref, dim) // unit * unit
    while t >= unit:
        if dim % t == 0:
            return t
        t -= unit
    return dim


def _mm(a, b, mode, out_dtype, name, tm=512, tn=None, tk=None):
    if mode == "nn":
        (M, K), (_, N) = a.shape, b.shape
    elif mode == "nt":
        (M, K), (N, _) = a.shape, b.shape
    else:
        (K, M), (_, N) = a.shape, b.shape
    tm = _tile(M, tm, LANES if mode == "tn" else 8)
    tn = _tile(N, tn or N, LANES)
    tk = _tile(K, tk or K, LANES)
    nk = K // tk
    dims = {"nn": (((1,), (0,)), ((), ())), "nt": (((1,), (1,)), ((), ())), "tn": (((0,), (0,)), ((), ()))}[mode]

    def kern(a_ref, b_ref, o_ref, *acc):
        part = lax.dot_general(a_ref[...].astype(BF16), b_ref[...].astype(BF16), dims, preferred_element_type=F32)
        if nk == 1:
            o_ref[...] = part.astype(out_dtype)
            return
        k = pl.program_id(2)

        @pl.when(k == 0)
        def _():
            acc[0][...] = part

        @pl.when(jnp.logical_and(k > 0, k < nk - 1))
        def _():
            acc[0][...] += part

        @pl.when(k == nk - 1)
        def _():
            o_ref[...] = (acc[0][...] + part).astype(out_dtype)

    a_spec = pl.BlockSpec((tk, tm), lambda i, j, k: (k, i)) if mode == "tn" else pl.BlockSpec((tm, tk), lambda i, j, k: (i, k))
    b_spec = pl.BlockSpec((tn, tk), lambda i, j, k: (j, k)) if mode == "nt" else pl.BlockSpec((tk, tn), lambda i, j, k: (k, j))
    return pl.pallas_call(
        kern, name=name, grid=(M // tm, N // tn, nk),
        in_specs=[a_spec, b_spec], out_specs=pl.BlockSpec((tm, tn), lambda i, j, k: (i, j)),
        out_shape=jax.ShapeDtypeStruct((M, N), out_dtype),
        scratch_shapes=[pltpu.VMEM((tm, tn), F32)] if nk > 1 else [],
        compiler_params=_ARB(3),
    )(a, b)


class Row:
    def __init__(self, a, width=None, cb=0, halo=False, dil=1):
        self.a, self.width, self.cb, self.halo, self.dil = a, width, cb, halo, dil


class Full:
    def __init__(self, a):
        self.a = a


HALO = 8


def _rows(name, body, tm, ins, outs):
    outs = [o if len(o) == 4 else o + (1,) for o in outs]
    L = next(s.a.shape[0] * s.dil for s in ins if isinstance(s, Row))
    tm = _tile(L, tm, 16)
    dilated = any(s.dil > 1 for s in ins if isinstance(s, Row)) or any(o[3] > 1 for o in outs)
    nt = L // tm
    nb8 = L // HALO
    step = tm // HALO
    in_specs, arrays, layout = [], [], []
    for s in ins:
        if isinstance(s, Full):
            nd = s.a.ndim
            in_specs.append(pl.BlockSpec(s.a.shape, lambda i, _nd=nd: (0,) * _nd))
            arrays.append(s.a)
            layout.append(1)
        else:
            w = s.width or s.a.shape[1]
            in_specs.append(pl.BlockSpec((tm // s.dil, w), lambda i, _cb=s.cb: (i, _cb)))
            arrays.append(s.a)
            if s.dil > 1:
                layout.append(-s.dil)
            elif s.halo:
                in_specs.append(pl.BlockSpec((HALO, w), lambda i, _cb=s.cb: (jnp.maximum(i * step - 1, 0), _cb)))
                in_specs.append(pl.BlockSpec((HALO, w), lambda i, _cb=s.cb: (jnp.minimum((i + 1) * step, nb8 - 1), _cb)))
                arrays += [s.a, s.a]
                layout.append(3)
            else:
                layout.append(1)
    out_specs, out_shapes = [], []
    for kind, shp, dt, dil in outs:
        if kind == "row":
            out_specs.append(pl.BlockSpec((tm // dil, dil * shp), lambda i: (i, 0)))
            out_shapes.append(jax.ShapeDtypeStruct((L // dil, dil * shp), dt))
        else:
            out_specs.append(pl.BlockSpec(shp, lambda i, _n=len(shp): (0,) * _n))
            out_shapes.append(jax.ShapeDtypeStruct(shp, dt))
    n_in, n_out = len(arrays), len(outs)

    def kern(*refs):
        i = pl.program_id(0)
        lo, hi = refs[n_in + n_out:] if dilated else (None, None)

        def undilate(ref, d):
            for j in range(d):
                rows = pl.ds(j, tm // d, stride=d)
                lo[rows, :] = ref[:, j * GROUP_W:j * GROUP_W + LANES].astype(F32)
                hi[rows, :] = ref[:, j * GROUP_W + LANES:(j + 1) * GROUP_W].astype(F32)
            return jnp.concatenate([lo[...], hi[...]], axis=1)

        def dilate(val, ref, d, dt):
            lo[...] = val[:, :LANES].astype(F32)
            hi[...] = val[:, LANES:].astype(F32)
            for j in range(d):
                rows = pl.ds(j, tm // d, stride=d)
                ref[:, j * GROUP_W:j * GROUP_W + LANES] = lo[rows, :].astype(dt)
                ref[:, j * GROUP_W + LANES:(j + 1) * GROUP_W] = hi[rows, :].astype(dt)

        vals, p = [], 0
        for n in layout:
            if n == 1:
                vals.append(refs[p][...])
            elif n < 0:
                vals.append(undilate(refs[p], -n))
                n = 1
            else:
                vals.append((refs[p + 1][...], refs[p][...], refs[p + 2][...]))
            p += n
        res = body(i, nt, *vals)
        if not isinstance(res, (tuple, list)):
            res = (res,)
        for (kind, shp, dt, dil), o_ref, r in zip(outs, refs[n_in:], res):
            if kind == "row" and dil > 1:
                dilate(r, o_ref, dil, dt)
            elif kind == "row":
                o_ref[...] = r.astype(dt)
            else:
                @pl.when(i == 0)
                def _(o_ref=o_ref):
                    o_ref[...] = jnp.zeros_like(o_ref)
                o_ref[...] += r.astype(dt)

    res = pl.pallas_call(
        kern, name=name, grid=(nt,), in_specs=in_specs, out_specs=out_specs, out_shape=out_shapes,
        scratch_shapes=[pltpu.VMEM((tm, LANES), F32)] * 2 if dilated else [], compiler_params=_ARB(1),
    )(*arrays)
    return res


def _shift(h, o, i, nt):
    prev, cur, nxt = h
    if o == 0:
        return cur
    tm = cur.shape[0]
    cat = jnp.concatenate([prev, cur, nxt], axis=0)
    sh = pltpu.roll(cat, (-o) % (tm + 2 * HALO), axis=0)[HALO:HALO + tm]
    row = lax.broadcasted_iota(jnp.int32, cur.shape, 0)
    if o < 0:
        ok = jnp.logical_or(i > 0, row >= -o)
    else:
        ok = jnp.logical_or(i < nt - 1, row < tm - o)
    return jnp.where(ok, sh, 0.0)


def _colsum(v):
    return jnp.sum(v, axis=0, keepdims=True)


def _sigmoid(x):
    return 1.0 / (1.0 + jnp.exp(-x))


def _softplus(x):
    return jnp.maximum(x, 0.0) + jnp.log1p(jnp.exp(-jnp.abs(x)))


def _silu(x):
    return x * _sigmoid(x)


def _dsilu(x):
    s = _sigmoid(x)
    return s * (1.0 + x * (1.0 - s))


_GELU_C = math.sqrt(2.0 / math.pi)


def _gelu(x):
    return 0.5 * x * (1.0 + jnp.tanh(_GELU_C * (x + 0.044715 * x * x * x)))


def _dgelu(x):
    t = jnp.tanh(_GELU_C * (x + 0.044715 * x * x * x))
    return 0.5 * (1.0 + t) + 0.5 * x * (1.0 - t * t) * _GELU_C * (1.0 + 3.0 * 0.044715 * x * x)


def _head_sum(v, bd):
    return jnp.dot(v, bd, precision=lax.Precision.HIGHEST, preferred_element_type=F32)


def _block_ones(n, blk):
    r = np.arange(n)
    return jnp.asarray((r[:, None] // blk == r[None, :] // blk).astype(np.float32))


def _rms_fwd(x, g, name):
    def body(i, nt, x, g):
        r = lax.rsqrt(jnp.mean(x * x, axis=-1, keepdims=True) + EPS)
        return x * r * g
    return _rows(name, body, 256, [Row(x), Full(g)], [("row", x.shape[1], BF16)])[0]


def _rms_resid_fwd(x, y, g, name):
    def body(i, nt, x, y, g):
        r = lax.rsqrt(jnp.mean(y * y, axis=-1, keepdims=True) + EPS)
        return x + y * r * g
    return _rows(name, body, 256, [Row(x), Row(y), Full(g)], [("row", x.shape[1], F32)])[0]


def _rms_bwd(dy, x, g, name, resid=None, out_dtype=F32):
    D = x.shape[1]

    def body(i, nt, dy, x, g, *rest):
        dy = dy.astype(F32)
        r = lax.rsqrt(jnp.mean(x * x, axis=-1, keepdims=True) + EPS)
        xh = x * r
        dxh = dy * g
        dx = r * (dxh - xh * jnp.mean(dxh * xh, axis=-1, keepdims=True))
        if rest:
            dx = dx + rest[0]
        return dx, _colsum(dy * xh)

    ins = [Row(dy), Row(x), Full(g)] + ([Row(resid)] if resid is not None else [])
    return _rows(name, body, 256, ins, [("row", D, out_dtype), ("acc", (1, D), F32)])


def _ffn_in_swiglu(h, w):
    (M, K), N = h.shape, w.shape[1]
    F = N // 2
    tm = _tile(M, 256, 16)

    def kern(a_ref, b_ref, gu_ref, act_ref):
        gu = _dot(a_ref[...], b_ref[...])
        gu_ref[...] = gu
        act_ref[...] = (_silu(gu[:, :F]) * gu[:, F:]).astype(BF16)

    return pl.pallas_call(
        kern, name="ffn_in_swiglu", grid=(M // tm,),
        in_specs=[pl.BlockSpec((tm, K), lambda i: (i, 0)), pl.BlockSpec((K, N), lambda i: (0, 0))],
        out_specs=[pl.BlockSpec((tm, N), lambda i: (i, 0)), pl.BlockSpec((tm, F), lambda i: (i, 0))],
        out_shape=[jax.ShapeDtypeStruct((M, N), F32), jax.ShapeDtypeStruct((M, F), BF16)], compiler_params=_ARB(1),
    )(h, w)


def _ffn_out_dx_swiglu(df, w, gu):
    (M, K), N = df.shape, gu.shape[1]
    F = N // 2
    tm = _tile(M, 256, 16)

    def kern(a_ref, b_ref, gu_ref, o_ref):
        da = _dot(a_ref[...], b_ref[...], _NT)
        gu = gu_ref[...]
        gate, up = gu[:, :F], gu[:, F:]
        o_ref[:, :F] = (da * up * _dsilu(gate)).astype(BF16)
        o_ref[:, F:] = (da * _silu(gate)).astype(BF16)

    return pl.pallas_call(
        kern, name="ffn_out_dx_swiglu", grid=(M // tm,),
        in_specs=[pl.BlockSpec((tm, K), lambda i: (i, 0)), pl.BlockSpec((F, K), lambda i: (0, 0)),
                  pl.BlockSpec((tm, N), lambda i: (i, 0))],
        out_specs=pl.BlockSpec((tm, N), lambda i: (i, 0)), out_shape=jax.ShapeDtypeStruct((M, N), BF16),
        compiler_params=_ARB(1),
    )(df, w, gu)


def _loss_fwd_bwd(y, target):
    D = y.shape[1]

    def body(i, nt, y, t):
        err = y - t
        part = 0.5 * jnp.sum(jnp.mean(err * err, axis=-1, keepdims=True), axis=0, keepdims=True)
        return err * (1.0 / D), jnp.broadcast_to(part, (1, LANES))
    dy, loss = _rows("loss", body, 256, [Row(y), Row(target)], [("row", D, F32), ("acc", (1, LANES), F32)])
    return loss[0, 0], dy


def _adamw(w, g, m, v, name):
    C = w.shape[1]
    bc1 = 1.0 - ADAM_B1 ** ADAM_STEP
    bc2 = 1.0 - ADAM_B2 ** ADAM_STEP

    def body(i, nt, w, g, m, v):
        m = ADAM_B1 * m + (1.0 - ADAM_B1) * g
        v = ADAM_B2 * v + (1.0 - ADAM_B2) * (g * g)
        delta = -ADAM_LR * ((m / bc1) / (jnp.sqrt(v / bc2) + ADAM_EPS) + ADAM_WD * w)
        return delta, m, v
    return _rows(name, body, 256, [Row(w), Row(g), Row(m), Row(v)], [("row", C, F32)] * 3)


def _expm1(x):
    return jnp.tanh(0.5 * x) * (jnp.exp(x) + 1.0)


def _lru_gates(xh, i, nt, cw, cb, wa, wx, ba, bx, lam):
    xc = cb
    for j in range(4):
        xc = xc + cw[j:j + 1] * _shift(xh, j - 2, i, nt)
    xcb = xc.astype(BF16)
    gates = []
    for e in range(2):
        r = _sigmoid(jnp.dot(xcb, wa[e], preferred_element_type=F32) + ba[e:e + 1])
        ig = _sigmoid(jnp.dot(xcb, wx[e], preferred_element_type=F32) + bx[e:e + 1])
        sp = _softplus(-lam[e:e + 1])
        la = -LRU_C * r * sp
        gates.append((r, ig, sp, jnp.exp(la), jnp.sqrt(-_expm1(2.0 * la))))
    return xc, xcb, gates


def _scan2(af, uf, ab, ub, adjoint, name):
    L, W = af.shape
    tm = _tile(L, 512, 8)
    nt, nb = L // tm, tm // 8

    def blk(A, U, h, reverse, row):
        for d in (1, 2, 4):
            if reverse:
                ok, sh = row < 8 - d, 8 - d
            else:
                ok, sh = row >= d, d
            As = jnp.where(ok, pltpu.roll(A, sh, axis=0), 1.0)
            Us = jnp.where(ok, pltpu.roll(U, sh, axis=0), 0.0)
            U = A * Us + U
            A = A * As
        return A * h + U

    def kern(af_ref, uf_ref, ab_ref, ub_ref, of_ref, ob_ref, c_ref):
        @pl.when(pl.program_id(0) == 0)
        def _():
            c_ref[...] = jnp.zeros_like(c_ref)

        row = lax.broadcasted_iota(jnp.int32, (8, W), 0)
        full = lambda v: jnp.broadcast_to(v, (8, W))

        def body(j, carry):
            hF, aF, hB, aB = carry
            r0 = pl.multiple_of(j * 8, 8)
            r1 = pl.multiple_of((nb - 1 - j) * 8, 8)
            A, U = af_ref[pl.ds(r0, 8), :], uf_ref[pl.ds(r0, 8), :]
            if adjoint:
                C = jnp.where(row == 0, aF, pltpu.roll(A, 1, axis=0))
                aF = full(A[7:8])
            else:
                C = A
            H = blk(C, U, hF, False, row)
            of_ref[pl.ds(r0, 8), :] = H
            hF = full(H[7:8])
            A, U = ab_ref[pl.ds(r1, 8), :], ub_ref[pl.ds(r1, 8), :]
            if adjoint:
                C = jnp.where(row == 7, aB, pltpu.roll(A, 7, axis=0))
                aB = full(A[0:1])
            else:
                C = A
            H = blk(C, U, hB, True, row)
            ob_ref[pl.ds(r1, 8), :] = H
            hB = full(H[0:1])
            return hF, aF, hB, aB

        carry = lax.fori_loop(0, nb, body, (c_ref[0], c_ref[1], c_ref[2], c_ref[3]))
        for n in range(4):
            c_ref[n] = carry[n]

    fwd = pl.BlockSpec((tm, W), lambda i: (i, 0))
    bwd = pl.BlockSpec((tm, W), lambda i: (nt - 1 - i, 0))
    return pl.pallas_call(
        kern, name=name, grid=(nt,), in_specs=[fwd, fwd, bwd, bwd], out_specs=[fwd, bwd],
        out_shape=[jax.ShapeDtypeStruct((L, W), F32)] * 2,
        scratch_shapes=[pltpu.VMEM((4, 8, W), F32)], compiler_params=_ARB(1),
    )(af, uf, ab, ub)


def _block_diag(w):
    out = jnp.zeros((2, GROUP_W, GROUP_W), w.dtype)
    for h in range(4):
        out = out.at[:, h * 64:(h + 1) * 64, h * 64:(h + 1) * 64].set(w[:, h])
    return out.astype(BF16)


def _diag_blocks(w):
    return jnp.stack([w[:, h * 64:(h + 1) * 64, h * 64:(h + 1) * 64] for h in range(4)], axis=1)


def _lru_params(W, l):
    return [Full(W["lru_conv_w"][l]), Full(W["lru_conv_b"][l][None]), Full(_block_diag(W["lru_w_a"][l])),
            Full(_block_diag(W["lru_w_x"][l])), Full(W["lru_b_a"][l]), Full(W["lru_b_x"][l]), Full(W["lru_lambda"][l])]


def _lru_fwd(p, W, l):
    def pre(i, nt, xh, *prm):
        xc, _, g = _lru_gates(xh, i, nt, *prm)
        return g[0][3], g[0][4] * (g[0][1] * xc), g[1][3], g[1][4] * (g[1][1] * xc)

    a0, u0, a1, u1 = _rows("lru_pre", pre, 256, [Row(p, GROUP_W, 7, halo=True)] + _lru_params(W, l),
                           [("row", GROUP_W, F32)] * 4)
    hf, hb = _scan2(a0, u0, a1, u1, False, "lru_scan")
    yc = _rows("lru_post", lambda i, nt, hf, hb, gc: (hf + hb) * _gelu(gc), 512,
               [Row(hf), Row(hb), Row(p, GROUP_W, 8)], [("row", GROUP_W, BF16)])[0]
    return yc, (a0, a1, hf, hb)


def _lru_bwd(dy, dy_cb, p, W, l, saved):
    a0, a1, hf, hb = saved

    def post(i, nt, dy, hf, hb, gc):
        return dy * _gelu(gc), dy * (hf + hb) * _dgelu(gc)

    dh, dgc = _rows("lru_post_bwd", post, 512, [Row(dy, GROUP_W, dy_cb), Row(hf), Row(hb), Row(p, GROUP_W, 8)],
                    [("row", GROUP_W, F32), ("row", GROUP_W, BF16)])
    gb, gf = _scan2(a1, dh, a0, dh, True, "lru_scan_adj")

    def gates_bwd(i, nt, xh, gf, gb, hfh, hbh, cw, cb, wa, wx, ba, bx, lam):
        xc, xcb, g = _lru_gates(xh, i, nt, cw, cb, wa, wx, ba, bx, lam)
        dxc = jnp.zeros_like(xc)
        dwa, dwx, dba, dbx, dlam = [], [], [], [], []
        for e, du, hprev in ((0, gf, _shift(hfh, -1, i, nt)), (1, gb, _shift(hbh, 1, i, nt))):
            r, ig, sp, a, s = g[e]
            dxc = dxc + du * s * ig
            dla = du * hprev * a - (du * ig * xc) * a * a / s
            dza = (dla * (-LRU_C) * sp) * r * (1.0 - r)
            dzx = (du * s * xc) * ig * (1.0 - ig)
            dlam.append(_colsum(dla * r) * (LRU_C * _sigmoid(-lam[e:e + 1])))
            dba.append(_colsum(dza))
            dbx.append(_colsum(dzx))
            dzab, dzxb = dza.astype(BF16), dzx.astype(BF16)
            tn = (((0,), (0,)), ((), ()))
            nt_ = (((1,), (1,)), ((), ()))
            dwa.append(lax.dot_general(xcb, dzab, tn, preferred_element_type=F32))
            dwx.append(lax.dot_general(xcb, dzxb, tn, preferred_element_type=F32))
            dxc = dxc + lax.dot_general(dzab, wa[e], nt_, preferred_element_type=F32)
            dxc = dxc + lax.dot_general(dzxb, wx[e], nt_, preferred_element_type=F32)
        cat = lambda v: jnp.concatenate(v, axis=0)
        return dxc, jnp.stack(dwa), jnp.stack(dwx), cat(dba), cat(dbx), cat(dlam)

    dxc, dwa, dwx, dba, dbx, dlam = _rows(
        "lru_gates_bwd", gates_bwd, 256,
        [Row(p, GROUP_W, 7, halo=True), Row(gf), Row(gb), Row(hf, halo=True), Row(hb, halo=True)] + _lru_params(W, l),
        [("row", GROUP_W, F32), ("acc", (2, GROUP_W, GROUP_W), F32), ("acc", (2, GROUP_W, GROUP_W), F32),
         ("acc", (2, GROUP_W), F32), ("acc", (2, GROUP_W), F32), ("acc", (2, GROUP_W), F32)])

    def conv_bwd(i, nt, dh_, xh, cw):
        dxb = jnp.zeros_like(dh_[1])
        dcw = []
        for j in range(4):
            dxb = dxb + cw[j:j + 1] * _shift(dh_, 2 - j, i, nt)
            dcw.append(_colsum(dh_[1] * _shift(xh, j - 2, i, nt)))
        return dxb, jnp.concatenate(dcw, axis=0), _colsum(dh_[1])

    dxb, dcw, dcb = _rows("lru_conv_bwd", conv_bwd, 512,
                          [Row(dxc, halo=True), Row(p, GROUP_W, 7, halo=True), Full(W["lru_conv_w"][l])],
                          [("row", GROUP_W, BF16), ("acc", (4, GROUP_W), F32), ("acc", (1, GROUP_W), F32)])
    grads = dict(lru_conv_w=dcw, lru_conv_b=dcb[0], lru_w_a=_diag_blocks(dwa), lru_w_x=_diag_blocks(dwx),
                 lru_b_a=dba, lru_b_x=dbx, lru_lambda=dlam)
    return dxb, dgc, grads


_NT = (((1,), (1,)), ((), ()))
_TN = (((0,), (0,)), ((), ()))


def _dot(a, b, dims=None):
    if dims is None:
        return jnp.dot(a, b, preferred_element_type=F32)
    return lax.dot_general(a, b, dims, preferred_element_type=F32)


def _dot_exact(a, b):
    return jnp.dot(a, b, precision=lax.Precision.HIGHEST, preferred_element_type=F32)


def _gla_gate_w(w_gate, b_gate):
    wg = jnp.zeros((LANES, 2 * GROUP_W), F32)
    for e in range(2):
        wg = wg.at[e * GLA_RANK:(e + 1) * GLA_RANK, e * GROUP_W:(e + 1) * GROUP_W].set(w_gate[e])
    return wg.astype(BF16), b_gate.reshape(1, 2 * GROUP_W)


def _gla_gates_fwd(p, wg, bg):
    def body(i, nt, z, wg, bg):
        logit = _dot(z.astype(BF16), wg) + bg
        la = -_softplus(-logit) * (1.0 / GLA_TAU)
        return la[:, :GROUP_W], la[:, GROUP_W:]
    return _rows("gla_gates", body, 512, [Row(p, LANES, Z_BLOCK), Full(wg), Full(bg)], [("row", GROUP_W, F32)] * 2)


def _gla_gates_bwd(p, dla0, dla1, wg, bg):
    def body(i, nt, z, d0, d1, wg, bg):
        zb = z.astype(BF16)
        logit = _dot(zb, wg) + bg
        dlogit = jnp.concatenate([d0, d1], axis=1) * (1.0 / GLA_TAU) * _sigmoid(-logit)
        dlb = dlogit.astype(BF16)
        return _dot(dlb, wg, _NT), _dot(zb, dlb, _TN), _colsum(dlogit)
    return _rows("gla_gates_bwd", body, 512, [Row(p, LANES, Z_BLOCK), Row(dla0), Row(dla1), Full(wg), Full(bg)],
                 [("row", LANES, BF16), ("acc", (LANES, 2 * GROUP_W), F32), ("acc", (1, 2 * GROUP_W), F32)])


def _gla_order(reverse):
    t = np.arange(GLA_CHUNK)
    m = (t[None, :] >= t[:, None]) if reverse else (t[None, :] <= t[:, None])
    return m.astype(np.float32), (32, 0) if reverse else (31, 63)


def _stack_heads(x, bd):
    return jnp.where(bd, jnp.concatenate([x] * 4, axis=0), 0.0)


def _diag_heads(r, bd):
    r = jnp.where(bd, r, 0.0)
    return r[0:64] + r[64:128] + r[128:192] + r[192:256]


def _gla_factors(q_ref, k_ref, rows, b, mid, last):
    bm, bl = b[mid:mid + 1], b[last:last + 1]
    qs = q_ref[rows, :] * ATT_SCALE
    k = k_ref[rows, :]
    P, N, E, Fd = jnp.exp(b - bm), jnp.exp(bm - b), jnp.exp(b), jnp.exp(bl - b)
    return (P, N, E, Fd, jnp.exp(bl)), (qs * P, k * N, qs * E, k * Fd)


def _gla_specs(L, walk_up):
    tm = _tile(L, 512, GLA_CHUNK)
    nt, nc = L // tm, tm // GLA_CHUNK
    specs = []
    for up in walk_up:
        t = (lambda i: i) if up else (lambda i: nt - 1 - i)
        specs.append(dict(
            col=lambda cb, _t=t: pl.BlockSpec((tm, GROUP_W), lambda i: (_t(i), cb)),
            row=pl.BlockSpec((tm, GROUP_W), lambda i, _t=t: (_t(i), 0)),
            state=pl.BlockSpec((nc, GROUP_W, GROUP_W), lambda i, _t=t: (_t(i), 0, 0))))
    return nt, nc, specs


def _gla_chunk_fwd(p, la0, la1, ride=None):
    L = la0.shape[0]
    nt, nc, specs = _gla_specs(L, (True, False))
    orders = [_gla_order(False), _gla_order(True)]

    def kern(q0, k0, v0, l0, q1, k1, v1, l1, m0_ref, m1_ref, bd_ref, o0, s0, o1, s1, st_ref):
        @pl.when(pl.program_id(0) == 0)
        def _():
            st_ref[...] = jnp.zeros_like(st_ref)

        bd = bd_ref[...] > 0.5
        dirs = []
        for e, (q_ref, k_ref, v_ref, la_ref, m_ref, o_ref, s_ref) in enumerate(
                ((q0, k0, v0, l0, m0_ref, o0, s0), (q1, k1, v1, l1, m1_ref, o1, s1))):
            mv = m_ref[...]
            dirs.append((q_ref, k_ref, v_ref, la_ref, mv, jnp.concatenate([mv] * 4, axis=0) > 0.5, o_ref, s_ref))

        def body(cc, carry):
            E = range(2)
            cs = [nc - 1 - cc if e else cc for e in E]
            rows = [pl.ds(pl.multiple_of(c * GLA_CHUNK, GLA_CHUNK), GLA_CHUNK) for c in cs]
            b = [_dot_exact(dirs[e][4], dirs[e][3][rows[e], :]) for e in E]
            t = [_gla_factors(dirs[e][0], dirs[e][1], rows[e], b[e], *orders[e][1]) for e in E]
            vb = [dirs[e][2][rows[e], :].astype(BF16) for e in E]
            st = [st_ref[e] for e in E]
            a = [_dot(_stack_heads(t[e][1][0], bd).astype(BF16), t[e][1][1].astype(BF16), _NT) for e in E]
            inter = [_dot(t[e][1][2].astype(BF16), st[e].astype(BF16), _NT) for e in E]
            kv = [_dot(vb[e], t[e][1][3].astype(BF16), _TN) for e in E]
            a = [jnp.where(dirs[e][5], a[e], 0.0).astype(BF16) for e in E]
            r = [_dot(a[e], vb[e]) for e in E]
            for e in E:
                dirs[e][7][cs[e]] = st[e]
                dirs[e][6][rows[e], :] = _diag_heads(r[e], bd) + inter[e]
                st_ref[e] = st[e] * t[e][0][4] + jnp.where(bd, kv[e], 0.0)
            return carry

        lax.fori_loop(0, nc, body, 0)

    const = lambda shp: pl.BlockSpec(shp, lambda i: (0, 0))
    in_specs, out_specs = [], []
    for sp in specs:
        in_specs += [sp["col"](0), sp["col"](1), sp["col"](2), sp["row"]]
        out_specs += [sp["row"], sp["state"]]
    return _call(
        kern, (p, p, p, la0, p, p, p, la1, jnp.asarray(orders[0][0]), jnp.asarray(orders[1][0]),
               _block_ones(GROUP_W, HEAD_DIM)),
        ride, lambda: (pl.program_id(0) == 0, pl.program_id(0) == nt - 1), name="gla_fwd", grid=(nt,),
        in_specs=in_specs + [const((GLA_CHUNK, GLA_CHUNK))] * 2 + [const((GROUP_W, GROUP_W))], out_specs=out_specs,
        out_shape=[jax.ShapeDtypeStruct((L, GROUP_W), F32),
                   jax.ShapeDtypeStruct((L // GLA_CHUNK, GROUP_W, GROUP_W), F32)] * 2,
        scratch_shapes=[pltpu.VMEM((2, GROUP_W, GROUP_W), F32)])


def _gla_chunk_bwd(p, la0, la1, do, sprev0, sprev1, ride=None):
    L = la0.shape[0]
    nt, nc, specs = _gla_specs(L, (False, True))
    orders = [_gla_order(False), _gla_order(True)]

    def kern(q0, k0, v0, l0, do0, s0, q1, k1, v1, l1, do1, s1, m0_ref, m1_ref, t0_ref, t1_ref, bd_ref, *rest):
        outs, dst_ref = (rest[0:4], rest[4:8]), rest[8]

        @pl.when(pl.program_id(0) == 0)
        def _():
            dst_ref[...] = jnp.zeros_like(dst_ref)

        bd = bd_ref[...] > 0.5
        row = lax.broadcasted_iota(jnp.int32, (GLA_CHUNK, GROUP_W), 0)
        dirs = []
        for ins, m_ref, t_ref in (((q0, k0, v0, l0, do0, s0), m0_ref, t0_ref), ((q1, k1, v1, l1, do1, s1), m1_ref, t1_ref)):
            mv = m_ref[...]
            dirs.append(ins + (mv, t_ref[...], jnp.concatenate([mv] * 4, axis=0) > 0.5))

        def body(cc, carry):
            E2 = range(2)
            cs = [cc if e else nc - 1 - cc for e in E2]
            rows = [pl.ds(pl.multiple_of(c * GLA_CHUNK, GLA_CHUNK), GLA_CHUNK) for c in cs]
            b = [_dot_exact(dirs[e][6], dirs[e][3][rows[e], :]) for e in E2]
            t = [_gla_factors(dirs[e][0], dirs[e][1], rows[e], b[e], *orders[e][1]) for e in E2]
            vb = [dirs[e][2][rows[e], :].astype(BF16) for e in E2]
            dov = [dirs[e][4][rows[e], :] for e in E2]
            dob = [x.astype(BF16) for x in dov]
            st = [dirs[e][5][cs[e]] for e in E2]
            dst = [dst_ref[e] for e in E2]
            stb, dstb = [x.astype(BF16) for x in st], [x.astype(BF16) for x in dst]
            qst = [_stack_heads(t[e][1][0], bd).astype(BF16) for e in E2]
            dost = [_stack_heads(dov[e], bd).astype(BF16) for e in E2]
            kNb, qEb, kFb = ([t[e][1][n].astype(BF16) for e in E2] for n in (1, 2, 3))
            a = [_dot(qst[e], kNb[e], _NT) for e in E2]
            da = [_dot(dost[e], vb[e], _NT) for e in E2]
            dqE = [_dot(dob[e], stb[e]) for e in E2]
            dkF = [_dot(vb[e], dstb[e]) for e in E2]
            dv_inter = [_dot(kFb[e], dstb[e], _NT) for e in E2]
            dst_in = [_dot(dob[e], qEb[e], _TN) for e in E2]
            a = [jnp.where(dirs[e][8], a[e], 0.0).astype(BF16) for e in E2]
            da = [jnp.where(dirs[e][8], da[e], 0.0).astype(BF16) for e in E2]
            dv_intra = [_dot(a[e], dost[e], _TN) for e in E2]
            dqP = [_dot(da[e], kNb[e]) for e in E2]
            dkN = [_dot(da[e], qst[e], _TN) for e in E2]
            db = []
            for e in E2:
                (P, N, Ef, Fd, d), (qP, kN, qE, kF) = t[e]
                mid, last = orders[e][1]
                dq_ref, dk_ref, dv_ref, _ = outs[e]
                dqp = _diag_heads(dqP[e], bd)
                dd = _colsum(dst[e] * st[e])
                dst_ref[e] = jnp.where(bd, dst_in[e], 0.0) + dst[e] * d
                tP, tN, tE, tF = dqp * qP, dkN[e] * kN, dqE[e] * qE, dkF[e] * kF
                db.append(tP - tN + tE - tF + jnp.where(row == mid, _colsum(tN - tP), 0.0)
                          + jnp.where(row == last, _colsum(tF) + dd * d, 0.0))
                dq_ref[rows[e], :] = (dqp * P + dqE[e] * Ef) * ATT_SCALE
                dk_ref[rows[e], :] = dkN[e] * N + dkF[e] * Fd
                dv_ref[rows[e], :] = dv_intra[e] + dv_inter[e]
            dla = [_dot_exact(dirs[e][7], db[e]) for e in E2]
            for e in E2:
                outs[e][3][rows[e], :] = dla[e]
            return carry

        lax.fori_loop(0, nc, body, 0)

    const = lambda shp: pl.BlockSpec(shp, lambda i: (0, 0))
    in_specs, out_specs = [], []
    for sp in specs:
        in_specs += [sp["col"](0), sp["col"](1), sp["col"](2), sp["row"], sp["row"], sp["state"]]
        out_specs += [sp["row"]] * 4
    m0, m1 = orders[0][0], orders[1][0]
    return _call(
        kern, (p, p, p, la0, do, sprev0, p, p, p, la1, do, sprev1, jnp.asarray(m0), jnp.asarray(m1),
               jnp.asarray(m0.T.copy()), jnp.asarray(m1.T.copy()), _block_ones(GROUP_W, HEAD_DIM)),
        ride, lambda: (pl.program_id(0) == 0, pl.program_id(0) == nt - 1), name="gla_bwd", grid=(nt,),
        in_specs=in_specs + [const((GLA_CHUNK, GLA_CHUNK))] * 4 + [const((GROUP_W, GROUP_W))], out_specs=out_specs,
        out_shape=[jax.ShapeDtypeStruct((L, GROUP_W), F32)] * 8,
        scratch_shapes=[pltpu.VMEM((2, GROUP_W, GROUP_W), F32)])


def _gla_fwd(p, W, l, ride=None):
    wg, bg = _gla_gate_w(W["gla_w_gate"][l], W["gla_b_gate"][l])
    la0, la1 = _gla_gates_fwd(p, wg, bg)
    (of, s0, ob, s1), got = _gla_chunk_fwd(p, la0, la1, ride)

    def post(i, nt, of, ob, g, ng, bd):
        o = of + ob
        r = lax.rsqrt(_head_sum(o * o, bd) * (1.0 / HEAD_DIM) + EPS)
        return o * r * ng * _silu(g)

    ya = _rows("gla_post", post, 512, [Row(of), Row(ob), Row(p, GROUP_W, 3), Full(W["gla_norm"][l][None]),
                                       Full(_block_ones(GROUP_W, HEAD_DIM))], [("row", GROUP_W, BF16)])[0]
    return ya, (la0, la1, of, ob, s0, s1), got


def _gla_bwd(dy, dy_cb, p, W, l, saved, ride=None):
    la0, la1, of, ob, s0, s1 = saved
    wg, bg = _gla_gate_w(W["gla_w_gate"][l], W["gla_b_gate"][l])

    def post(i, nt, dy, of, ob, g, ng, bd):
        o = of + ob
        r = lax.rsqrt(_head_sum(o * o, bd) * (1.0 / HEAD_DIM) + EPS)
        oh = o * r
        don = dy * _silu(g)
        doh = don * ng
        do = r * (doh - oh * _head_sum(doh * oh, bd) * (1.0 / HEAD_DIM))
        return do, dy * (oh * ng) * _dsilu(g), _colsum(don * oh)

    do, dg, dng = _rows("gla_post_bwd", post, 512,
                        [Row(dy, GROUP_W, dy_cb), Row(of), Row(ob), Row(p, GROUP_W, 3), Full(W["gla_norm"][l][None]),
                         Full(_block_ones(GROUP_W, HEAD_DIM))],
                        [("row", GROUP_W, F32), ("row", GROUP_W, BF16), ("acc", (1, GROUP_W), F32)])
    (dq0, dk0, dv0, dla0, dq1, dk1, dv1, dla1), got = _gla_chunk_bwd(p, la0, la1, do, s0, s1, ride)
    dq, dk, dv = _rows("gla_sum_bwd", lambda i, nt, a0, a1, b0, b1, c0, c1: (a0 + a1, b0 + b1, c0 + c1), 512,
                       [Row(t) for t in (dq0, dq1, dk0, dk1, dv0, dv1)], [("row", GROUP_W, BF16)] * 3)
    dz, dwg, dbg = _gla_gates_bwd(p, dla0, dla1, wg, bg)
    dw_gate = jnp.stack([dwg[e * GLA_RANK:(e + 1) * GLA_RANK, e * GROUP_W:(e + 1) * GROUP_W] for e in range(2)])
    grads = dict(gla_w_gate=dw_gate, gla_b_gate=dbg.reshape(2, GROUP_W), gla_norm=dng[0])
    return (dq, dk, dv, dg, dz), grads, got


def _rope_tables(L):
    pos = jnp.arange(L, dtype=F32)
    inv_freq = ROPE_THETA ** (-jnp.arange(0, HEAD_DIM, 2, dtype=F32) / HEAD_DIM)
    ang = pos[:, None] * inv_freq[None, :]
    cos, sin = jnp.cos(ang), jnp.sin(ang)
    return jnp.tile(jnp.concatenate([cos, cos], axis=1), (1, 4)), jnp.tile(jnp.concatenate([-sin, sin], axis=1), (1, 4))


def _swap_halves(t):
    lane = lax.broadcasted_iota(jnp.int32, t.shape, 1)
    first = (lane & (HEAD_DIM - 1)) < HEAD_DIM // 2
    return jnp.where(first, pltpu.roll(t, GROUP_W - HEAD_DIM // 2, axis=1), pltpu.roll(t, HEAD_DIM // 2, axis=1))


def _attn_prep(p, cosf, sinf):
    dils = [dil for _, dil in DIL_PAIRS]

    def body(i, nt, qb, kb, vb, qd, kd, vd, c, s):
        d = (qd * c + _swap_halves(qd) * s, kd * c + _swap_halves(kd) * s, vd)
        return (qb, kb, vb) + d * len(dils)
    ins = [Row(p, GROUP_W, cb) for cb in (4, 5, 6, 9, 10, 11)] + [Row(cosf), Row(sinf)]
    outs = [("row", GROUP_W, BF16)] * 3 + [("row", GROUP_W, BF16, dil) for dil in dils for _ in range(3)]
    res = _rows("attn_prep", body, 512, ins, outs)
    return tuple(res[:3]), {dil: tuple(res[3 + 3 * n:6 + 3 * n]) for n, dil in enumerate(dils)}


def _na_onehot():
    c = np.arange(GRID_W)
    dc = np.clip(c[None, :] - c[:, None], -(NA_COLS - 1), NA_COLS - 1) + NA_COLS - 1
    oh = np.zeros((LANES, GRID_W * GRID_W), np.float32)
    oh[dc.reshape(-1), np.arange(GRID_W * GRID_W)] = 1.0
    return jnp.asarray(oh)


def _na_colmask():
    c = np.arange(GRID_W)
    start = np.clip(c - NA_COLS // 2, 0, GRID_W - NA_COLS)
    ok = (c[None, :] >= start[:, None]) & (c[None, :] < start[:, None] + NA_COLS)
    return jnp.asarray(np.where(ok, 0.0, NEG).astype(np.float32))


N_DR = 2 * NA_ROWS - 1


NA_HALF = GRID_W // 2
NA_KCOLS = 48
NA_WIN = NA_ROWS * NA_KCOLS
NA_ROWS_PER_STEP = 2


def _na_bias(rpb):
    rp = jnp.zeros((GRID_W, LANES), F32).at[:4 * N_DR, :2 * NA_COLS - 1].set(rpb.reshape(4 * N_DR, 2 * NA_COLS - 1))

    def expand(r_ref, oh_ref, o_ref):
        o_ref[...] = _dot_exact(r_ref[...], oh_ref[...])

    r = pl.pallas_call(expand, name="na_bias_expand",
                       out_shape=jax.ShapeDtypeStruct((GRID_W, GRID_W * GRID_W), F32))(rp, _na_onehot())
    r = r[:4 * N_DR].reshape(4, N_DR, GRID_W, GRID_W)

    def build(r_ref, m_ref, o_ref):
        for h in range(4):
            for c in range(NA_ROWS):
                for half in range(2):
                    q0, k0 = NA_HALF * half, 16 * half
                    for i in range(NA_ROWS):
                        o_ref[h, c, half, :, i * NA_KCOLS:(i + 1) * NA_KCOLS] = (
                            r_ref[h, i - c + NA_ROWS - 1, q0:q0 + NA_HALF, k0:k0 + NA_KCOLS]
                            + m_ref[q0:q0 + NA_HALF, k0:k0 + NA_KCOLS])

    return pl.pallas_call(build, name="na_bias_build",
                          out_shape=jax.ShapeDtypeStruct((4, NA_ROWS, 2, NA_HALF, NA_WIN), F32))(r, _na_colmask())


def _na_bias_bwd(dbias):
    def fold(d_ref, o_ref):
        o_ref[...] = jnp.zeros_like(o_ref)
        for h in range(4):
            for a in range(N_DR):
                for half in range(2):
                    q0, k0 = NA_HALF * half, 16 * half
                    acc = jnp.zeros((NA_HALF, NA_KCOLS), F32)
                    for c in range(NA_ROWS):
                        i = a + c - (NA_ROWS - 1)
                        if 0 <= i < NA_ROWS:
                            acc = acc + d_ref[h, c, half, :, i * NA_KCOLS:(i + 1) * NA_KCOLS]
                    o_ref[h, a, q0:q0 + NA_HALF, k0:k0 + NA_KCOLS] = acc

    dr = pl.pallas_call(fold, name="na_bias_fold",
                        out_shape=jax.ShapeDtypeStruct((4, N_DR, GRID_W, GRID_W), F32))(dbias)
    dr = jnp.zeros((GRID_W, GRID_W * GRID_W), F32).at[:4 * N_DR].set(dr.reshape(4 * N_DR, GRID_W * GRID_W))

    def contract(d_ref, oh_ref, o_ref):
        o_ref[...] = lax.dot_general(d_ref[...], oh_ref[...], _NT, precision=lax.Precision.HIGHEST,
                                     preferred_element_type=F32)

    g = pl.pallas_call(contract, name="na_bias_contract",
                       out_shape=jax.ShapeDtypeStruct((GRID_W, LANES), F32))(dr, _na_onehot())
    return g[:4 * N_DR, :2 * NA_COLS - 1].reshape(4, N_DR, 2 * NA_COLS - 1)


def _na_window(r, n_rows):
    rs = jnp.clip(r - NA_ROWS // 2, 0, n_rows - NA_ROWS)
    return rs, r - rs


def _na_key_rows(rs, half, t):
    return pl.ds(pl.multiple_of((rs + t) * GRID_W + 16 * half, 16), NA_KCOLS)


def _na_keys(ref, rs, half):
    return jnp.concatenate([ref[_na_key_rows(rs, half, t), :] for t in range(NA_ROWS)], axis=0)


def _na_stack(x, first):
    zero = jnp.zeros_like(x)
    return jnp.concatenate([jnp.where(first, x, zero), jnp.where(first, zero, x)], axis=0)


def _na_bias_spec():
    return pl.BlockSpec((2, NA_ROWS, 2, NA_HALF, NA_WIN), lambda j, i: (j, 0, 0, 0, 0))


def _grid_edges(n0, n1):
    j, i = pl.program_id(0), pl.program_id(1)
    return jnp.logical_and(j == 0, i == 0), jnp.logical_and(j == n0 - 1, i == n1 - 1)


def _na_fwd(q, k, v, bias, ride=None):
    L = q.shape[0]
    n_rows = L // GRID_W
    tm = _tile(L, 512, GRID_W)
    nt, nr = L // tm, tm // GRID_W

    def kern(q_ref, k_ref, v_ref, b_ref, o_ref):
        i = pl.program_id(1)
        first = lax.broadcasted_iota(jnp.int32, (NA_HALF, LANES), 1) < HEAD_DIM

        def body(it, carry):
            parts = []
            for u in range(NA_ROWS_PER_STEP):
                rr = it * NA_ROWS_PER_STEP + u
                rs, c = _na_window(i * nr + rr, n_rows)
                for half in range(2):
                    rows = pl.ds(pl.multiple_of(rr * GRID_W + NA_HALF * half, NA_HALF), NA_HALF)
                    bias = jnp.concatenate([b_ref[0, c, half], b_ref[1, c, half]], axis=0)
                    parts.append((rows, _na_stack(q_ref[rows, :], first), bias, _na_keys(k_ref, rs, half),
                                  _na_keys(v_ref, rs, half)))
            s = [_dot(qs, kw, _NT) * ATT_SCALE + bias for _, qs, bias, kw, _ in parts]
            e = [jnp.exp(x - jnp.max(x, axis=-1, keepdims=True)) for x in s]
            pn = [(x / jnp.sum(x, axis=-1, keepdims=True)).astype(BF16) for x in e]
            o = [_dot(p, part[4]) for p, part in zip(pn, parts)]
            for x, (rows, *_) in zip(o, parts):
                o_ref[rows, :] = jnp.where(first, x[:NA_HALF], x[NA_HALF:]).astype(BF16)
            return carry

        lax.fori_loop(0, nr // NA_ROWS_PER_STEP, body, 0)

    qspec = pl.BlockSpec((tm, LANES), lambda j, i: (i, j))
    kvspec = pl.BlockSpec((L, LANES), lambda j, i: (0, j))
    (y,), got = _call(
        kern, (q, k, v, bias), ride, lambda: _grid_edges(2, nt), name="na_fwd", grid=(2, nt),
        in_specs=[qspec, kvspec, kvspec, _na_bias_spec()],
        out_specs=[qspec], out_shape=[jax.ShapeDtypeStruct((L, GROUP_W), BF16)], scratch_shapes=[])
    return y, got


def _na_bwd(dy, dy_block, q, k, v, bias, ride=None):
    L = q.shape[0]
    n_rows = L // GRID_W
    tm = _tile(L, 512, GRID_W)
    nt, nr = L // tm, tm // GRID_W

    def kern(dy_ref, q_ref, k_ref, v_ref, b_ref, dq_ref, dk_ref, dv_ref, db_ref):
        i = pl.program_id(1)

        @pl.when(i == 0)
        def _():
            dk_ref[...] = jnp.zeros_like(dk_ref)
            dv_ref[...] = jnp.zeros_like(dv_ref)
            db_ref[...] = jnp.zeros_like(db_ref)

        first = lax.broadcasted_iota(jnp.int32, (NA_HALF, LANES), 1) < HEAD_DIM

        def body(rr, carry):
            rs, c = _na_window(i * nr + rr, n_rows)
            parts = []
            for half in range(2):
                rows = pl.ds(pl.multiple_of(rr * GRID_W + NA_HALF * half, NA_HALF), NA_HALF)
                bias = jnp.concatenate([b_ref[0, c, half], b_ref[1, c, half]], axis=0)
                parts.append((rows, half, _na_stack(q_ref[rows, :], first), _na_stack(dy_ref[rows, :].astype(BF16), first),
                              bias, _na_keys(k_ref, rs, half), _na_keys(v_ref, rs, half)))
            s = [_dot(qs, kw, _NT) * ATT_SCALE + bias for _, _, qs, _, bias, kw, _ in parts]
            dp = [_dot(dos, vw, _NT) for _, _, _, dos, _, _, vw in parts]
            e = [jnp.exp(x - jnp.max(x, axis=-1, keepdims=True)) for x in s]
            pn = [x / jnp.sum(x, axis=-1, keepdims=True) for x in e]
            ds = [p * (d - jnp.sum(p * d, axis=-1, keepdims=True)) for p, d in zip(pn, dp)]
            dsb = [x.astype(BF16) for x in ds]
            pnb = [x.astype(BF16) for x in pn]
            dq = [_dot(x, part[5]) for x, part in zip(dsb, parts)]
            dk = [_dot(x, part[2], _TN) for x, part in zip(dsb, parts)]
            dv = [_dot(x, part[3], _TN) for x, part in zip(pnb, parts)]
            for n, (rows, half, *_) in enumerate(parts):
                db_ref[0, c, half] += ds[n][:NA_HALF]
                db_ref[1, c, half] += ds[n][NA_HALF:]
                dq_ref[rows, :] = (jnp.where(first, dq[n][:NA_HALF], dq[n][NA_HALF:]) * ATT_SCALE).astype(BF16)
                for t in range(NA_ROWS):
                    kr = _na_key_rows(rs, half, t)
                    dk_ref[kr, :] += dk[n][t * NA_KCOLS:(t + 1) * NA_KCOLS] * ATT_SCALE
                    dv_ref[kr, :] += dv[n][t * NA_KCOLS:(t + 1) * NA_KCOLS]
            return carry

        lax.fori_loop(0, nr, body, 0)

    qspec = pl.BlockSpec((tm, LANES), lambda j, i: (i, j))
    kvspec = pl.BlockSpec((L, LANES), lambda j, i: (0, j))
    return _call(
        kern, (dy, q, k, v, bias), ride, lambda: _grid_edges(2, nt), name="na_bwd", grid=(2, nt),
        in_specs=[pl.BlockSpec((tm, LANES), lambda j, i: (i, dy_block + j)), qspec, kvspec, kvspec, _na_bias_spec()],
        out_specs=[qspec, kvspec, kvspec, _na_bias_spec()],
        out_shape=[jax.ShapeDtypeStruct((L, GROUP_W), BF16), jax.ShapeDtypeStruct((L, GROUP_W), F32),
                   jax.ShapeDtypeStruct((L, GROUP_W), F32),
                   jax.ShapeDtypeStruct((4, NA_ROWS, 2, NA_HALF, NA_WIN), F32)], scratch_shapes=[])


def _dil_specs(n, tq):
    R = DIL_RADIUS
    step, nb = tq // R, n // R
    main = pl.BlockSpec((tq, LANES), lambda j, i: (i, j))
    prev = pl.BlockSpec((R, LANES), lambda j, i: (jnp.maximum(i * step - 1, 0), j))
    nxt = pl.BlockSpec((R, LANES), lambda j, i: (jnp.minimum((i + 1) * step, nb - 1), j))
    return main, prev, nxt


def _dil_valid(i, tq, n):
    R = DIL_RADIUS
    row = lax.broadcasted_iota(jnp.int32, (tq, tq + 2 * R), 0)
    col = lax.broadcasted_iota(jnp.int32, (tq, tq + 2 * R), 1)
    kpos = i * tq - R + col
    return (jnp.abs(col - R - row) <= R) & (kpos >= 0) & (kpos < n)


def _dil_fwd(q, k, v, dil, ride=None):
    n = q.shape[0]
    tq = _tile(n, 256, DIL_RADIUS)

    def kern(q_ref, kp_ref, k_ref, kn_ref, vp_ref, v_ref, vn_ref, o_ref, l_ref):
        i = pl.program_id(1)
        valid = _dil_valid(i, tq, n)
        qv = q_ref[...]
        ka = jnp.concatenate([kp_ref[...], k_ref[...], kn_ref[...]], axis=0)
        va = jnp.concatenate([vp_ref[...], v_ref[...], vn_ref[...]], axis=0)
        lane = lax.broadcasted_iota(jnp.int32, (tq, LANES), 1)
        first = lane < HEAD_DIM
        s = [_dot(jnp.where(first == (hh == 0), qv, jnp.zeros_like(qv)), ka, _NT) for hh in range(2)]
        s = [jnp.where(valid, x * ATT_SCALE, NEG) for x in s]
        m = [jnp.max(x, axis=-1, keepdims=True) for x in s]
        e = [jnp.exp(x - mx) for x, mx in zip(s, m)]
        den = [jnp.sum(x, axis=-1, keepdims=True) for x in e]
        o = [_dot((x / d).astype(BF16), va) for x, d in zip(e, den)]
        o_ref[...] = jnp.where(first, o[0], o[1])
        l_ref[...] = jnp.where(first, m[0] + jnp.log(den[0]), m[1] + jnp.log(den[1]))

    main, prev, nxt = _dil_specs(n, tq)
    (o, lse), got = _call(
        kern, (q, k, k, k, v, v, v), ride,
        lambda: _grid_edges(2 * dil, n // tq), name=f"dil_fwd_{dil}", grid=(2 * dil, n // tq),
        in_specs=[main, prev, main, nxt, prev, main, nxt], out_specs=[main, main],
        out_shape=[jax.ShapeDtypeStruct((n, dil * GROUP_W), F32)] * 2, scratch_shapes=[])
    return (o, lse), got


def _dil_bwd(q, k, v, do, lse, dterm, dil, ride=None):
    n = q.shape[0]
    R = DIL_RADIUS
    tq = _tile(n, 256, R)
    nq = n // tq

    def kern(q_ref, kp_ref, k_ref, kn_ref, vp_ref, v_ref, vn_ref, do_ref, l_ref, dt_ref, dq_ref, dk_ref, dv_ref):
        i = pl.program_id(1)

        @pl.when(i == 0)
        def _():
            dk_ref[...] = jnp.zeros_like(dk_ref)
            dv_ref[...] = jnp.zeros_like(dv_ref)

        valid = _dil_valid(i, tq, n)
        qv, dov = q_ref[...], do_ref[...]
        ka = jnp.concatenate([kp_ref[...], k_ref[...], kn_ref[...]], axis=0)
        va = jnp.concatenate([vp_ref[...], v_ref[...], vn_ref[...]], axis=0)
        lv, dtv = l_ref[...], dt_ref[...]
        lane = lax.broadcasted_iota(jnp.int32, (tq, LANES), 1)
        first = lane < HEAD_DIM
        H = range(2)
        qm = [jnp.where(first == (hh == 0), qv, jnp.zeros_like(qv)) for hh in H]
        dom = [jnp.where(first == (hh == 0), dov, jnp.zeros_like(dov)) for hh in H]
        s = [_dot(qm[hh], ka, _NT) for hh in H]
        dp = [_dot(dom[hh], va, _NT) for hh in H]
        pn = [jnp.where(valid, jnp.exp(s[hh] * ATT_SCALE - lv[:, hh * HEAD_DIM:hh * HEAD_DIM + 1]), 0.0) for hh in H]
        dsb = [(pn[hh] * (dp[hh] - dtv[:, hh * HEAD_DIM:hh * HEAD_DIM + 1])).astype(BF16) for hh in H]
        pnb = [x.astype(BF16) for x in pn]
        dq = [_dot(dsb[hh], ka) for hh in H]
        dk2 = [_dot(dsb[hh], qm[hh], _TN) for hh in H]
        dv2 = [_dot(pnb[hh], dom[hh], _TN) for hh in H]
        dka = (dk2[0] + dk2[1]) * ATT_SCALE
        dva = dv2[0] + dv2[1]
        dq_ref[...] = jnp.where(first, dq[0], dq[1]) * ATT_SCALE
        r0 = pl.multiple_of(i * tq, R)
        dk_ref[pl.ds(r0, tq), :] += dka[R:R + tq]
        dv_ref[pl.ds(r0, tq), :] += dva[R:R + tq]

        @pl.when(i > 0)
        def _():
            dk_ref[pl.ds(r0 - R, R), :] += dka[:R]
            dv_ref[pl.ds(r0 - R, R), :] += dva[:R]

        @pl.when(i < nq - 1)
        def _():
            dk_ref[pl.ds(r0 + tq, R), :] += dka[R + tq:]
            dv_ref[pl.ds(r0 + tq, R), :] += dva[R + tq:]

    main, prev, nxt = _dil_specs(n, tq)
    whole = pl.BlockSpec((n, LANES), lambda j, i: (0, j))
    shp = jax.ShapeDtypeStruct((n, dil * GROUP_W), F32)
    (dq, dk, dv), got = _call(
        kern, (q, k, k, k, v, v, v, do, lse, dterm), ride,
        lambda: _grid_edges(2 * dil, nq), name=f"dil_bwd_{dil}", grid=(2 * dil, nq),
        in_specs=[main, prev, main, nxt, prev, main, nxt, main, main, main], out_specs=[main, whole, whole],
        out_shape=[shp] * 3, scratch_shapes=[])
    return (dq, dk, dv), got


def _dil_weights(lses):
    m = jnp.maximum(jnp.maximum(lses[0], lses[1]), lses[2])
    e = [jnp.exp(l - m) for l in lses]
    tot = e[0] + e[1] + e[2]
    return [x / tot for x in e]


def _dilated_fwd(qkv, rides):
    dils = [dil for _, dil in DIL_PAIRS]
    res, got = [], {}
    for dil in dils:
        r, got[f"dil{dil}"] = _dil_fwd(*qkv[dil], dil, rides.get(f"dil{dil}"))
        res.append(r)

    def body(i, nt, o0, o1, o2, l0, l1, l2):
        w = _dil_weights((l0, l1, l2))
        return w[0] * o0 + w[1] * o1 + w[2] * o2

    ins = [Row(r[0], dil=d) for r, d in zip(res, dils)] + [Row(r[1], dil=d) for r, d in zip(res, dils)]
    return _rows("dil_combine", body, 512, ins, [("row", GROUP_W, BF16)])[0], res, got


def _dilated_bwd(dy, dy_cb, qkv, saved, cosf, sinf, rides):
    dils = [dil for _, dil in DIL_PAIRS]
    def split(i, nt, dy, o0, o1, o2, l0, l1, l2, bd):
        w = _dil_weights((l0, l1, l2))
        y = w[0] * o0 + w[1] * o1 + w[2] * o2
        dyy = _head_sum(dy * y, bd)
        return tuple(wg * dy for wg in w) + tuple(wg * dyy for wg in w)

    ins = ([Row(dy, GROUP_W, dy_cb)] + [Row(r[0], dil=d) for r, d in zip(saved, dils)]
           + [Row(r[1], dil=d) for r, d in zip(saved, dils)])
    outs = _rows("dil_split_bwd", split, 512, ins + [Full(_block_ones(GROUP_W, HEAD_DIM))],
                 [("row", GROUP_W, BF16, d) for d in dils] + [("row", GROUP_W, F32, d) for d in dils])
    g, got = [], {}
    for b, dil in enumerate(dils):
        r, got[f"dil{dil}"] = _dil_bwd(*qkv[dil], outs[b], saved[b][1], outs[3 + b], dil, rides.get(f"dil{dil}"))
        g.append(r)

    def finish(i, nt, q0, q1, q2, k0, k1, k2, v0, v1, v2, c, s):
        dq, dk = q0 + q1 + q2, k0 + k1 + k2
        return dq * c + _swap_halves(dq * s), dk * c + _swap_halves(dk * s), v0 + v1 + v2

    ins = [Row(g[b][t], dil=dils[b]) for t in range(3) for b in range(3)] + [Row(cosf), Row(sinf)]
    return _rows("dil_finish_bwd", finish, 512, ins, [("row", GROUP_W, BF16)] * 3), got


def _layer_fwd(x, W, l, cosf, sinf, rides):
    h1 = _rms_fwd(x, W["mix_norm_pre"][l][None], "mix_norm")
    p = _mm(h1, W["w_in"][l], "nn", F32, "proj_in")
    ya, sa, got_gla = _gla_fwd(p, W, l, rides.get("gla"))
    (qb, kb, vb), qkv_d = _attn_prep(p, cosf, sinf)
    bias = _na_bias(W["na_rpb"][l])
    yb, got_na = _na_fwd(qb, kb, vb, bias, rides.get("na"))
    yc, sc = _lru_fwd(p, W, l)
    yd, sd, got = _dilated_fwd(qkv_d, rides)
    got.update(gla=got_gla, na=got_na)
    ycat = jnp.concatenate([ya, yb, yc, yd], axis=1)
    ymix = _mm(ycat, W["w_out"][l], "nn", F32, "proj_out", tm=1024)
    xm = _rms_resid_fwd(x, ymix, W["mix_norm_post"][l][None], "mix_resid")
    h2 = _rms_fwd(xm, W["ffn_norm_pre"][l][None], "ffn_norm")
    gu, act = _ffn_in_swiglu(h2, W["ffn_w_in"][l])
    f = _mm(act, W["ffn_w_out"][l], "nn", F32, "ffn_out")
    xo = _rms_resid_fwd(xm, f, W["ffn_norm_post"][l][None], "ffn_resid")
    saved = dict(x=x, h1=h1, p=p, sa=sa, att=(qb, kb, vb, qkv_d), bias=bias, sc=sc, sd=sd, ycat=ycat, ymix=ymix,
                 xm=xm, h2=h2, gu=gu, act=act, f=f)
    return xo, saved, got


def _layer_bwd(dxo, W, l, S, cosf, sinf, rides):
    g = {}
    df, g["ffn_norm_post"] = _rms_bwd(dxo, S["f"], W["ffn_norm_post"][l][None], "ffn_resid_bwd", out_dtype=BF16)
    g["ffn_w_out"] = _mm(S["act"], df, "tn", BF16, "ffn_out_dw", tm=256, tk=4096)
    dgu = _ffn_out_dx_swiglu(df, W["ffn_w_out"][l], S["gu"])
    dh2 = _mm(dgu, W["ffn_w_in"][l], "nt", F32, "ffn_in_dx")
    g["ffn_w_in"] = _mm(S["h2"], dgu, "tn", BF16, "ffn_in_dw", tm=1024, tn=512, tk=4096)
    dxm, g["ffn_norm_pre"] = _rms_bwd(dh2, S["xm"], W["ffn_norm_pre"][l][None], "ffn_norm_bwd", resid=dxo)
    dymix, g["mix_norm_post"] = _rms_bwd(dxm, S["ymix"], W["mix_norm_post"][l][None], "mix_resid_bwd", out_dtype=BF16)
    dycat = _mm(dymix, W["w_out"][l], "nt", F32, "proj_out_dx", tm=1024)
    g["w_out"] = _mm(S["ycat"], dymix, "tn", BF16, "proj_out_dw", tm=1024, tn=512, tk=4096)
    p = S["p"]
    qb, kb, vb, qkv_d = S["att"]
    (dqa, dka, dva, dga, dz), ga, got_gla = _gla_bwd(dycat, 0, p, W, l, S["sa"], rides.get("gla"))
    (dqb, dkb, dvb, dbias), got_na = _na_bwd(dycat, 2, qb, kb, vb, S["bias"], rides.get("na"))
    g["na_rpb"] = _na_bias_bwd(dbias)
    dxc, dgc, gc = _lru_bwd(dycat, 2, p, W, l, S["sc"])
    (dqd, dkd, dvd), got = _dilated_bwd(dycat, 3, qkv_d, S["sd"], cosf, sinf, rides)
    got.update(gla=got_gla, na=got_na)
    g.update(ga)
    g.update(gc)
    dp = jnp.concatenate([dqa, dka, dva, dga, dqb, dkb.astype(BF16), dvb.astype(BF16), dxc, dgc, dqd, dkd, dvd, dz], axis=1)
    dh1 = _mm(dp, W["w_in"][l], "nt", F32, "proj_in_dx")
    g["w_in"] = _mm(S["h1"], dp, "tn", BF16, "proj_in_dw", tm=1024, tn=640, tk=4096)
    dx, g["mix_norm_pre"] = _rms_bwd(dh1, S["x"], W["mix_norm_pre"][l][None], "mix_norm_bwd", resid=dxm)
    for n in ("ffn_norm_post", "ffn_norm_pre", "mix_norm_post", "mix_norm_pre"):
        g[n] = g[n][0]
    return dx, g, got


MESH_AXES = ("x", "y", "c")


class Xfer:
    def __init__(self, arr, kind):
        self.arr, self.kind = arr, kind
        shp = arr.shape
        if kind == "all":
            self.out = (N_DEV,) + shp
        elif kind == "slot":
            self.out = shp
        elif kind == "rows":
            self.r = shp[1] // N_DEV
            self.out = (N_DEV, shp[0], self.r, shp[2])
        else:
            self.r = shp[1]
            self.out = (shp[0], N_DEV * shp[1], shp[2])

    def src(self, ref, peer):
        if self.kind == "slot":
            return ref.at[peer]
        if self.kind == "rows":
            return ref.at[:, pl.ds(peer * self.r, self.r), :]
        return ref

    def dst(self, ref, me):
        if self.kind == "place":
            return ref.at[:, pl.ds(me * self.r, self.r), :]
        return ref.at[me]


class Exchange:
    def __init__(self, items):
        n = len(items)
        self.items = items
        self.arrays = [it.arr for it in items]
        self.specs = [pl.BlockSpec(memory_space=pl.ANY)] * n
        self.out_shape = [jax.ShapeDtypeStruct(it.out, it.arr.dtype) for it in items]
        self.scratch = [pltpu.SemaphoreType.DMA((n * (N_DEV - 1),)), pltpu.SemaphoreType.DMA((n * (N_DEV - 1),)),
                        pltpu.SemaphoreType.DMA((n,))]

    def copies(self, ins, outs, sems):
        send_sems, recv_sems, local_sems = sems
        x, y, c = (lax.axis_index(a) for a in MESH_AXES)
        me = 4 * x + 2 * y + c
        out = []
        for t, it in enumerate(self.items):
            out.append(pltpu.make_async_copy(it.src(ins[t], me), it.dst(outs[t], me), local_sems.at[t]))
            for k in range(1, N_DEV):
                px, py, pc = x ^ ((k >> 2) & 1), y ^ ((k >> 1) & 1), c ^ (k & 1)
                s = t * (N_DEV - 1) + k - 1
                out.append(pltpu.make_async_remote_copy(
                    src_ref=it.src(ins[t], 4 * px + 2 * py + pc), dst_ref=it.dst(outs[t], me),
                    send_sem=send_sems.at[s], recv_sem=recv_sems.at[s], device_id=(px, py, pc),
                    device_id_type=pl.DeviceIdType.MESH))
        return out

    def start(self, ins, outs, sems):
        for cp in self.copies(ins, outs, sems):
            cp.start()

    def wait(self, ins, outs, sems):
        for cp in self.copies(ins, outs, sems):
            cp.wait()


def _exchange(items, name):
    ex = Exchange(items)
    n = len(items)

    def body(*refs):
        ex.start(refs[:n], refs[n:2 * n], refs[2 * n:])
        ex.wait(refs[:n], refs[n:2 * n], refs[2 * n:])

    return pl.pallas_call(body, name=name, out_shape=ex.out_shape, in_specs=ex.specs, out_specs=ex.specs,
                          scratch_shapes=ex.scratch)(*ex.arrays)


def _call(kern, arrays, ride, edges, *, name, grid, in_specs, out_specs, out_shape, scratch_shapes):
    params = _ARB(len(grid))
    if ride is None:
        return pl.pallas_call(kern, name=name, grid=grid, in_specs=in_specs, out_specs=out_specs, out_shape=out_shape,
                              scratch_shapes=scratch_shapes, compiler_params=params)(*arrays), None
    ni, no, ns, nx = len(in_specs), len(out_specs), len(scratch_shapes), len(ride.items)

    def wrapped(*refs):
        ins, xin = refs[:ni], refs[ni:ni + nx]
        outs, xout = refs[ni + nx:ni + nx + no], refs[ni + nx + no:ni + 2 * nx + no]
        scr, sems = refs[ni + 2 * nx + no:ni + 2 * nx + no + ns], refs[ni + 2 * nx + no + ns:]
        first, last = edges()

        @pl.when(first)
        def _():
            ride.start(xin, xout, sems)

        kern(*ins, *outs, *scr)

        @pl.when(last)
        def _():
            ride.wait(xin, xout, sems)

    res = pl.pallas_call(
        wrapped, name=name, grid=grid, in_specs=list(in_specs) + ride.specs, out_specs=list(out_specs) + ride.specs,
        out_shape=list(out_shape) + ride.out_shape, scratch_shapes=list(scratch_shapes) + ride.scratch,
        compiler_params=params)(*arrays, *ride.arrays)
    return res[:no], res[no:]


def _column_segments(width, permuted):
    z0, z1, zn = 4 * GROUP_W, 4 * GROUP_W + 2 * GLA_RANK, 12 * GROUP_W
    segs = []
    for d in range(N_DEV):
        lo, hi = d * width, (d + 1) * width
        if not permuted:
            segs.append([(0, width, lo)])
            continue
        runs = []
        for a, b, shift in ((0, z0, 0), (z0, z1, zn - z0), (z1, 10 ** 9, -(z1 - z0))):
            s, e = max(lo, a), min(hi, b)
            if s < e:
                runs.append((s - lo, e - lo, s + shift))
        segs.append(runs)
    return segs


def _cols_from_pieces(pieces, segs, cols, name):
    _, R, w = pieces.shape
    tm = _tile(R, 256, 16)
    used = max(f + (b - a) for runs in segs for a, b, f in runs)

    def kern(p_ref, o_ref):
        for d, runs in enumerate(segs):
            for a, b, f in runs:
                o_ref[:, f:f + (b - a)] = p_ref[d, :, a:b]
        if used < cols:
            o_ref[:, used:cols] = jnp.zeros((tm, cols - used), o_ref.dtype)

    return pl.pallas_call(
        kern, name=name, grid=(R // tm,), in_specs=[pl.BlockSpec((N_DEV, tm, w), lambda i: (0, i, 0))],
        out_specs=pl.BlockSpec((tm, cols), lambda i: (i, 0)), out_shape=jax.ShapeDtypeStruct((R, cols), pieces.dtype),
        compiler_params=_ARB(1),
    )(pieces)


def _pieces_from_cols(full, segs, w, name):
    R, cols = full.shape
    tm = _tile(R, 256, 16)

    def kern(f_ref, o_ref):
        for d, runs in enumerate(segs):
            for a, b, f in runs:
                o_ref[d, :, a:b] = f_ref[:, f:f + (b - a)]

    return pl.pallas_call(
        kern, name=name, grid=(R // tm,), in_specs=[pl.BlockSpec((tm, cols), lambda i: (i, 0))],
        out_specs=pl.BlockSpec((N_DEV, tm, w), lambda i: (0, i, 0)),
        out_shape=jax.ShapeDtypeStruct((N_DEV, R, w), full.dtype), compiler_params=_ARB(1),
    )(full)


def _sum_slots(recv, name):
    n, R, C = recv.shape
    tm = _tile(R, 256, 16)

    def kern(*refs):
        acc = refs[0][...].astype(F32)
        for r in refs[1:n]:
            acc = acc + r[...].astype(F32)
        refs[n][...] = acc

    return pl.pallas_call(
        kern, name=name, grid=(R // tm,),
        in_specs=[pl.BlockSpec((None, tm, C), lambda i, _s=s: (_s, i, 0)) for s in range(n)],
        out_specs=pl.BlockSpec((tm, C), lambda i: (i, 0)), out_shape=jax.ShapeDtypeStruct((R, C), F32),
        compiler_params=_ARB(1),
    )(*([recv] * n))


BIG = (("w_in", 2), ("w_out", 1), ("ffn_w_in", 2), ("ffn_w_out", 1))
SMALL_SHARDED = ("gla_w_gate", "gla_b_gate", "lru_conv_w", "lru_b_a", "lru_b_x", "lru_lambda")
REPLICATED = ("mix_norm_pre", "mix_norm_post", "gla_norm", "na_rpb", "lru_conv_b", "lru_w_a", "lru_w_x",
              "ffn_norm_pre", "ffn_norm_post")
WEIGHTS = ("mix_norm_pre", "mix_norm_post", "w_in", "gla_w_gate", "gla_b_gate", "gla_norm", "na_rpb", "lru_conv_w",
           "lru_conv_b", "lru_w_a", "lru_b_a", "lru_w_x", "lru_b_x", "lru_lambda", "w_out", "ffn_norm_pre",
           "ffn_norm_post", "ffn_w_in", "ffn_w_out")
FLAT_C = 1024


def _to_rows(vec, row_unit):
    n = vec.shape[-1]
    rows = -(-n // (FLAT_C * row_unit)) * row_unit
    pad = [(0, 0)] * (vec.ndim - 1) + [(0, rows * FLAT_C - n)]
    return jnp.pad(vec, pad).reshape(vec.shape[:-1] + (rows, FLAT_C))


def _unshard(parts, axis):
    t = jnp.moveaxis(parts, 0, axis)
    shp = list(t.shape)
    return t.reshape(shp[:axis] + [shp[axis] * shp[axis + 1]] + shp[axis + 2:])


def _shards(full, axis):
    shp = list(full.shape)
    t = full.reshape(shp[:axis] + [N_DEV, shp[axis] // N_DEV] + shp[axis + 1:])
    return jnp.moveaxis(t, axis, 0)


def _weight_rides(W, l):
    bf = lambda n: W[n][l].astype(BF16)
    ffn = bf("ffn_w_in")
    half = ffn.shape[0] // 2
    return {"gla": [Xfer(bf("w_in"), "all"), Xfer(bf("w_out")[None], "place")],
            "na": [Xfer(bf("ffn_w_out")[None], "place")],
            "dil1": [Xfer(ffn[:half], "all")], "dil4": [Xfer(ffn[half:], "all")]}


def _unpack_weights(full, W, got):
    w_in_w, ffn_w = W["w_in"].shape[-1], W["ffn_w_in"].shape[-1]
    full["w_in"].append(_cols_from_pieces(got["gla"][0], _column_segments(w_in_w, True), P_COLS, "unpack_w_in"))
    full["w_out"].append(got["gla"][1][0])
    full["ffn_w_out"].append(got["na"][0][0])
    full["ffn_w_in"].append(jnp.concatenate(
        [_cols_from_pieces(got[c][0], _column_segments(ffn_w, False), N_DEV * ffn_w, "unpack_ffn_w_in")
         for c in ("dil1", "dil4")], axis=0))


def _grad_rides(g, W):
    w_in_w, ffn_w = W["w_in"].shape[-1], W["ffn_w_in"].shape[-1]
    p_in = _pieces_from_cols(g["w_in"], _column_segments(w_in_w, True), w_in_w, "pack_w_in")
    p_ffn = _pieces_from_cols(g["ffn_w_in"], _column_segments(ffn_w, False), ffn_w, "pack_ffn_w_in")
    half = p_ffn.shape[1] // 2
    return {"gla": [Xfer(p_in, "slot"), Xfer(g["w_out"][None], "rows")],
            "na": [Xfer(g["ffn_w_out"][None], "rows"), Xfer(p_ffn[:, :half], "slot")],
            "dil1": [Xfer(p_ffn[:, half:], "slot")]}


def _sum_big(got):
    s = lambda r, n: _sum_slots(r.reshape(N_DEV, -1, r.shape[-1]), "sum_" + n)
    return {"w_in": s(got["gla"][0], "w_in"), "w_out": s(got["gla"][1], "w_out"), "ffn_w_out": s(got["na"][0], "ffn_w_out"),
            "ffn_w_in": jnp.concatenate([s(got["na"][1], "ffn_w_in"), s(got["dil1"][0], "ffn_w_in")], axis=0)}


def _exchange_named(rides, extra, name):
    names = list(rides)
    res = _exchange([it for n in names for it in rides[n]] + extra, name)
    got, at = {}, 0
    for n in names:
        got[n] = res[at:at + len(rides[n])]
        at += len(rides[n])
    return got, res[at:]


def _train(x, target, W):
    L = x.shape[0]
    depth = W["w_in"].shape[0]
    cosf, sinf = _rope_tables(L)
    small = jnp.concatenate([W[n].reshape(-1) for n in SMALL_SHARDED])
    small16 = _to_rows(lax.bitcast_convert_type(small, jnp.uint16).reshape(-1), 16)
    got, (sm,) = _exchange_named(_weight_rides(W, 0), [Xfer(small16, "all")], "gather_first")
    full = dict(W, w_in=[], w_out=[], ffn_w_in=[], ffn_w_out=[])
    _unpack_weights(full, W, got)
    sm = lax.bitcast_convert_type(sm.reshape(N_DEV, -1)[:, :2 * small.size].reshape(N_DEV, small.size, 2), F32)
    off = 0
    for n in SMALL_SHARDED:
        full[n] = _unshard(sm[:, off:off + W[n].size].reshape((N_DEV,) + W[n].shape), W[n].ndim - 1)
        off += W[n].size

    saved = []
    for l in range(depth):
        rides = {c: Exchange(it) for c, it in _weight_rides(W, l + 1).items()} if l + 1 < depth else {}
        x, S, got = _layer_fwd(x, full, l, cosf, sinf, rides)
        saved.append(S)
        if l + 1 < depth:
            _unpack_weights(full, W, got)
    loss, dx = _loss_fwd_bwd(x, target)

    grads, big, rides = [None] * depth, [None] * depth, {}
    for l in reversed(range(depth)):
        dx, grads[l], got = _layer_bwd(dx, full, l, saved[l], cosf, sinf, rides)
        if l + 1 < depth:
            big[l + 1] = _sum_big(got)
        rides = {c: Exchange(it) for c, it in _grad_rides(grads[l], W).items()} if l > 0 else {}
    G = {n: jnp.stack([g[n] for g in grads]) for n in SMALL_SHARDED + REPLICATED}
    small_g = jnp.concatenate([_shards(G[n], G[n].ndim - 1).reshape(N_DEV, -1) for n in SMALL_SHARDED], axis=1)
    repl_g = jnp.concatenate([G[n].reshape(-1) for n in REPLICATED])
    got, rest = _exchange_named(_grad_rides(grads[0], W), [Xfer(_to_rows(small_g, 8), "slot"),
                                                            Xfer(_to_rows(repl_g, 8), "all")], "exchange_last")
    big[0] = _sum_big(got)
    out = {n: jnp.stack([b[n] for b in big]).reshape(W[n].shape) for n, _ in BIG}
    for names, r, tag in ((SMALL_SHARDED, rest[0], "sum_small"), (REPLICATED, rest[1], "sum_replicated")):
        flat, off = _sum_slots(r, tag).reshape(-1), 0
        for n in names:
            out[n] = flat[off:off + W[n].size].reshape(W[n].shape)
            off += W[n].size
    return loss, dx, out


def _update(W, G, M, V):
    delta, new_m, new_v = {}, {}, {}
    for n, _ in BIG:
        two_d = lambda a: a.reshape(-1, a.shape[-1])
        d, m, v = _adamw(two_d(W[n]), two_d(G[n]), two_d(M[n]), two_d(V[n]), "adamw_" + n)
        delta[n], new_m[n], new_v[n] = (t.reshape(W[n].shape) for t in (d, m, v))
    rest = SMALL_SHARDED + REPLICATED
    pack = lambda D: _to_rows(jnp.concatenate([D[n].reshape(-1) for n in rest]), 16)
    d, m, v = _adamw(pack(W), pack(G), pack(M), pack(V), "adamw_small")
    off = 0
    for n in rest:
        sl = lambda t: t.reshape(-1)[off:off + W[n].size].reshape(W[n].shape)
        delta[n], new_m[n], new_v[n] = sl(d), sl(m), sl(v)
        off += W[n].size
    return delta, new_m, new_v


def kernel(x, mix_norm_pre, mix_norm_post, w_in, gla_w_gate, gla_b_gate, gla_norm, na_rpb, lru_conv_w, lru_conv_b, lru_w_a, lru_b_a, lru_w_x, lru_b_x, lru_lambda, w_out, ffn_norm_pre, ffn_norm_post, ffn_w_in, ffn_w_out, loss_target, m_mix_norm_pre, m_mix_norm_post, m_w_in, m_gla_w_gate, m_gla_b_gate, m_gla_norm, m_na_rpb, m_lru_conv_w, m_lru_conv_b, m_lru_w_a, m_lru_b_a, m_lru_w_x, m_lru_b_x, m_lru_lambda, m_w_out, m_ffn_norm_pre, m_ffn_norm_post, m_ffn_w_in, m_ffn_w_out, v_mix_norm_pre, v_mix_norm_post, v_w_in, v_gla_w_gate, v_gla_b_gate, v_gla_norm, v_na_rpb, v_lru_conv_w, v_lru_conv_b, v_lru_w_a, v_lru_b_a, v_lru_w_x, v_lru_b_x, v_lru_lambda, v_w_out, v_ffn_norm_pre, v_ffn_norm_post, v_ffn_w_in, v_ffn_w_out):
    W = dict(zip(WEIGHTS, (mix_norm_pre, mix_norm_post, w_in, gla_w_gate, gla_b_gate, gla_norm, na_rpb, lru_conv_w, lru_conv_b, lru_w_a, lru_b_a, lru_w_x, lru_b_x, lru_lambda, w_out, ffn_norm_pre, ffn_norm_post, ffn_w_in, ffn_w_out)))
    M = dict(zip(WEIGHTS, (m_mix_norm_pre, m_mix_norm_post, m_w_in, m_gla_w_gate, m_gla_b_gate, m_gla_norm, m_na_rpb, m_lru_conv_w, m_lru_conv_b, m_lru_w_a, m_lru_b_a, m_lru_w_x, m_lru_b_x, m_lru_lambda, m_w_out, m_ffn_norm_pre, m_ffn_norm_post, m_ffn_w_in, m_ffn_w_out)))
    V = dict(zip(WEIGHTS, (v_mix_norm_pre, v_mix_norm_post, v_w_in, v_gla_w_gate, v_gla_b_gate, v_gla_norm, v_na_rpb, v_lru_conv_w, v_lru_conv_b, v_lru_w_a, v_lru_b_a, v_lru_w_x, v_lru_b_x, v_lru_lambda, v_w_out, v_ffn_norm_pre, v_ffn_norm_post, v_ffn_w_in, v_ffn_w_out)))
    loss, dx, G = _train(x[0], loss_target[0], W)
    loss = lax.psum(loss, MESH_AXES)
    delta, new_m, new_v = _update(W, G, M, V)
    return (loss, dx[None], *[G[n] for n in WEIGHTS], *[delta[n] for n in WEIGHTS], *[new_m[n] for n in WEIGHTS],
            *[new_v[n] for n in WEIGHTS])
```

```python
import functools
import math

import numpy as np
import jax
import jax.numpy as jnp
from jax import lax
from jax.experimental import pallas as pl
from jax.experimental.pallas import tpu as pltpu

F32 = jnp.float32
BF16 = jnp.bfloat16

N_DEV = 8
HEAD_DIM = 64
GROUP_W = 256
GLA_RANK = 16
GLA_TAU = 16.0
GLA_CHUNK = 64
GRID_W = 64
NA_ROWS = 8
NA_COLS = 16
LRU_C = 8.0
DIL_PAIRS = ((128, 1), (512, 4), (2048, 16))
DIL_RADIUS = 64
ROPE_THETA = 10000.0
EPS = 1e-6
ATT_SCALE = HEAD_DIM ** -0.5
NEG = -1e30
LANES = 128
P_COLS = 12 * GROUP_W + LANES
Z_BLOCK = 12 * GROUP_W // LANES

ADAM_LR = 0.001
ADAM_B1 = 0.9
ADAM_B2 = 0.999
ADAM_EPS = 1e-08
ADAM_WD = 0.01
ADAM_STEP = 10

VMEM_LIMIT = 56 * 1024 * 1024
_ARB = lambda n: pltpu.CompilerParams(dimension_semantics=("arbitrary",) * n, vmem_limit_bytes=VMEM_LIMIT)


def _tile(dim, pref, unit):
    t = min(pref, dim) // unit * unit
    while t >= unit:
        if dim % t == 0:
            return t
        t -= unit
    return dim


def _mm(a, b, mode, out_dtype, name, tm=512, tn=None, tk=None):
    if mode == "nn":
        (M, K), (_, N) = a.shape, b.shape
    elif mode == "nt":
        (M, K), (N, _) = a.shape, b.shape
    else:
        (K, M), (_, N) = a.shape, b.shape
    tm = _tile(M, tm, LANES if mode == "tn" else 8)
    tn = _tile(N, tn or N, LANES)
    tk = _tile(K, tk or K, LANES)
    nk = K // tk
    dims = {"nn": (((1,), (0,)), ((), ())), "nt": (((1,), (1,)), ((), ())), "tn": (((0,), (0,)), ((), ()))}[mode]

    def kern(a_ref, b_ref, o_ref, *acc):
        part = lax.dot_general(a_ref[...].astype(BF16), b_ref[...].astype(BF16), dims, preferred_element_type=F32)
        if nk == 1:
            o_ref[...] = part.astype(out_dtype)
            return
        k = pl.program_id(2)

        @pl.when(k == 0)
        def _():
            acc[0][...] = part

        @pl.when(jnp.logical_and(k > 0, k < nk - 1))
        def _():
            acc[0][...] += part

        @pl.when(k == nk - 1)
        def _():
            o_ref[...] = (acc[0][...] + part).astype(out_dtype)

    a_spec = pl.BlockSpec((tk, tm), lambda i, j, k: (k, i)) if mode == "tn" else pl.BlockSpec((tm, tk), lambda i, j, k: (i, k))
    b_spec = pl.BlockSpec((tn, tk), lambda i, j, k: (j, k)) if mode == "nt" else pl.BlockSpec((tk, tn), lambda i, j, k: (k, j))
    return pl.pallas_call(
        kern, name=name, grid=(M // tm, N // tn, nk),
        in_specs=[a_spec, b_spec], out_specs=pl.BlockSpec((tm, tn), lambda i, j, k: (i, j)),
        out_shape=jax.ShapeDtypeStruct((M, N), out_dtype),
        scratch_shapes=[pltpu.VMEM((tm, tn), F32)] if nk > 1 else [],
        compiler_params=_ARB(3),
    )(a, b)


class Row:
    def __init__(self, a, width=None, cb=0, halo=False, dil=1):
        self.a, self.width, self.cb, self.halo, self.dil = a, width, cb, halo, dil


class Full:
    def __init__(self, a):
        self.a = a


HALO = 8


def _rows(name, body, tm, ins, outs):
    outs = [o if len(o) == 4 else o + (1,) for o in outs]
    L = next(s.a.shape[0] * s.dil for s in ins if isinstance(s, Row))
    tm = _tile(L, tm, 16)
    dilated = any(s.dil > 1 for s in ins if isinstance(s, Row)) or any(o[3] > 1 for o in outs)
    nt = L // tm
    nb8 = L // HALO
    step = tm // HALO
    in_specs, arrays, layout = [], [], []
    for s in ins:
        if isinstance(s, Full):
            nd = s.a.ndim
            in_specs.append(pl.BlockSpec(s.a.shape, lambda i, _nd=nd: (0,) * _nd))
            arrays.append(s.a)
            layout.append(1)
        else:
            w = s.width or s.a.shape[1]
            in_specs.append(pl.BlockSpec((tm // s.dil, w), lambda i, _cb=s.cb: (i, _cb)))
            arrays.append(s.a)
            if s.dil > 1:
                layout.append(-s.dil)
            elif s.halo:
                in_specs.append(pl.BlockSpec((HALO, w), lambda i, _cb=s.cb: (jnp.maximum(i * step - 1, 0), _cb)))
                in_specs.append(pl.BlockSpec((HALO, w), lambda i, _cb=s.cb: (jnp.minimum((i + 1) * step, nb8 - 1), _cb)))
                arrays += [s.a, s.a]
                layout.append(3)
            else:
                layout.append(1)
    out_specs, out_shapes = [], []
    for kind, shp, dt, dil in outs:
        if kind == "row":
            out_specs.append(pl.BlockSpec((tm // dil, dil * shp), lambda i: (i, 0)))
            out_shapes.append(jax.ShapeDtypeStruct((L // dil, dil * shp), dt))
        else:
            out_specs.append(pl.BlockSpec(shp, lambda i, _n=len(shp): (0,) * _n))
            out_shapes.append(jax.ShapeDtypeStruct(shp, dt))
    n_in, n_out = len(arrays), len(outs)

    def kern(*refs):
        i = pl.program_id(0)
        lo, hi = refs[n_in + n_out:] if dilated else (None, None)

        def undilate(ref, d):
            for j in range(d):
                rows = pl.ds(j, tm // d, stride=d)
                lo[rows, :] = ref[:, j * GROUP_W:j * GROUP_W + LANES].astype(F32)
                hi[rows, :] = ref[:, j * GROUP_W + LANES:(j + 1) * GROUP_W].astype(F32)
            return jnp.concatenate([lo[...], hi[...]], axis=1)

        def dilate(val, ref, d, dt):
            lo[...] = val[:, :LANES].astype(F32)
            hi[...] = val[:, LANES:].astype(F32)
            for j in range(d):
                rows = pl.ds(j, tm // d, stride=d)
                ref[:, j * GROUP_W:j * GROUP_W + LANES] = lo[rows, :].astype(dt)
                ref[:, j * GROUP_W + LANES:(j + 1) * GROUP_W] = hi[rows, :].astype(dt)

        vals, p = [], 0
        for n in layout:
            if n == 1:
                vals.append(refs[p][...])
            elif n < 0:
                vals.append(undilate(refs[p], -n))
                n = 1
            else:
                vals.append((refs[p + 1][...], refs[p][...], refs[p + 2][...]))
            p += n
        res = body(i, nt, *vals)
        if not isinstance(res, (tuple, list)):
            res = (res,)
        for (kind, shp, dt, dil), o_ref, r in zip(outs, refs[n_in:], res):
            if kind == "row" and dil > 1:
                dilate(r, o_ref, dil, dt)
            elif kind == "row":
                o_ref[...] = r.astype(dt)
            else:
                @pl.when(i == 0)
                def _(o_ref=o_ref):
                    o_ref[...] = jnp.zeros_like(o_ref)
                o_ref[...] += r.astype(dt)

    res = pl.pallas_call(
        kern, name=name, grid=(nt,), in_specs=in_specs, out_specs=out_specs, out_shape=out_shapes,
        scratch_shapes=[pltpu.VMEM((tm, LANES), F32)] * 2 if dilated else [], compiler_params=_ARB(1),
    )(*arrays)
    return res


def _shift(h, o, i, nt):
    prev, cur, nxt = h
    if o == 0:
        return cur
    tm = cur.shape[0]
    cat = jnp.concatenate([prev, cur, nxt], axis=0)
    sh = pltpu.roll(cat, (-o) % (tm + 2 * HALO), axis=0)[HALO:HALO + tm]
    row = lax.broadcasted_iota(jnp.int32, cur.shape, 0)
    if o < 0:
        ok = jnp.logical_or(i > 0, row >= -o)
    else:
        ok = jnp.logical_or(i < nt - 1, row < tm - o)
    return jnp.where(ok, sh, 0.0)


def _colsum(v):
    return jnp.sum(v, axis=0, keepdims=True)


def _sigmoid(x):
    return 1.0 / (1.0 + jnp.exp(-x))


def _softplus(x):
    return jnp.maximum(x, 0.0) + jnp.log1p(jnp.exp(-jnp.abs(x)))


def _silu(x):
    return x * _sigmoid(x)


def _dsilu(x):
    s = _sigmoid(x)
    return s * (1.0 + x * (1.0 - s))


_GELU_C = math.sqrt(2.0 / math.pi)


def _gelu(x):
    return 0.5 * x * (1.0 + jnp.tanh(_GELU_C * (x + 0.044715 * x * x * x)))


def _dgelu(x):
    t = jnp.tanh(_GELU_C * (x + 0.044715 * x * x * x))
    return 0.5 * (1.0 + t) + 0.5 * x * (1.0 - t * t) * _GELU_C * (1.0 + 3.0 * 0.044715 * x * x)


def _head_sum(v, bd):
    return jnp.dot(v, bd, precision=lax.Precision.HIGHEST, preferred_element_type=F32)


def _block_ones(n, blk):
    r = np.arange(n)
    return jnp.asarray((r[:, None] // blk == r[None, :] // blk).astype(np.float32))


def _rms_fwd(x, g, name):
    def body(i, nt, x, g):
        r = lax.rsqrt(jnp.mean(x * x, axis=-1, keepdims=True) + EPS)
        return x * r * g
    return _rows(name, body, 256, [Row(x), Full(g)], [("row", x.shape[1], BF16)])[0]


def _rms_resid_fwd(x, y, g, name):
    def body(i, nt, x, y, g):
        r = lax.rsqrt(jnp.mean(y * y, axis=-1, keepdims=True) + EPS)
        return x + y * r * g
    return _rows(name, body, 256, [Row(x), Row(y), Full(g)], [("row", x.shape[1], F32)])[0]


def _rms_bwd(dy, x, g, name, resid=None, out_dtype=F32):
    D = x.shape[1]

    def body(i, nt, dy, x, g, *rest):
        dy = dy.astype(F32)
        r = lax.rsqrt(jnp.mean(x * x, axis=-1, keepdims=True) + EPS)
        xh = x * r
        dxh = dy * g
        dx = r * (dxh - xh * jnp.mean(dxh * xh, axis=-1, keepdims=True))
        if rest:
            dx = dx + rest[0]
        return dx, _colsum(dy * xh)

    ins = [Row(dy), Row(x), Full(g)] + ([Row(resid)] if resid is not None else [])
    return _rows(name, body, 256, ins, [("row", D, out_dtype), ("acc", (1, D), F32)])


def _ffn_in_swiglu(h, w):
    (M, K), N = h.shape, w.shape[1]
    F = N // 2
    tm = _tile(M, 256, 16)

    def kern(a_ref, b_ref, gu_ref, act_ref):
        gu = _dot(a_ref[...], b_ref[...])
        gu_ref[...] = gu
        act_ref[...] = (_silu(gu[:, :F]) * gu[:, F:]).astype(BF16)

    return pl.pallas_call(
        kern, name="ffn_in_swiglu", grid=(M // tm,),
        in_specs=[pl.BlockSpec((tm, K), lambda i: (i, 0)), pl.BlockSpec((K, N), lambda i: (0, 0))],
        out_specs=[pl.BlockSpec((tm, N), lambda i: (i, 0)), pl.BlockSpec((tm, F), lambda i: (i, 0))],
        out_shape=[jax.ShapeDtypeStruct((M, N), F32), jax.ShapeDtypeStruct((M, F), BF16)], compiler_params=_ARB(1),
    )(h, w)


def _ffn_out_dx_swiglu(df, w, gu):
    (M, K), N = df.shape, gu.shape[1]
    F = N // 2
    tm = _tile(M, 256, 16)

    def kern(a_ref, b_ref, gu_ref, o_ref):
        da = _dot(a_ref[...], b_ref[...], _NT)
        gu = gu_ref[...]
        gate, up = gu[:, :F], gu[:, F:]
        o_ref[:, :F] = (da * up * _dsilu(gate)).astype(BF16)
        o_ref[:, F:] = (da * _silu(gate)).astype(BF16)

    return pl.pallas_call(
        kern, name="ffn_out_dx_swiglu", grid=(M // tm,),
        in_specs=[pl.BlockSpec((tm, K), lambda i: (i, 0)), pl.BlockSpec((F, K), lambda i: (0, 0)),
                  pl.BlockSpec((tm, N), lambda i: (i, 0))],
        out_specs=pl.BlockSpec((tm, N), lambda i: (i, 0)), out_shape=jax.ShapeDtypeStruct((M, N), BF16),
        compiler_params=_ARB(1),
    )(df, w, gu)


def _loss_fwd_bwd(y, target):
    D = y.shape[1]

    def body(i, nt, y, t):
        err = y - t
        part = 0.5 * jnp.sum(jnp.mean(err * err, axis=-1, keepdims=True), axis=0, keepdims=True)
        return err * (1.0 / D), jnp.broadcast_to(part, (1, LANES))
    dy, loss = _rows("loss", body, 256, [Row(y), Row(target)], [("row", D, F32), ("acc", (1, LANES), F32)])
    return loss[0, 0], dy


def _adamw(w, g, m, v, name):
    C = w.shape[1]
    bc1 = 1.0 - ADAM_B1 ** ADAM_STEP
    bc2 = 1.0 - ADAM_B2 ** ADAM_STEP

    def body(i, nt, w, g, m, v):
        m = ADAM_B1 * m + (1.0 - ADAM_B1) * g
        v = ADAM_B2 * v + (1.0 - ADAM_B2) * (g * g)
        delta = -ADAM_LR * ((m / bc1) / (jnp.sqrt(v / bc2) + ADAM_EPS) + ADAM_WD * w)
        return delta, m, v
    return _rows(name, body, 256, [Row(w), Row(g), Row(m), Row(v)], [("row", C, F32)] * 3)


def _expm1(x):
    return jnp.tanh(0.5 * x) * (jnp.exp(x) + 1.0)


def _lru_gates(xh, i, nt, cw, cb, wa, wx, ba, bx, lam):
    xc = cb
    for j in range(4):
        xc = xc + cw[j:j + 1] * _shift(xh, j - 2, i, nt)
    xcb = xc.astype(BF16)
    gates = []
    for e in range(2):
        r = _sigmoid(jnp.dot(xcb, wa[e], preferred_element_type=F32) + ba[e:e + 1])
        ig = _sigmoid(jnp.dot(xcb, wx[e], preferred_element_type=F32) + bx[e:e + 1])
        sp = _softplus(-lam[e:e + 1])
        la = -LRU_C * r * sp
        gates.append((r, ig, sp, jnp.exp(la), jnp.sqrt(-_expm1(2.0 * la))))
    return xc, xcb, gates


def _scan2(af, uf, ab, ub, adjoint, name):
    L, W = af.shape
    tm = _tile(L, 512, 8)
    nt, nb = L // tm, tm // 8

    def blk(A, U, h, reverse, row):
        for d in (1, 2, 4):
            if reverse:
                ok, sh = row < 8 - d, 8 - d
            else:
                ok, sh = row >= d, d
            As = jnp.where(ok, pltpu.roll(A, sh, axis=0), 1.0)
            Us = jnp.where(ok, pltpu.roll(U, sh, axis=0), 0.0)
            U = A * Us + U
            A = A * As
        return A * h + U

    def kern(af_ref, uf_ref, ab_ref, ub_ref, of_ref, ob_ref, c_ref):
        @pl.when(pl.program_id(0) == 0)
        def _():
            c_ref[...] = jnp.zeros_like(c_ref)

        row = lax.broadcasted_iota(jnp.int32, (8, W), 0)
        full = lambda v: jnp.broadcast_to(v, (8, W))

        def body(j, carry):
            hF, aF, hB, aB = carry
            r0 = pl.multiple_of(j * 8, 8)
            r1 = pl.multiple_of((nb - 1 - j) * 8, 8)
            A, U = af_ref[pl.ds(r0, 8), :], uf_ref[pl.ds(r0, 8), :]
            if adjoint:
                C = jnp.where(row == 0, aF, pltpu.roll(A, 1, axis=0))
                aF = full(A[7:8])
            else:
                C = A
            H = blk(C, U, hF, False, row)
            of_ref[pl.ds(r0, 8), :] = H
            hF = full(H[7:8])
            A, U = ab_ref[pl.ds(r1, 8), :], ub_ref[pl.ds(r1, 8), :]
            if adjoint:
                C = jnp.where(row == 7, aB, pltpu.roll(A, 7, axis=0))
                aB = full(A[0:1])
            else:
                C = A
            H = blk(C, U, hB, True, row)
            ob_ref[pl.ds(r1, 8), :] = H
            hB = full(H[0:1])
            return hF, aF, hB, aB

        carry = lax.fori_loop(0, nb, body, (c_ref[0], c_ref[1], c_ref[2], c_ref[3]))
        for n in range(4):
            c_ref[n] = carry[n]

    fwd = pl.BlockSpec((tm, W), lambda i: (i, 0))
    bwd = pl.BlockSpec((tm, W), lambda i: (nt - 1 - i, 0))
    return pl.pallas_call(
        kern, name=name, grid=(nt,), in_specs=[fwd, fwd, bwd, bwd], out_specs=[fwd, bwd],
        out_shape=[jax.ShapeDtypeStruct((L, W), F32)] * 2,
        scratch_shapes=[pltpu.VMEM((4, 8, W), F32)], compiler_params=_ARB(1),
    )(af, uf, ab, ub)


def _block_diag(w):
    rows = jnp.tile(w.reshape(2, GROUP_W, HEAD_DIM), (1, 1, 4))
    return jnp.where(_block_ones(GROUP_W, HEAD_DIM) > 0.5, rows, 0.0).astype(BF16)


def _diag_blocks(w):
    return jnp.stack([w[:, h * 64:(h + 1) * 64, h * 64:(h + 1) * 64] for h in range(4)], axis=1)


def _lru_params(W, l):
    return [Full(W["lru_conv_w"][l]), Full(W["lru_conv_b"][l][None]), Full(_block_diag(W["lru_w_a"][l])),
            Full(_block_diag(W["lru_w_x"][l])), Full(W["lru_b_a"][l]), Full(W["lru_b_x"][l]), Full(W["lru_lambda"][l])]


def _lru_fwd(p, W, l):
    def pre(i, nt, xh, *prm):
        xc, _, g = _lru_gates(xh, i, nt, *prm)
        return g[0][3], g[0][4] * (g[0][1] * xc), g[1][3], g[1][4] * (g[1][1] * xc)

    a0, u0, a1, u1 = _rows("lru_pre", pre, 256, [Row(p, GROUP_W, 7, halo=True)] + _lru_params(W, l),
                           [("row", GROUP_W, F32)] * 4)
    hf, hb = _scan2(a0, u0, a1, u1, False, "lru_scan")
    yc = _rows("lru_post", lambda i, nt, hf, hb, gc: (hf + hb) * _gelu(gc), 512,
               [Row(hf), Row(hb), Row(p, GROUP_W, 8)], [("row", GROUP_W, BF16)])[0]
    return yc, (a0, a1, hf, hb)


def _lru_bwd(dy, dy_cb, p, W, l, saved):
    a0, a1, hf, hb = saved

    def post(i, nt, dy, hf, hb, gc):
        return dy * _gelu(gc), dy * (hf + hb) * _dgelu(gc)

    dh, dgc = _rows("lru_post_bwd", post, 512, [Row(dy, GROUP_W, dy_cb), Row(hf), Row(hb), Row(p, GROUP_W, 8)],
                    [("row", GROUP_W, F32), ("row", GROUP_W, BF16)])
    gb, gf = _scan2(a1, dh, a0, dh, True, "lru_scan_adj")

    def gates_bwd(i, nt, xh, gf, gb, hfh, hbh, cw, cb, wa, wx, ba, bx, lam):
        xc, xcb, g = _lru_gates(xh, i, nt, cw, cb, wa, wx, ba, bx, lam)
        dxc = jnp.zeros_like(xc)
        dwa, dwx, dba, dbx, dlam = [], [], [], [], []
        for e, du, hprev in ((0, gf, _shift(hfh, -1, i, nt)), (1, gb, _shift(hbh, 1, i, nt))):
            r, ig, sp, a, s = g[e]
            dxc = dxc + du * s * ig
            dla = du * hprev * a - (du * ig * xc) * a * a / s
            dza = (dla * (-LRU_C) * sp) * r * (1.0 - r)
            dzx = (du * s * xc) * ig * (1.0 - ig)
            dlam.append(_colsum(dla * r) * (LRU_C * _sigmoid(-lam[e:e + 1])))
            dba.append(_colsum(dza))
            dbx.append(_colsum(dzx))
            dzab, dzxb = dza.astype(BF16), dzx.astype(BF16)
            tn = (((0,), (0,)), ((), ()))
            nt_ = (((1,), (1,)), ((), ()))
            dwa.append(lax.dot_general(xcb, dzab, tn, preferred_element_type=F32))
            dwx.append(lax.dot_general(xcb, dzxb, tn, preferred_element_type=F32))
            dxc = dxc + lax.dot_general(dzab, wa[e], nt_, preferred_element_type=F32)
            dxc = dxc + lax.dot_general(dzxb, wx[e], nt_, preferred_element_type=F32)
        cat = lambda v: jnp.concatenate(v, axis=0)
        return dxc, jnp.stack(dwa), jnp.stack(dwx), cat(dba), cat(dbx), cat(dlam)

    dxc, dwa, dwx, dba, dbx, dlam = _rows(
        "lru_gates_bwd", gates_bwd, 256,
        [Row(p, GROUP_W, 7, halo=True), Row(gf), Row(gb), Row(hf, halo=True), Row(hb, halo=True)] + _lru_params(W, l),
        [("row", GROUP_W, F32), ("acc", (2, GROUP_W, GROUP_W), F32), ("acc", (2, GROUP_W, GROUP_W), F32),
         ("acc", (2, GROUP_W), F32), ("acc", (2, GROUP_W), F32), ("acc", (2, GROUP_W), F32)])

    def conv_bwd(i, nt, dh_, xh, cw):
        dxb = jnp.zeros_like(dh_[1])
        dcw = []
        for j in range(4):
            dxb = dxb + cw[j:j + 1] * _shift(dh_, 2 - j, i, nt)
            dcw.append(_colsum(dh_[1] * _shift(xh, j - 2, i, nt)))
        return dxb, jnp.concatenate(dcw, axis=0), _colsum(dh_[1])

    dxb, dcw, dcb = _rows("lru_conv_bwd", conv_bwd, 512,
                          [Row(dxc, halo=True), Row(p, GROUP_W, 7, halo=True), Full(W["lru_conv_w"][l])],
                          [("row", GROUP_W, BF16), ("acc", (4, GROUP_W), F32), ("acc", (1, GROUP_W), F32)])
    grads = dict(lru_conv_w=dcw, lru_conv_b=dcb[0], lru_w_a=_diag_blocks(dwa), lru_w_x=_diag_blocks(dwx),
                 lru_b_a=dba, lru_b_x=dbx, lru_lambda=dlam)
    return dxb, dgc, grads


_NT = (((1,), (1,)), ((), ()))
_TN = (((0,), (0,)), ((), ()))


def _dot(a, b, dims=None):
    if dims is None:
        return jnp.dot(a, b, preferred_element_type=F32)
    return lax.dot_general(a, b, dims, preferred_element_type=F32)


def _dot_exact(a, b):
    return jnp.dot(a, b, precision=lax.Precision.HIGHEST, preferred_element_type=F32)


def _gla_gate_w(w_gate, b_gate):
    zero = jnp.zeros((GLA_RANK, GROUP_W), w_gate.dtype)
    wg = jnp.concatenate([jnp.concatenate([w_gate[0], zero], axis=1), jnp.concatenate([zero, w_gate[1]], axis=1),
                          jnp.zeros((LANES - 2 * GLA_RANK, 2 * GROUP_W), w_gate.dtype)], axis=0)
    return wg.astype(BF16), b_gate.reshape(1, 2 * GROUP_W)


def _gla_gates_fwd(p, wg, bg):
    def body(i, nt, z, wg, bg):
        logit = _dot(z.astype(BF16), wg) + bg
        la = -_softplus(-logit) * (1.0 / GLA_TAU)
        return la[:, :GROUP_W], la[:, GROUP_W:]
    return _rows("gla_gates", body, 512, [Row(p, LANES, Z_BLOCK), Full(wg), Full(bg)], [("row", GROUP_W, F32)] * 2)


def _gla_gates_bwd(p, dla0, dla1, wg, bg):
    def body(i, nt, z, d0, d1, wg, bg):
        zb = z.astype(BF16)
        logit = _dot(zb, wg) + bg
        dlogit = jnp.concatenate([d0, d1], axis=1) * (1.0 / GLA_TAU) * _sigmoid(-logit)
        dlb = dlogit.astype(BF16)
        return _dot(dlb, wg, _NT), _dot(zb, dlb, _TN), _colsum(dlogit)
    return _rows("gla_gates_bwd", body, 512, [Row(p, LANES, Z_BLOCK), Row(dla0), Row(dla1), Full(wg), Full(bg)],
                 [("row", LANES, BF16), ("acc", (LANES, 2 * GROUP_W), F32), ("acc", (1, 2 * GROUP_W), F32)])


def _gla_order(reverse):
    t = np.arange(GLA_CHUNK)
    m = (t[None, :] >= t[:, None]) if reverse else (t[None, :] <= t[:, None])
    return m.astype(np.float32), (32, 0) if reverse else (31, 63)


def _stack_heads(x, bd):
    return jnp.where(bd, jnp.concatenate([x] * 4, axis=0), 0.0)


def _diag_heads(r, bd):
    r = jnp.where(bd, r, 0.0)
    return r[0:64] + r[64:128] + r[128:192] + r[192:256]


def _gla_factors(q_ref, k_ref, rows, b, mid, last):
    bm, bl = b[mid:mid + 1], b[last:last + 1]
    qs = q_ref[rows, :] * ATT_SCALE
    k = k_ref[rows, :]
    P, N, E, Fd = jnp.exp(b - bm), jnp.exp(bm - b), jnp.exp(b), jnp.exp(bl - b)
    return (P, N, E, Fd, jnp.exp(bl)), (qs * P, k * N, qs * E, k * Fd)


def _gla_specs(L, walk_up):
    tm = _tile(L, 512, GLA_CHUNK)
    nt, nc = L // tm, tm // GLA_CHUNK
    specs = []
    for up in walk_up:
        t = (lambda i: i) if up else (lambda i: nt - 1 - i)
        specs.append(dict(
            col=lambda cb, _t=t: pl.BlockSpec((tm, GROUP_W), lambda i: (_t(i), cb)),
            row=pl.BlockSpec((tm, GROUP_W), lambda i, _t=t: (_t(i), 0)),
            state=pl.BlockSpec((nc, GROUP_W, GROUP_W), lambda i, _t=t: (_t(i), 0, 0))))
    return nt, nc, specs


def _gla_chunk_fwd(p, la0, la1, ride=None):
    L = la0.shape[0]
    nt, nc, specs = _gla_specs(L, (True, False))
    orders = [_gla_order(False), _gla_order(True)]

    def kern(q0, k0, v0, l0, q1, k1, v1, l1, m0_ref, m1_ref, bd_ref, o0, s0, o1, s1, st_ref):
        @pl.when(pl.program_id(0) == 0)
        def _():
            st_ref[...] = jnp.zeros_like(st_ref)

        bd = bd_ref[...] > 0.5
        dirs = []
        for e, (q_ref, k_ref, v_ref, la_ref, m_ref, o_ref, s_ref) in enumerate(
                ((q0, k0, v0, l0, m0_ref, o0, s0), (q1, k1, v1, l1, m1_ref, o1, s1))):
            mv = m_ref[...]
            dirs.append((q_ref, k_ref, v_ref, la_ref, mv, jnp.concatenate([mv] * 4, axis=0) > 0.5, o_ref, s_ref))

        def body(cc, carry):
            E = range(2)
            cs = [nc - 1 - cc if e else cc for e in E]
            rows = [pl.ds(pl.multiple_of(c * GLA_CHUNK, GLA_CHUNK), GLA_CHUNK) for c in cs]
            b = [_dot_exact(dirs[e][4], dirs[e][3][rows[e], :]) for e in E]
            t = [_gla_factors(dirs[e][0], dirs[e][1], rows[e], b[e], *orders[e][1]) for e in E]
            vb = [dirs[e][2][rows[e], :].astype(BF16) for e in E]
            st = [st_ref[e] for e in E]
            a = [_dot(_stack_heads(t[e][1][0], bd).astype(BF16), t[e][1][1].astype(BF16), _NT) for e in E]
            inter = [_dot(t[e][1][2].astype(BF16), st[e].astype(BF16), _NT) for e in E]
            kv = [_dot(vb[e], t[e][1][3].astype(BF16), _TN) for e in E]
            a = [jnp.where(dirs[e][5], a[e], 0.0).astype(BF16) for e in E]
            r = [_dot(a[e], vb[e]) for e in E]
            for e in E:
                dirs[e][7][cs[e]] = st[e]
                dirs[e][6][rows[e], :] = _diag_heads(r[e], bd) + inter[e]
                st_ref[e] = st[e] * t[e][0][4] + jnp.where(bd, kv[e], 0.0)
            return carry

        lax.fori_loop(0, nc, body, 0)

    const = lambda shp: pl.BlockSpec(shp, lambda i: (0, 0))
    in_specs, out_specs = [], []
    for sp in specs:
        in_specs += [sp["col"](0), sp["col"](1), sp["col"](2), sp["row"]]
        out_specs += [sp["row"], sp["state"]]
    return _call(
        kern, (p, p, p, la0, p, p, p, la1, jnp.asarray(orders[0][0]), jnp.asarray(orders[1][0]),
               _block_ones(GROUP_W, HEAD_DIM)),
        ride, lambda: (pl.program_id(0) == 0, pl.program_id(0) == nt - 1), name="gla_fwd", grid=(nt,),
        in_specs=in_specs + [const((GLA_CHUNK, GLA_CHUNK))] * 2 + [const((GROUP_W, GROUP_W))], out_specs=out_specs,
        out_shape=[jax.ShapeDtypeStruct((L, GROUP_W), F32),
                   jax.ShapeDtypeStruct((L // GLA_CHUNK, GROUP_W, GROUP_W), F32)] * 2,
        scratch_shapes=[pltpu.VMEM((2, GROUP_W, GROUP_W), F32)])


def _gla_chunk_bwd(p, la0, la1, do, sprev0, sprev1, ride=None):
    L = la0.shape[0]
    nt, nc, specs = _gla_specs(L, (False, True))
    orders = [_gla_order(False), _gla_order(True)]

    def kern(q0, k0, v0, l0, do0, s0, q1, k1, v1, l1, do1, s1, m0_ref, m1_ref, t0_ref, t1_ref, bd_ref, *rest):
        outs, dst_ref = (rest[0:4], rest[4:8]), rest[8]

        @pl.when(pl.program_id(0) == 0)
        def _():
            dst_ref[...] = jnp.zeros_like(dst_ref)

        bd = bd_ref[...] > 0.5
        row = lax.broadcasted_iota(jnp.int32, (GLA_CHUNK, GROUP_W), 0)
        dirs = []
        for ins, m_ref, t_ref in (((q0, k0, v0, l0, do0, s0), m0_ref, t0_ref), ((q1, k1, v1, l1, do1, s1), m1_ref, t1_ref)):
            mv = m_ref[...]
            dirs.append(ins + (mv, t_ref[...], jnp.concatenate([mv] * 4, axis=0) > 0.5))

        def body(cc, carry):
            E2 = range(2)
            cs = [cc if e else nc - 1 - cc for e in E2]
            rows = [pl.ds(pl.multiple_of(c * GLA_CHUNK, GLA_CHUNK), GLA_CHUNK) for c in cs]
            b = [_dot_exact(dirs[e][6], dirs[e][3][rows[e], :]) for e in E2]
            t = [_gla_factors(dirs[e][0], dirs[e][1], rows[e], b[e], *orders[e][1]) for e in E2]
            vb = [dirs[e][2][rows[e], :].astype(BF16) for e in E2]
            dov = [dirs[e][4][rows[e], :] for e in E2]
            dob = [x.astype(BF16) for x in dov]
            st = [dirs[e][5][cs[e]] for e in E2]
            dst = [dst_ref[e] for e in E2]
            stb, dstb = [x.astype(BF16) for x in st], [x.astype(BF16) for x in dst]
            qst = [_stack_heads(t[e][1][0], bd).astype(BF16) for e in E2]
            dost = [_stack_heads(dov[e], bd).astype(BF16) for e in E2]
            kNb, qEb, kFb = ([t[e][1][n].astype(BF16) for e in E2] for n in (1, 2, 3))
            a = [_dot(qst[e], kNb[e], _NT) for e in E2]
            da = [_dot(dost[e], vb[e], _NT) for e in E2]
            dqE = [_dot(dob[e], stb[e]) for e in E2]
            dkF = [_dot(vb[e], dstb[e]) for e in E2]
            dv_inter = [_dot(kFb[e], dstb[e], _NT) for e in E2]
            dst_in = [_dot(dob[e], qEb[e], _TN) for e in E2]
            a = [jnp.where(dirs[e][8], a[e], 0.0).astype(BF16) for e in E2]
            da = [jnp.where(dirs[e][8], da[e], 0.0).astype(BF16) for e in E2]
            dv_intra = [_dot(a[e], dost[e], _TN) for e in E2]
            dqP = [_dot(da[e], kNb[e]) for e in E2]
            dkN = [_dot(da[e], qst[e], _TN) for e in E2]
            db = []
            for e in E2:
                (P, N, Ef, Fd, d), (qP, kN, qE, kF) = t[e]
                mid, last = orders[e][1]
                dq_ref, dk_ref, dv_ref, _ = outs[e]
                dqp = _diag_heads(dqP[e], bd)
                dd = _colsum(dst[e] * st[e])
                dst_ref[e] = jnp.where(bd, dst_in[e], 0.0) + dst[e] * d
                tP, tN, tE, tF = dqp * qP, dkN[e] * kN, dqE[e] * qE, dkF[e] * kF
                db.append(tP - tN + tE - tF + jnp.where(row == mid, _colsum(tN - tP), 0.0)
                          + jnp.where(row == last, _colsum(tF) + dd * d, 0.0))
                dq_ref[rows[e], :] = (dqp * P + dqE[e] * Ef) * ATT_SCALE
                dk_ref[rows[e], :] = dkN[e] * N + dkF[e] * Fd
                dv_ref[rows[e], :] = dv_intra[e] + dv_inter[e]
            dla = [_dot_exact(dirs[e][7], db[e]) for e in E2]
            for e in E2:
                outs[e][3][rows[e], :] = dla[e]
            return carry

        lax.fori_loop(0, nc, body, 0)

    const = lambda shp: pl.BlockSpec(shp, lambda i: (0, 0))
    in_specs, out_specs = [], []
    for sp in specs:
        in_specs += [sp["col"](0), sp["col"](1), sp["col"](2), sp["row"], sp["row"], sp["state"]]
        out_specs += [sp["row"]] * 4
    m0, m1 = orders[0][0], orders[1][0]
    return _call(
        kern, (p, p, p, la0, do, sprev0, p, p, p, la1, do, sprev1, jnp.asarray(m0), jnp.asarray(m1),
               jnp.asarray(m0.T.copy()), jnp.asarray(m1.T.copy()), _block_ones(GROUP_W, HEAD_DIM)),
        ride, lambda: (pl.program_id(0) == 0, pl.program_id(0) == nt - 1), name="gla_bwd", grid=(nt,),
        in_specs=in_specs + [const((GLA_CHUNK, GLA_CHUNK))] * 4 + [const((GROUP_W, GROUP_W))], out_specs=out_specs,
        out_shape=[jax.ShapeDtypeStruct((L, GROUP_W), F32)] * 8,
        scratch_shapes=[pltpu.VMEM((2, GROUP_W, GROUP_W), F32)])


def _gla_fwd(p, W, l, ride=None):
    wg, bg = _gla_gate_w(W["gla_w_gate"][l], W["gla_b_gate"][l])
    la0, la1 = _gla_gates_fwd(p, wg, bg)
    (of, s0, ob, s1), got = _gla_chunk_fwd(p, la0, la1, ride)

    def post(i, nt, of, ob, g, ng, bd):
        o = of + ob
        r = lax.rsqrt(_head_sum(o * o, bd) * (1.0 / HEAD_DIM) + EPS)
        return o * r * ng * _silu(g)

    ya = _rows("gla_post", post, 512, [Row(of), Row(ob), Row(p, GROUP_W, 3), Full(W["gla_norm"][l][None]),
                                       Full(_block_ones(GROUP_W, HEAD_DIM))], [("row", GROUP_W, BF16)])[0]
    return ya, (la0, la1, of, ob, s0, s1), got


def _gla_bwd(dy, dy_cb, p, W, l, saved, ride=None):
    la0, la1, of, ob, s0, s1 = saved
    wg, bg = _gla_gate_w(W["gla_w_gate"][l], W["gla_b_gate"][l])

    def post(i, nt, dy, of, ob, g, ng, bd):
        o = of + ob
        r = lax.rsqrt(_head_sum(o * o, bd) * (1.0 / HEAD_DIM) + EPS)
        oh = o * r
        don = dy * _silu(g)
        doh = don * ng
        do = r * (doh - oh * _head_sum(doh * oh, bd) * (1.0 / HEAD_DIM))
        return do, dy * (oh * ng) * _dsilu(g), _colsum(don * oh)

    do, dg, dng = _rows("gla_post_bwd", post, 512,
                        [Row(dy, GROUP_W, dy_cb), Row(of), Row(ob), Row(p, GROUP_W, 3), Full(W["gla_norm"][l][None]),
                         Full(_block_ones(GROUP_W, HEAD_DIM))],
                        [("row", GROUP_W, F32), ("row", GROUP_W, BF16), ("acc", (1, GROUP_W), F32)])
    (dq0, dk0, dv0, dla0, dq1, dk1, dv1, dla1), got = _gla_chunk_bwd(p, la0, la1, do, s0, s1, ride)
    dq, dk, dv = _rows("gla_sum_bwd", lambda i, nt, a0, a1, b0, b1, c0, c1: (a0 + a1, b0 + b1, c0 + c1), 512,
                       [Row(t) for t in (dq0, dq1, dk0, dk1, dv0, dv1)], [("row", GROUP_W, BF16)] * 3)
    dz, dwg, dbg = _gla_gates_bwd(p, dla0, dla1, wg, bg)
    dw_gate = jnp.stack([dwg[e * GLA_RANK:(e + 1) * GLA_RANK, e * GROUP_W:(e + 1) * GROUP_W] for e in range(2)])
    grads = dict(gla_w_gate=dw_gate, gla_b_gate=dbg.reshape(2, GROUP_W), gla_norm=dng[0])
    return (dq, dk, dv, dg, dz), grads, got


def _rope_tables(L):
    pos = jnp.arange(L, dtype=F32)
    inv_freq = ROPE_THETA ** (-jnp.arange(0, HEAD_DIM, 2, dtype=F32) / HEAD_DIM)
    ang = pos[:, None] * inv_freq[None, :]
    cos, sin = jnp.cos(ang), jnp.sin(ang)
    return jnp.tile(jnp.concatenate([cos, cos], axis=1), (1, 4)), jnp.tile(jnp.concatenate([-sin, sin], axis=1), (1, 4))


def _swap_halves(t):
    lane = lax.broadcasted_iota(jnp.int32, t.shape, 1)
    first = (lane & (HEAD_DIM - 1)) < HEAD_DIM // 2
    return jnp.where(first, pltpu.roll(t, GROUP_W - HEAD_DIM // 2, axis=1), pltpu.roll(t, HEAD_DIM // 2, axis=1))


def _attn_prep(p, cosf, sinf):
    dils = [dil for _, dil in DIL_PAIRS]

    def body(i, nt, qb, kb, vb, qd, kd, vd, c, s):
        d = (qd * c + _swap_halves(qd) * s, kd * c + _swap_halves(kd) * s, vd)
        return (qb, kb, vb) + d * len(dils)
    ins = [Row(p, GROUP_W, cb) for cb in (4, 5, 6, 9, 10, 11)] + [Row(cosf), Row(sinf)]
    outs = [("row", GROUP_W, BF16)] * 3 + [("row", GROUP_W, BF16, dil) for dil in dils for _ in range(3)]
    res = _rows("attn_prep", body, 512, ins, outs)
    return tuple(res[:3]), {dil: tuple(res[3 + 3 * n:6 + 3 * n]) for n, dil in enumerate(dils)}


def _na_onehot():
    c = np.arange(GRID_W)
    dc = np.clip(c[None, :] - c[:, None], -(NA_COLS - 1), NA_COLS - 1) + NA_COLS - 1
    oh = np.zeros((LANES, GRID_W * GRID_W), np.float32)
    oh[dc.reshape(-1), np.arange(GRID_W * GRID_W)] = 1.0
    return jnp.asarray(oh)


def _na_colmask():
    c = np.arange(GRID_W)
    start = np.clip(c - NA_COLS // 2, 0, GRID_W - NA_COLS)
    ok = (c[None, :] >= start[:, None]) & (c[None, :] < start[:, None] + NA_COLS)
    return jnp.asarray(np.where(ok, 0.0, NEG).astype(np.float32))


N_DR = 2 * NA_ROWS - 1


NA_HALF = GRID_W // 2
NA_KCOLS = 48
NA_WIN = NA_ROWS * NA_KCOLS
NA_ROWS_PER_STEP = 2


def _na_bias(rpb):
    rp = jnp.pad(rpb.reshape(4 * N_DR, 2 * NA_COLS - 1), ((0, GRID_W - 4 * N_DR), (0, LANES - 2 * NA_COLS + 1)))

    def expand(r_ref, oh_ref, o_ref):
        o_ref[...] = _dot_exact(r_ref[...], oh_ref[...])

    r = pl.pallas_call(expand, name="na_bias_expand",
                       out_shape=jax.ShapeDtypeStruct((GRID_W, GRID_W * GRID_W), F32))(rp, _na_onehot())
    r = r[:4 * N_DR].reshape(4, N_DR, GRID_W, GRID_W)

    def build(r_ref, m_ref, o_ref):
        for h in range(4):
            for c in range(NA_ROWS):
                for half in range(2):
                    q0, k0 = NA_HALF * half, 16 * half
                    for i in range(NA_ROWS):
                        o_ref[h, c, half, :, i * NA_KCOLS:(i + 1) * NA_KCOLS] = (
                            r_ref[h, i - c + NA_ROWS - 1, q0:q0 + NA_HALF, k0:k0 + NA_KCOLS]
                            + m_ref[q0:q0 + NA_HALF, k0:k0 + NA_KCOLS])

    return pl.pallas_call(build, name="na_bias_build",
                          out_shape=jax.ShapeDtypeStruct((4, NA_ROWS, 2, NA_HALF, NA_WIN), F32))(r, _na_colmask())


def _na_bias_bwd(dbias):
    def fold(d_ref, o_ref):
        o_ref[...] = jnp.zeros_like(o_ref)
        for h in range(4):
            for a in range(N_DR):
                for half in range(2):
                    q0, k0 = NA_HALF * half, 16 * half
                    acc = jnp.zeros((NA_HALF, NA_KCOLS), F32)
                    for c in range(NA_ROWS):
                        i = a + c - (NA_ROWS - 1)
                        if 0 <= i < NA_ROWS:
                            acc = acc + d_ref[h, c, half, :, i * NA_KCOLS:(i + 1) * NA_KCOLS]
                    o_ref[h, a, q0:q0 + NA_HALF, k0:k0 + NA_KCOLS] = acc

    dr = pl.pallas_call(fold, name="na_bias_fold",
                        out_shape=jax.ShapeDtypeStruct((4, N_DR, GRID_W, GRID_W), F32))(dbias)
    dr = jnp.pad(dr.reshape(4 * N_DR, GRID_W * GRID_W), ((0, GRID_W - 4 * N_DR), (0, 0)))

    def contract(d_ref, oh_ref, o_ref):
        o_ref[...] = lax.dot_general(d_ref[...], oh_ref[...], _NT, precision=lax.Precision.HIGHEST,
                                     preferred_element_type=F32)

    g = pl.pallas_call(contract, name="na_bias_contract",
                       out_shape=jax.ShapeDtypeStruct((GRID_W, LANES), F32))(dr, _na_onehot())
    return g[:4 * N_DR, :2 * NA_COLS - 1].reshape(4, N_DR, 2 * NA_COLS - 1)


def _na_window(r, n_rows):
    rs = jnp.clip(r - NA_ROWS // 2, 0, n_rows - NA_ROWS)
    return rs, r - rs


def _na_key_rows(rs, half, t):
    return pl.ds(pl.multiple_of((rs + t) * GRID_W + 16 * half, 16), NA_KCOLS)


def _na_keys(ref, rs, half):
    return jnp.concatenate([ref[_na_key_rows(rs, half, t), :] for t in range(NA_ROWS)], axis=0)


def _na_stack(x, first):
    zero = jnp.zeros_like(x)
    return jnp.concatenate([jnp.where(first, x, zero), jnp.where(first, zero, x)], axis=0)


def _na_bias_spec():
    return pl.BlockSpec((2, NA_ROWS, 2, NA_HALF, NA_WIN), lambda j, i: (j, 0, 0, 0, 0))


def _grid_edges(n0, n1):
    j, i = pl.program_id(0), pl.program_id(1)
    return jnp.logical_and(j == 0, i == 0), jnp.logical_and(j == n0 - 1, i == n1 - 1)


def _na_fwd(q, k, v, bias, ride=None):
    L = q.shape[0]
    n_rows = L // GRID_W
    tm = _tile(L, 512, GRID_W)
    nt, nr = L // tm, tm // GRID_W

    def kern(q_ref, k_ref, v_ref, b_ref, o_ref):
        i = pl.program_id(1)
        first = lax.broadcasted_iota(jnp.int32, (NA_HALF, LANES), 1) < HEAD_DIM

        def body(it, carry):
            parts = []
            for u in range(NA_ROWS_PER_STEP):
                rr = it * NA_ROWS_PER_STEP + u
                rs, c = _na_window(i * nr + rr, n_rows)
                for half in range(2):
                    rows = pl.ds(pl.multiple_of(rr * GRID_W + NA_HALF * half, NA_HALF), NA_HALF)
                    bias = jnp.concatenate([b_ref[0, c, half], b_ref[1, c, half]], axis=0)
                    parts.append((rows, _na_stack(q_ref[rows, :], first), bias, _na_keys(k_ref, rs, half),
                                  _na_keys(v_ref, rs, half)))
            s = [_dot(qs, kw, _NT) * ATT_SCALE + bias for _, qs, bias, kw, _ in parts]
            e = [jnp.exp(x - jnp.max(x, axis=-1, keepdims=True)) for x in s]
            pn = [(x / jnp.sum(x, axis=-1, keepdims=True)).astype(BF16) for x in e]
            o = [_dot(p, part[4]) for p, part in zip(pn, parts)]
            for x, (rows, *_) in zip(o, parts):
                o_ref[rows, :] = jnp.where(first, x[:NA_HALF], x[NA_HALF:]).astype(BF16)
            return carry

        lax.fori_loop(0, nr // NA_ROWS_PER_STEP, body, 0)

    qspec = pl.BlockSpec((tm, LANES), lambda j, i: (i, j))
    kvspec = pl.BlockSpec((L, LANES), lambda j, i: (0, j))
    (y,), got = _call(
        kern, (q, k, v, bias), ride, lambda: _grid_edges(2, nt), name="na_fwd", grid=(2, nt),
        in_specs=[qspec, kvspec, kvspec, _na_bias_spec()],
        out_specs=[qspec], out_shape=[jax.ShapeDtypeStruct((L, GROUP_W), BF16)], scratch_shapes=[])
    return y, got


def _na_bwd(dy, dy_block, q, k, v, bias, ride=None):
    L = q.shape[0]
    n_rows = L // GRID_W
    tm = _tile(L, 512, GRID_W)
    nt, nr = L // tm, tm // GRID_W

    def kern(dy_ref, q_ref, k_ref, v_ref, b_ref, dq_ref, dk_ref, dv_ref, db_ref):
        i = pl.program_id(1)

        @pl.when(i == 0)
        def _():
            dk_ref[...] = jnp.zeros_like(dk_ref)
            dv_ref[...] = jnp.zeros_like(dv_ref)
            db_ref[...] = jnp.zeros_like(db_ref)

        first = lax.broadcasted_iota(jnp.int32, (NA_HALF, LANES), 1) < HEAD_DIM

        def body(rr, carry):
            rs, c = _na_window(i * nr + rr, n_rows)
            parts = []
            for half in range(2):
                rows = pl.ds(pl.multiple_of(rr * GRID_W + NA_HALF * half, NA_HALF), NA_HALF)
                bias = jnp.concatenate([b_ref[0, c, half], b_ref[1, c, half]], axis=0)
                parts.append((rows, half, _na_stack(q_ref[rows, :], first), _na_stack(dy_ref[rows, :].astype(BF16), first),
                              bias, _na_keys(k_ref, rs, half), _na_keys(v_ref, rs, half)))
            s = [_dot(qs, kw, _NT) * ATT_SCALE + bias for _, _, qs, _, bias, kw, _ in parts]
            dp = [_dot(dos, vw, _NT) for _, _, _, dos, _, _, vw in parts]
            e = [jnp.exp(x - jnp.max(x, axis=-1, keepdims=True)) for x in s]
            pn = [x / jnp.sum(x, axis=-1, keepdims=True) for x in e]
            ds = [p * (d - jnp.sum(p * d, axis=-1, keepdims=True)) for p, d in zip(pn, dp)]
            dsb = [x.astype(BF16) for x in ds]
            pnb = [x.astype(BF16) for x in pn]
            dq = [_dot(x, part[5]) for x, part in zip(dsb, parts)]
            dk = [_dot(x, part[2], _TN) for x, part in zip(dsb, parts)]
            dv = [_dot(x, part[3], _TN) for x, part in zip(pnb, parts)]
            for n, (rows, half, *_) in enumerate(parts):
                db_ref[0, c, half] += ds[n][:NA_HALF]
                db_ref[1, c, half] += ds[n][NA_HALF:]
                dq_ref[rows, :] = (jnp.where(first, dq[n][:NA_HALF], dq[n][NA_HALF:]) * ATT_SCALE).astype(BF16)
                for t in range(NA_ROWS):
                    kr = _na_key_rows(rs, half, t)
                    dk_ref[kr, :] += dk[n][t * NA_KCOLS:(t + 1) * NA_KCOLS] * ATT_SCALE
                    dv_ref[kr, :] += dv[n][t * NA_KCOLS:(t + 1) * NA_KCOLS]
            return carry

        lax.fori_loop(0, nr, body, 0)

    qspec = pl.BlockSpec((tm, LANES), lambda j, i: (i, j))
    kvspec = pl.BlockSpec((L, LANES), lambda j, i: (0, j))
    return _call(
        kern, (dy, q, k, v, bias), ride, lambda: _grid_edges(2, nt), name="na_bwd", grid=(2, nt),
        in_specs=[pl.BlockSpec((tm, LANES), lambda j, i: (i, dy_block + j)), qspec, kvspec, kvspec, _na_bias_spec()],
        out_specs=[qspec, kvspec, kvspec, _na_bias_spec()],
        out_shape=[jax.ShapeDtypeStruct((L, GROUP_W), BF16), jax.ShapeDtypeStruct((L, GROUP_W), F32),
                   jax.ShapeDtypeStruct((L, GROUP_W), F32),
                   jax.ShapeDtypeStruct((4, NA_ROWS, 2, NA_HALF, NA_WIN), F32)], scratch_shapes=[])


def _dil_specs(n, tq):
    R = DIL_RADIUS
    step, nb = tq // R, n // R
    main = pl.BlockSpec((tq, LANES), lambda j, i: (i, j))
    prev = pl.BlockSpec((R, LANES), lambda j, i: (jnp.maximum(i * step - 1, 0), j))
    nxt = pl.BlockSpec((R, LANES), lambda j, i: (jnp.minimum((i + 1) * step, nb - 1), j))
    return main, prev, nxt


def _dil_valid(i, sb, tq, n):
    R = DIL_RADIUS
    row = lax.broadcasted_iota(jnp.int32, (2 * R, 3 * R), 0) & (R - 1)
    col = lax.broadcasted_iota(jnp.int32, (2 * R, 3 * R), 1)
    kpos = i * tq + (sb - 1) * R + col
    return (jnp.abs(col - R - row) <= R) & (kpos >= 0) & (kpos < n)


def _dil_fwd(q, k, v, dil, ride=None):
    n = q.shape[0]
    tq = _tile(n, 256, DIL_RADIUS)

    def kern(q_ref, kp_ref, k_ref, kn_ref, vp_ref, v_ref, vn_ref, o_ref, l_ref):
        i = pl.program_id(1)
        R = DIL_RADIUS
        ka = jnp.concatenate([kp_ref[...], k_ref[...], kn_ref[...]], axis=0)
        va = jnp.concatenate([vp_ref[...], v_ref[...], vn_ref[...]], axis=0)
        first = lax.broadcasted_iota(jnp.int32, (R, LANES), 1) < HEAD_DIM
        subs = range(tq // R)
        keys = lambda a, sb: a[sb * R:(sb + 3) * R]
        qs = [_na_stack(q_ref[sb * R:(sb + 1) * R, :], first) for sb in subs]
        s = [_dot(qs[sb], keys(ka, sb), _NT) for sb in subs]
        s = [jnp.where(_dil_valid(i, sb, tq, n), s[sb] * ATT_SCALE, NEG) for sb in subs]
        m = [jnp.max(x, axis=-1, keepdims=True) for x in s]
        e = [jnp.exp(x - mx) for x, mx in zip(s, m)]
        den = [jnp.sum(x, axis=-1, keepdims=True) for x in e]
        o = [_dot((e[sb] / den[sb]).astype(BF16), keys(va, sb)) for sb in subs]
        for sb in subs:
            lse = m[sb] + jnp.log(den[sb])
            o_ref[sb * R:(sb + 1) * R, :] = jnp.where(first, o[sb][:R], o[sb][R:])
            l_ref[sb * R:(sb + 1) * R, :] = jnp.where(first, lse[:R], lse[R:])

    main, prev, nxt = _dil_specs(n, tq)
    (o, lse), got = _call(
        kern, (q, k, k, k, v, v, v), ride,
        lambda: _grid_edges(2 * dil, n // tq), name=f"dil_fwd_{dil}", grid=(2 * dil, n // tq),
        in_specs=[main, prev, main, nxt, prev, main, nxt], out_specs=[main, main],
        out_shape=[jax.ShapeDtypeStruct((n, dil * GROUP_W), F32)] * 2, scratch_shapes=[])
    return (o, lse), got


def _dil_bwd(q, k, v, do, lse, dterm, dil, ride=None):
    n = q.shape[0]
    R = DIL_RADIUS
    tq = _tile(n, 256, R)
    nq = n // tq

    def kern(q_ref, kp_ref, k_ref, kn_ref, vp_ref, v_ref, vn_ref, do_ref, l_ref, dt_ref, dq_ref, dk_ref, dv_ref):
        i = pl.program_id(1)

        @pl.when(i == 0)
        def _():
            dk_ref[...] = jnp.zeros_like(dk_ref)
            dv_ref[...] = jnp.zeros_like(dv_ref)

        ka = jnp.concatenate([kp_ref[...], k_ref[...], kn_ref[...]], axis=0)
        va = jnp.concatenate([vp_ref[...], v_ref[...], vn_ref[...]], axis=0)
        first = lax.broadcasted_iota(jnp.int32, (R, LANES), 1) < HEAD_DIM
        subs = range(tq // R)
        keys = lambda a, sb: a[sb * R:(sb + 3) * R]
        rows = lambda ref, sb: ref[sb * R:(sb + 1) * R, :]
        per_head = lambda t: jnp.concatenate([t[:, 0:1], t[:, HEAD_DIM:HEAD_DIM + 1]], axis=0)
        qs = [_na_stack(rows(q_ref, sb), first) for sb in subs]
        dos = [_na_stack(rows(do_ref, sb), first) for sb in subs]
        s = [_dot(qs[sb], keys(ka, sb), _NT) for sb in subs]
        dp = [_dot(dos[sb], keys(va, sb), _NT) for sb in subs]
        pn = [jnp.where(_dil_valid(i, sb, tq, n), jnp.exp(s[sb] * ATT_SCALE - per_head(rows(l_ref, sb))), 0.0) for sb in subs]
        dsb = [(pn[sb] * (dp[sb] - per_head(rows(dt_ref, sb)))).astype(BF16) for sb in subs]
        pnb = [x.astype(BF16) for x in pn]
        dq = [_dot(dsb[sb], keys(ka, sb)) for sb in subs]
        dk = [_dot(dsb[sb], qs[sb], _TN) for sb in subs]
        dv = [_dot(pnb[sb], dos[sb], _TN) for sb in subs]
        zeros = lambda blocks: [jnp.zeros((blocks * R, LANES), F32)] if blocks else []
        pad = lambda t, sb: jnp.concatenate(zeros(sb) + [t] + zeros(len(subs) - 1 - sb), axis=0)
        dka = sum(pad(dk[sb], sb) for sb in subs) * ATT_SCALE
        dva = sum(pad(dv[sb], sb) for sb in subs)
        for sb in subs:
            dq_ref[sb * R:(sb + 1) * R, :] = jnp.where(first, dq[sb][:R], dq[sb][R:]) * ATT_SCALE
        r0 = pl.multiple_of(i * tq, R)
        dk_ref[pl.ds(r0, tq), :] += dka[R:R + tq]
        dv_ref[pl.ds(r0, tq), :] += dva[R:R + tq]

        @pl.when(i > 0)
        def _():
            dk_ref[pl.ds(r0 - R, R), :] += dka[:R]
            dv_ref[pl.ds(r0 - R, R), :] += dva[:R]

        @pl.when(i < nq - 1)
        def _():
            dk_ref[pl.ds(r0 + tq, R), :] += dka[R + tq:]
            dv_ref[pl.ds(r0 + tq, R), :] += dva[R + tq:]

    main, prev, nxt = _dil_specs(n, tq)
    whole = pl.BlockSpec((n, LANES), lambda j, i: (0, j))
    shp = jax.ShapeDtypeStruct((n, dil * GROUP_W), F32)
    (dq, dk, dv), got = _call(
        kern, (q, k, k, k, v, v, v, do, lse, dterm), ride,
        lambda: _grid_edges(2 * dil, nq), name=f"dil_bwd_{dil}", grid=(2 * dil, nq),
        in_specs=[main, prev, main, nxt, prev, main, nxt, main, main, main], out_specs=[main, whole, whole],
        out_shape=[shp] * 3, scratch_shapes=[])
    return (dq, dk, dv), got


def _dil_weights(lses):
    m = jnp.maximum(jnp.maximum(lses[0], lses[1]), lses[2])
    e = [jnp.exp(l - m) for l in lses]
    tot = e[0] + e[1] + e[2]
    return [x / tot for x in e]


def _dilated_fwd(qkv, rides):
    dils = [dil for _, dil in DIL_PAIRS]
    res, got = [], {}
    for dil in dils:
        r, got[f"dil{dil}"] = _dil_fwd(*qkv[dil], dil, rides.get(f"dil{dil}"))
        res.append(r)

    def body(i, nt, o0, o1, o2, l0, l1, l2):
        w = _dil_weights((l0, l1, l2))
        return w[0] * o0 + w[1] * o1 + w[2] * o2

    ins = [Row(r[0], dil=d) for r, d in zip(res, dils)] + [Row(r[1], dil=d) for r, d in zip(res, dils)]
    return _rows("dil_combine", body, 512, ins, [("row", GROUP_W, BF16)])[0], res, got


def _dilated_bwd(dy, dy_cb, qkv, saved, cosf, sinf, rides):
    dils = [dil for _, dil in DIL_PAIRS]
    def split(i, nt, dy, o0, o1, o2, l0, l1, l2, bd):
        w = _dil_weights((l0, l1, l2))
        y = w[0] * o0 + w[1] * o1 + w[2] * o2
        dyy = _head_sum(dy * y, bd)
        return tuple(wg * dy for wg in w) + tuple(wg * dyy for wg in w)

    ins = ([Row(dy, GROUP_W, dy_cb)] + [Row(r[0], dil=d) for r, d in zip(saved, dils)]
           + [Row(r[1], dil=d) for r, d in zip(saved, dils)])
    outs = _rows("dil_split_bwd", split, 512, ins + [Full(_block_ones(GROUP_W, HEAD_DIM))],
                 [("row", GROUP_W, BF16, d) for d in dils] + [("row", GROUP_W, F32, d) for d in dils])
    g, got = [], {}
    for b, dil in enumerate(dils):
        r, got[f"dil{dil}"] = _dil_bwd(*qkv[dil], outs[b], saved[b][1], outs[3 + b], dil, rides.get(f"dil{dil}"))
        g.append(r)

    def finish(i, nt, q0, q1, q2, k0, k1, k2, v0, v1, v2, c, s):
        dq, dk = q0 + q1 + q2, k0 + k1 + k2
        return dq * c + _swap_halves(dq * s), dk * c + _swap_halves(dk * s), v0 + v1 + v2

    ins = [Row(g[b][t], dil=dils[b]) for t in range(3) for b in range(3)] + [Row(cosf), Row(sinf)]
    return _rows("dil_finish_bwd", finish, 512, ins, [("row", GROUP_W, BF16)] * 3), got


def _layer_fwd(x, W, l, cosf, sinf, rides):
    h1 = _rms_fwd(x, W["mix_norm_pre"][l][None], "mix_norm")
    p = _mm(h1, W["w_in"][l], "nn", F32, "proj_in")
    ya, sa, got_gla = _gla_fwd(p, W, l, rides.get("gla"))
    (qb, kb, vb), qkv_d = _attn_prep(p, cosf, sinf)
    bias = _na_bias(W["na_rpb"][l])
    yb, got_na = _na_fwd(qb, kb, vb, bias, rides.get("na"))
    yc, sc = _lru_fwd(p, W, l)
    yd, sd, got = _dilated_fwd(qkv_d, rides)
    got.update(gla=got_gla, na=got_na)
    ycat = jnp.concatenate([ya, yb, yc, yd], axis=1)
    ymix = _mm(ycat, W["w_out"][l], "nn", F32, "proj_out", tm=1024)
    xm = _rms_resid_fwd(x, ymix, W["mix_norm_post"][l][None], "mix_resid")
    h2 = _rms_fwd(xm, W["ffn_norm_pre"][l][None], "ffn_norm")
    gu, act = _ffn_in_swiglu(h2, W["ffn_w_in"][l])
    f = _mm(act, W["ffn_w_out"][l], "nn", F32, "ffn_out")
    xo = _rms_resid_fwd(xm, f, W["ffn_norm_post"][l][None], "ffn_resid")
    saved = dict(x=x, h1=h1, p=p, sa=sa, att=(qb, kb, vb, qkv_d), bias=bias, sc=sc, sd=sd, ycat=ycat, ymix=ymix,
                 xm=xm, h2=h2, gu=gu, act=act, f=f)
    return xo, saved, got


def _layer_bwd(dxo, W, l, S, cosf, sinf, rides):
    g = {}
    df, g["ffn_norm_post"] = _rms_bwd(dxo, S["f"], W["ffn_norm_post"][l][None], "ffn_resid_bwd", out_dtype=BF16)
    g["ffn_w_out"] = _mm(S["act"], df, "tn", BF16, "ffn_out_dw", tm=256, tk=4096)
    dgu = _ffn_out_dx_swiglu(df, W["ffn_w_out"][l], S["gu"])
    dh2 = _mm(dgu, W["ffn_w_in"][l], "nt", F32, "ffn_in_dx")
    g["ffn_w_in"] = _mm(S["h2"], dgu, "tn", BF16, "ffn_in_dw", tm=1024, tn=512, tk=4096)
    dxm, g["ffn_norm_pre"] = _rms_bwd(dh2, S["xm"], W["ffn_norm_pre"][l][None], "ffn_norm_bwd", resid=dxo)
    dymix, g["mix_norm_post"] = _rms_bwd(dxm, S["ymix"], W["mix_norm_post"][l][None], "mix_resid_bwd", out_dtype=BF16)
    dycat = _mm(dymix, W["w_out"][l], "nt", F32, "proj_out_dx", tm=1024)
    g["w_out"] = _mm(S["ycat"], dymix, "tn", BF16, "proj_out_dw", tm=1024, tn=512, tk=4096)
    p = S["p"]
    qb, kb, vb, qkv_d = S["att"]
    (dqa, dka, dva, dga, dz), ga, got_gla = _gla_bwd(dycat, 0, p, W, l, S["sa"], rides.get("gla"))
    (dqb, dkb, dvb, dbias), got_na = _na_bwd(dycat, 2, qb, kb, vb, S["bias"], rides.get("na"))
    g["na_rpb"] = _na_bias_bwd(dbias)
    dxc, dgc, gc = _lru_bwd(dycat, 2, p, W, l, S["sc"])
    (dqd, dkd, dvd), got = _dilated_bwd(dycat, 3, qkv_d, S["sd"], cosf, sinf, rides)
    got.update(gla=got_gla, na=got_na)
    g.update(ga)
    g.update(gc)
    dp = jnp.concatenate([dqa, dka, dva, dga, dqb, dkb.astype(BF16), dvb.astype(BF16), dxc, dgc, dqd, dkd, dvd, dz], axis=1)
    dh1 = _mm(dp, W["w_in"][l], "nt", F32, "proj_in_dx")
    g["w_in"] = _mm(S["h1"], dp, "tn", BF16, "proj_in_dw", tm=1024, tn=640, tk=4096)
    dx, g["mix_norm_pre"] = _rms_bwd(dh1, S["x"], W["mix_norm_pre"][l][None], "mix_norm_bwd", resid=dxm)
    for n in ("ffn_norm_post", "ffn_norm_pre", "mix_norm_post", "mix_norm_pre"):
        g[n] = g[n][0]
    return dx, g, got


MESH_AXES = ("x", "y", "c")


class Xfer:
    def __init__(self, arr, kind):
        self.arr, self.kind = arr, kind
        shp = arr.shape
        if kind == "all":
            self.out = (N_DEV,) + shp
        elif kind == "slot":
            self.out = shp
        elif kind == "rows":
            self.r = shp[1] // N_DEV
            self.out = (N_DEV, shp[0], self.r, shp[2])
        else:
            self.r = shp[1]
            self.out = (shp[0], N_DEV * shp[1], shp[2])

    def src(self, ref, peer):
        if self.kind == "slot":
            return ref.at[peer]
        if self.kind == "rows":
            return ref.at[:, pl.ds(peer * self.r, self.r), :]
        return ref

    def dst(self, ref, me):
        if self.kind == "place":
            return ref.at[:, pl.ds(me * self.r, self.r), :]
        return ref.at[me]


class Exchange:
    def __init__(self, items):
        n = len(items)
        self.items = items
        self.arrays = [it.arr for it in items]
        self.specs = [pl.BlockSpec(memory_space=pl.ANY)] * n
        self.out_shape = [jax.ShapeDtypeStruct(it.out, it.arr.dtype) for it in items]
        self.scratch = [pltpu.SemaphoreType.DMA((n * (N_DEV - 1),)), pltpu.SemaphoreType.DMA((n * (N_DEV - 1),)),
                        pltpu.SemaphoreType.DMA((n,))]

    def copies(self, ins, outs, sems):
        send_sems, recv_sems, local_sems = sems
        x, y, c = (lax.axis_index(a) for a in MESH_AXES)
        me = 4 * x + 2 * y + c
        out = []
        for t, it in enumerate(self.items):
            out.append(pltpu.make_async_copy(it.src(ins[t], me), it.dst(outs[t], me), local_sems.at[t]))
            for k in range(1, N_DEV):
                px, py, pc = x ^ ((k >> 2) & 1), y ^ ((k >> 1) & 1), c ^ (k & 1)
                s = t * (N_DEV - 1) + k - 1
                out.append(pltpu.make_async_remote_copy(
                    src_ref=it.src(ins[t], 4 * px + 2 * py + pc), dst_ref=it.dst(outs[t], me),
                    send_sem=send_sems.at[s], recv_sem=recv_sems.at[s], device_id=(px, py, pc),
                    device_id_type=pl.DeviceIdType.MESH))
        return out

    def start(self, ins, outs, sems):
        for cp in self.copies(ins, outs, sems):
            cp.start()

    def wait(self, ins, outs, sems):
        for cp in self.copies(ins, outs, sems):
            cp.wait()


def _exchange(items, name):
    ex = Exchange(items)
    n = len(items)

    def body(*refs):
        ex.start(refs[:n], refs[n:2 * n], refs[2 * n:])
        ex.wait(refs[:n], refs[n:2 * n], refs[2 * n:])

    return pl.pallas_call(body, name=name, out_shape=ex.out_shape, in_specs=ex.specs, out_specs=ex.specs,
                          scratch_shapes=ex.scratch)(*ex.arrays)


def _call(kern, arrays, ride, edges, *, name, grid, in_specs, out_specs, out_shape, scratch_shapes):
    params = _ARB(len(grid))
    if ride is None:
        return pl.pallas_call(kern, name=name, grid=grid, in_specs=in_specs, out_specs=out_specs, out_shape=out_shape,
                              scratch_shapes=scratch_shapes, compiler_params=params)(*arrays), None
    ni, no, ns, nx = len(in_specs), len(out_specs), len(scratch_shapes), len(ride.items)

    def wrapped(*refs):
        ins, xin = refs[:ni], refs[ni:ni + nx]
        outs, xout = refs[ni + nx:ni + nx + no], refs[ni + nx + no:ni + 2 * nx + no]
        scr, sems = refs[ni + 2 * nx + no:ni + 2 * nx + no + ns], refs[ni + 2 * nx + no + ns:]
        first, last = edges()

        @pl.when(first)
        def _():
            ride.start(xin, xout, sems)

        kern(*ins, *outs, *scr)

        @pl.when(last)
        def _():
            ride.wait(xin, xout, sems)

    res = pl.pallas_call(
        wrapped, name=name, grid=grid, in_specs=list(in_specs) + ride.specs, out_specs=list(out_specs) + ride.specs,
        out_shape=list(out_shape) + ride.out_shape, scratch_shapes=list(scratch_shapes) + ride.scratch,
        compiler_params=params)(*arrays, *ride.arrays)
    return res[:no], res[no:]


def _column_segments(width, permuted):
    z0, z1, zn = 4 * GROUP_W, 4 * GROUP_W + 2 * GLA_RANK, 12 * GROUP_W
    segs = []
    for d in range(N_DEV):
        lo, hi = d * width, (d + 1) * width
        if not permuted:
            segs.append([(0, width, lo)])
            continue
        runs = []
        for a, b, shift in ((0, z0, 0), (z0, z1, zn - z0), (z1, 10 ** 9, -(z1 - z0))):
            s, e = max(lo, a), min(hi, b)
            if s < e:
                runs.append((s - lo, e - lo, s + shift))
        segs.append(runs)
    return segs


def _cols_from_pieces(pieces, segs, cols, name):
    _, R, w = pieces.shape
    tm = _tile(R, 256, 16)
    used = max(f + (b - a) for runs in segs for a, b, f in runs)

    def kern(p_ref, o_ref):
        for d, runs in enumerate(segs):
            for a, b, f in runs:
                o_ref[:, f:f + (b - a)] = p_ref[d, :, a:b]
        if used < cols:
            o_ref[:, used:cols] = jnp.zeros((tm, cols - used), o_ref.dtype)

    return pl.pallas_call(
        kern, name=name, grid=(R // tm,), in_specs=[pl.BlockSpec((N_DEV, tm, w), lambda i: (0, i, 0))],
        out_specs=pl.BlockSpec((tm, cols), lambda i: (i, 0)), out_shape=jax.ShapeDtypeStruct((R, cols), pieces.dtype),
        compiler_params=_ARB(1),
    )(pieces)


def _pieces_from_cols(full, segs, w, name):
    R, cols = full.shape
    tm = _tile(R, 256, 16)

    def kern(f_ref, o_ref):
        for d, runs in enumerate(segs):
            for a, b, f in runs:
                o_ref[d, :, a:b] = f_ref[:, f:f + (b - a)]

    return pl.pallas_call(
        kern, name=name, grid=(R // tm,), in_specs=[pl.BlockSpec((tm, cols), lambda i: (i, 0))],
        out_specs=pl.BlockSpec((N_DEV, tm, w), lambda i: (0, i, 0)),
        out_shape=jax.ShapeDtypeStruct((N_DEV, R, w), full.dtype), compiler_params=_ARB(1),
    )(full)


def _sum_slots(recv, name):
    n, R, C = recv.shape
    tm = _tile(R, 256, 16)

    def kern(*refs):
        acc = refs[0][...].astype(F32)
        for r in refs[1:n]:
            acc = acc + r[...].astype(F32)
        refs[n][...] = acc

    return pl.pallas_call(
        kern, name=name, grid=(R // tm,),
        in_specs=[pl.BlockSpec((None, tm, C), lambda i, _s=s: (_s, i, 0)) for s in range(n)],
        out_specs=pl.BlockSpec((tm, C), lambda i: (i, 0)), out_shape=jax.ShapeDtypeStruct((R, C), F32),
        compiler_params=_ARB(1),
    )(*([recv] * n))


BIG = (("w_in", 2), ("w_out", 1), ("ffn_w_in", 2), ("ffn_w_out", 1))
SMALL_SHARDED = ("gla_w_gate", "gla_b_gate", "lru_conv_w", "lru_b_a", "lru_b_x", "lru_lambda")
REPLICATED = ("mix_norm_pre", "mix_norm_post", "gla_norm", "na_rpb", "lru_conv_b", "lru_w_a", "lru_w_x",
              "ffn_norm_pre", "ffn_norm_post")
WEIGHTS = ("mix_norm_pre", "mix_norm_post", "w_in", "gla_w_gate", "gla_b_gate", "gla_norm", "na_rpb", "lru_conv_w",
           "lru_conv_b", "lru_w_a", "lru_b_a", "lru_w_x", "lru_b_x", "lru_lambda", "w_out", "ffn_norm_pre",
           "ffn_norm_post", "ffn_w_in", "ffn_w_out")
FLAT_C = 1024


def _to_rows(vec, row_unit):
    n = vec.shape[-1]
    rows = -(-n // (FLAT_C * row_unit)) * row_unit
    pad = [(0, 0)] * (vec.ndim - 1) + [(0, rows * FLAT_C - n)]
    return jnp.pad(vec, pad).reshape(vec.shape[:-1] + (rows, FLAT_C))


def _unshard(parts, axis):
    t = jnp.moveaxis(parts, 0, axis)
    shp = list(t.shape)
    return t.reshape(shp[:axis] + [shp[axis] * shp[axis + 1]] + shp[axis + 2:])


def _shards(full, axis):
    shp = list(full.shape)
    t = full.reshape(shp[:axis] + [N_DEV, shp[axis] // N_DEV] + shp[axis + 1:])
    return jnp.moveaxis(t, axis, 0)


def _weight_rides(W, l):
    bf = lambda n: W[n][l].astype(BF16)
    ffn = bf("ffn_w_in")
    half = ffn.shape[0] // 2
    return {"gla": [Xfer(bf("w_in"), "all"), Xfer(bf("w_out")[None], "place")],
            "na": [Xfer(bf("ffn_w_out")[None], "place")],
            "dil1": [Xfer(ffn[:half], "all")], "dil4": [Xfer(ffn[half:], "all")]}


def _unpack_weights(full, W, got):
    w_in_w, ffn_w = W["w_in"].shape[-1], W["ffn_w_in"].shape[-1]
    full["w_in"].append(_cols_from_pieces(got["gla"][0], _column_segments(w_in_w, True), P_COLS, "unpack_w_in"))
    full["w_out"].append(got["gla"][1][0])
    full["ffn_w_out"].append(got["na"][0][0])
    full["ffn_w_in"].append(jnp.concatenate(
        [_cols_from_pieces(got[c][0], _column_segments(ffn_w, False), N_DEV * ffn_w, "unpack_ffn_w_in")
         for c in ("dil1", "dil4")], axis=0))


def _grad_rides(g, W):
    w_in_w, ffn_w = W["w_in"].shape[-1], W["ffn_w_in"].shape[-1]
    p_in = _pieces_from_cols(g["w_in"], _column_segments(w_in_w, True), w_in_w, "pack_w_in")
    p_ffn = _pieces_from_cols(g["ffn_w_in"], _column_segments(ffn_w, False), ffn_w, "pack_ffn_w_in")
    half = p_ffn.shape[1] // 2
    return {"gla": [Xfer(p_in, "slot"), Xfer(g["w_out"][None], "rows")],
            "na": [Xfer(g["ffn_w_out"][None], "rows"), Xfer(p_ffn[:, :half], "slot")],
            "dil1": [Xfer(p_ffn[:, half:], "slot")]}


def _sum_big(got):
    s = lambda r, n: _sum_slots(r.reshape(N_DEV, -1, r.shape[-1]), "sum_" + n)
    return {"w_in": s(got["gla"][0], "w_in"), "w_out": s(got["gla"][1], "w_out"), "ffn_w_out": s(got["na"][0], "ffn_w_out"),
            "ffn_w_in": jnp.concatenate([s(got["na"][1], "ffn_w_in"), s(got["dil1"][0], "ffn_w_in")], axis=0)}


def _exchange_named(rides, extra, name):
    names = list(rides)
    res = _exchange([it for n in names for it in rides[n]] + extra, name)
    got, at = {}, 0
    for n in names:
        got[n] = res[at:at + len(rides[n])]
        at += len(rides[n])
    return got, res[at:]


def _train(x, target, W):
    L = x.shape[0]
    depth = W["w_in"].shape[0]
    cosf, sinf = _rope_tables(L)
    small = jnp.concatenate([W[n].reshape(-1) for n in SMALL_SHARDED])
    small16 = _to_rows(lax.bitcast_convert_type(small, jnp.uint16).reshape(-1), 16)
    got, (sm,) = _exchange_named(_weight_rides(W, 0), [Xfer(small16, "all")], "gather_first")
    full = dict(W, w_in=[], w_out=[], ffn_w_in=[], ffn_w_out=[])
    _unpack_weights(full, W, got)
    sm = lax.bitcast_convert_type(sm.reshape(N_DEV, -1)[:, :2 * small.size].reshape(N_DEV, small.size, 2), F32)
    off = 0
    for n in SMALL_SHARDED:
        full[n] = _unshard(sm[:, off:off + W[n].size].reshape((N_DEV,) + W[n].shape), W[n].ndim - 1)
        off += W[n].size

    saved = []
    for l in range(depth):
        rides = {c: Exchange(it) for c, it in _weight_rides(W, l + 1).items()} if l + 1 < depth else {}
        x, S, got = _layer_fwd(x, full, l, cosf, sinf, rides)
        saved.append(S)
        if l + 1 < depth:
            _unpack_weights(full, W, got)
    loss, dx = _loss_fwd_bwd(x, target)

    grads, big, rides = [None] * depth, [None] * depth, {}
    for l in reversed(range(depth)):
        dx, grads[l], got = _layer_bwd(dx, full, l, saved[l], cosf, sinf, rides)
        if l + 1 < depth:
            big[l + 1] = _sum_big(got)
        rides = {c: Exchange(it) for c, it in _grad_rides(grads[l], W).items()} if l > 0 else {}
    G = {n: jnp.stack([g[n] for g in grads]) for n in SMALL_SHARDED + REPLICATED}
    small_g = jnp.concatenate([_shards(G[n], G[n].ndim - 1).reshape(N_DEV, -1) for n in SMALL_SHARDED], axis=1)
    repl_g = jnp.concatenate([G[n].reshape(-1) for n in REPLICATED])
    got, rest = _exchange_named(_grad_rides(grads[0], W), [Xfer(_to_rows(small_g, 8), "slot"),
                                                            Xfer(_to_rows(repl_g, 8), "all")], "exchange_last")
    big[0] = _sum_big(got)
    out = {n: jnp.stack([b[n] for b in big]).reshape(W[n].shape) for n, _ in BIG}
    for names, r, tag in ((SMALL_SHARDED, rest[0], "sum_small"), (REPLICATED, rest[1], "sum_replicated")):
        flat, off = _sum_slots(r, tag).reshape(-1), 0
        for n in names:
            out[n] = flat[off:off + W[n].size].reshape(W[n].shape)
            off += W[n].size
    return loss, dx, out


def _update(W, G, M, V):
    delta, new_m, new_v = {}, {}, {}
    for n, _ in BIG:
        two_d = lambda a: a.reshape(-1, a.shape[-1])
        d, m, v = _adamw(two_d(W[n]), two_d(G[n]), two_d(M[n]), two_d(V[n]), "adamw_" + n)
        delta[n], new_m[n], new_v[n] = (t.reshape(W[n].shape) for t in (d, m, v))
    rest = SMALL_SHARDED + REPLICATED
    pack = lambda D: _to_rows(jnp.concatenate([D[n].reshape(-1) for n in rest]), 16)
    d, m, v = _adamw(pack(W), pack(G), pack(M), pack(V), "adamw_small")
    off = 0
    for n in rest:
        sl = lambda t: t.reshape(-1)[off:off + W[n].size].reshape(W[n].shape)
        delta[n], new_m[n], new_v[n] = sl(d), sl(m), sl(v)
        off += W[n].size
    return delta, new_m, new_v


def kernel(x, mix_norm_pre, mix_norm_post, w_in, gla_w_gate, gla_b_gate, gla_norm, na_rpb, lru_conv_w, lru_conv_b, lru_w_a, lru_b_a, lru_w_x, lru_b_x, lru_lambda, w_out, ffn_norm_pre, ffn_norm_post, ffn_w_in, ffn_w_out, loss_target, m_mix_norm_pre, m_mix_norm_post, m_w_in, m_gla_w_gate, m_gla_b_gate, m_gla_norm, m_na_rpb, m_lru_conv_w, m_lru_conv_b, m_lru_w_a, m_lru_b_a, m_lru_w_x, m_lru_b_x, m_lru_lambda, m_w_out, m_ffn_norm_pre, m_ffn_norm_post, m_ffn_w_in, m_ffn_w_out, v_mix_norm_pre, v_mix_norm_post, v_w_in, v_gla_w_gate, v_gla_b_gate, v_gla_norm, v_na_rpb, v_lru_conv_w, v_lru_conv_b, v_lru_w_a, v_lru_b_a, v_lru_w_x, v_lru_b_x, v_lru_lambda, v_w_out, v_ffn_norm_pre, v_ffn_norm_post, v_ffn_w_in, v_ffn_w_out):
    W = dict(zip(WEIGHTS, (mix_norm_pre, mix_norm_post, w_in, gla_w_gate, gla_b_gate, gla_norm, na_rpb, lru_conv_w, lru_conv_b, lru_w_a, lru_b_a, lru_w_x, lru_b_x, lru_lambda, w_out, ffn_norm_pre, ffn_norm_post, ffn_w_in, ffn_w_out)))
    M = dict(zip(WEIGHTS, (m_mix_norm_pre, m_mix_norm_post, m_w_in, m_gla_w_gate, m_gla_b_gate, m_gla_norm, m_na_rpb, m_lru_conv_w, m_lru_conv_b, m_lru_w_a, m_lru_b_a, m_lru_w_x, m_lru_b_x, m_lru_lambda, m_w_out, m_ffn_norm_pre, m_ffn_norm_post, m_ffn_w_in, m_ffn_w_out)))
    V = dict(zip(WEIGHTS, (v_mix_norm_pre, v_mix_norm_post, v_w_in, v_gla_w_gate, v_gla_b_gate, v_gla_norm, v_na_rpb, v_lru_conv_w, v_lru_conv_b, v_lru_w_a, v_lru_b_a, v_lru_w_x, v_lru_b_x, v_lru_lambda, v_w_out, v_ffn_norm_pre, v_ffn_norm_post, v_ffn_w_in, v_ffn_w_out)))
    loss, dx, G = _train(x[0], loss_target[0], W)
    loss = lax.psum(loss, MESH_AXES)
    delta, new_m, new_v = _update(W, G, M, V)
    return (loss, dx[None], *[G[n] for n in WEIGHTS], *[delta[n] for n in WEIGHTS], *[new_m[n] for n in WEIGHTS],
            *[new_v[n] for n in WEIGHTS])
```

```python
import functools
import math

import numpy as np
import jax
import jax.numpy as jnp
from jax import lax
from jax.experimental import pallas as pl
from jax.experimental.pallas import tpu as pltpu

F32 = jnp.float32
BF16 = jnp.bfloat16

N_DEV = 8
HEAD_DIM = 64
GROUP_W = 256
GLA_RANK = 16
GLA_TAU = 16.0
GLA_CHUNK = 64
GRID_W = 64
NA_ROWS = 8
NA_COLS = 16
LRU_C = 8.0
DIL_PAIRS = ((128, 1), (512, 4), (2048, 16))
DIL_RADIUS = 64
ROPE_THETA = 10000.0
EPS = 1e-6
ATT_SCALE = HEAD_DIM ** -0.5
NEG = -1e30
LANES = 128
P_COLS = 12 * GROUP_W + LANES
Z_BLOCK = 12 * GROUP_W // LANES

ADAM_LR = 0.001
ADAM_B1 = 0.9
ADAM_B2 = 0.999
ADAM_EPS = 1e-08
ADAM_WD = 0.01
ADAM_STEP = 10

VMEM_LIMIT = 56 * 1024 * 1024
_ARB = lambda n: pltpu.CompilerParams(dimension_semantics=("arbitrary",) * n, vmem_limit_bytes=VMEM_LIMIT)


def _tile(dim, pref, unit):
    t = min(pref, dim) // unit * unit
    while t >= unit:
        if dim % t == 0:
            return t
        t -= unit
    return dim


def _mm(a, b, mode, out_dtype, name, tm=512, tn=None, tk=None):
    if mode == "nn":
        (M, K), (_, N) = a.shape, b.shape
    elif mode == "nt":
        (M, K), (N, _) = a.shape, b.shape
    else:
        (K, M), (_, N) = a.shape, b.shape
    tm = _tile(M, tm, LANES if mode == "tn" else 8)
    tn = _tile(N, tn or N, LANES)
    tk = _tile(K, tk or K, LANES)
    nk = K // tk
    dims = {"nn": (((1,), (0,)), ((), ())), "nt": (((1,), (1,)), ((), ())), "tn": (((0,), (0,)), ((), ()))}[mode]

    def kern(a_ref, b_ref, o_ref, *acc):
        part = lax.dot_general(a_ref[...].astype(BF16), b_ref[...].astype(BF16), dims, preferred_element_type=F32)
        if nk == 1:
            o_ref[...] = part.astype(out_dtype)
            return
        k = pl.program_id(2)

        @pl.when(k == 0)
        def _():
            acc[0][...] = part

        @pl.when(jnp.logical_and(k > 0, k < nk - 1))
        def _():
            acc[0][...] += part

        @pl.when(k == nk - 1)
        def _():
            o_ref[...] = (acc[0][...] + part).astype(out_dtype)

    a_spec = pl.BlockSpec((tk, tm), lambda i, j, k: (k, i)) if mode == "tn" else pl.BlockSpec((tm, tk), lambda i, j, k: (i, k))
    b_spec = pl.BlockSpec((tn, tk), lambda i, j, k: (j, k)) if mode == "nt" else pl.BlockSpec((tk, tn), lambda i, j, k: (k, j))
    return pl.pallas_call(
        kern, name=name, grid=(M // tm, N // tn, nk),
        in_specs=[a_spec, b_spec], out_specs=pl.BlockSpec((tm, tn), lambda i, j, k: (i, j)),
        out_shape=jax.ShapeDtypeStruct((M, N), out_dtype),
        scratch_shapes=[pltpu.VMEM((tm, tn), F32)] if nk > 1 else [],
        compiler_params=_ARB(3),
    )(a, b)


class Row:
    def __init__(self, a, width=None, cb=0, halo=False, dil=1):
        self.a, self.width, self.cb, self.halo, self.dil = a, width, cb, halo, dil


class Full:
    def __init__(self, a):
        self.a = a


HALO = 8


def _rows(name, body, tm, ins, outs):
    outs = [o if len(o) == 4 else o + (1,) for o in outs]
    L = next(s.a.shape[0] * s.dil for s in ins if isinstance(s, Row))
    tm = _tile(L, tm, 16)
    dilated = any(s.dil > 1 for s in ins if isinstance(s, Row)) or any(o[3] > 1 for o in outs)
    nt = L // tm
    nb8 = L // HALO
    step = tm // HALO
    in_specs, arrays, layout = [], [], []
    for s in ins:
        if isinstance(s, Full):
            nd = s.a.ndim
            in_specs.append(pl.BlockSpec(s.a.shape, lambda i, _nd=nd: (0,) * _nd))
            arrays.append(s.a)
            layout.append(1)
        else:
            w = s.width or s.a.shape[1]
            in_specs.append(pl.BlockSpec((tm // s.dil, w), lambda i, _cb=s.cb: (i, _cb)))
            arrays.append(s.a)
            if s.dil > 1:
                layout.append(-s.dil)
            elif s.halo:
                in_specs.append(pl.BlockSpec((HALO, w), lambda i, _cb=s.cb: (jnp.maximum(i * step - 1, 0), _cb)))
                in_specs.append(pl.BlockSpec((HALO, w), lambda i, _cb=s.cb: (jnp.minimum((i + 1) * step, nb8 - 1), _cb)))
                arrays += [s.a, s.a]
                layout.append(3)
            else:
                layout.append(1)
    out_specs, out_shapes = [], []
    for kind, shp, dt, dil in outs:
        if kind == "row":
            out_specs.append(pl.BlockSpec((tm // dil, dil * shp), lambda i: (i, 0)))
            out_shapes.append(jax.ShapeDtypeStruct((L // dil, dil * shp), dt))
        else:
            out_specs.append(pl.BlockSpec(shp, lambda i, _n=len(shp): (0,) * _n))
            out_shapes.append(jax.ShapeDtypeStruct(shp, dt))
    n_in, n_out = len(arrays), len(outs)

    def kern(*refs):
        i = pl.program_id(0)
        lo, hi = refs[n_in + n_out:] if dilated else (None, None)

        def undilate(ref, d):
            for j in range(d):
                rows = pl.ds(j, tm // d, stride=d)
                lo[rows, :] = ref[:, j * GROUP_W:j * GROUP_W + LANES].astype(F32)
                hi[rows, :] = ref[:, j * GROUP_W + LANES:(j + 1) * GROUP_W].astype(F32)
            return jnp.concatenate([lo[...], hi[...]], axis=1)

        def dilate(val, ref, d, dt):
            lo[...] = val[:, :LANES].astype(F32)
            hi[...] = val[:, LANES:].astype(F32)
            for j in range(d):
                rows = pl.ds(j, tm // d, stride=d)
                ref[:, j * GROUP_W:j * GROUP_W + LANES] = lo[rows, :].astype(dt)
                ref[:, j * GROUP_W + LANES:(j + 1) * GROUP_W] = hi[rows, :].astype(dt)

        vals, p = [], 0
        for n in layout:
            if n == 1:
                vals.append(refs[p][...])
            elif n < 0:
                vals.append(undilate(refs[p], -n))
                n = 1
            else:
                vals.append((refs[p + 1][...], refs[p][...], refs[p + 2][...]))
            p += n
        res = body(i, nt, *vals)
        if not isinstance(res, (tuple, list)):
            res = (res,)
        for (kind, shp, dt, dil), o_ref, r in zip(outs, refs[n_in:], res):
            if kind == "row" and dil > 1:
                dilate(r, o_ref, dil, dt)
            elif kind == "row":
                o_ref[...] = r.astype(dt)
            else:
                @pl.when(i == 0)
                def _(o_ref=o_ref):
                    o_ref[...] = jnp.zeros_like(o_ref)
                o_ref[...] += r.astype(dt)

    res = pl.pallas_call(
        kern, name=name, grid=(nt,), in_specs=in_specs, out_specs=out_specs, out_shape=out_shapes,
        scratch_shapes=[pltpu.VMEM((tm, LANES), F32)] * 2 if dilated else [], compiler_params=_ARB(1),
    )(*arrays)
    return res


def _shift(h, o, i, nt):
    prev, cur, nxt = h
    if o == 0:
        return cur
    tm = cur.shape[0]
    cat = jnp.concatenate([prev, cur, nxt], axis=0)
    sh = pltpu.roll(cat, (-o) % (tm + 2 * HALO), axis=0)[HALO:HALO + tm]
    row = lax.broadcasted_iota(jnp.int32, cur.shape, 0)
    if o < 0:
        ok = jnp.logical_or(i > 0, row >= -o)
    else:
        ok = jnp.logical_or(i < nt - 1, row < tm - o)
    return jnp.where(ok, sh, 0.0)


def _colsum(v):
    return jnp.sum(v, axis=0, keepdims=True)


def _sigmoid(x):
    return 1.0 / (1.0 + jnp.exp(-x))


def _softplus(x):
    return jnp.maximum(x, 0.0) + jnp.log1p(jnp.exp(-jnp.abs(x)))


def _silu(x):
    return x * _sigmoid(x)


def _dsilu(x):
    s = _sigmoid(x)
    return s * (1.0 + x * (1.0 - s))


_GELU_C = math.sqrt(2.0 / math.pi)


def _gelu(x):
    return 0.5 * x * (1.0 + jnp.tanh(_GELU_C * (x + 0.044715 * x * x * x)))


def _dgelu(x):
    t = jnp.tanh(_GELU_C * (x + 0.044715 * x * x * x))
    return 0.5 * (1.0 + t) + 0.5 * x * (1.0 - t * t) * _GELU_C * (1.0 + 3.0 * 0.044715 * x * x)


def _head_sum(v, bd):
    return jnp.dot(v, bd, precision=lax.Precision.HIGHEST, preferred_element_type=F32)


def _block_ones(n, blk):
    r = np.arange(n)
    return jnp.asarray((r[:, None] // blk == r[None, :] // blk).astype(np.float32))


def _rms_fwd(x, g, name):
    def body(i, nt, x, g):
        r = lax.rsqrt(jnp.mean(x * x, axis=-1, keepdims=True) + EPS)
        return x * r * g
    return _rows(name, body, 256, [Row(x), Full(g)], [("row", x.shape[1], BF16)])[0]


def _rms_resid_fwd(x, y, g, name):
    def body(i, nt, x, y, g):
        r = lax.rsqrt(jnp.mean(y * y, axis=-1, keepdims=True) + EPS)
        return x + y * r * g
    return _rows(name, body, 256, [Row(x), Row(y), Full(g)], [("row", x.shape[1], F32)])[0]


def _rms_bwd(dy, x, g, name, resid=None, out_dtype=F32):
    D = x.shape[1]

    def body(i, nt, dy, x, g, *rest):
        dy = dy.astype(F32)
        r = lax.rsqrt(jnp.mean(x * x, axis=-1, keepdims=True) + EPS)
        xh = x * r
        dxh = dy * g
        dx = r * (dxh - xh * jnp.mean(dxh * xh, axis=-1, keepdims=True))
        if rest:
            dx = dx + rest[0]
        return dx, _colsum(dy * xh)

    ins = [Row(dy), Row(x), Full(g)] + ([Row(resid)] if resid is not None else [])
    return _rows(name, body, 256, ins, [("row", D, out_dtype), ("acc", (1, D), F32)])


def _ffn_in_swiglu(h, w):
    (M, K), N = h.shape, w.shape[1]
    F = N // 2
    tm = _tile(M, 256, 16)

    def kern(a_ref, b_ref, gu_ref, act_ref):
        gu = _dot(a_ref[...], b_ref[...])
        gu_ref[...] = gu
        act_ref[...] = (_silu(gu[:, :F]) * gu[:, F:]).astype(BF16)

    return pl.pallas_call(
        kern, name="ffn_in_swiglu", grid=(M // tm,),
        in_specs=[pl.BlockSpec((tm, K), lambda i: (i, 0)), pl.BlockSpec((K, N), lambda i: (0, 0))],
        out_specs=[pl.BlockSpec((tm, N), lambda i: (i, 0)), pl.BlockSpec((tm, F), lambda i: (i, 0))],
        out_shape=[jax.ShapeDtypeStruct((M, N), F32), jax.ShapeDtypeStruct((M, F), BF16)], compiler_params=_ARB(1),
    )(h, w)


def _ffn_out_dx_swiglu(df, w, gu):
    (M, K), N = df.shape, gu.shape[1]
    F = N // 2
    tm = _tile(M, 256, 16)

    def kern(a_ref, b_ref, gu_ref, o_ref):
        da = _dot(a_ref[...], b_ref[...], _NT)
        gu = gu_ref[...]
        gate, up = gu[:, :F], gu[:, F:]
        o_ref[:, :F] = (da * up * _dsilu(gate)).astype(BF16)
        o_ref[:, F:] = (da * _silu(gate)).astype(BF16)

    return pl.pallas_call(
        kern, name="ffn_out_dx_swiglu", grid=(M // tm,),
        in_specs=[pl.BlockSpec((tm, K), lambda i: (i, 0)), pl.BlockSpec((F, K), lambda i: (0, 0)),
                  pl.BlockSpec((tm, N), lambda i: (i, 0))],
        out_specs=pl.BlockSpec((tm, N), lambda i: (i, 0)), out_shape=jax.ShapeDtypeStruct((M, N), BF16),
        compiler_params=_ARB(1),
    )(df, w, gu)


def _loss_fwd_bwd(y, target):
    D = y.shape[1]

    def body(i, nt, y, t):
        err = y - t
        part = 0.5 * jnp.sum(jnp.mean(err * err, axis=-1, keepdims=True), axis=0, keepdims=True)
        return err * (1.0 / D), jnp.broadcast_to(part, (1, LANES))
    dy, loss = _rows("loss", body, 256, [Row(y), Row(target)], [("row", D, F32), ("acc", (1, LANES), F32)])
    return loss[0, 0], dy


def _adamw(w, g, m, v, name):
    C = w.shape[1]
    bc1 = 1.0 - ADAM_B1 ** ADAM_STEP
    bc2 = 1.0 - ADAM_B2 ** ADAM_STEP

    def body(i, nt, w, g, m, v):
        m = ADAM_B1 * m + (1.0 - ADAM_B1) * g
        v = ADAM_B2 * v + (1.0 - ADAM_B2) * (g * g)
        delta = -ADAM_LR * ((m / bc1) / (jnp.sqrt(v / bc2) + ADAM_EPS) + ADAM_WD * w)
        return delta, m, v
    return _rows(name, body, 256, [Row(w), Row(g), Row(m), Row(v)], [("row", C, F32)] * 3)


def _expm1(x):
    return jnp.tanh(0.5 * x) * (jnp.exp(x) + 1.0)


def _lru_gates(xh, i, nt, cw, cb, wa, wx, ba, bx, lam):
    xc = cb
    for j in range(4):
        xc = xc + cw[j:j + 1] * _shift(xh, j - 2, i, nt)
    xcb = xc.astype(BF16)
    gates = []
    for e in range(2):
        r = _sigmoid(jnp.dot(xcb, wa[e], preferred_element_type=F32) + ba[e:e + 1])
        ig = _sigmoid(jnp.dot(xcb, wx[e], preferred_element_type=F32) + bx[e:e + 1])
        sp = _softplus(-lam[e:e + 1])
        la = -LRU_C * r * sp
        gates.append((r, ig, sp, jnp.exp(la), jnp.sqrt(-_expm1(2.0 * la))))
    return xc, xcb, gates


def _scan2(af, uf, ab, ub, adjoint, name):
    L, W = af.shape
    tm = _tile(L, 512, 8)
    nt, nb = L // tm, tm // 8

    def blk(A, U, h, reverse, row):
        for d in (1, 2, 4):
            if reverse:
                ok, sh = row < 8 - d, 8 - d
            else:
                ok, sh = row >= d, d
            As = jnp.where(ok, pltpu.roll(A, sh, axis=0), 1.0)
            Us = jnp.where(ok, pltpu.roll(U, sh, axis=0), 0.0)
            U = A * Us + U
            A = A * As
        return A * h + U

    def kern(af_ref, uf_ref, ab_ref, ub_ref, of_ref, ob_ref, c_ref):
        @pl.when(pl.program_id(0) == 0)
        def _():
            c_ref[...] = jnp.zeros_like(c_ref)

        row = lax.broadcasted_iota(jnp.int32, (8, W), 0)
        full = lambda v: jnp.broadcast_to(v, (8, W))

        def body(j, carry):
            hF, aF, hB, aB = carry
            r0 = pl.multiple_of(j * 8, 8)
            r1 = pl.multiple_of((nb - 1 - j) * 8, 8)
            A, U = af_ref[pl.ds(r0, 8), :], uf_ref[pl.ds(r0, 8), :]
            if adjoint:
                C = jnp.where(row == 0, aF, pltpu.roll(A, 1, axis=0))
                aF = full(A[7:8])
            else:
                C = A
            H = blk(C, U, hF, False, row)
            of_ref[pl.ds(r0, 8), :] = H
            hF = full(H[7:8])
            A, U = ab_ref[pl.ds(r1, 8), :], ub_ref[pl.ds(r1, 8), :]
            if adjoint:
                C = jnp.where(row == 7, aB, pltpu.roll(A, 7, axis=0))
                aB = full(A[0:1])
            else:
                C = A
            H = blk(C, U, hB, True, row)
            ob_ref[pl.ds(r1, 8), :] = H
            hB = full(H[0:1])
            return hF, aF, hB, aB

        carry = lax.fori_loop(0, nb, body, (c_ref[0], c_ref[1], c_ref[2], c_ref[3]))
        for n in range(4):
            c_ref[n] = carry[n]

    fwd = pl.BlockSpec((tm, W), lambda i: (i, 0))
    bwd = pl.BlockSpec((tm, W), lambda i: (nt - 1 - i, 0))
    return pl.pallas_call(
        kern, name=name, grid=(nt,), in_specs=[fwd, fwd, bwd, bwd], out_specs=[fwd, bwd],
        out_shape=[jax.ShapeDtypeStruct((L, W), F32)] * 2,
        scratch_shapes=[pltpu.VMEM((4, 8, W), F32)], compiler_params=_ARB(1),
    )(af, uf, ab, ub)


def _block_diag(w):
    rows = jnp.tile(w.reshape(2, GROUP_W, HEAD_DIM), (1, 1, 4))
    return jnp.where(_block_ones(GROUP_W, HEAD_DIM) > 0.5, rows, 0.0).astype(BF16)


def _diag_blocks(w):
    return jnp.stack([w[:, h * 64:(h + 1) * 64, h * 64:(h + 1) * 64] for h in range(4)], axis=1)


def _lru_params(W, l):
    return [Full(W["lru_conv_w"][l]), Full(W["lru_conv_b"][l][None]), Full(_block_diag(W["lru_w_a"][l])),
            Full(_block_diag(W["lru_w_x"][l])), Full(W["lru_b_a"][l]), Full(W["lru_b_x"][l]), Full(W["lru_lambda"][l])]


def _lru_fwd(p, W, l):
    def pre(i, nt, xh, *prm):
        xc, _, g = _lru_gates(xh, i, nt, *prm)
        return g[0][3], g[0][4] * (g[0][1] * xc), g[1][3], g[1][4] * (g[1][1] * xc)

    a0, u0, a1, u1 = _rows("lru_pre", pre, 256, [Row(p, GROUP_W, 7, halo=True)] + _lru_params(W, l),
                           [("row", GROUP_W, F32)] * 4)
    hf, hb = _scan2(a0, u0, a1, u1, False, "lru_scan")
    yc = _rows("lru_post", lambda i, nt, hf, hb, gc: (hf + hb) * _gelu(gc), 512,
               [Row(hf), Row(hb), Row(p, GROUP_W, 8)], [("row", GROUP_W, BF16)])[0]
    return yc, (a0, a1, hf, hb)


def _lru_bwd(dy, dy_cb, p, W, l, saved):
    a0, a1, hf, hb = saved

    def post(i, nt, dy, hf, hb, gc):
        return dy * _gelu(gc), dy * (hf + hb) * _dgelu(gc)

    dh, dgc = _rows("lru_post_bwd", post, 512, [Row(dy, GROUP_W, dy_cb), Row(hf), Row(hb), Row(p, GROUP_W, 8)],
                    [("row", GROUP_W, F32), ("row", GROUP_W, BF16)])
    gb, gf = _scan2(a1, dh, a0, dh, True, "lru_scan_adj")

    def gates_bwd(i, nt, xh, gf, gb, hfh, hbh, cw, cb, wa, wx, ba, bx, lam):
        xc, xcb, g = _lru_gates(xh, i, nt, cw, cb, wa, wx, ba, bx, lam)
        dxc = jnp.zeros_like(xc)
        dwa, dwx, dba, dbx, dlam = [], [], [], [], []
        for e, du, hprev in ((0, gf, _shift(hfh, -1, i, nt)), (1, gb, _shift(hbh, 1, i, nt))):
            r, ig, sp, a, s = g[e]
            dxc = dxc + du * s * ig
            dla = du * hprev * a - (du * ig * xc) * a * a / s
            dza = (dla * (-LRU_C) * sp) * r * (1.0 - r)
            dzx = (du * s * xc) * ig * (1.0 - ig)
            dlam.append(_colsum(dla * r) * (LRU_C * _sigmoid(-lam[e:e + 1])))
            dba.append(_colsum(dza))
            dbx.append(_colsum(dzx))
            dzab, dzxb = dza.astype(BF16), dzx.astype(BF16)
            tn = (((0,), (0,)), ((), ()))
            nt_ = (((1,), (1,)), ((), ()))
            dwa.append(lax.dot_general(xcb, dzab, tn, preferred_element_type=F32))
            dwx.append(lax.dot_general(xcb, dzxb, tn, preferred_element_type=F32))
            dxc = dxc + lax.dot_general(dzab, wa[e], nt_, preferred_element_type=F32)
            dxc = dxc + lax.dot_general(dzxb, wx[e], nt_, preferred_element_type=F32)
        cat = lambda v: jnp.concatenate(v, axis=0)
        return dxc, jnp.stack(dwa), jnp.stack(dwx), cat(dba), cat(dbx), cat(dlam)

    dxc, dwa, dwx, dba, dbx, dlam = _rows(
        "lru_gates_bwd", gates_bwd, 256,
        [Row(p, GROUP_W, 7, halo=True), Row(gf), Row(gb), Row(hf, halo=True), Row(hb, halo=True)] + _lru_params(W, l),
        [("row", GROUP_W, F32), ("acc", (2, GROUP_W, GROUP_W), F32), ("acc", (2, GROUP_W, GROUP_W), F32),
         ("acc", (2, GROUP_W), F32), ("acc", (2, GROUP_W), F32), ("acc", (2, GROUP_W), F32)])

    def conv_bwd(i, nt, dh_, xh, cw):
        dxb = jnp.zeros_like(dh_[1])
        dcw = []
        for j in range(4):
            dxb = dxb + cw[j:j + 1] * _shift(dh_, 2 - j, i, nt)
            dcw.append(_colsum(dh_[1] * _shift(xh, j - 2, i, nt)))
        return dxb, jnp.concatenate(dcw, axis=0), _colsum(dh_[1])

    dxb, dcw, dcb = _rows("lru_conv_bwd", conv_bwd, 512,
                          [Row(dxc, halo=True), Row(p, GROUP_W, 7, halo=True), Full(W["lru_conv_w"][l])],
                          [("row", GROUP_W, BF16), ("acc", (4, GROUP_W), F32), ("acc", (1, GROUP_W), F32)])
    grads = dict(lru_conv_w=dcw, lru_conv_b=dcb[0], lru_w_a=_diag_blocks(dwa), lru_w_x=_diag_blocks(dwx),
                 lru_b_a=dba, lru_b_x=dbx, lru_lambda=dlam)
    return dxb, dgc, grads


_NT = (((1,), (1,)), ((), ()))
_TN = (((0,), (0,)), ((), ()))


def _dot(a, b, dims=None):
    if dims is None:
        return jnp.dot(a, b, preferred_element_type=F32)
    return lax.dot_general(a, b, dims, preferred_element_type=F32)


def _dot_exact(a, b):
    return jnp.dot(a, b, precision=lax.Precision.HIGHEST, preferred_element_type=F32)


def _gla_gate_w(w_gate, b_gate):
    zero = jnp.zeros((GLA_RANK, GROUP_W), w_gate.dtype)
    wg = jnp.concatenate([jnp.concatenate([w_gate[0], zero], axis=1), jnp.concatenate([zero, w_gate[1]], axis=1),
                          jnp.zeros((LANES - 2 * GLA_RANK, 2 * GROUP_W), w_gate.dtype)], axis=0)
    return wg.astype(BF16), b_gate.reshape(1, 2 * GROUP_W)


def _gla_gates_fwd(p, wg, bg):
    def body(i, nt, z, wg, bg):
        logit = _dot(z.astype(BF16), wg) + bg
        la = -_softplus(-logit) * (1.0 / GLA_TAU)
        return la[:, :GROUP_W], la[:, GROUP_W:]
    return _rows("gla_gates", body, 512, [Row(p, LANES, Z_BLOCK), Full(wg), Full(bg)], [("row", GROUP_W, F32)] * 2)


def _gla_gates_bwd(p, dla0, dla1, wg, bg):
    def body(i, nt, z, d0, d1, wg, bg):
        zb = z.astype(BF16)
        logit = _dot(zb, wg) + bg
        dlogit = jnp.concatenate([d0, d1], axis=1) * (1.0 / GLA_TAU) * _sigmoid(-logit)
        dlb = dlogit.astype(BF16)
        return _dot(dlb, wg, _NT), _dot(zb, dlb, _TN), _colsum(dlogit)
    return _rows("gla_gates_bwd", body, 512, [Row(p, LANES, Z_BLOCK), Row(dla0), Row(dla1), Full(wg), Full(bg)],
                 [("row", LANES, BF16), ("acc", (LANES, 2 * GROUP_W), F32), ("acc", (1, 2 * GROUP_W), F32)])


def _gla_order(reverse):
    t = np.arange(GLA_CHUNK)
    m = (t[None, :] >= t[:, None]) if reverse else (t[None, :] <= t[:, None])
    return m.astype(np.float32), (32, 0) if reverse else (31, 63)


def _stack_heads(x, bd):
    return jnp.where(bd, jnp.concatenate([x] * 4, axis=0), 0.0)


def _diag_heads(r, bd):
    r = jnp.where(bd, r, 0.0)
    return r[0:64] + r[64:128] + r[128:192] + r[192:256]


def _gla_factors(q_ref, k_ref, rows, b, mid, last):
    bm, bl = b[mid:mid + 1], b[last:last + 1]
    qs = q_ref[rows, :] * ATT_SCALE
    k = k_ref[rows, :]
    P, N, E, Fd = jnp.exp(b - bm), jnp.exp(bm - b), jnp.exp(b), jnp.exp(bl - b)
    return (P, N, E, Fd, jnp.exp(bl)), (qs * P, k * N, qs * E, k * Fd)


def _gla_specs(L, walk_up):
    tm = _tile(L, 512, GLA_CHUNK)
    nt, nc = L // tm, tm // GLA_CHUNK
    specs = []
    for up in walk_up:
        t = (lambda i: i) if up else (lambda i: nt - 1 - i)
        specs.append(dict(
            col=lambda cb, _t=t: pl.BlockSpec((tm, GROUP_W), lambda i: (_t(i), cb)),
            row=pl.BlockSpec((tm, GROUP_W), lambda i, _t=t: (_t(i), 0)),
            state=pl.BlockSpec((nc, GROUP_W, GROUP_W), lambda i, _t=t: (_t(i), 0, 0))))
    return nt, nc, specs


def _gla_chunk_fwd(p, la0, la1, ride=None):
    L = la0.shape[0]
    nt, nc, specs = _gla_specs(L, (True, False))
    orders = [_gla_order(False), _gla_order(True)]

    def kern(q0, k0, v0, l0, q1, k1, v1, l1, m0_ref, m1_ref, bd_ref, o0, s0, o1, s1, st_ref):
        @pl.when(pl.program_id(0) == 0)
        def _():
            st_ref[...] = jnp.zeros_like(st_ref)

        bd = bd_ref[...] > 0.5
        dirs = []
        for e, (q_ref, k_ref, v_ref, la_ref, m_ref, o_ref, s_ref) in enumerate(
                ((q0, k0, v0, l0, m0_ref, o0, s0), (q1, k1, v1, l1, m1_ref, o1, s1))):
            mv = m_ref[...]
            dirs.append((q_ref, k_ref, v_ref, la_ref, mv, jnp.concatenate([mv] * 4, axis=0) > 0.5, o_ref, s_ref))

        def body(cc, carry):
            E = range(2)
            cs = [nc - 1 - cc if e else cc for e in E]
            rows = [pl.ds(pl.multiple_of(c * GLA_CHUNK, GLA_CHUNK), GLA_CHUNK) for c in cs]
            b = [_dot_exact(dirs[e][4], dirs[e][3][rows[e], :]) for e in E]
            t = [_gla_factors(dirs[e][0], dirs[e][1], rows[e], b[e], *orders[e][1]) for e in E]
            vb = [dirs[e][2][rows[e], :].astype(BF16) for e in E]
            st = [st_ref[e] for e in E]
            a = [_dot(_stack_heads(t[e][1][0], bd).astype(BF16), t[e][1][1].astype(BF16), _NT) for e in E]
            inter = [_dot(t[e][1][2].astype(BF16), st[e].astype(BF16), _NT) for e in E]
            kv = [_dot(vb[e], t[e][1][3].astype(BF16), _TN) for e in E]
            a = [jnp.where(dirs[e][5], a[e], 0.0).astype(BF16) for e in E]
            r = [_dot(a[e], vb[e]) for e in E]
            for e in E:
                dirs[e][7][cs[e]] = st[e]
                dirs[e][6][rows[e], :] = _diag_heads(r[e], bd) + inter[e]
                st_ref[e] = st[e] * t[e][0][4] + jnp.where(bd, kv[e], 0.0)
            return carry

        lax.fori_loop(0, nc, body, 0)

    const = lambda shp: pl.BlockSpec(shp, lambda i: (0, 0))
    in_specs, out_specs = [], []
    for sp in specs:
        in_specs += [sp["col"](0), sp["col"](1), sp["col"](2), sp["row"]]
        out_specs += [sp["row"], sp["state"]]
    return _call(
        kern, (p, p, p, la0, p, p, p, la1, jnp.asarray(orders[0][0]), jnp.asarray(orders[1][0]),
               _block_ones(GROUP_W, HEAD_DIM)),
        ride, lambda: (pl.program_id(0) == 0, pl.program_id(0) == nt - 1), name="gla_fwd", grid=(nt,),
        in_specs=in_specs + [const((GLA_CHUNK, GLA_CHUNK))] * 2 + [const((GROUP_W, GROUP_W))], out_specs=out_specs,
        out_shape=[jax.ShapeDtypeStruct((L, GROUP_W), F32),
                   jax.ShapeDtypeStruct((L // GLA_CHUNK, GROUP_W, GROUP_W), F32)] * 2,
        scratch_shapes=[pltpu.VMEM((2, GROUP_W, GROUP_W), F32)])


def _gla_chunk_bwd(p, la0, la1, do, sprev0, sprev1, ride=None):
    L = la0.shape[0]
    nt, nc, specs = _gla_specs(L, (False, True))
    orders = [_gla_order(False), _gla_order(True)]

    def kern(q0, k0, v0, l0, do0, s0, q1, k1, v1, l1, do1, s1, m0_ref, m1_ref, t0_ref, t1_ref, bd_ref, *rest):
        outs, dst_ref = (rest[0:4], rest[4:8]), rest[8]

        @pl.when(pl.program_id(0) == 0)
        def _():
            dst_ref[...] = jnp.zeros_like(dst_ref)

        bd = bd_ref[...] > 0.5
        row = lax.broadcasted_iota(jnp.int32, (GLA_CHUNK, GROUP_W), 0)
        dirs = []
        for ins, m_ref, t_ref in (((q0, k0, v0, l0, do0, s0), m0_ref, t0_ref), ((q1, k1, v1, l1, do1, s1), m1_ref, t1_ref)):
            mv = m_ref[...]
            dirs.append(ins + (mv, t_ref[...], jnp.concatenate([mv] * 4, axis=0) > 0.5))

        def body(cc, carry):
            E2 = range(2)
            cs = [cc if e else nc - 1 - cc for e in E2]
            rows = [pl.ds(pl.multiple_of(c * GLA_CHUNK, GLA_CHUNK), GLA_CHUNK) for c in cs]
            b = [_dot_exact(dirs[e][6], dirs[e][3][rows[e], :]) for e in E2]
            t = [_gla_factors(dirs[e][0], dirs[e][1], rows[e], b[e], *orders[e][1]) for e in E2]
            vb = [dirs[e][2][rows[e], :].astype(BF16) for e in E2]
            dov = [dirs[e][4][rows[e], :] for e in E2]
            dob = [x.astype(BF16) for x in dov]
            st = [dirs[e][5][cs[e]] for e in E2]
            dst = [dst_ref[e] for e in E2]
            stb, dstb = [x.astype(BF16) for x in st], [x.astype(BF16) for x in dst]
            qst = [_stack_heads(t[e][1][0], bd).astype(BF16) for e in E2]
            dost = [_stack_heads(dov[e], bd).astype(BF16) for e in E2]
            kNb, qEb, kFb = ([t[e][1][n].astype(BF16) for e in E2] for n in (1, 2, 3))
            a = [_dot(qst[e], kNb[e], _NT) for e in E2]
            da = [_dot(dost[e], vb[e], _NT) for e in E2]
            dqE = [_dot(dob[e], stb[e]) for e in E2]
            dkF = [_dot(vb[e], dstb[e]) for e in E2]
            dv_inter = [_dot(kFb[e], dstb[e], _NT) for e in E2]
            dst_in = [_dot(dob[e], qEb[e], _TN) for e in E2]
            a = [jnp.where(dirs[e][8], a[e], 0.0).astype(BF16) for e in E2]
            da = [jnp.where(dirs[e][8], da[e], 0.0).astype(BF16) for e in E2]
            dv_intra = [_dot(a[e], dost[e], _TN) for e in E2]
            dqP = [_dot(da[e], kNb[e]) for e in E2]
            dkN = [_dot(da[e], qst[e], _TN) for e in E2]
            db = []
            for e in E2:
                (P, N, Ef, Fd, d), (qP, kN, qE, kF) = t[e]
                mid, last = orders[e][1]
                dq_ref, dk_ref, dv_ref, _ = outs[e]
                dqp = _diag_heads(dqP[e], bd)
                dd = _colsum(dst[e] * st[e])
                dst_ref[e] = jnp.where(bd, dst_in[e], 0.0) + dst[e] * d
                tP, tN, tE, tF = dqp * qP, dkN[e] * kN, dqE[e] * qE, dkF[e] * kF
                db.append(tP - tN + tE - tF + jnp.where(row == mid, _colsum(tN - tP), 0.0)
                          + jnp.where(row == last, _colsum(tF) + dd * d, 0.0))
                dq_ref[rows[e], :] = (dqp * P + dqE[e] * Ef) * ATT_SCALE
                dk_ref[rows[e], :] = dkN[e] * N + dkF[e] * Fd
                dv_ref[rows[e], :] = dv_intra[e] + dv_inter[e]
            dla = [_dot_exact(dirs[e][7], db[e]) for e in E2]
            for e in E2:
                outs[e][3][rows[e], :] = dla[e]
            return carry

        lax.fori_loop(0, nc, body, 0)

    const = lambda shp: pl.BlockSpec(shp, lambda i: (0, 0))
    in_specs, out_specs = [], []
    for sp in specs:
        in_specs += [sp["col"](0), sp["col"](1), sp["col"](2), sp["row"], sp["row"], sp["state"]]
        out_specs += [sp["row"]] * 4
    m0, m1 = orders[0][0], orders[1][0]
    return _call(
        kern, (p, p, p, la0, do, sprev0, p, p, p, la1, do, sprev1, jnp.asarray(m0), jnp.asarray(m1),
               jnp.asarray(m0.T.copy()), jnp.asarray(m1.T.copy()), _block_ones(GROUP_W, HEAD_DIM)),
        ride, lambda: (pl.program_id(0) == 0, pl.program_id(0) == nt - 1), name="gla_bwd", grid=(nt,),
        in_specs=in_specs + [const((GLA_CHUNK, GLA_CHUNK))] * 4 + [const((GROUP_W, GROUP_W))], out_specs=out_specs,
        out_shape=[jax.ShapeDtypeStruct((L, GROUP_W), F32)] * 8,
        scratch_shapes=[pltpu.VMEM((2, GROUP_W, GROUP_W), F32)])


def _gla_fwd(p, W, l, ride=None):
    wg, bg = _gla_gate_w(W["gla_w_gate"][l], W["gla_b_gate"][l])
    la0, la1 = _gla_gates_fwd(p, wg, bg)
    (of, s0, ob, s1), got = _gla_chunk_fwd(p, la0, la1, ride)

    def post(i, nt, of, ob, g, ng, bd):
        o = of + ob
        r = lax.rsqrt(_head_sum(o * o, bd) * (1.0 / HEAD_DIM) + EPS)
        return o * r * ng * _silu(g)

    ya = _rows("gla_post", post, 512, [Row(of), Row(ob), Row(p, GROUP_W, 3), Full(W["gla_norm"][l][None]),
                                       Full(_block_ones(GROUP_W, HEAD_DIM))], [("row", GROUP_W, BF16)])[0]
    return ya, (la0, la1, of, ob, s0, s1), got


def _gla_bwd(dy, dy_cb, p, W, l, saved, ride=None):
    la0, la1, of, ob, s0, s1 = saved
    wg, bg = _gla_gate_w(W["gla_w_gate"][l], W["gla_b_gate"][l])

    def post(i, nt, dy, of, ob, g, ng, bd):
        o = of + ob
        r = lax.rsqrt(_head_sum(o * o, bd) * (1.0 / HEAD_DIM) + EPS)
        oh = o * r
        don = dy * _silu(g)
        doh = don * ng
        do = r * (doh - oh * _head_sum(doh * oh, bd) * (1.0 / HEAD_DIM))
        return do, dy * (oh * ng) * _dsilu(g), _colsum(don * oh)

    do, dg, dng = _rows("gla_post_bwd", post, 512,
                        [Row(dy, GROUP_W, dy_cb), Row(of), Row(ob), Row(p, GROUP_W, 3), Full(W["gla_norm"][l][None]),
                         Full(_block_ones(GROUP_W, HEAD_DIM))],
                        [("row", GROUP_W, F32), ("row", GROUP_W, BF16), ("acc", (1, GROUP_W), F32)])
    (dq0, dk0, dv0, dla0, dq1, dk1, dv1, dla1), got = _gla_chunk_bwd(p, la0, la1, do, s0, s1, ride)
    dq, dk, dv = _rows("gla_sum_bwd", lambda i, nt, a0, a1, b0, b1, c0, c1: (a0 + a1, b0 + b1, c0 + c1), 512,
                       [Row(t) for t in (dq0, dq1, dk0, dk1, dv0, dv1)], [("row", GROUP_W, BF16)] * 3)
    dz, dwg, dbg = _gla_gates_bwd(p, dla0, dla1, wg, bg)
    dw_gate = jnp.stack([dwg[e * GLA_RANK:(e + 1) * GLA_RANK, e * GROUP_W:(e + 1) * GROUP_W] for e in range(2)])
    grads = dict(gla_w_gate=dw_gate, gla_b_gate=dbg.reshape(2, GROUP_W), gla_norm=dng[0])
    return (dq, dk, dv, dg, dz), grads, got


def _rope_tables(L):
    pos = jnp.arange(L, dtype=F32)
    inv_freq = ROPE_THETA ** (-jnp.arange(0, HEAD_DIM, 2, dtype=F32) / HEAD_DIM)
    ang = pos[:, None] * inv_freq[None, :]
    cos, sin = jnp.cos(ang), jnp.sin(ang)
    return jnp.tile(jnp.concatenate([cos, cos], axis=1), (1, 4)), jnp.tile(jnp.concatenate([-sin, sin], axis=1), (1, 4))


def _swap_halves(t):
    lane = lax.broadcasted_iota(jnp.int32, t.shape, 1)
    first = (lane & (HEAD_DIM - 1)) < HEAD_DIM // 2
    return jnp.where(first, pltpu.roll(t, GROUP_W - HEAD_DIM // 2, axis=1), pltpu.roll(t, HEAD_DIM // 2, axis=1))


def _attn_prep(p, cosf, sinf):
    dils = [dil for _, dil in DIL_PAIRS]

    def body(i, nt, qb, kb, vb, qd, kd, vd, c, s):
        d = (qd * c + _swap_halves(qd) * s, kd * c + _swap_halves(kd) * s, vd)
        return (qb, kb, vb) + d * len(dils)
    ins = [Row(p, GROUP_W, cb) for cb in (4, 5, 6, 9, 10, 11)] + [Row(cosf), Row(sinf)]
    outs = [("row", GROUP_W, BF16)] * 3 + [("row", GROUP_W, BF16, dil) for dil in dils for _ in range(3)]
    res = _rows("attn_prep", body, 512, ins, outs)
    return tuple(res[:3]), {dil: tuple(res[3 + 3 * n:6 + 3 * n]) for n, dil in enumerate(dils)}


def _na_onehot():
    c = np.arange(GRID_W)
    dc = np.clip(c[None, :] - c[:, None], -(NA_COLS - 1), NA_COLS - 1) + NA_COLS - 1
    oh = np.zeros((LANES, GRID_W * GRID_W), np.float32)
    oh[dc.reshape(-1), np.arange(GRID_W * GRID_W)] = 1.0
    return jnp.asarray(oh)


def _na_colmask():
    c = np.arange(GRID_W)
    start = np.clip(c - NA_COLS // 2, 0, GRID_W - NA_COLS)
    ok = (c[None, :] >= start[:, None]) & (c[None, :] < start[:, None] + NA_COLS)
    return jnp.asarray(np.where(ok, 0.0, NEG).astype(np.float32))


N_DR = 2 * NA_ROWS - 1


NA_HALF = GRID_W // 2
NA_KCOLS = 48
NA_WIN = NA_ROWS * NA_KCOLS
NA_ROWS_PER_STEP = 2


def _na_bias(rpb):
    rp = jnp.pad(rpb.reshape(4 * N_DR, 2 * NA_COLS - 1), ((0, GRID_W - 4 * N_DR), (0, LANES - 2 * NA_COLS + 1)))

    def expand(r_ref, oh_ref, o_ref):
        o_ref[...] = _dot_exact(r_ref[...], oh_ref[...])

    r = pl.pallas_call(expand, name="na_bias_expand",
                       out_shape=jax.ShapeDtypeStruct((GRID_W, GRID_W * GRID_W), F32))(rp, _na_onehot())
    r = r[:4 * N_DR].reshape(4, N_DR, GRID_W, GRID_W)

    def build(r_ref, m_ref, o_ref):
        for h in range(4):
            for c in range(NA_ROWS):
                for half in range(2):
                    q0, k0 = NA_HALF * half, 16 * half
                    for i in range(NA_ROWS):
                        o_ref[h, c, half, :, i * NA_KCOLS:(i + 1) * NA_KCOLS] = (
                            r_ref[h, i - c + NA_ROWS - 1, q0:q0 + NA_HALF, k0:k0 + NA_KCOLS]
                            + m_ref[q0:q0 + NA_HALF, k0:k0 + NA_KCOLS])

    return pl.pallas_call(build, name="na_bias_build",
                          out_shape=jax.ShapeDtypeStruct((4, NA_ROWS, 2, NA_HALF, NA_WIN), F32))(r, _na_colmask())


def _na_bias_bwd(dbias):
    def fold(d_ref, o_ref):
        o_ref[...] = jnp.zeros_like(o_ref)
        for h in range(4):
            for a in range(N_DR):
                for half in range(2):
                    q0, k0 = NA_HALF * half, 16 * half
                    acc = jnp.zeros((NA_HALF, NA_KCOLS), F32)
                    for c in range(NA_ROWS):
                        i = a + c - (NA_ROWS - 1)
                        if 0 <= i < NA_ROWS:
                            acc = acc + d_ref[h, c, half, :, i * NA_KCOLS:(i + 1) * NA_KCOLS]
                    o_ref[h, a, q0:q0 + NA_HALF, k0:k0 + NA_KCOLS] = acc

    dr = pl.pallas_call(fold, name="na_bias_fold",
                        out_shape=jax.ShapeDtypeStruct((4, N_DR, GRID_W, GRID_W), F32))(dbias)
    dr = jnp.pad(dr.reshape(4 * N_DR, GRID_W * GRID_W), ((0, GRID_W - 4 * N_DR), (0, 0)))

    def contract(d_ref, oh_ref, o_ref):
        o_ref[...] = lax.dot_general(d_ref[...], oh_ref[...], _NT, precision=lax.Precision.HIGHEST,
                                     preferred_element_type=F32)

    g = pl.pallas_call(contract, name="na_bias_contract",
                       out_shape=jax.ShapeDtypeStruct((GRID_W, LANES), F32))(dr, _na_onehot())
    return g[:4 * N_DR, :2 * NA_COLS - 1].reshape(4, N_DR, 2 * NA_COLS - 1)


def _na_window(r, n_rows):
    rs = jnp.clip(r - NA_ROWS // 2, 0, n_rows - NA_ROWS)
    return rs, r - rs


def _na_key_rows(rs, half, t):
    return pl.ds(pl.multiple_of((rs + t) * GRID_W + 16 * half, 16), NA_KCOLS)


def _na_keys(ref, rs, half):
    return jnp.concatenate([ref[_na_key_rows(rs, half, t), :] for t in range(NA_ROWS)], axis=0)


def _na_stack(x, first):
    zero = jnp.zeros_like(x)
    return jnp.concatenate([jnp.where(first, x, zero), jnp.where(first, zero, x)], axis=0)


def _na_bias_spec():
    return pl.BlockSpec((2, NA_ROWS, 2, NA_HALF, NA_WIN), lambda j, i: (j, 0, 0, 0, 0))


def _grid_edges(n0, n1):
    j, i = pl.program_id(0), pl.program_id(1)
    return jnp.logical_and(j == 0, i == 0), jnp.logical_and(j == n0 - 1, i == n1 - 1)


def _na_fwd(q, k, v, bias, ride=None):
    L = q.shape[0]
    n_rows = L // GRID_W
    tm = _tile(L, 512, GRID_W)
    nt, nr = L // tm, tm // GRID_W

    def kern(q_ref, k_ref, v_ref, b_ref, o_ref):
        i = pl.program_id(1)
        first = lax.broadcasted_iota(jnp.int32, (NA_HALF, LANES), 1) < HEAD_DIM

        def body(it, carry):
            parts = []
            for u in range(NA_ROWS_PER_STEP):
                rr = it * NA_ROWS_PER_STEP + u
                rs, c = _na_window(i * nr + rr, n_rows)
                for half in range(2):
                    rows = pl.ds(pl.multiple_of(rr * GRID_W + NA_HALF * half, NA_HALF), NA_HALF)
                    bias = jnp.concatenate([b_ref[0, c, half], b_ref[1, c, half]], axis=0)
                    parts.append((rows, _na_stack(q_ref[rows, :], first), bias, _na_keys(k_ref, rs, half),
                                  _na_keys(v_ref, rs, half)))
            s = [_dot(qs, kw, _NT) * ATT_SCALE + bias for _, qs, bias, kw, _ in parts]
            e = [jnp.exp(x - jnp.max(x, axis=-1, keepdims=True)) for x in s]
            pn = [(x / jnp.sum(x, axis=-1, keepdims=True)).astype(BF16) for x in e]
            o = [_dot(p, part[4]) for p, part in zip(pn, parts)]
            for x, (rows, *_) in zip(o, parts):
                o_ref[rows, :] = jnp.where(first, x[:NA_HALF], x[NA_HALF:]).astype(BF16)
            return carry

        lax.fori_loop(0, nr // NA_ROWS_PER_STEP, body, 0)

    qspec = pl.BlockSpec((tm, LANES), lambda j, i: (i, j))
    kvspec = pl.BlockSpec((L, LANES), lambda j, i: (0, j))
    (y,), got = _call(
        kern, (q, k, v, bias), ride, lambda: _grid_edges(2, nt), name="na_fwd", grid=(2, nt),
        in_specs=[qspec, kvspec, kvspec, _na_bias_spec()],
        out_specs=[qspec], out_shape=[jax.ShapeDtypeStruct((L, GROUP_W), BF16)], scratch_shapes=[])
    return y, got


def _na_bwd(dy, dy_block, q, k, v, bias, ride=None):
    L = q.shape[0]
    n_rows = L // GRID_W
    tm = _tile(L, 512, GRID_W)
    nt, nr = L // tm, tm // GRID_W

    def kern(dy_ref, q_ref, k_ref, v_ref, b_ref, dq_ref, dk_ref, dv_ref, db_ref):
        i = pl.program_id(1)

        @pl.when(i == 0)
        def _():
            dk_ref[...] = jnp.zeros_like(dk_ref)
            dv_ref[...] = jnp.zeros_like(dv_ref)
            db_ref[...] = jnp.zeros_like(db_ref)

        first = lax.broadcasted_iota(jnp.int32, (NA_HALF, LANES), 1) < HEAD_DIM

        def body(rr, carry):
            rs, c = _na_window(i * nr + rr, n_rows)
            parts = []
            for half in range(2):
                rows = pl.ds(pl.multiple_of(rr * GRID_W + NA_HALF * half, NA_HALF), NA_HALF)
                bias = jnp.concatenate([b_ref[0, c, half], b_ref[1, c, half]], axis=0)
                parts.append((rows, half, _na_stack(q_ref[rows, :], first), _na_stack(dy_ref[rows, :].astype(BF16), first),
                              bias, _na_keys(k_ref, rs, half), _na_keys(v_ref, rs, half)))
            s = [_dot(qs, kw, _NT) * ATT_SCALE + bias for _, _, qs, _, bias, kw, _ in parts]
            dp = [_dot(dos, vw, _NT) for _, _, _, dos, _, _, vw in parts]
            e = [jnp.exp(x - jnp.max(x, axis=-1, keepdims=True)) for x in s]
            pn = [x / jnp.sum(x, axis=-1, keepdims=True) for x in e]
            ds = [p * (d - jnp.sum(p * d, axis=-1, keepdims=True)) for p, d in zip(pn, dp)]
            dsb = [x.astype(BF16) for x in ds]
            pnb = [x.astype(BF16) for x in pn]
            dq = [_dot(x, part[5]) for x, part in zip(dsb, parts)]
            dk = [_dot(x, part[2], _TN) for x, part in zip(dsb, parts)]
            dv = [_dot(x, part[3], _TN) for x, part in zip(pnb, parts)]
            for n, (rows, half, *_) in enumerate(parts):
                db_ref[0, c, half] += ds[n][:NA_HALF]
                db_ref[1, c, half] += ds[n][NA_HALF:]
                dq_ref[rows, :] = (jnp.where(first, dq[n][:NA_HALF], dq[n][NA_HALF:]) * ATT_SCALE).astype(BF16)
                for t in range(NA_ROWS):
                    kr = _na_key_rows(rs, half, t)
                    dk_ref[kr, :] += dk[n][t * NA_KCOLS:(t + 1) * NA_KCOLS] * ATT_SCALE
                    dv_ref[kr, :] += dv[n][t * NA_KCOLS:(t + 1) * NA_KCOLS]
            return carry

        lax.fori_loop(0, nr, body, 0)

    qspec = pl.BlockSpec((tm, LANES), lambda j, i: (i, j))
    kvspec = pl.BlockSpec((L, LANES), lambda j, i: (0, j))
    return _call(
        kern, (dy, q, k, v, bias), ride, lambda: _grid_edges(2, nt), name="na_bwd", grid=(2, nt),
        in_specs=[pl.BlockSpec((tm, LANES), lambda j, i: (i, dy_block + j)), qspec, kvspec, kvspec, _na_bias_spec()],
        out_specs=[qspec, kvspec, kvspec, _na_bias_spec()],
        out_shape=[jax.ShapeDtypeStruct((L, GROUP_W), BF16), jax.ShapeDtypeStruct((L, GROUP_W), F32),
                   jax.ShapeDtypeStruct((L, GROUP_W), F32),
                   jax.ShapeDtypeStruct((4, NA_ROWS, 2, NA_HALF, NA_WIN), F32)], scratch_shapes=[])


def _dil_specs(n, tq):
    R = DIL_RADIUS
    step, nb = tq // R, n // R
    main = pl.BlockSpec((tq, LANES), lambda j, i: (i, j))
    prev = pl.BlockSpec((R, LANES), lambda j, i: (jnp.maximum(i * step - 1, 0), j))
    nxt = pl.BlockSpec((R, LANES), lambda j, i: (jnp.minimum((i + 1) * step, nb - 1), j))
    return main, prev, nxt


def _dil_masks():
    R = DIL_RADIUS
    r = np.arange(2 * R)[:, None] & (R - 1)
    c = np.arange(3 * R)[None, :]
    band = np.abs(c - R - r) <= R
    ok = np.stack([band, band & (c >= R), band & (c < 2 * R), band & (c >= R) & (c < 2 * R)])
    return jnp.asarray(np.where(ok, 0.0, NEG).astype(np.float32))


def _dil_mask_spec():
    return pl.BlockSpec((4, 2 * DIL_RADIUS, 3 * DIL_RADIUS), lambda j, i: (0, 0, 0))


def _dil_mask(m_ref, i, sb, n_tiles, n_blocks):
    idx = 0
    if sb == 0:
        idx = idx + jnp.where(i == 0, 1, 0)
    if sb == n_blocks - 1:
        idx = idx + jnp.where(i == n_tiles - 1, 2, 0)
    return m_ref[idx]


def _dil_fwd(q, k, v, dil, ride=None):
    n = q.shape[0]
    tq = _tile(n, 256, DIL_RADIUS)

    def kern(q_ref, kp_ref, k_ref, kn_ref, vp_ref, v_ref, vn_ref, m_ref, o_ref, l_ref):
        i = pl.program_id(1)
        R = DIL_RADIUS
        ka = jnp.concatenate([kp_ref[...], k_ref[...], kn_ref[...]], axis=0)
        va = jnp.concatenate([vp_ref[...], v_ref[...], vn_ref[...]], axis=0)
        first = lax.broadcasted_iota(jnp.int32, (R, LANES), 1) < HEAD_DIM
        subs = range(tq // R)
        keys = lambda a, sb: a[sb * R:(sb + 3) * R]
        qs = [_na_stack(q_ref[sb * R:(sb + 1) * R, :], first) for sb in subs]
        s = [_dot(qs[sb], keys(ka, sb), _NT) for sb in subs]
        s = [s[sb] * ATT_SCALE + _dil_mask(m_ref, i, sb, n // tq, len(subs)) for sb in subs]
        m = [jnp.max(x, axis=-1, keepdims=True) for x in s]
        e = [jnp.exp(x - mx) for x, mx in zip(s, m)]
        den = [jnp.sum(x, axis=-1, keepdims=True) for x in e]
        o = [_dot((e[sb] / den[sb]).astype(BF16), keys(va, sb)) for sb in subs]
        for sb in subs:
            lse = m[sb] + jnp.log(den[sb])
            o_ref[sb * R:(sb + 1) * R, :] = jnp.where(first, o[sb][:R], o[sb][R:])
            l_ref[sb * R:(sb + 1) * R, :] = jnp.where(first, lse[:R], lse[R:])

    main, prev, nxt = _dil_specs(n, tq)
    (o, lse), got = _call(
        kern, (q, k, k, k, v, v, v, _dil_masks()), ride,
        lambda: _grid_edges(2 * dil, n // tq), name=f"dil_fwd_{dil}", grid=(2 * dil, n // tq),
        in_specs=[main, prev, main, nxt, prev, main, nxt, _dil_mask_spec()], out_specs=[main, main],
        out_shape=[jax.ShapeDtypeStruct((n, dil * GROUP_W), F32)] * 2, scratch_shapes=[])
    return (o, lse), got


def _dil_bwd(q, k, v, do, lse, dterm, dil, ride=None):
    n = q.shape[0]
    R = DIL_RADIUS
    tq = _tile(n, 256, R)
    nq = n // tq

    def kern(q_ref, kp_ref, k_ref, kn_ref, vp_ref, v_ref, vn_ref, do_ref, l_ref, dt_ref, m_ref, dq_ref, dk_ref, dv_ref):
        i = pl.program_id(1)

        @pl.when(i == 0)
        def _():
            dk_ref[...] = jnp.zeros_like(dk_ref)
            dv_ref[...] = jnp.zeros_like(dv_ref)

        ka = jnp.concatenate([kp_ref[...], k_ref[...], kn_ref[...]], axis=0)
        va = jnp.concatenate([vp_ref[...], v_ref[...], vn_ref[...]], axis=0)
        first = lax.broadcasted_iota(jnp.int32, (R, LANES), 1) < HEAD_DIM
        subs = range(tq // R)
        keys = lambda a, sb: a[sb * R:(sb + 3) * R]
        rows = lambda ref, sb: ref[sb * R:(sb + 1) * R, :]
        per_head = lambda t: jnp.concatenate([t[:, 0:1], t[:, HEAD_DIM:HEAD_DIM + 1]], axis=0)
        qs = [_na_stack(rows(q_ref, sb), first) for sb in subs]
        dos = [_na_stack(rows(do_ref, sb), first) for sb in subs]
        s = [_dot(qs[sb], keys(ka, sb), _NT) for sb in subs]
        dp = [_dot(dos[sb], keys(va, sb), _NT) for sb in subs]
        pn = [jnp.exp(s[sb] * ATT_SCALE + _dil_mask(m_ref, i, sb, nq, len(subs)) - per_head(rows(l_ref, sb))) for sb in subs]
        dsb = [(pn[sb] * (dp[sb] - per_head(rows(dt_ref, sb)))).astype(BF16) for sb in subs]
        pnb = [x.astype(BF16) for x in pn]
        dq = [_dot(dsb[sb], keys(ka, sb)) for sb in subs]
        dk = [_dot(dsb[sb], qs[sb], _TN) for sb in subs]
        dv = [_dot(pnb[sb], dos[sb], _TN) for sb in subs]
        zeros = lambda blocks: [jnp.zeros((blocks * R, LANES), F32)] if blocks else []
        pad = lambda t, sb: jnp.concatenate(zeros(sb) + [t] + zeros(len(subs) - 1 - sb), axis=0)
        dka = sum(pad(dk[sb], sb) for sb in subs) * ATT_SCALE
        dva = sum(pad(dv[sb], sb) for sb in subs)
        for sb in subs:
            dq_ref[sb * R:(sb + 1) * R, :] = jnp.where(first, dq[sb][:R], dq[sb][R:]) * ATT_SCALE
        r0 = pl.multiple_of(i * tq, R)
        dk_ref[pl.ds(r0, tq), :] += dka[R:R + tq]
        dv_ref[pl.ds(r0, tq), :] += dva[R:R + tq]

        @pl.when(i > 0)
        def _():
            dk_ref[pl.ds(r0 - R, R), :] += dka[:R]
            dv_ref[pl.ds(r0 - R, R), :] += dva[:R]

        @pl.when(i < nq - 1)
        def _():
            dk_ref[pl.ds(r0 + tq, R), :] += dka[R + tq:]
            dv_ref[pl.ds(r0 + tq, R), :] += dva[R + tq:]

    main, prev, nxt = _dil_specs(n, tq)
    whole = pl.BlockSpec((n, LANES), lambda j, i: (0, j))
    shp = jax.ShapeDtypeStruct((n, dil * GROUP_W), F32)
    (dq, dk, dv), got = _call(
        kern, (q, k, k, k, v, v, v, do, lse, dterm, _dil_masks()), ride,
        lambda: _grid_edges(2 * dil, nq), name=f"dil_bwd_{dil}", grid=(2 * dil, nq),
        in_specs=[main, prev, main, nxt, prev, main, nxt, main, main, main, _dil_mask_spec()],
        out_specs=[main, whole, whole],
        out_shape=[shp] * 3, scratch_shapes=[])
    return (dq, dk, dv), got


def _dil_weights(lses):
    m = jnp.maximum(jnp.maximum(lses[0], lses[1]), lses[2])
    e = [jnp.exp(l - m) for l in lses]
    tot = e[0] + e[1] + e[2]
    return [x / tot for x in e]


def _dilated_fwd(qkv, rides):
    dils = [dil for _, dil in DIL_PAIRS]
    res, got = [], {}
    for dil in dils:
        r, got[f"dil{dil}"] = _dil_fwd(*qkv[dil], dil, rides.get(f"dil{dil}"))
        res.append(r)

    def body(i, nt, o0, o1, o2, l0, l1, l2):
        w = _dil_weights((l0, l1, l2))
        return w[0] * o0 + w[1] * o1 + w[2] * o2

    ins = [Row(r[0], dil=d) for r, d in zip(res, dils)] + [Row(r[1], dil=d) for r, d in zip(res, dils)]
    return _rows("dil_combine", body, 512, ins, [("row", GROUP_W, BF16)])[0], res, got


def _dilated_bwd(dy, dy_cb, qkv, saved, cosf, sinf, rides):
    dils = [dil for _, dil in DIL_PAIRS]
    def split(i, nt, dy, o0, o1, o2, l0, l1, l2, bd):
        w = _dil_weights((l0, l1, l2))
        y = w[0] * o0 + w[1] * o1 + w[2] * o2
        dyy = _head_sum(dy * y, bd)
        return tuple(wg * dy for wg in w) + tuple(wg * dyy for wg in w)

    ins = ([Row(dy, GROUP_W, dy_cb)] + [Row(r[0], dil=d) for r, d in zip(saved, dils)]
           + [Row(r[1], dil=d) for r, d in zip(saved, dils)])
    outs = _rows("dil_split_bwd", split, 512, ins + [Full(_block_ones(GROUP_W, HEAD_DIM))],
                 [("row", GROUP_W, BF16, d) for d in dils] + [("row", GROUP_W, F32, d) for d in dils])
    g, got = [], {}
    for b, dil in enumerate(dils):
        r, got[f"dil{dil}"] = _dil_bwd(*qkv[dil], outs[b], saved[b][1], outs[3 + b], dil, rides.get(f"dil{dil}"))
        g.append(r)

    def finish(i, nt, q0, q1, q2, k0, k1, k2, v0, v1, v2, c, s):
        dq, dk = q0 + q1 + q2, k0 + k1 + k2
        return dq * c + _swap_halves(dq * s), dk * c + _swap_halves(dk * s), v0 + v1 + v2

    ins = [Row(g[b][t], dil=dils[b]) for t in range(3) for b in range(3)] + [Row(cosf), Row(sinf)]
    return _rows("dil_finish_bwd", finish, 512, ins, [("row", GROUP_W, BF16)] * 3), got


def _layer_fwd(x, W, l, cosf, sinf, rides):
    rides = {c: Exchange(items) for c, items in rides.items()}
    h1 = _rms_fwd(x, W["mix_norm_pre"][l][None], "mix_norm")
    p = _mm(h1, W["w_in"][l], "nn", F32, "proj_in")
    ya, sa, got_gla = _gla_fwd(p, W, l, rides.get("gla"))
    (qb, kb, vb), qkv_d = _attn_prep(p, cosf, sinf)
    bias = _na_bias(W["na_rpb"][l])
    yb, got_na = _na_fwd(qb, kb, vb, bias, rides.get("na"))
    yc, sc = _lru_fwd(p, W, l)
    yd, sd, got = _dilated_fwd(qkv_d, rides)
    got.update(gla=got_gla, na=got_na)
    ycat = jnp.concatenate([ya, yb, yc, yd], axis=1)
    ymix = _mm(ycat, W["w_out"][l], "nn", F32, "proj_out", tm=1024)
    xm = _rms_resid_fwd(x, ymix, W["mix_norm_post"][l][None], "mix_resid")
    h2 = _rms_fwd(xm, W["ffn_norm_pre"][l][None], "ffn_norm")
    gu, act = _ffn_in_swiglu(h2, W["ffn_w_in"][l])
    f = _mm(act, W["ffn_w_out"][l], "nn", F32, "ffn_out")
    xo = _rms_resid_fwd(xm, f, W["ffn_norm_post"][l][None], "ffn_resid")
    saved = dict(x=x, h1=h1, p=p, sa=sa, att=(qb, kb, vb, qkv_d), bias=bias, sc=sc, sd=sd, ycat=ycat, ymix=ymix,
                 xm=xm, h2=h2, gu=gu, act=act, f=f)
    return xo, saved, got


def _layer_bwd(dxo, W, l, S, cosf, sinf, rides, early=None):
    g = {}
    df, g["ffn_norm_post"] = _rms_bwd(dxo, S["f"], W["ffn_norm_post"][l][None], "ffn_resid_bwd", out_dtype=BF16)
    g["ffn_w_out"] = _mm(S["act"], df, "tn", BF16, "ffn_out_dw", tm=256, tk=4096)
    dgu = _ffn_out_dx_swiglu(df, W["ffn_w_out"][l], S["gu"])
    dh2 = _mm(dgu, W["ffn_w_in"][l], "nt", F32, "ffn_in_dx")
    g["ffn_w_in"] = _mm(S["h2"], dgu, "tn", BF16, "ffn_in_dw", tm=1024, tn=512, tk=4096)
    dxm, g["ffn_norm_pre"] = _rms_bwd(dh2, S["xm"], W["ffn_norm_pre"][l][None], "ffn_norm_bwd", resid=dxo)
    dymix, g["mix_norm_post"] = _rms_bwd(dxm, S["ymix"], W["mix_norm_post"][l][None], "mix_resid_bwd", out_dtype=BF16)
    dycat = _mm(dymix, W["w_out"][l], "nt", F32, "proj_out_dx", tm=1024)
    g["w_out"] = _mm(S["ycat"], dymix, "tn", BF16, "proj_out_dw", tm=1024, tn=512, tk=4096)
    if early is not None:
        for c, items in early(g).items():
            rides = {**rides, c: rides.get(c, []) + items}
    rides = {c: Exchange(items) for c, items in rides.items()}
    p = S["p"]
    qb, kb, vb, qkv_d = S["att"]
    (dqa, dka, dva, dga, dz), ga, got_gla = _gla_bwd(dycat, 0, p, W, l, S["sa"], rides.get("gla"))
    (dqb, dkb, dvb, dbias), got_na = _na_bwd(dycat, 2, qb, kb, vb, S["bias"], rides.get("na"))
    g["na_rpb"] = _na_bias_bwd(dbias)
    dxc, dgc, gc = _lru_bwd(dycat, 2, p, W, l, S["sc"])
    (dqd, dkd, dvd), got = _dilated_bwd(dycat, 3, qkv_d, S["sd"], cosf, sinf, rides)
    got.update(gla=got_gla, na=got_na)
    g.update(ga)
    g.update(gc)
    dp = jnp.concatenate([dqa, dka, dva, dga, dqb, dkb.astype(BF16), dvb.astype(BF16), dxc, dgc, dqd, dkd, dvd, dz], axis=1)
    dh1 = _mm(dp, W["w_in"][l], "nt", F32, "proj_in_dx")
    g["w_in"] = _mm(S["h1"], dp, "tn", BF16, "proj_in_dw", tm=1024, tn=640, tk=4096)
    dx, g["mix_norm_pre"] = _rms_bwd(dh1, S["x"], W["mix_norm_pre"][l][None], "mix_norm_bwd", resid=dxm)
    for n in ("ffn_norm_post", "ffn_norm_pre", "mix_norm_post", "mix_norm_pre"):
        g[n] = g[n][0]
    return dx, g, got


MESH_AXES = ("x", "y", "c")


class Xfer:
    def __init__(self, arr, kind):
        self.arr, self.kind = arr, kind
        shp = arr.shape
        if kind == "all":
            self.out = (N_DEV,) + shp
        elif kind == "slot":
            self.out = shp
        elif kind == "rows":
            self.r = shp[1] // N_DEV
            self.out = (N_DEV, shp[0], self.r, shp[2])
        else:
            self.r = shp[1]
            self.out = (shp[0], N_DEV * shp[1], shp[2])

    def src(self, ref, peer):
        if self.kind == "slot":
            return ref.at[peer]
        if self.kind == "rows":
            return ref.at[:, pl.ds(peer * self.r, self.r), :]
        return ref

    def dst(self, ref, me):
        if self.kind == "place":
            return ref.at[:, pl.ds(me * self.r, self.r), :]
        return ref.at[me]


class Exchange:
    def __init__(self, items):
        n = len(items)
        self.items = items
        self.arrays = [it.arr for it in items]
        self.specs = [pl.BlockSpec(memory_space=pl.ANY)] * n
        self.out_shape = [jax.ShapeDtypeStruct(it.out, it.arr.dtype) for it in items]
        self.scratch = [pltpu.SemaphoreType.DMA((n * (N_DEV - 1),)), pltpu.SemaphoreType.DMA((n * (N_DEV - 1),)),
                        pltpu.SemaphoreType.DMA((n,))]

    def copies(self, ins, outs, sems):
        send_sems, recv_sems, local_sems = sems
        x, y, c = (lax.axis_index(a) for a in MESH_AXES)
        me = 4 * x + 2 * y + c
        out = []
        for t, it in enumerate(self.items):
            out.append(pltpu.make_async_copy(it.src(ins[t], me), it.dst(outs[t], me), local_sems.at[t]))
            for k in range(1, N_DEV):
                px, py, pc = x ^ ((k >> 2) & 1), y ^ ((k >> 1) & 1), c ^ (k & 1)
                s = t * (N_DEV - 1) + k - 1
                out.append(pltpu.make_async_remote_copy(
                    src_ref=it.src(ins[t], 4 * px + 2 * py + pc), dst_ref=it.dst(outs[t], me),
                    send_sem=send_sems.at[s], recv_sem=recv_sems.at[s], device_id=(px, py, pc),
                    device_id_type=pl.DeviceIdType.MESH))
        return out

    def start(self, ins, outs, sems):
        for cp in self.copies(ins, outs, sems):
            cp.start()

    def wait(self, ins, outs, sems):
        for cp in self.copies(ins, outs, sems):
            cp.wait()


def _exchange(items, name):
    ex = Exchange(items)
    n = len(items)

    def body(*refs):
        ex.start(refs[:n], refs[n:2 * n], refs[2 * n:])
        ex.wait(refs[:n], refs[n:2 * n], refs[2 * n:])

    return pl.pallas_call(body, name=name, out_shape=ex.out_shape, in_specs=ex.specs, out_specs=ex.specs,
                          scratch_shapes=ex.scratch)(*ex.arrays)


def _call(kern, arrays, ride, edges, *, name, grid, in_specs, out_specs, out_shape, scratch_shapes):
    params = _ARB(len(grid))
    if ride is None:
        return pl.pallas_call(kern, name=name, grid=grid, in_specs=in_specs, out_specs=out_specs, out_shape=out_shape,
                              scratch_shapes=scratch_shapes, compiler_params=params)(*arrays), None
    ni, no, ns, nx = len(in_specs), len(out_specs), len(scratch_shapes), len(ride.items)

    def wrapped(*refs):
        ins, xin = refs[:ni], refs[ni:ni + nx]
        outs, xout = refs[ni + nx:ni + nx + no], refs[ni + nx + no:ni + 2 * nx + no]
        scr, sems = refs[ni + 2 * nx + no:ni + 2 * nx + no + ns], refs[ni + 2 * nx + no + ns:]
        first, last = edges()

        @pl.when(first)
        def _():
            ride.start(xin, xout, sems)

        kern(*ins, *outs, *scr)

        @pl.when(last)
        def _():
            ride.wait(xin, xout, sems)

    res = pl.pallas_call(
        wrapped, name=name, grid=grid, in_specs=list(in_specs) + ride.specs, out_specs=list(out_specs) + ride.specs,
        out_shape=list(out_shape) + ride.out_shape, scratch_shapes=list(scratch_shapes) + ride.scratch,
        compiler_params=params)(*arrays, *ride.arrays)
    return res[:no], res[no:]


def _column_segments(width, permuted):
    z0, z1, zn = 4 * GROUP_W, 4 * GROUP_W + 2 * GLA_RANK, 12 * GROUP_W
    segs = []
    for d in range(N_DEV):
        lo, hi = d * width, (d + 1) * width
        if not permuted:
            segs.append([(0, width, lo)])
            continue
        runs = []
        for a, b, shift in ((0, z0, 0), (z0, z1, zn - z0), (z1, 10 ** 9, -(z1 - z0))):
            s, e = max(lo, a), min(hi, b)
            if s < e:
                runs.append((s - lo, e - lo, s + shift))
        segs.append(runs)
    return segs


def _cols_from_pieces(pieces, segs, cols, name):
    _, R, w = pieces.shape
    tm = _tile(R, 256, 16)
    used = max(f + (b - a) for runs in segs for a, b, f in runs)

    def kern(p_ref, o_ref):
        for d, runs in enumerate(segs):
            for a, b, f in runs:
                o_ref[:, f:f + (b - a)] = p_ref[d, :, a:b]
        if used < cols:
            o_ref[:, used:cols] = jnp.zeros((tm, cols - used), o_ref.dtype)

    return pl.pallas_call(
        kern, name=name, grid=(R // tm,), in_specs=[pl.BlockSpec((N_DEV, tm, w), lambda i: (0, i, 0))],
        out_specs=pl.BlockSpec((tm, cols), lambda i: (i, 0)), out_shape=jax.ShapeDtypeStruct((R, cols), pieces.dtype),
        compiler_params=_ARB(1),
    )(pieces)


def _pieces_from_cols(full, segs, w, name):
    R, cols = full.shape
    tm = _tile(R, 256, 16)

    def kern(f_ref, o_ref):
        for d, runs in enumerate(segs):
            for a, b, f in runs:
                o_ref[d, :, a:b] = f_ref[:, f:f + (b - a)]

    return pl.pallas_call(
        kern, name=name, grid=(R // tm,), in_specs=[pl.BlockSpec((tm, cols), lambda i: (i, 0))],
        out_specs=pl.BlockSpec((N_DEV, tm, w), lambda i: (0, i, 0)),
        out_shape=jax.ShapeDtypeStruct((N_DEV, R, w), full.dtype), compiler_params=_ARB(1),
    )(full)


def _sum_slots(recv, name):
    n, R, C = recv.shape
    tm = _tile(R, 256, 16)

    def kern(*refs):
        acc = refs[0][...].astype(F32)
        for r in refs[1:n]:
            acc = acc + r[...].astype(F32)
        refs[n][...] = acc

    return pl.pallas_call(
        kern, name=name, grid=(R // tm,),
        in_specs=[pl.BlockSpec((None, tm, C), lambda i, _s=s: (_s, i, 0)) for s in range(n)],
        out_specs=pl.BlockSpec((tm, C), lambda i: (i, 0)), out_shape=jax.ShapeDtypeStruct((R, C), F32),
        compiler_params=_ARB(1),
    )(*([recv] * n))


BIG = (("w_in", 2), ("w_out", 1), ("ffn_w_in", 2), ("ffn_w_out", 1))
SMALL_SHARDED = ("gla_w_gate", "gla_b_gate", "lru_conv_w", "lru_b_a", "lru_b_x", "lru_lambda")
REPLICATED = ("mix_norm_pre", "mix_norm_post", "gla_norm", "na_rpb", "lru_conv_b", "lru_w_a", "lru_w_x",
              "ffn_norm_pre", "ffn_norm_post")
WEIGHTS = ("mix_norm_pre", "mix_norm_post", "w_in", "gla_w_gate", "gla_b_gate", "gla_norm", "na_rpb", "lru_conv_w",
           "lru_conv_b", "lru_w_a", "lru_b_a", "lru_w_x", "lru_b_x", "lru_lambda", "w_out", "ffn_norm_pre",
           "ffn_norm_post", "ffn_w_in", "ffn_w_out")
FLAT_C = 1024


def _to_rows(vec, row_unit):
    n = vec.shape[-1]
    rows = -(-n // (FLAT_C * row_unit)) * row_unit
    pad = [(0, 0)] * (vec.ndim - 1) + [(0, rows * FLAT_C - n)]
    return jnp.pad(vec, pad).reshape(vec.shape[:-1] + (rows, FLAT_C))


def _unshard(parts, axis):
    t = jnp.moveaxis(parts, 0, axis)
    shp = list(t.shape)
    return t.reshape(shp[:axis] + [shp[axis] * shp[axis + 1]] + shp[axis + 2:])


def _shards(full, axis):
    shp = list(full.shape)
    t = full.reshape(shp[:axis] + [N_DEV, shp[axis] // N_DEV] + shp[axis + 1:])
    return jnp.moveaxis(t, axis, 0)


def _weight_rides(W, l):
    bf = lambda n: W[n][l].astype(BF16)
    ffn = bf("ffn_w_in")
    half = ffn.shape[0] // 2
    return {"gla": [Xfer(bf("w_in"), "all"), Xfer(bf("w_out")[None], "place")],
            "na": [Xfer(bf("ffn_w_out")[None], "place")],
            "dil1": [Xfer(ffn[:half], "all")], "dil4": [Xfer(ffn[half:], "all")]}


def _unpack_weights(full, W, got):
    w_in_w, ffn_w = W["w_in"].shape[-1], W["ffn_w_in"].shape[-1]
    full["w_in"].append(_cols_from_pieces(got["gla"][0], _column_segments(w_in_w, True), P_COLS, "unpack_w_in"))
    full["w_out"].append(got["gla"][1][0])
    full["ffn_w_out"].append(got["na"][0][0])
    full["ffn_w_in"].append(jnp.concatenate(
        [_cols_from_pieces(got[c][0], _column_segments(ffn_w, False), N_DEV * ffn_w, "unpack_ffn_w_in")
         for c in ("dil1", "dil4")], axis=0))


def _w_in_grad_pieces(g, W):
    w_in_w = W["w_in"].shape[-1]
    return _pieces_from_cols(g["w_in"], _column_segments(w_in_w, True), w_in_w, "pack_w_in")


def _ffn_grad_halves(g, W):
    ffn_w = W["ffn_w_in"].shape[-1]
    p_ffn = _pieces_from_cols(g["ffn_w_in"], _column_segments(ffn_w, False), ffn_w, "pack_ffn_w_in")
    half = p_ffn.shape[1] // 2
    return p_ffn[:, :half], p_ffn[:, half:]


def _grad_rides(g, W):
    top, bottom = _ffn_grad_halves(g, W)
    return {"gla": [Xfer(_w_in_grad_pieces(g, W), "slot"), Xfer(g["w_out"][None], "rows")],
            "na": [Xfer(g["ffn_w_out"][None], "rows"), Xfer(top, "slot")], "dil1": [Xfer(bottom, "slot")]}


def _early_grad_rides(g, W):
    top, bottom = _ffn_grad_halves(g, W)
    return {"gla": [Xfer(g["w_out"][None], "rows"), Xfer(g["ffn_w_out"][None], "rows")], "na": [Xfer(top, "slot")],
            "dil4": [Xfer(bottom, "slot")]}


def _sum_big(w_in, w_out, ffn_w_out, ffn_top, ffn_bottom):
    s = lambda r, n: _sum_slots(r.reshape(N_DEV, -1, r.shape[-1]), "sum_" + n)
    return {"w_in": s(w_in, "w_in"), "w_out": s(w_out, "w_out"), "ffn_w_out": s(ffn_w_out, "ffn_w_out"),
            "ffn_w_in": jnp.concatenate([s(ffn_top, "ffn_w_in"), s(ffn_bottom, "ffn_w_in")], axis=0)}


def _exchange_named(rides, extra, name):
    names = list(rides)
    res = _exchange([it for n in names for it in rides[n]] + extra, name)
    got, at = {}, 0
    for n in names:
        got[n] = res[at:at + len(rides[n])]
        at += len(rides[n])
    return got, res[at:]


def _train(x, target, W):
    L = x.shape[0]
    depth = W["w_in"].shape[0]
    cosf, sinf = _rope_tables(L)
    small = jnp.concatenate([W[n].reshape(-1) for n in SMALL_SHARDED])
    small16 = _to_rows(lax.bitcast_convert_type(small, jnp.uint16).reshape(-1), 16)
    got, (sm,) = _exchange_named(_weight_rides(W, 0), [Xfer(small16, "all")], "gather_first")
    full = dict(W, w_in=[], w_out=[], ffn_w_in=[], ffn_w_out=[])
    _unpack_weights(full, W, got)
    sm = lax.bitcast_convert_type(sm.reshape(N_DEV, -1)[:, :2 * small.size].reshape(N_DEV, small.size, 2), F32)
    off = 0
    for n in SMALL_SHARDED:
        full[n] = _unshard(sm[:, off:off + W[n].size].reshape((N_DEV,) + W[n].shape), W[n].ndim - 1)
        off += W[n].size

    saved = []
    for l in range(depth):
        x, S, got = _layer_fwd(x, full, l, cosf, sinf, _weight_rides(W, l + 1) if l + 1 < depth else {})
        saved.append(S)
        if l + 1 < depth:
            _unpack_weights(full, W, got)
    loss, dx = _loss_fwd_bwd(x, target)

    grads, big, rides = [None] * depth, [None] * depth, {}
    for l in reversed(range(depth)):
        early = (lambda g: _early_grad_rides(g, W)) if l == 0 else None
        dx, grads[l], got = _layer_bwd(dx, full, l, saved[l], cosf, sinf, rides, early)
        if l + 1 < depth:
            big[l + 1] = _sum_big(got["gla"][0], got["gla"][1], got["na"][0], got["na"][1], got["dil1"][0])
        late = {c: len(items) for c, items in rides.items()}
        rides = _grad_rides(grads[l], W) if l > 0 else {}
    G = {n: jnp.stack([g[n] for g in grads]) for n in SMALL_SHARDED + REPLICATED}
    small_g = jnp.concatenate([_shards(G[n], G[n].ndim - 1).reshape(N_DEV, -1) for n in SMALL_SHARDED], axis=1)
    repl_g = jnp.concatenate([G[n].reshape(-1) for n in REPLICATED])
    rest = _exchange([Xfer(_w_in_grad_pieces(grads[0], W), "slot"), Xfer(_to_rows(small_g, 8), "slot"),
                      Xfer(_to_rows(repl_g, 8), "all")], "exchange_last")
    own = lambda c, k: got[c][late.get(c, 0) + k]
    big[0] = _sum_big(rest[0], own("gla", 0), own("gla", 1), own("na", 0), own("dil4", 0))
    rest = rest[1:]
    out = {n: jnp.stack([b[n] for b in big]).reshape(W[n].shape) for n, _ in BIG}
    for names, r, tag in ((SMALL_SHARDED, rest[0], "sum_small"), (REPLICATED, rest[1], "sum_replicated")):
        flat, off = _sum_slots(r, tag).reshape(-1), 0
        for n in names:
            out[n] = flat[off:off + W[n].size].reshape(W[n].shape)
            off += W[n].size
    return loss, dx, out


def _update(W, G, M, V):
    delta, new_m, new_v = {}, {}, {}
    for n, _ in BIG:
        two_d = lambda a: a.reshape(-1, a.shape[-1])
        d, m, v = _adamw(two_d(W[n]), two_d(G[n]), two_d(M[n]), two_d(V[n]), "adamw_" + n)
        delta[n], new_m[n], new_v[n] = (t.reshape(W[n].shape) for t in (d, m, v))
    rest = SMALL_SHARDED + REPLICATED
    pack = lambda D: _to_rows(jnp.concatenate([D[n].reshape(-1) for n in rest]), 16)
    d, m, v = _adamw(pack(W), pack(G), pack(M), pack(V), "adamw_small")
    off = 0
    for n in rest:
        sl = lambda t: t.reshape(-1)[off:off + W[n].size].reshape(W[n].shape)
        delta[n], new_m[n], new_v[n] = sl(d), sl(m), sl(v)
        off += W[n].size
    return delta, new_m, new_v


def kernel(x, mix_norm_pre, mix_norm_post, w_in, gla_w_gate, gla_b_gate, gla_norm, na_rpb, lru_conv_w, lru_conv_b, lru_w_a, lru_b_a, lru_w_x, lru_b_x, lru_lambda, w_out, ffn_norm_pre, ffn_norm_post, ffn_w_in, ffn_w_out, loss_target, m_mix_norm_pre, m_mix_norm_post, m_w_in, m_gla_w_gate, m_gla_b_gate, m_gla_norm, m_na_rpb, m_lru_conv_w, m_lru_conv_b, m_lru_w_a, m_lru_b_a, m_lru_w_x, m_lru_b_x, m_lru_lambda, m_w_out, m_ffn_norm_pre, m_ffn_norm_post, m_ffn_w_in, m_ffn_w_out, v_mix_norm_pre, v_mix_norm_post, v_w_in, v_gla_w_gate, v_gla_b_gate, v_gla_norm, v_na_rpb, v_lru_conv_w, v_lru_conv_b, v_lru_w_a, v_lru_b_a, v_lru_w_x, v_lru_b_x, v_lru_lambda, v_w_out, v_ffn_norm_pre, v_ffn_norm_post, v_ffn_w_in, v_ffn_w_out):
    W = dict(zip(WEIGHTS, (mix_norm_pre, mix_norm_post, w_in, gla_w_gate, gla_b_gate, gla_norm, na_rpb, lru_conv_w, lru_conv_b, lru_w_a, lru_b_a, lru_w_x, lru_b_x, lru_lambda, w_out, ffn_norm_pre, ffn_norm_post, ffn_w_in, ffn_w_out)))
    M = dict(zip(WEIGHTS, (m_mix_norm_pre, m_mix_norm_post, m_w_in, m_gla_w_gate, m_gla_b_gate, m_gla_norm, m_na_rpb, m_lru_conv_w, m_lru_conv_b, m_lru_w_a, m_lru_b_a, m_lru_w_x, m_lru_b_x, m_lru_lambda, m_w_out, m_ffn_norm_pre, m_ffn_norm_post, m_ffn_w_in, m_ffn_w_out)))
    V = dict(zip(WEIGHTS, (v_mix_norm_pre, v_mix_norm_post, v_w_in, v_gla_w_gate, v_gla_b_gate, v_gla_norm, v_na_rpb, v_lru_conv_w, v_lru_conv_b, v_lru_w_a, v_lru_b_a, v_lru_w_x, v_lru_b_x, v_lru_lambda, v_w_out, v_ffn_norm_pre, v_ffn_norm_post, v_ffn_w_in, v_ffn_w_out)))
    loss, dx, G = _train(x[0], loss_target[0], W)
    loss = lax.psum(loss, MESH_AXES)
    delta, new_m, new_v = _update(W, G, M, V)
    return (loss, dx[None], *[G[n] for n in WEIGHTS], *[delta[n] for n in WEIGHTS], *[new_m[n] for n in WEIGHTS],
            *[new_v[n] for n in WEIGHTS])
```

```python
import functools
import math

import numpy as np
import jax
import jax.numpy as jnp
from jax import lax
from jax.experimental import pallas as pl
from jax.experimental.pallas import tpu as pltpu

F32 = jnp.float32
BF16 = jnp.bfloat16

N_DEV = 8
HEAD_DIM = 64
GROUP_W = 256
GLA_RANK = 16
GLA_TAU = 16.0
GLA_CHUNK = 64
GRID_W = 64
NA_ROWS = 8
NA_COLS = 16
LRU_C = 8.0
DIL_PAIRS = ((128, 1), (512, 4), (2048, 16))
DIL_RADIUS = 64
ROPE_THETA = 10000.0
EPS = 1e-6
ATT_SCALE = HEAD_DIM ** -0.5
NEG = -1e30
LANES = 128
P_COLS = 12 * GROUP_W + LANES
Z_BLOCK = 12 * GROUP_W // LANES

ADAM_LR = 0.001
ADAM_B1 = 0.9
ADAM_B2 = 0.999
ADAM_EPS = 1e-08
ADAM_WD = 0.01
ADAM_STEP = 10

VMEM_LIMIT = 56 * 1024 * 1024
_ARB = lambda n: pltpu.CompilerParams(dimension_semantics=("arbitrary",) * n, vmem_limit_bytes=VMEM_LIMIT)


def _tile(dim, pref, unit):
    t = min(pref, dim) // unit * unit
    while t >= unit:
        if dim % t == 0:
            return t
        t -= unit
    return dim


def _mm(a, b, mode, out_dtype, name, tm=512, tn=None, tk=None):
    if mode == "nn":
        (M, K), (_, N) = a.shape, b.shape
    elif mode == "nt":
        (M, K), (N, _) = a.shape, b.shape
    else:
        (K, M), (_, N) = a.shape, b.shape
    tm = _tile(M, tm, LANES if mode == "tn" else 8)
    tn = _tile(N, tn or N, LANES)
    tk = _tile(K, tk or K, LANES)
    nk = K // tk
    dims = {"nn": (((1,), (0,)), ((), ())), "nt": (((1,), (1,)), ((), ())), "tn": (((0,), (0,)), ((), ()))}[mode]

    def kern(a_ref, b_ref, o_ref, *acc):
        part = lax.dot_general(a_ref[...].astype(BF16), b_ref[...].astype(BF16), dims, preferred_element_type=F32)
        if nk == 1:
            o_ref[...] = part.astype(out_dtype)
            return
        k = pl.program_id(2)

        @pl.when(k == 0)
        def _():
            acc[0][...] = part

        @pl.when(jnp.logical_and(k > 0, k < nk - 1))
        def _():
            acc[0][...] += part

        @pl.when(k == nk - 1)
        def _():
            o_ref[...] = (acc[0][...] + part).astype(out_dtype)

    a_spec = pl.BlockSpec((tk, tm), lambda i, j, k: (k, i)) if mode == "tn" else pl.BlockSpec((tm, tk), lambda i, j, k: (i, k))
    b_spec = pl.BlockSpec((tn, tk), lambda i, j, k: (j, k)) if mode == "nt" else pl.BlockSpec((tk, tn), lambda i, j, k: (k, j))
    return pl.pallas_call(
        kern, name=name, grid=(M // tm, N // tn, nk),
        in_specs=[a_spec, b_spec], out_specs=pl.BlockSpec((tm, tn), lambda i, j, k: (i, j)),
        out_shape=jax.ShapeDtypeStruct((M, N), out_dtype),
        scratch_shapes=[pltpu.VMEM((tm, tn), F32)] if nk > 1 else [],
        compiler_params=_ARB(3),
    )(a, b)


class Row:
    def __init__(self, a, width=None, cb=0, halo=False, dil=1):
        self.a, self.width, self.cb, self.halo, self.dil = a, width, cb, halo, dil


class Full:
    def __init__(self, a):
        self.a = a


HALO = 8


def _rows(name, body, tm, ins, outs):
    outs = [o if len(o) == 4 else o + (1,) for o in outs]
    L = next(s.a.shape[0] * s.dil for s in ins if isinstance(s, Row))
    tm = _tile(L, tm, 16)
    dilated = any(s.dil > 1 for s in ins if isinstance(s, Row)) or any(o[3] > 1 for o in outs)
    nt = L // tm
    nb8 = L // HALO
    step = tm // HALO
    in_specs, arrays, layout = [], [], []
    for s in ins:
        if isinstance(s, Full):
            nd = s.a.ndim
            in_specs.append(pl.BlockSpec(s.a.shape, lambda i, _nd=nd: (0,) * _nd))
            arrays.append(s.a)
            layout.append(1)
        else:
            w = s.width or s.a.shape[1]
            in_specs.append(pl.BlockSpec((tm // s.dil, w), lambda i, _cb=s.cb: (i, _cb)))
            arrays.append(s.a)
            if s.dil > 1:
                layout.append(-s.dil)
            elif s.halo:
                in_specs.append(pl.BlockSpec((HALO, w), lambda i, _cb=s.cb: (jnp.maximum(i * step - 1, 0), _cb)))
                in_specs.append(pl.BlockSpec((HALO, w), lambda i, _cb=s.cb: (jnp.minimum((i + 1) * step, nb8 - 1), _cb)))
                arrays += [s.a, s.a]
                layout.append(3)
            else:
                layout.append(1)
    out_specs, out_shapes = [], []
    for kind, shp, dt, dil in outs:
        if kind == "row":
            out_specs.append(pl.BlockSpec((tm // dil, dil * shp), lambda i: (i, 0)))
            out_shapes.append(jax.ShapeDtypeStruct((L // dil, dil * shp), dt))
        else:
            out_specs.append(pl.BlockSpec(shp, lambda i, _n=len(shp): (0,) * _n))
            out_shapes.append(jax.ShapeDtypeStruct(shp, dt))
    n_in, n_out = len(arrays), len(outs)

    def kern(*refs):
        i = pl.program_id(0)
        lo, hi = refs[n_in + n_out:] if dilated else (None, None)

        def undilate(ref, d):
            for j in range(d):
                rows = pl.ds(j, tm // d, stride=d)
                lo[rows, :] = ref[:, j * GROUP_W:j * GROUP_W + LANES].astype(F32)
                hi[rows, :] = ref[:, j * GROUP_W + LANES:(j + 1) * GROUP_W].astype(F32)
            return jnp.concatenate([lo[...], hi[...]], axis=1)

        def dilate(val, ref, d, dt):
            lo[...] = val[:, :LANES].astype(F32)
            hi[...] = val[:, LANES:].astype(F32)
            for j in range(d):
                rows = pl.ds(j, tm // d, stride=d)
                ref[:, j * GROUP_W:j * GROUP_W + LANES] = lo[rows, :].astype(dt)
                ref[:, j * GROUP_W + LANES:(j + 1) * GROUP_W] = hi[rows, :].astype(dt)

        vals, p = [], 0
        for n in layout:
            if n == 1:
                vals.append(refs[p][...])
            elif n < 0:
                vals.append(undilate(refs[p], -n))
                n = 1
            else:
                vals.append((refs[p + 1][...], refs[p][...], refs[p + 2][...]))
            p += n
        res = body(i, nt, *vals)
        if not isinstance(res, (tuple, list)):
            res = (res,)
        for (kind, shp, dt, dil), o_ref, r in zip(outs, refs[n_in:], res):
            if kind == "row" and dil > 1:
                dilate(r, o_ref, dil, dt)
            elif kind == "row":
                o_ref[...] = r.astype(dt)
            else:
                @pl.when(i == 0)
                def _(o_ref=o_ref):
                    o_ref[...] = jnp.zeros_like(o_ref)
                o_ref[...] += r.astype(dt)

    res = pl.pallas_call(
        kern, name=name, grid=(nt,), in_specs=in_specs, out_specs=out_specs, out_shape=out_shapes,
        scratch_shapes=[pltpu.VMEM((tm, LANES), F32)] * 2 if dilated else [], compiler_params=_ARB(1),
    )(*arrays)
    return res


def _shift(h, o, i, nt):
    prev, cur, nxt = h
    if o == 0:
        return cur
    tm = cur.shape[0]
    cat = jnp.concatenate([prev, cur, nxt], axis=0)
    sh = pltpu.roll(cat, (-o) % (tm + 2 * HALO), axis=0)[HALO:HALO + tm]
    row = lax.broadcasted_iota(jnp.int32, cur.shape, 0)
    if o < 0:
        ok = jnp.logical_or(i > 0, row >= -o)
    else:
        ok = jnp.logical_or(i < nt - 1, row < tm - o)
    return jnp.where(ok, sh, 0.0)


def _colsum(v):
    return jnp.sum(v, axis=0, keepdims=True)


def _sigmoid(x):
    return 1.0 / (1.0 + jnp.exp(-x))


def _softplus(x):
    return jnp.maximum(x, 0.0) + jnp.log1p(jnp.exp(-jnp.abs(x)))


def _silu(x):
    return x * _sigmoid(x)


def _dsilu(x):
    s = _sigmoid(x)
    return s * (1.0 + x * (1.0 - s))


_GELU_C = math.sqrt(2.0 / math.pi)


def _gelu(x):
    return 0.5 * x * (1.0 + jnp.tanh(_GELU_C * (x + 0.044715 * x * x * x)))


def _dgelu(x):
    t = jnp.tanh(_GELU_C * (x + 0.044715 * x * x * x))
    return 0.5 * (1.0 + t) + 0.5 * x * (1.0 - t * t) * _GELU_C * (1.0 + 3.0 * 0.044715 * x * x)


def _head_sum(v, bd):
    return jnp.dot(v, bd, precision=lax.Precision.HIGHEST, preferred_element_type=F32)


def _block_ones(n, blk):
    r = np.arange(n)
    return jnp.asarray((r[:, None] // blk == r[None, :] // blk).astype(np.float32))


def _rms_fwd(x, g, name):
    def body(i, nt, x, g):
        r = lax.rsqrt(jnp.mean(x * x, axis=-1, keepdims=True) + EPS)
        return x * r * g
    return _rows(name, body, 256, [Row(x), Full(g)], [("row", x.shape[1], BF16)])[0]


def _rms_resid_fwd(x, y, g, name):
    def body(i, nt, x, y, g):
        r = lax.rsqrt(jnp.mean(y * y, axis=-1, keepdims=True) + EPS)
        return x + y * r * g
    return _rows(name, body, 256, [Row(x), Row(y), Full(g)], [("row", x.shape[1], F32)])[0]


def _rms_bwd(dy, x, g, name, resid=None, out_dtype=F32):
    D = x.shape[1]

    def body(i, nt, dy, x, g, *rest):
        dy = dy.astype(F32)
        r = lax.rsqrt(jnp.mean(x * x, axis=-1, keepdims=True) + EPS)
        xh = x * r
        dxh = dy * g
        dx = r * (dxh - xh * jnp.mean(dxh * xh, axis=-1, keepdims=True))
        if rest:
            dx = dx + rest[0]
        return dx, _colsum(dy * xh)

    ins = [Row(dy), Row(x), Full(g)] + ([Row(resid)] if resid is not None else [])
    return _rows(name, body, 256, ins, [("row", D, out_dtype), ("acc", (1, D), F32)])


def _ffn_in_swiglu(h, w):
    (M, K), N = h.shape, w.shape[1]
    F = N // 2
    tm = _tile(M, 256, 16)

    def kern(a_ref, b_ref, gu_ref, act_ref):
        gu = _dot(a_ref[...], b_ref[...])
        gu_ref[...] = gu
        act_ref[...] = (_silu(gu[:, :F]) * gu[:, F:]).astype(BF16)

    return pl.pallas_call(
        kern, name="ffn_in_swiglu", grid=(M // tm,),
        in_specs=[pl.BlockSpec((tm, K), lambda i: (i, 0)), pl.BlockSpec((K, N), lambda i: (0, 0))],
        out_specs=[pl.BlockSpec((tm, N), lambda i: (i, 0)), pl.BlockSpec((tm, F), lambda i: (i, 0))],
        out_shape=[jax.ShapeDtypeStruct((M, N), F32), jax.ShapeDtypeStruct((M, F), BF16)], compiler_params=_ARB(1),
    )(h, w)


def _ffn_out_dx_swiglu(df, w, gu):
    (M, K), N = df.shape, gu.shape[1]
    F = N // 2
    tm = _tile(M, 256, 16)

    def kern(a_ref, b_ref, gu_ref, o_ref):
        da = _dot(a_ref[...], b_ref[...], _NT)
        gu = gu_ref[...]
        gate, up = gu[:, :F], gu[:, F:]
        o_ref[:, :F] = (da * up * _dsilu(gate)).astype(BF16)
        o_ref[:, F:] = (da * _silu(gate)).astype(BF16)

    return pl.pallas_call(
        kern, name="ffn_out_dx_swiglu", grid=(M // tm,),
        in_specs=[pl.BlockSpec((tm, K), lambda i: (i, 0)), pl.BlockSpec((F, K), lambda i: (0, 0)),
                  pl.BlockSpec((tm, N), lambda i: (i, 0))],
        out_specs=pl.BlockSpec((tm, N), lambda i: (i, 0)), out_shape=jax.ShapeDtypeStruct((M, N), BF16),
        compiler_params=_ARB(1),
    )(df, w, gu)


def _loss_fwd_bwd(y, target):
    D = y.shape[1]

    def body(i, nt, y, t):
        err = y - t
        part = 0.5 * jnp.sum(jnp.mean(err * err, axis=-1, keepdims=True), axis=0, keepdims=True)
        return err * (1.0 / D), jnp.broadcast_to(part, (1, LANES))
    dy, loss = _rows("loss", body, 256, [Row(y), Row(target)], [("row", D, F32), ("acc", (1, LANES), F32)])
    return loss[0, 0], dy


def _adamw(w, g, m, v, name):
    C = w.shape[1]
    bc1 = 1.0 - ADAM_B1 ** ADAM_STEP
    bc2 = 1.0 - ADAM_B2 ** ADAM_STEP

    def body(i, nt, w, g, m, v):
        m = ADAM_B1 * m + (1.0 - ADAM_B1) * g
        v = ADAM_B2 * v + (1.0 - ADAM_B2) * (g * g)
        delta = -ADAM_LR * ((m / bc1) / (jnp.sqrt(v / bc2) + ADAM_EPS) + ADAM_WD * w)
        return delta, m, v
    return _rows(name, body, 256, [Row(w), Row(g), Row(m), Row(v)], [("row", C, F32)] * 3)


def _expm1(x):
    return jnp.tanh(0.5 * x) * (jnp.exp(x) + 1.0)


def _lru_gates(xh, i, nt, cw, cb, wa, wx, ba, bx, lam):
    xc = cb
    for j in range(4):
        xc = xc + cw[j:j + 1] * _shift(xh, j - 2, i, nt)
    xcb = xc.astype(BF16)
    gates = []
    for e in range(2):
        r = _sigmoid(jnp.dot(xcb, wa[e], preferred_element_type=F32) + ba[e:e + 1])
        ig = _sigmoid(jnp.dot(xcb, wx[e], preferred_element_type=F32) + bx[e:e + 1])
        sp = _softplus(-lam[e:e + 1])
        la = -LRU_C * r * sp
        gates.append((r, ig, sp, jnp.exp(la), jnp.sqrt(-_expm1(2.0 * la))))
    return xc, xcb, gates


def _scan2(af, uf, ab, ub, adjoint, name):
    L, W = af.shape
    tm = _tile(L, 512, 8)
    nt, nb = L // tm, tm // 8

    def blk(A, U, h, reverse, row):
        for d in (1, 2, 4):
            if reverse:
                ok, sh = row < 8 - d, 8 - d
            else:
                ok, sh = row >= d, d
            As = jnp.where(ok, pltpu.roll(A, sh, axis=0), 1.0)
            Us = jnp.where(ok, pltpu.roll(U, sh, axis=0), 0.0)
            U = A * Us + U
            A = A * As
        return A * h + U

    def kern(af_ref, uf_ref, ab_ref, ub_ref, of_ref, ob_ref, c_ref):
        @pl.when(pl.program_id(0) == 0)
        def _():
            c_ref[...] = jnp.zeros_like(c_ref)

        row = lax.broadcasted_iota(jnp.int32, (8, W), 0)
        full = lambda v: jnp.broadcast_to(v, (8, W))

        def body(j, carry):
            hF, aF, hB, aB = carry
            r0 = pl.multiple_of(j * 8, 8)
            r1 = pl.multiple_of((nb - 1 - j) * 8, 8)
            A, U = af_ref[pl.ds(r0, 8), :], uf_ref[pl.ds(r0, 8), :]
            if adjoint:
                C = jnp.where(row == 0, aF, pltpu.roll(A, 1, axis=0))
                aF = full(A[7:8])
            else:
                C = A
            H = blk(C, U, hF, False, row)
            of_ref[pl.ds(r0, 8), :] = H
            hF = full(H[7:8])
            A, U = ab_ref[pl.ds(r1, 8), :], ub_ref[pl.ds(r1, 8), :]
            if adjoint:
                C = jnp.where(row == 7, aB, pltpu.roll(A, 7, axis=0))
                aB = full(A[0:1])
            else:
                C = A
            H = blk(C, U, hB, True, row)
            ob_ref[pl.ds(r1, 8), :] = H
            hB = full(H[0:1])
            return hF, aF, hB, aB

        carry = lax.fori_loop(0, nb, body, (c_ref[0], c_ref[1], c_ref[2], c_ref[3]))
        for n in range(4):
            c_ref[n] = carry[n]

    fwd = pl.BlockSpec((tm, W), lambda i: (i, 0))
    bwd = pl.BlockSpec((tm, W), lambda i: (nt - 1 - i, 0))
    return pl.pallas_call(
        kern, name=name, grid=(nt,), in_specs=[fwd, fwd, bwd, bwd], out_specs=[fwd, bwd],
        out_shape=[jax.ShapeDtypeStruct((L, W), F32)] * 2,
        scratch_shapes=[pltpu.VMEM((4, 8, W), F32)], compiler_params=_ARB(1),
    )(af, uf, ab, ub)


def _block_diag(w):
    rows = jnp.tile(w.reshape(2, GROUP_W, HEAD_DIM), (1, 1, 4))
    return jnp.where(_block_ones(GROUP_W, HEAD_DIM) > 0.5, rows, 0.0).astype(BF16)


def _diag_blocks(w):
    return jnp.stack([w[:, h * 64:(h + 1) * 64, h * 64:(h + 1) * 64] for h in range(4)], axis=1)


def _lru_params(W, l):
    return [Full(W["lru_conv_w"][l]), Full(W["lru_conv_b"][l][None]), Full(_block_diag(W["lru_w_a"][l])),
            Full(_block_diag(W["lru_w_x"][l])), Full(W["lru_b_a"][l]), Full(W["lru_b_x"][l]), Full(W["lru_lambda"][l])]


def _lru_fwd(p, W, l):
    def pre(i, nt, xh, *prm):
        xc, _, g = _lru_gates(xh, i, nt, *prm)
        return g[0][3], g[0][4] * (g[0][1] * xc), g[1][3], g[1][4] * (g[1][1] * xc)

    a0, u0, a1, u1 = _rows("lru_pre", pre, 256, [Row(p, GROUP_W, 7, halo=True)] + _lru_params(W, l),
                           [("row", GROUP_W, F32)] * 4)
    hf, hb = _scan2(a0, u0, a1, u1, False, "lru_scan")
    yc = _rows("lru_post", lambda i, nt, hf, hb, gc: (hf + hb) * _gelu(gc), 512,
               [Row(hf), Row(hb), Row(p, GROUP_W, 8)], [("row", GROUP_W, BF16)])[0]
    return yc, (a0, a1, hf, hb)


def _lru_bwd(dy, dy_cb, p, W, l, saved):
    a0, a1, hf, hb = saved

    def post(i, nt, dy, hf, hb, gc):
        return dy * _gelu(gc), dy * (hf + hb) * _dgelu(gc)

    dh, dgc = _rows("lru_post_bwd", post, 512, [Row(dy, GROUP_W, dy_cb), Row(hf), Row(hb), Row(p, GROUP_W, 8)],
                    [("row", GROUP_W, F32), ("row", GROUP_W, BF16)])
    gb, gf = _scan2(a1, dh, a0, dh, True, "lru_scan_adj")

    def gates_bwd(i, nt, xh, gf, gb, hfh, hbh, cw, cb, wa, wx, ba, bx, lam):
        xc, xcb, g = _lru_gates(xh, i, nt, cw, cb, wa, wx, ba, bx, lam)
        dxc = jnp.zeros_like(xc)
        dwa, dwx, dba, dbx, dlam = [], [], [], [], []
        for e, du, hprev in ((0, gf, _shift(hfh, -1, i, nt)), (1, gb, _shift(hbh, 1, i, nt))):
            r, ig, sp, a, s = g[e]
            dxc = dxc + du * s * ig
            dla = du * hprev * a - (du * ig * xc) * a * a / s
            dza = (dla * (-LRU_C) * sp) * r * (1.0 - r)
            dzx = (du * s * xc) * ig * (1.0 - ig)
            dlam.append(_colsum(dla * r) * (LRU_C * _sigmoid(-lam[e:e + 1])))
            dba.append(_colsum(dza))
            dbx.append(_colsum(dzx))
            dzab, dzxb = dza.astype(BF16), dzx.astype(BF16)
            tn = (((0,), (0,)), ((), ()))
            nt_ = (((1,), (1,)), ((), ()))
            dwa.append(lax.dot_general(xcb, dzab, tn, preferred_element_type=F32))
            dwx.append(lax.dot_general(xcb, dzxb, tn, preferred_element_type=F32))
            dxc = dxc + lax.dot_general(dzab, wa[e], nt_, preferred_element_type=F32)
            dxc = dxc + lax.dot_general(dzxb, wx[e], nt_, preferred_element_type=F32)
        cat = lambda v: jnp.concatenate(v, axis=0)
        return dxc, jnp.stack(dwa), jnp.stack(dwx), cat(dba), cat(dbx), cat(dlam)

    dxc, dwa, dwx, dba, dbx, dlam = _rows(
        "lru_gates_bwd", gates_bwd, 256,
        [Row(p, GROUP_W, 7, halo=True), Row(gf), Row(gb), Row(hf, halo=True), Row(hb, halo=True)] + _lru_params(W, l),
        [("row", GROUP_W, F32), ("acc", (2, GROUP_W, GROUP_W), F32), ("acc", (2, GROUP_W, GROUP_W), F32),
         ("acc", (2, GROUP_W), F32), ("acc", (2, GROUP_W), F32), ("acc", (2, GROUP_W), F32)])

    def conv_bwd(i, nt, dh_, xh, cw):
        dxb = jnp.zeros_like(dh_[1])
        dcw = []
        for j in range(4):
            dxb = dxb + cw[j:j + 1] * _shift(dh_, 2 - j, i, nt)
            dcw.append(_colsum(dh_[1] * _shift(xh, j - 2, i, nt)))
        return dxb, jnp.concatenate(dcw, axis=0), _colsum(dh_[1])

    dxb, dcw, dcb = _rows("lru_conv_bwd", conv_bwd, 512,
                          [Row(dxc, halo=True), Row(p, GROUP_W, 7, halo=True), Full(W["lru_conv_w"][l])],
                          [("row", GROUP_W, BF16), ("acc", (4, GROUP_W), F32), ("acc", (1, GROUP_W), F32)])
    grads = dict(lru_conv_w=dcw, lru_conv_b=dcb[0], lru_w_a=_diag_blocks(dwa), lru_w_x=_diag_blocks(dwx),
                 lru_b_a=dba, lru_b_x=dbx, lru_lambda=dlam)
    return dxb, dgc, grads


_NT = (((1,), (1,)), ((), ()))
_TN = (((0,), (0,)), ((), ()))


def _dot(a, b, dims=None):
    if dims is None:
        return jnp.dot(a, b, preferred_element_type=F32)
    return lax.dot_general(a, b, dims, preferred_element_type=F32)


def _dot_exact(a, b):
    return jnp.dot(a, b, precision=lax.Precision.HIGHEST, preferred_element_type=F32)


def _gla_gate_w(w_gate, b_gate):
    zero = jnp.zeros((GLA_RANK, GROUP_W), w_gate.dtype)
    wg = jnp.concatenate([jnp.concatenate([w_gate[0], zero], axis=1), jnp.concatenate([zero, w_gate[1]], axis=1),
                          jnp.zeros((LANES - 2 * GLA_RANK, 2 * GROUP_W), w_gate.dtype)], axis=0)
    return wg.astype(BF16), b_gate.reshape(1, 2 * GROUP_W)


def _gla_gates_fwd(p, wg, bg):
    def body(i, nt, z, wg, bg):
        logit = _dot(z.astype(BF16), wg) + bg
        la = -_softplus(-logit) * (1.0 / GLA_TAU)
        return la[:, :GROUP_W], la[:, GROUP_W:]
    return _rows("gla_gates", body, 512, [Row(p, LANES, Z_BLOCK), Full(wg), Full(bg)], [("row", GROUP_W, F32)] * 2)


def _gla_gates_bwd(p, dla0, dla1, wg, bg):
    def body(i, nt, z, d0, d1, wg, bg):
        zb = z.astype(BF16)
        logit = _dot(zb, wg) + bg
        dlogit = jnp.concatenate([d0, d1], axis=1) * (1.0 / GLA_TAU) * _sigmoid(-logit)
        dlb = dlogit.astype(BF16)
        return _dot(dlb, wg, _NT), _dot(zb, dlb, _TN), _colsum(dlogit)
    return _rows("gla_gates_bwd", body, 512, [Row(p, LANES, Z_BLOCK), Row(dla0), Row(dla1), Full(wg), Full(bg)],
                 [("row", LANES, BF16), ("acc", (LANES, 2 * GROUP_W), F32), ("acc", (1, 2 * GROUP_W), F32)])


def _gla_order(reverse):
    t = np.arange(GLA_CHUNK)
    m = (t[None, :] >= t[:, None]) if reverse else (t[None, :] <= t[:, None])
    return m.astype(np.float32), (32, 0) if reverse else (31, 63)


def _stack_heads(x, bd):
    return jnp.where(bd, jnp.concatenate([x] * 4, axis=0), 0.0)


def _diag_heads(r, bd):
    r = jnp.where(bd, r, 0.0)
    return r[0:64] + r[64:128] + r[128:192] + r[192:256]


def _gla_factors(q_ref, k_ref, rows, b, mid, last):
    bm, bl = b[mid:mid + 1], b[last:last + 1]
    qs = q_ref[rows, :] * ATT_SCALE
    k = k_ref[rows, :]
    P, N, E, Fd = jnp.exp(b - bm), jnp.exp(bm - b), jnp.exp(b), jnp.exp(bl - b)
    return (P, N, E, Fd, jnp.exp(bl)), (qs * P, k * N, qs * E, k * Fd)


def _gla_specs(L, walk_up):
    tm = _tile(L, 512, GLA_CHUNK)
    nt, nc = L // tm, tm // GLA_CHUNK
    specs = []
    for up in walk_up:
        t = (lambda i: i) if up else (lambda i: nt - 1 - i)
        specs.append(dict(
            col=lambda cb, _t=t: pl.BlockSpec((tm, GROUP_W), lambda i: (_t(i), cb)),
            row=pl.BlockSpec((tm, GROUP_W), lambda i, _t=t: (_t(i), 0)),
            state=pl.BlockSpec((nc, GROUP_W, GROUP_W), lambda i, _t=t: (_t(i), 0, 0))))
    return nt, nc, specs


def _gla_chunk_fwd(p, la0, la1, ride=None):
    L = la0.shape[0]
    nt, nc, specs = _gla_specs(L, (True, False))
    orders = [_gla_order(False), _gla_order(True)]

    def kern(q0, k0, v0, l0, q1, k1, v1, l1, m0_ref, m1_ref, bd_ref, o0, s0, o1, s1, st_ref):
        @pl.when(pl.program_id(0) == 0)
        def _():
            st_ref[...] = jnp.zeros_like(st_ref)

        bd = bd_ref[...] > 0.5
        dirs = []
        for e, (q_ref, k_ref, v_ref, la_ref, m_ref, o_ref, s_ref) in enumerate(
                ((q0, k0, v0, l0, m0_ref, o0, s0), (q1, k1, v1, l1, m1_ref, o1, s1))):
            mv = m_ref[...]
            dirs.append((q_ref, k_ref, v_ref, la_ref, mv, jnp.concatenate([mv] * 4, axis=0) > 0.5, o_ref, s_ref))

        def body(cc, carry):
            E = range(2)
            cs = [nc - 1 - cc if e else cc for e in E]
            rows = [pl.ds(pl.multiple_of(c * GLA_CHUNK, GLA_CHUNK), GLA_CHUNK) for c in cs]
            b = [_dot_exact(dirs[e][4], dirs[e][3][rows[e], :]) for e in E]
            t = [_gla_factors(dirs[e][0], dirs[e][1], rows[e], b[e], *orders[e][1]) for e in E]
            vb = [dirs[e][2][rows[e], :].astype(BF16) for e in E]
            st = [st_ref[e] for e in E]
            a = [_dot(_stack_heads(t[e][1][0], bd).astype(BF16), t[e][1][1].astype(BF16), _NT) for e in E]
            inter = [_dot(t[e][1][2].astype(BF16), st[e].astype(BF16), _NT) for e in E]
            kv = [_dot(vb[e], t[e][1][3].astype(BF16), _TN) for e in E]
            a = [jnp.where(dirs[e][5], a[e], 0.0).astype(BF16) for e in E]
            r = [_dot(a[e], vb[e]) for e in E]
            for e in E:
                dirs[e][7][cs[e]] = st[e]
                dirs[e][6][rows[e], :] = _diag_heads(r[e], bd) + inter[e]
                st_ref[e] = st[e] * t[e][0][4] + jnp.where(bd, kv[e], 0.0)
            return carry

        lax.fori_loop(0, nc, body, 0)

    const = lambda shp: pl.BlockSpec(shp, lambda i: (0, 0))
    in_specs, out_specs = [], []
    for sp in specs:
        in_specs += [sp["col"](0), sp["col"](1), sp["col"](2), sp["row"]]
        out_specs += [sp["row"], sp["state"]]
    return _call(
        kern, (p, p, p, la0, p, p, p, la1, jnp.asarray(orders[0][0]), jnp.asarray(orders[1][0]),
               _block_ones(GROUP_W, HEAD_DIM)),
        ride, lambda: (pl.program_id(0) == 0, pl.program_id(0) == nt - 1), name="gla_fwd", grid=(nt,),
        in_specs=in_specs + [const((GLA_CHUNK, GLA_CHUNK))] * 2 + [const((GROUP_W, GROUP_W))], out_specs=out_specs,
        out_shape=[jax.ShapeDtypeStruct((L, GROUP_W), F32),
                   jax.ShapeDtypeStruct((L // GLA_CHUNK, GROUP_W, GROUP_W), F32)] * 2,
        scratch_shapes=[pltpu.VMEM((2, GROUP_W, GROUP_W), F32)])


def _gla_chunk_bwd(p, la0, la1, do, sprev0, sprev1, ride=None):
    L = la0.shape[0]
    nt, nc, specs = _gla_specs(L, (False, True))
    orders = [_gla_order(False), _gla_order(True)]

    def kern(q0, k0, v0, l0, do0, s0, q1, k1, v1, l1, do1, s1, m0_ref, m1_ref, t0_ref, t1_ref, bd_ref, *rest):
        outs, dst_ref = (rest[0:4], rest[4:8]), rest[8]

        @pl.when(pl.program_id(0) == 0)
        def _():
            dst_ref[...] = jnp.zeros_like(dst_ref)

        bd = bd_ref[...] > 0.5
        row = lax.broadcasted_iota(jnp.int32, (GLA_CHUNK, GROUP_W), 0)
        dirs = []
        for ins, m_ref, t_ref in (((q0, k0, v0, l0, do0, s0), m0_ref, t0_ref), ((q1, k1, v1, l1, do1, s1), m1_ref, t1_ref)):
            mv = m_ref[...]
            dirs.append(ins + (mv, t_ref[...], jnp.concatenate([mv] * 4, axis=0) > 0.5))

        def body(cc, carry):
            E2 = range(2)
            cs = [cc if e else nc - 1 - cc for e in E2]
            rows = [pl.ds(pl.multiple_of(c * GLA_CHUNK, GLA_CHUNK), GLA_CHUNK) for c in cs]
            b = [_dot_exact(dirs[e][6], dirs[e][3][rows[e], :]) for e in E2]
            t = [_gla_factors(dirs[e][0], dirs[e][1], rows[e], b[e], *orders[e][1]) for e in E2]
            vb = [dirs[e][2][rows[e], :].astype(BF16) for e in E2]
            dov = [dirs[e][4][rows[e], :] for e in E2]
            dob = [x.astype(BF16) for x in dov]
            st = [dirs[e][5][cs[e]] for e in E2]
            dst = [dst_ref[e] for e in E2]
            stb, dstb = [x.astype(BF16) for x in st], [x.astype(BF16) for x in dst]
            qst = [_stack_heads(t[e][1][0], bd).astype(BF16) for e in E2]
            dost = [_stack_heads(dov[e], bd).astype(BF16) for e in E2]
            kNb, qEb, kFb = ([t[e][1][n].astype(BF16) for e in E2] for n in (1, 2, 3))
            a = [_dot(qst[e], kNb[e], _NT) for e in E2]
            da = [_dot(dost[e], vb[e], _NT) for e in E2]
            dqE = [_dot(dob[e], stb[e]) for e in E2]
            dkF = [_dot(vb[e], dstb[e]) for e in E2]
            dv_inter = [_dot(kFb[e], dstb[e], _NT) for e in E2]
            dst_in = [_dot(dob[e], qEb[e], _TN) for e in E2]
            a = [jnp.where(dirs[e][8], a[e], 0.0).astype(BF16) for e in E2]
            da = [jnp.where(dirs[e][8], da[e], 0.0).astype(BF16) for e in E2]
            dv_intra = [_dot(a[e], dost[e], _TN) for e in E2]
            dqP = [_dot(da[e], kNb[e]) for e in E2]
            dkN = [_dot(da[e], qst[e], _TN) for e in E2]
            db = []
            for e in E2:
                (P, N, Ef, Fd, d), (qP, kN, qE, kF) = t[e]
                mid, last = orders[e][1]
                dq_ref, dk_ref, dv_ref, _ = outs[e]
                dqp = _diag_heads(dqP[e], bd)
                dd = _colsum(dst[e] * st[e])
                dst_ref[e] = jnp.where(bd, dst_in[e], 0.0) + dst[e] * d
                tP, tN, tE, tF = dqp * qP, dkN[e] * kN, dqE[e] * qE, dkF[e] * kF
                db.append(tP - tN + tE - tF + jnp.where(row == mid, _colsum(tN - tP), 0.0)
                          + jnp.where(row == last, _colsum(tF) + dd * d, 0.0))
                dq_ref[rows[e], :] = (dqp * P + dqE[e] * Ef) * ATT_SCALE
                dk_ref[rows[e], :] = dkN[e] * N + dkF[e] * Fd
                dv_ref[rows[e], :] = dv_intra[e] + dv_inter[e]
            dla = [_dot_exact(dirs[e][7], db[e]) for e in E2]
            for e in E2:
                outs[e][3][rows[e], :] = dla[e]
            return carry

        lax.fori_loop(0, nc, body, 0)

    const = lambda shp: pl.BlockSpec(shp, lambda i: (0, 0))
    in_specs, out_specs = [], []
    for sp in specs:
        in_specs += [sp["col"](0), sp["col"](1), sp["col"](2), sp["row"], sp["row"], sp["state"]]
        out_specs += [sp["row"]] * 4
    m0, m1 = orders[0][0], orders[1][0]
    return _call(
        kern, (p, p, p, la0, do, sprev0, p, p, p, la1, do, sprev1, jnp.asarray(m0), jnp.asarray(m1),
               jnp.asarray(m0.T.copy()), jnp.asarray(m1.T.copy()), _block_ones(GROUP_W, HEAD_DIM)),
        ride, lambda: (pl.program_id(0) == 0, pl.program_id(0) == nt - 1), name="gla_bwd", grid=(nt,),
        in_specs=in_specs + [const((GLA_CHUNK, GLA_CHUNK))] * 4 + [const((GROUP_W, GROUP_W))], out_specs=out_specs,
        out_shape=[jax.ShapeDtypeStruct((L, GROUP_W), F32)] * 8,
        scratch_shapes=[pltpu.VMEM((2, GROUP_W, GROUP_W), F32)])


def _gla_fwd(p, W, l, ride=None):
    wg, bg = _gla_gate_w(W["gla_w_gate"][l], W["gla_b_gate"][l])
    la0, la1 = _gla_gates_fwd(p, wg, bg)
    (of, s0, ob, s1), got = _gla_chunk_fwd(p, la0, la1, ride)

    def post(i, nt, of, ob, g, ng, bd):
        o = of + ob
        r = lax.rsqrt(_head_sum(o * o, bd) * (1.0 / HEAD_DIM) + EPS)
        return o * r * ng * _silu(g)

    ya = _rows("gla_post", post, 512, [Row(of), Row(ob), Row(p, GROUP_W, 3), Full(W["gla_norm"][l][None]),
                                       Full(_block_ones(GROUP_W, HEAD_DIM))], [("row", GROUP_W, BF16)])[0]
    return ya, (la0, la1, of, ob, s0, s1), got


def _gla_bwd(dy, dy_cb, p, W, l, saved, ride=None):
    la0, la1, of, ob, s0, s1 = saved
    wg, bg = _gla_gate_w(W["gla_w_gate"][l], W["gla_b_gate"][l])

    def post(i, nt, dy, of, ob, g, ng, bd):
        o = of + ob
        r = lax.rsqrt(_head_sum(o * o, bd) * (1.0 / HEAD_DIM) + EPS)
        oh = o * r
        don = dy * _silu(g)
        doh = don * ng
        do = r * (doh - oh * _head_sum(doh * oh, bd) * (1.0 / HEAD_DIM))
        return do, dy * (oh * ng) * _dsilu(g), _colsum(don * oh)

    do, dg, dng = _rows("gla_post_bwd", post, 512,
                        [Row(dy, GROUP_W, dy_cb), Row(of), Row(ob), Row(p, GROUP_W, 3), Full(W["gla_norm"][l][None]),
                         Full(_block_ones(GROUP_W, HEAD_DIM))],
                        [("row", GROUP_W, F32), ("row", GROUP_W, BF16), ("acc", (1, GROUP_W), F32)])
    (dq0, dk0, dv0, dla0, dq1, dk1, dv1, dla1), got = _gla_chunk_bwd(p, la0, la1, do, s0, s1, ride)
    dq, dk, dv = _rows("gla_sum_bwd", lambda i, nt, a0, a1, b0, b1, c0, c1: (a0 + a1, b0 + b1, c0 + c1), 512,
                       [Row(t) for t in (dq0, dq1, dk0, dk1, dv0, dv1)], [("row", GROUP_W, BF16)] * 3)
    dz, dwg, dbg = _gla_gates_bwd(p, dla0, dla1, wg, bg)
    dw_gate = jnp.stack([dwg[e * GLA_RANK:(e + 1) * GLA_RANK, e * GROUP_W:(e + 1) * GROUP_W] for e in range(2)])
    grads = dict(gla_w_gate=dw_gate, gla_b_gate=dbg.reshape(2, GROUP_W), gla_norm=dng[0])
    return (dq, dk, dv, dg, dz), grads, got


def _rope_tables(L):
    pos = jnp.arange(L, dtype=F32)
    inv_freq = ROPE_THETA ** (-jnp.arange(0, HEAD_DIM, 2, dtype=F32) / HEAD_DIM)
    ang = pos[:, None] * inv_freq[None, :]
    cos, sin = jnp.cos(ang), jnp.sin(ang)
    return jnp.tile(jnp.concatenate([cos, cos], axis=1), (1, 4)), jnp.tile(jnp.concatenate([-sin, sin], axis=1), (1, 4))


def _swap_halves(t):
    lane = lax.broadcasted_iota(jnp.int32, t.shape, 1)
    first = (lane & (HEAD_DIM - 1)) < HEAD_DIM // 2
    return jnp.where(first, pltpu.roll(t, GROUP_W - HEAD_DIM // 2, axis=1), pltpu.roll(t, HEAD_DIM // 2, axis=1))


def _attn_prep(p, cosf, sinf):
    dils = [dil for _, dil in DIL_PAIRS]

    def body(i, nt, qb, kb, vb, qd, kd, vd, c, s):
        d = (qd * c + _swap_halves(qd) * s, kd * c + _swap_halves(kd) * s, vd)
        return (qb, kb, vb) + d * len(dils)
    ins = [Row(p, GROUP_W, cb) for cb in (4, 5, 6, 9, 10, 11)] + [Row(cosf), Row(sinf)]
    outs = [("row", GROUP_W, BF16)] * 3 + [("row", GROUP_W, BF16, dil) for dil in dils for _ in range(3)]
    res = _rows("attn_prep", body, 512, ins, outs)
    return tuple(res[:3]), {dil: tuple(res[3 + 3 * n:6 + 3 * n]) for n, dil in enumerate(dils)}


def _na_onehot():
    c = np.arange(GRID_W)
    dc = np.clip(c[None, :] - c[:, None], -(NA_COLS - 1), NA_COLS - 1) + NA_COLS - 1
    oh = np.zeros((LANES, GRID_W * GRID_W), np.float32)
    oh[dc.reshape(-1), np.arange(GRID_W * GRID_W)] = 1.0
    return jnp.asarray(oh)


def _na_colmask():
    c = np.arange(GRID_W)
    start = np.clip(c - NA_COLS // 2, 0, GRID_W - NA_COLS)
    ok = (c[None, :] >= start[:, None]) & (c[None, :] < start[:, None] + NA_COLS)
    return jnp.asarray(np.where(ok, 0.0, NEG).astype(np.float32))


N_DR = 2 * NA_ROWS - 1


NA_HALF = GRID_W // 2
NA_KCOLS = 48
NA_WIN = NA_ROWS * NA_KCOLS
NA_ROWS_PER_STEP = 4
NA_BWD_ROWS_PER_STEP = 2


def _na_bias(rpb):
    rp = jnp.pad(rpb.reshape(4 * N_DR, 2 * NA_COLS - 1), ((0, GRID_W - 4 * N_DR), (0, LANES - 2 * NA_COLS + 1)))

    def expand(r_ref, oh_ref, o_ref):
        o_ref[...] = _dot_exact(r_ref[...], oh_ref[...])

    r = pl.pallas_call(expand, name="na_bias_expand",
                       out_shape=jax.ShapeDtypeStruct((GRID_W, GRID_W * GRID_W), F32))(rp, _na_onehot())
    r = r[:4 * N_DR].reshape(4, N_DR, GRID_W, GRID_W)

    def build(r_ref, m_ref, o_ref):
        for h in range(4):
            for c in range(NA_ROWS):
                for half in range(2):
                    q0, k0 = NA_HALF * half, 16 * half
                    for i in range(NA_ROWS):
                        o_ref[h, c, half, :, i * NA_KCOLS:(i + 1) * NA_KCOLS] = (
                            r_ref[h, i - c + NA_ROWS - 1, q0:q0 + NA_HALF, k0:k0 + NA_KCOLS]
                            + m_ref[q0:q0 + NA_HALF, k0:k0 + NA_KCOLS])

    return pl.pallas_call(build, name="na_bias_build",
                          out_shape=jax.ShapeDtypeStruct((4, NA_ROWS, 2, NA_HALF, NA_WIN), F32))(r, _na_colmask())


def _na_bias_bwd(dbias):
    def fold(d_ref, o_ref):
        o_ref[...] = jnp.zeros_like(o_ref)
        for h in range(4):
            for a in range(N_DR):
                for half in range(2):
                    q0, k0 = NA_HALF * half, 16 * half
                    acc = jnp.zeros((NA_HALF, NA_KCOLS), F32)
                    for c in range(NA_ROWS):
                        i = a + c - (NA_ROWS - 1)
                        if 0 <= i < NA_ROWS:
                            acc = acc + d_ref[h, c, half, :, i * NA_KCOLS:(i + 1) * NA_KCOLS]
                    o_ref[h, a, q0:q0 + NA_HALF, k0:k0 + NA_KCOLS] = acc

    dr = pl.pallas_call(fold, name="na_bias_fold",
                        out_shape=jax.ShapeDtypeStruct((4, N_DR, GRID_W, GRID_W), F32))(dbias)
    dr = jnp.pad(dr.reshape(4 * N_DR, GRID_W * GRID_W), ((0, GRID_W - 4 * N_DR), (0, 0)))

    def contract(d_ref, oh_ref, o_ref):
        o_ref[...] = lax.dot_general(d_ref[...], oh_ref[...], _NT, precision=lax.Precision.HIGHEST,
                                     preferred_element_type=F32)

    g = pl.pallas_call(contract, name="na_bias_contract",
                       out_shape=jax.ShapeDtypeStruct((GRID_W, LANES), F32))(dr, _na_onehot())
    return g[:4 * N_DR, :2 * NA_COLS - 1].reshape(4, N_DR, 2 * NA_COLS - 1)


def _na_window(r, n_rows):
    rs = jnp.clip(r - NA_ROWS // 2, 0, n_rows - NA_ROWS)
    return rs, r - rs


def _na_key_rows(rs, half, t):
    return pl.ds(pl.multiple_of((rs + t) * GRID_W + 16 * half, 16), NA_KCOLS)


def _na_keys(ref, rs, half):
    return jnp.concatenate([ref[_na_key_rows(rs, half, t), :] for t in range(NA_ROWS)], axis=0)


def _na_stack(x, first):
    zero = jnp.zeros_like(x)
    return jnp.concatenate([jnp.where(first, x, zero), jnp.where(first, zero, x)], axis=0)


def _na_bias_spec():
    return pl.BlockSpec((2, NA_ROWS, 2, NA_HALF, NA_WIN), lambda j, i: (j, 0, 0, 0, 0))


def _grid_edges(n0, n1):
    j, i = pl.program_id(0), pl.program_id(1)
    return jnp.logical_and(j == 0, i == 0), jnp.logical_and(j == n0 - 1, i == n1 - 1)


def _na_fwd(q, k, v, bias, ride=None):
    L = q.shape[0]
    n_rows = L // GRID_W
    tm = _tile(L, 512, GRID_W)
    nt, nr = L // tm, tm // GRID_W

    def kern(q_ref, k_ref, v_ref, b_ref, o_ref):
        i = pl.program_id(1)
        first = lax.broadcasted_iota(jnp.int32, (NA_HALF, LANES), 1) < HEAD_DIM

        def body(it, carry):
            parts = []
            for u in range(NA_ROWS_PER_STEP):
                rr = it * NA_ROWS_PER_STEP + u
                rs, c = _na_window(i * nr + rr, n_rows)
                for half in range(2):
                    rows = pl.ds(pl.multiple_of(rr * GRID_W + NA_HALF * half, NA_HALF), NA_HALF)
                    bias = jnp.concatenate([b_ref[0, c, half], b_ref[1, c, half]], axis=0)
                    parts.append((rows, _na_stack(q_ref[rows, :], first), bias, _na_keys(k_ref, rs, half),
                                  _na_keys(v_ref, rs, half)))
            s = [_dot(qs, kw, _NT) * ATT_SCALE + bias for _, qs, bias, kw, _ in parts]
            e = [jnp.exp(x - jnp.max(x, axis=-1, keepdims=True)) for x in s]
            pn = [(x / jnp.sum(x, axis=-1, keepdims=True)).astype(BF16) for x in e]
            o = [_dot(p, part[4]) for p, part in zip(pn, parts)]
            for x, (rows, *_) in zip(o, parts):
                o_ref[rows, :] = jnp.where(first, x[:NA_HALF], x[NA_HALF:]).astype(BF16)
            return carry

        lax.fori_loop(0, nr // NA_ROWS_PER_STEP, body, 0)

    qspec = pl.BlockSpec((tm, LANES), lambda j, i: (i, j))
    kvspec = pl.BlockSpec((L, LANES), lambda j, i: (0, j))
    (y,), got = _call(
        kern, (q, k, v, bias), ride, lambda: _grid_edges(2, nt), name="na_fwd", grid=(2, nt),
        in_specs=[qspec, kvspec, kvspec, _na_bias_spec()],
        out_specs=[qspec], out_shape=[jax.ShapeDtypeStruct((L, GROUP_W), BF16)], scratch_shapes=[])
    return y, got


def _na_bwd(dy, dy_block, q, k, v, bias, ride=None):
    L = q.shape[0]
    n_rows = L // GRID_W
    tm = _tile(L, 512, GRID_W)
    nt, nr = L // tm, tm // GRID_W

    def kern(dy_ref, q_ref, k_ref, v_ref, b_ref, dq_ref, dk_ref, dv_ref, db_ref):
        i = pl.program_id(1)

        @pl.when(i == 0)
        def _():
            dk_ref[...] = jnp.zeros_like(dk_ref)
            dv_ref[...] = jnp.zeros_like(dv_ref)
            db_ref[...] = jnp.zeros_like(db_ref)

        first = lax.broadcasted_iota(jnp.int32, (NA_HALF, LANES), 1) < HEAD_DIM

        def body(it, carry):
            parts = []
            for u in range(NA_BWD_ROWS_PER_STEP):
                rr = it * NA_BWD_ROWS_PER_STEP + u
                rs, c = _na_window(i * nr + rr, n_rows)
                for half in range(2):
                    rows = pl.ds(pl.multiple_of(rr * GRID_W + NA_HALF * half, NA_HALF), NA_HALF)
                    bias = jnp.concatenate([b_ref[0, c, half], b_ref[1, c, half]], axis=0)
                    parts.append((rows, half, _na_stack(q_ref[rows, :], first),
                                  _na_stack(dy_ref[rows, :].astype(BF16), first), bias, _na_keys(k_ref, rs, half),
                                  _na_keys(v_ref, rs, half), rs, c))
            s = [_dot(part[2], part[5], _NT) * ATT_SCALE + part[4] for part in parts]
            dp = [_dot(part[3], part[6], _NT) for part in parts]
            e = [jnp.exp(x - jnp.max(x, axis=-1, keepdims=True)) for x in s]
            pn = [x / jnp.sum(x, axis=-1, keepdims=True) for x in e]
            ds = [p * (d - jnp.sum(p * d, axis=-1, keepdims=True)) for p, d in zip(pn, dp)]
            dsb = [x.astype(BF16) for x in ds]
            pnb = [x.astype(BF16) for x in pn]
            dq = [_dot(x, part[5]) for x, part in zip(dsb, parts)]
            dk = [_dot(x, part[2], _TN) for x, part in zip(dsb, parts)]
            dv = [_dot(x, part[3], _TN) for x, part in zip(pnb, parts)]
            for n, (rows, half, _, _, _, _, _, rs, c) in enumerate(parts):
                db_ref[0, c, half] += ds[n][:NA_HALF]
                db_ref[1, c, half] += ds[n][NA_HALF:]
                dq_ref[rows, :] = (jnp.where(first, dq[n][:NA_HALF], dq[n][NA_HALF:]) * ATT_SCALE).astype(BF16)
                for t in range(NA_ROWS):
                    kr = _na_key_rows(rs, half, t)
                    dk_ref[kr, :] += dk[n][t * NA_KCOLS:(t + 1) * NA_KCOLS] * ATT_SCALE
                    dv_ref[kr, :] += dv[n][t * NA_KCOLS:(t + 1) * NA_KCOLS]
            return carry

        lax.fori_loop(0, nr // NA_BWD_ROWS_PER_STEP, body, 0)

    qspec = pl.BlockSpec((tm, LANES), lambda j, i: (i, j))
    kvspec = pl.BlockSpec((L, LANES), lambda j, i: (0, j))
    return _call(
        kern, (dy, q, k, v, bias), ride, lambda: _grid_edges(2, nt), name="na_bwd", grid=(2, nt),
        in_specs=[pl.BlockSpec((tm, LANES), lambda j, i: (i, dy_block + j)), qspec, kvspec, kvspec, _na_bias_spec()],
        out_specs=[qspec, kvspec, kvspec, _na_bias_spec()],
        out_shape=[jax.ShapeDtypeStruct((L, GROUP_W), BF16), jax.ShapeDtypeStruct((L, GROUP_W), F32),
                   jax.ShapeDtypeStruct((L, GROUP_W), F32),
                   jax.ShapeDtypeStruct((4, NA_ROWS, 2, NA_HALF, NA_WIN), F32)], scratch_shapes=[])


def _dil_specs(n, tq):
    R = DIL_RADIUS
    step, nb = tq // R, n // R
    main = pl.BlockSpec((tq, LANES), lambda j, i: (i, j))
    prev = pl.BlockSpec((R, LANES), lambda j, i: (jnp.maximum(i * step - 1, 0), j))
    nxt = pl.BlockSpec((R, LANES), lambda j, i: (jnp.minimum((i + 1) * step, nb - 1), j))
    return main, prev, nxt


def _dil_masks():
    R = DIL_RADIUS
    r = np.arange(2 * R)[:, None] & (R - 1)
    c = np.arange(3 * R)[None, :]
    band = np.abs(c - R - r) <= R
    ok = np.stack([band, band & (c >= R), band & (c < 2 * R), band & (c >= R) & (c < 2 * R)])
    return jnp.asarray(np.where(ok, 0.0, NEG).astype(np.float32))


def _dil_mask_spec():
    return pl.BlockSpec((4, 2 * DIL_RADIUS, 3 * DIL_RADIUS), lambda j, i: (0, 0, 0))


def _dil_mask(m_ref, i, sb, n_tiles, n_blocks):
    idx = 0
    if sb == 0:
        idx = idx + jnp.where(i == 0, 1, 0)
    if sb == n_blocks - 1:
        idx = idx + jnp.where(i == n_tiles - 1, 2, 0)
    return m_ref[idx]


def _dil_fwd(q, k, v, dil, ride=None):
    n = q.shape[0]
    tq = _tile(n, 256, DIL_RADIUS)

    def kern(q_ref, kp_ref, k_ref, kn_ref, vp_ref, v_ref, vn_ref, m_ref, o_ref, l_ref):
        i = pl.program_id(1)
        R = DIL_RADIUS
        ka = jnp.concatenate([kp_ref[...], k_ref[...], kn_ref[...]], axis=0)
        va = jnp.concatenate([vp_ref[...], v_ref[...], vn_ref[...]], axis=0)
        first = lax.broadcasted_iota(jnp.int32, (R, LANES), 1) < HEAD_DIM
        subs = range(tq // R)
        keys = lambda a, sb: a[sb * R:(sb + 3) * R]
        qs = [_na_stack(q_ref[sb * R:(sb + 1) * R, :], first) for sb in subs]
        s = [_dot(qs[sb], keys(ka, sb), _NT) for sb in subs]
        s = [s[sb] * ATT_SCALE + _dil_mask(m_ref, i, sb, n // tq, len(subs)) for sb in subs]
        m = [jnp.max(x, axis=-1, keepdims=True) for x in s]
        e = [jnp.exp(x - mx) for x, mx in zip(s, m)]
        den = [jnp.sum(x, axis=-1, keepdims=True) for x in e]
        o = [_dot((e[sb] / den[sb]).astype(BF16), keys(va, sb)) for sb in subs]
        for sb in subs:
            lse = m[sb] + jnp.log(den[sb])
            o_ref[sb * R:(sb + 1) * R, :] = jnp.where(first, o[sb][:R], o[sb][R:])
            l_ref[sb * R:(sb + 1) * R, :] = jnp.where(first, lse[:R], lse[R:])

    main, prev, nxt = _dil_specs(n, tq)
    (o, lse), got = _call(
        kern, (q, k, k, k, v, v, v, _dil_masks()), ride,
        lambda: _grid_edges(2 * dil, n // tq), name=f"dil_fwd_{dil}", grid=(2 * dil, n // tq),
        in_specs=[main, prev, main, nxt, prev, main, nxt, _dil_mask_spec()], out_specs=[main, main],
        out_shape=[jax.ShapeDtypeStruct((n, dil * GROUP_W), F32)] * 2, scratch_shapes=[])
    return (o, lse), got


def _dil_bwd(q, k, v, do, lse, dterm, dil, ride=None):
    n = q.shape[0]
    R = DIL_RADIUS
    tq = _tile(n, 256, R)
    nq = n // tq

    def kern(q_ref, kp_ref, k_ref, kn_ref, vp_ref, v_ref, vn_ref, do_ref, l_ref, dt_ref, m_ref, dq_ref, dk_ref, dv_ref):
        i = pl.program_id(1)

        @pl.when(i == 0)
        def _():
            dk_ref[...] = jnp.zeros_like(dk_ref)
            dv_ref[...] = jnp.zeros_like(dv_ref)

        ka = jnp.concatenate([kp_ref[...], k_ref[...], kn_ref[...]], axis=0)
        va = jnp.concatenate([vp_ref[...], v_ref[...], vn_ref[...]], axis=0)
        first = lax.broadcasted_iota(jnp.int32, (R, LANES), 1) < HEAD_DIM
        subs = range(tq // R)
        keys = lambda a, sb: a[sb * R:(sb + 3) * R]
        rows = lambda ref, sb: ref[sb * R:(sb + 1) * R, :]
        per_head = lambda t: jnp.concatenate([t[:, 0:1], t[:, HEAD_DIM:HEAD_DIM + 1]], axis=0)
        qs = [_na_stack(rows(q_ref, sb), first) for sb in subs]
        dos = [_na_stack(rows(do_ref, sb), first) for sb in subs]
        s = [_dot(qs[sb], keys(ka, sb), _NT) for sb in subs]
        dp = [_dot(dos[sb], keys(va, sb), _NT) for sb in subs]
        pn = [jnp.exp(s[sb] * ATT_SCALE + _dil_mask(m_ref, i, sb, nq, len(subs)) - per_head(rows(l_ref, sb))) for sb in subs]
        dsb = [(pn[sb] * (dp[sb] - per_head(rows(dt_ref, sb)))).astype(BF16) for sb in subs]
        pnb = [x.astype(BF16) for x in pn]
        dq = [_dot(dsb[sb], keys(ka, sb)) for sb in subs]
        dk = [_dot(dsb[sb], qs[sb], _TN) for sb in subs]
        dv = [_dot(pnb[sb], dos[sb], _TN) for sb in subs]
        zeros = lambda blocks: [jnp.zeros((blocks * R, LANES), F32)] if blocks else []
        pad = lambda t, sb: jnp.concatenate(zeros(sb) + [t] + zeros(len(subs) - 1 - sb), axis=0)
        dka = sum(pad(dk[sb], sb) for sb in subs) * ATT_SCALE
        dva = sum(pad(dv[sb], sb) for sb in subs)
        for sb in subs:
            dq_ref[sb * R:(sb + 1) * R, :] = jnp.where(first, dq[sb][:R], dq[sb][R:]) * ATT_SCALE
        r0 = pl.multiple_of(i * tq, R)
        dk_ref[pl.ds(r0, tq), :] += dka[R:R + tq]
        dv_ref[pl.ds(r0, tq), :] += dva[R:R + tq]

        @pl.when(i > 0)
        def _():
            dk_ref[pl.ds(r0 - R, R), :] += dka[:R]
            dv_ref[pl.ds(r0 - R, R), :] += dva[:R]

        @pl.when(i < nq - 1)
        def _():
            dk_ref[pl.ds(r0 + tq, R), :] += dka[R + tq:]
            dv_ref[pl.ds(r0 + tq, R), :] += dva[R + tq:]

    main, prev, nxt = _dil_specs(n, tq)
    whole = pl.BlockSpec((n, LANES), lambda j, i: (0, j))
    shp = jax.ShapeDtypeStruct((n, dil * GROUP_W), F32)
    (dq, dk, dv), got = _call(
        kern, (q, k, k, k, v, v, v, do, lse, dterm, _dil_masks()), ride,
        lambda: _grid_edges(2 * dil, nq), name=f"dil_bwd_{dil}", grid=(2 * dil, nq),
        in_specs=[main, prev, main, nxt, prev, main, nxt, main, main, main, _dil_mask_spec()],
        out_specs=[main, whole, whole],
        out_shape=[shp] * 3, scratch_shapes=[])
    return (dq, dk, dv), got


def _dil_weights(lses):
    m = jnp.maximum(jnp.maximum(lses[0], lses[1]), lses[2])
    e = [jnp.exp(l - m) for l in lses]
    tot = e[0] + e[1] + e[2]
    return [x / tot for x in e]


def _dilated_fwd(qkv, rides):
    dils = [dil for _, dil in DIL_PAIRS]
    res, got = [], {}
    for dil in dils:
        r, got[f"dil{dil}"] = _dil_fwd(*qkv[dil], dil, rides.get(f"dil{dil}"))
        res.append(r)

    def body(i, nt, o0, o1, o2, l0, l1, l2):
        w = _dil_weights((l0, l1, l2))
        return w[0] * o0 + w[1] * o1 + w[2] * o2

    ins = [Row(r[0], dil=d) for r, d in zip(res, dils)] + [Row(r[1], dil=d) for r, d in zip(res, dils)]
    return _rows("dil_combine", body, 512, ins, [("row", GROUP_W, BF16)])[0], res, got


def _dilated_bwd(dy, dy_cb, qkv, saved, cosf, sinf, rides):
    dils = [dil for _, dil in DIL_PAIRS]
    def split(i, nt, dy, o0, o1, o2, l0, l1, l2, bd):
        w = _dil_weights((l0, l1, l2))
        y = w[0] * o0 + w[1] * o1 + w[2] * o2
        dyy = _head_sum(dy * y, bd)
        return tuple(wg * dy for wg in w) + tuple(wg * dyy for wg in w)

    ins = ([Row(dy, GROUP_W, dy_cb)] + [Row(r[0], dil=d) for r, d in zip(saved, dils)]
           + [Row(r[1], dil=d) for r, d in zip(saved, dils)])
    outs = _rows("dil_split_bwd", split, 512, ins + [Full(_block_ones(GROUP_W, HEAD_DIM))],
                 [("row", GROUP_W, BF16, d) for d in dils] + [("row", GROUP_W, F32, d) for d in dils])
    g, got = [], {}
    for b, dil in enumerate(dils):
        r, got[f"dil{dil}"] = _dil_bwd(*qkv[dil], outs[b], saved[b][1], outs[3 + b], dil, rides.get(f"dil{dil}"))
        g.append(r)

    def finish(i, nt, q0, q1, q2, k0, k1, k2, v0, v1, v2, c, s):
        dq, dk = q0 + q1 + q2, k0 + k1 + k2
        return dq * c + _swap_halves(dq * s), dk * c + _swap_halves(dk * s), v0 + v1 + v2

    ins = [Row(g[b][t], dil=dils[b]) for t in range(3) for b in range(3)] + [Row(cosf), Row(sinf)]
    return _rows("dil_finish_bwd", finish, 512, ins, [("row", GROUP_W, BF16)] * 3), got


def _layer_fwd(x, W, l, cosf, sinf, rides):
    rides = {c: Exchange(items) for c, items in rides.items()}
    h1 = _rms_fwd(x, W["mix_norm_pre"][l][None], "mix_norm")
    p = _mm(h1, W["w_in"][l], "nn", F32, "proj_in")
    ya, sa, got_gla = _gla_fwd(p, W, l, rides.get("gla"))
    (qb, kb, vb), qkv_d = _attn_prep(p, cosf, sinf)
    bias = _na_bias(W["na_rpb"][l])
    yb, got_na = _na_fwd(qb, kb, vb, bias, rides.get("na"))
    yc, sc = _lru_fwd(p, W, l)
    yd, sd, got = _dilated_fwd(qkv_d, rides)
    got.update(gla=got_gla, na=got_na)
    ycat = jnp.concatenate([ya, yb, yc, yd], axis=1)
    ymix = _mm(ycat, W["w_out"][l], "nn", F32, "proj_out", tm=1024)
    xm = _rms_resid_fwd(x, ymix, W["mix_norm_post"][l][None], "mix_resid")
    h2 = _rms_fwd(xm, W["ffn_norm_pre"][l][None], "ffn_norm")
    gu, act = _ffn_in_swiglu(h2, W["ffn_w_in"][l])
    f = _mm(act, W["ffn_w_out"][l], "nn", F32, "ffn_out")
    xo = _rms_resid_fwd(xm, f, W["ffn_norm_post"][l][None], "ffn_resid")
    saved = dict(x=x, h1=h1, p=p, sa=sa, att=(qb, kb, vb, qkv_d), bias=bias, sc=sc, sd=sd, ycat=ycat, ymix=ymix,
                 xm=xm, h2=h2, gu=gu, act=act, f=f)
    return xo, saved, got


def _layer_bwd(dxo, W, l, S, cosf, sinf, rides, early=None):
    g = {}
    df, g["ffn_norm_post"] = _rms_bwd(dxo, S["f"], W["ffn_norm_post"][l][None], "ffn_resid_bwd", out_dtype=BF16)
    g["ffn_w_out"] = _mm(S["act"], df, "tn", BF16, "ffn_out_dw", tm=256, tk=4096)
    dgu = _ffn_out_dx_swiglu(df, W["ffn_w_out"][l], S["gu"])
    dh2 = _mm(dgu, W["ffn_w_in"][l], "nt", F32, "ffn_in_dx")
    g["ffn_w_in"] = _mm(S["h2"], dgu, "tn", BF16, "ffn_in_dw", tm=1024, tn=512, tk=4096)
    dxm, g["ffn_norm_pre"] = _rms_bwd(dh2, S["xm"], W["ffn_norm_pre"][l][None], "ffn_norm_bwd", resid=dxo)
    dymix, g["mix_norm_post"] = _rms_bwd(dxm, S["ymix"], W["mix_norm_post"][l][None], "mix_resid_bwd", out_dtype=BF16)
    dycat = _mm(dymix, W["w_out"][l], "nt", F32, "proj_out_dx", tm=1024)
    g["w_out"] = _mm(S["ycat"], dymix, "tn", BF16, "proj_out_dw", tm=1024, tn=512, tk=4096)
    if early is not None:
        for c, items in early(g).items():
            rides = {**rides, c: rides.get(c, []) + items}
    rides = {c: Exchange(items) for c, items in rides.items()}
    p = S["p"]
    qb, kb, vb, qkv_d = S["att"]
    (dqa, dka, dva, dga, dz), ga, got_gla = _gla_bwd(dycat, 0, p, W, l, S["sa"], rides.get("gla"))
    (dqb, dkb, dvb, dbias), got_na = _na_bwd(dycat, 2, qb, kb, vb, S["bias"], rides.get("na"))
    g["na_rpb"] = _na_bias_bwd(dbias)
    dxc, dgc, gc = _lru_bwd(dycat, 2, p, W, l, S["sc"])
    (dqd, dkd, dvd), got = _dilated_bwd(dycat, 3, qkv_d, S["sd"], cosf, sinf, rides)
    got.update(gla=got_gla, na=got_na)
    g.update(ga)
    g.update(gc)
    dp = jnp.concatenate([dqa, dka, dva, dga, dqb, dkb.astype(BF16), dvb.astype(BF16), dxc, dgc, dqd, dkd, dvd, dz], axis=1)
    dh1 = _mm(dp, W["w_in"][l], "nt", F32, "proj_in_dx")
    g["w_in"] = _mm(S["h1"], dp, "tn", BF16, "proj_in_dw", tm=1024, tn=640, tk=4096)
    dx, g["mix_norm_pre"] = _rms_bwd(dh1, S["x"], W["mix_norm_pre"][l][None], "mix_norm_bwd", resid=dxm)
    for n in ("ffn_norm_post", "ffn_norm_pre", "mix_norm_post", "mix_norm_pre"):
        g[n] = g[n][0]
    return dx, g, got


MESH_AXES = ("x", "y", "c")


class Xfer:
    def __init__(self, arr, kind):
        self.arr, self.kind = arr, kind
        shp = arr.shape
        if kind == "all":
            self.out = (N_DEV,) + shp
        elif kind == "slot":
            self.out = shp
        elif kind == "rows":
            self.r = shp[1] // N_DEV
            self.out = (N_DEV, shp[0], self.r, shp[2])
        else:
            self.r = shp[1]
            self.out = (shp[0], N_DEV * shp[1], shp[2])

    def src(self, ref, peer):
        if self.kind == "slot":
            return ref.at[peer]
        if self.kind == "rows":
            return ref.at[:, pl.ds(peer * self.r, self.r), :]
        return ref

    def dst(self, ref, me):
        if self.kind == "place":
            return ref.at[:, pl.ds(me * self.r, self.r), :]
        return ref.at[me]


class Exchange:
    def __init__(self, items):
        n = len(items)
        self.items = items
        self.arrays = [it.arr for it in items]
        self.specs = [pl.BlockSpec(memory_space=pl.ANY)] * n
        self.out_shape = [jax.ShapeDtypeStruct(it.out, it.arr.dtype) for it in items]
        self.scratch = [pltpu.SemaphoreType.DMA((n * (N_DEV - 1),)), pltpu.SemaphoreType.DMA((n * (N_DEV - 1),)),
                        pltpu.SemaphoreType.DMA((n,))]

    def copies(self, ins, outs, sems):
        send_sems, recv_sems, local_sems = sems
        x, y, c = (lax.axis_index(a) for a in MESH_AXES)
        me = 4 * x + 2 * y + c
        out = []
        for t, it in enumerate(self.items):
            out.append(pltpu.make_async_copy(it.src(ins[t], me), it.dst(outs[t], me), local_sems.at[t]))
            for k in range(1, N_DEV):
                px, py, pc = x ^ ((k >> 2) & 1), y ^ ((k >> 1) & 1), c ^ (k & 1)
                s = t * (N_DEV - 1) + k - 1
                out.append(pltpu.make_async_remote_copy(
                    src_ref=it.src(ins[t], 4 * px + 2 * py + pc), dst_ref=it.dst(outs[t], me),
                    send_sem=send_sems.at[s], recv_sem=recv_sems.at[s], device_id=(px, py, pc),
                    device_id_type=pl.DeviceIdType.MESH))
        return out

    def start(self, ins, outs, sems):
        for cp in self.copies(ins, outs, sems):
            cp.start()

    def wait(self, ins, outs, sems):
        for cp in self.copies(ins, outs, sems):
            cp.wait()


def _exchange(items, name):
    ex = Exchange(items)
    n = len(items)

    def body(*refs):
        ex.start(refs[:n], refs[n:2 * n], refs[2 * n:])
        ex.wait(refs[:n], refs[n:2 * n], refs[2 * n:])

    return pl.pallas_call(body, name=name, out_shape=ex.out_shape, in_specs=ex.specs, out_specs=ex.specs,
                          scratch_shapes=ex.scratch)(*ex.arrays)


def _call(kern, arrays, ride, edges, *, name, grid, in_specs, out_specs, out_shape, scratch_shapes):
    params = _ARB(len(grid))
    if ride is None:
        return pl.pallas_call(kern, name=name, grid=grid, in_specs=in_specs, out_specs=out_specs, out_shape=out_shape,
                              scratch_shapes=scratch_shapes, compiler_params=params)(*arrays), None
    ni, no, ns, nx = len(in_specs), len(out_specs), len(scratch_shapes), len(ride.items)

    def wrapped(*refs):
        ins, xin = refs[:ni], refs[ni:ni + nx]
        outs, xout = refs[ni + nx:ni + nx + no], refs[ni + nx + no:ni + 2 * nx + no]
        scr, sems = refs[ni + 2 * nx + no:ni + 2 * nx + no + ns], refs[ni + 2 * nx + no + ns:]
        first, last = edges()

        @pl.when(first)
        def _():
            ride.start(xin, xout, sems)

        kern(*ins, *outs, *scr)

        @pl.when(last)
        def _():
            ride.wait(xin, xout, sems)

    res = pl.pallas_call(
        wrapped, name=name, grid=grid, in_specs=list(in_specs) + ride.specs, out_specs=list(out_specs) + ride.specs,
        out_shape=list(out_shape) + ride.out_shape, scratch_shapes=list(scratch_shapes) + ride.scratch,
        compiler_params=params)(*arrays, *ride.arrays)
    return res[:no], res[no:]


def _column_segments(width, permuted):
    z0, z1, zn = 4 * GROUP_W, 4 * GROUP_W + 2 * GLA_RANK, 12 * GROUP_W
    segs = []
    for d in range(N_DEV):
        lo, hi = d * width, (d + 1) * width
        if not permuted:
            segs.append([(0, width, lo)])
            continue
        runs = []
        for a, b, shift in ((0, z0, 0), (z0, z1, zn - z0), (z1, 10 ** 9, -(z1 - z0))):
            s, e = max(lo, a), min(hi, b)
            if s < e:
                runs.append((s - lo, e - lo, s + shift))
        segs.append(runs)
    return segs


def _cols_from_pieces(pieces, segs, cols, name):
    _, R, w = pieces.shape
    tm = _tile(R, 256, 16)
    used = max(f + (b - a) for runs in segs for a, b, f in runs)

    def kern(p_ref, o_ref):
        for d, runs in enumerate(segs):
            for a, b, f in runs:
                o_ref[:, f:f + (b - a)] = p_ref[d, :, a:b]
        if used < cols:
            o_ref[:, used:cols] = jnp.zeros((tm, cols - used), o_ref.dtype)

    return pl.pallas_call(
        kern, name=name, grid=(R // tm,), in_specs=[pl.BlockSpec((N_DEV, tm, w), lambda i: (0, i, 0))],
        out_specs=pl.BlockSpec((tm, cols), lambda i: (i, 0)), out_shape=jax.ShapeDtypeStruct((R, cols), pieces.dtype),
        compiler_params=_ARB(1),
    )(pieces)


def _pieces_from_cols(full, segs, w, name):
    R, cols = full.shape
    tm = _tile(R, 256, 16)

    def kern(f_ref, o_ref):
        for d, runs in enumerate(segs):
            for a, b, f in runs:
                o_ref[d, :, a:b] = f_ref[:, f:f + (b - a)]

    return pl.pallas_call(
        kern, name=name, grid=(R // tm,), in_specs=[pl.BlockSpec((tm, cols), lambda i: (i, 0))],
        out_specs=pl.BlockSpec((N_DEV, tm, w), lambda i: (0, i, 0)),
        out_shape=jax.ShapeDtypeStruct((N_DEV, R, w), full.dtype), compiler_params=_ARB(1),
    )(full)


def _sum_slots(recv, name):
    n, R, C = recv.shape
    tm = _tile(R, 256, 16)

    def kern(*refs):
        acc = refs[0][...].astype(F32)
        for r in refs[1:n]:
            acc = acc + r[...].astype(F32)
        refs[n][...] = acc

    return pl.pallas_call(
        kern, name=name, grid=(R // tm,),
        in_specs=[pl.BlockSpec((None, tm, C), lambda i, _s=s: (_s, i, 0)) for s in range(n)],
        out_specs=pl.BlockSpec((tm, C), lambda i: (i, 0)), out_shape=jax.ShapeDtypeStruct((R, C), F32),
        compiler_params=_ARB(1),
    )(*([recv] * n))


BIG = (("w_in", 2), ("w_out", 1), ("ffn_w_in", 2), ("ffn_w_out", 1))
SMALL_SHARDED = ("gla_w_gate", "gla_b_gate", "lru_conv_w", "lru_b_a", "lru_b_x", "lru_lambda")
REPLICATED = ("mix_norm_pre", "mix_norm_post", "gla_norm", "na_rpb", "lru_conv_b", "lru_w_a", "lru_w_x",
              "ffn_norm_pre", "ffn_norm_post")
WEIGHTS = ("mix_norm_pre", "mix_norm_post", "w_in", "gla_w_gate", "gla_b_gate", "gla_norm", "na_rpb", "lru_conv_w",
           "lru_conv_b", "lru_w_a", "lru_b_a", "lru_w_x", "lru_b_x", "lru_lambda", "w_out", "ffn_norm_pre",
           "ffn_norm_post", "ffn_w_in", "ffn_w_out")
FLAT_C = 1024


def _to_rows(vec, row_unit):
    n = vec.shape[-1]
    rows = -(-n // (FLAT_C * row_unit)) * row_unit
    pad = [(0, 0)] * (vec.ndim - 1) + [(0, rows * FLAT_C - n)]
    return jnp.pad(vec, pad).reshape(vec.shape[:-1] + (rows, FLAT_C))


def _unshard(parts, axis):
    t = jnp.moveaxis(parts, 0, axis)
    shp = list(t.shape)
    return t.reshape(shp[:axis] + [shp[axis] * shp[axis + 1]] + shp[axis + 2:])


def _shards(full, axis):
    shp = list(full.shape)
    t = full.reshape(shp[:axis] + [N_DEV, shp[axis] // N_DEV] + shp[axis + 1:])
    return jnp.moveaxis(t, axis, 0)


def _weight_rides(W, l):
    bf = lambda n: W[n][l].astype(BF16)
    ffn = bf("ffn_w_in")
    half = ffn.shape[0] // 2
    return {"gla": [Xfer(bf("w_in"), "all"), Xfer(bf("w_out")[None], "place")],
            "na": [Xfer(bf("ffn_w_out")[None], "place")],
            "dil1": [Xfer(ffn[:half], "all")], "dil4": [Xfer(ffn[half:], "all")]}


def _unpack_weights(full, W, got):
    w_in_w, ffn_w = W["w_in"].shape[-1], W["ffn_w_in"].shape[-1]
    full["w_in"].append(_cols_from_pieces(got["gla"][0], _column_segments(w_in_w, True), P_COLS, "unpack_w_in"))
    full["w_out"].append(got["gla"][1][0])
    full["ffn_w_out"].append(got["na"][0][0])
    full["ffn_w_in"].append(jnp.concatenate(
        [_cols_from_pieces(got[c][0], _column_segments(ffn_w, False), N_DEV * ffn_w, "unpack_ffn_w_in")
         for c in ("dil1", "dil4")], axis=0))


def _w_in_grad_pieces(g, W):
    w_in_w = W["w_in"].shape[-1]
    return _pieces_from_cols(g["w_in"], _column_segments(w_in_w, True), w_in_w, "pack_w_in")


def _ffn_grad_halves(g, W):
    ffn_w = W["ffn_w_in"].shape[-1]
    p_ffn = _pieces_from_cols(g["ffn_w_in"], _column_segments(ffn_w, False), ffn_w, "pack_ffn_w_in")
    half = p_ffn.shape[1] // 2
    return p_ffn[:, :half], p_ffn[:, half:]


def _grad_rides(g, W):
    top, bottom = _ffn_grad_halves(g, W)
    return {"gla": [Xfer(_w_in_grad_pieces(g, W), "slot"), Xfer(g["w_out"][None], "rows")],
            "na": [Xfer(g["ffn_w_out"][None], "rows"), Xfer(top, "slot")], "dil1": [Xfer(bottom, "slot")]}


def _early_grad_rides(g, W):
    top, bottom = _ffn_grad_halves(g, W)
    return {"gla": [Xfer(g["w_out"][None], "rows"), Xfer(g["ffn_w_out"][None], "rows")], "na": [Xfer(top, "slot")],
            "dil4": [Xfer(bottom, "slot")]}


def _sum_big(w_in, w_out, ffn_w_out, ffn_top, ffn_bottom):
    s = lambda r, n: _sum_slots(r.reshape(N_DEV, -1, r.shape[-1]), "sum_" + n)
    return {"w_in": s(w_in, "w_in"), "w_out": s(w_out, "w_out"), "ffn_w_out": s(ffn_w_out, "ffn_w_out"),
            "ffn_w_in": jnp.concatenate([s(ffn_top, "ffn_w_in"), s(ffn_bottom, "ffn_w_in")], axis=0)}


def _exchange_named(rides, extra, name):
    names = list(rides)
    res = _exchange([it for n in names for it in rides[n]] + extra, name)
    got, at = {}, 0
    for n in names:
        got[n] = res[at:at + len(rides[n])]
        at += len(rides[n])
    return got, res[at:]


def _train(x, target, W):
    L = x.shape[0]
    depth = W["w_in"].shape[0]
    cosf, sinf = _rope_tables(L)
    small = jnp.concatenate([W[n].reshape(-1) for n in SMALL_SHARDED])
    small16 = _to_rows(lax.bitcast_convert_type(small, jnp.uint16).reshape(-1), 16)
    got, (sm,) = _exchange_named(_weight_rides(W, 0), [Xfer(small16, "all")], "gather_first")
    full = dict(W, w_in=[], w_out=[], ffn_w_in=[], ffn_w_out=[])
    _unpack_weights(full, W, got)
    sm = lax.bitcast_convert_type(sm.reshape(N_DEV, -1)[:, :2 * small.size].reshape(N_DEV, small.size, 2), F32)
    off = 0
    for n in SMALL_SHARDED:
        full[n] = _unshard(sm[:, off:off + W[n].size].reshape((N_DEV,) + W[n].shape), W[n].ndim - 1)
        off += W[n].size

    saved = []
    for l in range(depth):
        x, S, got = _layer_fwd(x, full, l, cosf, sinf, _weight_rides(W, l + 1) if l + 1 < depth else {})
        saved.append(S)
        if l + 1 < depth:
            _unpack_weights(full, W, got)
    loss, dx = _loss_fwd_bwd(x, target)

    grads, big, rides = [None] * depth, [None] * depth, {}
    for l in reversed(range(depth)):
        early = (lambda g: _early_grad_rides(g, W)) if l == 0 else None
        dx, grads[l], got = _layer_bwd(dx, full, l, saved[l], cosf, sinf, rides, early)
        if l + 1 < depth:
            big[l + 1] = _sum_big(got["gla"][0], got["gla"][1], got["na"][0], got["na"][1], got["dil1"][0])
        late = {c: len(items) for c, items in rides.items()}
        rides = _grad_rides(grads[l], W) if l > 0 else {}
    G = {n: jnp.stack([g[n] for g in grads]) for n in SMALL_SHARDED + REPLICATED}
    small_g = jnp.concatenate([_shards(G[n], G[n].ndim - 1).reshape(N_DEV, -1) for n in SMALL_SHARDED], axis=1)
    repl_g = jnp.concatenate([G[n].reshape(-1) for n in REPLICATED])
    rest = _exchange([Xfer(_w_in_grad_pieces(grads[0], W), "slot"), Xfer(_to_rows(small_g, 8), "slot"),
                      Xfer(_to_rows(repl_g, 8), "all")], "exchange_last")
    own = lambda c, k: got[c][late.get(c, 0) + k]
    big[0] = _sum_big(rest[0], own("gla", 0), own("gla", 1), own("na", 0), own("dil4", 0))
    rest = rest[1:]
    out = {n: jnp.stack([b[n] for b in big]).reshape(W[n].shape) for n, _ in BIG}
    for names, r, tag in ((SMALL_SHARDED, rest[0], "sum_small"), (REPLICATED, rest[1], "sum_replicated")):
        flat, off = _sum_slots(r, tag).reshape(-1), 0
        for n in names:
            out[n] = flat[off:off + W[n].size].reshape(W[n].shape)
            off += W[n].size
    return loss, dx, out


def _update(W, G, M, V):
    delta, new_m, new_v = {}, {}, {}
    for n, _ in BIG:
        two_d = lambda a: a.reshape(-1, a.shape[-1])
        d, m, v = _adamw(two_d(W[n]), two_d(G[n]), two_d(M[n]), two_d(V[n]), "adamw_" + n)
        delta[n], new_m[n], new_v[n] = (t.reshape(W[n].shape) for t in (d, m, v))
    rest = SMALL_SHARDED + REPLICATED
    pack = lambda D: _to_rows(jnp.concatenate([D[n].reshape(-1) for n in rest]), 16)
    d, m, v = _adamw(pack(W), pack(G), pack(M), pack(V), "adamw_small")
    off = 0
    for n in rest:
        sl = lambda t: t.reshape(-1)[off:off + W[n].size].reshape(W[n].shape)
        delta[n], new_m[n], new_v[n] = sl(d), sl(m), sl(v)
        off += W[n].size
    return delta, new_m, new_v


def kernel(x, mix_norm_pre, mix_norm_post, w_in, gla_w_gate, gla_b_gate, gla_norm, na_rpb, lru_conv_w, lru_conv_b, lru_w_a, lru_b_a, lru_w_x, lru_b_x, lru_lambda, w_out, ffn_norm_pre, ffn_norm_post, ffn_w_in, ffn_w_out, loss_target, m_mix_norm_pre, m_mix_norm_post, m_w_in, m_gla_w_gate, m_gla_b_gate, m_gla_norm, m_na_rpb, m_lru_conv_w, m_lru_conv_b, m_lru_w_a, m_lru_b_a, m_lru_w_x, m_lru_b_x, m_lru_lambda, m_w_out, m_ffn_norm_pre, m_ffn_norm_post, m_ffn_w_in, m_ffn_w_out, v_mix_norm_pre, v_mix_norm_post, v_w_in, v_gla_w_gate, v_gla_b_gate, v_gla_norm, v_na_rpb, v_lru_conv_w, v_lru_conv_b, v_lru_w_a, v_lru_b_a, v_lru_w_x, v_lru_b_x, v_lru_lambda, v_w_out, v_ffn_norm_pre, v_ffn_norm_post, v_ffn_w_in, v_ffn_w_out):
    W = dict(zip(WEIGHTS, (mix_norm_pre, mix_norm_post, w_in, gla_w_gate, gla_b_gate, gla_norm, na_rpb, lru_conv_w, lru_conv_b, lru_w_a, lru_b_a, lru_w_x, lru_b_x, lru_lambda, w_out, ffn_norm_pre, ffn_norm_post, ffn_w_in, ffn_w_out)))
    M = dict(zip(WEIGHTS, (m_mix_norm_pre, m_mix_norm_post, m_w_in, m_gla_w_gate, m_gla_b_gate, m_gla_norm, m_na_rpb, m_lru_conv_w, m_lru_conv_b, m_lru_w_a, m_lru_b_a, m_lru_w_x, m_lru_b_x, m_lru_lambda, m_w_out, m_ffn_norm_pre, m_ffn_norm_post, m_ffn_w_in, m_ffn_w_out)))
    V = dict(zip(WEIGHTS, (v_mix_norm_pre, v_mix_norm_post, v_w_in, v_gla_w_gate, v_gla_b_gate, v_gla_norm, v_na_rpb, v_lru_conv_w, v_lru_conv_b, v_lru_w_a, v_lru_b_a, v_lru_w_x, v_lru_b_x, v_lru_lambda, v_w_out, v_ffn_norm_pre, v_ffn_norm_post, v_ffn_w_in, v_ffn_w_out)))
    loss, dx, G = _train(x[0], loss_target[0], W)
    loss = lax.psum(loss, MESH_AXES)
    delta, new_m, new_v = _update(W, G, M, V)
    return (loss, dx[None], *[G[n] for n in WEIGHTS], *[delta[n] for n in WEIGHTS], *[new_m[n] for n in WEIGHTS],
            *[new_v[n] for n in WEIGHTS])
```

```python
import functools
import math

import numpy as np
import jax
import jax.numpy as jnp
from jax import lax
from jax.experimental import pallas as pl
from jax.experimental.pallas import tpu as pltpu

F32 = jnp.float32
BF16 = jnp.bfloat16

N_DEV = 8
HEAD_DIM = 64
GROUP_W = 256
GLA_RANK = 16
GLA_TAU = 16.0
GLA_CHUNK = 64
GRID_W = 64
NA_ROWS = 8
NA_COLS = 16
LRU_C = 8.0
DIL_PAIRS = ((128, 1), (512, 4), (2048, 16))
DIL_RADIUS = 64
DIL_TILE = 512
ROPE_THETA = 10000.0
EPS = 1e-6
ATT_SCALE = HEAD_DIM ** -0.5
NEG = -1e30
LANES = 128
P_COLS = 12 * GROUP_W + LANES
Z_BLOCK = 12 * GROUP_W // LANES

ADAM_LR = 0.001
ADAM_B1 = 0.9
ADAM_B2 = 0.999
ADAM_EPS = 1e-08
ADAM_WD = 0.01
ADAM_STEP = 10

VMEM_LIMIT = 56 * 1024 * 1024
_ARB = lambda n: pltpu.CompilerParams(dimension_semantics=("arbitrary",) * n, vmem_limit_bytes=VMEM_LIMIT)


def _tile(dim, pref, unit):
    t = min(pref, dim) // unit * unit
    while t >= unit:
        if dim % t == 0:
            return t
        t -= unit
    return dim


def _mm(a, b, mode, out_dtype, name, tm=512, tn=None, tk=None):
    if mode == "nn":
        (M, K), (_, N) = a.shape, b.shape
    elif mode == "nt":
        (M, K), (N, _) = a.shape, b.shape
    else:
        (K, M), (_, N) = a.shape, b.shape
    tm = _tile(M, tm, LANES if mode == "tn" else 8)
    tn = _tile(N, tn or N, LANES)
    tk = _tile(K, tk or K, LANES)
    nk = K // tk
    dims = {"nn": (((1,), (0,)), ((), ())), "nt": (((1,), (1,)), ((), ())), "tn": (((0,), (0,)), ((), ()))}[mode]

    def kern(a_ref, b_ref, o_ref, *acc):
        part = lax.dot_general(a_ref[...].astype(BF16), b_ref[...].astype(BF16), dims, preferred_element_type=F32)
        if nk == 1:
            o_ref[...] = part.astype(out_dtype)
            return
        k = pl.program_id(2)

        @pl.when(k == 0)
        def _():
            acc[0][...] = part

        @pl.when(jnp.logical_and(k > 0, k < nk - 1))
        def _():
            acc[0][...] += part

        @pl.when(k == nk - 1)
        def _():
            o_ref[...] = (acc[0][...] + part).astype(out_dtype)

    a_spec = pl.BlockSpec((tk, tm), lambda i, j, k: (k, i)) if mode == "tn" else pl.BlockSpec((tm, tk), lambda i, j, k: (i, k))
    b_spec = pl.BlockSpec((tn, tk), lambda i, j, k: (j, k)) if mode == "nt" else pl.BlockSpec((tk, tn), lambda i, j, k: (k, j))
    return pl.pallas_call(
        kern, name=name, grid=(M // tm, N // tn, nk),
        in_specs=[a_spec, b_spec], out_specs=pl.BlockSpec((tm, tn), lambda i, j, k: (i, j)),
        out_shape=jax.ShapeDtypeStruct((M, N), out_dtype),
        scratch_shapes=[pltpu.VMEM((tm, tn), F32)] if nk > 1 else [],
        compiler_params=_ARB(3),
    )(a, b)


class Row:
    def __init__(self, a, width=None, cb=0, halo=False, dil=1):
        self.a, self.width, self.cb, self.halo, self.dil = a, width, cb, halo, dil


class Full:
    def __init__(self, a):
        self.a = a


HALO = 8


def _rows(name, body, tm, ins, outs):
    outs = [o if len(o) == 4 else o + (1,) for o in outs]
    L = next(s.a.shape[0] * s.dil for s in ins if isinstance(s, Row))
    tm = _tile(L, tm, 16)
    dilated = any(s.dil > 1 for s in ins if isinstance(s, Row)) or any(o[3] > 1 for o in outs)
    nt = L // tm
    nb8 = L // HALO
    step = tm // HALO
    in_specs, arrays, layout = [], [], []
    for s in ins:
        if isinstance(s, Full):
            nd = s.a.ndim
            in_specs.append(pl.BlockSpec(s.a.shape, lambda i, _nd=nd: (0,) * _nd))
            arrays.append(s.a)
            layout.append(1)
        else:
            w = s.width or s.a.shape[1]
            in_specs.append(pl.BlockSpec((tm // s.dil, w), lambda i, _cb=s.cb: (i, _cb)))
            arrays.append(s.a)
            if s.dil > 1:
                layout.append(-s.dil)
            elif s.halo:
                in_specs.append(pl.BlockSpec((HALO, w), lambda i, _cb=s.cb: (jnp.maximum(i * step - 1, 0), _cb)))
                in_specs.append(pl.BlockSpec((HALO, w), lambda i, _cb=s.cb: (jnp.minimum((i + 1) * step, nb8 - 1), _cb)))
                arrays += [s.a, s.a]
                layout.append(3)
            else:
                layout.append(1)
    out_specs, out_shapes = [], []
    for kind, shp, dt, dil in outs:
        if kind == "row":
            out_specs.append(pl.BlockSpec((tm // dil, dil * shp), lambda i: (i, 0)))
            out_shapes.append(jax.ShapeDtypeStruct((L // dil, dil * shp), dt))
        else:
            out_specs.append(pl.BlockSpec(shp, lambda i, _n=len(shp): (0,) * _n))
            out_shapes.append(jax.ShapeDtypeStruct(shp, dt))
    n_in, n_out = len(arrays), len(outs)

    def kern(*refs):
        i = pl.program_id(0)
        lo, hi = refs[n_in + n_out:] if dilated else (None, None)

        def undilate(ref, d):
            for j in range(d):
                rows = pl.ds(j, tm // d, stride=d)
                lo[rows, :] = ref[:, j * GROUP_W:j * GROUP_W + LANES].astype(F32)
                hi[rows, :] = ref[:, j * GROUP_W + LANES:(j + 1) * GROUP_W].astype(F32)
            return jnp.concatenate([lo[...], hi[...]], axis=1)

        def dilate(val, ref, d, dt):
            lo[...] = val[:, :LANES].astype(F32)
            hi[...] = val[:, LANES:].astype(F32)
            for j in range(d):
                rows = pl.ds(j, tm // d, stride=d)
                ref[:, j * GROUP_W:j * GROUP_W + LANES] = lo[rows, :].astype(dt)
                ref[:, j * GROUP_W + LANES:(j + 1) * GROUP_W] = hi[rows, :].astype(dt)

        vals, p = [], 0
        for n in layout:
            if n == 1:
                vals.append(refs[p][...])
            elif n < 0:
                vals.append(undilate(refs[p], -n))
                n = 1
            else:
                vals.append((refs[p + 1][...], refs[p][...], refs[p + 2][...]))
            p += n
        res = body(i, nt, *vals)
        if not isinstance(res, (tuple, list)):
            res = (res,)
        for (kind, shp, dt, dil), o_ref, r in zip(outs, refs[n_in:], res):
            if kind == "row" and dil > 1:
                dilate(r, o_ref, dil, dt)
            elif kind == "row":
                o_ref[...] = r.astype(dt)
            else:
                @pl.when(i == 0)
                def _(o_ref=o_ref):
                    o_ref[...] = jnp.zeros_like(o_ref)
                o_ref[...] += r.astype(dt)

    res = pl.pallas_call(
        kern, name=name, grid=(nt,), in_specs=in_specs, out_specs=out_specs, out_shape=out_shapes,
        scratch_shapes=[pltpu.VMEM((tm, LANES), F32)] * 2 if dilated else [], compiler_params=_ARB(1),
    )(*arrays)
    return res


def _shift(h, o, i, nt):
    prev, cur, nxt = h
    if o == 0:
        return cur
    tm = cur.shape[0]
    cat = jnp.concatenate([prev, cur, nxt], axis=0)
    sh = pltpu.roll(cat, (-o) % (tm + 2 * HALO), axis=0)[HALO:HALO + tm]
    row = lax.broadcasted_iota(jnp.int32, cur.shape, 0)
    if o < 0:
        ok = jnp.logical_or(i > 0, row >= -o)
    else:
        ok = jnp.logical_or(i < nt - 1, row < tm - o)
    return jnp.where(ok, sh, 0.0)


def _colsum(v):
    return jnp.sum(v, axis=0, keepdims=True)


def _sigmoid(x):
    return 1.0 / (1.0 + jnp.exp(-x))


def _softplus(x):
    return jnp.maximum(x, 0.0) + jnp.log1p(jnp.exp(-jnp.abs(x)))


def _silu(x):
    return x * _sigmoid(x)


def _dsilu(x):
    s = _sigmoid(x)
    return s * (1.0 + x * (1.0 - s))


_GELU_C = math.sqrt(2.0 / math.pi)


def _gelu(x):
    return 0.5 * x * (1.0 + jnp.tanh(_GELU_C * (x + 0.044715 * x * x * x)))


def _dgelu(x):
    t = jnp.tanh(_GELU_C * (x + 0.044715 * x * x * x))
    return 0.5 * (1.0 + t) + 0.5 * x * (1.0 - t * t) * _GELU_C * (1.0 + 3.0 * 0.044715 * x * x)


def _head_sum(v, bd):
    return jnp.dot(v, bd, precision=lax.Precision.HIGHEST, preferred_element_type=F32)


def _block_ones(n, blk):
    r = np.arange(n)
    return jnp.asarray((r[:, None] // blk == r[None, :] // blk).astype(np.float32))


def _rms_fwd(x, g, name):
    def body(i, nt, x, g):
        r = lax.rsqrt(jnp.mean(x * x, axis=-1, keepdims=True) + EPS)
        return x * r * g
    return _rows(name, body, 256, [Row(x), Full(g)], [("row", x.shape[1], BF16)])[0]


def _rms_resid_fwd(x, y, g, name):
    def body(i, nt, x, y, g):
        r = lax.rsqrt(jnp.mean(y * y, axis=-1, keepdims=True) + EPS)
        return x + y * r * g
    return _rows(name, body, 256, [Row(x), Row(y), Full(g)], [("row", x.shape[1], F32)])[0]


def _rms_bwd(dy, x, g, name, resid=None, out_dtype=F32):
    D = x.shape[1]

    def body(i, nt, dy, x, g, *rest):
        dy = dy.astype(F32)
        r = lax.rsqrt(jnp.mean(x * x, axis=-1, keepdims=True) + EPS)
        xh = x * r
        dxh = dy * g
        dx = r * (dxh - xh * jnp.mean(dxh * xh, axis=-1, keepdims=True))
        if rest:
            dx = dx + rest[0]
        return dx, _colsum(dy * xh)

    ins = [Row(dy), Row(x), Full(g)] + ([Row(resid)] if resid is not None else [])
    return _rows(name, body, 256, ins, [("row", D, out_dtype), ("acc", (1, D), F32)])


def _ffn_in_swiglu(h, w):
    (M, K), N = h.shape, w.shape[1]
    F = N // 2
    tm = _tile(M, 256, 16)

    def kern(a_ref, b_ref, gu_ref, act_ref):
        gu = _dot(a_ref[...], b_ref[...])
        gu_ref[...] = gu
        act_ref[...] = (_silu(gu[:, :F]) * gu[:, F:]).astype(BF16)

    return pl.pallas_call(
        kern, name="ffn_in_swiglu", grid=(M // tm,),
        in_specs=[pl.BlockSpec((tm, K), lambda i: (i, 0)), pl.BlockSpec((K, N), lambda i: (0, 0))],
        out_specs=[pl.BlockSpec((tm, N), lambda i: (i, 0)), pl.BlockSpec((tm, F), lambda i: (i, 0))],
        out_shape=[jax.ShapeDtypeStruct((M, N), F32), jax.ShapeDtypeStruct((M, F), BF16)], compiler_params=_ARB(1),
    )(h, w)


def _ffn_out_dx_swiglu(df, w, gu):
    (M, K), N = df.shape, gu.shape[1]
    F = N // 2
    tm = _tile(M, 256, 16)

    def kern(a_ref, b_ref, gu_ref, o_ref):
        da = _dot(a_ref[...], b_ref[...], _NT)
        gu = gu_ref[...]
        gate, up = gu[:, :F], gu[:, F:]
        o_ref[:, :F] = (da * up * _dsilu(gate)).astype(BF16)
        o_ref[:, F:] = (da * _silu(gate)).astype(BF16)

    return pl.pallas_call(
        kern, name="ffn_out_dx_swiglu", grid=(M // tm,),
        in_specs=[pl.BlockSpec((tm, K), lambda i: (i, 0)), pl.BlockSpec((F, K), lambda i: (0, 0)),
                  pl.BlockSpec((tm, N), lambda i: (i, 0))],
        out_specs=pl.BlockSpec((tm, N), lambda i: (i, 0)), out_shape=jax.ShapeDtypeStruct((M, N), BF16),
        compiler_params=_ARB(1),
    )(df, w, gu)


def _loss_fwd_bwd(y, target):
    D = y.shape[1]

    def body(i, nt, y, t):
        err = y - t
        part = 0.5 * jnp.sum(jnp.mean(err * err, axis=-1, keepdims=True), axis=0, keepdims=True)
        return err * (1.0 / D), jnp.broadcast_to(part, (1, LANES))
    dy, loss = _rows("loss", body, 256, [Row(y), Row(target)], [("row", D, F32), ("acc", (1, LANES), F32)])
    return loss[0, 0], dy


def _adamw(w, g, m, v, name):
    C = w.shape[1]
    bc1 = 1.0 - ADAM_B1 ** ADAM_STEP
    bc2 = 1.0 - ADAM_B2 ** ADAM_STEP

    def body(i, nt, w, g, m, v):
        m = ADAM_B1 * m + (1.0 - ADAM_B1) * g
        v = ADAM_B2 * v + (1.0 - ADAM_B2) * (g * g)
        delta = -ADAM_LR * ((m / bc1) / (jnp.sqrt(v / bc2) + ADAM_EPS) + ADAM_WD * w)
        return delta, m, v
    return _rows(name, body, 256, [Row(w), Row(g), Row(m), Row(v)], [("row", C, F32)] * 3)


def _expm1(x):
    return jnp.tanh(0.5 * x) * (jnp.exp(x) + 1.0)


def _lru_gates(xh, i, nt, cw, cb, wa, wx, ba, bx, lam):
    xc = cb
    for j in range(4):
        xc = xc + cw[j:j + 1] * _shift(xh, j - 2, i, nt)
    xcb = xc.astype(BF16)
    gates = []
    for e in range(2):
        r = _sigmoid(jnp.dot(xcb, wa[e], preferred_element_type=F32) + ba[e:e + 1])
        ig = _sigmoid(jnp.dot(xcb, wx[e], preferred_element_type=F32) + bx[e:e + 1])
        sp = _softplus(-lam[e:e + 1])
        la = -LRU_C * r * sp
        gates.append((r, ig, sp, jnp.exp(la), jnp.sqrt(-_expm1(2.0 * la))))
    return xc, xcb, gates


def _scan2(af, uf, ab, ub, adjoint, name):
    L, W = af.shape
    tm = _tile(L, 512, 8)
    nt, nb = L // tm, tm // 8

    def blk(A, U, h, reverse, row):
        for d in (1, 2, 4):
            if reverse:
                ok, sh = row < 8 - d, 8 - d
            else:
                ok, sh = row >= d, d
            As = jnp.where(ok, pltpu.roll(A, sh, axis=0), 1.0)
            Us = jnp.where(ok, pltpu.roll(U, sh, axis=0), 0.0)
            U = A * Us + U
            A = A * As
        return A * h + U

    def kern(af_ref, uf_ref, ab_ref, ub_ref, of_ref, ob_ref, c_ref):
        @pl.when(pl.program_id(0) == 0)
        def _():
            c_ref[...] = jnp.zeros_like(c_ref)

        row = lax.broadcasted_iota(jnp.int32, (8, W), 0)
        full = lambda v: jnp.broadcast_to(v, (8, W))

        def body(j, carry):
            hF, aF, hB, aB = carry
            r0 = pl.multiple_of(j * 8, 8)
            r1 = pl.multiple_of((nb - 1 - j) * 8, 8)
            A, U = af_ref[pl.ds(r0, 8), :], uf_ref[pl.ds(r0, 8), :]
            if adjoint:
                C = jnp.where(row == 0, aF, pltpu.roll(A, 1, axis=0))
                aF = full(A[7:8])
            else:
                C = A
            H = blk(C, U, hF, False, row)
            of_ref[pl.ds(r0, 8), :] = H
            hF = full(H[7:8])
            A, U = ab_ref[pl.ds(r1, 8), :], ub_ref[pl.ds(r1, 8), :]
            if adjoint:
                C = jnp.where(row == 7, aB, pltpu.roll(A, 7, axis=0))
                aB = full(A[0:1])
            else:
                C = A
            H = blk(C, U, hB, True, row)
            ob_ref[pl.ds(r1, 8), :] = H
            hB = full(H[0:1])
            return hF, aF, hB, aB

        carry = lax.fori_loop(0, nb, body, (c_ref[0], c_ref[1], c_ref[2], c_ref[3]))
        for n in range(4):
            c_ref[n] = carry[n]

    fwd = pl.BlockSpec((tm, W), lambda i: (i, 0))
    bwd = pl.BlockSpec((tm, W), lambda i: (nt - 1 - i, 0))
    return pl.pallas_call(
        kern, name=name, grid=(nt,), in_specs=[fwd, fwd, bwd, bwd], out_specs=[fwd, bwd],
        out_shape=[jax.ShapeDtypeStruct((L, W), F32)] * 2,
        scratch_shapes=[pltpu.VMEM((4, 8, W), F32)], compiler_params=_ARB(1),
    )(af, uf, ab, ub)


def _block_diag(w):
    rows = jnp.tile(w.reshape(2, GROUP_W, HEAD_DIM), (1, 1, 4))
    return jnp.where(_block_ones(GROUP_W, HEAD_DIM) > 0.5, rows, 0.0).astype(BF16)


def _diag_blocks(w):
    return jnp.stack([w[:, h * 64:(h + 1) * 64, h * 64:(h + 1) * 64] for h in range(4)], axis=1)


def _lru_params(W, l):
    return [Full(W["lru_conv_w"][l]), Full(W["lru_conv_b"][l][None]), Full(_block_diag(W["lru_w_a"][l])),
            Full(_block_diag(W["lru_w_x"][l])), Full(W["lru_b_a"][l]), Full(W["lru_b_x"][l]), Full(W["lru_lambda"][l])]


def _lru_fwd(p, W, l):
    def pre(i, nt, xh, *prm):
        xc, _, g = _lru_gates(xh, i, nt, *prm)
        return g[0][3], g[0][4] * (g[0][1] * xc), g[1][3], g[1][4] * (g[1][1] * xc)

    a0, u0, a1, u1 = _rows("lru_pre", pre, 256, [Row(p, GROUP_W, 7, halo=True)] + _lru_params(W, l),
                           [("row", GROUP_W, F32)] * 4)
    hf, hb = _scan2(a0, u0, a1, u1, False, "lru_scan")
    yc = _rows("lru_post", lambda i, nt, hf, hb, gc: (hf + hb) * _gelu(gc), 512,
               [Row(hf), Row(hb), Row(p, GROUP_W, 8)], [("row", GROUP_W, BF16)])[0]
    return yc, (a0, a1, hf, hb)


def _lru_bwd(dy, dy_cb, p, W, l, saved):
    a0, a1, hf, hb = saved

    def post(i, nt, dy, hf, hb, gc):
        return dy * _gelu(gc), dy * (hf + hb) * _dgelu(gc)

    dh, dgc = _rows("lru_post_bwd", post, 512, [Row(dy, GROUP_W, dy_cb), Row(hf), Row(hb), Row(p, GROUP_W, 8)],
                    [("row", GROUP_W, F32), ("row", GROUP_W, BF16)])
    gb, gf = _scan2(a1, dh, a0, dh, True, "lru_scan_adj")

    def gates_bwd(i, nt, xh, gf, gb, hfh, hbh, cw, cb, wa, wx, ba, bx, lam):
        xc, xcb, g = _lru_gates(xh, i, nt, cw, cb, wa, wx, ba, bx, lam)
        dxc = jnp.zeros_like(xc)
        dwa, dwx, dba, dbx, dlam = [], [], [], [], []
        for e, du, hprev in ((0, gf, _shift(hfh, -1, i, nt)), (1, gb, _shift(hbh, 1, i, nt))):
            r, ig, sp, a, s = g[e]
            dxc = dxc + du * s * ig
            dla = du * hprev * a - (du * ig * xc) * a * a / s
            dza = (dla * (-LRU_C) * sp) * r * (1.0 - r)
            dzx = (du * s * xc) * ig * (1.0 - ig)
            dlam.append(_colsum(dla * r) * (LRU_C * _sigmoid(-lam[e:e + 1])))
            dba.append(_colsum(dza))
            dbx.append(_colsum(dzx))
            dzab, dzxb = dza.astype(BF16), dzx.astype(BF16)
            tn = (((0,), (0,)), ((), ()))
            nt_ = (((1,), (1,)), ((), ()))
            dwa.append(lax.dot_general(xcb, dzab, tn, preferred_element_type=F32))
            dwx.append(lax.dot_general(xcb, dzxb, tn, preferred_element_type=F32))
            dxc = dxc + lax.dot_general(dzab, wa[e], nt_, preferred_element_type=F32)
            dxc = dxc + lax.dot_general(dzxb, wx[e], nt_, preferred_element_type=F32)
        cat = lambda v: jnp.concatenate(v, axis=0)
        return dxc, jnp.stack(dwa), jnp.stack(dwx), cat(dba), cat(dbx), cat(dlam)

    dxc, dwa, dwx, dba, dbx, dlam = _rows(
        "lru_gates_bwd", gates_bwd, 256,
        [Row(p, GROUP_W, 7, halo=True), Row(gf), Row(gb), Row(hf, halo=True), Row(hb, halo=True)] + _lru_params(W, l),
        [("row", GROUP_W, F32), ("acc", (2, GROUP_W, GROUP_W), F32), ("acc", (2, GROUP_W, GROUP_W), F32),
         ("acc", (2, GROUP_W), F32), ("acc", (2, GROUP_W), F32), ("acc", (2, GROUP_W), F32)])

    def conv_bwd(i, nt, dh_, xh, cw):
        dxb = jnp.zeros_like(dh_[1])
        dcw = []
        for j in range(4):
            dxb = dxb + cw[j:j + 1] * _shift(dh_, 2 - j, i, nt)
            dcw.append(_colsum(dh_[1] * _shift(xh, j - 2, i, nt)))
        return dxb, jnp.concatenate(dcw, axis=0), _colsum(dh_[1])

    dxb, dcw, dcb = _rows("lru_conv_bwd", conv_bwd, 512,
                          [Row(dxc, halo=True), Row(p, GROUP_W, 7, halo=True), Full(W["lru_conv_w"][l])],
                          [("row", GROUP_W, BF16), ("acc", (4, GROUP_W), F32), ("acc", (1, GROUP_W), F32)])
    grads = dict(lru_conv_w=dcw, lru_conv_b=dcb[0], lru_w_a=_diag_blocks(dwa), lru_w_x=_diag_blocks(dwx),
                 lru_b_a=dba, lru_b_x=dbx, lru_lambda=dlam)
    return dxb, dgc, grads


_NT = (((1,), (1,)), ((), ()))
_TN = (((0,), (0,)), ((), ()))


def _dot(a, b, dims=None):
    if dims is None:
        return jnp.dot(a, b, preferred_element_type=F32)
    return lax.dot_general(a, b, dims, preferred_element_type=F32)


def _dot_exact(a, b):
    return jnp.dot(a, b, precision=lax.Precision.HIGHEST, preferred_element_type=F32)


def _gla_gate_w(w_gate, b_gate):
    zero = jnp.zeros((GLA_RANK, GROUP_W), w_gate.dtype)
    wg = jnp.concatenate([jnp.concatenate([w_gate[0], zero], axis=1), jnp.concatenate([zero, w_gate[1]], axis=1),
                          jnp.zeros((LANES - 2 * GLA_RANK, 2 * GROUP_W), w_gate.dtype)], axis=0)
    return wg.astype(BF16), b_gate.reshape(1, 2 * GROUP_W)


def _gla_gates_fwd(p, wg, bg):
    def body(i, nt, z, wg, bg):
        logit = _dot(z.astype(BF16), wg) + bg
        la = -_softplus(-logit) * (1.0 / GLA_TAU)
        return la[:, :GROUP_W], la[:, GROUP_W:]
    return _rows("gla_gates", body, 512, [Row(p, LANES, Z_BLOCK), Full(wg), Full(bg)], [("row", GROUP_W, F32)] * 2)


def _gla_gates_bwd(p, dla0, dla1, wg, bg):
    def body(i, nt, z, d0, d1, wg, bg):
        zb = z.astype(BF16)
        logit = _dot(zb, wg) + bg
        dlogit = jnp.concatenate([d0, d1], axis=1) * (1.0 / GLA_TAU) * _sigmoid(-logit)
        dlb = dlogit.astype(BF16)
        return _dot(dlb, wg, _NT), _dot(zb, dlb, _TN), _colsum(dlogit)
    return _rows("gla_gates_bwd", body, 512, [Row(p, LANES, Z_BLOCK), Row(dla0), Row(dla1), Full(wg), Full(bg)],
                 [("row", LANES, BF16), ("acc", (LANES, 2 * GROUP_W), F32), ("acc", (1, 2 * GROUP_W), F32)])


def _gla_order(reverse):
    t = np.arange(GLA_CHUNK)
    m = (t[None, :] >= t[:, None]) if reverse else (t[None, :] <= t[:, None])
    return m.astype(np.float32), (32, 0) if reverse else (31, 63)


def _stack_heads(x, bd):
    return jnp.where(bd, jnp.concatenate([x] * 4, axis=0), 0.0)


def _diag_heads(r, bd):
    r = jnp.where(bd, r, 0.0)
    return r[0:64] + r[64:128] + r[128:192] + r[192:256]


def _gla_factors(q_ref, k_ref, rows, b, mid, last):
    bm, bl = b[mid:mid + 1], b[last:last + 1]
    qs = q_ref[rows, :] * ATT_SCALE
    k = k_ref[rows, :]
    P, N, E, Fd = jnp.exp(b - bm), jnp.exp(bm - b), jnp.exp(b), jnp.exp(bl - b)
    return (P, N, E, Fd, jnp.exp(bl)), (qs * P, k * N, qs * E, k * Fd)


def _gla_specs(L, walk_up):
    tm = _tile(L, 512, GLA_CHUNK)
    nt, nc = L // tm, tm // GLA_CHUNK
    specs = []
    for up in walk_up:
        t = (lambda i: i) if up else (lambda i: nt - 1 - i)
        specs.append(dict(
            col=lambda cb, _t=t: pl.BlockSpec((tm, GROUP_W), lambda i: (_t(i), cb)),
            row=pl.BlockSpec((tm, GROUP_W), lambda i, _t=t: (_t(i), 0)),
            state=pl.BlockSpec((nc, GROUP_W, GROUP_W), lambda i, _t=t: (_t(i), 0, 0))))
    return nt, nc, specs


def _gla_chunk_fwd(p, la0, la1, ride=None):
    L = la0.shape[0]
    nt, nc, specs = _gla_specs(L, (True, False))
    orders = [_gla_order(False), _gla_order(True)]

    def kern(q0, k0, v0, l0, q1, k1, v1, l1, m0_ref, m1_ref, bd_ref, o0, s0, o1, s1, st_ref):
        @pl.when(pl.program_id(0) == 0)
        def _():
            st_ref[...] = jnp.zeros_like(st_ref)

        bd = bd_ref[...] > 0.5
        dirs = []
        for e, (q_ref, k_ref, v_ref, la_ref, m_ref, o_ref, s_ref) in enumerate(
                ((q0, k0, v0, l0, m0_ref, o0, s0), (q1, k1, v1, l1, m1_ref, o1, s1))):
            mv = m_ref[...]
            dirs.append((q_ref, k_ref, v_ref, la_ref, mv, jnp.concatenate([mv] * 4, axis=0) > 0.5, o_ref, s_ref))

        def body(cc, carry):
            E = range(2)
            cs = [nc - 1 - cc if e else cc for e in E]
            rows = [pl.ds(pl.multiple_of(c * GLA_CHUNK, GLA_CHUNK), GLA_CHUNK) for c in cs]
            b = [_dot_exact(dirs[e][4], dirs[e][3][rows[e], :]) for e in E]
            t = [_gla_factors(dirs[e][0], dirs[e][1], rows[e], b[e], *orders[e][1]) for e in E]
            vb = [dirs[e][2][rows[e], :].astype(BF16) for e in E]
            st = [st_ref[e] for e in E]
            a = [_dot(_stack_heads(t[e][1][0], bd).astype(BF16), t[e][1][1].astype(BF16), _NT) for e in E]
            inter = [_dot(t[e][1][2].astype(BF16), st[e].astype(BF16), _NT) for e in E]
            kv = [_dot(vb[e], t[e][1][3].astype(BF16), _TN) for e in E]
            a = [jnp.where(dirs[e][5], a[e], 0.0).astype(BF16) for e in E]
            r = [_dot(a[e], vb[e]) for e in E]
            for e in E:
                dirs[e][7][cs[e]] = st[e]
                dirs[e][6][rows[e], :] = _diag_heads(r[e], bd) + inter[e]
                st_ref[e] = st[e] * t[e][0][4] + jnp.where(bd, kv[e], 0.0)
            return carry

        lax.fori_loop(0, nc, body, 0)

    const = lambda shp: pl.BlockSpec(shp, lambda i: (0, 0))
    in_specs, out_specs = [], []
    for sp in specs:
        in_specs += [sp["col"](0), sp["col"](1), sp["col"](2), sp["row"]]
        out_specs += [sp["row"], sp["state"]]
    return _call(
        kern, (p, p, p, la0, p, p, p, la1, jnp.asarray(orders[0][0]), jnp.asarray(orders[1][0]),
               _block_ones(GROUP_W, HEAD_DIM)),
        ride, lambda: (pl.program_id(0) == 0, pl.program_id(0) == nt - 1), name="gla_fwd", grid=(nt,),
        in_specs=in_specs + [const((GLA_CHUNK, GLA_CHUNK))] * 2 + [const((GROUP_W, GROUP_W))], out_specs=out_specs,
        out_shape=[jax.ShapeDtypeStruct((L, GROUP_W), F32),
                   jax.ShapeDtypeStruct((L // GLA_CHUNK, GROUP_W, GROUP_W), F32)] * 2,
        scratch_shapes=[pltpu.VMEM((2, GROUP_W, GROUP_W), F32)])


def _gla_chunk_bwd(p, la0, la1, do, sprev0, sprev1, ride=None):
    L = la0.shape[0]
    nt, nc, specs = _gla_specs(L, (False, True))
    orders = [_gla_order(False), _gla_order(True)]

    def kern(q0, k0, v0, l0, do0, s0, q1, k1, v1, l1, do1, s1, m0_ref, m1_ref, t0_ref, t1_ref, bd_ref, *rest):
        outs, dst_ref = (rest[0:4], rest[4:8]), rest[8]

        @pl.when(pl.program_id(0) == 0)
        def _():
            dst_ref[...] = jnp.zeros_like(dst_ref)

        bd = bd_ref[...] > 0.5
        row = lax.broadcasted_iota(jnp.int32, (GLA_CHUNK, GROUP_W), 0)
        dirs = []
        for ins, m_ref, t_ref in (((q0, k0, v0, l0, do0, s0), m0_ref, t0_ref), ((q1, k1, v1, l1, do1, s1), m1_ref, t1_ref)):
            mv = m_ref[...]
            dirs.append(ins + (mv, t_ref[...], jnp.concatenate([mv] * 4, axis=0) > 0.5))

        def body(cc, carry):
            E2 = range(2)
            cs = [cc if e else nc - 1 - cc for e in E2]
            rows = [pl.ds(pl.multiple_of(c * GLA_CHUNK, GLA_CHUNK), GLA_CHUNK) for c in cs]
            b = [_dot_exact(dirs[e][6], dirs[e][3][rows[e], :]) for e in E2]
            t = [_gla_factors(dirs[e][0], dirs[e][1], rows[e], b[e], *orders[e][1]) for e in E2]
            vb = [dirs[e][2][rows[e], :].astype(BF16) for e in E2]
            dov = [dirs[e][4][rows[e], :] for e in E2]
            dob = [x.astype(BF16) for x in dov]
            st = [dirs[e][5][cs[e]] for e in E2]
            dst = [dst_ref[e] for e in E2]
            stb, dstb = [x.astype(BF16) for x in st], [x.astype(BF16) for x in dst]
            qst = [_stack_heads(t[e][1][0], bd).astype(BF16) for e in E2]
            dost = [_stack_heads(dov[e], bd).astype(BF16) for e in E2]
            kNb, qEb, kFb = ([t[e][1][n].astype(BF16) for e in E2] for n in (1, 2, 3))
            a = [_dot(qst[e], kNb[e], _NT) for e in E2]
            da = [_dot(dost[e], vb[e], _NT) for e in E2]
            dqE = [_dot(dob[e], stb[e]) for e in E2]
            dkF = [_dot(vb[e], dstb[e]) for e in E2]
            dv_inter = [_dot(kFb[e], dstb[e], _NT) for e in E2]
            dst_in = [_dot(dob[e], qEb[e], _TN) for e in E2]
            a = [jnp.where(dirs[e][8], a[e], 0.0).astype(BF16) for e in E2]
            da = [jnp.where(dirs[e][8], da[e], 0.0).astype(BF16) for e in E2]
            dv_intra = [_dot(a[e], dost[e], _TN) for e in E2]
            dqP = [_dot(da[e], kNb[e]) for e in E2]
            dkN = [_dot(da[e], qst[e], _TN) for e in E2]
            db = []
            for e in E2:
                (P, N, Ef, Fd, d), (qP, kN, qE, kF) = t[e]
                mid, last = orders[e][1]
                dq_ref, dk_ref, dv_ref, _ = outs[e]
                dqp = _diag_heads(dqP[e], bd)
                dd = _colsum(dst[e] * st[e])
                dst_ref[e] = jnp.where(bd, dst_in[e], 0.0) + dst[e] * d
                tP, tN, tE, tF = dqp * qP, dkN[e] * kN, dqE[e] * qE, dkF[e] * kF
                db.append(tP - tN + tE - tF + jnp.where(row == mid, _colsum(tN - tP), 0.0)
                          + jnp.where(row == last, _colsum(tF) + dd * d, 0.0))
                dq_ref[rows[e], :] = (dqp * P + dqE[e] * Ef) * ATT_SCALE
                dk_ref[rows[e], :] = dkN[e] * N + dkF[e] * Fd
                dv_ref[rows[e], :] = dv_intra[e] + dv_inter[e]
            dla = [_dot_exact(dirs[e][7], db[e]) for e in E2]
            for e in E2:
                outs[e][3][rows[e], :] = dla[e]
            return carry

        lax.fori_loop(0, nc, body, 0)

    const = lambda shp: pl.BlockSpec(shp, lambda i: (0, 0))
    in_specs, out_specs = [], []
    for sp in specs:
        in_specs += [sp["col"](0), sp["col"](1), sp["col"](2), sp["row"], sp["row"], sp["state"]]
        out_specs += [sp["row"]] * 4
    m0, m1 = orders[0][0], orders[1][0]
    return _call(
        kern, (p, p, p, la0, do, sprev0, p, p, p, la1, do, sprev1, jnp.asarray(m0), jnp.asarray(m1),
               jnp.asarray(m0.T.copy()), jnp.asarray(m1.T.copy()), _block_ones(GROUP_W, HEAD_DIM)),
        ride, lambda: (pl.program_id(0) == 0, pl.program_id(0) == nt - 1), name="gla_bwd", grid=(nt,),
        in_specs=in_specs + [const((GLA_CHUNK, GLA_CHUNK))] * 4 + [const((GROUP_W, GROUP_W))], out_specs=out_specs,
        out_shape=[jax.ShapeDtypeStruct((L, GROUP_W), F32)] * 8,
        scratch_shapes=[pltpu.VMEM((2, GROUP_W, GROUP_W), F32)])


def _gla_fwd(p, W, l, ride=None):
    wg, bg = _gla_gate_w(W["gla_w_gate"][l], W["gla_b_gate"][l])
    la0, la1 = _gla_gates_fwd(p, wg, bg)
    (of, s0, ob, s1), got = _gla_chunk_fwd(p, la0, la1, ride)

    def post(i, nt, of, ob, g, ng, bd):
        o = of + ob
        r = lax.rsqrt(_head_sum(o * o, bd) * (1.0 / HEAD_DIM) + EPS)
        return o * r * ng * _silu(g)

    ya = _rows("gla_post", post, 512, [Row(of), Row(ob), Row(p, GROUP_W, 3), Full(W["gla_norm"][l][None]),
                                       Full(_block_ones(GROUP_W, HEAD_DIM))], [("row", GROUP_W, BF16)])[0]
    return ya, (la0, la1, of, ob, s0, s1), got


def _gla_bwd(dy, dy_cb, p, W, l, saved, ride=None):
    la0, la1, of, ob, s0, s1 = saved
    wg, bg = _gla_gate_w(W["gla_w_gate"][l], W["gla_b_gate"][l])

    def post(i, nt, dy, of, ob, g, ng, bd):
        o = of + ob
        r = lax.rsqrt(_head_sum(o * o, bd) * (1.0 / HEAD_DIM) + EPS)
        oh = o * r
        don = dy * _silu(g)
        doh = don * ng
        do = r * (doh - oh * _head_sum(doh * oh, bd) * (1.0 / HEAD_DIM))
        return do, dy * (oh * ng) * _dsilu(g), _colsum(don * oh)

    do, dg, dng = _rows("gla_post_bwd", post, 512,
                        [Row(dy, GROUP_W, dy_cb), Row(of), Row(ob), Row(p, GROUP_W, 3), Full(W["gla_norm"][l][None]),
                         Full(_block_ones(GROUP_W, HEAD_DIM))],
                        [("row", GROUP_W, F32), ("row", GROUP_W, BF16), ("acc", (1, GROUP_W), F32)])
    (dq0, dk0, dv0, dla0, dq1, dk1, dv1, dla1), got = _gla_chunk_bwd(p, la0, la1, do, s0, s1, ride)
    dq, dk, dv = _rows("gla_sum_bwd", lambda i, nt, a0, a1, b0, b1, c0, c1: (a0 + a1, b0 + b1, c0 + c1), 512,
                       [Row(t) for t in (dq0, dq1, dk0, dk1, dv0, dv1)], [("row", GROUP_W, BF16)] * 3)
    dz, dwg, dbg = _gla_gates_bwd(p, dla0, dla1, wg, bg)
    dw_gate = jnp.stack([dwg[e * GLA_RANK:(e + 1) * GLA_RANK, e * GROUP_W:(e + 1) * GROUP_W] for e in range(2)])
    grads = dict(gla_w_gate=dw_gate, gla_b_gate=dbg.reshape(2, GROUP_W), gla_norm=dng[0])
    return (dq, dk, dv, dg, dz), grads, got


def _rope_tables(L):
    pos = jnp.arange(L, dtype=F32)
    inv_freq = ROPE_THETA ** (-jnp.arange(0, HEAD_DIM, 2, dtype=F32) / HEAD_DIM)
    ang = pos[:, None] * inv_freq[None, :]
    cos, sin = jnp.cos(ang), jnp.sin(ang)
    return jnp.tile(jnp.concatenate([cos, cos], axis=1), (1, 4)), jnp.tile(jnp.concatenate([-sin, sin], axis=1), (1, 4))


def _swap_halves(t):
    lane = lax.broadcasted_iota(jnp.int32, t.shape, 1)
    first = (lane & (HEAD_DIM - 1)) < HEAD_DIM // 2
    return jnp.where(first, pltpu.roll(t, GROUP_W - HEAD_DIM // 2, axis=1), pltpu.roll(t, HEAD_DIM // 2, axis=1))


def _attn_prep(p, cosf, sinf):
    dils = [dil for _, dil in DIL_PAIRS]

    def body(i, nt, qb, kb, vb, qd, kd, vd, c, s):
        d = (qd * c + _swap_halves(qd) * s, kd * c + _swap_halves(kd) * s, vd)
        return (qb, kb, vb) + d * len(dils)
    ins = [Row(p, GROUP_W, cb) for cb in (4, 5, 6, 9, 10, 11)] + [Row(cosf), Row(sinf)]
    outs = [("row", GROUP_W, BF16)] * 3 + [("row", GROUP_W, BF16, dil) for dil in dils for _ in range(3)]
    res = _rows("attn_prep", body, 512, ins, outs)
    return tuple(res[:3]), {dil: tuple(res[3 + 3 * n:6 + 3 * n]) for n, dil in enumerate(dils)}


def _na_onehot():
    c = np.arange(GRID_W)
    dc = np.clip(c[None, :] - c[:, None], -(NA_COLS - 1), NA_COLS - 1) + NA_COLS - 1
    oh = np.zeros((LANES, GRID_W * GRID_W), np.float32)
    oh[dc.reshape(-1), np.arange(GRID_W * GRID_W)] = 1.0
    return jnp.asarray(oh)


def _na_colmask():
    c = np.arange(GRID_W)
    start = np.clip(c - NA_COLS // 2, 0, GRID_W - NA_COLS)
    ok = (c[None, :] >= start[:, None]) & (c[None, :] < start[:, None] + NA_COLS)
    return jnp.asarray(np.where(ok, 0.0, NEG).astype(np.float32))


N_DR = 2 * NA_ROWS - 1


NA_HALF = GRID_W // 2
NA_KCOLS = 48
NA_WIN = NA_ROWS * NA_KCOLS
NA_ROWS_PER_STEP = 4
NA_BWD_ROWS_PER_STEP = 2


def _na_bias(rpb):
    rp = jnp.pad(rpb.reshape(4 * N_DR, 2 * NA_COLS - 1), ((0, GRID_W - 4 * N_DR), (0, LANES - 2 * NA_COLS + 1)))

    def expand(r_ref, oh_ref, o_ref):
        o_ref[...] = _dot_exact(r_ref[...], oh_ref[...])

    r = pl.pallas_call(expand, name="na_bias_expand",
                       out_shape=jax.ShapeDtypeStruct((GRID_W, GRID_W * GRID_W), F32))(rp, _na_onehot())
    r = r[:4 * N_DR].reshape(4, N_DR, GRID_W, GRID_W)

    def build(r_ref, m_ref, o_ref):
        for h in range(4):
            for c in range(NA_ROWS):
                for half in range(2):
                    q0, k0 = NA_HALF * half, 16 * half
                    for i in range(NA_ROWS):
                        o_ref[h, c, half, :, i * NA_KCOLS:(i + 1) * NA_KCOLS] = (
                            r_ref[h, i - c + NA_ROWS - 1, q0:q0 + NA_HALF, k0:k0 + NA_KCOLS]
                            + m_ref[q0:q0 + NA_HALF, k0:k0 + NA_KCOLS])

    return pl.pallas_call(build, name="na_bias_build",
                          out_shape=jax.ShapeDtypeStruct((4, NA_ROWS, 2, NA_HALF, NA_WIN), F32))(r, _na_colmask())


def _na_bias_bwd(dbias):
    def fold(d_ref, o_ref):
        o_ref[...] = jnp.zeros_like(o_ref)
        for h in range(4):
            for a in range(N_DR):
                for half in range(2):
                    q0, k0 = NA_HALF * half, 16 * half
                    acc = jnp.zeros((NA_HALF, NA_KCOLS), F32)
                    for c in range(NA_ROWS):
                        i = a + c - (NA_ROWS - 1)
                        if 0 <= i < NA_ROWS:
                            acc = acc + d_ref[h, c, half, :, i * NA_KCOLS:(i + 1) * NA_KCOLS]
                    o_ref[h, a, q0:q0 + NA_HALF, k0:k0 + NA_KCOLS] = acc

    dr = pl.pallas_call(fold, name="na_bias_fold",
                        out_shape=jax.ShapeDtypeStruct((4, N_DR, GRID_W, GRID_W), F32))(dbias)
    dr = jnp.pad(dr.reshape(4 * N_DR, GRID_W * GRID_W), ((0, GRID_W - 4 * N_DR), (0, 0)))

    def contract(d_ref, oh_ref, o_ref):
        o_ref[...] = lax.dot_general(d_ref[...], oh_ref[...], _NT, precision=lax.Precision.HIGHEST,
                                     preferred_element_type=F32)

    g = pl.pallas_call(contract, name="na_bias_contract",
                       out_shape=jax.ShapeDtypeStruct((GRID_W, LANES), F32))(dr, _na_onehot())
    return g[:4 * N_DR, :2 * NA_COLS - 1].reshape(4, N_DR, 2 * NA_COLS - 1)


def _na_window(r, n_rows):
    rs = jnp.clip(r - NA_ROWS // 2, 0, n_rows - NA_ROWS)
    return rs, r - rs


def _na_key_rows(rs, half, t):
    return pl.ds(pl.multiple_of((rs + t) * GRID_W + 16 * half, 16), NA_KCOLS)


def _na_keys(ref, rs, half):
    return jnp.concatenate([ref[_na_key_rows(rs, half, t), :] for t in range(NA_ROWS)], axis=0)


def _na_stack(x, first):
    zero = jnp.zeros_like(x)
    return jnp.concatenate([jnp.where(first, x, zero), jnp.where(first, zero, x)], axis=0)


def _na_bias_spec():
    return pl.BlockSpec((2, NA_ROWS, 2, NA_HALF, NA_WIN), lambda j, i: (j, 0, 0, 0, 0))


def _grid_edges(n0, n1):
    j, i = pl.program_id(0), pl.program_id(1)
    return jnp.logical_and(j == 0, i == 0), jnp.logical_and(j == n0 - 1, i == n1 - 1)


def _na_fwd(q, k, v, bias, ride=None):
    L = q.shape[0]
    n_rows = L // GRID_W
    tm = _tile(L, 512, GRID_W)
    nt, nr = L // tm, tm // GRID_W

    def kern(q_ref, k_ref, v_ref, b_ref, o_ref):
        i = pl.program_id(1)
        first = lax.broadcasted_iota(jnp.int32, (NA_HALF, LANES), 1) < HEAD_DIM

        def body(it, carry):
            parts = []
            for u in range(NA_ROWS_PER_STEP):
                rr = it * NA_ROWS_PER_STEP + u
                rs, c = _na_window(i * nr + rr, n_rows)
                for half in range(2):
                    rows = pl.ds(pl.multiple_of(rr * GRID_W + NA_HALF * half, NA_HALF), NA_HALF)
                    bias = jnp.concatenate([b_ref[0, c, half], b_ref[1, c, half]], axis=0)
                    parts.append((rows, _na_stack(q_ref[rows, :], first), bias, _na_keys(k_ref, rs, half),
                                  _na_keys(v_ref, rs, half)))
            s = [_dot(qs, kw, _NT) * ATT_SCALE + bias for _, qs, bias, kw, _ in parts]
            e = [jnp.exp(x - jnp.max(x, axis=-1, keepdims=True)) for x in s]
            pn = [(x / jnp.sum(x, axis=-1, keepdims=True)).astype(BF16) for x in e]
            o = [_dot(p, part[4]) for p, part in zip(pn, parts)]
            for x, (rows, *_) in zip(o, parts):
                o_ref[rows, :] = jnp.where(first, x[:NA_HALF], x[NA_HALF:]).astype(BF16)
            return carry

        lax.fori_loop(0, nr // NA_ROWS_PER_STEP, body, 0)

    qspec = pl.BlockSpec((tm, LANES), lambda j, i: (i, j))
    kvspec = pl.BlockSpec((L, LANES), lambda j, i: (0, j))
    (y,), got = _call(
        kern, (q, k, v, bias), ride, lambda: _grid_edges(2, nt), name="na_fwd", grid=(2, nt),
        in_specs=[qspec, kvspec, kvspec, _na_bias_spec()],
        out_specs=[qspec], out_shape=[jax.ShapeDtypeStruct((L, GROUP_W), BF16)], scratch_shapes=[])
    return y, got


def _na_bwd(dy, dy_block, q, k, v, bias, ride=None):
    L = q.shape[0]
    n_rows = L // GRID_W
    tm = _tile(L, 512, GRID_W)
    nt, nr = L // tm, tm // GRID_W

    def kern(dy_ref, q_ref, k_ref, v_ref, b_ref, dq_ref, dk_ref, dv_ref, db_ref):
        i = pl.program_id(1)

        @pl.when(i == 0)
        def _():
            dk_ref[...] = jnp.zeros_like(dk_ref)
            dv_ref[...] = jnp.zeros_like(dv_ref)
            db_ref[...] = jnp.zeros_like(db_ref)

        first = lax.broadcasted_iota(jnp.int32, (NA_HALF, LANES), 1) < HEAD_DIM

        def body(it, carry):
            parts = []
            for u in range(NA_BWD_ROWS_PER_STEP):
                rr = it * NA_BWD_ROWS_PER_STEP + u
                rs, c = _na_window(i * nr + rr, n_rows)
                for half in range(2):
                    rows = pl.ds(pl.multiple_of(rr * GRID_W + NA_HALF * half, NA_HALF), NA_HALF)
                    bias = jnp.concatenate([b_ref[0, c, half], b_ref[1, c, half]], axis=0)
                    parts.append((rows, half, _na_stack(q_ref[rows, :], first),
                                  _na_stack(dy_ref[rows, :].astype(BF16), first), bias, _na_keys(k_ref, rs, half),
                                  _na_keys(v_ref, rs, half), rs, c))
            s = [_dot(part[2], part[5], _NT) * ATT_SCALE + part[4] for part in parts]
            dp = [_dot(part[3], part[6], _NT) for part in parts]
            e = [jnp.exp(x - jnp.max(x, axis=-1, keepdims=True)) for x in s]
            pn = [x / jnp.sum(x, axis=-1, keepdims=True) for x in e]
            ds = [p * (d - jnp.sum(p * d, axis=-1, keepdims=True)) for p, d in zip(pn, dp)]
            dsb = [x.astype(BF16) for x in ds]
            pnb = [x.astype(BF16) for x in pn]
            dq = [_dot(x, part[5]) for x, part in zip(dsb, parts)]
            dk = [_dot(x, part[2], _TN) for x, part in zip(dsb, parts)]
            dv = [_dot(x, part[3], _TN) for x, part in zip(pnb, parts)]
            for n, (rows, half, _, _, _, _, _, rs, c) in enumerate(parts):
                db_ref[0, c, half] += ds[n][:NA_HALF]
                db_ref[1, c, half] += ds[n][NA_HALF:]
                dq_ref[rows, :] = (jnp.where(first, dq[n][:NA_HALF], dq[n][NA_HALF:]) * ATT_SCALE).astype(BF16)
                for t in range(NA_ROWS):
                    kr = _na_key_rows(rs, half, t)
                    dk_ref[kr, :] += dk[n][t * NA_KCOLS:(t + 1) * NA_KCOLS] * ATT_SCALE
                    dv_ref[kr, :] += dv[n][t * NA_KCOLS:(t + 1) * NA_KCOLS]
            return carry

        lax.fori_loop(0, nr // NA_BWD_ROWS_PER_STEP, body, 0)

    qspec = pl.BlockSpec((tm, LANES), lambda j, i: (i, j))
    kvspec = pl.BlockSpec((L, LANES), lambda j, i: (0, j))
    return _call(
        kern, (dy, q, k, v, bias), ride, lambda: _grid_edges(2, nt), name="na_bwd", grid=(2, nt),
        in_specs=[pl.BlockSpec((tm, LANES), lambda j, i: (i, dy_block + j)), qspec, kvspec, kvspec, _na_bias_spec()],
        out_specs=[qspec, kvspec, kvspec, _na_bias_spec()],
        out_shape=[jax.ShapeDtypeStruct((L, GROUP_W), BF16), jax.ShapeDtypeStruct((L, GROUP_W), F32),
                   jax.ShapeDtypeStruct((L, GROUP_W), F32),
                   jax.ShapeDtypeStruct((4, NA_ROWS, 2, NA_HALF, NA_WIN), F32)], scratch_shapes=[])


def _dil_specs(n, tq):
    R = DIL_RADIUS
    step, nb = tq // R, n // R
    main = pl.BlockSpec((tq, LANES), lambda j, i: (i, j))
    prev = pl.BlockSpec((R, LANES), lambda j, i: (jnp.maximum(i * step - 1, 0), j))
    nxt = pl.BlockSpec((R, LANES), lambda j, i: (jnp.minimum((i + 1) * step, nb - 1), j))
    return main, prev, nxt


def _dil_masks():
    R = DIL_RADIUS
    r = np.arange(2 * R)[:, None] & (R - 1)
    c = np.arange(3 * R)[None, :]
    band = np.abs(c - R - r) <= R
    ok = np.stack([band, band & (c >= R), band & (c < 2 * R), band & (c >= R) & (c < 2 * R)])
    return jnp.asarray(np.where(ok, 0.0, NEG).astype(np.float32))


def _dil_mask_spec():
    return pl.BlockSpec((4, 2 * DIL_RADIUS, 3 * DIL_RADIUS), lambda j, i: (0, 0, 0))


def _dil_mask(m_ref, i, sb, n_tiles, n_blocks):
    idx = 0
    if sb == 0:
        idx = idx + jnp.where(i == 0, 1, 0)
    if sb == n_blocks - 1:
        idx = idx + jnp.where(i == n_tiles - 1, 2, 0)
    return m_ref[idx]


def _dil_fwd(q, k, v, dil, ride=None):
    n = q.shape[0]
    tq = _tile(n, DIL_TILE, DIL_RADIUS)

    def kern(q_ref, kp_ref, k_ref, kn_ref, vp_ref, v_ref, vn_ref, m_ref, o_ref, l_ref):
        i = pl.program_id(1)
        R = DIL_RADIUS
        ka = jnp.concatenate([kp_ref[...], k_ref[...], kn_ref[...]], axis=0)
        va = jnp.concatenate([vp_ref[...], v_ref[...], vn_ref[...]], axis=0)
        first = lax.broadcasted_iota(jnp.int32, (R, LANES), 1) < HEAD_DIM
        subs = range(tq // R)
        keys = lambda a, sb: a[sb * R:(sb + 3) * R]
        qs = [_na_stack(q_ref[sb * R:(sb + 1) * R, :], first) for sb in subs]
        s = [_dot(qs[sb], keys(ka, sb), _NT) for sb in subs]
        s = [s[sb] * ATT_SCALE + _dil_mask(m_ref, i, sb, n // tq, len(subs)) for sb in subs]
        m = [jnp.max(x, axis=-1, keepdims=True) for x in s]
        e = [jnp.exp(x - mx) for x, mx in zip(s, m)]
        den = [jnp.sum(x, axis=-1, keepdims=True) for x in e]
        o = [_dot((e[sb] / den[sb]).astype(BF16), keys(va, sb)) for sb in subs]
        for sb in subs:
            lse = m[sb] + jnp.log(den[sb])
            o_ref[sb * R:(sb + 1) * R, :] = jnp.where(first, o[sb][:R], o[sb][R:])
            l_ref[sb * R:(sb + 1) * R, :] = jnp.where(first, lse[:R], lse[R:])

    main, prev, nxt = _dil_specs(n, tq)
    (o, lse), got = _call(
        kern, (q, k, k, k, v, v, v, _dil_masks()), ride,
        lambda: _grid_edges(2 * dil, n // tq), name=f"dil_fwd_{dil}", grid=(2 * dil, n // tq),
        in_specs=[main, prev, main, nxt, prev, main, nxt, _dil_mask_spec()], out_specs=[main, main],
        out_shape=[jax.ShapeDtypeStruct((n, dil * GROUP_W), F32)] * 2, scratch_shapes=[])
    return (o, lse), got


def _dil_bwd(q, k, v, do, lse, dterm, dil, ride=None):
    n = q.shape[0]
    R = DIL_RADIUS
    tq = _tile(n, DIL_TILE, R)
    nq = n // tq

    def kern(q_ref, kp_ref, k_ref, kn_ref, vp_ref, v_ref, vn_ref, do_ref, l_ref, dt_ref, m_ref, dq_ref, dk_ref, dv_ref):
        i = pl.program_id(1)

        @pl.when(i == 0)
        def _():
            dk_ref[...] = jnp.zeros_like(dk_ref)
            dv_ref[...] = jnp.zeros_like(dv_ref)

        ka = jnp.concatenate([kp_ref[...], k_ref[...], kn_ref[...]], axis=0)
        va = jnp.concatenate([vp_ref[...], v_ref[...], vn_ref[...]], axis=0)
        first = lax.broadcasted_iota(jnp.int32, (R, LANES), 1) < HEAD_DIM
        subs = range(tq // R)
        keys = lambda a, sb: a[sb * R:(sb + 3) * R]
        rows = lambda ref, sb: ref[sb * R:(sb + 1) * R, :]
        per_head = lambda t: jnp.concatenate([t[:, 0:1], t[:, HEAD_DIM:HEAD_DIM + 1]], axis=0)
        qs = [_na_stack(rows(q_ref, sb), first) for sb in subs]
        dos = [_na_stack(rows(do_ref, sb), first) for sb in subs]
        s = [_dot(qs[sb], keys(ka, sb), _NT) for sb in subs]
        dp = [_dot(dos[sb], keys(va, sb), _NT) for sb in subs]
        pn = [jnp.exp(s[sb] * ATT_SCALE + _dil_mask(m_ref, i, sb, nq, len(subs)) - per_head(rows(l_ref, sb))) for sb in subs]
        dsb = [(pn[sb] * (dp[sb] - per_head(rows(dt_ref, sb)))).astype(BF16) for sb in subs]
        pnb = [x.astype(BF16) for x in pn]
        dq = [_dot(dsb[sb], keys(ka, sb)) for sb in subs]
        dk = [_dot(dsb[sb], qs[sb], _TN) for sb in subs]
        dv = [_dot(pnb[sb], dos[sb], _TN) for sb in subs]
        zeros = lambda blocks: [jnp.zeros((blocks * R, LANES), F32)] if blocks else []
        pad = lambda t, sb: jnp.concatenate(zeros(sb) + [t] + zeros(len(subs) - 1 - sb), axis=0)
        dka = sum(pad(dk[sb], sb) for sb in subs) * ATT_SCALE
        dva = sum(pad(dv[sb], sb) for sb in subs)
        for sb in subs:
            dq_ref[sb * R:(sb + 1) * R, :] = jnp.where(first, dq[sb][:R], dq[sb][R:]) * ATT_SCALE
        r0 = pl.multiple_of(i * tq, R)
        dk_ref[pl.ds(r0, tq), :] += dka[R:R + tq]
        dv_ref[pl.ds(r0, tq), :] += dva[R:R + tq]

        @pl.when(i > 0)
        def _():
            dk_ref[pl.ds(r0 - R, R), :] += dka[:R]
            dv_ref[pl.ds(r0 - R, R), :] += dva[:R]

        @pl.when(i < nq - 1)
        def _():
            dk_ref[pl.ds(r0 + tq, R), :] += dka[R + tq:]
            dv_ref[pl.ds(r0 + tq, R), :] += dva[R + tq:]

    main, prev, nxt = _dil_specs(n, tq)
    whole = pl.BlockSpec((n, LANES), lambda j, i: (0, j))
    shp = jax.ShapeDtypeStruct((n, dil * GROUP_W), F32)
    (dq, dk, dv), got = _call(
        kern, (q, k, k, k, v, v, v, do, lse, dterm, _dil_masks()), ride,
        lambda: _grid_edges(2 * dil, nq), name=f"dil_bwd_{dil}", grid=(2 * dil, nq),
        in_specs=[main, prev, main, nxt, prev, main, nxt, main, main, main, _dil_mask_spec()],
        out_specs=[main, whole, whole],
        out_shape=[shp] * 3, scratch_shapes=[])
    return (dq, dk, dv), got


def _dil_weights(lses):
    m = jnp.maximum(jnp.maximum(lses[0], lses[1]), lses[2])
    e = [jnp.exp(l - m) for l in lses]
    tot = e[0] + e[1] + e[2]
    return [x / tot for x in e]


def _dilated_fwd(qkv, rides):
    dils = [dil for _, dil in DIL_PAIRS]
    res, got = [], {}
    for dil in dils:
        r, got[f"dil{dil}"] = _dil_fwd(*qkv[dil], dil, rides.get(f"dil{dil}"))
        res.append(r)

    def body(i, nt, o0, o1, o2, l0, l1, l2):
        w = _dil_weights((l0, l1, l2))
        return w[0] * o0 + w[1] * o1 + w[2] * o2

    ins = [Row(r[0], dil=d) for r, d in zip(res, dils)] + [Row(r[1], dil=d) for r, d in zip(res, dils)]
    return _rows("dil_combine", body, 512, ins, [("row", GROUP_W, BF16)])[0], res, got


def _dilated_bwd(dy, dy_cb, qkv, saved, cosf, sinf, rides):
    dils = [dil for _, dil in DIL_PAIRS]
    def split(i, nt, dy, o0, o1, o2, l0, l1, l2, bd):
        w = _dil_weights((l0, l1, l2))
        y = w[0] * o0 + w[1] * o1 + w[2] * o2
        dyy = _head_sum(dy * y, bd)
        return tuple(wg * dy for wg in w) + tuple(wg * dyy for wg in w)

    ins = ([Row(dy, GROUP_W, dy_cb)] + [Row(r[0], dil=d) for r, d in zip(saved, dils)]
           + [Row(r[1], dil=d) for r, d in zip(saved, dils)])
    outs = _rows("dil_split_bwd", split, 512, ins + [Full(_block_ones(GROUP_W, HEAD_DIM))],
                 [("row", GROUP_W, BF16, d) for d in dils] + [("row", GROUP_W, F32, d) for d in dils])
    g, got = [], {}
    for b, dil in enumerate(dils):
        r, got[f"dil{dil}"] = _dil_bwd(*qkv[dil], outs[b], saved[b][1], outs[3 + b], dil, rides.get(f"dil{dil}"))
        g.append(r)

    def finish(i, nt, q0, q1, q2, k0, k1, k2, v0, v1, v2, c, s):
        dq, dk = q0 + q1 + q2, k0 + k1 + k2
        return dq * c + _swap_halves(dq * s), dk * c + _swap_halves(dk * s), v0 + v1 + v2

    ins = [Row(g[b][t], dil=dils[b]) for t in range(3) for b in range(3)] + [Row(cosf), Row(sinf)]
    return _rows("dil_finish_bwd", finish, 512, ins, [("row", GROUP_W, BF16)] * 3), got


def _layer_fwd(x, W, l, cosf, sinf, rides):
    rides = {c: Exchange(items) for c, items in rides.items()}
    h1 = _rms_fwd(x, W["mix_norm_pre"][l][None], "mix_norm")
    p = _mm(h1, W["w_in"][l], "nn", F32, "proj_in")
    ya, sa, got_gla = _gla_fwd(p, W, l, rides.get("gla"))
    (qb, kb, vb), qkv_d = _attn_prep(p, cosf, sinf)
    bias = _na_bias(W["na_rpb"][l])
    yb, got_na = _na_fwd(qb, kb, vb, bias, rides.get("na"))
    yc, sc = _lru_fwd(p, W, l)
    yd, sd, got = _dilated_fwd(qkv_d, rides)
    got.update(gla=got_gla, na=got_na)
    ycat = jnp.concatenate([ya, yb, yc, yd], axis=1)
    ymix = _mm(ycat, W["w_out"][l], "nn", F32, "proj_out", tm=1024)
    xm = _rms_resid_fwd(x, ymix, W["mix_norm_post"][l][None], "mix_resid")
    h2 = _rms_fwd(xm, W["ffn_norm_pre"][l][None], "ffn_norm")
    gu, act = _ffn_in_swiglu(h2, W["ffn_w_in"][l])
    f = _mm(act, W["ffn_w_out"][l], "nn", F32, "ffn_out")
    xo = _rms_resid_fwd(xm, f, W["ffn_norm_post"][l][None], "ffn_resid")
    saved = dict(x=x, h1=h1, p=p, sa=sa, att=(qb, kb, vb, qkv_d), bias=bias, sc=sc, sd=sd, ycat=ycat, ymix=ymix,
                 xm=xm, h2=h2, gu=gu, act=act, f=f)
    return xo, saved, got


def _layer_bwd(dxo, W, l, S, cosf, sinf, rides, early=None):
    g = {}
    df, g["ffn_norm_post"] = _rms_bwd(dxo, S["f"], W["ffn_norm_post"][l][None], "ffn_resid_bwd", out_dtype=BF16)
    g["ffn_w_out"] = _mm(S["act"], df, "tn", BF16, "ffn_out_dw", tm=256, tk=4096)
    dgu = _ffn_out_dx_swiglu(df, W["ffn_w_out"][l], S["gu"])
    dh2 = _mm(dgu, W["ffn_w_in"][l], "nt", F32, "ffn_in_dx")
    g["ffn_w_in"] = _mm(S["h2"], dgu, "tn", BF16, "ffn_in_dw", tm=1024, tn=512, tk=4096)
    dxm, g["ffn_norm_pre"] = _rms_bwd(dh2, S["xm"], W["ffn_norm_pre"][l][None], "ffn_norm_bwd", resid=dxo)
    dymix, g["mix_norm_post"] = _rms_bwd(dxm, S["ymix"], W["mix_norm_post"][l][None], "mix_resid_bwd", out_dtype=BF16)
    dycat = _mm(dymix, W["w_out"][l], "nt", F32, "proj_out_dx", tm=1024)
    g["w_out"] = _mm(S["ycat"], dymix, "tn", BF16, "proj_out_dw", tm=1024, tn=512, tk=4096)
    if early is not None:
        for c, items in early(g).items():
            rides = {**rides, c: rides.get(c, []) + items}
    rides = {c: Exchange(items) for c, items in rides.items()}
    p = S["p"]
    qb, kb, vb, qkv_d = S["att"]
    (dqa, dka, dva, dga, dz), ga, got_gla = _gla_bwd(dycat, 0, p, W, l, S["sa"], rides.get("gla"))
    (dqb, dkb, dvb, dbias), got_na = _na_bwd(dycat, 2, qb, kb, vb, S["bias"], rides.get("na"))
    g["na_rpb"] = _na_bias_bwd(dbias)
    dxc, dgc, gc = _lru_bwd(dycat, 2, p, W, l, S["sc"])
    (dqd, dkd, dvd), got = _dilated_bwd(dycat, 3, qkv_d, S["sd"], cosf, sinf, rides)
    got.update(gla=got_gla, na=got_na)
    g.update(ga)
    g.update(gc)
    dp = jnp.concatenate([dqa, dka, dva, dga, dqb, dkb.astype(BF16), dvb.astype(BF16), dxc, dgc, dqd, dkd, dvd, dz], axis=1)
    dh1 = _mm(dp, W["w_in"][l], "nt", F32, "proj_in_dx")
    g["w_in"] = _mm(S["h1"], dp, "tn", BF16, "proj_in_dw", tm=1024, tn=640, tk=4096)
    dx, g["mix_norm_pre"] = _rms_bwd(dh1, S["x"], W["mix_norm_pre"][l][None], "mix_norm_bwd", resid=dxm)
    for n in ("ffn_norm_post", "ffn_norm_pre", "mix_norm_post", "mix_norm_pre"):
        g[n] = g[n][0]
    return dx, g, got


MESH_AXES = ("x", "y", "c")


class Xfer:
    def __init__(self, arr, kind):
        self.arr, self.kind = arr, kind
        shp = arr.shape
        if kind == "all":
            self.out = (N_DEV,) + shp
        elif kind == "slot":
            self.out = shp
        elif kind == "rows":
            self.r = shp[1] // N_DEV
            self.out = (N_DEV, shp[0], self.r, shp[2])
        else:
            self.r = shp[1]
            self.out = (shp[0], N_DEV * shp[1], shp[2])

    def src(self, ref, peer):
        if self.kind == "slot":
            return ref.at[peer]
        if self.kind == "rows":
            return ref.at[:, pl.ds(peer * self.r, self.r), :]
        return ref

    def dst(self, ref, me):
        if self.kind == "place":
            return ref.at[:, pl.ds(me * self.r, self.r), :]
        return ref.at[me]


class Exchange:
    def __init__(self, items):
        n = len(items)
        self.items = items
        self.arrays = [it.arr for it in items]
        self.specs = [pl.BlockSpec(memory_space=pl.ANY)] * n
        self.out_shape = [jax.ShapeDtypeStruct(it.out, it.arr.dtype) for it in items]
        self.scratch = [pltpu.SemaphoreType.DMA((n * (N_DEV - 1),)), pltpu.SemaphoreType.DMA((n * (N_DEV - 1),)),
                        pltpu.SemaphoreType.DMA((n,))]

    def copies(self, ins, outs, sems):
        send_sems, recv_sems, local_sems = sems
        x, y, c = (lax.axis_index(a) for a in MESH_AXES)
        me = 4 * x + 2 * y + c
        out = []
        for t, it in enumerate(self.items):
            out.append(pltpu.make_async_copy(it.src(ins[t], me), it.dst(outs[t], me), local_sems.at[t]))
            for k in range(1, N_DEV):
                px, py, pc = x ^ ((k >> 2) & 1), y ^ ((k >> 1) & 1), c ^ (k & 1)
                s = t * (N_DEV - 1) + k - 1
                out.append(pltpu.make_async_remote_copy(
                    src_ref=it.src(ins[t], 4 * px + 2 * py + pc), dst_ref=it.dst(outs[t], me),
                    send_sem=send_sems.at[s], recv_sem=recv_sems.at[s], device_id=(px, py, pc),
                    device_id_type=pl.DeviceIdType.MESH))
        return out

    def start(self, ins, outs, sems):
        for cp in self.copies(ins, outs, sems):
            cp.start()

    def wait(self, ins, outs, sems):
        for cp in self.copies(ins, outs, sems):
            cp.wait()


def _exchange(items, name):
    ex = Exchange(items)
    n = len(items)

    def body(*refs):
        ex.start(refs[:n], refs[n:2 * n], refs[2 * n:])
        ex.wait(refs[:n], refs[n:2 * n], refs[2 * n:])

    return pl.pallas_call(body, name=name, out_shape=ex.out_shape, in_specs=ex.specs, out_specs=ex.specs,
                          scratch_shapes=ex.scratch)(*ex.arrays)


def _call(kern, arrays, ride, edges, *, name, grid, in_specs, out_specs, out_shape, scratch_shapes):
    params = _ARB(len(grid))
    if ride is None:
        return pl.pallas_call(kern, name=name, grid=grid, in_specs=in_specs, out_specs=out_specs, out_shape=out_shape,
                              scratch_shapes=scratch_shapes, compiler_params=params)(*arrays), None
    ni, no, ns, nx = len(in_specs), len(out_specs), len(scratch_shapes), len(ride.items)

    def wrapped(*refs):
        ins, xin = refs[:ni], refs[ni:ni + nx]
        outs, xout = refs[ni + nx:ni + nx + no], refs[ni + nx + no:ni + 2 * nx + no]
        scr, sems = refs[ni + 2 * nx + no:ni + 2 * nx + no + ns], refs[ni + 2 * nx + no + ns:]
        first, last = edges()

        @pl.when(first)
        def _():
            ride.start(xin, xout, sems)

        kern(*ins, *outs, *scr)

        @pl.when(last)
        def _():
            ride.wait(xin, xout, sems)

    res = pl.pallas_call(
        wrapped, name=name, grid=grid, in_specs=list(in_specs) + ride.specs, out_specs=list(out_specs) + ride.specs,
        out_shape=list(out_shape) + ride.out_shape, scratch_shapes=list(scratch_shapes) + ride.scratch,
        compiler_params=params)(*arrays, *ride.arrays)
    return res[:no], res[no:]


def _column_segments(width, permuted):
    z0, z1, zn = 4 * GROUP_W, 4 * GROUP_W + 2 * GLA_RANK, 12 * GROUP_W
    segs = []
    for d in range(N_DEV):
        lo, hi = d * width, (d + 1) * width
        if not permuted:
            segs.append([(0, width, lo)])
            continue
        runs = []
        for a, b, shift in ((0, z0, 0), (z0, z1, zn - z0), (z1, 10 ** 9, -(z1 - z0))):
            s, e = max(lo, a), min(hi, b)
            if s < e:
                runs.append((s - lo, e - lo, s + shift))
        segs.append(runs)
    return segs


def _cols_from_pieces(pieces, segs, cols, name):
    _, R, w = pieces.shape
    tm = _tile(R, 256, 16)
    used = max(f + (b - a) for runs in segs for a, b, f in runs)

    def kern(p_ref, o_ref):
        for d, runs in enumerate(segs):
            for a, b, f in runs:
                o_ref[:, f:f + (b - a)] = p_ref[d, :, a:b]
        if used < cols:
            o_ref[:, used:cols] = jnp.zeros((tm, cols - used), o_ref.dtype)

    return pl.pallas_call(
        kern, name=name, grid=(R // tm,), in_specs=[pl.BlockSpec((N_DEV, tm, w), lambda i: (0, i, 0))],
        out_specs=pl.BlockSpec((tm, cols), lambda i: (i, 0)), out_shape=jax.ShapeDtypeStruct((R, cols), pieces.dtype),
        compiler_params=_ARB(1),
    )(pieces)


def _pieces_from_cols(full, segs, w, name):
    R, cols = full.shape
    tm = _tile(R, 256, 16)

    def kern(f_ref, o_ref):
        for d, runs in enumerate(segs):
            for a, b, f in runs:
                o_ref[d, :, a:b] = f_ref[:, f:f + (b - a)]

    return pl.pallas_call(
        kern, name=name, grid=(R // tm,), in_specs=[pl.BlockSpec((tm, cols), lambda i: (i, 0))],
        out_specs=pl.BlockSpec((N_DEV, tm, w), lambda i: (0, i, 0)),
        out_shape=jax.ShapeDtypeStruct((N_DEV, R, w), full.dtype), compiler_params=_ARB(1),
    )(full)


def _sum_slots(recv, name):
    n, R, C = recv.shape
    tm = _tile(R, 256, 16)

    def kern(*refs):
        acc = refs[0][...].astype(F32)
        for r in refs[1:n]:
            acc = acc + r[...].astype(F32)
        refs[n][...] = acc

    return pl.pallas_call(
        kern, name=name, grid=(R // tm,),
        in_specs=[pl.BlockSpec((None, tm, C), lambda i, _s=s: (_s, i, 0)) for s in range(n)],
        out_specs=pl.BlockSpec((tm, C), lambda i: (i, 0)), out_shape=jax.ShapeDtypeStruct((R, C), F32),
        compiler_params=_ARB(1),
    )(*([recv] * n))


BIG = (("w_in", 2), ("w_out", 1), ("ffn_w_in", 2), ("ffn_w_out", 1))
SMALL_SHARDED = ("gla_w_gate", "gla_b_gate", "lru_conv_w", "lru_b_a", "lru_b_x", "lru_lambda")
REPLICATED = ("mix_norm_pre", "mix_norm_post", "gla_norm", "na_rpb", "lru_conv_b", "lru_w_a", "lru_w_x",
              "ffn_norm_pre", "ffn_norm_post")
WEIGHTS = ("mix_norm_pre", "mix_norm_post", "w_in", "gla_w_gate", "gla_b_gate", "gla_norm", "na_rpb", "lru_conv_w",
           "lru_conv_b", "lru_w_a", "lru_b_a", "lru_w_x", "lru_b_x", "lru_lambda", "w_out", "ffn_norm_pre",
           "ffn_norm_post", "ffn_w_in", "ffn_w_out")
FLAT_C = 1024


def _to_rows(vec, row_unit):
    n = vec.shape[-1]
    rows = -(-n // (FLAT_C * row_unit)) * row_unit
    pad = [(0, 0)] * (vec.ndim - 1) + [(0, rows * FLAT_C - n)]
    return jnp.pad(vec, pad).reshape(vec.shape[:-1] + (rows, FLAT_C))


def _unshard(parts, axis):
    t = jnp.moveaxis(parts, 0, axis)
    shp = list(t.shape)
    return t.reshape(shp[:axis] + [shp[axis] * shp[axis + 1]] + shp[axis + 2:])


def _shards(full, axis):
    shp = list(full.shape)
    t = full.reshape(shp[:axis] + [N_DEV, shp[axis] // N_DEV] + shp[axis + 1:])
    return jnp.moveaxis(t, axis, 0)


def _weight_rides(W, l):
    bf = lambda n: W[n][l].astype(BF16)
    ffn = bf("ffn_w_in")
    half = ffn.shape[0] // 2
    return {"gla": [Xfer(bf("w_in"), "all"), Xfer(bf("w_out")[None], "place")],
            "na": [Xfer(bf("ffn_w_out")[None], "place")],
            "dil1": [Xfer(ffn[:half], "all")], "dil4": [Xfer(ffn[half:], "all")]}


def _unpack_weights(full, W, got):
    w_in_w, ffn_w = W["w_in"].shape[-1], W["ffn_w_in"].shape[-1]
    full["w_in"].append(_cols_from_pieces(got["gla"][0], _column_segments(w_in_w, True), P_COLS, "unpack_w_in"))
    full["w_out"].append(got["gla"][1][0])
    full["ffn_w_out"].append(got["na"][0][0])
    full["ffn_w_in"].append(jnp.concatenate(
        [_cols_from_pieces(got[c][0], _column_segments(ffn_w, False), N_DEV * ffn_w, "unpack_ffn_w_in")
         for c in ("dil1", "dil4")], axis=0))


def _w_in_grad_pieces(g, W):
    w_in_w = W["w_in"].shape[-1]
    return _pieces_from_cols(g["w_in"], _column_segments(w_in_w, True), w_in_w, "pack_w_in")


def _ffn_grad_halves(g, W):
    ffn_w = W["ffn_w_in"].shape[-1]
    p_ffn = _pieces_from_cols(g["ffn_w_in"], _column_segments(ffn_w, False), ffn_w, "pack_ffn_w_in")
    half = p_ffn.shape[1] // 2
    return p_ffn[:, :half], p_ffn[:, half:]


def _grad_rides(g, W):
    top, bottom = _ffn_grad_halves(g, W)
    return {"gla": [Xfer(_w_in_grad_pieces(g, W), "slot"), Xfer(g["w_out"][None], "rows")],
            "na": [Xfer(g["ffn_w_out"][None], "rows"), Xfer(top, "slot")], "dil1": [Xfer(bottom, "slot")]}


def _early_grad_rides(g, W):
    top, bottom = _ffn_grad_halves(g, W)
    return {"gla": [Xfer(g["w_out"][None], "rows"), Xfer(g["ffn_w_out"][None], "rows")], "na": [Xfer(top, "slot")],
            "dil4": [Xfer(bottom, "slot")]}


def _sum_big(w_in, w_out, ffn_w_out, ffn_top, ffn_bottom):
    s = lambda r, n: _sum_slots(r.reshape(N_DEV, -1, r.shape[-1]), "sum_" + n)
    return {"w_in": s(w_in, "w_in"), "w_out": s(w_out, "w_out"), "ffn_w_out": s(ffn_w_out, "ffn_w_out"),
            "ffn_w_in": jnp.concatenate([s(ffn_top, "ffn_w_in"), s(ffn_bottom, "ffn_w_in")], axis=0)}


def _exchange_named(rides, extra, name):
    names = list(rides)
    res = _exchange([it for n in names for it in rides[n]] + extra, name)
    got, at = {}, 0
    for n in names:
        got[n] = res[at:at + len(rides[n])]
        at += len(rides[n])
    return got, res[at:]


def _train(x, target, W):
    L = x.shape[0]
    depth = W["w_in"].shape[0]
    cosf, sinf = _rope_tables(L)
    small = jnp.concatenate([W[n].reshape(-1) for n in SMALL_SHARDED])
    small16 = _to_rows(lax.bitcast_convert_type(small, jnp.uint16).reshape(-1), 16)
    got, (sm,) = _exchange_named(_weight_rides(W, 0), [Xfer(small16, "all")], "gather_first")
    full = dict(W, w_in=[], w_out=[], ffn_w_in=[], ffn_w_out=[])
    _unpack_weights(full, W, got)
    sm = lax.bitcast_convert_type(sm.reshape(N_DEV, -1)[:, :2 * small.size].reshape(N_DEV, small.size, 2), F32)
    off = 0
    for n in SMALL_SHARDED:
        full[n] = _unshard(sm[:, off:off + W[n].size].reshape((N_DEV,) + W[n].shape), W[n].ndim - 1)
        off += W[n].size

    saved = []
    for l in range(depth):
        x, S, got = _layer_fwd(x, full, l, cosf, sinf, _weight_rides(W, l + 1) if l + 1 < depth else {})
        saved.append(S)
        if l + 1 < depth:
            _unpack_weights(full, W, got)
    loss, dx = _loss_fwd_bwd(x, target)

    grads, big, rides = [None] * depth, [None] * depth, {}
    for l in reversed(range(depth)):
        early = (lambda g: _early_grad_rides(g, W)) if l == 0 else None
        dx, grads[l], got = _layer_bwd(dx, full, l, saved[l], cosf, sinf, rides, early)
        if l + 1 < depth:
            big[l + 1] = _sum_big(got["gla"][0], got["gla"][1], got["na"][0], got["na"][1], got["dil1"][0])
        late = {c: len(items) for c, items in rides.items()}
        rides = _grad_rides(grads[l], W) if l > 0 else {}
    G = {n: jnp.stack([g[n] for g in grads]) for n in SMALL_SHARDED + REPLICATED}
    small_g = jnp.concatenate([_shards(G[n], G[n].ndim - 1).reshape(N_DEV, -1) for n in SMALL_SHARDED], axis=1)
    repl_g = jnp.concatenate([G[n].reshape(-1) for n in REPLICATED])
    rest = _exchange([Xfer(_w_in_grad_pieces(grads[0], W), "slot"), Xfer(_to_rows(small_g, 8), "slot"),
                      Xfer(_to_rows(repl_g, 8), "all")], "exchange_last")
    own = lambda c, k: got[c][late.get(c, 0) + k]
    big[0] = _sum_big(rest[0], own("gla", 0), own("gla", 1), own("na", 0), own("dil4", 0))
    rest = rest[1:]
    out = {n: jnp.stack([b[n] for b in big]).reshape(W[n].shape) for n, _ in BIG}
    for names, r, tag in ((SMALL_SHARDED, rest[0], "sum_small"), (REPLICATED, rest[1], "sum_replicated")):
        flat, off = _sum_slots(r, tag).reshape(-1), 0
        for n in names:
            out[n] = flat[off:off + W[n].size].reshape(W[n].shape)
            off += W[n].size
    return loss, dx, out


def _update(W, G, M, V):
    delta, new_m, new_v = {}, {}, {}
    for n, _ in BIG:
        two_d = lambda a: a.reshape(-1, a.shape[-1])
        d, m, v = _adamw(two_d(W[n]), two_d(G[n]), two_d(M[n]), two_d(V[n]), "adamw_" + n)
        delta[n], new_m[n], new_v[n] = (t.reshape(W[n].shape) for t in (d, m, v))
    rest = SMALL_SHARDED + REPLICATED
    pack = lambda D: _to_rows(jnp.concatenate([D[n].reshape(-1) for n in rest]), 16)
    d, m, v = _adamw(pack(W), pack(G), pack(M), pack(V), "adamw_small")
    off = 0
    for n in rest:
        sl = lambda t: t.reshape(-1)[off:off + W[n].size].reshape(W[n].shape)
        delta[n], new_m[n], new_v[n] = sl(d), sl(m), sl(v)
        off += W[n].size
    return delta, new_m, new_v


def kernel(x, mix_norm_pre, mix_norm_post, w_in, gla_w_gate, gla_b_gate, gla_norm, na_rpb, lru_conv_w, lru_conv_b, lru_w_a, lru_b_a, lru_w_x, lru_b_x, lru_lambda, w_out, ffn_norm_pre, ffn_norm_post, ffn_w_in, ffn_w_out, loss_target, m_mix_norm_pre, m_mix_norm_post, m_w_in, m_gla_w_gate, m_gla_b_gate, m_gla_norm, m_na_rpb, m_lru_conv_w, m_lru_conv_b, m_lru_w_a, m_lru_b_a, m_lru_w_x, m_lru_b_x, m_lru_lambda, m_w_out, m_ffn_norm_pre, m_ffn_norm_post, m_ffn_w_in, m_ffn_w_out, v_mix_norm_pre, v_mix_norm_post, v_w_in, v_gla_w_gate, v_gla_b_gate, v_gla_norm, v_na_rpb, v_lru_conv_w, v_lru_conv_b, v_lru_w_a, v_lru_b_a, v_lru_w_x, v_lru_b_x, v_lru_lambda, v_w_out, v_ffn_norm_pre, v_ffn_norm_post, v_ffn_w_in, v_ffn_w_out):
    W = dict(zip(WEIGHTS, (mix_norm_pre, mix_norm_post, w_in, gla_w_gate, gla_b_gate, gla_norm, na_rpb, lru_conv_w, lru_conv_b, lru_w_a, lru_b_a, lru_w_x, lru_b_x, lru_lambda, w_out, ffn_norm_pre, ffn_norm_post, ffn_w_in, ffn_w_out)))
    M = dict(zip(WEIGHTS, (m_mix_norm_pre, m_mix_norm_post, m_w_in, m_gla_w_gate, m_gla_b_gate, m_gla_norm, m_na_rpb, m_lru_conv_w, m_lru_conv_b, m_lru_w_a, m_lru_b_a, m_lru_w_x, m_lru_b_x, m_lru_lambda, m_w_out, m_ffn_norm_pre, m_ffn_norm_post, m_ffn_w_in, m_ffn_w_out)))
    V = dict(zip(WEIGHTS, (v_mix_norm_pre, v_mix_norm_post, v_w_in, v_gla_w_gate, v_gla_b_gate, v_gla_norm, v_na_rpb, v_lru_conv_w, v_lru_conv_b, v_lru_w_a, v_lru_b_a, v_lru_w_x, v_lru_b_x, v_lru_lambda, v_w_out, v_ffn_norm_pre, v_ffn_norm_post, v_ffn_w_in, v_ffn_w_out)))
    loss, dx, G = _train(x[0], loss_target[0], W)
    loss = lax.psum(loss, MESH_AXES)
    delta, new_m, new_v = _update(W, G, M, V)
    return (loss, dx[None], *[G[n] for n in WEIGHTS], *[delta[n] for n in WEIGHTS], *[new_m[n] for n in WEIGHTS],
            *[new_v[n] for n in WEIGHTS])
```

```python
import functools
import math

import numpy as np
import jax
import jax.numpy as jnp
from jax import lax
from jax.experimental import pallas as pl
from jax.experimental.pallas import tpu as pltpu

F32 = jnp.float32
BF16 = jnp.bfloat16

N_DEV = 8
HEAD_DIM = 64
GROUP_W = 256
GLA_RANK = 16
GLA_TAU = 16.0
GLA_CHUNK = 64
GRID_W = 64
NA_ROWS = 8
NA_COLS = 16
LRU_C = 8.0
DIL_PAIRS = ((128, 1), (512, 4), (2048, 16))
DIL_RADIUS = 64
DIL_TILE = 512
ROPE_THETA = 10000.0
EPS = 1e-6
ATT_SCALE = HEAD_DIM ** -0.5
NEG = -1e30
LANES = 128
P_COLS = 12 * GROUP_W + LANES
Z_BLOCK = 12 * GROUP_W // LANES

ADAM_LR = 0.001
ADAM_B1 = 0.9
ADAM_B2 = 0.999
ADAM_EPS = 1e-08
ADAM_WD = 0.01
ADAM_STEP = 10

VMEM_LIMIT = 56 * 1024 * 1024
_ARB = lambda n: pltpu.CompilerParams(dimension_semantics=("arbitrary",) * n, vmem_limit_bytes=VMEM_LIMIT)


def _tile(dim, pref, unit):
    t = min(pref, dim) // unit * unit
    while t >= unit:
        if dim % t == 0:
            return t
        t -= unit
    return dim


def _mm(a, b, mode, out_dtype, name, tm=512, tn=None, tk=None):
    if mode == "nn":
        (M, K), (_, N) = a.shape, b.shape
    elif mode == "nt":
        (M, K), (N, _) = a.shape, b.shape
    else:
        (K, M), (_, N) = a.shape, b.shape
    tm = _tile(M, tm, LANES if mode == "tn" else 8)
    tn = _tile(N, tn or N, LANES)
    tk = _tile(K, tk or K, LANES)
    nk = K // tk
    dims = {"nn": (((1,), (0,)), ((), ())), "nt": (((1,), (1,)), ((), ())), "tn": (((0,), (0,)), ((), ()))}[mode]

    def kern(a_ref, b_ref, o_ref, *acc):
        part = lax.dot_general(a_ref[...].astype(BF16), b_ref[...].astype(BF16), dims, preferred_element_type=F32)
        if nk == 1:
            o_ref[...] = part.astype(out_dtype)
            return
        k = pl.program_id(2)

        @pl.when(k == 0)
        def _():
            acc[0][...] = part

        @pl.when(jnp.logical_and(k > 0, k < nk - 1))
        def _():
            acc[0][...] += part

        @pl.when(k == nk - 1)
        def _():
            o_ref[...] = (acc[0][...] + part).astype(out_dtype)

    a_spec = pl.BlockSpec((tk, tm), lambda i, j, k: (k, i)) if mode == "tn" else pl.BlockSpec((tm, tk), lambda i, j, k: (i, k))
    b_spec = pl.BlockSpec((tn, tk), lambda i, j, k: (j, k)) if mode == "nt" else pl.BlockSpec((tk, tn), lambda i, j, k: (k, j))
    return pl.pallas_call(
        kern, name=name, grid=(M // tm, N // tn, nk),
        in_specs=[a_spec, b_spec], out_specs=pl.BlockSpec((tm, tn), lambda i, j, k: (i, j)),
        out_shape=jax.ShapeDtypeStruct((M, N), out_dtype),
        scratch_shapes=[pltpu.VMEM((tm, tn), F32)] if nk > 1 else [],
        compiler_params=_ARB(3),
    )(a, b)


class Row:
    def __init__(self, a, width=None, cb=0, halo=False, dil=1):
        self.a, self.width, self.cb, self.halo, self.dil = a, width, cb, halo, dil


class Full:
    def __init__(self, a):
        self.a = a


HALO = 8


def _rows(name, body, tm, ins, outs):
    outs = [o if len(o) == 4 else o + (1,) for o in outs]
    L = next(s.a.shape[0] * s.dil for s in ins if isinstance(s, Row))
    tm = _tile(L, tm, 16)
    dilated = any(s.dil > 1 for s in ins if isinstance(s, Row)) or any(o[3] > 1 for o in outs)
    nt = L // tm
    nb8 = L // HALO
    step = tm // HALO
    in_specs, arrays, layout = [], [], []
    for s in ins:
        if isinstance(s, Full):
            nd = s.a.ndim
            in_specs.append(pl.BlockSpec(s.a.shape, lambda i, _nd=nd: (0,) * _nd))
            arrays.append(s.a)
            layout.append(1)
        else:
            w = s.width or s.a.shape[1]
            in_specs.append(pl.BlockSpec((tm // s.dil, w), lambda i, _cb=s.cb: (i, _cb)))
            arrays.append(s.a)
            if s.dil > 1:
                layout.append(-s.dil)
            elif s.halo:
                in_specs.append(pl.BlockSpec((HALO, w), lambda i, _cb=s.cb: (jnp.maximum(i * step - 1, 0), _cb)))
                in_specs.append(pl.BlockSpec((HALO, w), lambda i, _cb=s.cb: (jnp.minimum((i + 1) * step, nb8 - 1), _cb)))
                arrays += [s.a, s.a]
                layout.append(3)
            else:
                layout.append(1)
    out_specs, out_shapes = [], []
    for kind, shp, dt, dil in outs:
        if kind == "row":
            out_specs.append(pl.BlockSpec((tm // dil, dil * shp), lambda i: (i, 0)))
            out_shapes.append(jax.ShapeDtypeStruct((L // dil, dil * shp), dt))
        else:
            out_specs.append(pl.BlockSpec(shp, lambda i, _n=len(shp): (0,) * _n))
            out_shapes.append(jax.ShapeDtypeStruct(shp, dt))
    n_in, n_out = len(arrays), len(outs)

    def kern(*refs):
        i = pl.program_id(0)
        lo, hi = refs[n_in + n_out:] if dilated else (None, None)

        def undilate(ref, d):
            for j in range(d):
                rows = pl.ds(j, tm // d, stride=d)
                lo[rows, :] = ref[:, j * GROUP_W:j * GROUP_W + LANES].astype(F32)
                hi[rows, :] = ref[:, j * GROUP_W + LANES:(j + 1) * GROUP_W].astype(F32)
            return jnp.concatenate([lo[...], hi[...]], axis=1)

        def dilate(val, ref, d, dt):
            lo[...] = val[:, :LANES].astype(F32)
            hi[...] = val[:, LANES:].astype(F32)
            for j in range(d):
                rows = pl.ds(j, tm // d, stride=d)
                ref[:, j * GROUP_W:j * GROUP_W + LANES] = lo[rows, :].astype(dt)
                ref[:, j * GROUP_W + LANES:(j + 1) * GROUP_W] = hi[rows, :].astype(dt)

        vals, p = [], 0
        for n in layout:
            if n == 1:
                vals.append(refs[p][...])
            elif n < 0:
                vals.append(undilate(refs[p], -n))
                n = 1
            else:
                vals.append((refs[p + 1][...], refs[p][...], refs[p + 2][...]))
            p += n
        res = body(i, nt, *vals)
        if not isinstance(res, (tuple, list)):
            res = (res,)
        for (kind, shp, dt, dil), o_ref, r in zip(outs, refs[n_in:], res):
            if kind == "row" and dil > 1:
                dilate(r, o_ref, dil, dt)
            elif kind == "row":
                o_ref[...] = r.astype(dt)
            else:
                @pl.when(i == 0)
                def _(o_ref=o_ref):
                    o_ref[...] = jnp.zeros_like(o_ref)
                o_ref[...] += r.astype(dt)

    res = pl.pallas_call(
        kern, name=name, grid=(nt,), in_specs=in_specs, out_specs=out_specs, out_shape=out_shapes,
        scratch_shapes=[pltpu.VMEM((tm, LANES), F32)] * 2 if dilated else [], compiler_params=_ARB(1),
    )(*arrays)
    return res


def _shift(h, o, i, nt):
    prev, cur, nxt = h
    if o == 0:
        return cur
    tm = cur.shape[0]
    cat = jnp.concatenate([prev, cur, nxt], axis=0)
    sh = pltpu.roll(cat, (-o) % (tm + 2 * HALO), axis=0)[HALO:HALO + tm]
    row = lax.broadcasted_iota(jnp.int32, cur.shape, 0)
    if o < 0:
        ok = jnp.logical_or(i > 0, row >= -o)
    else:
        ok = jnp.logical_or(i < nt - 1, row < tm - o)
    return jnp.where(ok, sh, 0.0)


def _colsum(v):
    return jnp.sum(v, axis=0, keepdims=True)


def _sigmoid(x):
    return 1.0 / (1.0 + jnp.exp(-x))


def _softplus(x):
    return jnp.maximum(x, 0.0) + jnp.log1p(jnp.exp(-jnp.abs(x)))


def _silu(x):
    return x * _sigmoid(x)


def _dsilu(x):
    s = _sigmoid(x)
    return s * (1.0 + x * (1.0 - s))


_GELU_C = math.sqrt(2.0 / math.pi)


def _gelu(x):
    return 0.5 * x * (1.0 + jnp.tanh(_GELU_C * (x + 0.044715 * x * x * x)))


def _dgelu(x):
    t = jnp.tanh(_GELU_C * (x + 0.044715 * x * x * x))
    return 0.5 * (1.0 + t) + 0.5 * x * (1.0 - t * t) * _GELU_C * (1.0 + 3.0 * 0.044715 * x * x)


def _head_sum(v, bd):
    return jnp.dot(v, bd, precision=lax.Precision.HIGHEST, preferred_element_type=F32)


def _block_ones(n, blk):
    r = np.arange(n)
    return jnp.asarray((r[:, None] // blk == r[None, :] // blk).astype(np.float32))


def _rms_fwd(x, g, name):
    def body(i, nt, x, g):
        r = lax.rsqrt(jnp.mean(x * x, axis=-1, keepdims=True) + EPS)
        return x * r * g
    return _rows(name, body, 256, [Row(x), Full(g)], [("row", x.shape[1], BF16)])[0]


def _rms_resid_fwd(x, y, g, name):
    def body(i, nt, x, y, g):
        r = lax.rsqrt(jnp.mean(y * y, axis=-1, keepdims=True) + EPS)
        return x + y * r * g
    return _rows(name, body, 256, [Row(x), Row(y), Full(g)], [("row", x.shape[1], F32)])[0]


def _rms_bwd(dy, x, g, name, resid=None, out_dtype=F32):
    D = x.shape[1]

    def body(i, nt, dy, x, g, *rest):
        dy = dy.astype(F32)
        r = lax.rsqrt(jnp.mean(x * x, axis=-1, keepdims=True) + EPS)
        xh = x * r
        dxh = dy * g
        dx = r * (dxh - xh * jnp.mean(dxh * xh, axis=-1, keepdims=True))
        if rest:
            dx = dx + rest[0]
        return dx, _colsum(dy * xh)

    ins = [Row(dy), Row(x), Full(g)] + ([Row(resid)] if resid is not None else [])
    return _rows(name, body, 256, ins, [("row", D, out_dtype), ("acc", (1, D), F32)])


def _ffn_in_swiglu(h, w):
    (M, K), N = h.shape, w.shape[1]
    F = N // 2
    tm = _tile(M, 256, 16)

    def kern(a_ref, b_ref, gu_ref, act_ref):
        gu = _dot(a_ref[...], b_ref[...])
        gu_ref[...] = gu
        act_ref[...] = (_silu(gu[:, :F]) * gu[:, F:]).astype(BF16)

    return pl.pallas_call(
        kern, name="ffn_in_swiglu", grid=(M // tm,),
        in_specs=[pl.BlockSpec((tm, K), lambda i: (i, 0)), pl.BlockSpec((K, N), lambda i: (0, 0))],
        out_specs=[pl.BlockSpec((tm, N), lambda i: (i, 0)), pl.BlockSpec((tm, F), lambda i: (i, 0))],
        out_shape=[jax.ShapeDtypeStruct((M, N), F32), jax.ShapeDtypeStruct((M, F), BF16)], compiler_params=_ARB(1),
    )(h, w)


def _ffn_out_dx_swiglu(df, w, gu):
    (M, K), N = df.shape, gu.shape[1]
    F = N // 2
    tm = _tile(M, 256, 16)

    def kern(a_ref, b_ref, gu_ref, o_ref):
        da = _dot(a_ref[...], b_ref[...], _NT)
        gu = gu_ref[...]
        gate, up = gu[:, :F], gu[:, F:]
        o_ref[:, :F] = (da * up * _dsilu(gate)).astype(BF16)
        o_ref[:, F:] = (da * _silu(gate)).astype(BF16)

    return pl.pallas_call(
        kern, name="ffn_out_dx_swiglu", grid=(M // tm,),
        in_specs=[pl.BlockSpec((tm, K), lambda i: (i, 0)), pl.BlockSpec((F, K), lambda i: (0, 0)),
                  pl.BlockSpec((tm, N), lambda i: (i, 0))],
        out_specs=pl.BlockSpec((tm, N), lambda i: (i, 0)), out_shape=jax.ShapeDtypeStruct((M, N), BF16),
        compiler_params=_ARB(1),
    )(df, w, gu)


def _loss_fwd_bwd(y, target):
    D = y.shape[1]

    def body(i, nt, y, t):
        err = y - t
        part = 0.5 * jnp.sum(jnp.mean(err * err, axis=-1, keepdims=True), axis=0, keepdims=True)
        return err * (1.0 / D), jnp.broadcast_to(part, (1, LANES))
    dy, loss = _rows("loss", body, 256, [Row(y), Row(target)], [("row", D, F32), ("acc", (1, LANES), F32)])
    return loss[0, 0], dy


def _adamw(w, g, m, v, name):
    C = w.shape[1]
    bc1 = 1.0 - ADAM_B1 ** ADAM_STEP
    bc2 = 1.0 - ADAM_B2 ** ADAM_STEP

    def body(i, nt, w, g, m, v):
        m = ADAM_B1 * m + (1.0 - ADAM_B1) * g
        v = ADAM_B2 * v + (1.0 - ADAM_B2) * (g * g)
        delta = -ADAM_LR * ((m / bc1) / (jnp.sqrt(v / bc2) + ADAM_EPS) + ADAM_WD * w)
        return delta, m, v
    return _rows(name, body, 256, [Row(w), Row(g), Row(m), Row(v)], [("row", C, F32)] * 3)


def _expm1(x):
    return jnp.tanh(0.5 * x) * (jnp.exp(x) + 1.0)


def _lru_gates(xh, i, nt, cw, cb, wa, wx, ba, bx, lam):
    xc = cb
    for j in range(4):
        xc = xc + cw[j:j + 1] * _shift(xh, j - 2, i, nt)
    xcb = xc.astype(BF16)
    gates = []
    for e in range(2):
        r = _sigmoid(jnp.dot(xcb, wa[e], preferred_element_type=F32) + ba[e:e + 1])
        ig = _sigmoid(jnp.dot(xcb, wx[e], preferred_element_type=F32) + bx[e:e + 1])
        sp = _softplus(-lam[e:e + 1])
        la = -LRU_C * r * sp
        gates.append((r, ig, sp, jnp.exp(la), jnp.sqrt(-_expm1(2.0 * la))))
    return xc, xcb, gates


def _scan2(af, uf, ab, ub, adjoint, name):
    L, W = af.shape
    tm = _tile(L, 512, 8)
    nt, nb = L // tm, tm // 8

    def blk(A, U, h, reverse, row):
        for d in (1, 2, 4):
            if reverse:
                ok, sh = row < 8 - d, 8 - d
            else:
                ok, sh = row >= d, d
            As = jnp.where(ok, pltpu.roll(A, sh, axis=0), 1.0)
            Us = jnp.where(ok, pltpu.roll(U, sh, axis=0), 0.0)
            U = A * Us + U
            A = A * As
        return A * h + U

    def kern(af_ref, uf_ref, ab_ref, ub_ref, of_ref, ob_ref, c_ref):
        @pl.when(pl.program_id(0) == 0)
        def _():
            c_ref[...] = jnp.zeros_like(c_ref)

        row = lax.broadcasted_iota(jnp.int32, (8, W), 0)
        full = lambda v: jnp.broadcast_to(v, (8, W))

        def body(j, carry):
            hF, aF, hB, aB = carry
            r0 = pl.multiple_of(j * 8, 8)
            r1 = pl.multiple_of((nb - 1 - j) * 8, 8)
            A, U = af_ref[pl.ds(r0, 8), :], uf_ref[pl.ds(r0, 8), :]
            if adjoint:
                C = jnp.where(row == 0, aF, pltpu.roll(A, 1, axis=0))
                aF = full(A[7:8])
            else:
                C = A
            H = blk(C, U, hF, False, row)
            of_ref[pl.ds(r0, 8), :] = H
            hF = full(H[7:8])
            A, U = ab_ref[pl.ds(r1, 8), :], ub_ref[pl.ds(r1, 8), :]
            if adjoint:
                C = jnp.where(row == 7, aB, pltpu.roll(A, 7, axis=0))
                aB = full(A[0:1])
            else:
                C = A
            H = blk(C, U, hB, True, row)
            ob_ref[pl.ds(r1, 8), :] = H
            hB = full(H[0:1])
            return hF, aF, hB, aB

        carry = lax.fori_loop(0, nb, body, (c_ref[0], c_ref[1], c_ref[2], c_ref[3]))
        for n in range(4):
            c_ref[n] = carry[n]

    fwd = pl.BlockSpec((tm, W), lambda i: (i, 0))
    bwd = pl.BlockSpec((tm, W), lambda i: (nt - 1 - i, 0))
    return pl.pallas_call(
        kern, name=name, grid=(nt,), in_specs=[fwd, fwd, bwd, bwd], out_specs=[fwd, bwd],
        out_shape=[jax.ShapeDtypeStruct((L, W), F32)] * 2,
        scratch_shapes=[pltpu.VMEM((4, 8, W), F32)], compiler_params=_ARB(1),
    )(af, uf, ab, ub)


def _block_diag(w):
    rows = jnp.tile(w.reshape(2, GROUP_W, HEAD_DIM), (1, 1, 4))
    return jnp.where(_block_ones(GROUP_W, HEAD_DIM) > 0.5, rows, 0.0).astype(BF16)


def _diag_blocks(w):
    return jnp.stack([w[:, h * 64:(h + 1) * 64, h * 64:(h + 1) * 64] for h in range(4)], axis=1)


def _lru_params(W, l):
    return [Full(W["lru_conv_w"][l]), Full(W["lru_conv_b"][l][None]), Full(_block_diag(W["lru_w_a"][l])),
            Full(_block_diag(W["lru_w_x"][l])), Full(W["lru_b_a"][l]), Full(W["lru_b_x"][l]), Full(W["lru_lambda"][l])]


def _lru_fwd(p, W, l):
    def pre(i, nt, xh, *prm):
        xc, _, g = _lru_gates(xh, i, nt, *prm)
        return g[0][3], g[0][4] * (g[0][1] * xc), g[1][3], g[1][4] * (g[1][1] * xc)

    a0, u0, a1, u1 = _rows("lru_pre", pre, 256, [Row(p, GROUP_W, 7, halo=True)] + _lru_params(W, l),
                           [("row", GROUP_W, F32)] * 4)
    hf, hb = _scan2(a0, u0, a1, u1, False, "lru_scan")
    yc = _rows("lru_post", lambda i, nt, hf, hb, gc: (hf + hb) * _gelu(gc), 512,
               [Row(hf), Row(hb), Row(p, GROUP_W, 8)], [("row", GROUP_W, BF16)])[0]
    return yc, (a0, a1, hf, hb)


def _lru_bwd(dy, dy_cb, p, W, l, saved):
    a0, a1, hf, hb = saved

    def post(i, nt, dy, hf, hb, gc):
        return dy * _gelu(gc), dy * (hf + hb) * _dgelu(gc)

    dh, dgc = _rows("lru_post_bwd", post, 512, [Row(dy, GROUP_W, dy_cb), Row(hf), Row(hb), Row(p, GROUP_W, 8)],
                    [("row", GROUP_W, F32), ("row", GROUP_W, BF16)])
    gb, gf = _scan2(a1, dh, a0, dh, True, "lru_scan_adj")

    def gates_bwd(i, nt, xh, gf, gb, hfh, hbh, cw, cb, wa, wx, ba, bx, lam):
        xc, xcb, g = _lru_gates(xh, i, nt, cw, cb, wa, wx, ba, bx, lam)
        dxc = jnp.zeros_like(xc)
        dwa, dwx, dba, dbx, dlam = [], [], [], [], []
        for e, du, hprev in ((0, gf, _shift(hfh, -1, i, nt)), (1, gb, _shift(hbh, 1, i, nt))):
            r, ig, sp, a, s = g[e]
            dxc = dxc + du * s * ig
            dla = du * hprev * a - (du * ig * xc) * a * a / s
            dza = (dla * (-LRU_C) * sp) * r * (1.0 - r)
            dzx = (du * s * xc) * ig * (1.0 - ig)
            dlam.append(_colsum(dla * r) * (LRU_C * _sigmoid(-lam[e:e + 1])))
            dba.append(_colsum(dza))
            dbx.append(_colsum(dzx))
            dzab, dzxb = dza.astype(BF16), dzx.astype(BF16)
            tn = (((0,), (0,)), ((), ()))
            nt_ = (((1,), (1,)), ((), ()))
            dwa.append(lax.dot_general(xcb, dzab, tn, preferred_element_type=F32))
            dwx.append(lax.dot_general(xcb, dzxb, tn, preferred_element_type=F32))
            dxc = dxc + lax.dot_general(dzab, wa[e], nt_, preferred_element_type=F32)
            dxc = dxc + lax.dot_general(dzxb, wx[e], nt_, preferred_element_type=F32)
        cat = lambda v: jnp.concatenate(v, axis=0)
        return dxc, jnp.stack(dwa), jnp.stack(dwx), cat(dba), cat(dbx), cat(dlam)

    dxc, dwa, dwx, dba, dbx, dlam = _rows(
        "lru_gates_bwd", gates_bwd, 256,
        [Row(p, GROUP_W, 7, halo=True), Row(gf), Row(gb), Row(hf, halo=True), Row(hb, halo=True)] + _lru_params(W, l),
        [("row", GROUP_W, F32), ("acc", (2, GROUP_W, GROUP_W), F32), ("acc", (2, GROUP_W, GROUP_W), F32),
         ("acc", (2, GROUP_W), F32), ("acc", (2, GROUP_W), F32), ("acc", (2, GROUP_W), F32)])

    def conv_bwd(i, nt, dh_, xh, cw):
        dxb = jnp.zeros_like(dh_[1])
        dcw = []
        for j in range(4):
            dxb = dxb + cw[j:j + 1] * _shift(dh_, 2 - j, i, nt)
            dcw.append(_colsum(dh_[1] * _shift(xh, j - 2, i, nt)))
        return dxb, jnp.concatenate(dcw, axis=0), _colsum(dh_[1])

    dxb, dcw, dcb = _rows("lru_conv_bwd", conv_bwd, 512,
                          [Row(dxc, halo=True), Row(p, GROUP_W, 7, halo=True), Full(W["lru_conv_w"][l])],
                          [("row", GROUP_W, BF16), ("acc", (4, GROUP_W), F32), ("acc", (1, GROUP_W), F32)])
    grads = dict(lru_conv_w=dcw, lru_conv_b=dcb[0], lru_w_a=_diag_blocks(dwa), lru_w_x=_diag_blocks(dwx),
                 lru_b_a=dba, lru_b_x=dbx, lru_lambda=dlam)
    return dxb, dgc, grads


_NT = (((1,), (1,)), ((), ()))
_TN = (((0,), (0,)), ((), ()))


def _dot(a, b, dims=None):
    if dims is None:
        return jnp.dot(a, b, preferred_element_type=F32)
    return lax.dot_general(a, b, dims, preferred_element_type=F32)


def _dot_exact(a, b):
    return jnp.dot(a, b, precision=lax.Precision.HIGHEST, preferred_element_type=F32)


def _gla_gate_w(w_gate, b_gate):
    zero = jnp.zeros((GLA_RANK, GROUP_W), w_gate.dtype)
    wg = jnp.concatenate([jnp.concatenate([w_gate[0], zero], axis=1), jnp.concatenate([zero, w_gate[1]], axis=1),
                          jnp.zeros((LANES - 2 * GLA_RANK, 2 * GROUP_W), w_gate.dtype)], axis=0)
    return wg.astype(BF16), b_gate.reshape(1, 2 * GROUP_W)


def _gla_gates_fwd(p, wg, bg):
    def body(i, nt, z, wg, bg):
        logit = _dot(z.astype(BF16), wg) + bg
        la = -_softplus(-logit) * (1.0 / GLA_TAU)
        return la[:, :GROUP_W], la[:, GROUP_W:]
    return _rows("gla_gates", body, 512, [Row(p, LANES, Z_BLOCK), Full(wg), Full(bg)], [("row", GROUP_W, F32)] * 2)


def _gla_gates_bwd(p, dla0, dla1, wg, bg):
    def body(i, nt, z, d0, d1, wg, bg):
        zb = z.astype(BF16)
        logit = _dot(zb, wg) + bg
        dlogit = jnp.concatenate([d0, d1], axis=1) * (1.0 / GLA_TAU) * _sigmoid(-logit)
        dlb = dlogit.astype(BF16)
        return _dot(dlb, wg, _NT), _dot(zb, dlb, _TN), _colsum(dlogit)
    return _rows("gla_gates_bwd", body, 512, [Row(p, LANES, Z_BLOCK), Row(dla0), Row(dla1), Full(wg), Full(bg)],
                 [("row", LANES, BF16), ("acc", (LANES, 2 * GROUP_W), F32), ("acc", (1, 2 * GROUP_W), F32)])


def _gla_order(reverse):
    t = np.arange(GLA_CHUNK)
    m = (t[None, :] >= t[:, None]) if reverse else (t[None, :] <= t[:, None])
    return m.astype(np.float32), (32, 0) if reverse else (31, 63)


def _stack_heads(x, bd):
    return jnp.where(bd, jnp.concatenate([x] * 4, axis=0), 0.0)


def _diag_heads(r, bd):
    r = jnp.where(bd, r, 0.0)
    return r[0:64] + r[64:128] + r[128:192] + r[192:256]


def _gla_factors(q_ref, k_ref, rows, b, mid, last):
    bm, bl = b[mid:mid + 1], b[last:last + 1]
    qs = q_ref[rows, :] * ATT_SCALE
    k = k_ref[rows, :]
    P, N, E, Fd = jnp.exp(b - bm), jnp.exp(bm - b), jnp.exp(b), jnp.exp(bl - b)
    return (P, N, E, Fd, jnp.exp(bl)), (qs * P, k * N, qs * E, k * Fd)


def _gla_specs(L, walk_up):
    tm = _tile(L, 512, GLA_CHUNK)
    nt, nc = L // tm, tm // GLA_CHUNK
    specs = []
    for up in walk_up:
        t = (lambda i: i) if up else (lambda i: nt - 1 - i)
        specs.append(dict(
            col=lambda cb, _t=t: pl.BlockSpec((tm, GROUP_W), lambda i: (_t(i), cb)),
            row=pl.BlockSpec((tm, GROUP_W), lambda i, _t=t: (_t(i), 0)),
            state=pl.BlockSpec((nc, GROUP_W, GROUP_W), lambda i, _t=t: (_t(i), 0, 0))))
    return nt, nc, specs


def _gla_chunk_fwd(p, la0, la1, ride=None):
    L = la0.shape[0]
    nt, nc, specs = _gla_specs(L, (True, False))
    orders = [_gla_order(False), _gla_order(True)]

    def kern(q0, k0, v0, l0, q1, k1, v1, l1, m0_ref, m1_ref, bd_ref, o0, s0, o1, s1, st_ref):
        @pl.when(pl.program_id(0) == 0)
        def _():
            st_ref[...] = jnp.zeros_like(st_ref)

        bd = bd_ref[...] > 0.5
        dirs = []
        for e, (q_ref, k_ref, v_ref, la_ref, m_ref, o_ref, s_ref) in enumerate(
                ((q0, k0, v0, l0, m0_ref, o0, s0), (q1, k1, v1, l1, m1_ref, o1, s1))):
            mv = m_ref[...]
            dirs.append((q_ref, k_ref, v_ref, la_ref, mv, jnp.concatenate([mv] * 4, axis=0) > 0.5, o_ref, s_ref))

        def body(cc, carry):
            E = range(2)
            cs = [nc - 1 - cc if e else cc for e in E]
            rows = [pl.ds(pl.multiple_of(c * GLA_CHUNK, GLA_CHUNK), GLA_CHUNK) for c in cs]
            b = [_dot_exact(dirs[e][4], dirs[e][3][rows[e], :]) for e in E]
            t = [_gla_factors(dirs[e][0], dirs[e][1], rows[e], b[e], *orders[e][1]) for e in E]
            vb = [dirs[e][2][rows[e], :].astype(BF16) for e in E]
            st = [st_ref[e] for e in E]
            a = [_dot(_stack_heads(t[e][1][0], bd).astype(BF16), t[e][1][1].astype(BF16), _NT) for e in E]
            inter = [_dot(t[e][1][2].astype(BF16), st[e].astype(BF16), _NT) for e in E]
            kv = [_dot(vb[e], t[e][1][3].astype(BF16), _TN) for e in E]
            a = [jnp.where(dirs[e][5], a[e], 0.0).astype(BF16) for e in E]
            r = [_dot(a[e], vb[e]) for e in E]
            for e in E:
                dirs[e][7][cs[e]] = st[e]
                dirs[e][6][rows[e], :] = _diag_heads(r[e], bd) + inter[e]
                st_ref[e] = st[e] * t[e][0][4] + jnp.where(bd, kv[e], 0.0)
            return carry

        lax.fori_loop(0, nc, body, 0)

    const = lambda shp: pl.BlockSpec(shp, lambda i: (0, 0))
    in_specs, out_specs = [], []
    for sp in specs:
        in_specs += [sp["col"](0), sp["col"](1), sp["col"](2), sp["row"]]
        out_specs += [sp["row"], sp["state"]]
    return _call(
        kern, (p, p, p, la0, p, p, p, la1, jnp.asarray(orders[0][0]), jnp.asarray(orders[1][0]),
               _block_ones(GROUP_W, HEAD_DIM)),
        ride, lambda: (pl.program_id(0) == 0, pl.program_id(0) == nt - 1), name="gla_fwd", grid=(nt,),
        in_specs=in_specs + [const((GLA_CHUNK, GLA_CHUNK))] * 2 + [const((GROUP_W, GROUP_W))], out_specs=out_specs,
        out_shape=[jax.ShapeDtypeStruct((L, GROUP_W), F32),
                   jax.ShapeDtypeStruct((L // GLA_CHUNK, GROUP_W, GROUP_W), F32)] * 2,
        scratch_shapes=[pltpu.VMEM((2, GROUP_W, GROUP_W), F32)])


def _gla_chunk_bwd(p, la0, la1, do, sprev0, sprev1, ride=None):
    L = la0.shape[0]
    nt, nc, specs = _gla_specs(L, (False, True))
    orders = [_gla_order(False), _gla_order(True)]

    def kern(q0, k0, v0, l0, do0, s0, q1, k1, v1, l1, do1, s1, m0_ref, m1_ref, t0_ref, t1_ref, bd_ref, *rest):
        outs, dst_ref = (rest[0:4], rest[4:8]), rest[8]

        @pl.when(pl.program_id(0) == 0)
        def _():
            dst_ref[...] = jnp.zeros_like(dst_ref)

        bd = bd_ref[...] > 0.5
        row = lax.broadcasted_iota(jnp.int32, (GLA_CHUNK, GROUP_W), 0)
        dirs = []
        for ins, m_ref, t_ref in (((q0, k0, v0, l0, do0, s0), m0_ref, t0_ref), ((q1, k1, v1, l1, do1, s1), m1_ref, t1_ref)):
            mv = m_ref[...]
            dirs.append(ins + (mv, t_ref[...], jnp.concatenate([mv] * 4, axis=0) > 0.5))

        def body(cc, carry):
            E2 = range(2)
            cs = [cc if e else nc - 1 - cc for e in E2]
            rows = [pl.ds(pl.multiple_of(c * GLA_CHUNK, GLA_CHUNK), GLA_CHUNK) for c in cs]
            b = [_dot_exact(dirs[e][6], dirs[e][3][rows[e], :]) for e in E2]
            t = [_gla_factors(dirs[e][0], dirs[e][1], rows[e], b[e], *orders[e][1]) for e in E2]
            vb = [dirs[e][2][rows[e], :].astype(BF16) for e in E2]
            dov = [dirs[e][4][rows[e], :] for e in E2]
            dob = [x.astype(BF16) for x in dov]
            st = [dirs[e][5][cs[e]] for e in E2]
            dst = [dst_ref[e] for e in E2]
            stb, dstb = [x.astype(BF16) for x in st], [x.astype(BF16) for x in dst]
            qst = [_stack_heads(t[e][1][0], bd).astype(BF16) for e in E2]
            dost = [_stack_heads(dov[e], bd).astype(BF16) for e in E2]
            kNb, qEb, kFb = ([t[e][1][n].astype(BF16) for e in E2] for n in (1, 2, 3))
            a = [_dot(qst[e], kNb[e], _NT) for e in E2]
            da = [_dot(dost[e], vb[e], _NT) for e in E2]
            dqE = [_dot(dob[e], stb[e]) for e in E2]
            dkF = [_dot(vb[e], dstb[e]) for e in E2]
            dv_inter = [_dot(kFb[e], dstb[e], _NT) for e in E2]
            dst_in = [_dot(dob[e], qEb[e], _TN) for e in E2]
            a = [jnp.where(dirs[e][8], a[e], 0.0).astype(BF16) for e in E2]
            da = [jnp.where(dirs[e][8], da[e], 0.0).astype(BF16) for e in E2]
            dv_intra = [_dot(a[e], dost[e], _TN) for e in E2]
            dqP = [_dot(da[e], kNb[e]) for e in E2]
            dkN = [_dot(da[e], qst[e], _TN) for e in E2]
            db = []
            for e in E2:
                (P, N, Ef, Fd, d), (qP, kN, qE, kF) = t[e]
                mid, last = orders[e][1]
                dq_ref, dk_ref, dv_ref, _ = outs[e]
                dqp = _diag_heads(dqP[e], bd)
                dd = _colsum(dst[e] * st[e])
                dst_ref[e] = jnp.where(bd, dst_in[e], 0.0) + dst[e] * d
                tP, tN, tE, tF = dqp * qP, dkN[e] * kN, dqE[e] * qE, dkF[e] * kF
                db.append(tP - tN + tE - tF + jnp.where(row == mid, _colsum(tN - tP), 0.0)
                          + jnp.where(row == last, _colsum(tF) + dd * d, 0.0))
                dq_ref[rows[e], :] = (dqp * P + dqE[e] * Ef) * ATT_SCALE
                dk_ref[rows[e], :] = dkN[e] * N + dkF[e] * Fd
                dv_ref[rows[e], :] = dv_intra[e] + dv_inter[e]
            dla = [_dot_exact(dirs[e][7], db[e]) for e in E2]
            for e in E2:
                outs[e][3][rows[e], :] = dla[e]
            return carry

        lax.fori_loop(0, nc, body, 0)

    const = lambda shp: pl.BlockSpec(shp, lambda i: (0, 0))
    in_specs, out_specs = [], []
    for sp in specs:
        in_specs += [sp["col"](0), sp["col"](1), sp["col"](2), sp["row"], sp["row"], sp["state"]]
        out_specs += [sp["row"]] * 4
    m0, m1 = orders[0][0], orders[1][0]
    return _call(
        kern, (p, p, p, la0, do, sprev0, p, p, p, la1, do, sprev1, jnp.asarray(m0), jnp.asarray(m1),
               jnp.asarray(m0.T.copy()), jnp.asarray(m1.T.copy()), _block_ones(GROUP_W, HEAD_DIM)),
        ride, lambda: (pl.program_id(0) == 0, pl.program_id(0) == nt - 1), name="gla_bwd", grid=(nt,),
        in_specs=in_specs + [const((GLA_CHUNK, GLA_CHUNK))] * 4 + [const((GROUP_W, GROUP_W))], out_specs=out_specs,
        out_shape=[jax.ShapeDtypeStruct((L, GROUP_W), F32)] * 8,
        scratch_shapes=[pltpu.VMEM((2, GROUP_W, GROUP_W), F32)])


def _gla_fwd(p, W, l, ride=None):
    wg, bg = _gla_gate_w(W["gla_w_gate"][l], W["gla_b_gate"][l])
    la0, la1 = _gla_gates_fwd(p, wg, bg)
    (of, s0, ob, s1), got = _gla_chunk_fwd(p, la0, la1, ride)

    def post(i, nt, of, ob, g, ng, bd):
        o = of + ob
        r = lax.rsqrt(_head_sum(o * o, bd) * (1.0 / HEAD_DIM) + EPS)
        return o * r * ng * _silu(g)

    ya = _rows("gla_post", post, 512, [Row(of), Row(ob), Row(p, GROUP_W, 3), Full(W["gla_norm"][l][None]),
                                       Full(_block_ones(GROUP_W, HEAD_DIM))], [("row", GROUP_W, BF16)])[0]
    return ya, (la0, la1, of, ob, s0, s1), got


def _gla_bwd(dy, dy_cb, p, W, l, saved, ride=None):
    la0, la1, of, ob, s0, s1 = saved
    wg, bg = _gla_gate_w(W["gla_w_gate"][l], W["gla_b_gate"][l])

    def post(i, nt, dy, of, ob, g, ng, bd):
        o = of + ob
        r = lax.rsqrt(_head_sum(o * o, bd) * (1.0 / HEAD_DIM) + EPS)
        oh = o * r
        don = dy * _silu(g)
        doh = don * ng
        do = r * (doh - oh * _head_sum(doh * oh, bd) * (1.0 / HEAD_DIM))
        return do, dy * (oh * ng) * _dsilu(g), _colsum(don * oh)

    do, dg, dng = _rows("gla_post_bwd", post, 512,
                        [Row(dy, GROUP_W, dy_cb), Row(of), Row(ob), Row(p, GROUP_W, 3), Full(W["gla_norm"][l][None]),
                         Full(_block_ones(GROUP_W, HEAD_DIM))],
                        [("row", GROUP_W, F32), ("row", GROUP_W, BF16), ("acc", (1, GROUP_W), F32)])
    (dq0, dk0, dv0, dla0, dq1, dk1, dv1, dla1), got = _gla_chunk_bwd(p, la0, la1, do, s0, s1, ride)
    dq, dk, dv = _rows("gla_sum_bwd", lambda i, nt, a0, a1, b0, b1, c0, c1: (a0 + a1, b0 + b1, c0 + c1), 512,
                       [Row(t) for t in (dq0, dq1, dk0, dk1, dv0, dv1)], [("row", GROUP_W, BF16)] * 3)
    dz, dwg, dbg = _gla_gates_bwd(p, dla0, dla1, wg, bg)
    dw_gate = jnp.stack([dwg[e * GLA_RANK:(e + 1) * GLA_RANK, e * GROUP_W:(e + 1) * GROUP_W] for e in range(2)])
    grads = dict(gla_w_gate=dw_gate, gla_b_gate=dbg.reshape(2, GROUP_W), gla_norm=dng[0])
    return (dq, dk, dv, dg, dz), grads, got


def _rope_tables(L):
    pos = jnp.arange(L, dtype=F32)
    inv_freq = ROPE_THETA ** (-jnp.arange(0, HEAD_DIM, 2, dtype=F32) / HEAD_DIM)
    ang = pos[:, None] * inv_freq[None, :]
    cos, sin = jnp.cos(ang), jnp.sin(ang)
    return jnp.tile(jnp.concatenate([cos, cos], axis=1), (1, 4)), jnp.tile(jnp.concatenate([-sin, sin], axis=1), (1, 4))


def _swap_halves(t):
    lane = lax.broadcasted_iota(jnp.int32, t.shape, 1)
    first = (lane & (HEAD_DIM - 1)) < HEAD_DIM // 2
    return jnp.where(first, pltpu.roll(t, GROUP_W - HEAD_DIM // 2, axis=1), pltpu.roll(t, HEAD_DIM // 2, axis=1))


def _attn_prep(p, cosf, sinf):
    dils = [dil for _, dil in DIL_PAIRS]

    def body(i, nt, qb, kb, vb, qd, kd, vd, c, s):
        d = (qd * c + _swap_halves(qd) * s, kd * c + _swap_halves(kd) * s, vd)
        return (qb, kb, vb) + d * len(dils)
    ins = [Row(p, GROUP_W, cb) for cb in (4, 5, 6, 9, 10, 11)] + [Row(cosf), Row(sinf)]
    outs = [("row", GROUP_W, BF16)] * 3 + [("row", GROUP_W, BF16, dil) for dil in dils for _ in range(3)]
    res = _rows("attn_prep", body, 512, ins, outs)
    return tuple(res[:3]), {dil: tuple(res[3 + 3 * n:6 + 3 * n]) for n, dil in enumerate(dils)}


def _na_onehot():
    c = np.arange(GRID_W)
    dc = np.clip(c[None, :] - c[:, None], -(NA_COLS - 1), NA_COLS - 1) + NA_COLS - 1
    oh = np.zeros((LANES, GRID_W * GRID_W), np.float32)
    oh[dc.reshape(-1), np.arange(GRID_W * GRID_W)] = 1.0
    return jnp.asarray(oh)


def _na_colmask():
    c = np.arange(GRID_W)
    start = np.clip(c - NA_COLS // 2, 0, GRID_W - NA_COLS)
    ok = (c[None, :] >= start[:, None]) & (c[None, :] < start[:, None] + NA_COLS)
    return jnp.asarray(np.where(ok, 0.0, NEG).astype(np.float32))


N_DR = 2 * NA_ROWS - 1


NA_HALF = GRID_W // 2
NA_KCOLS = 48
NA_WIN = NA_ROWS * NA_KCOLS
NA_ROWS_PER_STEP = 4
NA_BWD_ROWS_PER_STEP = 4


def _na_bias(rpb):
    rp = jnp.pad(rpb.reshape(4 * N_DR, 2 * NA_COLS - 1), ((0, GRID_W - 4 * N_DR), (0, LANES - 2 * NA_COLS + 1)))

    def expand(r_ref, oh_ref, o_ref):
        o_ref[...] = _dot_exact(r_ref[...], oh_ref[...])

    r = pl.pallas_call(expand, name="na_bias_expand",
                       out_shape=jax.ShapeDtypeStruct((GRID_W, GRID_W * GRID_W), F32))(rp, _na_onehot())
    r = r[:4 * N_DR].reshape(4, N_DR, GRID_W, GRID_W)

    def build(r_ref, m_ref, o_ref):
        for h in range(4):
            for c in range(NA_ROWS):
                for half in range(2):
                    q0, k0 = NA_HALF * half, 16 * half
                    for i in range(NA_ROWS):
                        o_ref[h, c, half, :, i * NA_KCOLS:(i + 1) * NA_KCOLS] = (
                            r_ref[h, i - c + NA_ROWS - 1, q0:q0 + NA_HALF, k0:k0 + NA_KCOLS]
                            + m_ref[q0:q0 + NA_HALF, k0:k0 + NA_KCOLS])

    return pl.pallas_call(build, name="na_bias_build",
                          out_shape=jax.ShapeDtypeStruct((4, NA_ROWS, 2, NA_HALF, NA_WIN), F32))(r, _na_colmask())


def _na_bias_bwd(dbias):
    def fold(d_ref, o_ref):
        o_ref[...] = jnp.zeros_like(o_ref)
        for h in range(4):
            for a in range(N_DR):
                for half in range(2):
                    q0, k0 = NA_HALF * half, 16 * half
                    acc = jnp.zeros((NA_HALF, NA_KCOLS), F32)
                    for c in range(NA_ROWS):
                        i = a + c - (NA_ROWS - 1)
                        if 0 <= i < NA_ROWS:
                            acc = acc + d_ref[h, c, half, :, i * NA_KCOLS:(i + 1) * NA_KCOLS]
                    o_ref[h, a, q0:q0 + NA_HALF, k0:k0 + NA_KCOLS] = acc

    dr = pl.pallas_call(fold, name="na_bias_fold",
                        out_shape=jax.ShapeDtypeStruct((4, N_DR, GRID_W, GRID_W), F32))(dbias)
    dr = jnp.pad(dr.reshape(4 * N_DR, GRID_W * GRID_W), ((0, GRID_W - 4 * N_DR), (0, 0)))

    def contract(d_ref, oh_ref, o_ref):
        o_ref[...] = lax.dot_general(d_ref[...], oh_ref[...], _NT, precision=lax.Precision.HIGHEST,
                                     preferred_element_type=F32)

    g = pl.pallas_call(contract, name="na_bias_contract",
                       out_shape=jax.ShapeDtypeStruct((GRID_W, LANES), F32))(dr, _na_onehot())
    return g[:4 * N_DR, :2 * NA_COLS - 1].reshape(4, N_DR, 2 * NA_COLS - 1)


def _na_window(r, n_rows):
    rs = jnp.clip(r - NA_ROWS // 2, 0, n_rows - NA_ROWS)
    return rs, r - rs


def _na_key_rows(rs, half, t):
    return pl.ds(pl.multiple_of((rs + t) * GRID_W + 16 * half, 16), NA_KCOLS)


def _na_keys(ref, rs, half):
    return jnp.concatenate([ref[_na_key_rows(rs, half, t), :] for t in range(NA_ROWS)], axis=0)


def _na_stack(x, first):
    zero = jnp.zeros_like(x)
    return jnp.concatenate([jnp.where(first, x, zero), jnp.where(first, zero, x)], axis=0)


def _na_bias_spec():
    return pl.BlockSpec((2, NA_ROWS, 2, NA_HALF, NA_WIN), lambda j, i: (j, 0, 0, 0, 0))


def _grid_edges(n0, n1):
    j, i = pl.program_id(0), pl.program_id(1)
    return jnp.logical_and(j == 0, i == 0), jnp.logical_and(j == n0 - 1, i == n1 - 1)


def _na_fwd(q, k, v, bias, ride=None):
    L = q.shape[0]
    n_rows = L // GRID_W
    tm = _tile(L, 512, GRID_W)
    nt, nr = L // tm, tm // GRID_W

    def kern(q_ref, k_ref, v_ref, b_ref, o_ref):
        i = pl.program_id(1)
        first = lax.broadcasted_iota(jnp.int32, (NA_HALF, LANES), 1) < HEAD_DIM

        def body(it, carry):
            parts = []
            for u in range(NA_ROWS_PER_STEP):
                rr = it * NA_ROWS_PER_STEP + u
                rs, c = _na_window(i * nr + rr, n_rows)
                for half in range(2):
                    rows = pl.ds(pl.multiple_of(rr * GRID_W + NA_HALF * half, NA_HALF), NA_HALF)
                    bias = jnp.concatenate([b_ref[0, c, half], b_ref[1, c, half]], axis=0)
                    parts.append((rows, _na_stack(q_ref[rows, :], first), bias, _na_keys(k_ref, rs, half),
                                  _na_keys(v_ref, rs, half)))
            s = [_dot(qs, kw, _NT) * ATT_SCALE + bias for _, qs, bias, kw, _ in parts]
            e = [jnp.exp(x - jnp.max(x, axis=-1, keepdims=True)) for x in s]
            pn = [(x / jnp.sum(x, axis=-1, keepdims=True)).astype(BF16) for x in e]
            o = [_dot(p, part[4]) for p, part in zip(pn, parts)]
            for x, (rows, *_) in zip(o, parts):
                o_ref[rows, :] = jnp.where(first, x[:NA_HALF], x[NA_HALF:]).astype(BF16)
            return carry

        lax.fori_loop(0, nr // NA_ROWS_PER_STEP, body, 0)

    qspec = pl.BlockSpec((tm, LANES), lambda j, i: (i, j))
    kvspec = pl.BlockSpec((L, LANES), lambda j, i: (0, j))
    (y,), got = _call(
        kern, (q, k, v, bias), ride, lambda: _grid_edges(2, nt), name="na_fwd", grid=(2, nt),
        in_specs=[qspec, kvspec, kvspec, _na_bias_spec()],
        out_specs=[qspec], out_shape=[jax.ShapeDtypeStruct((L, GROUP_W), BF16)], scratch_shapes=[])
    return y, got


def _na_bwd(dy, dy_block, q, k, v, bias, ride=None):
    L = q.shape[0]
    n_rows = L // GRID_W
    tm = _tile(L, 512, GRID_W)
    nt, nr = L // tm, tm // GRID_W

    def kern(dy_ref, q_ref, k_ref, v_ref, b_ref, dq_ref, dk_ref, dv_ref, db_ref):
        i = pl.program_id(1)

        @pl.when(i == 0)
        def _():
            dk_ref[...] = jnp.zeros_like(dk_ref)
            dv_ref[...] = jnp.zeros_like(dv_ref)
            db_ref[...] = jnp.zeros_like(db_ref)

        first = lax.broadcasted_iota(jnp.int32, (NA_HALF, LANES), 1) < HEAD_DIM

        def body(it, carry):
            parts = []
            for u in range(NA_BWD_ROWS_PER_STEP):
                rr = it * NA_BWD_ROWS_PER_STEP + u
                rs, c = _na_window(i * nr + rr, n_rows)
                for half in range(2):
                    rows = pl.ds(pl.multiple_of(rr * GRID_W + NA_HALF * half, NA_HALF), NA_HALF)
                    bias = jnp.concatenate([b_ref[0, c, half], b_ref[1, c, half]], axis=0)
                    parts.append((rows, half, _na_stack(q_ref[rows, :], first),
                                  _na_stack(dy_ref[rows, :].astype(BF16), first), bias, _na_keys(k_ref, rs, half),
                                  _na_keys(v_ref, rs, half), rs, c))
            s = [_dot(part[2], part[5], _NT) * ATT_SCALE + part[4] for part in parts]
            dp = [_dot(part[3], part[6], _NT) for part in parts]
            e = [jnp.exp(x - jnp.max(x, axis=-1, keepdims=True)) for x in s]
            pn = [x / jnp.sum(x, axis=-1, keepdims=True) for x in e]
            ds = [p * (d - jnp.sum(p * d, axis=-1, keepdims=True)) for p, d in zip(pn, dp)]
            dsb = [x.astype(BF16) for x in ds]
            pnb = [x.astype(BF16) for x in pn]
            dq = [_dot(x, part[5]) for x, part in zip(dsb, parts)]
            dk = [_dot(x, part[2], _TN) for x, part in zip(dsb, parts)]
            dv = [_dot(x, part[3], _TN) for x, part in zip(pnb, parts)]
            for n, (rows, half, _, _, _, _, _, rs, c) in enumerate(parts):
                db_ref[0, c, half] += ds[n][:NA_HALF]
                db_ref[1, c, half] += ds[n][NA_HALF:]
                dq_ref[rows, :] = (jnp.where(first, dq[n][:NA_HALF], dq[n][NA_HALF:]) * ATT_SCALE).astype(BF16)
                for t in range(NA_ROWS):
                    kr = _na_key_rows(rs, half, t)
                    dk_ref[kr, :] += dk[n][t * NA_KCOLS:(t + 1) * NA_KCOLS] * ATT_SCALE
                    dv_ref[kr, :] += dv[n][t * NA_KCOLS:(t + 1) * NA_KCOLS]
            return carry

        lax.fori_loop(0, nr // NA_BWD_ROWS_PER_STEP, body, 0)

    qspec = pl.BlockSpec((tm, LANES), lambda j, i: (i, j))
    kvspec = pl.BlockSpec((L, LANES), lambda j, i: (0, j))
    return _call(
        kern, (dy, q, k, v, bias), ride, lambda: _grid_edges(2, nt), name="na_bwd", grid=(2, nt),
        in_specs=[pl.BlockSpec((tm, LANES), lambda j, i: (i, dy_block + j)), qspec, kvspec, kvspec, _na_bias_spec()],
        out_specs=[qspec, kvspec, kvspec, _na_bias_spec()],
        out_shape=[jax.ShapeDtypeStruct((L, GROUP_W), BF16), jax.ShapeDtypeStruct((L, GROUP_W), F32),
                   jax.ShapeDtypeStruct((L, GROUP_W), F32),
                   jax.ShapeDtypeStruct((4, NA_ROWS, 2, NA_HALF, NA_WIN), F32)], scratch_shapes=[])


def _dil_specs(n, tq):
    R = DIL_RADIUS
    step, nb = tq // R, n // R
    main = pl.BlockSpec((tq, LANES), lambda j, i: (i, j))
    prev = pl.BlockSpec((R, LANES), lambda j, i: (jnp.maximum(i * step - 1, 0), j))
    nxt = pl.BlockSpec((R, LANES), lambda j, i: (jnp.minimum((i + 1) * step, nb - 1), j))
    return main, prev, nxt


def _dil_masks():
    R = DIL_RADIUS
    r = np.arange(2 * R)[:, None] & (R - 1)
    c = np.arange(3 * R)[None, :]
    band = np.abs(c - R - r) <= R
    ok = np.stack([band, band & (c >= R), band & (c < 2 * R), band & (c >= R) & (c < 2 * R)])
    return jnp.asarray(np.where(ok, 0.0, NEG).astype(np.float32))


def _dil_mask_spec():
    return pl.BlockSpec((4, 2 * DIL_RADIUS, 3 * DIL_RADIUS), lambda j, i: (0, 0, 0))


def _dil_mask(m_ref, i, sb, n_tiles, n_blocks):
    idx = 0
    if sb == 0:
        idx = idx + jnp.where(i == 0, 1, 0)
    if sb == n_blocks - 1:
        idx = idx + jnp.where(i == n_tiles - 1, 2, 0)
    return m_ref[idx]


def _dil_fwd(q, k, v, dil, ride=None):
    n = q.shape[0]
    tq = _tile(n, DIL_TILE, DIL_RADIUS)

    def kern(q_ref, kp_ref, k_ref, kn_ref, vp_ref, v_ref, vn_ref, m_ref, o_ref, l_ref):
        i = pl.program_id(1)
        R = DIL_RADIUS
        ka = jnp.concatenate([kp_ref[...], k_ref[...], kn_ref[...]], axis=0)
        va = jnp.concatenate([vp_ref[...], v_ref[...], vn_ref[...]], axis=0)
        first = lax.broadcasted_iota(jnp.int32, (R, LANES), 1) < HEAD_DIM
        subs = range(tq // R)
        keys = lambda a, sb: a[sb * R:(sb + 3) * R]
        qs = [_na_stack(q_ref[sb * R:(sb + 1) * R, :], first) for sb in subs]
        s = [_dot(qs[sb], keys(ka, sb), _NT) for sb in subs]
        s = [s[sb] * ATT_SCALE + _dil_mask(m_ref, i, sb, n // tq, len(subs)) for sb in subs]
        m = [jnp.max(x, axis=-1, keepdims=True) for x in s]
        e = [jnp.exp(x - mx) for x, mx in zip(s, m)]
        den = [jnp.sum(x, axis=-1, keepdims=True) for x in e]
        o = [_dot((e[sb] / den[sb]).astype(BF16), keys(va, sb)) for sb in subs]
        for sb in subs:
            lse = m[sb] + jnp.log(den[sb])
            o_ref[sb * R:(sb + 1) * R, :] = jnp.where(first, o[sb][:R], o[sb][R:])
            l_ref[sb * R:(sb + 1) * R, :] = jnp.where(first, lse[:R], lse[R:])

    main, prev, nxt = _dil_specs(n, tq)
    (o, lse), got = _call(
        kern, (q, k, k, k, v, v, v, _dil_masks()), ride,
        lambda: _grid_edges(2 * dil, n // tq), name=f"dil_fwd_{dil}", grid=(2 * dil, n // tq),
        in_specs=[main, prev, main, nxt, prev, main, nxt, _dil_mask_spec()], out_specs=[main, main],
        out_shape=[jax.ShapeDtypeStruct((n, dil * GROUP_W), F32)] * 2, scratch_shapes=[])
    return (o, lse), got


def _dil_bwd(q, k, v, do, lse, dterm, dil, ride=None):
    n = q.shape[0]
    R = DIL_RADIUS
    tq = _tile(n, DIL_TILE, R)
    nq = n // tq

    def kern(q_ref, kp_ref, k_ref, kn_ref, vp_ref, v_ref, vn_ref, do_ref, l_ref, dt_ref, m_ref, dq_ref, dk_ref, dv_ref):
        i = pl.program_id(1)

        @pl.when(i == 0)
        def _():
            dk_ref[...] = jnp.zeros_like(dk_ref)
            dv_ref[...] = jnp.zeros_like(dv_ref)

        ka = jnp.concatenate([kp_ref[...], k_ref[...], kn_ref[...]], axis=0)
        va = jnp.concatenate([vp_ref[...], v_ref[...], vn_ref[...]], axis=0)
        first = lax.broadcasted_iota(jnp.int32, (R, LANES), 1) < HEAD_DIM
        subs = range(tq // R)
        keys = lambda a, sb: a[sb * R:(sb + 3) * R]
        rows = lambda ref, sb: ref[sb * R:(sb + 1) * R, :]
        per_head = lambda t: jnp.concatenate([t[:, 0:1], t[:, HEAD_DIM:HEAD_DIM + 1]], axis=0)
        qs = [_na_stack(rows(q_ref, sb), first) for sb in subs]
        dos = [_na_stack(rows(do_ref, sb), first) for sb in subs]
        s = [_dot(qs[sb], keys(ka, sb), _NT) for sb in subs]
        dp = [_dot(dos[sb], keys(va, sb), _NT) for sb in subs]
        pn = [jnp.exp(s[sb] * ATT_SCALE + _dil_mask(m_ref, i, sb, nq, len(subs)) - per_head(rows(l_ref, sb))) for sb in subs]
        dsb = [(pn[sb] * (dp[sb] - per_head(rows(dt_ref, sb)))).astype(BF16) for sb in subs]
        pnb = [x.astype(BF16) for x in pn]
        dq = [_dot(dsb[sb], keys(ka, sb)) for sb in subs]
        dk = [_dot(dsb[sb], qs[sb], _TN) for sb in subs]
        dv = [_dot(pnb[sb], dos[sb], _TN) for sb in subs]
        zeros = lambda blocks: [jnp.zeros((blocks * R, LANES), F32)] if blocks else []
        pad = lambda t, sb: jnp.concatenate(zeros(sb) + [t] + zeros(len(subs) - 1 - sb), axis=0)
        dka = sum(pad(dk[sb], sb) for sb in subs) * ATT_SCALE
        dva = sum(pad(dv[sb], sb) for sb in subs)
        for sb in subs:
            dq_ref[sb * R:(sb + 1) * R, :] = jnp.where(first, dq[sb][:R], dq[sb][R:]) * ATT_SCALE
        r0 = pl.multiple_of(i * tq, R)
        dk_ref[pl.ds(r0, tq), :] += dka[R:R + tq]
        dv_ref[pl.ds(r0, tq), :] += dva[R:R + tq]

        @pl.when(i > 0)
        def _():
            dk_ref[pl.ds(r0 - R, R), :] += dka[:R]
            dv_ref[pl.ds(r0 - R, R), :] += dva[:R]

        @pl.when(i < nq - 1)
        def _():
            dk_ref[pl.ds(r0 + tq, R), :] += dka[R + tq:]
            dv_ref[pl.ds(r0 + tq, R), :] += dva[R + tq:]

    main, prev, nxt = _dil_specs(n, tq)
    whole = pl.BlockSpec((n, LANES), lambda j, i: (0, j))
    shp = jax.ShapeDtypeStruct((n, dil * GROUP_W), F32)
    (dq, dk, dv), got = _call(
        kern, (q, k, k, k, v, v, v, do, lse, dterm, _dil_masks()), ride,
        lambda: _grid_edges(2 * dil, nq), name=f"dil_bwd_{dil}", grid=(2 * dil, nq),
        in_specs=[main, prev, main, nxt, prev, main, nxt, main, main, main, _dil_mask_spec()],
        out_specs=[main, whole, whole],
        out_shape=[shp] * 3, scratch_shapes=[])
    return (dq, dk, dv), got


def _dil_weights(lses):
    m = jnp.maximum(jnp.maximum(lses[0], lses[1]), lses[2])
    e = [jnp.exp(l - m) for l in lses]
    tot = e[0] + e[1] + e[2]
    return [x / tot for x in e]


def _dilated_fwd(qkv, rides):
    dils = [dil for _, dil in DIL_PAIRS]
    res, got = [], {}
    for dil in dils:
        r, got[f"dil{dil}"] = _dil_fwd(*qkv[dil], dil, rides.get(f"dil{dil}"))
        res.append(r)

    def body(i, nt, o0, o1, o2, l0, l1, l2):
        w = _dil_weights((l0, l1, l2))
        return w[0] * o0 + w[1] * o1 + w[2] * o2

    ins = [Row(r[0], dil=d) for r, d in zip(res, dils)] + [Row(r[1], dil=d) for r, d in zip(res, dils)]
    return _rows("dil_combine", body, 512, ins, [("row", GROUP_W, BF16)])[0], res, got


def _dilated_bwd(dy, dy_cb, qkv, saved, cosf, sinf, rides):
    dils = [dil for _, dil in DIL_PAIRS]
    def split(i, nt, dy, o0, o1, o2, l0, l1, l2, bd):
        w = _dil_weights((l0, l1, l2))
        y = w[0] * o0 + w[1] * o1 + w[2] * o2
        dyy = _head_sum(dy * y, bd)
        return tuple(wg * dy for wg in w) + tuple(wg * dyy for wg in w)

    ins = ([Row(dy, GROUP_W, dy_cb)] + [Row(r[0], dil=d) for r, d in zip(saved, dils)]
           + [Row(r[1], dil=d) for r, d in zip(saved, dils)])
    outs = _rows("dil_split_bwd", split, 512, ins + [Full(_block_ones(GROUP_W, HEAD_DIM))],
                 [("row", GROUP_W, BF16, d) for d in dils] + [("row", GROUP_W, F32, d) for d in dils])
    g, got = [], {}
    for b, dil in enumerate(dils):
        r, got[f"dil{dil}"] = _dil_bwd(*qkv[dil], outs[b], saved[b][1], outs[3 + b], dil, rides.get(f"dil{dil}"))
        g.append(r)

    def finish(i, nt, q0, q1, q2, k0, k1, k2, v0, v1, v2, c, s):
        dq, dk = q0 + q1 + q2, k0 + k1 + k2
        return dq * c + _swap_halves(dq * s), dk * c + _swap_halves(dk * s), v0 + v1 + v2

    ins = [Row(g[b][t], dil=dils[b]) for t in range(3) for b in range(3)] + [Row(cosf), Row(sinf)]
    return _rows("dil_finish_bwd", finish, 512, ins, [("row", GROUP_W, BF16)] * 3), got


def _layer_fwd(x, W, l, cosf, sinf, rides):
    rides = {c: Exchange(items) for c, items in rides.items()}
    h1 = _rms_fwd(x, W["mix_norm_pre"][l][None], "mix_norm")
    p = _mm(h1, W["w_in"][l], "nn", F32, "proj_in")
    ya, sa, got_gla = _gla_fwd(p, W, l, rides.get("gla"))
    (qb, kb, vb), qkv_d = _attn_prep(p, cosf, sinf)
    bias = _na_bias(W["na_rpb"][l])
    yb, got_na = _na_fwd(qb, kb, vb, bias, rides.get("na"))
    yc, sc = _lru_fwd(p, W, l)
    yd, sd, got = _dilated_fwd(qkv_d, rides)
    got.update(gla=got_gla, na=got_na)
    ycat = jnp.concatenate([ya, yb, yc, yd], axis=1)
    ymix = _mm(ycat, W["w_out"][l], "nn", F32, "proj_out", tm=1024)
    xm = _rms_resid_fwd(x, ymix, W["mix_norm_post"][l][None], "mix_resid")
    h2 = _rms_fwd(xm, W["ffn_norm_pre"][l][None], "ffn_norm")
    gu, act = _ffn_in_swiglu(h2, W["ffn_w_in"][l])
    f = _mm(act, W["ffn_w_out"][l], "nn", F32, "ffn_out")
    xo = _rms_resid_fwd(xm, f, W["ffn_norm_post"][l][None], "ffn_resid")
    saved = dict(x=x, h1=h1, p=p, sa=sa, att=(qb, kb, vb, qkv_d), bias=bias, sc=sc, sd=sd, ycat=ycat, ymix=ymix,
                 xm=xm, h2=h2, gu=gu, act=act, f=f)
    return xo, saved, got


def _layer_bwd(dxo, W, l, S, cosf, sinf, rides, early=None):
    g = {}
    df, g["ffn_norm_post"] = _rms_bwd(dxo, S["f"], W["ffn_norm_post"][l][None], "ffn_resid_bwd", out_dtype=BF16)
    g["ffn_w_out"] = _mm(S["act"], df, "tn", BF16, "ffn_out_dw", tm=256, tk=4096)
    dgu = _ffn_out_dx_swiglu(df, W["ffn_w_out"][l], S["gu"])
    dh2 = _mm(dgu, W["ffn_w_in"][l], "nt", F32, "ffn_in_dx")
    g["ffn_w_in"] = _mm(S["h2"], dgu, "tn", BF16, "ffn_in_dw", tm=1024, tn=512, tk=4096)
    dxm, g["ffn_norm_pre"] = _rms_bwd(dh2, S["xm"], W["ffn_norm_pre"][l][None], "ffn_norm_bwd", resid=dxo)
    dymix, g["mix_norm_post"] = _rms_bwd(dxm, S["ymix"], W["mix_norm_post"][l][None], "mix_resid_bwd", out_dtype=BF16)
    dycat = _mm(dymix, W["w_out"][l], "nt", F32, "proj_out_dx", tm=1024)
    g["w_out"] = _mm(S["ycat"], dymix, "tn", BF16, "proj_out_dw", tm=1024, tn=512, tk=4096)
    if early is not None:
        for c, items in early(g).items():
            rides = {**rides, c: rides.get(c, []) + items}
    rides = {c: Exchange(items) for c, items in rides.items()}
    p = S["p"]
    qb, kb, vb, qkv_d = S["att"]
    (dqa, dka, dva, dga, dz), ga, got_gla = _gla_bwd(dycat, 0, p, W, l, S["sa"], rides.get("gla"))
    (dqb, dkb, dvb, dbias), got_na = _na_bwd(dycat, 2, qb, kb, vb, S["bias"], rides.get("na"))
    g["na_rpb"] = _na_bias_bwd(dbias)
    dxc, dgc, gc = _lru_bwd(dycat, 2, p, W, l, S["sc"])
    (dqd, dkd, dvd), got = _dilated_bwd(dycat, 3, qkv_d, S["sd"], cosf, sinf, rides)
    got.update(gla=got_gla, na=got_na)
    g.update(ga)
    g.update(gc)
    dp = jnp.concatenate([dqa, dka, dva, dga, dqb, dkb.astype(BF16), dvb.astype(BF16), dxc, dgc, dqd, dkd, dvd, dz], axis=1)
    dh1 = _mm(dp, W["w_in"][l], "nt", F32, "proj_in_dx")
    g["w_in"] = _mm(S["h1"], dp, "tn", BF16, "proj_in_dw", tm=1024, tn=640, tk=4096)
    dx, g["mix_norm_pre"] = _rms_bwd(dh1, S["x"], W["mix_norm_pre"][l][None], "mix_norm_bwd", resid=dxm)
    for n in ("ffn_norm_post", "ffn_norm_pre", "mix_norm_post", "mix_norm_pre"):
        g[n] = g[n][0]
    return dx, g, got


MESH_AXES = ("x", "y", "c")


class Xfer:
    def __init__(self, arr, kind):
        self.arr, self.kind = arr, kind
        shp = arr.shape
        if kind == "all":
            self.out = (N_DEV,) + shp
        elif kind == "slot":
            self.out = shp
        elif kind == "rows":
            self.r = shp[1] // N_DEV
            self.out = (N_DEV, shp[0], self.r, shp[2])
        else:
            self.r = shp[1]
            self.out = (shp[0], N_DEV * shp[1], shp[2])

    def src(self, ref, peer):
        if self.kind == "slot":
            return ref.at[peer]
        if self.kind == "rows":
            return ref.at[:, pl.ds(peer * self.r, self.r), :]
        return ref

    def dst(self, ref, me):
        if self.kind == "place":
            return ref.at[:, pl.ds(me * self.r, self.r), :]
        return ref.at[me]


class Exchange:
    def __init__(self, items):
        n = len(items)
        self.items = items
        self.arrays = [it.arr for it in items]
        self.specs = [pl.BlockSpec(memory_space=pl.ANY)] * n
        self.out_shape = [jax.ShapeDtypeStruct(it.out, it.arr.dtype) for it in items]
        self.scratch = [pltpu.SemaphoreType.DMA((n * (N_DEV - 1),)), pltpu.SemaphoreType.DMA((n * (N_DEV - 1),)),
                        pltpu.SemaphoreType.DMA((n,))]

    def copies(self, ins, outs, sems):
        send_sems, recv_sems, local_sems = sems
        x, y, c = (lax.axis_index(a) for a in MESH_AXES)
        me = 4 * x + 2 * y + c
        out = []
        for t, it in enumerate(self.items):
            out.append(pltpu.make_async_copy(it.src(ins[t], me), it.dst(outs[t], me), local_sems.at[t]))
            for k in range(1, N_DEV):
                px, py, pc = x ^ ((k >> 2) & 1), y ^ ((k >> 1) & 1), c ^ (k & 1)
                s = t * (N_DEV - 1) + k - 1
                out.append(pltpu.make_async_remote_copy(
                    src_ref=it.src(ins[t], 4 * px + 2 * py + pc), dst_ref=it.dst(outs[t], me),
                    send_sem=send_sems.at[s], recv_sem=recv_sems.at[s], device_id=(px, py, pc),
                    device_id_type=pl.DeviceIdType.MESH))
        return out

    def start(self, ins, outs, sems):
        for cp in self.copies(ins, outs, sems):
            cp.start()

    def wait(self, ins, outs, sems):
        for cp in self.copies(ins, outs, sems):
            cp.wait()


def _exchange(items, name):
    ex = Exchange(items)
    n = len(items)

    def body(*refs):
        ex.start(refs[:n], refs[n:2 * n], refs[2 * n:])
        ex.wait(refs[:n], refs[n:2 * n], refs[2 * n:])

    return pl.pallas_call(body, name=name, out_shape=ex.out_shape, in_specs=ex.specs, out_specs=ex.specs,
                          scratch_shapes=ex.scratch)(*ex.arrays)


def _call(kern, arrays, ride, edges, *, name, grid, in_specs, out_specs, out_shape, scratch_shapes):
    params = _ARB(len(grid))
    if ride is None:
        return pl.pallas_call(kern, name=name, grid=grid, in_specs=in_specs, out_specs=out_specs, out_shape=out_shape,
                              scratch_shapes=scratch_shapes, compiler_params=params)(*arrays), None
    ni, no, ns, nx = len(in_specs), len(out_specs), len(scratch_shapes), len(ride.items)

    def wrapped(*refs):
        ins, xin = refs[:ni], refs[ni:ni + nx]
        outs, xout = refs[ni + nx:ni + nx + no], refs[ni + nx + no:ni + 2 * nx + no]
        scr, sems = refs[ni + 2 * nx + no:ni + 2 * nx + no + ns], refs[ni + 2 * nx + no + ns:]
        first, last = edges()

        @pl.when(first)
        def _():
            ride.start(xin, xout, sems)

        kern(*ins, *outs, *scr)

        @pl.when(last)
        def _():
            ride.wait(xin, xout, sems)

    res = pl.pallas_call(
        wrapped, name=name, grid=grid, in_specs=list(in_specs) + ride.specs, out_specs=list(out_specs) + ride.specs,
        out_shape=list(out_shape) + ride.out_shape, scratch_shapes=list(scratch_shapes) + ride.scratch,
        compiler_params=params)(*arrays, *ride.arrays)
    return res[:no], res[no:]


def _column_segments(width, permuted):
    z0, z1, zn = 4 * GROUP_W, 4 * GROUP_W + 2 * GLA_RANK, 12 * GROUP_W
    segs = []
    for d in range(N_DEV):
        lo, hi = d * width, (d + 1) * width
        if not permuted:
            segs.append([(0, width, lo)])
            continue
        runs = []
        for a, b, shift in ((0, z0, 0), (z0, z1, zn - z0), (z1, 10 ** 9, -(z1 - z0))):
            s, e = max(lo, a), min(hi, b)
            if s < e:
                runs.append((s - lo, e - lo, s + shift))
        segs.append(runs)
    return segs


def _cols_from_pieces(pieces, segs, cols, name):
    _, R, w = pieces.shape
    tm = _tile(R, 256, 16)
    used = max(f + (b - a) for runs in segs for a, b, f in runs)

    def kern(p_ref, o_ref):
        for d, runs in enumerate(segs):
            for a, b, f in runs:
                o_ref[:, f:f + (b - a)] = p_ref[d, :, a:b]
        if used < cols:
            o_ref[:, used:cols] = jnp.zeros((tm, cols - used), o_ref.dtype)

    return pl.pallas_call(
        kern, name=name, grid=(R // tm,), in_specs=[pl.BlockSpec((N_DEV, tm, w), lambda i: (0, i, 0))],
        out_specs=pl.BlockSpec((tm, cols), lambda i: (i, 0)), out_shape=jax.ShapeDtypeStruct((R, cols), pieces.dtype),
        compiler_params=_ARB(1),
    )(pieces)


def _pieces_from_cols(full, segs, w, name):
    R, cols = full.shape
    tm = _tile(R, 256, 16)

    def kern(f_ref, o_ref):
        for d, runs in enumerate(segs):
            for a, b, f in runs:
                o_ref[d, :, a:b] = f_ref[:, f:f + (b - a)]

    return pl.pallas_call(
        kern, name=name, grid=(R // tm,), in_specs=[pl.BlockSpec((tm, cols), lambda i: (i, 0))],
        out_specs=pl.BlockSpec((N_DEV, tm, w), lambda i: (0, i, 0)),
        out_shape=jax.ShapeDtypeStruct((N_DEV, R, w), full.dtype), compiler_params=_ARB(1),
    )(full)


def _sum_slots(recv, name):
    n, R, C = recv.shape
    tm = _tile(R, 256, 16)

    def kern(*refs):
        acc = refs[0][...].astype(F32)
        for r in refs[1:n]:
            acc = acc + r[...].astype(F32)
        refs[n][...] = acc

    return pl.pallas_call(
        kern, name=name, grid=(R // tm,),
        in_specs=[pl.BlockSpec((None, tm, C), lambda i, _s=s: (_s, i, 0)) for s in range(n)],
        out_specs=pl.BlockSpec((tm, C), lambda i: (i, 0)), out_shape=jax.ShapeDtypeStruct((R, C), F32),
        compiler_params=_ARB(1),
    )(*([recv] * n))


BIG = (("w_in", 2), ("w_out", 1), ("ffn_w_in", 2), ("ffn_w_out", 1))
SMALL_SHARDED = ("gla_w_gate", "gla_b_gate", "lru_conv_w", "lru_b_a", "lru_b_x", "lru_lambda")
REPLICATED = ("mix_norm_pre", "mix_norm_post", "gla_norm", "na_rpb", "lru_conv_b", "lru_w_a", "lru_w_x",
              "ffn_norm_pre", "ffn_norm_post")
WEIGHTS = ("mix_norm_pre", "mix_norm_post", "w_in", "gla_w_gate", "gla_b_gate", "gla_norm", "na_rpb", "lru_conv_w",
           "lru_conv_b", "lru_w_a", "lru_b_a", "lru_w_x", "lru_b_x", "lru_lambda", "w_out", "ffn_norm_pre",
           "ffn_norm_post", "ffn_w_in", "ffn_w_out")
FLAT_C = 1024


def _to_rows(vec, row_unit):
    n = vec.shape[-1]
    rows = -(-n // (FLAT_C * row_unit)) * row_unit
    pad = [(0, 0)] * (vec.ndim - 1) + [(0, rows * FLAT_C - n)]
    return jnp.pad(vec, pad).reshape(vec.shape[:-1] + (rows, FLAT_C))


def _unshard(parts, axis):
    t = jnp.moveaxis(parts, 0, axis)
    shp = list(t.shape)
    return t.reshape(shp[:axis] + [shp[axis] * shp[axis + 1]] + shp[axis + 2:])


def _shards(full, axis):
    shp = list(full.shape)
    t = full.reshape(shp[:axis] + [N_DEV, shp[axis] // N_DEV] + shp[axis + 1:])
    return jnp.moveaxis(t, axis, 0)


def _weight_rides(W, l):
    bf = lambda n: W[n][l].astype(BF16)
    ffn = bf("ffn_w_in")
    half = ffn.shape[0] // 2
    return {"gla": [Xfer(bf("w_in"), "all"), Xfer(bf("w_out")[None], "place")],
            "na": [Xfer(bf("ffn_w_out")[None], "place")],
            "dil1": [Xfer(ffn[:half], "all")], "dil4": [Xfer(ffn[half:], "all")]}


def _unpack_weights(full, W, got):
    w_in_w, ffn_w = W["w_in"].shape[-1], W["ffn_w_in"].shape[-1]
    full["w_in"].append(_cols_from_pieces(got["gla"][0], _column_segments(w_in_w, True), P_COLS, "unpack_w_in"))
    full["w_out"].append(got["gla"][1][0])
    full["ffn_w_out"].append(got["na"][0][0])
    full["ffn_w_in"].append(jnp.concatenate(
        [_cols_from_pieces(got[c][0], _column_segments(ffn_w, False), N_DEV * ffn_w, "unpack_ffn_w_in")
         for c in ("dil1", "dil4")], axis=0))


def _w_in_grad_pieces(g, W):
    w_in_w = W["w_in"].shape[-1]
    return _pieces_from_cols(g["w_in"], _column_segments(w_in_w, True), w_in_w, "pack_w_in")


def _ffn_grad_halves(g, W):
    ffn_w = W["ffn_w_in"].shape[-1]
    p_ffn = _pieces_from_cols(g["ffn_w_in"], _column_segments(ffn_w, False), ffn_w, "pack_ffn_w_in")
    half = p_ffn.shape[1] // 2
    return p_ffn[:, :half], p_ffn[:, half:]


def _grad_rides(g, W):
    top, bottom = _ffn_grad_halves(g, W)
    return {"gla": [Xfer(_w_in_grad_pieces(g, W), "slot"), Xfer(g["w_out"][None], "rows")],
            "na": [Xfer(g["ffn_w_out"][None], "rows"), Xfer(top, "slot")], "dil1": [Xfer(bottom, "slot")]}


def _early_grad_rides(g, W):
    top, bottom = _ffn_grad_halves(g, W)
    return {"gla": [Xfer(g["w_out"][None], "rows"), Xfer(g["ffn_w_out"][None], "rows")], "na": [Xfer(top, "slot")],
            "dil4": [Xfer(bottom, "slot")]}


def _sum_big(w_in, w_out, ffn_w_out, ffn_top, ffn_bottom):
    s = lambda r, n: _sum_slots(r.reshape(N_DEV, -1, r.shape[-1]), "sum_" + n)
    return {"w_in": s(w_in, "w_in"), "w_out": s(w_out, "w_out"), "ffn_w_out": s(ffn_w_out, "ffn_w_out"),
            "ffn_w_in": jnp.concatenate([s(ffn_top, "ffn_w_in"), s(ffn_bottom, "ffn_w_in")], axis=0)}


def _exchange_named(rides, extra, name):
    names = list(rides)
    res = _exchange([it for n in names for it in rides[n]] + extra, name)
    got, at = {}, 0
    for n in names:
        got[n] = res[at:at + len(rides[n])]
        at += len(rides[n])
    return got, res[at:]


def _train(x, target, W):
    L = x.shape[0]
    depth = W["w_in"].shape[0]
    cosf, sinf = _rope_tables(L)
    small = jnp.concatenate([W[n].reshape(-1) for n in SMALL_SHARDED])
    small16 = _to_rows(lax.bitcast_convert_type(small, jnp.uint16).reshape(-1), 16)
    got, (sm,) = _exchange_named(_weight_rides(W, 0), [Xfer(small16, "all")], "gather_first")
    full = dict(W, w_in=[], w_out=[], ffn_w_in=[], ffn_w_out=[])
    _unpack_weights(full, W, got)
    sm = lax.bitcast_convert_type(sm.reshape(N_DEV, -1)[:, :2 * small.size].reshape(N_DEV, small.size, 2), F32)
    off = 0
    for n in SMALL_SHARDED:
        full[n] = _unshard(sm[:, off:off + W[n].size].reshape((N_DEV,) + W[n].shape), W[n].ndim - 1)
        off += W[n].size

    saved = []
    for l in range(depth):
        x, S, got = _layer_fwd(x, full, l, cosf, sinf, _weight_rides(W, l + 1) if l + 1 < depth else {})
        saved.append(S)
        if l + 1 < depth:
            _unpack_weights(full, W, got)
    loss, dx = _loss_fwd_bwd(x, target)

    grads, big, rides = [None] * depth, [None] * depth, {}
    for l in reversed(range(depth)):
        early = (lambda g: _early_grad_rides(g, W)) if l == 0 else None
        dx, grads[l], got = _layer_bwd(dx, full, l, saved[l], cosf, sinf, rides, early)
        if l + 1 < depth:
            big[l + 1] = _sum_big(got["gla"][0], got["gla"][1], got["na"][0], got["na"][1], got["dil1"][0])
        late = {c: len(items) for c, items in rides.items()}
        rides = _grad_rides(grads[l], W) if l > 0 else {}
    G = {n: jnp.stack([g[n] for g in grads]) for n in SMALL_SHARDED + REPLICATED}
    small_g = jnp.concatenate([_shards(G[n], G[n].ndim - 1).reshape(N_DEV, -1) for n in SMALL_SHARDED], axis=1)
    repl_g = jnp.concatenate([G[n].reshape(-1) for n in REPLICATED])
    rest = _exchange([Xfer(_w_in_grad_pieces(grads[0], W), "slot"), Xfer(_to_rows(small_g, 8), "slot"),
                      Xfer(_to_rows(repl_g, 8), "all")], "exchange_last")
    own = lambda c, k: got[c][late.get(c, 0) + k]
    big[0] = _sum_big(rest[0], own("gla", 0), own("gla", 1), own("na", 0), own("dil4", 0))
    rest = rest[1:]
    out = {n: jnp.stack([b[n] for b in big]).reshape(W[n].shape) for n, _ in BIG}
    for names, r, tag in ((SMALL_SHARDED, rest[0], "sum_small"), (REPLICATED, rest[1], "sum_replicated")):
        flat, off = _sum_slots(r, tag).reshape(-1), 0
        for n in names:
            out[n] = flat[off:off + W[n].size].reshape(W[n].shape)
            off += W[n].size
    return loss, dx, out


def _update(W, G, M, V):
    delta, new_m, new_v = {}, {}, {}
    for n, _ in BIG:
        two_d = lambda a: a.reshape(-1, a.shape[-1])
        d, m, v = _adamw(two_d(W[n]), two_d(G[n]), two_d(M[n]), two_d(V[n]), "adamw_" + n)
        delta[n], new_m[n], new_v[n] = (t.reshape(W[n].shape) for t in (d, m, v))
    rest = SMALL_SHARDED + REPLICATED
    pack = lambda D: _to_rows(jnp.concatenate([D[n].reshape(-1) for n in rest]), 16)
    d, m, v = _adamw(pack(W), pack(G), pack(M), pack(V), "adamw_small")
    off = 0
    for n in rest:
        sl = lambda t: t.reshape(-1)[off:off + W[n].size].reshape(W[n].shape)
        delta[n], new_m[n], new_v[n] = sl(d), sl(m), sl(v)
        off += W[n].size
    return delta, new_m, new_v


def kernel(x, mix_norm_pre, mix_norm_post, w_in, gla_w_gate, gla_b_gate, gla_norm, na_rpb, lru_conv_w, lru_conv_b, lru_w_a, lru_b_a, lru_w_x, lru_b_x, lru_lambda, w_out, ffn_norm_pre, ffn_norm_post, ffn_w_in, ffn_w_out, loss_target, m_mix_norm_pre, m_mix_norm_post, m_w_in, m_gla_w_gate, m_gla_b_gate, m_gla_norm, m_na_rpb, m_lru_conv_w, m_lru_conv_b, m_lru_w_a, m_lru_b_a, m_lru_w_x, m_lru_b_x, m_lru_lambda, m_w_out, m_ffn_norm_pre, m_ffn_norm_post, m_ffn_w_in, m_ffn_w_out, v_mix_norm_pre, v_mix_norm_post, v_w_in, v_gla_w_gate, v_gla_b_gate, v_gla_norm, v_na_rpb, v_lru_conv_w, v_lru_conv_b, v_lru_w_a, v_lru_b_a, v_lru_w_x, v_lru_b_x, v_lru_lambda, v_w_out, v_ffn_norm_pre, v_ffn_norm_post, v_ffn_w_in, v_ffn_w_out):
    W = dict(zip(WEIGHTS, (mix_norm_pre, mix_norm_post, w_in, gla_w_gate, gla_b_gate, gla_norm, na_rpb, lru_conv_w, lru_conv_b, lru_w_a, lru_b_a, lru_w_x, lru_b_x, lru_lambda, w_out, ffn_norm_pre, ffn_norm_post, ffn_w_in, ffn_w_out)))
    M = dict(zip(WEIGHTS, (m_mix_norm_pre, m_mix_norm_post, m_w_in, m_gla_w_gate, m_gla_b_gate, m_gla_norm, m_na_rpb, m_lru_conv_w, m_lru_conv_b, m_lru_w_a, m_lru_b_a, m_lru_w_x, m_lru_b_x, m_lru_lambda, m_w_out, m_ffn_norm_pre, m_ffn_norm_post, m_ffn_w_in, m_ffn_w_out)))
    V = dict(zip(WEIGHTS, (v_mix_norm_pre, v_mix_norm_post, v_w_in, v_gla_w_gate, v_gla_b_gate, v_gla_norm, v_na_rpb, v_lru_conv_w, v_lru_conv_b, v_lru_w_a, v_lru_b_a, v_lru_w_x, v_lru_b_x, v_lru_lambda, v_w_out, v_ffn_norm_pre, v_ffn_norm_post, v_ffn_w_in, v_ffn_w_out)))
    loss, dx, G = _train(x[0], loss_target[0], W)
    loss = lax.psum(loss, MESH_AXES)
    delta, new_m, new_v = _update(W, G, M, V)
    return (loss, dx[None], *[G[n] for n in WEIGHTS], *[delta[n] for n in WEIGHTS], *[new_m[n] for n in WEIGHTS],
            *[new_v[n] for n in WEIGHTS])
```

```python
import functools
import math

import numpy as np
import jax
import jax.numpy as jnp
from jax import lax
from jax.experimental import pallas as pl
from jax.experimental.pallas import tpu as pltpu

F32 = jnp.float32
BF16 = jnp.bfloat16

N_DEV = 8
HEAD_DIM = 64
GROUP_W = 256
GLA_RANK = 16
GLA_TAU = 16.0
GLA_CHUNK = 64
GRID_W = 64
NA_ROWS = 8
NA_COLS = 16
LRU_C = 8.0
DIL_PAIRS = ((128, 1), (512, 4), (2048, 16))
DIL_RADIUS = 64
DIL_TILE = 512
ROPE_THETA = 10000.0
EPS = 1e-6
ATT_SCALE = HEAD_DIM ** -0.5
NEG = -1e30
LANES = 128
P_COLS = 12 * GROUP_W + LANES
Z_BLOCK = 12 * GROUP_W // LANES

ADAM_LR = 0.001
ADAM_B1 = 0.9
ADAM_B2 = 0.999
ADAM_EPS = 1e-08
ADAM_WD = 0.01
ADAM_STEP = 10

VMEM_LIMIT = 56 * 1024 * 1024
_ARB = lambda n: pltpu.CompilerParams(dimension_semantics=("arbitrary",) * n, vmem_limit_bytes=VMEM_LIMIT)


def _tile(dim, pref, unit):
    t = min(pref, dim) // unit * unit
    while t >= unit:
        if dim % t == 0:
            return t
        t -= unit
    return dim


def _mm(a, b, mode, out_dtype, name, tm=512, tn=None, tk=None):
    if mode == "nn":
        (M, K), (_, N) = a.shape, b.shape
    elif mode == "nt":
        (M, K), (N, _) = a.shape, b.shape
    else:
        (K, M), (_, N) = a.shape, b.shape
    tm = _tile(M, tm, LANES if mode == "tn" else 8)
    tn = _tile(N, tn or N, LANES)
    tk = _tile(K, tk or K, LANES)
    nk = K // tk
    dims = {"nn": (((1,), (0,)), ((), ())), "nt": (((1,), (1,)), ((), ())), "tn": (((0,), (0,)), ((), ()))}[mode]

    def kern(a_ref, b_ref, o_ref, *acc):
        part = lax.dot_general(a_ref[...].astype(BF16), b_ref[...].astype(BF16), dims, preferred_element_type=F32)
        if nk == 1:
            o_ref[...] = part.astype(out_dtype)
            return
        k = pl.program_id(2)

        @pl.when(k == 0)
        def _():
            acc[0][...] = part

        @pl.when(jnp.logical_and(k > 0, k < nk - 1))
        def _():
            acc[0][...] += part

        @pl.when(k == nk - 1)
        def _():
            o_ref[...] = (acc[0][...] + part).astype(out_dtype)

    a_spec = pl.BlockSpec((tk, tm), lambda i, j, k: (k, i)) if mode == "tn" else pl.BlockSpec((tm, tk), lambda i, j, k: (i, k))
    b_spec = pl.BlockSpec((tn, tk), lambda i, j, k: (j, k)) if mode == "nt" else pl.BlockSpec((tk, tn), lambda i, j, k: (k, j))
    return pl.pallas_call(
        kern, name=name, grid=(M // tm, N // tn, nk),
        in_specs=[a_spec, b_spec], out_specs=pl.BlockSpec((tm, tn), lambda i, j, k: (i, j)),
        out_shape=jax.ShapeDtypeStruct((M, N), out_dtype),
        scratch_shapes=[pltpu.VMEM((tm, tn), F32)] if nk > 1 else [],
        compiler_params=_ARB(3),
    )(a, b)


class Row:
    def __init__(self, a, width=None, cb=0, halo=False, dil=1):
        self.a, self.width, self.cb, self.halo, self.dil = a, width, cb, halo, dil


class Full:
    def __init__(self, a):
        self.a = a


HALO = 8


def _rows(name, body, tm, ins, outs):
    outs = [o if len(o) == 4 else o + (1,) for o in outs]
    L = next(s.a.shape[0] * s.dil for s in ins if isinstance(s, Row))
    tm = _tile(L, tm, 16)
    dilated = any(s.dil > 1 for s in ins if isinstance(s, Row)) or any(o[3] > 1 for o in outs)
    nt = L // tm
    nb8 = L // HALO
    step = tm // HALO
    in_specs, arrays, layout = [], [], []
    for s in ins:
        if isinstance(s, Full):
            nd = s.a.ndim
            in_specs.append(pl.BlockSpec(s.a.shape, lambda i, _nd=nd: (0,) * _nd))
            arrays.append(s.a)
            layout.append(1)
        else:
            w = s.width or s.a.shape[1]
            in_specs.append(pl.BlockSpec((tm // s.dil, w), lambda i, _cb=s.cb: (i, _cb)))
            arrays.append(s.a)
            if s.dil > 1:
                layout.append(-s.dil)
            elif s.halo:
                in_specs.append(pl.BlockSpec((HALO, w), lambda i, _cb=s.cb: (jnp.maximum(i * step - 1, 0), _cb)))
                in_specs.append(pl.BlockSpec((HALO, w), lambda i, _cb=s.cb: (jnp.minimum((i + 1) * step, nb8 - 1), _cb)))
                arrays += [s.a, s.a]
                layout.append(3)
            else:
                layout.append(1)
    out_specs, out_shapes = [], []
    for kind, shp, dt, dil in outs:
        if kind == "row":
            out_specs.append(pl.BlockSpec((tm // dil, dil * shp), lambda i: (i, 0)))
            out_shapes.append(jax.ShapeDtypeStruct((L // dil, dil * shp), dt))
        else:
            out_specs.append(pl.BlockSpec(shp, lambda i, _n=len(shp): (0,) * _n))
            out_shapes.append(jax.ShapeDtypeStruct(shp, dt))
    n_in, n_out = len(arrays), len(outs)

    def kern(*refs):
        i = pl.program_id(0)
        lo, hi = refs[n_in + n_out:] if dilated else (None, None)

        def undilate(ref, d):
            for j in range(d):
                rows = pl.ds(j, tm // d, stride=d)
                lo[rows, :] = ref[:, j * GROUP_W:j * GROUP_W + LANES].astype(F32)
                hi[rows, :] = ref[:, j * GROUP_W + LANES:(j + 1) * GROUP_W].astype(F32)
            return jnp.concatenate([lo[...], hi[...]], axis=1)

        def dilate(val, ref, d, dt):
            lo[...] = val[:, :LANES].astype(F32)
            hi[...] = val[:, LANES:].astype(F32)
            for j in range(d):
                rows = pl.ds(j, tm // d, stride=d)
                ref[:, j * GROUP_W:j * GROUP_W + LANES] = lo[rows, :].astype(dt)
                ref[:, j * GROUP_W + LANES:(j + 1) * GROUP_W] = hi[rows, :].astype(dt)

        vals, p = [], 0
        for n in layout:
            if n == 1:
                vals.append(refs[p][...])
            elif n < 0:
                vals.append(undilate(refs[p], -n))
                n = 1
            else:
                vals.append((refs[p + 1][...], refs[p][...], refs[p + 2][...]))
            p += n
        res = body(i, nt, *vals)
        if not isinstance(res, (tuple, list)):
            res = (res,)
        for (kind, shp, dt, dil), o_ref, r in zip(outs, refs[n_in:], res):
            if kind == "row" and dil > 1:
                dilate(r, o_ref, dil, dt)
            elif kind == "row":
                o_ref[...] = r.astype(dt)
            else:
                @pl.when(i == 0)
                def _(o_ref=o_ref):
                    o_ref[...] = jnp.zeros_like(o_ref)
                o_ref[...] += r.astype(dt)

    res = pl.pallas_call(
        kern, name=name, grid=(nt,), in_specs=in_specs, out_specs=out_specs, out_shape=out_shapes,
        scratch_shapes=[pltpu.VMEM((tm, LANES), F32)] * 2 if dilated else [], compiler_params=_ARB(1),
    )(*arrays)
    return res


def _shift(h, o, i, nt):
    prev, cur, nxt = h
    if o == 0:
        return cur
    tm = cur.shape[0]
    cat = jnp.concatenate([prev, cur, nxt], axis=0)
    sh = pltpu.roll(cat, (-o) % (tm + 2 * HALO), axis=0)[HALO:HALO + tm]
    row = lax.broadcasted_iota(jnp.int32, cur.shape, 0)
    if o < 0:
        ok = jnp.logical_or(i > 0, row >= -o)
    else:
        ok = jnp.logical_or(i < nt - 1, row < tm - o)
    return jnp.where(ok, sh, 0.0)


def _colsum(v):
    return jnp.sum(v, axis=0, keepdims=True)


def _sigmoid(x):
    return 1.0 / (1.0 + jnp.exp(-x))


def _softplus(x):
    return jnp.maximum(x, 0.0) + jnp.log1p(jnp.exp(-jnp.abs(x)))


def _silu(x):
    return x * _sigmoid(x)


def _dsilu(x):
    s = _sigmoid(x)
    return s * (1.0 + x * (1.0 - s))


_GELU_C = math.sqrt(2.0 / math.pi)


def _gelu(x):
    return 0.5 * x * (1.0 + jnp.tanh(_GELU_C * (x + 0.044715 * x * x * x)))


def _dgelu(x):
    t = jnp.tanh(_GELU_C * (x + 0.044715 * x * x * x))
    return 0.5 * (1.0 + t) + 0.5 * x * (1.0 - t * t) * _GELU_C * (1.0 + 3.0 * 0.044715 * x * x)


def _head_sum(v, bd):
    return jnp.dot(v, bd, precision=lax.Precision.HIGHEST, preferred_element_type=F32)


def _block_ones(n, blk):
    r = np.arange(n)
    return jnp.asarray((r[:, None] // blk == r[None, :] // blk).astype(np.float32))


def _rms_fwd(x, g, name):
    def body(i, nt, x, g):
        r = lax.rsqrt(jnp.mean(x * x, axis=-1, keepdims=True) + EPS)
        return x * r * g
    return _rows(name, body, 256, [Row(x), Full(g)], [("row", x.shape[1], BF16)])[0]


def _rms_resid_fwd(x, y, g, name):
    def body(i, nt, x, y, g):
        r = lax.rsqrt(jnp.mean(y * y, axis=-1, keepdims=True) + EPS)
        return x + y * r * g
    return _rows(name, body, 256, [Row(x), Row(y), Full(g)], [("row", x.shape[1], F32)])[0]


def _rms_bwd(dy, x, g, name, resid=None, out_dtype=F32):
    D = x.shape[1]

    def body(i, nt, dy, x, g, *rest):
        dy = dy.astype(F32)
        r = lax.rsqrt(jnp.mean(x * x, axis=-1, keepdims=True) + EPS)
        xh = x * r
        dxh = dy * g
        dx = r * (dxh - xh * jnp.mean(dxh * xh, axis=-1, keepdims=True))
        if rest:
            dx = dx + rest[0]
        return dx, _colsum(dy * xh)

    ins = [Row(dy), Row(x), Full(g)] + ([Row(resid)] if resid is not None else [])
    return _rows(name, body, 256, ins, [("row", D, out_dtype), ("acc", (1, D), F32)])


def _ffn_in_swiglu(h, w):
    (M, K), N = h.shape, w.shape[1]
    F = N // 2
    tm = _tile(M, 256, 16)

    def kern(a_ref, b_ref, gu_ref, act_ref):
        gu = _dot(a_ref[...], b_ref[...])
        gu_ref[...] = gu
        act_ref[...] = (_silu(gu[:, :F]) * gu[:, F:]).astype(BF16)

    return pl.pallas_call(
        kern, name="ffn_in_swiglu", grid=(M // tm,),
        in_specs=[pl.BlockSpec((tm, K), lambda i: (i, 0)), pl.BlockSpec((K, N), lambda i: (0, 0))],
        out_specs=[pl.BlockSpec((tm, N), lambda i: (i, 0)), pl.BlockSpec((tm, F), lambda i: (i, 0))],
        out_shape=[jax.ShapeDtypeStruct((M, N), F32), jax.ShapeDtypeStruct((M, F), BF16)], compiler_params=_ARB(1),
    )(h, w)


def _ffn_out_dx_swiglu(df, w, gu):
    (M, K), N = df.shape, gu.shape[1]
    F = N // 2
    tm = _tile(M, 256, 16)

    def kern(a_ref, b_ref, gu_ref, o_ref):
        da = _dot(a_ref[...], b_ref[...], _NT)
        gu = gu_ref[...]
        gate, up = gu[:, :F], gu[:, F:]
        o_ref[:, :F] = (da * up * _dsilu(gate)).astype(BF16)
        o_ref[:, F:] = (da * _silu(gate)).astype(BF16)

    return pl.pallas_call(
        kern, name="ffn_out_dx_swiglu", grid=(M // tm,),
        in_specs=[pl.BlockSpec((tm, K), lambda i: (i, 0)), pl.BlockSpec((F, K), lambda i: (0, 0)),
                  pl.BlockSpec((tm, N), lambda i: (i, 0))],
        out_specs=pl.BlockSpec((tm, N), lambda i: (i, 0)), out_shape=jax.ShapeDtypeStruct((M, N), BF16),
        compiler_params=_ARB(1),
    )(df, w, gu)


def _loss_fwd_bwd(y, target):
    D = y.shape[1]

    def body(i, nt, y, t):
        err = y - t
        part = 0.5 * jnp.sum(jnp.mean(err * err, axis=-1, keepdims=True), axis=0, keepdims=True)
        return err * (1.0 / D), jnp.broadcast_to(part, (1, LANES))
    dy, loss = _rows("loss", body, 256, [Row(y), Row(target)], [("row", D, F32), ("acc", (1, LANES), F32)])
    return loss[0, 0], dy


def _adamw(w, g, m, v, name):
    C = w.shape[1]
    bc1 = 1.0 - ADAM_B1 ** ADAM_STEP
    bc2 = 1.0 - ADAM_B2 ** ADAM_STEP

    def body(i, nt, w, g, m, v):
        m = ADAM_B1 * m + (1.0 - ADAM_B1) * g
        v = ADAM_B2 * v + (1.0 - ADAM_B2) * (g * g)
        delta = -ADAM_LR * ((m / bc1) / (jnp.sqrt(v / bc2) + ADAM_EPS) + ADAM_WD * w)
        return delta, m, v
    return _rows(name, body, 256, [Row(w), Row(g), Row(m), Row(v)], [("row", C, F32)] * 3)


def _expm1(x):
    return jnp.tanh(0.5 * x) * (jnp.exp(x) + 1.0)


def _lru_gates(xh, i, nt, cw, cb, wa, wx, ba, bx, lam):
    xc = cb
    for j in range(4):
        xc = xc + cw[j:j + 1] * _shift(xh, j - 2, i, nt)
    xcb = xc.astype(BF16)
    gates = []
    for e in range(2):
        r = _sigmoid(jnp.dot(xcb, wa[e], preferred_element_type=F32) + ba[e:e + 1])
        ig = _sigmoid(jnp.dot(xcb, wx[e], preferred_element_type=F32) + bx[e:e + 1])
        sp = _softplus(-lam[e:e + 1])
        la = -LRU_C * r * sp
        gates.append((r, ig, sp, jnp.exp(la), jnp.sqrt(-_expm1(2.0 * la))))
    return xc, xcb, gates


def _scan2(af, uf, ab, ub, adjoint, name):
    L, W = af.shape
    tm = _tile(L, 512, 8)
    nt, nb = L // tm, tm // 8

    def blk(A, U, h, reverse, row):
        for d in (1, 2, 4):
            if reverse:
                ok, sh = row < 8 - d, 8 - d
            else:
                ok, sh = row >= d, d
            As = jnp.where(ok, pltpu.roll(A, sh, axis=0), 1.0)
            Us = jnp.where(ok, pltpu.roll(U, sh, axis=0), 0.0)
            U = A * Us + U
            A = A * As
        return A * h + U

    def kern(af_ref, uf_ref, ab_ref, ub_ref, of_ref, ob_ref, c_ref):
        @pl.when(pl.program_id(0) == 0)
        def _():
            c_ref[...] = jnp.zeros_like(c_ref)

        row = lax.broadcasted_iota(jnp.int32, (8, W), 0)
        full = lambda v: jnp.broadcast_to(v, (8, W))

        def body(j, carry):
            hF, aF, hB, aB = carry
            r0 = pl.multiple_of(j * 8, 8)
            r1 = pl.multiple_of((nb - 1 - j) * 8, 8)
            A, U = af_ref[pl.ds(r0, 8), :], uf_ref[pl.ds(r0, 8), :]
            if adjoint:
                C = jnp.where(row == 0, aF, pltpu.roll(A, 1, axis=0))
                aF = full(A[7:8])
            else:
                C = A
            H = blk(C, U, hF, False, row)
            of_ref[pl.ds(r0, 8), :] = H
            hF = full(H[7:8])
            A, U = ab_ref[pl.ds(r1, 8), :], ub_ref[pl.ds(r1, 8), :]
            if adjoint:
                C = jnp.where(row == 7, aB, pltpu.roll(A, 7, axis=0))
                aB = full(A[0:1])
            else:
                C = A
            H = blk(C, U, hB, True, row)
            ob_ref[pl.ds(r1, 8), :] = H
            hB = full(H[0:1])
            return hF, aF, hB, aB

        carry = lax.fori_loop(0, nb, body, (c_ref[0], c_ref[1], c_ref[2], c_ref[3]))
        for n in range(4):
            c_ref[n] = carry[n]

    fwd = pl.BlockSpec((tm, W), lambda i: (i, 0))
    bwd = pl.BlockSpec((tm, W), lambda i: (nt - 1 - i, 0))
    return pl.pallas_call(
        kern, name=name, grid=(nt,), in_specs=[fwd, fwd, bwd, bwd], out_specs=[fwd, bwd],
        out_shape=[jax.ShapeDtypeStruct((L, W), F32)] * 2,
        scratch_shapes=[pltpu.VMEM((4, 8, W), F32)], compiler_params=_ARB(1),
    )(af, uf, ab, ub)


def _block_diag(w):
    rows = jnp.tile(w.reshape(2, GROUP_W, HEAD_DIM), (1, 1, 4))
    return jnp.where(_block_ones(GROUP_W, HEAD_DIM) > 0.5, rows, 0.0).astype(BF16)


def _diag_blocks(w):
    return jnp.stack([w[:, h * 64:(h + 1) * 64, h * 64:(h + 1) * 64] for h in range(4)], axis=1)


def _lru_params(W, l):
    return [Full(W["lru_conv_w"][l]), Full(W["lru_conv_b"][l][None]), Full(_block_diag(W["lru_w_a"][l])),
            Full(_block_diag(W["lru_w_x"][l])), Full(W["lru_b_a"][l]), Full(W["lru_b_x"][l]), Full(W["lru_lambda"][l])]


def _lru_fwd(p, W, l):
    def pre(i, nt, xh, *prm):
        xc, _, g = _lru_gates(xh, i, nt, *prm)
        return g[0][3], g[0][4] * (g[0][1] * xc), g[1][3], g[1][4] * (g[1][1] * xc)

    a0, u0, a1, u1 = _rows("lru_pre", pre, 256, [Row(p, GROUP_W, 7, halo=True)] + _lru_params(W, l),
                           [("row", GROUP_W, F32)] * 4)
    hf, hb = _scan2(a0, u0, a1, u1, False, "lru_scan")
    yc = _rows("lru_post", lambda i, nt, hf, hb, gc: (hf + hb) * _gelu(gc), 512,
               [Row(hf), Row(hb), Row(p, GROUP_W, 8)], [("row", GROUP_W, BF16)])[0]
    return yc, (a0, a1, hf, hb)


def _lru_bwd(dy, dy_cb, p, W, l, saved):
    a0, a1, hf, hb = saved

    def post(i, nt, dy, hf, hb, gc):
        return dy * _gelu(gc), dy * (hf + hb) * _dgelu(gc)

    dh, dgc = _rows("lru_post_bwd", post, 512, [Row(dy, GROUP_W, dy_cb), Row(hf), Row(hb), Row(p, GROUP_W, 8)],
                    [("row", GROUP_W, F32), ("row", GROUP_W, BF16)])
    gb, gf = _scan2(a1, dh, a0, dh, True, "lru_scan_adj")

    def gates_bwd(i, nt, xh, gf, gb, hfh, hbh, cw, cb, wa, wx, ba, bx, lam):
        xc, xcb, g = _lru_gates(xh, i, nt, cw, cb, wa, wx, ba, bx, lam)
        dxc = jnp.zeros_like(xc)
        dwa, dwx, dba, dbx, dlam = [], [], [], [], []
        for e, du, hprev in ((0, gf, _shift(hfh, -1, i, nt)), (1, gb, _shift(hbh, 1, i, nt))):
            r, ig, sp, a, s = g[e]
            dxc = dxc + du * s * ig
            dla = du * hprev * a - (du * ig * xc) * a * a / s
            dza = (dla * (-LRU_C) * sp) * r * (1.0 - r)
            dzx = (du * s * xc) * ig * (1.0 - ig)
            dlam.append(_colsum(dla * r) * (LRU_C * _sigmoid(-lam[e:e + 1])))
            dba.append(_colsum(dza))
            dbx.append(_colsum(dzx))
            dzab, dzxb = dza.astype(BF16), dzx.astype(BF16)
            tn = (((0,), (0,)), ((), ()))
            nt_ = (((1,), (1,)), ((), ()))
            dwa.append(lax.dot_general(xcb, dzab, tn, preferred_element_type=F32))
            dwx.append(lax.dot_general(xcb, dzxb, tn, preferred_element_type=F32))
            dxc = dxc + lax.dot_general(dzab, wa[e], nt_, preferred_element_type=F32)
            dxc = dxc + lax.dot_general(dzxb, wx[e], nt_, preferred_element_type=F32)
        cat = lambda v: jnp.concatenate(v, axis=0)
        return dxc, jnp.stack(dwa), jnp.stack(dwx), cat(dba), cat(dbx), cat(dlam)

    dxc, dwa, dwx, dba, dbx, dlam = _rows(
        "lru_gates_bwd", gates_bwd, 256,
        [Row(p, GROUP_W, 7, halo=True), Row(gf), Row(gb), Row(hf, halo=True), Row(hb, halo=True)] + _lru_params(W, l),
        [("row", GROUP_W, F32), ("acc", (2, GROUP_W, GROUP_W), F32), ("acc", (2, GROUP_W, GROUP_W), F32),
         ("acc", (2, GROUP_W), F32), ("acc", (2, GROUP_W), F32), ("acc", (2, GROUP_W), F32)])

    def conv_bwd(i, nt, dh_, xh, cw):
        dxb = jnp.zeros_like(dh_[1])
        dcw = []
        for j in range(4):
            dxb = dxb + cw[j:j + 1] * _shift(dh_, 2 - j, i, nt)
            dcw.append(_colsum(dh_[1] * _shift(xh, j - 2, i, nt)))
        return dxb, jnp.concatenate(dcw, axis=0), _colsum(dh_[1])

    dxb, dcw, dcb = _rows("lru_conv_bwd", conv_bwd, 512,
                          [Row(dxc, halo=True), Row(p, GROUP_W, 7, halo=True), Full(W["lru_conv_w"][l])],
                          [("row", GROUP_W, BF16), ("acc", (4, GROUP_W), F32), ("acc", (1, GROUP_W), F32)])
    grads = dict(lru_conv_w=dcw, lru_conv_b=dcb[0], lru_w_a=_diag_blocks(dwa), lru_w_x=_diag_blocks(dwx),
                 lru_b_a=dba, lru_b_x=dbx, lru_lambda=dlam)
    return dxb, dgc, grads


_NT = (((1,), (1,)), ((), ()))
_TN = (((0,), (0,)), ((), ()))


def _dot(a, b, dims=None):
    if dims is None:
        return jnp.dot(a, b, preferred_element_type=F32)
    return lax.dot_general(a, b, dims, preferred_element_type=F32)


def _dot_exact(a, b):
    return jnp.dot(a, b, precision=lax.Precision.HIGHEST, preferred_element_type=F32)


def _gla_gate_w(w_gate, b_gate):
    zero = jnp.zeros((GLA_RANK, GROUP_W), w_gate.dtype)
    wg = jnp.concatenate([jnp.concatenate([w_gate[0], zero], axis=1), jnp.concatenate([zero, w_gate[1]], axis=1),
                          jnp.zeros((LANES - 2 * GLA_RANK, 2 * GROUP_W), w_gate.dtype)], axis=0)
    return wg.astype(BF16), b_gate.reshape(1, 2 * GROUP_W)


def _gla_gates_fwd(p, wg, bg):
    def body(i, nt, z, wg, bg):
        logit = _dot(z.astype(BF16), wg) + bg
        la = -_softplus(-logit) * (1.0 / GLA_TAU)
        return la[:, :GROUP_W], la[:, GROUP_W:]
    return _rows("gla_gates", body, 512, [Row(p, LANES, Z_BLOCK), Full(wg), Full(bg)], [("row", GROUP_W, F32)] * 2)


def _gla_gates_bwd(p, dla0, dla1, wg, bg):
    def body(i, nt, z, d0, d1, wg, bg):
        zb = z.astype(BF16)
        logit = _dot(zb, wg) + bg
        dlogit = jnp.concatenate([d0, d1], axis=1) * (1.0 / GLA_TAU) * _sigmoid(-logit)
        dlb = dlogit.astype(BF16)
        return _dot(dlb, wg, _NT), _dot(zb, dlb, _TN), _colsum(dlogit)
    return _rows("gla_gates_bwd", body, 512, [Row(p, LANES, Z_BLOCK), Row(dla0), Row(dla1), Full(wg), Full(bg)],
                 [("row", LANES, BF16), ("acc", (LANES, 2 * GROUP_W), F32), ("acc", (1, 2 * GROUP_W), F32)])


def _gla_order(reverse):
    t = np.arange(GLA_CHUNK)
    m = (t[None, :] >= t[:, None]) if reverse else (t[None, :] <= t[:, None])
    return m.astype(np.float32), (32, 0) if reverse else (31, 63)


def _stack_heads(x, bd):
    return jnp.where(bd, jnp.concatenate([x] * 4, axis=0), 0.0)


def _diag_heads(r, bd):
    r = jnp.where(bd, r, 0.0)
    return r[0:64] + r[64:128] + r[128:192] + r[192:256]


def _gla_factors(q_ref, k_ref, rows, b, mid, last):
    bm, bl = b[mid:mid + 1], b[last:last + 1]
    qs = q_ref[rows, :] * ATT_SCALE
    k = k_ref[rows, :]
    P, N, E, Fd = jnp.exp(b - bm), jnp.exp(bm - b), jnp.exp(b), jnp.exp(bl - b)
    return (P, N, E, Fd, jnp.exp(bl)), (qs * P, k * N, qs * E, k * Fd)


def _gla_specs(L, walk_up):
    tm = _tile(L, 512, GLA_CHUNK)
    nt, nc = L // tm, tm // GLA_CHUNK
    specs = []
    for up in walk_up:
        t = (lambda i: i) if up else (lambda i: nt - 1 - i)
        specs.append(dict(
            col=lambda cb, _t=t: pl.BlockSpec((tm, GROUP_W), lambda i: (_t(i), cb)),
            row=pl.BlockSpec((tm, GROUP_W), lambda i, _t=t: (_t(i), 0)),
            state=pl.BlockSpec((nc, GROUP_W, GROUP_W), lambda i, _t=t: (_t(i), 0, 0))))
    return nt, nc, specs


def _gla_chunk_fwd(p, la0, la1, ride=None):
    L = la0.shape[0]
    nt, nc, specs = _gla_specs(L, (True, False))
    orders = [_gla_order(False), _gla_order(True)]

    def kern(q0, k0, v0, l0, q1, k1, v1, l1, m0_ref, m1_ref, bd_ref, o0, s0, o1, s1, st_ref):
        @pl.when(pl.program_id(0) == 0)
        def _():
            st_ref[...] = jnp.zeros_like(st_ref)

        bd = bd_ref[...] > 0.5
        dirs = []
        for e, (q_ref, k_ref, v_ref, la_ref, m_ref, o_ref, s_ref) in enumerate(
                ((q0, k0, v0, l0, m0_ref, o0, s0), (q1, k1, v1, l1, m1_ref, o1, s1))):
            mv = m_ref[...]
            dirs.append((q_ref, k_ref, v_ref, la_ref, mv, jnp.concatenate([mv] * 4, axis=0) > 0.5, o_ref, s_ref))

        def body(cc, carry):
            E = range(2)
            cs = [nc - 1 - cc if e else cc for e in E]
            rows = [pl.ds(pl.multiple_of(c * GLA_CHUNK, GLA_CHUNK), GLA_CHUNK) for c in cs]
            b = [_dot_exact(dirs[e][4], dirs[e][3][rows[e], :]) for e in E]
            t = [_gla_factors(dirs[e][0], dirs[e][1], rows[e], b[e], *orders[e][1]) for e in E]
            vb = [dirs[e][2][rows[e], :].astype(BF16) for e in E]
            st = [st_ref[e] for e in E]
            a = [_dot(_stack_heads(t[e][1][0], bd).astype(BF16), t[e][1][1].astype(BF16), _NT) for e in E]
            inter = [_dot(t[e][1][2].astype(BF16), st[e].astype(BF16), _NT) for e in E]
            kv = [_dot(vb[e], t[e][1][3].astype(BF16), _TN) for e in E]
            a = [jnp.where(dirs[e][5], a[e], 0.0).astype(BF16) for e in E]
            r = [_dot(a[e], vb[e]) for e in E]
            for e in E:
                dirs[e][7][cs[e]] = st[e]
                dirs[e][6][rows[e], :] = _diag_heads(r[e], bd) + inter[e]
                st_ref[e] = st[e] * t[e][0][4] + jnp.where(bd, kv[e], 0.0)
            return carry

        lax.fori_loop(0, nc, body, 0)

    const = lambda shp: pl.BlockSpec(shp, lambda i: (0, 0))
    in_specs, out_specs = [], []
    for sp in specs:
        in_specs += [sp["col"](0), sp["col"](1), sp["col"](2), sp["row"]]
        out_specs += [sp["row"], sp["state"]]
    return _call(
        kern, (p, p, p, la0, p, p, p, la1, jnp.asarray(orders[0][0]), jnp.asarray(orders[1][0]),
               _block_ones(GROUP_W, HEAD_DIM)),
        ride, lambda: (pl.program_id(0) == 0, pl.program_id(0) == nt - 1), name="gla_fwd", grid=(nt,),
        in_specs=in_specs + [const((GLA_CHUNK, GLA_CHUNK))] * 2 + [const((GROUP_W, GROUP_W))], out_specs=out_specs,
        out_shape=[jax.ShapeDtypeStruct((L, GROUP_W), F32),
                   jax.ShapeDtypeStruct((L // GLA_CHUNK, GROUP_W, GROUP_W), F32)] * 2,
        scratch_shapes=[pltpu.VMEM((2, GROUP_W, GROUP_W), F32)])


def _gla_chunk_bwd(p, la0, la1, do, sprev0, sprev1, ride=None):
    L = la0.shape[0]
    nt, nc, specs = _gla_specs(L, (False, True))
    orders = [_gla_order(False), _gla_order(True)]

    def kern(q0, k0, v0, l0, do0, s0, q1, k1, v1, l1, do1, s1, m0_ref, m1_ref, t0_ref, t1_ref, bd_ref, *rest):
        outs, dst_ref = (rest[0:4], rest[4:8]), rest[8]

        @pl.when(pl.program_id(0) == 0)
        def _():
            dst_ref[...] = jnp.zeros_like(dst_ref)

        bd = bd_ref[...] > 0.5
        row = lax.broadcasted_iota(jnp.int32, (GLA_CHUNK, GROUP_W), 0)
        dirs = []
        for ins, m_ref, t_ref in (((q0, k0, v0, l0, do0, s0), m0_ref, t0_ref), ((q1, k1, v1, l1, do1, s1), m1_ref, t1_ref)):
            mv = m_ref[...]
            dirs.append(ins + (mv, t_ref[...], jnp.concatenate([mv] * 4, axis=0) > 0.5))

        def body(cc, carry):
            E2 = range(2)
            cs = [cc if e else nc - 1 - cc for e in E2]
            rows = [pl.ds(pl.multiple_of(c * GLA_CHUNK, GLA_CHUNK), GLA_CHUNK) for c in cs]
            b = [_dot_exact(dirs[e][6], dirs[e][3][rows[e], :]) for e in E2]
            t = [_gla_factors(dirs[e][0], dirs[e][1], rows[e], b[e], *orders[e][1]) for e in E2]
            vb = [dirs[e][2][rows[e], :].astype(BF16) for e in E2]
            dov = [dirs[e][4][rows[e], :] for e in E2]
            dob = [x.astype(BF16) for x in dov]
            st = [dirs[e][5][cs[e]] for e in E2]
            dst = [dst_ref[e] for e in E2]
            stb, dstb = [x.astype(BF16) for x in st], [x.astype(BF16) for x in dst]
            qst = [_stack_heads(t[e][1][0], bd).astype(BF16) for e in E2]
            dost = [_stack_heads(dov[e], bd).astype(BF16) for e in E2]
            kNb, qEb, kFb = ([t[e][1][n].astype(BF16) for e in E2] for n in (1, 2, 3))
            a = [_dot(qst[e], kNb[e], _NT) for e in E2]
            da = [_dot(dost[e], vb[e], _NT) for e in E2]
            dqE = [_dot(dob[e], stb[e]) for e in E2]
            dkF = [_dot(vb[e], dstb[e]) for e in E2]
            dv_inter = [_dot(kFb[e], dstb[e], _NT) for e in E2]
            dst_in = [_dot(dob[e], qEb[e], _TN) for e in E2]
            a = [jnp.where(dirs[e][8], a[e], 0.0).astype(BF16) for e in E2]
            da = [jnp.where(dirs[e][8], da[e], 0.0).astype(BF16) for e in E2]
            dv_intra = [_dot(a[e], dost[e], _TN) for e in E2]
            dqP = [_dot(da[e], kNb[e]) for e in E2]
            dkN = [_dot(da[e], qst[e], _TN) for e in E2]
            db = []
            for e in E2:
                (P, N, Ef, Fd, d), (qP, kN, qE, kF) = t[e]
                mid, last = orders[e][1]
                dq_ref, dk_ref, dv_ref, _ = outs[e]
                dqp = _diag_heads(dqP[e], bd)
                dd = _colsum(dst[e] * st[e])
                dst_ref[e] = jnp.where(bd, dst_in[e], 0.0) + dst[e] * d
                tP, tN, tE, tF = dqp * qP, dkN[e] * kN, dqE[e] * qE, dkF[e] * kF
                db.append(tP - tN + tE - tF + jnp.where(row == mid, _colsum(tN - tP), 0.0)
                          + jnp.where(row == last, _colsum(tF) + dd * d, 0.0))
                dq_ref[rows[e], :] = (dqp * P + dqE[e] * Ef) * ATT_SCALE
                dk_ref[rows[e], :] = dkN[e] * N + dkF[e] * Fd
                dv_ref[rows[e], :] = dv_intra[e] + dv_inter[e]
            dla = [_dot_exact(dirs[e][7], db[e]) for e in E2]
            for e in E2:
                outs[e][3][rows[e], :] = dla[e]
            return carry

        lax.fori_loop(0, nc, body, 0)

    const = lambda shp: pl.BlockSpec(shp, lambda i: (0, 0))
    in_specs, out_specs = [], []
    for sp in specs:
        in_specs += [sp["col"](0), sp["col"](1), sp["col"](2), sp["row"], sp["row"], sp["state"]]
        out_specs += [sp["row"]] * 4
    m0, m1 = orders[0][0], orders[1][0]
    return _call(
        kern, (p, p, p, la0, do, sprev0, p, p, p, la1, do, sprev1, jnp.asarray(m0), jnp.asarray(m1),
               jnp.asarray(m0.T.copy()), jnp.asarray(m1.T.copy()), _block_ones(GROUP_W, HEAD_DIM)),
        ride, lambda: (pl.program_id(0) == 0, pl.program_id(0) == nt - 1), name="gla_bwd", grid=(nt,),
        in_specs=in_specs + [const((GLA_CHUNK, GLA_CHUNK))] * 4 + [const((GROUP_W, GROUP_W))], out_specs=out_specs,
        out_shape=[jax.ShapeDtypeStruct((L, GROUP_W), F32)] * 8,
        scratch_shapes=[pltpu.VMEM((2, GROUP_W, GROUP_W), F32)])


def _gla_fwd(p, W, l, ride=None):
    wg, bg = _gla_gate_w(W["gla_w_gate"][l], W["gla_b_gate"][l])
    la0, la1 = _gla_gates_fwd(p, wg, bg)
    (of, s0, ob, s1), got = _gla_chunk_fwd(p, la0, la1, ride)

    def post(i, nt, of, ob, g, ng, bd):
        o = of + ob
        r = lax.rsqrt(_head_sum(o * o, bd) * (1.0 / HEAD_DIM) + EPS)
        return o * r * ng * _silu(g)

    ya = _rows("gla_post", post, 512, [Row(of), Row(ob), Row(p, GROUP_W, 3), Full(W["gla_norm"][l][None]),
                                       Full(_block_ones(GROUP_W, HEAD_DIM))], [("row", GROUP_W, BF16)])[0]
    return ya, (la0, la1, of, ob, s0, s1), got


def _gla_bwd(dy, dy_cb, p, W, l, saved, ride=None):
    la0, la1, of, ob, s0, s1 = saved
    wg, bg = _gla_gate_w(W["gla_w_gate"][l], W["gla_b_gate"][l])

    def post(i, nt, dy, of, ob, g, ng, bd):
        o = of + ob
        r = lax.rsqrt(_head_sum(o * o, bd) * (1.0 / HEAD_DIM) + EPS)
        oh = o * r
        don = dy * _silu(g)
        doh = don * ng
        do = r * (doh - oh * _head_sum(doh * oh, bd) * (1.0 / HEAD_DIM))
        return do, dy * (oh * ng) * _dsilu(g), _colsum(don * oh)

    do, dg, dng = _rows("gla_post_bwd", post, 512,
                        [Row(dy, GROUP_W, dy_cb), Row(of), Row(ob), Row(p, GROUP_W, 3), Full(W["gla_norm"][l][None]),
                         Full(_block_ones(GROUP_W, HEAD_DIM))],
                        [("row", GROUP_W, F32), ("row", GROUP_W, BF16), ("acc", (1, GROUP_W), F32)])
    (dq0, dk0, dv0, dla0, dq1, dk1, dv1, dla1), got = _gla_chunk_bwd(p, la0, la1, do, s0, s1, ride)
    dq, dk, dv = _rows("gla_sum_bwd", lambda i, nt, a0, a1, b0, b1, c0, c1: (a0 + a1, b0 + b1, c0 + c1), 512,
                       [Row(t) for t in (dq0, dq1, dk0, dk1, dv0, dv1)], [("row", GROUP_W, BF16)] * 3)
    dz, dwg, dbg = _gla_gates_bwd(p, dla0, dla1, wg, bg)
    dw_gate = jnp.stack([dwg[e * GLA_RANK:(e + 1) * GLA_RANK, e * GROUP_W:(e + 1) * GROUP_W] for e in range(2)])
    grads = dict(gla_w_gate=dw_gate, gla_b_gate=dbg.reshape(2, GROUP_W), gla_norm=dng[0])
    return (dq, dk, dv, dg, dz), grads, got


def _rope_tables(L):
    pos = jnp.arange(L, dtype=F32)
    inv_freq = ROPE_THETA ** (-jnp.arange(0, HEAD_DIM, 2, dtype=F32) / HEAD_DIM)
    ang = pos[:, None] * inv_freq[None, :]
    cos, sin = jnp.cos(ang), jnp.sin(ang)
    return jnp.tile(jnp.concatenate([cos, cos], axis=1), (1, 4)), jnp.tile(jnp.concatenate([-sin, sin], axis=1), (1, 4))


def _swap_halves(t):
    lane = lax.broadcasted_iota(jnp.int32, t.shape, 1)
    first = (lane & (HEAD_DIM - 1)) < HEAD_DIM // 2
    return jnp.where(first, pltpu.roll(t, GROUP_W - HEAD_DIM // 2, axis=1), pltpu.roll(t, HEAD_DIM // 2, axis=1))


def _attn_prep(p, cosf, sinf):
    dils = [dil for _, dil in DIL_PAIRS]

    def body(i, nt, qb, kb, vb, qd, kd, vd, c, s):
        d = (qd * c + _swap_halves(qd) * s, kd * c + _swap_halves(kd) * s, vd)
        return (qb, kb, vb) + d * len(dils)
    ins = [Row(p, GROUP_W, cb) for cb in (4, 5, 6, 9, 10, 11)] + [Row(cosf), Row(sinf)]
    outs = [("row", GROUP_W, BF16)] * 3 + [("row", GROUP_W, BF16, dil) for dil in dils for _ in range(3)]
    res = _rows("attn_prep", body, 512, ins, outs)
    return tuple(res[:3]), {dil: tuple(res[3 + 3 * n:6 + 3 * n]) for n, dil in enumerate(dils)}


def _na_onehot():
    c = np.arange(GRID_W)
    dc = np.clip(c[None, :] - c[:, None], -(NA_COLS - 1), NA_COLS - 1) + NA_COLS - 1
    oh = np.zeros((LANES, GRID_W * GRID_W), np.float32)
    oh[dc.reshape(-1), np.arange(GRID_W * GRID_W)] = 1.0
    return jnp.asarray(oh)


def _na_colmask():
    c = np.arange(GRID_W)
    start = np.clip(c - NA_COLS // 2, 0, GRID_W - NA_COLS)
    ok = (c[None, :] >= start[:, None]) & (c[None, :] < start[:, None] + NA_COLS)
    return jnp.asarray(np.where(ok, 0.0, NEG).astype(np.float32))


N_DR = 2 * NA_ROWS - 1


NA_HALF = GRID_W // 2
NA_KCOLS = 48
NA_WIN = NA_ROWS * NA_KCOLS
NA_ROWS_PER_STEP = 4
NA_BWD_ROWS_PER_STEP = 4


def _na_bias(rpb):
    rp = jnp.pad(rpb.reshape(4 * N_DR, 2 * NA_COLS - 1), ((0, GRID_W - 4 * N_DR), (0, LANES - 2 * NA_COLS + 1)))

    def expand(r_ref, oh_ref, o_ref):
        o_ref[...] = _dot_exact(r_ref[...], oh_ref[...])

    r = pl.pallas_call(expand, name="na_bias_expand",
                       out_shape=jax.ShapeDtypeStruct((GRID_W, GRID_W * GRID_W), F32))(rp, _na_onehot())
    r = r[:4 * N_DR].reshape(4, N_DR, GRID_W, GRID_W)

    def build(r_ref, m_ref, o_ref):
        for h in range(4):
            for c in range(NA_ROWS):
                for half in range(2):
                    q0, k0 = NA_HALF * half, 16 * half
                    for i in range(NA_ROWS):
                        o_ref[h, c, half, :, i * NA_KCOLS:(i + 1) * NA_KCOLS] = (
                            r_ref[h, i - c + NA_ROWS - 1, q0:q0 + NA_HALF, k0:k0 + NA_KCOLS]
                            + m_ref[q0:q0 + NA_HALF, k0:k0 + NA_KCOLS])

    return pl.pallas_call(build, name="na_bias_build",
                          out_shape=jax.ShapeDtypeStruct((4, NA_ROWS, 2, NA_HALF, NA_WIN), F32))(r, _na_colmask())


def _na_bias_bwd(dbias):
    def fold(d_ref, o_ref):
        o_ref[...] = jnp.zeros_like(o_ref)
        for h in range(4):
            for a in range(N_DR):
                for half in range(2):
                    q0, k0 = NA_HALF * half, 16 * half
                    acc = jnp.zeros((NA_HALF, NA_KCOLS), F32)
                    for c in range(NA_ROWS):
                        i = a + c - (NA_ROWS - 1)
                        if 0 <= i < NA_ROWS:
                            acc = acc + d_ref[h, c, half, :, i * NA_KCOLS:(i + 1) * NA_KCOLS]
                    o_ref[h, a, q0:q0 + NA_HALF, k0:k0 + NA_KCOLS] = acc

    dr = pl.pallas_call(fold, name="na_bias_fold",
                        out_shape=jax.ShapeDtypeStruct((4, N_DR, GRID_W, GRID_W), F32))(dbias)
    dr = jnp.pad(dr.reshape(4 * N_DR, GRID_W * GRID_W), ((0, GRID_W - 4 * N_DR), (0, 0)))

    def contract(d_ref, oh_ref, o_ref):
        o_ref[...] = lax.dot_general(d_ref[...], oh_ref[...], _NT, precision=lax.Precision.HIGHEST,
                                     preferred_element_type=F32)

    g = pl.pallas_call(contract, name="na_bias_contract",
                       out_shape=jax.ShapeDtypeStruct((GRID_W, LANES), F32))(dr, _na_onehot())
    return g[:4 * N_DR, :2 * NA_COLS - 1].reshape(4, N_DR, 2 * NA_COLS - 1)


def _na_window(r, n_rows):
    rs = jnp.clip(r - NA_ROWS // 2, 0, n_rows - NA_ROWS)
    return rs, r - rs


def _na_key_rows(rs, half, t):
    return pl.ds(pl.multiple_of((rs + t) * GRID_W + 16 * half, 16), NA_KCOLS)


def _na_keys(ref, rs, half):
    return jnp.concatenate([ref[_na_key_rows(rs, half, t), :] for t in range(NA_ROWS)], axis=0)


def _na_stack(x, first):
    zero = jnp.zeros_like(x)
    return jnp.concatenate([jnp.where(first, x, zero), jnp.where(first, zero, x)], axis=0)


def _na_bias_spec():
    return pl.BlockSpec((2, NA_ROWS, 2, NA_HALF, NA_WIN), lambda j, i: (j, 0, 0, 0, 0))


def _grid_edges(n0, n1):
    j, i = pl.program_id(0), pl.program_id(1)
    return jnp.logical_and(j == 0, i == 0), jnp.logical_and(j == n0 - 1, i == n1 - 1)


def _na_fwd(q, k, v, bias, ride=None):
    L = q.shape[0]
    n_rows = L // GRID_W
    tm = _tile(L, 512, GRID_W)
    nt, nr = L // tm, tm // GRID_W

    def kern(q_ref, k_ref, v_ref, b_ref, o_ref):
        i = pl.program_id(1)
        first = lax.broadcasted_iota(jnp.int32, (NA_HALF, LANES), 1) < HEAD_DIM

        def body(it, carry):
            parts = []
            for u in range(NA_ROWS_PER_STEP):
                rr = it * NA_ROWS_PER_STEP + u
                rs, c = _na_window(i * nr + rr, n_rows)
                for half in range(2):
                    rows = pl.ds(pl.multiple_of(rr * GRID_W + NA_HALF * half, NA_HALF), NA_HALF)
                    bias = jnp.concatenate([b_ref[0, c, half], b_ref[1, c, half]], axis=0)
                    parts.append((rows, _na_stack(q_ref[rows, :], first), bias, _na_keys(k_ref, rs, half),
                                  _na_keys(v_ref, rs, half)))
            s = [_dot(qs, kw, _NT) * ATT_SCALE + bias for _, qs, bias, kw, _ in parts]
            e = [jnp.exp(x - jnp.max(x, axis=-1, keepdims=True)) for x in s]
            pn = [(x / jnp.sum(x, axis=-1, keepdims=True)).astype(BF16) for x in e]
            o = [_dot(p, part[4]) for p, part in zip(pn, parts)]
            for x, (rows, *_) in zip(o, parts):
                o_ref[rows, :] = jnp.where(first, x[:NA_HALF], x[NA_HALF:]).astype(BF16)
            return carry

        lax.fori_loop(0, nr // NA_ROWS_PER_STEP, body, 0)

    qspec = pl.BlockSpec((tm, LANES), lambda j, i: (i, j))
    kvspec = pl.BlockSpec((L, LANES), lambda j, i: (0, j))
    (y,), got = _call(
        kern, (q, k, v, bias), ride, lambda: _grid_edges(2, nt), name="na_fwd", grid=(2, nt),
        in_specs=[qspec, kvspec, kvspec, _na_bias_spec()],
        out_specs=[qspec], out_shape=[jax.ShapeDtypeStruct((L, GROUP_W), BF16)], scratch_shapes=[])
    return y, got


def _na_bwd(dy, dy_block, q, k, v, bias, ride=None):
    L = q.shape[0]
    n_rows = L // GRID_W
    tm = _tile(L, 512, GRID_W)
    nt, nr = L // tm, tm // GRID_W

    def kern(dy_ref, q_ref, k_ref, v_ref, b_ref, dq_ref, dk_ref, dv_ref, db_ref):
        i = pl.program_id(1)

        @pl.when(i == 0)
        def _():
            dk_ref[...] = jnp.zeros_like(dk_ref)
            dv_ref[...] = jnp.zeros_like(dv_ref)
            db_ref[...] = jnp.zeros_like(db_ref)

        first = lax.broadcasted_iota(jnp.int32, (NA_HALF, LANES), 1) < HEAD_DIM

        def body(it, carry):
            parts = []
            for u in range(NA_BWD_ROWS_PER_STEP):
                rr = it * NA_BWD_ROWS_PER_STEP + u
                rs, c = _na_window(i * nr + rr, n_rows)
                for half in range(2):
                    rows = pl.ds(pl.multiple_of(rr * GRID_W + NA_HALF * half, NA_HALF), NA_HALF)
                    bias = jnp.concatenate([b_ref[0, c, half], b_ref[1, c, half]], axis=0)
                    parts.append((rows, half, _na_stack(q_ref[rows, :], first),
                                  _na_stack(dy_ref[rows, :].astype(BF16), first), bias, _na_keys(k_ref, rs, half),
                                  _na_keys(v_ref, rs, half), rs, c))
            s = [_dot(part[2], part[5], _NT) * ATT_SCALE + part[4] for part in parts]
            dp = [_dot(part[3], part[6], _NT) for part in parts]
            e = [jnp.exp(x - jnp.max(x, axis=-1, keepdims=True)) for x in s]
            pn = [x / jnp.sum(x, axis=-1, keepdims=True) for x in e]
            ds = [p * (d - jnp.sum(p * d, axis=-1, keepdims=True)) for p, d in zip(pn, dp)]
            dsb = [x.astype(BF16) for x in ds]
            pnb = [x.astype(BF16) for x in pn]
            dq = [_dot(x, part[5]) for x, part in zip(dsb, parts)]
            dk = [_dot(x, part[2], _TN) for x, part in zip(dsb, parts)]
            dv = [_dot(x, part[3], _TN) for x, part in zip(pnb, parts)]
            for n, (rows, half, _, _, _, _, _, rs, c) in enumerate(parts):
                db_ref[0, c, half] += ds[n][:NA_HALF]
                db_ref[1, c, half] += ds[n][NA_HALF:]
                dq_ref[rows, :] = (jnp.where(first, dq[n][:NA_HALF], dq[n][NA_HALF:]) * ATT_SCALE).astype(BF16)
                for t in range(NA_ROWS):
                    kr = _na_key_rows(rs, half, t)
                    dk_ref[kr, :] += dk[n][t * NA_KCOLS:(t + 1) * NA_KCOLS] * ATT_SCALE
                    dv_ref[kr, :] += dv[n][t * NA_KCOLS:(t + 1) * NA_KCOLS]
            return carry

        lax.fori_loop(0, nr // NA_BWD_ROWS_PER_STEP, body, 0)

    qspec = pl.BlockSpec((tm, LANES), lambda j, i: (i, j))
    kvspec = pl.BlockSpec((L, LANES), lambda j, i: (0, j))
    return _call(
        kern, (dy, q, k, v, bias), ride, lambda: _grid_edges(2, nt), name="na_bwd", grid=(2, nt),
        in_specs=[pl.BlockSpec((tm, LANES), lambda j, i: (i, dy_block + j)), qspec, kvspec, kvspec, _na_bias_spec()],
        out_specs=[qspec, kvspec, kvspec, _na_bias_spec()],
        out_shape=[jax.ShapeDtypeStruct((L, GROUP_W), BF16), jax.ShapeDtypeStruct((L, GROUP_W), F32),
                   jax.ShapeDtypeStruct((L, GROUP_W), F32),
                   jax.ShapeDtypeStruct((4, NA_ROWS, 2, NA_HALF, NA_WIN), F32)], scratch_shapes=[])


def _dil_specs(n, tq):
    R = DIL_RADIUS
    step, nb = tq // R, n // R
    main = pl.BlockSpec((tq, LANES), lambda j, i: (i, j))
    prev = pl.BlockSpec((R, LANES), lambda j, i: (jnp.maximum(i * step - 1, 0), j))
    nxt = pl.BlockSpec((R, LANES), lambda j, i: (jnp.minimum((i + 1) * step, nb - 1), j))
    return main, prev, nxt


def _dil_masks():
    R = DIL_RADIUS
    r = np.arange(2 * R)[:, None] & (R - 1)
    c = np.arange(3 * R)[None, :]
    band = np.abs(c - R - r) <= R
    ok = np.stack([band, band & (c >= R), band & (c < 2 * R), band & (c >= R) & (c < 2 * R)])
    return jnp.asarray(np.where(ok, 0.0, NEG).astype(np.float32))


def _dil_mask_spec():
    return pl.BlockSpec((4, 2 * DIL_RADIUS, 3 * DIL_RADIUS), lambda j, i: (0, 0, 0))


def _dil_mask(m_ref, i, sb, n_tiles, n_blocks):
    idx = 0
    if sb == 0:
        idx = idx + jnp.where(i == 0, 1, 0)
    if sb == n_blocks - 1:
        idx = idx + jnp.where(i == n_tiles - 1, 2, 0)
    return m_ref[idx]


def _dil_fwd(q, k, v, dil, ride=None):
    n = q.shape[0]
    tq = _tile(n, DIL_TILE, DIL_RADIUS)

    def kern(q_ref, kp_ref, k_ref, kn_ref, vp_ref, v_ref, vn_ref, m_ref, o_ref, l_ref):
        i = pl.program_id(1)
        R = DIL_RADIUS
        ka = jnp.concatenate([kp_ref[...], k_ref[...], kn_ref[...]], axis=0)
        va = jnp.concatenate([vp_ref[...], v_ref[...], vn_ref[...]], axis=0)
        first = lax.broadcasted_iota(jnp.int32, (R, LANES), 1) < HEAD_DIM
        subs = range(tq // R)
        keys = lambda a, sb: a[sb * R:(sb + 3) * R]
        qs = [_na_stack(q_ref[sb * R:(sb + 1) * R, :], first) for sb in subs]
        s = [_dot(qs[sb], keys(ka, sb), _NT) for sb in subs]
        s = [s[sb] * ATT_SCALE + _dil_mask(m_ref, i, sb, n // tq, len(subs)) for sb in subs]
        m = [jnp.max(x, axis=-1, keepdims=True) for x in s]
        e = [jnp.exp(x - mx) for x, mx in zip(s, m)]
        den = [jnp.sum(x, axis=-1, keepdims=True) for x in e]
        o = [_dot((e[sb] / den[sb]).astype(BF16), keys(va, sb)) for sb in subs]
        for sb in subs:
            lse = m[sb] + jnp.log(den[sb])
            o_ref[sb * R:(sb + 1) * R, :] = jnp.where(first, o[sb][:R], o[sb][R:])
            l_ref[sb * R:(sb + 1) * R, :] = jnp.where(first, lse[:R], lse[R:])

    main, prev, nxt = _dil_specs(n, tq)
    (o, lse), got = _call(
        kern, (q, k, k, k, v, v, v, _dil_masks()), ride,
        lambda: _grid_edges(2 * dil, n // tq), name=f"dil_fwd_{dil}", grid=(2 * dil, n // tq),
        in_specs=[main, prev, main, nxt, prev, main, nxt, _dil_mask_spec()], out_specs=[main, main],
        out_shape=[jax.ShapeDtypeStruct((n, dil * GROUP_W), F32)] * 2, scratch_shapes=[])
    return (o, lse), got


def _dil_bwd(q, k, v, do, lse, dterm, dil, ride=None):
    n = q.shape[0]
    R = DIL_RADIUS
    tq = _tile(n, DIL_TILE, R)
    nq = n // tq

    def kern(q_ref, kp_ref, k_ref, kn_ref, vp_ref, v_ref, vn_ref, do_ref, l_ref, dt_ref, m_ref, dq_ref, dk_ref, dv_ref):
        i = pl.program_id(1)

        @pl.when(i == 0)
        def _():
            dk_ref[...] = jnp.zeros_like(dk_ref)
            dv_ref[...] = jnp.zeros_like(dv_ref)

        ka = jnp.concatenate([kp_ref[...], k_ref[...], kn_ref[...]], axis=0)
        va = jnp.concatenate([vp_ref[...], v_ref[...], vn_ref[...]], axis=0)
        first = lax.broadcasted_iota(jnp.int32, (R, LANES), 1) < HEAD_DIM
        subs = range(tq // R)
        keys = lambda a, sb: a[sb * R:(sb + 3) * R]
        rows = lambda ref, sb: ref[sb * R:(sb + 1) * R, :]
        per_head = lambda t: jnp.concatenate([t[:, 0:1], t[:, HEAD_DIM:HEAD_DIM + 1]], axis=0)
        qs = [_na_stack(rows(q_ref, sb), first) for sb in subs]
        dos = [_na_stack(rows(do_ref, sb), first) for sb in subs]
        s = [_dot(qs[sb], keys(ka, sb), _NT) for sb in subs]
        dp = [_dot(dos[sb], keys(va, sb), _NT) for sb in subs]
        pn = [jnp.exp(s[sb] * ATT_SCALE + _dil_mask(m_ref, i, sb, nq, len(subs)) - per_head(rows(l_ref, sb))) for sb in subs]
        dsb = [(pn[sb] * (dp[sb] - per_head(rows(dt_ref, sb)))).astype(BF16) for sb in subs]
        pnb = [x.astype(BF16) for x in pn]
        dq = [_dot(dsb[sb], keys(ka, sb)) for sb in subs]
        dk = [_dot(dsb[sb], qs[sb], _TN) for sb in subs]
        dv = [_dot(pnb[sb], dos[sb], _TN) for sb in subs]
        zeros = lambda blocks: [jnp.zeros((blocks * R, LANES), F32)] if blocks else []
        pad = lambda t, sb: jnp.concatenate(zeros(sb) + [t] + zeros(len(subs) - 1 - sb), axis=0)
        dka = sum(pad(dk[sb], sb) for sb in subs) * ATT_SCALE
        dva = sum(pad(dv[sb], sb) for sb in subs)
        for sb in subs:
            dq_ref[sb * R:(sb + 1) * R, :] = jnp.where(first, dq[sb][:R], dq[sb][R:]) * ATT_SCALE
        r0 = pl.multiple_of(i * tq, R)
        dk_ref[pl.ds(r0, tq), :] += dka[R:R + tq]
        dv_ref[pl.ds(r0, tq), :] += dva[R:R + tq]

        @pl.when(i > 0)
        def _():
            dk_ref[pl.ds(r0 - R, R), :] += dka[:R]
            dv_ref[pl.ds(r0 - R, R), :] += dva[:R]

        @pl.when(i < nq - 1)
        def _():
            dk_ref[pl.ds(r0 + tq, R), :] += dka[R + tq:]
            dv_ref[pl.ds(r0 + tq, R), :] += dva[R + tq:]

    main, prev, nxt = _dil_specs(n, tq)
    whole = pl.BlockSpec((n, LANES), lambda j, i: (0, j))
    shp = jax.ShapeDtypeStruct((n, dil * GROUP_W), F32)
    (dq, dk, dv), got = _call(
        kern, (q, k, k, k, v, v, v, do, lse, dterm, _dil_masks()), ride,
        lambda: _grid_edges(2 * dil, nq), name=f"dil_bwd_{dil}", grid=(2 * dil, nq),
        in_specs=[main, prev, main, nxt, prev, main, nxt, main, main, main, _dil_mask_spec()],
        out_specs=[main, whole, whole],
        out_shape=[shp] * 3, scratch_shapes=[])
    return (dq, dk, dv), got


def _dil_weights(lses):
    m = jnp.maximum(jnp.maximum(lses[0], lses[1]), lses[2])
    e = [jnp.exp(l - m) for l in lses]
    tot = e[0] + e[1] + e[2]
    return [x / tot for x in e]


def _dilated_fwd(qkv, rides):
    dils = [dil for _, dil in DIL_PAIRS]
    res, got = [], {}
    for dil in dils:
        r, got[f"dil{dil}"] = _dil_fwd(*qkv[dil], dil, rides.get(f"dil{dil}"))
        res.append(r)

    def body(i, nt, o0, o1, o2, l0, l1, l2):
        w = _dil_weights((l0, l1, l2))
        return w[0] * o0 + w[1] * o1 + w[2] * o2

    ins = [Row(r[0], dil=d) for r, d in zip(res, dils)] + [Row(r[1], dil=d) for r, d in zip(res, dils)]
    return _rows("dil_combine", body, 512, ins, [("row", GROUP_W, BF16)])[0], res, got


def _dilated_bwd(dy, dy_cb, qkv, saved, cosf, sinf, rides):
    dils = [dil for _, dil in DIL_PAIRS]
    def split(i, nt, dy, o0, o1, o2, l0, l1, l2, bd):
        w = _dil_weights((l0, l1, l2))
        y = w[0] * o0 + w[1] * o1 + w[2] * o2
        dyy = _head_sum(dy * y, bd)
        return tuple(wg * dy for wg in w) + tuple(wg * dyy for wg in w)

    ins = ([Row(dy, GROUP_W, dy_cb)] + [Row(r[0], dil=d) for r, d in zip(saved, dils)]
           + [Row(r[1], dil=d) for r, d in zip(saved, dils)])
    outs = _rows("dil_split_bwd", split, 512, ins + [Full(_block_ones(GROUP_W, HEAD_DIM))],
                 [("row", GROUP_W, BF16, d) for d in dils] + [("row", GROUP_W, F32, d) for d in dils])
    g, got = [], {}
    for b, dil in enumerate(dils):
        r, got[f"dil{dil}"] = _dil_bwd(*qkv[dil], outs[b], saved[b][1], outs[3 + b], dil, rides.get(f"dil{dil}"))
        g.append(r)

    def finish(i, nt, q0, q1, q2, k0, k1, k2, v0, v1, v2, c, s):
        dq, dk = q0 + q1 + q2, k0 + k1 + k2
        return dq * c + _swap_halves(dq * s), dk * c + _swap_halves(dk * s), v0 + v1 + v2

    ins = [Row(g[b][t], dil=dils[b]) for t in range(3) for b in range(3)] + [Row(cosf), Row(sinf)]
    return _rows("dil_finish_bwd", finish, 512, ins, [("row", GROUP_W, BF16)] * 3), got


def _layer_fwd(x, W, l, cosf, sinf, rides):
    rides = {c: Exchange(items) for c, items in rides.items()}
    h1 = _rms_fwd(x, W["mix_norm_pre"][l][None], "mix_norm")
    p = _mm(h1, W["w_in"][l], "nn", F32, "proj_in")
    ya, sa, got_gla = _gla_fwd(p, W, l, rides.get("gla"))
    (qb, kb, vb), qkv_d = _attn_prep(p, cosf, sinf)
    bias = _na_bias(W["na_rpb"][l])
    yb, got_na = _na_fwd(qb, kb, vb, bias, rides.get("na"))
    yc, sc = _lru_fwd(p, W, l)
    yd, sd, got = _dilated_fwd(qkv_d, rides)
    got.update(gla=got_gla, na=got_na)
    ycat = jnp.concatenate([ya, yb, yc, yd], axis=1)
    ymix = _mm(ycat, W["w_out"][l], "nn", F32, "proj_out", tm=1024)
    xm = _rms_resid_fwd(x, ymix, W["mix_norm_post"][l][None], "mix_resid")
    h2 = _rms_fwd(xm, W["ffn_norm_pre"][l][None], "ffn_norm")
    gu, act = _ffn_in_swiglu(h2, W["ffn_w_in"][l])
    f = _mm(act, W["ffn_w_out"][l], "nn", F32, "ffn_out")
    xo = _rms_resid_fwd(xm, f, W["ffn_norm_post"][l][None], "ffn_resid")
    saved = dict(x=x, h1=h1, p=p, sa=sa, att=(qb, kb, vb, qkv_d), bias=bias, sc=sc, sd=sd, ycat=ycat, ymix=ymix,
                 xm=xm, h2=h2, gu=gu, act=act, f=f)
    return xo, saved, got


def _layer_bwd(dxo, W, l, S, cosf, sinf, rides, early=None):
    g = {}
    df, g["ffn_norm_post"] = _rms_bwd(dxo, S["f"], W["ffn_norm_post"][l][None], "ffn_resid_bwd", out_dtype=BF16)
    g["ffn_w_out"] = _mm(S["act"], df, "tn", BF16, "ffn_out_dw", tm=256, tk=4096)
    dgu = _ffn_out_dx_swiglu(df, W["ffn_w_out"][l], S["gu"])
    dh2 = _mm(dgu, W["ffn_w_in"][l], "nt", F32, "ffn_in_dx")
    g["ffn_w_in"] = _mm(S["h2"], dgu, "tn", BF16, "ffn_in_dw", tm=1024, tn=512, tk=4096)
    dxm, g["ffn_norm_pre"] = _rms_bwd(dh2, S["xm"], W["ffn_norm_pre"][l][None], "ffn_norm_bwd", resid=dxo)
    dymix, g["mix_norm_post"] = _rms_bwd(dxm, S["ymix"], W["mix_norm_post"][l][None], "mix_resid_bwd", out_dtype=BF16)
    dycat = _mm(dymix, W["w_out"][l], "nt", F32, "proj_out_dx", tm=1024)
    g["w_out"] = _mm(S["ycat"], dymix, "tn", BF16, "proj_out_dw", tm=1024, tn=512, tk=4096)
    if early is not None:
        for c, items in early(g).items():
            rides = {**rides, c: rides.get(c, []) + items}
    rides = {c: Exchange(items) for c, items in rides.items()}
    p = S["p"]
    qb, kb, vb, qkv_d = S["att"]
    (dqa, dka, dva, dga, dz), ga, got_gla = _gla_bwd(dycat, 0, p, W, l, S["sa"], rides.get("gla"))
    (dqb, dkb, dvb, dbias), got_na = _na_bwd(dycat, 2, qb, kb, vb, S["bias"], rides.get("na"))
    g["na_rpb"] = _na_bias_bwd(dbias)
    dxc, dgc, gc = _lru_bwd(dycat, 2, p, W, l, S["sc"])
    (dqd, dkd, dvd), got = _dilated_bwd(dycat, 3, qkv_d, S["sd"], cosf, sinf, rides)
    got.update(gla=got_gla, na=got_na)
    g.update(ga)
    g.update(gc)
    dp = jnp.concatenate([dqa, dka, dva, dga, dqb, dkb.astype(BF16), dvb.astype(BF16), dxc, dgc, dqd, dkd, dvd, dz], axis=1)
    dh1 = _mm(dp, W["w_in"][l], "nt", F32, "proj_in_dx")
    g["w_in"] = _mm(S["h1"], dp, "tn", BF16, "proj_in_dw", tm=1024, tn=640, tk=4096)
    dx, g["mix_norm_pre"] = _rms_bwd(dh1, S["x"], W["mix_norm_pre"][l][None], "mix_norm_bwd", resid=dxm)
    for n in ("ffn_norm_post", "ffn_norm_pre", "mix_norm_post", "mix_norm_pre"):
        g[n] = g[n][0]
    return dx, g, got


MESH_AXES = ("x", "y", "c")


class Xfer:
    def __init__(self, arr, kind):
        self.arr, self.kind = arr, kind
        shp = arr.shape
        if kind == "all":
            self.out = (N_DEV,) + shp
        elif kind == "slot":
            self.out = shp
        elif kind == "rows":
            self.r = shp[1] // N_DEV
            self.out = (N_DEV, shp[0], self.r, shp[2])
        else:
            self.r = shp[1]
            self.out = (shp[0], N_DEV * shp[1], shp[2])

    def src(self, ref, peer):
        if self.kind == "slot":
            return ref.at[peer]
        if self.kind == "rows":
            return ref.at[:, pl.ds(peer * self.r, self.r), :]
        return ref

    def dst(self, ref, me):
        if self.kind == "place":
            return ref.at[:, pl.ds(me * self.r, self.r), :]
        return ref.at[me]


class Exchange:
    def __init__(self, items):
        n = len(items)
        self.items = items
        self.arrays = [it.arr for it in items]
        self.specs = [pl.BlockSpec(memory_space=pl.ANY)] * n
        self.out_shape = [jax.ShapeDtypeStruct(it.out, it.arr.dtype) for it in items]
        self.scratch = [pltpu.SemaphoreType.DMA((n * (N_DEV - 1),)), pltpu.SemaphoreType.DMA((n * (N_DEV - 1),)),
                        pltpu.SemaphoreType.DMA((n,))]

    def copies(self, ins, outs, sems):
        send_sems, recv_sems, local_sems = sems
        x, y, c = (lax.axis_index(a) for a in MESH_AXES)
        me = 4 * x + 2 * y + c
        out = []
        for t, it in enumerate(self.items):
            out.append(pltpu.make_async_copy(it.src(ins[t], me), it.dst(outs[t], me), local_sems.at[t]))
            for k in range(1, N_DEV):
                px, py, pc = x ^ ((k >> 2) & 1), y ^ ((k >> 1) & 1), c ^ (k & 1)
                s = t * (N_DEV - 1) + k - 1
                out.append(pltpu.make_async_remote_copy(
                    src_ref=it.src(ins[t], 4 * px + 2 * py + pc), dst_ref=it.dst(outs[t], me),
                    send_sem=send_sems.at[s], recv_sem=recv_sems.at[s], device_id=(px, py, pc),
                    device_id_type=pl.DeviceIdType.MESH))
        return out

    def start(self, ins, outs, sems):
        for cp in self.copies(ins, outs, sems):
            cp.start()

    def wait(self, ins, outs, sems):
        for cp in self.copies(ins, outs, sems):
            cp.wait()


def _exchange(items, name):
    ex = Exchange(items)
    n = len(items)

    def body(*refs):
        ex.start(refs[:n], refs[n:2 * n], refs[2 * n:])
        ex.wait(refs[:n], refs[n:2 * n], refs[2 * n:])

    return pl.pallas_call(body, name=name, out_shape=ex.out_shape, in_specs=ex.specs, out_specs=ex.specs,
                          scratch_shapes=ex.scratch)(*ex.arrays)


def _call(kern, arrays, ride, edges, *, name, grid, in_specs, out_specs, out_shape, scratch_shapes):
    params = _ARB(len(grid))
    if ride is None:
        return pl.pallas_call(kern, name=name, grid=grid, in_specs=in_specs, out_specs=out_specs, out_shape=out_shape,
                              scratch_shapes=scratch_shapes, compiler_params=params)(*arrays), None
    ni, no, ns, nx = len(in_specs), len(out_specs), len(scratch_shapes), len(ride.items)

    def wrapped(*refs):
        ins, xin = refs[:ni], refs[ni:ni + nx]
        outs, xout = refs[ni + nx:ni + nx + no], refs[ni + nx + no:ni + 2 * nx + no]
        scr, sems = refs[ni + 2 * nx + no:ni + 2 * nx + no + ns], refs[ni + 2 * nx + no + ns:]
        first, last = edges()

        @pl.when(first)
        def _():
            ride.start(xin, xout, sems)

        kern(*ins, *outs, *scr)

        @pl.when(last)
        def _():
            ride.wait(xin, xout, sems)

    res = pl.pallas_call(
        wrapped, name=name, grid=grid, in_specs=list(in_specs) + ride.specs, out_specs=list(out_specs) + ride.specs,
        out_shape=list(out_shape) + ride.out_shape, scratch_shapes=list(scratch_shapes) + ride.scratch,
        compiler_params=params)(*arrays, *ride.arrays)
    return res[:no], res[no:]


def _column_segments(width, permuted):
    z0, z1, zn = 4 * GROUP_W, 4 * GROUP_W + 2 * GLA_RANK, 12 * GROUP_W
    segs = []
    for d in range(N_DEV):
        lo, hi = d * width, (d + 1) * width
        if not permuted:
            segs.append([(0, width, lo)])
            continue
        runs = []
        for a, b, shift in ((0, z0, 0), (z0, z1, zn - z0), (z1, 10 ** 9, -(z1 - z0))):
            s, e = max(lo, a), min(hi, b)
            if s < e:
                runs.append((s - lo, e - lo, s + shift))
        segs.append(runs)
    return segs


def _cols_from_pieces(pieces, segs, cols, name):
    _, R, w = pieces.shape
    tm = _tile(R, 256, 16)
    used = max(f + (b - a) for runs in segs for a, b, f in runs)

    def kern(p_ref, o_ref):
        for d, runs in enumerate(segs):
            for a, b, f in runs:
                o_ref[:, f:f + (b - a)] = p_ref[d, :, a:b]
        if used < cols:
            o_ref[:, used:cols] = jnp.zeros((tm, cols - used), o_ref.dtype)

    return pl.pallas_call(
        kern, name=name, grid=(R // tm,), in_specs=[pl.BlockSpec((N_DEV, tm, w), lambda i: (0, i, 0))],
        out_specs=pl.BlockSpec((tm, cols), lambda i: (i, 0)), out_shape=jax.ShapeDtypeStruct((R, cols), pieces.dtype),
        compiler_params=_ARB(1),
    )(pieces)


def _pieces_from_cols(full, segs, w, name):
    R, cols = full.shape
    tm = _tile(R, 256, 16)

    def kern(f_ref, o_ref):
        for d, runs in enumerate(segs):
            for a, b, f in runs:
                o_ref[d, :, a:b] = f_ref[:, f:f + (b - a)]

    return pl.pallas_call(
        kern, name=name, grid=(R // tm,), in_specs=[pl.BlockSpec((tm, cols), lambda i: (i, 0))],
        out_specs=pl.BlockSpec((N_DEV, tm, w), lambda i: (0, i, 0)),
        out_shape=jax.ShapeDtypeStruct((N_DEV, R, w), full.dtype), compiler_params=_ARB(1),
    )(full)


def _sum_slots(recv, name):
    n, R, C = recv.shape
    tm = _tile(R, 256, 16)

    def kern(*refs):
        acc = refs[0][...].astype(F32)
        for r in refs[1:n]:
            acc = acc + r[...].astype(F32)
        refs[n][...] = acc

    return pl.pallas_call(
        kern, name=name, grid=(R // tm,),
        in_specs=[pl.BlockSpec((None, tm, C), lambda i, _s=s: (_s, i, 0)) for s in range(n)],
        out_specs=pl.BlockSpec((tm, C), lambda i: (i, 0)), out_shape=jax.ShapeDtypeStruct((R, C), F32),
        compiler_params=_ARB(1),
    )(*([recv] * n))


BIG = (("w_in", 2), ("w_out", 1), ("ffn_w_in", 2), ("ffn_w_out", 1))
SMALL_SHARDED = ("gla_w_gate", "gla_b_gate", "lru_conv_w", "lru_b_a", "lru_b_x", "lru_lambda")
REPLICATED = ("mix_norm_pre", "mix_norm_post", "gla_norm", "na_rpb", "lru_conv_b", "lru_w_a", "lru_w_x",
              "ffn_norm_pre", "ffn_norm_post")
WEIGHTS = ("mix_norm_pre", "mix_norm_post", "w_in", "gla_w_gate", "gla_b_gate", "gla_norm", "na_rpb", "lru_conv_w",
           "lru_conv_b", "lru_w_a", "lru_b_a", "lru_w_x", "lru_b_x", "lru_lambda", "w_out", "ffn_norm_pre",
           "ffn_norm_post", "ffn_w_in", "ffn_w_out")
FLAT_C = 1024


def _to_rows(vec, row_unit):
    n = vec.shape[-1]
    rows = -(-n // (FLAT_C * row_unit)) * row_unit
    pad = [(0, 0)] * (vec.ndim - 1) + [(0, rows * FLAT_C - n)]
    return jnp.pad(vec, pad).reshape(vec.shape[:-1] + (rows, FLAT_C))


def _unshard(parts, axis):
    t = jnp.moveaxis(parts, 0, axis)
    shp = list(t.shape)
    return t.reshape(shp[:axis] + [shp[axis] * shp[axis + 1]] + shp[axis + 2:])


def _shards(full, axis):
    shp = list(full.shape)
    t = full.reshape(shp[:axis] + [N_DEV, shp[axis] // N_DEV] + shp[axis + 1:])
    return jnp.moveaxis(t, axis, 0)


def _weight_rides(W, l):
    bf = lambda n: W[n][l].astype(BF16)
    ffn = bf("ffn_w_in")
    half = ffn.shape[0] // 2
    return {"gla": [Xfer(bf("w_in"), "all"), Xfer(bf("w_out")[None], "place")],
            "na": [Xfer(bf("ffn_w_out")[None], "place")],
            "dil1": [Xfer(ffn[:half], "all")], "dil4": [Xfer(ffn[half:], "all")]}


def _unpack_weights(full, W, got):
    w_in_w, ffn_w = W["w_in"].shape[-1], W["ffn_w_in"].shape[-1]
    full["w_in"].append(_cols_from_pieces(got["gla"][0], _column_segments(w_in_w, True), P_COLS, "unpack_w_in"))
    full["w_out"].append(got["gla"][1][0])
    full["ffn_w_out"].append(got["na"][0][0])
    full["ffn_w_in"].append(jnp.concatenate(
        [_cols_from_pieces(got[c][0], _column_segments(ffn_w, False), N_DEV * ffn_w, "unpack_ffn_w_in")
         for c in ("dil1", "dil4")], axis=0))


def _w_in_grad_pieces(g, W):
    w_in_w = W["w_in"].shape[-1]
    return _pieces_from_cols(g["w_in"], _column_segments(w_in_w, True), w_in_w, "pack_w_in")


def _ffn_grad_halves(g, W):
    ffn_w = W["ffn_w_in"].shape[-1]
    p_ffn = _pieces_from_cols(g["ffn_w_in"], _column_segments(ffn_w, False), ffn_w, "pack_ffn_w_in")
    half = p_ffn.shape[1] // 2
    return p_ffn[:, :half], p_ffn[:, half:]


def _grad_rides(g, W):
    top, bottom = _ffn_grad_halves(g, W)
    return {"gla": [Xfer(_w_in_grad_pieces(g, W), "slot"), Xfer(g["w_out"][None], "rows")],
            "na": [Xfer(g["ffn_w_out"][None], "rows")], "dil16": [Xfer(top, "slot")], "dil1": [Xfer(bottom, "slot")]}


def _early_grad_rides(g, W):
    top, bottom = _ffn_grad_halves(g, W)
    return {"gla": [Xfer(g["w_out"][None], "rows"), Xfer(g["ffn_w_out"][None], "rows")], "na": [Xfer(top, "slot")],
            "dil4": [Xfer(bottom, "slot")]}


def _sum_big(w_in, w_out, ffn_w_out, ffn_top, ffn_bottom):
    s = lambda r, n: _sum_slots(r.reshape(N_DEV, -1, r.shape[-1]), "sum_" + n)
    return {"w_in": s(w_in, "w_in"), "w_out": s(w_out, "w_out"), "ffn_w_out": s(ffn_w_out, "ffn_w_out"),
            "ffn_w_in": jnp.concatenate([s(ffn_top, "ffn_w_in"), s(ffn_bottom, "ffn_w_in")], axis=0)}


def _exchange_named(rides, extra, name):
    names = list(rides)
    res = _exchange([it for n in names for it in rides[n]] + extra, name)
    got, at = {}, 0
    for n in names:
        got[n] = res[at:at + len(rides[n])]
        at += len(rides[n])
    return got, res[at:]


def _train(x, target, W):
    L = x.shape[0]
    depth = W["w_in"].shape[0]
    cosf, sinf = _rope_tables(L)
    small = jnp.concatenate([W[n].reshape(-1) for n in SMALL_SHARDED])
    small16 = _to_rows(lax.bitcast_convert_type(small, jnp.uint16).reshape(-1), 16)
    got, (sm,) = _exchange_named(_weight_rides(W, 0), [Xfer(small16, "all")], "gather_first")
    full = dict(W, w_in=[], w_out=[], ffn_w_in=[], ffn_w_out=[])
    _unpack_weights(full, W, got)
    sm = lax.bitcast_convert_type(sm.reshape(N_DEV, -1)[:, :2 * small.size].reshape(N_DEV, small.size, 2), F32)
    off = 0
    for n in SMALL_SHARDED:
        full[n] = _unshard(sm[:, off:off + W[n].size].reshape((N_DEV,) + W[n].shape), W[n].ndim - 1)
        off += W[n].size

    saved = []
    for l in range(depth):
        x, S, got = _layer_fwd(x, full, l, cosf, sinf, _weight_rides(W, l + 1) if l + 1 < depth else {})
        saved.append(S)
        if l + 1 < depth:
            _unpack_weights(full, W, got)
    loss, dx = _loss_fwd_bwd(x, target)

    grads, big, rides = [None] * depth, [None] * depth, {}
    for l in reversed(range(depth)):
        early = (lambda g: _early_grad_rides(g, W)) if l == 0 else None
        dx, grads[l], got = _layer_bwd(dx, full, l, saved[l], cosf, sinf, rides, early)
        if l + 1 < depth:
            big[l + 1] = _sum_big(got["gla"][0], got["gla"][1], got["na"][0], got["dil16"][0], got["dil1"][0])
        late = {c: len(items) for c, items in rides.items()}
        rides = _grad_rides(grads[l], W) if l > 0 else {}
    G = {n: jnp.stack([g[n] for g in grads]) for n in SMALL_SHARDED + REPLICATED}
    small_g = jnp.concatenate([_shards(G[n], G[n].ndim - 1).reshape(N_DEV, -1) for n in SMALL_SHARDED], axis=1)
    repl_g = jnp.concatenate([G[n].reshape(-1) for n in REPLICATED])
    rest = _exchange([Xfer(_w_in_grad_pieces(grads[0], W), "slot"), Xfer(_to_rows(small_g, 8), "slot"),
                      Xfer(_to_rows(repl_g, 8), "all")], "exchange_last")
    own = lambda c, k: got[c][late.get(c, 0) + k]
    big[0] = _sum_big(rest[0], own("gla", 0), own("gla", 1), own("na", 0), own("dil4", 0))
    rest = rest[1:]
    out = {n: jnp.stack([b[n] for b in big]).reshape(W[n].shape) for n, _ in BIG}
    for names, r, tag in ((SMALL_SHARDED, rest[0], "sum_small"), (REPLICATED, rest[1], "sum_replicated")):
        flat, off = _sum_slots(r, tag).reshape(-1), 0
        for n in names:
            out[n] = flat[off:off + W[n].size].reshape(W[n].shape)
            off += W[n].size
    return loss, dx, out


def _update(W, G, M, V):
    delta, new_m, new_v = {}, {}, {}
    for n, _ in BIG:
        two_d = lambda a: a.reshape(-1, a.shape[-1])
        d, m, v = _adamw(two_d(W[n]), two_d(G[n]), two_d(M[n]), two_d(V[n]), "adamw_" + n)
        delta[n], new_m[n], new_v[n] = (t.reshape(W[n].shape) for t in (d, m, v))
    rest = SMALL_SHARDED + REPLICATED
    pack = lambda D: _to_rows(jnp.concatenate([D[n].reshape(-1) for n in rest]), 16)
    d, m, v = _adamw(pack(W), pack(G), pack(M), pack(V), "adamw_small")
    off = 0
    for n in rest:
        sl = lambda t: t.reshape(-1)[off:off + W[n].size].reshape(W[n].shape)
        delta[n], new_m[n], new_v[n] = sl(d), sl(m), sl(v)
        off += W[n].size
    return delta, new_m, new_v


def kernel(x, mix_norm_pre, mix_norm_post, w_in, gla_w_gate, gla_b_gate, gla_norm, na_rpb, lru_conv_w, lru_conv_b, lru_w_a, lru_b_a, lru_w_x, lru_b_x, lru_lambda, w_out, ffn_norm_pre, ffn_norm_post, ffn_w_in, ffn_w_out, loss_target, m_mix_norm_pre, m_mix_norm_post, m_w_in, m_gla_w_gate, m_gla_b_gate, m_gla_norm, m_na_rpb, m_lru_conv_w, m_lru_conv_b, m_lru_w_a, m_lru_b_a, m_lru_w_x, m_lru_b_x, m_lru_lambda, m_w_out, m_ffn_norm_pre, m_ffn_norm_post, m_ffn_w_in, m_ffn_w_out, v_mix_norm_pre, v_mix_norm_post, v_w_in, v_gla_w_gate, v_gla_b_gate, v_gla_norm, v_na_rpb, v_lru_conv_w, v_lru_conv_b, v_lru_w_a, v_lru_b_a, v_lru_w_x, v_lru_b_x, v_lru_lambda, v_w_out, v_ffn_norm_pre, v_ffn_norm_post, v_ffn_w_in, v_ffn_w_out):
    W = dict(zip(WEIGHTS, (mix_norm_pre, mix_norm_post, w_in, gla_w_gate, gla_b_gate, gla_norm, na_rpb, lru_conv_w, lru_conv_b, lru_w_a, lru_b_a, lru_w_x, lru_b_x, lru_lambda, w_out, ffn_norm_pre, ffn_norm_post, ffn_w_in, ffn_w_out)))
    M = dict(zip(WEIGHTS, (m_mix_norm_pre, m_mix_norm_post, m_w_in, m_gla_w_gate, m_gla_b_gate, m_gla_norm, m_na_rpb, m_lru_conv_w, m_lru_conv_b, m_lru_w_a, m_lru_b_a, m_lru_w_x, m_lru_b_x, m_lru_lambda, m_w_out, m_ffn_norm_pre, m_ffn_norm_post, m_ffn_w_in, m_ffn_w_out)))
    V = dict(zip(WEIGHTS, (v_mix_norm_pre, v_mix_norm_post, v_w_in, v_gla_w_gate, v_gla_b_gate, v_gla_norm, v_na_rpb, v_lru_conv_w, v_lru_conv_b, v_lru_w_a, v_lru_b_a, v_lru_w_x, v_lru_b_x, v_lru_lambda, v_w_out, v_ffn_norm_pre, v_ffn_norm_post, v_ffn_w_in, v_ffn_w_out)))
    loss, dx, G = _train(x[0], loss_target[0], W)
    loss = lax.psum(loss, MESH_AXES)
    delta, new_m, new_v = _update(W, G, M, V)
    return (loss, dx[None], *[G[n] for n in WEIGHTS], *[delta[n] for n in WEIGHTS], *[new_m[n] for n in WEIGHTS],
            *[new_v[n] for n in WEIGHTS])
```
